```python
import jax, jax.numpy as jnp
from jax import lax
import numpy as np

D_MODEL = 2048
BATCH = 8
SEQ = 4096
DEPTH = 1

HEAD_DIM = 64
ATTN_WIDTH = D_MODEL // 2
CONV_WIDTH_CH = D_MODEL - ATTN_WIDTH
N_Q_HEADS = ATTN_WIDTH // HEAD_DIM
N_KV_HEADS = 4
GROUP = N_Q_HEADS // N_KV_HEADS
KV_WIDTH = N_KV_HEADS * HEAD_DIM
N_CONV_GROUPS = CONV_WIDTH_CH // HEAD_DIM
CONV_WIDTH = 3
WINDOW = 128
ROT_DIM = HEAD_DIM // 4
ROPE_THETA = 500000.0
D_FF = ((8 * D_MODEL // 3 + 255) // 256) * 256
IN_WIDTH = ATTN_WIDTH + 2 * KV_WIDTH + 3 * CONV_WIDTH_CH
SPLITS = [ATTN_WIDTH,
          ATTN_WIDTH + KV_WIDTH,
          ATTN_WIDTH + 2 * KV_WIDTH,
          ATTN_WIDTH + 2 * KV_WIDTH + CONV_WIDTH_CH,
          ATTN_WIDTH + 2 * KV_WIDTH + 2 * CONV_WIDTH_CH]
ATTN_SCALE = HEAD_DIM ** -0.5
DEEPNORM_ALPHA = (2 * DEPTH) ** 0.25
DEEPNORM_BETA = (8 * DEPTH) ** -0.25
LN_EPS = 1e-5
RMS_EPS = 1e-6

kernel_name = "hymba_conv_swa_sink_deepnorm_layer"


def _layer_norm(x, g, b):
    xf = x.astype(jnp.float32)
    mu = jnp.mean(xf, axis=-1, keepdims=True)
    var = jnp.mean(jnp.square(xf - mu), axis=-1, keepdims=True)
    y = (xf - mu) * lax.rsqrt(var + LN_EPS)
    return (y * g.astype(jnp.float32) + b.astype(jnp.float32)).astype(x.dtype)


def _rms_norm(x, g):
    xf = x.astype(jnp.float32)
    y = xf * lax.rsqrt(jnp.mean(jnp.square(xf), axis=-1, keepdims=True) + RMS_EPS)
    return (y * g.astype(jnp.float32)).astype(x.dtype)


def _partial_rope(t, positions):
    inv_freq = ROPE_THETA ** (-jnp.arange(0, ROT_DIM, 2, dtype=jnp.float32) / ROT_DIM)
    ang = positions.astype(jnp.float32)[..., None] * inv_freq
    cos = jnp.cos(ang)[:, :, None, :]
    sin = jnp.sin(ang)[:, :, None, :]
    tr = t[..., :ROT_DIM].astype(jnp.float32)
    t1, t2 = tr[..., :ROT_DIM // 2], tr[..., ROT_DIM // 2:]
    rot = jnp.concatenate([t1 * cos - t2 * sin, t2 * cos + t1 * sin], axis=-1).astype(t.dtype)
    return jnp.concatenate([rot, t[..., ROT_DIM:]], axis=-1)


def _sliding_window_attention(q, k, v, sinks):
    b, s = q.shape[0], q.shape[1]
    nb = s // WINDOW
    qb = q.reshape(b, nb, WINDOW, N_KV_HEADS, GROUP, HEAD_DIM)
    kb = k.reshape(b, nb, WINDOW, N_KV_HEADS, HEAD_DIM)
    vb = v.reshape(b, nb, WINDOW, N_KV_HEADS, HEAD_DIM)

    def with_prev(t):
        prev = jnp.concatenate([jnp.zeros_like(t[:, :1]), t[:, :-1]], axis=1)
        return jnp.concatenate([prev, t], axis=2)

    kk, vv = with_prev(kb), with_prev(vb)
    scores = jnp.einsum('bnqhgd,bnkhd->bnhgqk', qb, kk,
                        preferred_element_type=jnp.float32) * ATTN_SCALE
    qi = jnp.arange(WINDOW)[:, None]
    kj = jnp.arange(2 * WINDOW)[None, :]
    rel = qi + WINDOW - kj
    band = (rel >= 0) & (rel < WINDOW)
    first_block = (jnp.arange(nb) == 0)[:, None, None]
    valid = band[None] & ~(first_block & (kj < WINDOW)[None])
    scores = jnp.where(valid[None, :, None, None], scores, -jnp.inf)
    sink_col = jnp.broadcast_to(
        sinks.astype(jnp.float32).reshape(N_KV_HEADS, GROUP)[None, None, :, :, None, None],
        scores.shape[:-1] + (1,))
    probs = jax.nn.softmax(jnp.concatenate([scores, sink_col], axis=-1), axis=-1)[..., :-1]
    out = jnp.einsum('bnhgqk,bnkhd->bnqhgd', probs.astype(v.dtype), vv)
    return out.reshape(b, s, N_Q_HEADS * HEAD_DIM)


def _short_gated_conv(c_gate, b_gate, u, conv_w):
    z = c_gate * u
    s = z.shape[1]
    zp = jnp.pad(z, ((0, 0), (CONV_WIDTH - 1, 0), (0, 0)))
    y = conv_w[0] * zp[:, 0:s] + conv_w[1] * zp[:, 1:s + 1] + conv_w[2] * zp[:, 2:s + 2]
    return b_gate * y


def _fwd_setup_inputs(seed: int = 0) -> dict:
    key = jax.random.key(seed)
    ks = jax.random.split(key, 16)
    f32 = jnp.float32
    x = jax.random.normal(ks[0], (BATCH, SEQ, D_MODEL), f32)
    offset = jax.random.randint(ks[1], (BATCH, 1), 0, 1024, dtype=jnp.int32)
    positions = (offset + jnp.arange(SEQ, dtype=jnp.int32)[None, :]).astype(jnp.int32)
    w_in = jax.random.normal(ks[2], (DEPTH, D_MODEL, IN_WIDTH), f32) * D_MODEL ** -0.5
    conv_w = jax.random.normal(ks[3], (DEPTH, CONV_WIDTH, CONV_WIDTH_CH), f32) * CONV_WIDTH ** -0.5
    sinks = jax.random.normal(ks[4], (DEPTH, N_Q_HEADS), f32) * 0.5
    g_attn = 1.0 + 0.02 * jax.random.normal(ks[5], (DEPTH, ATTN_WIDTH), f32)
    g_conv = 1.0 + 0.02 * jax.random.normal(ks[6], (DEPTH, CONV_WIDTH_CH), f32)
    w_out = jax.random.normal(ks[7], (DEPTH, D_MODEL, D_MODEL), f32) * (DEEPNORM_BETA * D_MODEL ** -0.5)
    ln1_g = 1.0 + 0.02 * jax.random.normal(ks[8], (DEPTH, D_MODEL), f32)
    ln1_b = 0.02 * jax.random.normal(ks[9], (DEPTH, D_MODEL), f32)
    w_gate = jax.random.normal(ks[10], (DEPTH, D_MODEL, D_FF), f32) * D_MODEL ** -0.5
    w_up = jax.random.normal(ks[11], (DEPTH, D_MODEL, D_FF), f32) * D_MODEL ** -0.5
    w_down = jax.random.normal(ks[12], (DEPTH, D_FF, D_MODEL), f32) * (DEEPNORM_BETA * D_FF ** -0.5)
    ln2_g = 1.0 + 0.02 * jax.random.normal(ks[13], (DEPTH, D_MODEL), f32)
    ln2_b = 0.02 * jax.random.normal(ks[14], (DEPTH, D_MODEL), f32)
    return {"x": x, "positions": positions, "w_in": w_in, "conv_w": conv_w,
            "sinks": sinks, "g_attn": g_attn, "g_conv": g_conv, "w_out": w_out,
            "ln1_g": ln1_g, "ln1_b": ln1_b, "w_gate": w_gate, "w_up": w_up,
            "w_down": w_down, "ln2_g": ln2_g, "ln2_b": ln2_b}


def _fwd_reference(x, positions, w_in, conv_w, sinks, g_attn, g_conv, w_out,
              ln1_g, ln1_b, w_gate, w_up, w_down, ln2_g, ln2_b):
    b, s, _ = x.shape
    h = x
    for l in range(DEPTH):
        proj = h @ w_in[l]
        q, k, v, c_gate, b_gate, u = jnp.split(proj, SPLITS, axis=-1)
        q = _partial_rope(q.reshape(b, s, N_Q_HEADS, HEAD_DIM), positions)
        k = _partial_rope(k.reshape(b, s, N_KV_HEADS, HEAD_DIM), positions)
        v = v.reshape(b, s, N_KV_HEADS, HEAD_DIM)
        attn = _sliding_window_attention(q, k, v, sinks[l])
        conv = _short_gated_conv(c_gate, b_gate, u, conv_w[l])
        mixed = jnp.concatenate([_rms_norm(attn, g_attn[l]), _rms_norm(conv, g_conv[l])], axis=-1)
        mix_out = mixed @ w_out[l]
        h = _layer_norm(DEEPNORM_ALPHA * h + mix_out, ln1_g[l], ln1_b[l])
        ffn = (jax.nn.silu(h @ w_gate[l]) * (h @ w_up[l])) @ w_down[l]
        h = _layer_norm(DEEPNORM_ALPHA * h + ffn, ln2_g[l], ln2_b[l])
    return h


import jax as _jax
import jax.numpy as _jnp

TWIN_FORMAT = 'train_step'
FWD_PARAMS = ['x', 'positions', 'w_in', 'conv_w', 'sinks', 'g_attn', 'g_conv', 'w_out', 'ln1_g', 'ln1_b', 'w_gate', 'w_up', 'w_down', 'ln2_g', 'ln2_b']
TWIN_WEIGHTS = ['w_in', 'conv_w', 'sinks', 'g_attn', 'g_conv', 'w_out', 'ln1_g', 'ln1_b', 'w_gate', 'w_up', 'w_down', 'ln2_g', 'ln2_b']
TWIN_DIFF_INPUT = 'x'
TWIN_INPUTS = ['x', 'positions', 'w_in', 'conv_w', 'sinks', 'g_attn', 'g_conv', 'w_out', 'ln1_g', 'ln1_b', 'w_gate', 'w_up', 'w_down', 'ln2_g', 'ln2_b', 'loss_target', 'm_w_in', 'm_conv_w', 'm_sinks', 'm_g_attn', 'm_g_conv', 'm_w_out', 'm_ln1_g', 'm_ln1_b', 'm_w_gate', 'm_w_up', 'm_w_down', 'm_ln2_g', 'm_ln2_b', 'v_w_in', 'v_conv_w', 'v_sinks', 'v_g_attn', 'v_g_conv', 'v_w_out', 'v_ln1_g', 'v_ln1_b', 'v_w_gate', 'v_w_up', 'v_w_down', 'v_ln2_g', 'v_ln2_b']
TWIN_OUTPUTS = ['loss', 'grad_x', 'grad_w_in', 'grad_conv_w', 'grad_sinks', 'grad_g_attn', 'grad_g_conv', 'grad_w_out', 'grad_ln1_g', 'grad_ln1_b', 'grad_w_gate', 'grad_w_up', 'grad_w_down', 'grad_ln2_g', 'grad_ln2_b', 'delta_w_in', 'delta_conv_w', 'delta_sinks', 'delta_g_attn', 'delta_g_conv', 'delta_w_out', 'delta_ln1_g', 'delta_ln1_b', 'delta_w_gate', 'delta_w_up', 'delta_w_down', 'delta_ln2_g', 'delta_ln2_b', 'new_m_w_in', 'new_m_conv_w', 'new_m_sinks', 'new_m_g_attn', 'new_m_g_conv', 'new_m_w_out', 'new_m_ln1_g', 'new_m_ln1_b', 'new_m_w_gate', 'new_m_w_up', 'new_m_w_down', 'new_m_ln2_g', 'new_m_ln2_b', 'new_v_w_in', 'new_v_conv_w', 'new_v_sinks', 'new_v_g_attn', 'new_v_g_conv', 'new_v_w_out', 'new_v_ln1_g', 'new_v_ln1_b', 'new_v_w_gate', 'new_v_w_up', 'new_v_w_down', 'new_v_ln2_g', 'new_v_ln2_b']
TWIN_LEAF_KINDS = {'loss': 'loss', 'grad_x': 'grad_x', 'grad_w_in': 'grad_w', 'grad_conv_w': 'grad_w', 'grad_sinks': 'grad_w', 'grad_g_attn': 'grad_w', 'grad_g_conv': 'grad_w', 'grad_w_out': 'grad_w', 'grad_ln1_g': 'grad_w', 'grad_ln1_b': 'grad_w', 'grad_w_gate': 'grad_w', 'grad_w_up': 'grad_w', 'grad_w_down': 'grad_w', 'grad_ln2_g': 'grad_w', 'grad_ln2_b': 'grad_w', 'delta_w_in': 'delta_w', 'delta_conv_w': 'delta_w', 'delta_sinks': 'delta_w', 'delta_g_attn': 'delta_w', 'delta_g_conv': 'delta_w', 'delta_w_out': 'delta_w', 'delta_ln1_g': 'delta_w', 'delta_ln1_b': 'delta_w', 'delta_w_gate': 'delta_w', 'delta_w_up': 'delta_w', 'delta_w_down': 'delta_w', 'delta_ln2_g': 'delta_w', 'delta_ln2_b': 'delta_w', 'new_m_w_in': 'new_m', 'new_m_conv_w': 'new_m', 'new_m_sinks': 'new_m', 'new_m_g_attn': 'new_m', 'new_m_g_conv': 'new_m', 'new_m_w_out': 'new_m', 'new_m_ln1_g': 'new_m', 'new_m_ln1_b': 'new_m', 'new_m_w_gate': 'new_m', 'new_m_w_up': 'new_m', 'new_m_w_down': 'new_m', 'new_m_ln2_g': 'new_m', 'new_m_ln2_b': 'new_m', 'new_v_w_in': 'new_v', 'new_v_conv_w': 'new_v', 'new_v_sinks': 'new_v', 'new_v_g_attn': 'new_v', 'new_v_g_conv': 'new_v', 'new_v_w_out': 'new_v', 'new_v_ln1_g': 'new_v', 'new_v_ln1_b': 'new_v', 'new_v_w_gate': 'new_v', 'new_v_w_up': 'new_v', 'new_v_w_down': 'new_v', 'new_v_ln2_g': 'new_v', 'new_v_ln2_b': 'new_v'}


def _forward(args):
    return _fwd_reference(*[args[k] for k in FWD_PARAMS])


def _output_shape():
    def fwd():
        inp = _fwd_setup_inputs(0)
        return _fwd_reference(*[inp[k] for k in FWD_PARAMS])
    out = _jax.eval_shape(fwd)
    return out.shape, out.dtype

N_MICROBATCH = 1
ADAM_LR = 0.001
ADAM_B1 = 0.9
ADAM_B2 = 0.999
ADAM_EPS = 1e-08
ADAM_WD = 0.01
ADAM_STEP = 10
PER_EXAMPLE_BATCH_AXIS = {'x': 0, 'positions': 0, 'loss_target': 0}
SHARED_INPUTS = []
_WEIGHT_DTYPES = {'w_in': _jnp.float32, 'conv_w': _jnp.float32, 'sinks': _jnp.float32, 'g_attn': _jnp.float32, 'g_conv': _jnp.float32, 'w_out': _jnp.float32, 'ln1_g': _jnp.float32, 'ln1_b': _jnp.float32, 'w_gate': _jnp.float32, 'w_up': _jnp.float32, 'w_down': _jnp.float32, 'ln2_g': _jnp.float32, 'ln2_b': _jnp.float32}
MOMENT_SCALE = {'w_in': 4.772449e-02, 'conv_w': 4.207751e-02, 'sinks': 1.418772e-02, 'g_attn': 4.233038e-02, 'g_conv': 4.458684e-02, 'w_out': 6.992915e-02, 'ln1_g': 5.437723e-01, 'ln1_b': 2.774332e-01, 'w_gate': 1.576985e-02, 'w_up': 1.531647e-02, 'w_down': 4.268123e-02, 'ln2_g': 1.600763e+01, 'ln2_b': 4.122828e-01}


def _to_microbatches(a, axis):
    t = _jnp.moveaxis(a, axis, 0)
    t = t.reshape((N_MICROBATCH, t.shape[0] // N_MICROBATCH) + t.shape[1:])
    return _jnp.moveaxis(t, 1, axis + 1)


def setup_inputs(seed: int = 0) -> dict:
    inp = _fwd_setup_inputs(seed)
    key = _jax.random.fold_in(_jax.random.key(seed), 7919)
    shape, _ = _output_shape()
    out = dict(inp)
    out["loss_target"] = _jax.random.normal(_jax.random.fold_in(key, 0), shape, _jnp.float32)
    for i, name in enumerate(TWIN_WEIGHTS):
        w = inp[name].astype(_jnp.float32)
        if MOMENT_SCALE is None:
            s = _jnp.sqrt(_jnp.mean(_jnp.square(w)) + 1e-30)
        else:
            s = MOMENT_SCALE[name]
        km, kv = _jax.random.split(_jax.random.fold_in(key, i + 1))
        out[name] = w
        out["m_" + name] = s * _jax.random.normal(km, w.shape, _jnp.float32)
        out["v_" + name] = (s * s) * _jax.random.uniform(kv, w.shape, _jnp.float32, 0.5, 1.5)
    if N_MICROBATCH > 1:
        for name, axis in PER_EXAMPLE_BATCH_AXIS.items():
            out[name] = _to_microbatches(out[name], axis)
    return {'x': out['x'], 'positions': out['positions'], 'w_in': out['w_in'], 'conv_w': out['conv_w'], 'sinks': out['sinks'], 'g_attn': out['g_attn'], 'g_conv': out['g_conv'], 'w_out': out['w_out'], 'ln1_g': out['ln1_g'], 'ln1_b': out['ln1_b'], 'w_gate': out['w_gate'], 'w_up': out['w_up'], 'w_down': out['w_down'], 'ln2_g': out['ln2_g'], 'ln2_b': out['ln2_b'], 'loss_target': out['loss_target'], 'm_w_in': out['m_w_in'], 'm_conv_w': out['m_conv_w'], 'm_sinks': out['m_sinks'], 'm_g_attn': out['m_g_attn'], 'm_g_conv': out['m_g_conv'], 'm_w_out': out['m_w_out'], 'm_ln1_g': out['m_ln1_g'], 'm_ln1_b': out['m_ln1_b'], 'm_w_gate': out['m_w_gate'], 'm_w_up': out['m_w_up'], 'm_w_down': out['m_w_down'], 'm_ln2_g': out['m_ln2_g'], 'm_ln2_b': out['m_ln2_b'], 'v_w_in': out['v_w_in'], 'v_conv_w': out['v_conv_w'], 'v_sinks': out['v_sinks'], 'v_g_attn': out['v_g_attn'], 'v_g_conv': out['v_g_conv'], 'v_w_out': out['v_w_out'], 'v_ln1_g': out['v_ln1_g'], 'v_ln1_b': out['v_ln1_b'], 'v_w_gate': out['v_w_gate'], 'v_w_up': out['v_w_up'], 'v_w_down': out['v_w_down'], 'v_ln2_g': out['v_ln2_g'], 'v_ln2_b': out['v_ln2_b']}


def _loss(weights, diff, rest, loss_target):
    with _jax.named_scope("forward"):
        args = {**rest, TWIN_DIFF_INPUT: diff, **{k: w.astype(_WEIGHT_DTYPES[k]) for k, w in weights.items()}}
        y = _forward(args)
    with _jax.named_scope("loss_head"):
        err = _jnp.square(y.astype(_jnp.float32) - loss_target)
        return 0.5 * _jnp.sum(_jnp.mean(err, axis=-1)) if err.ndim else 0.5 * err


def _adamw(w, g, m, v):
    m = ADAM_B1 * m + (1.0 - ADAM_B1) * g
    v = ADAM_B2 * v + (1.0 - ADAM_B2) * _jnp.square(g)
    m_hat = m / (1.0 - ADAM_B1 ** ADAM_STEP)
    v_hat = v / (1.0 - ADAM_B2 ** ADAM_STEP)
    delta = -ADAM_LR * (m_hat / (_jnp.sqrt(v_hat) + ADAM_EPS) + ADAM_WD * w)
    return delta, m, v


def reference(x, positions, w_in, conv_w, sinks, g_attn, g_conv, w_out, ln1_g, ln1_b, w_gate, w_up, w_down, ln2_g, ln2_b, loss_target, m_w_in, m_conv_w, m_sinks, m_g_attn, m_g_conv, m_w_out, m_ln1_g, m_ln1_b, m_w_gate, m_w_up, m_w_down, m_ln2_g, m_ln2_b, v_w_in, v_conv_w, v_sinks, v_g_attn, v_g_conv, v_w_out, v_ln1_g, v_ln1_b, v_w_gate, v_w_up, v_w_down, v_ln2_g, v_ln2_b):
    given = dict(x=x, positions=positions, w_in=w_in, conv_w=conv_w, sinks=sinks, g_attn=g_attn, g_conv=g_conv, w_out=w_out, ln1_g=ln1_g, ln1_b=ln1_b, w_gate=w_gate, w_up=w_up, w_down=w_down, ln2_g=ln2_g, ln2_b=ln2_b, loss_target=loss_target, m_w_in=m_w_in, m_conv_w=m_conv_w, m_sinks=m_sinks, m_g_attn=m_g_attn, m_g_conv=m_g_conv, m_w_out=m_w_out, m_ln1_g=m_ln1_g, m_ln1_b=m_ln1_b, m_w_gate=m_w_gate, m_w_up=m_w_up, m_w_down=m_w_down, m_ln2_g=m_ln2_g, m_ln2_b=m_ln2_b, v_w_in=v_w_in, v_conv_w=v_conv_w, v_sinks=v_sinks, v_g_attn=v_g_attn, v_g_conv=v_g_conv, v_w_out=v_w_out, v_ln1_g=v_ln1_g, v_ln1_b=v_ln1_b, v_w_gate=v_w_gate, v_w_up=v_w_up, v_w_down=v_w_down, v_ln2_g=v_ln2_g, v_ln2_b=v_ln2_b)
    weights = {n: given[n] for n in TWIN_WEIGHTS}
    shared = {n: given[n] for n in SHARED_INPUTS}
    per_example = {n: given[n] for n in ['x', 'positions']}
    grad_fn = _jax.value_and_grad(_loss, argnums=(0, 1))

    def one_microbatch(ex, loss_target):
        ex = dict(ex)
        diff = ex.pop(TWIN_DIFF_INPUT)
        return grad_fn(weights, diff, {**shared, **ex}, loss_target)

    if N_MICROBATCH == 1:
        loss, (grad_w, grad_x) = one_microbatch(per_example, given["loss_target"])
    else:
        def body(carry, xs):
            loss_sum, grad_sum = carry
            l_k, (gw_k, gx_k) = one_microbatch(xs[0], xs[1])
            with _jax.named_scope("update"):
                return (loss_sum + l_k, _jax.tree.map(_jnp.add, grad_sum, gw_k)), gx_k

        init = (_jnp.zeros((), _jnp.float32), _jax.tree.map(_jnp.zeros_like, weights))
        (loss, grad_w), grad_x = _jax.lax.scan(body, init, (per_example, given["loss_target"]))
    with _jax.named_scope("update"):
        delta_w, new_m, new_v = {}, {}, {}
        for n in TWIN_WEIGHTS:
            delta_w[n], new_m[n], new_v[n] = _adamw(weights[n], grad_w[n], given["m_" + n], given["v_" + n])
    return (loss, grad_x, *[grad_w[n] for n in TWIN_WEIGHTS], *[delta_w[n] for n in TWIN_WEIGHTS],
            *[new_m[n] for n in TWIN_WEIGHTS], *[new_v[n] for n in TWIN_WEIGHTS])
```

```python
import functools

import jax
import jax.numpy as jnp
from jax import lax
from jax.experimental import pallas as pl
from jax.experimental.pallas import tpu as pltpu

_F32 = jnp.float32
_CDT = jnp.bfloat16

HEAD_DIM = 64
WINDOW = 128
N_KV_HEADS = 4
KV_WIDTH = N_KV_HEADS * HEAD_DIM
ROT_DIM = HEAD_DIM // 4
ROPE_THETA = 500000.0
ATTN_SCALE = HEAD_DIM ** -0.5
DEPTH = 1
DEEPNORM_ALPHA = (2 * DEPTH) ** 0.25
LN_EPS = 1e-5
RMS_EPS = 1e-6
ADAM_LR = 0.001
ADAM_B1 = 0.9
ADAM_B2 = 0.999
ADAM_EPS = 1e-08
ADAM_WD = 0.01
ADAM_STEP = 10
N_DEV = 8
MASKED = -1e30

V7X_VMEM_BYTES = 64 * 1024 * 1024
V7X_LANES = 128
V7X_SUBLANES = 8
_MESH = pl.DeviceIdType.MESH
_ANY = pl.BlockSpec(memory_space=pl.ANY)


def _vmem_limit(block_bytes, scratch_bytes=0):
    want = 2 * block_bytes + scratch_bytes + 16 * 1024 * 1024
    return int(min(max(want, 32 * 1024 * 1024), V7X_VMEM_BYTES - 8 * 1024 * 1024))


def _nbytes(shape, dtype):
    n = 1
    for s in shape:
        n *= s
    return n * jnp.dtype(dtype).itemsize


def _pick(n, candidates):
    for c in candidates:
        if n % c == 0:
            return c
    raise ValueError(f"no tile of {candidates} divides {n}")


_DOT_DIMS = {"nn": ((1,), (0,)), "nt": ((1,), (1,)), "tn": ((0,), (0,))}


def _dot(a, b, mode):
    return lax.dot_general(a.astype(_CDT), b.astype(_CDT), (_DOT_DIMS[mode], ((), ())),
                           preferred_element_type=_F32)


def _accumulate(ref, val, first):
    @pl.when(first)
    def _():
        ref[...] = val

    @pl.when(jnp.logical_not(first))
    def _():
        ref[...] += val


def _matmul(name, groups, m, n, k, tm, tn, tk, extras, outs, epilogue):
    assert m % tm == 0 and n % tn == 0 and k % tk == 0, (name, m, n, k, tm, tn, tk)
    nk = k // tk
    terms = [t for g in groups for t in g]
    operands, in_specs, block_bytes = [], [], 0
    for a, b, mode in terms:
        assert a.shape == ((k, m) if mode == "tn" else (m, k)), (name, a.shape, mode)
        assert b.shape == ((n, k) if mode == "nt" else (k, n)), (name, b.shape, mode)
        if mode == "tn":
            a_blk, a_map = (tk, tm), (lambda i, j, kk: (kk, i))
        else:
            a_blk, a_map = (tm, tk), (lambda i, j, kk: (i, kk))
        if mode == "nt":
            b_blk, b_map = (tn, tk), (lambda i, j, kk: (j, kk))
        else:
            b_blk, b_map = (tk, tn), (lambda i, j, kk: (kk, j))
        operands += [a, b]
        in_specs += [pl.BlockSpec(a_blk, a_map), pl.BlockSpec(b_blk, b_map)]
        block_bytes += _nbytes(a_blk, a.dtype) + _nbytes(b_blk, b.dtype)
    for arr, blk, imap in extras:
        operands.append(arr)
        in_specs.append(pl.BlockSpec(blk, lambda i, j, kk, imap=imap: imap(i, j)))
        block_bytes += _nbytes(blk, arr.dtype)
    out_shape, out_specs = [], []
    for shape, dtype, blk, imap in outs:
        out_shape.append(jax.ShapeDtypeStruct(shape, dtype))
        out_specs.append(pl.BlockSpec(blk, lambda i, j, kk, imap=imap: imap(i, j)))
        block_bytes += _nbytes(blk, dtype)
    n_terms, n_extra, n_out, n_groups = len(terms), len(extras), len(outs), len(groups)
    scratch = [pltpu.VMEM((tm, tn), _F32) for _ in range(n_groups)] if nk > 1 else []

    def body(*refs):
        term_refs = refs[:2 * n_terms]
        extra_refs = refs[2 * n_terms:2 * n_terms + n_extra]
        out_refs = refs[2 * n_terms + n_extra:2 * n_terms + n_extra + n_out]
        acc_refs = refs[2 * n_terms + n_extra + n_out:]
        i, j, kk = pl.program_id(0), pl.program_id(1), pl.program_id(2)
        first = jnp.logical_and(i == 0, j == 0)
        partial, t = [], 0
        for g in groups:
            s = None
            for _, _, mode in g:
                d = _dot(term_refs[2 * t][...], term_refs[2 * t + 1][...], mode)
                s = d if s is None else s + d
                t += 1
            partial.append(s)
        if nk == 1:
            epilogue(partial, extra_refs, out_refs, first)
        else:
            for acc, p in zip(acc_refs, partial):
                _accumulate(acc, p, kk == 0)

            @pl.when(kk == nk - 1)
            def _():
                epilogue([acc[...] for acc in acc_refs], extra_refs, out_refs, first)

    return pl.pallas_call(
        body, name=name, grid=(m // tm, n // tn, nk),
        in_specs=in_specs, out_specs=out_specs, out_shape=out_shape, scratch_shapes=scratch,
        compiler_params=pltpu.CompilerParams(
            dimension_semantics=("arbitrary", "arbitrary", "arbitrary"),
            vmem_limit_bytes=_vmem_limit(block_bytes, n_groups * tm * tn * 4 if nk > 1 else 0)),
    )(*operands)


def _store_epilogue(accs, extra_refs, out_refs, first):
    for acc, ref in zip(accs, out_refs):
        ref[...] = acc.astype(ref.dtype)


def _tile_ij(i, j):
    return (i, j)


def _row_i(i, j):
    return (i, 0)


def _whole(i, j):
    return (0, 0)


def _mean(v):
    return jnp.mean(v, axis=-1, keepdims=True)


def _ln_fwd(r, g, b):
    xc = r - _mean(r)
    rstd = lax.rsqrt(_mean(xc * xc) + LN_EPS)
    xhat = xc * rstd
    return xhat * g + b, xhat, rstd


def _ln_bwd(dy, xhat, rstd, g):
    dxh = dy * g
    dr = rstd * (dxh - _mean(dxh) - xhat * _mean(dxh * xhat))
    return dr, jnp.sum(dy * xhat, axis=0, keepdims=True), jnp.sum(dy, axis=0, keepdims=True)


def _rms_fwd(a, g):
    rstd = lax.rsqrt(_mean(a * a) + RMS_EPS)
    return a * rstd * g


def _rms_bwd(dm, a, g):
    nhat = a * lax.rsqrt(_mean(a * a) + RMS_EPS)
    rstd = lax.rsqrt(_mean(a * a) + RMS_EPS)
    dn = dm * g
    da = rstd * (dn - nhat * _mean(dn * nhat))
    return da, jnp.sum(dm * nhat, axis=0, keepdims=True)


def _lane(shape):
    return lax.broadcasted_iota(jnp.int32, shape, 1)


def _row(shape):
    return lax.broadcasted_iota(jnp.int32, shape, 0)


def _rope_tables(pos, invf):
    ang = pos.astype(_F32) * invf
    lane = _lane(ang.shape)
    in_rot = (lane % HEAD_DIM) < ROT_DIM
    first = (lane % ROT_DIM) < ROT_DIM // 2
    cos = jnp.where(in_rot, jnp.cos(ang), 1.0)
    sin = jnp.sin(ang)
    sgn = jnp.where(in_rot, jnp.where(first, -sin, sin), 0.0)
    return cos, sgn


def _rope(t, cos, sgn, sign):
    half = ROT_DIM // 2
    first = (_lane(t.shape) % ROT_DIM) < half
    partner = jnp.where(first, pltpu.roll(t, V7X_LANES - half, 1), pltpu.roll(t, half, 1))
    return t * cos + partner * (sgn * sign)


def _dup_head(t, h):
    g = t[:, 128 * (h // 2):128 * (h // 2) + 128]
    r = pltpu.roll(g, HEAD_DIM, 1)
    lo = _lane(g.shape) < HEAD_DIM
    return jnp.where(lo, g, r) if h % 2 == 0 else jnp.where(lo, r, g)


def _fold_halves(t):
    return t + pltpu.roll(t, HEAD_DIM, 1)


def _halves(t):
    lo = _lane(t.shape) < HEAD_DIM
    zero = jnp.zeros_like(t)
    return jnp.where(lo, t, zero), jnp.where(lo, zero, t)


def _band_mask(n_keys, first_block):
    i = _row((WINDOW, n_keys))
    j = _lane((WINDOW, n_keys))
    valid = jnp.logical_and(j >= i + 1, j <= i + WINDOW)
    if first_block is not None:
        valid = jnp.logical_and(valid, jnp.logical_or(j >= WINDOW, jnp.logical_not(first_block)))
    return valid


def _shift_down(z, halo, k):
    rows = z.shape[0]
    out = pltpu.roll(z, k, 0)
    r = _row(z.shape)
    for t in range(k):
        out = jnp.where(r == t, halo[V7X_SUBLANES - k + t:V7X_SUBLANES - k + t + 1, :], out)
    del rows
    return out


def _shift_up(z, halo, k):
    rows = z.shape[0]
    out = pltpu.roll(z, rows - k, 0)
    r = _row(z.shape)
    for t in range(k):
        out = jnp.where(r == rows - k + t, halo[t:t + 1, :], out)
    return out


class _Dims:
    def __init__(self, s, d, d_ff):
        self.s, self.d, self.d_ff = s, d, d_ff
        self.aw = d // 2
        self.cw = d - self.aw
        self.nq = self.aw // HEAD_DIM
        self.group = self.nq // N_KV_HEADS
        assert self.group % 2 == 0, "a 128-lane pair of query heads must share its kv head"
        self.inw = self.aw + 2 * KV_WIDTH + 3 * self.cw
        self.o_k = self.aw
        self.o_v = self.aw + KV_WIDTH
        self.o_cg = self.aw + 2 * KV_WIDTH
        self.o_bg = self.o_cg + self.cw
        self.o_u = self.o_bg + self.cw
        self.nb = s // WINDOW
        assert s % WINDOW == 0


def _mixer_fwd(dm, proj, pos, invf, sinks, g_attn, g_conv, conv_w8):
    s, d, aw, cw, nq, inw, nb = dm.s, dm.d, dm.aw, dm.cw, dm.nq, dm.inw, dm.nb

    def body(pp_ref, pc_ref, posp_ref, posc_ref, invf_ref, sinks_ref, ga_ref, gc_ref, cw_ref,
             mixed_ref, attn_ref, lse_ref, y_ref):
        n = pl.program_id(0)
        cos_c, sgn_c = _rope_tables(posc_ref[...], invf_ref[...])
        cos_p, sgn_p = _rope_tables(posp_ref[...], invf_ref[...])
        kk = jnp.concatenate(
            [jnp.concatenate([_rope(ref[:, dm.o_k + 128 * g:dm.o_k + 128 * g + 128], c, sg, 1.0)
                              for g in range(KV_WIDTH // 128)], axis=1)
             for ref, c, sg in ((pp_ref, cos_p, sgn_p), (pc_ref, cos_c, sgn_c))], axis=0)
        vv = jnp.concatenate([pp_ref[:, dm.o_v:dm.o_v + KV_WIDTH], pc_ref[:, dm.o_v:dm.o_v + KV_WIDTH]], axis=0)
        k2 = [_dup_head(kk, h).astype(_CDT) for h in range(N_KV_HEADS)]
        v2 = [_halves(_dup_head(vv, h).astype(_CDT)) for h in range(N_KV_HEADS)]
        valid = _band_mask(2 * WINDOW, n == 0)
        for j in range(nq // 2):
            h = (2 * j) // dm.group
            qp = _rope(pc_ref[:, 128 * j:128 * j + 128], cos_c, sgn_c, 1.0).astype(_CDT)
            out = None
            for half, qh in enumerate(_halves(qp)):
                hq = 2 * j + half
                sc = jnp.where(valid, _dot(qh, k2[h], "nt") * ATTN_SCALE, MASKED)
                sink = sinks_ref[0, hq]
                mx = jnp.maximum(jnp.max(sc, axis=1, keepdims=True), sink)
                p = jnp.exp(sc - mx)
                den = jnp.sum(p, axis=1, keepdims=True) + jnp.exp(sink - mx)
                o = _dot(p / den, v2[h][half], "nn")
                out = o if out is None else out + o
                lse_ref[:, hq:hq + 1] = mx + jnp.log(den)
            attn_ref[:, 128 * j:128 * j + 128] = out
        mixed_ref[:, 0:aw] = _rms_fwd(attn_ref[...], ga_ref[...]).astype(mixed_ref.dtype)

        z = pc_ref[:, dm.o_cg:dm.o_cg + cw] * pc_ref[:, dm.o_u:dm.o_u + cw]
        top = WINDOW - V7X_SUBLANES
        halo = pp_ref[top:WINDOW, dm.o_cg:dm.o_cg + cw] * pp_ref[top:WINDOW, dm.o_u:dm.o_u + cw]
        halo = jnp.where(n == 0, jnp.zeros_like(halo), halo)
        y = cw_ref[0:1, :] * _shift_down(z, halo, 2) + cw_ref[1:2, :] * _shift_down(z, halo, 1) + cw_ref[2:3, :] * z
        y_ref[...] = y
        conv = pc_ref[:, dm.o_bg:dm.o_bg + cw] * y
        mixed_ref[:, aw:d] = _rms_fwd(conv, gc_ref[...]).astype(mixed_ref.dtype)

    prev = lambda n: (jnp.maximum(n - 1, 0), 0)
    cur = lambda n: (n, 0)
    fixed = lambda n: (0, 0)
    blocks = 2 * WINDOW * inw * 4 + WINDOW * (d * 2 + aw * 4 + cw * 4 + nq * 4)
    return pl.pallas_call(
        body, name="mixer_fwd", grid=(nb,),
        in_specs=[pl.BlockSpec((WINDOW, inw), prev), pl.BlockSpec((WINDOW, inw), cur),
                  pl.BlockSpec((WINDOW, 1), prev), pl.BlockSpec((WINDOW, 1), cur),
                  pl.BlockSpec((1, V7X_LANES), fixed), pl.BlockSpec(memory_space=pltpu.SMEM),
                  pl.BlockSpec((1, aw), fixed), pl.BlockSpec((1, cw), fixed), pl.BlockSpec((V7X_SUBLANES, cw), fixed)],
        out_specs=[pl.BlockSpec((WINDOW, d), cur), pl.BlockSpec((WINDOW, aw), cur),
                   pl.BlockSpec((WINDOW, nq), cur), pl.BlockSpec((WINDOW, cw), cur)],
        out_shape=[jax.ShapeDtypeStruct((s, d), _CDT), jax.ShapeDtypeStruct((s, aw), _F32),
                   jax.ShapeDtypeStruct((s, nq), _F32), jax.ShapeDtypeStruct((s, cw), _F32)],
        compiler_params=pltpu.CompilerParams(dimension_semantics=("arbitrary",), vmem_limit_bytes=_vmem_limit(blocks)),
    )(proj, proj, pos, pos, invf, sinks, g_attn, g_conv, conv_w8)


def _mixer_bwd(dm, proj, pos, invf, sinks, g_attn, g_conv, conv_w8, dmixed, attn, lse, y):
    s, d, aw, cw, nq, inw, nb = dm.s, dm.d, dm.aw, dm.cw, dm.nq, dm.inw, dm.nb

    def body(pp_ref, pc_ref, pn_ref, posp_ref, posc_ref, posn_ref, dmc_ref, dmn_ref, ac_ref, an_ref,
             lsec_ref, lsen_ref, yc_ref, yn_ref, invf_ref, sinks_ref, ga_ref, gc_ref, cw_ref,
             dproj_ref, dga_ref, dgc_ref, dsinks_ref, dcw_ref):
        n = pl.program_id(0)
        first = n == 0
        has_next = n < nb - 1
        cos_p, sgn_p = _rope_tables(posp_ref[...], invf_ref[...])
        cos_c, sgn_c = _rope_tables(posc_ref[...], invf_ref[...])
        cos_n, sgn_n = _rope_tables(posn_ref[...], invf_ref[...])

        da_c, dga = _rms_bwd(dmc_ref[:, 0:aw], ac_ref[...], ga_ref[...])
        da_n, _ = _rms_bwd(dmn_ref[:, 0:aw], an_ref[...], ga_ref[...])
        _accumulate(dga_ref, dga, first)
        kk = jnp.concatenate(
            [jnp.concatenate([_rope(ref[:, dm.o_k + 128 * g:dm.o_k + 128 * g + 128], c, sg, 1.0)
                              for g in range(KV_WIDTH // 128)], axis=1)
             for ref, c, sg in ((pp_ref, cos_p, sgn_p), (pc_ref, cos_c, sgn_c))], axis=0)
        vv = jnp.concatenate([pp_ref[:, dm.o_v:dm.o_v + KV_WIDTH], pc_ref[:, dm.o_v:dm.o_v + KV_WIDTH]], axis=0)
        k2 = [_dup_head(kk, h).astype(_CDT) for h in range(N_KV_HEADS)]
        v2 = [_dup_head(vv, h).astype(_CDT) for h in range(N_KV_HEADS)]
        valid_c = _band_mask(2 * WINDOW, first)
        valid_n = jnp.logical_and(_band_mask(WINDOW, None), has_next)
        dk2 = [None] * N_KV_HEADS
        dv2 = [None] * N_KV_HEADS
        dsinks = jnp.zeros((1, nq), _F32)
        head_lane = _lane((1, nq))

        def tile(q_half, do_half, o_pair, da_pair, lse_col, keys, vals, valid, half):
            sc = _dot(q_half, keys, "nt") * ATTN_SCALE
            p = jnp.exp(jnp.where(valid, sc - lse_col, MASKED))
            lo = _lane(o_pair.shape) < HEAD_DIM
            mine = lo if half == 0 else jnp.logical_not(lo)
            delta = jnp.sum(jnp.where(mine, o_pair * da_pair, 0.0), axis=1, keepdims=True)
            dp = _dot(do_half, vals, "nt")
            return p, p * (dp - delta) * ATTN_SCALE, delta

        for j in range(nq // 2):
            h = (2 * j) // dm.group
            cols = slice(128 * j, 128 * j + 128)
            q_c = _halves(_rope(pc_ref[:, cols], cos_c, sgn_c, 1.0).astype(_CDT))
            q_n = _halves(_rope(pn_ref[:, cols], cos_n, sgn_n, 1.0).astype(_CDT))
            do_c = _halves(da_c[:, cols].astype(_CDT))
            do_n = _halves(da_n[:, cols].astype(_CDT))
            k_halves = _halves(k2[h])
            dq = None
            for half in range(2):
                hq = 2 * j + half
                p, ds, delta = tile(q_c[half], do_c[half], ac_ref[:, cols], da_c[:, cols], lsec_ref[:, hq:hq + 1],
                                    k2[h], v2[h], valid_c, half)
                ds = ds.astype(_CDT)
                t = _dot(ds, k_halves[half], "nn")
                dq = t if dq is None else dq + t
                dk = _dot(ds, q_c[half], "tn")[WINDOW:2 * WINDOW, :]
                dv = _dot(p.astype(_CDT), do_c[half], "tn")[WINDOW:2 * WINDOW, :]
                psink = jnp.exp(sinks_ref[0, hq] - lsec_ref[:, hq:hq + 1])
                dsinks = dsinks + jnp.where(head_lane == hq, -jnp.sum(psink * delta), 0.0)
                p, ds, _ = tile(q_n[half], do_n[half], an_ref[:, cols], da_n[:, cols], lsen_ref[:, hq:hq + 1],
                                k2[h][WINDOW:2 * WINDOW, :], v2[h][WINDOW:2 * WINDOW, :], valid_n, half)
                dk = dk + _dot(ds.astype(_CDT), q_n[half], "tn")
                dv = dv + _dot(p.astype(_CDT), do_n[half], "tn")
                dk2[h] = dk if dk2[h] is None else dk2[h] + dk
                dv2[h] = dv if dv2[h] is None else dv2[h] + dv
            dproj_ref[:, cols] = _rope(dq, cos_c, sgn_c, -1.0).astype(dproj_ref.dtype)
        _accumulate(dsinks_ref, dsinks, first)
        lo = _lane((WINDOW, 128)) < HEAD_DIM
        for g in range(KV_WIDTH // 128):
            dk = jnp.where(lo, _fold_halves(dk2[2 * g]), _fold_halves(dk2[2 * g + 1]))
            dv = jnp.where(lo, _fold_halves(dv2[2 * g]), _fold_halves(dv2[2 * g + 1]))
            dproj_ref[:, dm.o_k + 128 * g:dm.o_k + 128 * g + 128] = _rope(dk, cos_c, sgn_c, -1.0).astype(dproj_ref.dtype)
            dproj_ref[:, dm.o_v + 128 * g:dm.o_v + 128 * g + 128] = dv.astype(dproj_ref.dtype)

        bg = pc_ref[:, dm.o_bg:dm.o_bg + cw]
        yc = yc_ref[...]
        dconv, dgc = _rms_bwd(dmc_ref[:, aw:d], bg * yc, gc_ref[...])
        _accumulate(dgc_ref, dgc, first)
        dproj_ref[:, dm.o_bg:dm.o_bg + cw] = (dconv * yc).astype(dproj_ref.dtype)
        dy = dconv * bg
        bg_n = pn_ref[0:V7X_SUBLANES, dm.o_bg:dm.o_bg + cw]
        dconv_n, _ = _rms_bwd(dmn_ref[0:V7X_SUBLANES, aw:d], bg_n * yn_ref[...], gc_ref[...])
        halo = jnp.where(has_next, dconv_n * bg_n, 0.0)
        dy1 = _shift_up(dy, halo, 1)
        dy2 = _shift_up(dy, halo, 2)
        dz = cw_ref[2:3, :] * dy + cw_ref[1:2, :] * dy1 + cw_ref[0:1, :] * dy2
        cg = pc_ref[:, dm.o_cg:dm.o_cg + cw]
        u = pc_ref[:, dm.o_u:dm.o_u + cw]
        dproj_ref[:, dm.o_cg:dm.o_cg + cw] = (dz * u).astype(dproj_ref.dtype)
        dproj_ref[:, dm.o_u:dm.o_u + cw] = (dz * cg).astype(dproj_ref.dtype)
        z = cg * u
        dcw = jnp.concatenate(
            [jnp.sum(z * t, axis=0, keepdims=True) for t in (dy2, dy1, dy)]
            + [jnp.zeros((V7X_SUBLANES - 3, cw), _F32)], axis=0)
        _accumulate(dcw_ref, dcw, first)

    prev = lambda n: (jnp.maximum(n - 1, 0), 0)
    cur = lambda n: (n, 0)
    nxt = lambda n: (jnp.minimum(n + 1, nb - 1), 0)
    nxt8 = lambda n: (jnp.minimum((n + 1) * (WINDOW // V7X_SUBLANES), s // V7X_SUBLANES - 1), 0)
    fixed = lambda n: (0, 0)
    blocks = WINDOW * (3 * inw * 4 + 2 * d * 4 + 2 * aw * 4 + cw * 4 + inw * 2)
    return pl.pallas_call(
        body, name="mixer_bwd", grid=(nb,),
        in_specs=[pl.BlockSpec((WINDOW, inw), prev), pl.BlockSpec((WINDOW, inw), cur), pl.BlockSpec((WINDOW, inw), nxt),
                  pl.BlockSpec((WINDOW, 1), prev), pl.BlockSpec((WINDOW, 1), cur), pl.BlockSpec((WINDOW, 1), nxt),
                  pl.BlockSpec((WINDOW, d), cur), pl.BlockSpec((WINDOW, d), nxt),
                  pl.BlockSpec((WINDOW, aw), cur), pl.BlockSpec((WINDOW, aw), nxt),
                  pl.BlockSpec((WINDOW, nq), cur), pl.BlockSpec((WINDOW, nq), nxt),
                  pl.BlockSpec((WINDOW, cw), cur), pl.BlockSpec((V7X_SUBLANES, cw), nxt8),
                  pl.BlockSpec((1, V7X_LANES), fixed), pl.BlockSpec(memory_space=pltpu.SMEM),
                  pl.BlockSpec((1, aw), fixed), pl.BlockSpec((1, cw), fixed), pl.BlockSpec((V7X_SUBLANES, cw), fixed)],
        out_specs=[pl.BlockSpec((WINDOW, inw), cur), pl.BlockSpec((1, aw), fixed), pl.BlockSpec((1, cw), fixed),
                   pl.BlockSpec((1, nq), fixed), pl.BlockSpec((V7X_SUBLANES, cw), fixed)],
        out_shape=[jax.ShapeDtypeStruct((s, inw), _CDT), jax.ShapeDtypeStruct((1, aw), _F32),
                   jax.ShapeDtypeStruct((1, cw), _F32), jax.ShapeDtypeStruct((1, nq), _F32),
                   jax.ShapeDtypeStruct((V7X_SUBLANES, cw), _F32)],
        compiler_params=pltpu.CompilerParams(dimension_semantics=("arbitrary",), vmem_limit_bytes=_vmem_limit(blocks)),
    )(proj, proj, proj, pos, pos, pos, dmixed, dmixed, attn, attn, lse, lse, y, y, invf, sinks, g_attn, g_conv, conv_w8)


def _position():
    return lax.axis_index("x"), lax.axis_index("y"), lax.axis_index("c")


def _linear(px, py, pc):
    return 4 * px + 2 * py + pc


def _all_gather(name, shards):
    n = len(shards)

    def body(*refs):
        ins, outs = refs[:n], refs[n:2 * n]
        send_sems, recv_sems, local_sems = refs[2 * n:]
        x, y, c = _position()
        me, sibling = (x, y, c), (x, y, 1 - c)
        chips = [(1 - x, y), (x, 1 - y), (1 - x, 1 - y)]

        def rows(a, px, py, pc):
            r = ins[a].shape[0]
            return outs[a].at[pl.ds(pl.multiple_of(_linear(px, py, pc) * r, 16), r), :]

        def copy(a, k, block, to, src=None):
            return pltpu.make_async_remote_copy(
                src_ref=rows(a, *block) if src is None else src, dst_ref=rows(a, *block),
                send_sem=send_sems.at[a, k], recv_sem=recv_sems.at[a, k], device_id=to, device_id_type=_MESH)

        mine = [pltpu.make_async_copy(ins[a], rows(a, *me), local_sems.at[a]) for a in range(n)]
        for cp in mine:
            cp.start()
        first = []
        for a in range(n):
            first.append(copy(a, 0, me, sibling, src=ins[a]))
            first += [copy(a, 1 + j, me, (*chip, c), src=ins[a]) for j, chip in enumerate(chips)]
        for cp in first:
            cp.start()
        passed = [[copy(a, 4 + j, (*chip, c), sibling) for j, chip in enumerate(chips)] for a in range(n)]
        for a in range(n):
            for j, chip in enumerate(chips):
                copy(a, 1 + j, (*chip, c), me).wait_recv()
                passed[a][j].start()
        for a in range(n):
            copy(a, 0, sibling, me).wait_recv()
            for j, chip in enumerate(chips):
                copy(a, 4 + j, (*chip, 1 - c), me).wait_recv()
        for cp in first + [cp for row in passed for cp in row]:
            cp.wait_send()
        for cp in mine:
            cp.wait()

    return pl.pallas_call(
        body, name=name,
        out_shape=[jax.ShapeDtypeStruct((N_DEV * sh.shape[0], sh.shape[1]), sh.dtype) for sh in shards],
        in_specs=[_ANY] * n, out_specs=[_ANY] * n,
        scratch_shapes=[pltpu.SemaphoreType.DMA((n, 7)), pltpu.SemaphoreType.DMA((n, 7)), pltpu.SemaphoreType.DMA((n,))],
    )(*shards)


def _peers(x, y, c):
    out = []
    for k in range(1, N_DEV):
        fx, fy, fc = (k >> 2) & 1, (k >> 1) & 1, k & 1
        out.append((1 - x if fx else x, 1 - y if fy else y, 1 - c if fc else c))
    return out


def _scatter_blocks(name, partials):
    n = len(partials)

    def body(*refs):
        ins, outs = refs[:n], refs[n:2 * n]
        send_sems, recv_sems, local_sems = refs[2 * n:]
        x, y, c = _position()
        me = _linear(x, y, c)
        peers = _peers(x, y, c)

        def block(a, idx):
            r = outs[a].shape[1]
            return ins[a].at[pl.ds(pl.multiple_of(idx * r, 16), r), :]

        mine = [pltpu.make_async_copy(block(a, me), outs[a].at[me], local_sems.at[a]) for a in range(n)]
        for cp in mine:
            cp.start()
        sends = [pltpu.make_async_remote_copy(
            src_ref=block(a, _linear(*peer)), dst_ref=outs[a].at[me], send_sem=send_sems.at[a, k],
            recv_sem=recv_sems.at[a, k], device_id=peer, device_id_type=_MESH)
            for a in range(n) for k, peer in enumerate(peers)]
        for cp in sends:
            cp.start()
        for a in range(n):
            for k, peer in enumerate(peers):
                pltpu.make_async_remote_copy(
                    src_ref=block(a, me), dst_ref=outs[a].at[_linear(*peer)], send_sem=send_sems.at[a, k],
                    recv_sem=recv_sems.at[a, k], device_id=peer, device_id_type=_MESH).wait_recv()
        for cp in sends:
            cp.wait_send()
        for cp in mine:
            cp.wait()

    return pl.pallas_call(
        body, name=name,
        out_shape=[jax.ShapeDtypeStruct((N_DEV, p.shape[0] // N_DEV, p.shape[1]), p.dtype) for p in partials],
        in_specs=[_ANY] * n, out_specs=[_ANY] * n,
        scratch_shapes=[pltpu.SemaphoreType.DMA((n, 7)), pltpu.SemaphoreType.DMA((n, 7)), pltpu.SemaphoreType.DMA((n,))],
    )(*partials)


def _all_reduce_small(name, v):
    rows = v.shape[0]

    def body(v_ref, out_ref, land_ref, send_sems, recv_sems):
        x, y, c = _position()
        me = _linear(x, y, c)
        peers = _peers(x, y, c)
        land_ref[me] = v_ref[...]
        sends = [pltpu.make_async_remote_copy(
            src_ref=v_ref, dst_ref=land_ref.at[me], send_sem=send_sems.at[k], recv_sem=recv_sems.at[k],
            device_id=peer, device_id_type=_MESH) for k, peer in enumerate(peers)]
        for cp in sends:
            cp.start()
        for k, peer in enumerate(peers):
            pltpu.make_async_remote_copy(
                src_ref=v_ref, dst_ref=land_ref.at[_linear(*peer)], send_sem=send_sems.at[k], recv_sem=recv_sems.at[k],
                device_id=peer, device_id_type=_MESH).wait_recv()
        for cp in sends:
            cp.wait_send()
        total = land_ref[0]
        for s in range(1, N_DEV):
            total = total + land_ref[s]
        out_ref[...] = total

    return pl.pallas_call(
        body, name=name, out_shape=jax.ShapeDtypeStruct(v.shape, _F32),
        in_specs=[pl.BlockSpec(memory_space=pltpu.VMEM)], out_specs=pl.BlockSpec(memory_space=pltpu.VMEM),
        scratch_shapes=[pltpu.VMEM((N_DEV, rows, V7X_LANES), _F32), pltpu.SemaphoreType.DMA((7,)), pltpu.SemaphoreType.DMA((7,))],
    )(v)


def _sum_slots(name, landing):
    _, rows, cols = landing.shape
    tr = _pick(rows, (64, 32, 16))

    def body(l_ref, o_ref):
        total = l_ref[0].astype(_F32)
        for s in range(1, N_DEV):
            total = total + l_ref[s].astype(_F32)
        o_ref[...] = total

    return pl.pallas_call(
        body, name=name, grid=(rows // tr,),
        in_specs=[pl.BlockSpec((N_DEV, tr, cols), lambda i: (0, i, 0))],
        out_specs=pl.BlockSpec((tr, cols), lambda i: (i, 0)),
        out_shape=jax.ShapeDtypeStruct((rows, cols), _F32),
        compiler_params=pltpu.CompilerParams(dimension_semantics=("arbitrary",)),
    )(landing)


def _adamw(name, w, g, m, v):
    rows, cols = w.shape
    tr = _pick(rows, (256, 128, 64, 32, 16, 8))

    def body(w_ref, g_ref, m_ref, v_ref, d_ref, nm_ref, nv_ref):
        g = g_ref[...]
        nm = ADAM_B1 * m_ref[...] + (1.0 - ADAM_B1) * g
        nv = ADAM_B2 * v_ref[...] + (1.0 - ADAM_B2) * (g * g)
        m_hat = nm / (1.0 - ADAM_B1 ** ADAM_STEP)
        v_hat = nv / (1.0 - ADAM_B2 ** ADAM_STEP)
        d_ref[...] = -ADAM_LR * (m_hat / (jnp.sqrt(v_hat) + ADAM_EPS) + ADAM_WD * w_ref[...])
        nm_ref[...] = nm
        nv_ref[...] = nv

    spec = pl.BlockSpec((tr, cols), lambda i: (i, 0))
    return pl.pallas_call(
        body, name=name, grid=(rows // tr,), in_specs=[spec] * 4, out_specs=[spec] * 3,
        out_shape=[jax.ShapeDtypeStruct((rows, cols), _F32)] * 3,
        compiler_params=pltpu.CompilerParams(dimension_semantics=("arbitrary",)),
    )(w, g, m, v)


def _pad_rows(a, rows):
    return jnp.pad(a, ((0, rows - a.shape[0]), (0, 0)))


def _pack(parts):
    rows, spans, at = [], [], 0
    for p in parts:
        p = p.reshape(-1)
        r = -(-p.shape[0] // V7X_LANES)
        rows.append(jnp.pad(p, (0, r * V7X_LANES - p.shape[0])).reshape(r, V7X_LANES))
        spans.append((at, r, p.shape[0]))
        at += r
    packed = jnp.concatenate(rows, axis=0)
    return _pad_rows(packed, -(-at // V7X_SUBLANES) * V7X_SUBLANES), spans


def _unpack(packed, spans, shapes):
    return [packed[at:at + r].reshape(-1)[:size].reshape(shape) for (at, r, size), shape in zip(spans, shapes)]


def kernel(x, positions, w_in, conv_w, sinks, g_attn, g_conv, w_out, ln1_g, ln1_b, w_gate, w_up, w_down, ln2_g, ln2_b, loss_target, m_w_in, m_conv_w, m_sinks, m_g_attn, m_g_conv, m_w_out, m_ln1_g, m_ln1_b, m_w_gate, m_w_up, m_w_down, m_ln2_g, m_ln2_b, v_w_in, v_conv_w, v_sinks, v_g_attn, v_g_conv, v_w_out, v_ln1_g, v_ln1_b, v_w_gate, v_w_up, v_w_down, v_ln2_g, v_ln2_b):
    _, s, d = x.shape
    d_ff = N_DEV * w_gate.shape[2]
    dm = _Dims(s, d, d_ff)
    aw, cw, nq, inw = dm.aw, dm.cw, dm.nq, dm.inw
    x2 = x[0]
    x_c = x2.astype(_CDT)
    pos = positions[0].reshape(s, 1)
    inv_freq = ROPE_THETA ** (-jnp.arange(0, ROT_DIM, 2, dtype=_F32) / ROT_DIM)
    invf = jnp.tile(inv_freq, V7X_LANES // (ROT_DIM // 2)).reshape(1, V7X_LANES)

    conv_cols = conv_w.shape[2]
    w_in_t, w_out_f, w_gate_t, w_up_t, w_down_f, conv_all = _all_gather("gather_weights", [
        w_in[0].T.astype(_CDT), w_out[0].astype(_CDT), w_gate[0].T.astype(_CDT), w_up[0].T.astype(_CDT),
        w_down[0].astype(_CDT), _pad_rows(_pad_rows(conv_w[0], V7X_SUBLANES), 16)])
    conv_full = conv_all.reshape(N_DEV, 16, conv_cols)[:, :3, :].transpose(1, 0, 2).reshape(3, cw)
    conv_w8 = _pad_rows(conv_full, V7X_SUBLANES)

    tm = _pick(s, (512, 256, 128))
    tt = _pick(s, (1024, 512, 256, 128))
    tn_in = _pick(inw, (512, 256, 128))
    tn_ff = _pick(d_ff, (512, 256, 128))
    tk_ff = _pick(d_ff, (1408, 1024, 512, 256, 128))
    tk_in = _pick(inw, (1536, 1280, 1024, 512, 256, 128))
    tr = _pick(s, (256, 128))

    (proj,) = _matmul("proj", [[(x_c, w_in_t, "nt")]], s, inw, d, tm, tn_in, d, [],
                      [((s, inw), _F32, (tm, tn_in), _tile_ij)], _store_epilogue)
    mixed, attn, lse, y_conv = _mixer_fwd(dm, proj, pos, invf, sinks, g_attn, g_conv, conv_w8)

    def ln1_epilogue(accs, ex, out, first):
        x_ref, g_ref, b_ref = ex
        h1, xhat, rstd = _ln_fwd(DEEPNORM_ALPHA * x_ref[...] + accs[0], g_ref[...], b_ref[...])
        out[0][...] = h1
        out[1][...] = h1.astype(_CDT)
        out[2][...] = xhat
        out[3][...] = rstd

    h1, h1_c, xhat1, rstd1 = _matmul(
        "out_proj_ln1", [[(mixed, w_out_f, "nn")]], s, d, d, tr, d, d,
        [(x2, (tr, d), _row_i), (ln1_g, (1, d), _whole), (ln1_b, (1, d), _whole)],
        [((s, d), _F32, (tr, d), _row_i), ((s, d), _CDT, (tr, d), _row_i), ((s, d), _F32, (tr, d), _row_i),
         ((s, 1), _F32, (tr, 1), _row_i)], ln1_epilogue)

    def swiglu_epilogue(accs, ex, out, first):
        gate, up = accs
        out[0][...] = gate
        out[1][...] = up
        out[2][...] = (gate * jax.nn.sigmoid(gate) * up).astype(_CDT)

    gate, up, act = _matmul(
        "gate_up", [[(h1_c, w_gate_t, "nt")], [(h1_c, w_up_t, "nt")]], s, d_ff, d, tm, tn_ff, d, [],
        [((s, d_ff), _F32, (tm, tn_ff), _tile_ij), ((s, d_ff), _F32, (tm, tn_ff), _tile_ij),
         ((s, d_ff), _CDT, (tm, tn_ff), _tile_ij)], swiglu_epilogue)

    def ln2_loss_epilogue(accs, ex, out, first):
        h1_ref, tgt_ref, g_ref, b_ref = ex
        yv, xhat, rstd = _ln_fwd(DEEPNORM_ALPHA * h1_ref[...] + accs[0], g_ref[...], b_ref[...])
        err = yv - tgt_ref[...]
        dr2, dg, db = _ln_bwd(err * (1.0 / d), xhat, rstd, g_ref[...])
        out[0][...] = dr2
        out[1][...] = dr2.astype(_CDT)
        _accumulate(out[2], jnp.zeros(out[2].shape, _F32) + 0.5 * jnp.sum(err * err) * (1.0 / d), first)
        _accumulate(out[3], dg, first)
        _accumulate(out[4], db, first)

    dr2, dr2_c, loss_acc, d_ln2_g, d_ln2_b = _matmul(
        "down_ln2_loss", [[(act, w_down_f, "nn")]], s, d, d_ff, tr, d, tk_ff,
        [(h1, (tr, d), _row_i), (loss_target[0], (tr, d), _row_i), (ln2_g, (1, d), _whole), (ln2_b, (1, d), _whole)],
        [((s, d), _F32, (tr, d), _row_i), ((s, d), _CDT, (tr, d), _row_i),
         ((V7X_SUBLANES, V7X_LANES), _F32, (V7X_SUBLANES, V7X_LANES), _whole),
         ((1, d), _F32, (1, d), _whole), ((1, d), _F32, (1, d), _whole)], ln2_loss_epilogue)

    def swiglu_bwd_epilogue(accs, ex, out, first):
        gate_v, up_v = ex[0][...], ex[1][...]
        sig = jax.nn.sigmoid(gate_v)
        out[0][...] = (accs[0] * up_v * (sig * (1.0 + gate_v * (1.0 - sig)))).astype(_CDT)
        out[1][...] = (accs[0] * (gate_v * sig)).astype(_CDT)

    dgate, dup = _matmul(
        "dact", [[(dr2_c, w_down_f, "nt")]], s, d_ff, d, tm, tn_ff, d,
        [(gate, (tm, tn_ff), _tile_ij), (up, (tm, tn_ff), _tile_ij)],
        [((s, d_ff), _CDT, (tm, tn_ff), _tile_ij), ((s, d_ff), _CDT, (tm, tn_ff), _tile_ij)], swiglu_bwd_epilogue)
    (dw_down,) = _matmul("dw_down", [[(act, dr2_c, "tn")]], d_ff, d, s, tn_ff, d, tt, [],
                         [((d_ff, d), _CDT, (tn_ff, d), _row_i)], _store_epilogue)
    dw_gate_t, dw_up_t = _matmul(
        "dw_gate_up", [[(dgate, h1_c, "tn")], [(dup, h1_c, "tn")]], d_ff, d, s, tn_ff, d, tt, [],
        [((d_ff, d), _CDT, (tn_ff, d), _row_i), ((d_ff, d), _CDT, (tn_ff, d), _row_i)], _store_epilogue)

    def ln1_bwd_epilogue(accs, ex, out, first):
        dr2_ref, xhat_ref, rstd_ref, g_ref = ex
        dr1, dg, db = _ln_bwd(DEEPNORM_ALPHA * dr2_ref[...] + accs[0], xhat_ref[...], rstd_ref[...], g_ref[...])
        out[0][...] = dr1
        out[1][...] = dr1.astype(_CDT)
        _accumulate(out[2], dg, first)
        _accumulate(out[3], db, first)

    dr1, dr1_c, d_ln1_g, d_ln1_b = _matmul(
        "dh1_ln1_bwd", [[(dgate, w_gate_t, "nn"), (dup, w_up_t, "nn")]], s, d, d_ff, tr, d, tk_ff,
        [(dr2, (tr, d), _row_i), (xhat1, (tr, d), _row_i), (rstd1, (tr, 1), _row_i), (ln1_g, (1, d), _whole)],
        [((s, d), _F32, (tr, d), _row_i), ((s, d), _CDT, (tr, d), _row_i),
         ((1, d), _F32, (1, d), _whole), ((1, d), _F32, (1, d), _whole)], ln1_bwd_epilogue)
    (dmixed,) = _matmul("dmixed", [[(dr1_c, w_out_f, "nt")]], s, d, d, tm, _pick(d, (512,)), d, [],
                        [((s, d), _F32, (tm, _pick(d, (512,))), _tile_ij)], _store_epilogue)
    (dw_out,) = _matmul("dw_out", [[(mixed, dr1_c, "tn")]], d, d, s, _pick(d, (512,)), d, tt, [],
                        [((d, d), _CDT, (_pick(d, (512,)), d), _row_i)], _store_epilogue)
    dproj, d_g_attn, d_g_conv, d_sinks, d_conv8 = _mixer_bwd(
        dm, proj, pos, invf, sinks, g_attn, g_conv, conv_w8, dmixed, attn, lse, y_conv)
    (dw_in_t,) = _matmul("dw_in", [[(dproj, x_c, "tn")]], inw, d, s, tn_in, d, tt, [],
                         [((inw, d), _CDT, (tn_in, d), _row_i)], _store_epilogue)

    def dx_epilogue(accs, ex, out, first):
        out[0][...] = DEEPNORM_ALPHA * ex[0][...] + accs[0]

    (grad_x,) = _matmul("dx", [[(dproj, w_in_t, "nn")]], s, d, inw, tr, d, tk_in,
                        [(dr1, (tr, d), _row_i)], [((s, d), _F32, (tr, d), _row_i)], dx_epilogue)

    landings = _scatter_blocks("scatter_grads", [dw_in_t, dw_out, dw_gate_t, dw_up_t, dw_down])
    g_in_t, g_out, g_gate_t, g_up_t, g_down = [
        _sum_slots(f"sum_{nm}", ld) for nm, ld in zip(("w_in", "w_out", "w_gate", "w_up", "w_down"), landings)]
    small_parts = [d_conv8[:3], d_sinks, d_g_attn, d_g_conv, d_ln1_g, d_ln1_b, d_ln2_g, d_ln2_b]
    packed, spans = _pack(small_parts)
    reduced = _unpack(_all_reduce_small("reduce_small", packed), spans, [p.shape for p in small_parts])
    g_conv_full, g_sinks, g_g_attn, g_g_conv, g_ln1_g, g_ln1_b, g_ln2_g, g_ln2_b = reduced
    me = _linear(*_position())
    g_conv_w = lax.dynamic_slice(g_conv_full, (0, me * conv_cols), (3, conv_cols))
    loss = lax.psum(loss_acc[0, 0], ("x", "y", "c"))

    big = {"w_in": (w_in[0], g_in_t.T, m_w_in[0], v_w_in[0]), "w_out": (w_out[0], g_out, m_w_out[0], v_w_out[0]),
           "w_gate": (w_gate[0], g_gate_t.T, m_w_gate[0], v_w_gate[0]), "w_up": (w_up[0], g_up_t.T, m_w_up[0], v_w_up[0]),
           "w_down": (w_down[0], g_down, m_w_down[0], v_w_down[0])}
    res = {nm: (g,) + tuple(_adamw(f"adamw_{nm}", w, g, m, v)) for nm, (w, g, m, v) in big.items()}
    small_names = ["conv_w", "sinks", "g_attn", "g_conv", "ln1_g", "ln1_b", "ln2_g", "ln2_b"]
    small_w = [conv_w, sinks, g_attn, g_conv, ln1_g, ln1_b, ln2_g, ln2_b]
    small_g = [g_conv_w[None], g_sinks, g_g_attn, g_g_conv, g_ln1_g, g_ln1_b, g_ln2_g, g_ln2_b]
    small_m = [m_conv_w, m_sinks, m_g_attn, m_g_conv, m_ln1_g, m_ln1_b, m_ln2_g, m_ln2_b]
    small_v = [v_conv_w, v_sinks, v_g_attn, v_g_conv, v_ln1_g, v_ln1_b, v_ln2_g, v_ln2_b]
    pw, sp = _pack(small_w)
    pg, _ = _pack(small_g)
    pm, _ = _pack(small_m)
    pv, _ = _pack(small_v)
    shapes = [w.shape for w in small_w]
    sd, sm, sv = [_unpack(p, sp, shapes) for p in _adamw("adamw_small", pw, pg, pm, pv)]
    for i, nm in enumerate(small_names):
        res[nm] = (small_g[i].reshape(shapes[i]), sd[i], sm[i], sv[i])

    order = ["w_in", "conv_w", "sinks", "g_attn", "g_conv", "w_out", "ln1_g", "ln1_b", "w_gate", "w_up", "w_down", "ln2_g", "ln2_b"]

    def lead(a, nm):
        return a[None] if nm in big else a

    return (loss, grad_x[None],
            *[lead(res[nm][0], nm) for nm in order], *[lead(res[nm][1], nm) for nm in order],
            *[lead(res[nm][2], nm) for nm in order], *[lead(res[nm][3], nm) for nm in order])
```

```python
import functools

import jax
import jax.numpy as jnp
from jax import lax
from jax.experimental import pallas as pl
from jax.experimental.pallas import tpu as pltpu

_F32 = jnp.float32
_CDT = jnp.bfloat16

HEAD_DIM = 64
WINDOW = 128
N_KV_HEADS = 4
KV_WIDTH = N_KV_HEADS * HEAD_DIM
ROT_DIM = HEAD_DIM // 4
ROPE_THETA = 500000.0
ATTN_SCALE = HEAD_DIM ** -0.5
DEPTH = 1
DEEPNORM_ALPHA = (2 * DEPTH) ** 0.25
LN_EPS = 1e-5
RMS_EPS = 1e-6
ADAM_LR = 0.001
ADAM_B1 = 0.9
ADAM_B2 = 0.999
ADAM_EPS = 1e-08
ADAM_WD = 0.01
ADAM_STEP = 10
N_DEV = 8
MASKED = -1e30

V7X_VMEM_BYTES = 64 * 1024 * 1024
V7X_LANES = 128
V7X_SUBLANES = 8
_MESH = pl.DeviceIdType.MESH
_ANY = pl.BlockSpec(memory_space=pl.ANY)


def _vmem_limit(block_bytes, scratch_bytes=0):
    want = 2 * block_bytes + scratch_bytes + 16 * 1024 * 1024
    return int(min(max(want, 32 * 1024 * 1024), V7X_VMEM_BYTES - 8 * 1024 * 1024))


def _nbytes(shape, dtype):
    n = 1
    for s in shape:
        n *= s
    return n * jnp.dtype(dtype).itemsize


def _pick(n, candidates):
    for c in candidates:
        if n % c == 0:
            return c
    raise ValueError(f"no tile of {candidates} divides {n}")


_DOT_DIMS = {"nn": ((1,), (0,)), "nt": ((1,), (1,)), "tn": ((0,), (0,))}


def _dot(a, b, mode):
    return lax.dot_general(a.astype(_CDT), b.astype(_CDT), (_DOT_DIMS[mode], ((), ())),
                           preferred_element_type=_F32)


def _accumulate(ref, val, first):
    @pl.when(first)
    def _():
        ref[...] = val

    @pl.when(jnp.logical_not(first))
    def _():
        ref[...] += val


class _Comm:
    def __init__(self, inputs, outputs, aliases, sems, start, finish):
        self.inputs, self.outputs, self.aliases, self.sems = inputs, outputs, aliases, sems
        self.start, self.finish = start, finish


class _CommArgs:
    def __init__(self, comms, n_in_before, n_out_before):
        self.comms, self.operands, self.out_shape, self.aliases, self.sems, self.at = comms, [], [], {}, [], []
        for cm in comms:
            self.at.append((len(self.operands), len(self.out_shape), len(self.sems)))
            for i_in, i_out in cm.aliases.items():
                self.aliases[n_in_before + len(self.operands) + i_in] = n_out_before + len(self.out_shape) + i_out
            self.operands += cm.inputs
            self.out_shape += cm.outputs
            self.sems += cm.sems

    def _each(self, in_refs, out_refs, sem_refs):
        for cm, (i0, o0, s0) in zip(self.comms, self.at):
            yield cm, (in_refs[i0:i0 + len(cm.inputs)], out_refs[o0:o0 + len(cm.outputs)], sem_refs[s0:s0 + len(cm.sems)])

    def start(self, in_refs, out_refs, sem_refs):
        for cm, refs in self._each(in_refs, out_refs, sem_refs):
            cm.start(*refs)

    def finish(self, in_refs, out_refs, sem_refs):
        for cm, refs in self._each(in_refs, out_refs, sem_refs):
            cm.finish(*refs)


def _matmul(name, groups, m, n, k, tm, tn, tk, extras, outs, epilogue, comm=()):
    assert m % tm == 0 and n % tn == 0 and k % tk == 0, (name, m, n, k, tm, tn, tk)
    nk = k // tk
    terms = [t for g in groups for t in g]
    operands, in_specs, block_bytes = [], [], 0
    for a, b, mode in terms:
        assert a.shape == ((k, m) if mode == "tn" else (m, k)), (name, a.shape, mode)
        assert b.shape == ((n, k) if mode == "nt" else (k, n)), (name, b.shape, mode)
        if mode == "tn":
            a_blk, a_map = (tk, tm), (lambda i, j, kk: (kk, i))
        else:
            a_blk, a_map = (tm, tk), (lambda i, j, kk: (i, kk))
        if mode == "nt":
            b_blk, b_map = (tn, tk), (lambda i, j, kk: (j, kk))
        else:
            b_blk, b_map = (tk, tn), (lambda i, j, kk: (kk, j))
        operands += [a, b]
        in_specs += [pl.BlockSpec(a_blk, a_map), pl.BlockSpec(b_blk, b_map)]
        block_bytes += _nbytes(a_blk, a.dtype) + _nbytes(b_blk, b.dtype)
    for arr, blk, imap in extras:
        operands.append(arr)
        in_specs.append(pl.BlockSpec(blk, lambda i, j, kk, imap=imap: imap(i, j)))
        block_bytes += _nbytes(blk, arr.dtype)
    out_shape, out_specs = [], []
    for shape, dtype, blk, imap in outs:
        out_shape.append(jax.ShapeDtypeStruct(shape, dtype))
        out_specs.append(pl.BlockSpec(blk, lambda i, j, kk, imap=imap: imap(i, j)))
        block_bytes += _nbytes(blk, dtype)
    n_terms, n_extra, n_out, n_groups = len(terms), len(extras), len(outs), len(groups)
    scratch = [pltpu.VMEM((tm, tn), _F32) for _ in range(n_groups)] if nk > 1 else []
    ca = _CommArgs(list(comm), len(operands), n_out)
    n_cin, n_cout, n_acc = len(ca.operands), len(ca.out_shape), len(scratch)
    grid = (m // tm, n // tn, nk)

    def body(*refs):
        refs = list(refs)
        term_refs = [refs.pop(0) for _ in range(2 * n_terms)]
        extra_refs = [refs.pop(0) for _ in range(n_extra)]
        cin_refs = [refs.pop(0) for _ in range(n_cin)]
        out_refs = [refs.pop(0) for _ in range(n_out)]
        cout_refs = [refs.pop(0) for _ in range(n_cout)]
        acc_refs = [refs.pop(0) for _ in range(n_acc)]
        sem_refs = refs
        i, j, kk = pl.program_id(0), pl.program_id(1), pl.program_id(2)
        first = jnp.logical_and(i == 0, j == 0)
        if comm:
            @pl.when(jnp.logical_and(first, kk == 0))
            def _():
                ca.start(cin_refs, cout_refs, sem_refs)
        partial, t = [], 0
        for g in groups:
            s = None
            for _, _, mode in g:
                d = _dot(term_refs[2 * t][...], term_refs[2 * t + 1][...], mode)
                s = d if s is None else s + d
                t += 1
            partial.append(s)
        if nk == 1:
            epilogue(partial, extra_refs, out_refs, first)
        else:
            for acc, p in zip(acc_refs, partial):
                _accumulate(acc, p, kk == 0)

            @pl.when(kk == nk - 1)
            def _():
                epilogue([acc[...] for acc in acc_refs], extra_refs, out_refs, first)
        if comm:
            @pl.when(jnp.logical_and(jnp.logical_and(i == grid[0] - 1, j == grid[1] - 1), kk == nk - 1))
            def _():
                ca.finish(cin_refs, cout_refs, sem_refs)

    res = pl.pallas_call(
        body, name=name, grid=grid,
        in_specs=in_specs + [_ANY] * n_cin, out_specs=out_specs + [_ANY] * n_cout,
        out_shape=out_shape + ca.out_shape, scratch_shapes=scratch + ca.sems, input_output_aliases=ca.aliases,
        compiler_params=pltpu.CompilerParams(
            dimension_semantics=("arbitrary", "arbitrary", "arbitrary"),
            vmem_limit_bytes=_vmem_limit(block_bytes, n_groups * tm * tn * 4 if nk > 1 else 0)),
    )(*operands, *ca.operands)
    return list(res[:n_out]) + list(res[n_out:])


def _store_epilogue(accs, extra_refs, out_refs, first):
    for acc, ref in zip(accs, out_refs):
        ref[...] = acc.astype(ref.dtype)


def _tile_ij(i, j):
    return (i, j)


def _row_i(i, j):
    return (i, 0)


def _whole(i, j):
    return (0, 0)


def _mean(v):
    return jnp.mean(v, axis=-1, keepdims=True)


def _ln_fwd(r, g, b):
    xc = r - _mean(r)
    rstd = lax.rsqrt(_mean(xc * xc) + LN_EPS)
    xhat = xc * rstd
    return xhat * g + b, xhat, rstd


def _ln_bwd(dy, xhat, rstd, g):
    dxh = dy * g
    dr = rstd * (dxh - _mean(dxh) - xhat * _mean(dxh * xhat))
    return dr, jnp.sum(dy * xhat, axis=0, keepdims=True), jnp.sum(dy, axis=0, keepdims=True)


def _rms_fwd(a, g):
    rstd = lax.rsqrt(_mean(a * a) + RMS_EPS)
    return a * rstd * g


def _rms_bwd(dm, a, g):
    rstd = lax.rsqrt(_mean(a * a) + RMS_EPS)
    nhat = a * rstd
    dn = dm * g
    da = rstd * (dn - nhat * _mean(dn * nhat))
    return da, jnp.sum(dm * nhat, axis=0, keepdims=True)


def _lane(shape):
    return lax.broadcasted_iota(jnp.int32, shape, 1)


def _row(shape):
    return lax.broadcasted_iota(jnp.int32, shape, 0)


def _rope_tables(pos, invf):
    ang = pos.astype(_F32) * invf
    lane = _lane(ang.shape)
    in_rot = (lane % HEAD_DIM) < ROT_DIM
    first = (lane % ROT_DIM) < ROT_DIM // 2
    cos = jnp.where(in_rot, jnp.cos(ang), 1.0)
    sin = jnp.sin(ang)
    sgn = jnp.where(in_rot, jnp.where(first, -sin, sin), 0.0)
    return cos, sgn


def _rope(t, cos, sgn, sign):
    half = ROT_DIM // 2
    first = (_lane(t.shape) % ROT_DIM) < half
    partner = jnp.where(first, pltpu.roll(t, V7X_LANES - half, 1), pltpu.roll(t, half, 1))
    return t * cos + partner * (sgn * sign)


def _dup_head(t, h):
    g = t[:, 128 * (h // 2):128 * (h // 2) + 128]
    r = pltpu.roll(g, HEAD_DIM, 1)
    lo = _lane(g.shape) < HEAD_DIM
    return jnp.where(lo, g, r) if h % 2 == 0 else jnp.where(lo, r, g)


def _fold_halves(t):
    return t + pltpu.roll(t, HEAD_DIM, 1)


def _halves(t):
    lo = _lane(t.shape) < HEAD_DIM
    zero = jnp.zeros_like(t)
    return jnp.where(lo, t, zero), jnp.where(lo, zero, t)


def _band_mask(n_keys, first_block):
    i = _row((WINDOW, n_keys))
    j = _lane((WINDOW, n_keys))
    valid = jnp.logical_and(j >= i + 1, j <= i + WINDOW)
    if first_block is not None:
        valid = jnp.logical_and(valid, jnp.logical_or(j >= WINDOW, jnp.logical_not(first_block)))
    return valid


def _shift_down(z, halo, k):
    rows = z.shape[0]
    out = pltpu.roll(z, k, 0)
    r = _row(z.shape)
    for t in range(k):
        out = jnp.where(r == t, halo[V7X_SUBLANES - k + t:V7X_SUBLANES - k + t + 1, :], out)
    del rows
    return out


def _shift_up(z, halo, k):
    rows = z.shape[0]
    out = pltpu.roll(z, rows - k, 0)
    r = _row(z.shape)
    for t in range(k):
        out = jnp.where(r == rows - k + t, halo[t:t + 1, :], out)
    return out


class _Dims:
    def __init__(self, s, d, d_ff):
        self.s, self.d, self.d_ff = s, d, d_ff
        self.aw = d // 2
        self.cw = d - self.aw
        self.nq = self.aw // HEAD_DIM
        self.group = self.nq // N_KV_HEADS
        assert self.group % 2 == 0, "a 128-lane pair of query heads must share its kv head"
        self.inw = self.aw + 2 * KV_WIDTH + 3 * self.cw
        self.o_k = self.aw
        self.o_v = self.aw + KV_WIDTH
        self.o_cg = self.aw + 2 * KV_WIDTH
        self.o_bg = self.o_cg + self.cw
        self.o_u = self.o_bg + self.cw
        self.nb = s // WINDOW
        assert s % WINDOW == 0


def _carrying(body, n_in, n_out, n_steps, ca):
    n_cin, n_cout = len(ca.operands), len(ca.out_shape)

    def wrapped(*refs):
        refs = list(refs)
        in_refs = [refs.pop(0) for _ in range(n_in)]
        cin_refs = [refs.pop(0) for _ in range(n_cin)]
        out_refs = [refs.pop(0) for _ in range(n_out)]
        cout_refs = [refs.pop(0) for _ in range(n_cout)]
        if ca.comms:
            @pl.when(pl.program_id(0) == 0)
            def _():
                ca.start(cin_refs, cout_refs, refs)
        body(*in_refs, *out_refs)
        if ca.comms:
            @pl.when(pl.program_id(0) == n_steps - 1)
            def _():
                ca.finish(cin_refs, cout_refs, refs)

    return wrapped


def _mixer_fwd(dm, proj, pos, invf, sinks, g_attn, g_conv, conv_w8, comm=()):
    s, d, aw, cw, nq, inw, nb = dm.s, dm.d, dm.aw, dm.cw, dm.nq, dm.inw, dm.nb

    def body(pp_ref, pc_ref, posp_ref, posc_ref, invf_ref, sinks_ref, ga_ref, gc_ref, cw_ref,
             mixed_ref, attn_ref, lse_ref, y_ref):
        n = pl.program_id(0)
        cos_c, sgn_c = _rope_tables(posc_ref[...], invf_ref[...])
        cos_p, sgn_p = _rope_tables(posp_ref[...], invf_ref[...])
        kk = jnp.concatenate(
            [jnp.concatenate([_rope(ref[:, dm.o_k + 128 * g:dm.o_k + 128 * g + 128], c, sg, 1.0)
                              for g in range(KV_WIDTH // 128)], axis=1)
             for ref, c, sg in ((pp_ref, cos_p, sgn_p), (pc_ref, cos_c, sgn_c))], axis=0)
        vv = jnp.concatenate([pp_ref[:, dm.o_v:dm.o_v + KV_WIDTH], pc_ref[:, dm.o_v:dm.o_v + KV_WIDTH]], axis=0)
        k2 = [_dup_head(kk, h).astype(_CDT) for h in range(N_KV_HEADS)]
        v2 = [_halves(_dup_head(vv, h).astype(_CDT)) for h in range(N_KV_HEADS)]
        valid = _band_mask(2 * WINDOW, n == 0)
        for j in range(nq // 2):
            h = (2 * j) // dm.group
            qp = _rope(pc_ref[:, 128 * j:128 * j + 128], cos_c, sgn_c, 1.0).astype(_CDT)
            out = None
            for half, qh in enumerate(_halves(qp)):
                hq = 2 * j + half
                sc = jnp.where(valid, _dot(qh, k2[h], "nt") * ATTN_SCALE, MASKED)
                sink = sinks_ref[0, hq]
                mx = jnp.maximum(jnp.max(sc, axis=1, keepdims=True), sink)
                p = jnp.exp(sc - mx)
                den = jnp.sum(p, axis=1, keepdims=True) + jnp.exp(sink - mx)
                o = _dot(p / den, v2[h][half], "nn")
                out = o if out is None else out + o
                lse_ref[:, hq:hq + 1] = mx + jnp.log(den)
            attn_ref[:, 128 * j:128 * j + 128] = out
        mixed_ref[:, 0:aw] = _rms_fwd(attn_ref[...], ga_ref[...]).astype(mixed_ref.dtype)

        z = pc_ref[:, dm.o_cg:dm.o_cg + cw] * pc_ref[:, dm.o_u:dm.o_u + cw]
        top = WINDOW - V7X_SUBLANES
        halo = pp_ref[top:WINDOW, dm.o_cg:dm.o_cg + cw] * pp_ref[top:WINDOW, dm.o_u:dm.o_u + cw]
        halo = jnp.where(n == 0, jnp.zeros_like(halo), halo)
        y = cw_ref[0:1, :] * _shift_down(z, halo, 2) + cw_ref[1:2, :] * _shift_down(z, halo, 1) + cw_ref[2:3, :] * z
        y_ref[...] = y
        conv = pc_ref[:, dm.o_bg:dm.o_bg + cw] * y
        mixed_ref[:, aw:d] = _rms_fwd(conv, gc_ref[...]).astype(mixed_ref.dtype)

    prev = lambda n: (jnp.maximum(n - 1, 0), 0)
    cur = lambda n: (n, 0)
    fixed = lambda n: (0, 0)
    blocks = 2 * WINDOW * inw * 4 + WINDOW * (d * 2 + aw * 4 + cw * 4 + nq * 4)
    ca = _CommArgs(list(comm), 9, 4)
    return pl.pallas_call(
        _carrying(body, 9, 4, nb, ca), name="mixer_fwd", grid=(nb,),
        in_specs=[pl.BlockSpec((WINDOW, inw), prev), pl.BlockSpec((WINDOW, inw), cur),
                  pl.BlockSpec((WINDOW, 1), prev), pl.BlockSpec((WINDOW, 1), cur),
                  pl.BlockSpec((1, V7X_LANES), fixed), pl.BlockSpec(memory_space=pltpu.SMEM),
                  pl.BlockSpec((1, aw), fixed), pl.BlockSpec((1, cw), fixed), pl.BlockSpec((V7X_SUBLANES, cw), fixed)]
        + [_ANY] * len(ca.operands),
        out_specs=[pl.BlockSpec((WINDOW, d), cur), pl.BlockSpec((WINDOW, aw), cur),
                   pl.BlockSpec((WINDOW, nq), cur), pl.BlockSpec((WINDOW, cw), cur)] + [_ANY] * len(ca.out_shape),
        out_shape=[jax.ShapeDtypeStruct((s, d), _CDT), jax.ShapeDtypeStruct((s, aw), _F32),
                   jax.ShapeDtypeStruct((s, nq), _F32), jax.ShapeDtypeStruct((s, cw), _F32)] + ca.out_shape,
        scratch_shapes=ca.sems, input_output_aliases=ca.aliases,
        compiler_params=pltpu.CompilerParams(dimension_semantics=("arbitrary",), vmem_limit_bytes=_vmem_limit(blocks)),
    )(proj, proj, pos, pos, invf, sinks, g_attn, g_conv, conv_w8, *ca.operands)


def _mixer_bwd(dm, proj, pos, invf, sinks, g_attn, g_conv, conv_w8, dmixed, attn, lse, y, comm=()):
    s, d, aw, cw, nq, inw, nb = dm.s, dm.d, dm.aw, dm.cw, dm.nq, dm.inw, dm.nb

    def body(pp_ref, pc_ref, pn_ref, posp_ref, posc_ref, posn_ref, dmc_ref, dmn_ref, ac_ref, an_ref,
             lsec_ref, lsen_ref, yc_ref, yn_ref, invf_ref, sinks_ref, ga_ref, gc_ref, cw_ref,
             dproj_ref, dga_ref, dgc_ref, dsinks_ref, dcw_ref):
        n = pl.program_id(0)
        first = n == 0
        has_next = n < nb - 1
        cos_p, sgn_p = _rope_tables(posp_ref[...], invf_ref[...])
        cos_c, sgn_c = _rope_tables(posc_ref[...], invf_ref[...])
        cos_n, sgn_n = _rope_tables(posn_ref[...], invf_ref[...])

        da_c, dga = _rms_bwd(dmc_ref[:, 0:aw], ac_ref[...], ga_ref[...])
        da_n, _ = _rms_bwd(dmn_ref[:, 0:aw], an_ref[...], ga_ref[...])
        _accumulate(dga_ref, dga, first)
        kk = jnp.concatenate(
            [jnp.concatenate([_rope(ref[:, dm.o_k + 128 * g:dm.o_k + 128 * g + 128], c, sg, 1.0)
                              for g in range(KV_WIDTH // 128)], axis=1)
             for ref, c, sg in ((pp_ref, cos_p, sgn_p), (pc_ref, cos_c, sgn_c))], axis=0)
        vv = jnp.concatenate([pp_ref[:, dm.o_v:dm.o_v + KV_WIDTH], pc_ref[:, dm.o_v:dm.o_v + KV_WIDTH]], axis=0)
        k2 = [_dup_head(kk, h).astype(_CDT) for h in range(N_KV_HEADS)]
        v2 = [_dup_head(vv, h).astype(_CDT) for h in range(N_KV_HEADS)]
        valid_c = _band_mask(2 * WINDOW, first)
        valid_n = jnp.logical_and(_band_mask(WINDOW, None), has_next)
        dk2 = [None] * N_KV_HEADS
        dv2 = [None] * N_KV_HEADS
        dsinks = jnp.zeros((1, nq), _F32)
        head_lane = _lane((1, nq))

        def tile(q_half, do_half, o_pair, da_pair, lse_col, keys, vals, valid, half):
            sc = _dot(q_half, keys, "nt") * ATTN_SCALE
            p = jnp.exp(jnp.where(valid, sc - lse_col, MASKED))
            lo = _lane(o_pair.shape) < HEAD_DIM
            mine = lo if half == 0 else jnp.logical_not(lo)
            delta = jnp.sum(jnp.where(mine, o_pair * da_pair, 0.0), axis=1, keepdims=True)
            dp = _dot(do_half, vals, "nt")
            return p, p * (dp - delta) * ATTN_SCALE, delta

        for j in range(nq // 2):
            h = (2 * j) // dm.group
            cols = slice(128 * j, 128 * j + 128)
            q_c = _halves(_rope(pc_ref[:, cols], cos_c, sgn_c, 1.0).astype(_CDT))
            q_n = _halves(_rope(pn_ref[:, cols], cos_n, sgn_n, 1.0).astype(_CDT))
            do_c = _halves(da_c[:, cols].astype(_CDT))
            do_n = _halves(da_n[:, cols].astype(_CDT))
            k_halves = _halves(k2[h])
            dq = None
            for half in range(2):
                hq = 2 * j + half
                p, ds, delta = tile(q_c[half], do_c[half], ac_ref[:, cols], da_c[:, cols], lsec_ref[:, hq:hq + 1],
                                    k2[h], v2[h], valid_c, half)
                ds = ds.astype(_CDT)
                t = _dot(ds, k_halves[half], "nn")
                dq = t if dq is None else dq + t
                dk = _dot(ds, q_c[half], "tn")[WINDOW:2 * WINDOW, :]
                dv = _dot(p.astype(_CDT), do_c[half], "tn")[WINDOW:2 * WINDOW, :]
                psink = jnp.exp(sinks_ref[0, hq] - lsec_ref[:, hq:hq + 1])
                dsinks = dsinks + jnp.where(head_lane == hq, -jnp.sum(psink * delta), 0.0)
                p, ds, _ = tile(q_n[half], do_n[half], an_ref[:, cols], da_n[:, cols], lsen_ref[:, hq:hq + 1],
                                k2[h][WINDOW:2 * WINDOW, :], v2[h][WINDOW:2 * WINDOW, :], valid_n, half)
                dk = dk + _dot(ds.astype(_CDT), q_n[half], "tn")
                dv = dv + _dot(p.astype(_CDT), do_n[half], "tn")
                dk2[h] = dk if dk2[h] is None else dk2[h] + dk
                dv2[h] = dv if dv2[h] is None else dv2[h] + dv
            dproj_ref[:, cols] = _rope(dq, cos_c, sgn_c, -1.0).astype(dproj_ref.dtype)
        _accumulate(dsinks_ref, dsinks, first)
        lo = _lane((WINDOW, 128)) < HEAD_DIM
        for g in range(KV_WIDTH // 128):
            dk = jnp.where(lo, _fold_halves(dk2[2 * g]), _fold_halves(dk2[2 * g + 1]))
            dv = jnp.where(lo, _fold_halves(dv2[2 * g]), _fold_halves(dv2[2 * g + 1]))
            dproj_ref[:, dm.o_k + 128 * g:dm.o_k + 128 * g + 128] = _rope(dk, cos_c, sgn_c, -1.0).astype(dproj_ref.dtype)
            dproj_ref[:, dm.o_v + 128 * g:dm.o_v + 128 * g + 128] = dv.astype(dproj_ref.dtype)

        bg = pc_ref[:, dm.o_bg:dm.o_bg + cw]
        yc = yc_ref[...]
        dconv, dgc = _rms_bwd(dmc_ref[:, aw:d], bg * yc, gc_ref[...])
        _accumulate(dgc_ref, dgc, first)
        dproj_ref[:, dm.o_bg:dm.o_bg + cw] = (dconv * yc).astype(dproj_ref.dtype)
        dy = dconv * bg
        bg_n = pn_ref[0:V7X_SUBLANES, dm.o_bg:dm.o_bg + cw]
        dconv_n, _ = _rms_bwd(dmn_ref[0:V7X_SUBLANES, aw:d], bg_n * yn_ref[...], gc_ref[...])
        halo = jnp.where(has_next, dconv_n * bg_n, 0.0)
        dy1 = _shift_up(dy, halo, 1)
        dy2 = _shift_up(dy, halo, 2)
        dz = cw_ref[2:3, :] * dy + cw_ref[1:2, :] * dy1 + cw_ref[0:1, :] * dy2
        cg = pc_ref[:, dm.o_cg:dm.o_cg + cw]
        u = pc_ref[:, dm.o_u:dm.o_u + cw]
        dproj_ref[:, dm.o_cg:dm.o_cg + cw] = (dz * u).astype(dproj_ref.dtype)
        dproj_ref[:, dm.o_u:dm.o_u + cw] = (dz * cg).astype(dproj_ref.dtype)
        z = cg * u
        dcw = jnp.concatenate(
            [jnp.sum(z * t, axis=0, keepdims=True) for t in (dy2, dy1, dy)]
            + [jnp.zeros((V7X_SUBLANES - 3, cw), _F32)], axis=0)
        _accumulate(dcw_ref, dcw, first)

    prev = lambda n: (jnp.maximum(n - 1, 0), 0)
    cur = lambda n: (n, 0)
    nxt = lambda n: (jnp.minimum(n + 1, nb - 1), 0)
    nxt8 = lambda n: (jnp.minimum((n + 1) * (WINDOW // V7X_SUBLANES), s // V7X_SUBLANES - 1), 0)
    fixed = lambda n: (0, 0)
    blocks = WINDOW * (3 * inw * 4 + 2 * d * 4 + 2 * aw * 4 + cw * 4 + inw * 2)
    ca = _CommArgs(list(comm), 19, 5)
    return pl.pallas_call(
        _carrying(body, 19, 5, nb, ca), name="mixer_bwd", grid=(nb,),
        in_specs=[pl.BlockSpec((WINDOW, inw), prev), pl.BlockSpec((WINDOW, inw), cur), pl.BlockSpec((WINDOW, inw), nxt),
                  pl.BlockSpec((WINDOW, 1), prev), pl.BlockSpec((WINDOW, 1), cur), pl.BlockSpec((WINDOW, 1), nxt),
                  pl.BlockSpec((WINDOW, d), cur), pl.BlockSpec((WINDOW, d), nxt),
                  pl.BlockSpec((WINDOW, aw), cur), pl.BlockSpec((WINDOW, aw), nxt),
                  pl.BlockSpec((WINDOW, nq), cur), pl.BlockSpec((WINDOW, nq), nxt),
                  pl.BlockSpec((WINDOW, cw), cur), pl.BlockSpec((V7X_SUBLANES, cw), nxt8),
                  pl.BlockSpec((1, V7X_LANES), fixed), pl.BlockSpec(memory_space=pltpu.SMEM),
                  pl.BlockSpec((1, aw), fixed), pl.BlockSpec((1, cw), fixed), pl.BlockSpec((V7X_SUBLANES, cw), fixed)]
        + [_ANY] * len(ca.operands),
        out_specs=[pl.BlockSpec((WINDOW, inw), cur), pl.BlockSpec((1, aw), fixed), pl.BlockSpec((1, cw), fixed),
                   pl.BlockSpec((1, nq), fixed), pl.BlockSpec((V7X_SUBLANES, cw), fixed)] + [_ANY] * len(ca.out_shape),
        out_shape=[jax.ShapeDtypeStruct((s, inw), _CDT), jax.ShapeDtypeStruct((1, aw), _F32),
                   jax.ShapeDtypeStruct((1, cw), _F32), jax.ShapeDtypeStruct((1, nq), _F32),
                   jax.ShapeDtypeStruct((V7X_SUBLANES, cw), _F32)] + ca.out_shape,
        scratch_shapes=ca.sems, input_output_aliases=ca.aliases,
        compiler_params=pltpu.CompilerParams(dimension_semantics=("arbitrary",), vmem_limit_bytes=_vmem_limit(blocks)),
    )(proj, proj, proj, pos, pos, pos, dmixed, dmixed, attn, attn, lse, lse, y, y, invf, sinks, g_attn, g_conv, conv_w8,
      *ca.operands)


def _position():
    return lax.axis_index("x"), lax.axis_index("y"), lax.axis_index("c")


def _linear(px, py, pc):
    return 4 * px + 2 * py + pc


def _comm_kernel(name, comm):
    ca = _CommArgs(list(comm), 0, 0)
    n_cin, n_cout = len(ca.operands), len(ca.out_shape)

    def body(*refs):
        cin, cout, sems = refs[:n_cin], refs[n_cin:n_cin + n_cout], refs[n_cin + n_cout:]
        ca.start(cin, cout, sems)
        ca.finish(cin, cout, sems)

    return pl.pallas_call(
        body, name=name, out_shape=ca.out_shape, in_specs=[_ANY] * n_cin, out_specs=[_ANY] * n_cout,
        scratch_shapes=ca.sems, input_output_aliases=ca.aliases,
    )(*ca.operands)


def _gather_op(units):
    n = len(units)
    inputs, outputs, aliases = [], [], {}
    for shard, _, _, _ in units:
        inputs.append(shard)
        outputs.append(jax.ShapeDtypeStruct((N_DEV * shard.shape[0], shard.shape[1]), shard.dtype))
    for u, (_, buf, _, _) in enumerate(units):
        if buf is not None:
            aliases[len(inputs)] = u
            inputs.append(buf)

    def plan(ins, outs, sems):
        send_sems, recv_sems, local_sems = sems
        x, y, c = _position()
        me, sibling = (x, y, c), (x, y, 1 - c)
        chips = [(1 - x, y), (x, 1 - y), (1 - x, 1 - y)]

        def rows(u, px, py, pc):
            shard, _, r0, r1 = units[u]
            return outs[u].at[pl.ds(pl.multiple_of(_linear(px, py, pc) * shard.shape[0] + r0, 16), r1 - r0), :]

        def own(u):
            _, _, r0, r1 = units[u]
            return ins[u].at[pl.ds(r0, r1 - r0), :]

        def copy(u, k, block, to, src=None):
            return pltpu.make_async_remote_copy(
                src_ref=rows(u, *block) if src is None else src, dst_ref=rows(u, *block),
                send_sem=send_sems.at[u, k], recv_sem=recv_sems.at[u, k], device_id=to, device_id_type=_MESH)

        mine = [pltpu.make_async_copy(own(u), rows(u, *me), local_sems.at[u]) for u in range(n)]
        first = []
        for u in range(n):
            first.append(copy(u, 0, me, sibling, src=own(u)))
            first += [copy(u, 1 + j, me, (*chip, c), src=own(u)) for j, chip in enumerate(chips)]
        passed = [[copy(u, 4 + j, (*chip, c), sibling) for j, chip in enumerate(chips)] for u in range(n)]
        landed = [[copy(u, 1 + j, (*chip, c), me) for j, chip in enumerate(chips)] for u in range(n)]
        rest = [[copy(u, 0, sibling, me)] + [copy(u, 4 + j, (*chip, 1 - c), me) for j, chip in enumerate(chips)]
                for u in range(n)]
        return mine, first, passed, landed, rest

    def start(ins, outs, sems):
        mine, first, _, _, _ = plan(ins, outs, sems)
        for cp in mine + first:
            cp.start()

    def finish(ins, outs, sems):
        mine, first, passed, landed, rest = plan(ins, outs, sems)
        for u in range(n):
            for arrived, onward in zip(landed[u], passed[u]):
                arrived.wait_recv()
                onward.start()
        for u in range(n):
            for cp in rest[u]:
                cp.wait_recv()
        for cp in first + [cp for row in passed for cp in row]:
            cp.wait_send()
        for cp in mine:
            cp.wait()

    sems = [pltpu.SemaphoreType.DMA((n, 7)), pltpu.SemaphoreType.DMA((n, 7)), pltpu.SemaphoreType.DMA((n,))]
    return _Comm(inputs, outputs, aliases, sems, start, finish)


def _peers(x, y, c):
    out = []
    for k in range(1, N_DEV):
        fx, fy, fc = (k >> 2) & 1, (k >> 1) & 1, k & 1
        out.append((1 - x if fx else x, 1 - y if fy else y, 1 - c if fc else c))
    return out


def _exchange_op(partials):
    n = len(partials)
    outputs = [jax.ShapeDtypeStruct((4, p.shape[0] // N_DEV, p.shape[1]), p.dtype) for p in partials]

    def plan(ins, outs, sems):
        send_sems, recv_sems = sems
        x, y, c = _position()
        out = []
        for a in range(n):
            r = outs[a].shape[1]
            for ch in range(4):
                out.append(pltpu.make_async_remote_copy(
                    src_ref=ins[a].at[pl.ds(pl.multiple_of((2 * ch + 1 - c) * r, 16), r), :], dst_ref=outs[a].at[ch],
                    send_sem=send_sems.at[a, ch], recv_sem=recv_sems.at[a, ch], device_id=(x, y, 1 - c),
                    device_id_type=_MESH))
        return out

    def start(ins, outs, sems):
        for cp in plan(ins, outs, sems):
            cp.start()

    def finish(ins, outs, sems):
        copies = plan(ins, outs, sems)
        for cp in copies:
            cp.wait_recv()
        for cp in copies:
            cp.wait_send()

    sems = [pltpu.SemaphoreType.DMA((n, 4)), pltpu.SemaphoreType.DMA((n, 4))]
    return _Comm(list(partials), outputs, {}, sems, start, finish)


def _chip_send_op(units):
    n = len(units)
    inputs, outputs, aliases = [], [], {}
    for q, _, _, _ in units:
        inputs.append(q)
        outputs.append(jax.ShapeDtypeStruct(q.shape, q.dtype))
    for u, (_, buf, _, _) in enumerate(units):
        if buf is not None:
            aliases[len(inputs)] = u
            inputs.append(buf)

    def plan(ins, outs, sems):
        send_sems, recv_sems, local_sems = sems
        x, y, c = _position()
        my_chip = 2 * x + y
        chips = [(1 - x, y), (x, 1 - y), (1 - x, 1 - y)]
        mine, sends, arrivals = [], [], []
        for u, (_, _, r0, r1) in enumerate(units):
            span = pl.ds(r0, r1 - r0)
            mine.append(pltpu.make_async_copy(ins[u].at[my_chip, span, :], outs[u].at[my_chip, span, :], local_sems.at[u]))
            for k, (px, py) in enumerate(chips):
                sends.append(pltpu.make_async_remote_copy(
                    src_ref=ins[u].at[2 * px + py, span, :], dst_ref=outs[u].at[my_chip, span, :],
                    send_sem=send_sems.at[u, k], recv_sem=recv_sems.at[u, k], device_id=(px, py, c), device_id_type=_MESH))
                arrivals.append(pltpu.make_async_remote_copy(
                    src_ref=ins[u].at[my_chip, span, :], dst_ref=outs[u].at[2 * px + py, span, :],
                    send_sem=send_sems.at[u, k], recv_sem=recv_sems.at[u, k], device_id=(px, py, c), device_id_type=_MESH))
        return mine, sends, arrivals

    def start(ins, outs, sems):
        mine, sends, _ = plan(ins, outs, sems)
        for cp in mine + sends:
            cp.start()

    def finish(ins, outs, sems):
        mine, sends, arrivals = plan(ins, outs, sems)
        for cp in arrivals:
            cp.wait_recv()
        for cp in sends:
            cp.wait_send()
        for cp in mine:
            cp.wait()

    sems = [pltpu.SemaphoreType.DMA((n, 3)), pltpu.SemaphoreType.DMA((n, 3)), pltpu.SemaphoreType.DMA((n,))]
    return _Comm(inputs, outputs, aliases, sems, start, finish)


def _pair_sum(name, partial, received):
    _, rows, cols = received.shape
    tr = _pick(rows, (64, 32, 16))
    p4 = partial.reshape(4, 2, rows, cols)
    kind = jnp.reshape(lax.axis_index("c"), (1,)).astype(jnp.int32)

    def body(kind_ref, p_ref, r_ref, o_ref):
        o_ref[0] = (p_ref[0, 0].astype(_F32) + r_ref[0].astype(_F32)).astype(o_ref.dtype)

    return pl.pallas_call(
        body, name=name,
        grid_spec=pltpu.PrefetchScalarGridSpec(
            num_scalar_prefetch=1, grid=(4, rows // tr),
            in_specs=[pl.BlockSpec((1, 1, tr, cols), lambda ch, i, kind_ref: (ch, kind_ref[0], i, 0)),
                      pl.BlockSpec((1, tr, cols), lambda ch, i, kind_ref: (ch, i, 0))],
            out_specs=pl.BlockSpec((1, tr, cols), lambda ch, i, kind_ref: (ch, i, 0))),
        out_shape=jax.ShapeDtypeStruct(received.shape, received.dtype),
        compiler_params=pltpu.CompilerParams(dimension_semantics=("arbitrary", "arbitrary")),
    )(kind, p4, received)


def _all_reduce_small(name, v):
    rows = v.shape[0]

    def body(v_ref, out_ref, land_ref, send_sems, recv_sems):
        x, y, c = _position()
        me = _linear(x, y, c)
        peers = _peers(x, y, c)
        land_ref[me] = v_ref[...]
        sends = [pltpu.make_async_remote_copy(
            src_ref=v_ref, dst_ref=land_ref.at[me], send_sem=send_sems.at[k], recv_sem=recv_sems.at[k],
            device_id=peer, device_id_type=_MESH) for k, peer in enumerate(peers)]
        for cp in sends:
            cp.start()
        for k, peer in enumerate(peers):
            pltpu.make_async_remote_copy(
                src_ref=v_ref, dst_ref=land_ref.at[_linear(*peer)], send_sem=send_sems.at[k], recv_sem=recv_sems.at[k],
                device_id=peer, device_id_type=_MESH).wait_recv()
        for cp in sends:
            cp.wait_send()
        total = land_ref[0]
        for s in range(1, N_DEV):
            total = total + land_ref[s]
        out_ref[...] = total

    return pl.pallas_call(
        body, name=name, out_shape=jax.ShapeDtypeStruct(v.shape, _F32),
        in_specs=[pl.BlockSpec(memory_space=pltpu.VMEM)], out_specs=pl.BlockSpec(memory_space=pltpu.VMEM),
        scratch_shapes=[pltpu.VMEM((N_DEV, rows, V7X_LANES), _F32), pltpu.SemaphoreType.DMA((7,)), pltpu.SemaphoreType.DMA((7,))],
    )(v)


def _sum_slots(name, landing):
    n_slots, rows, cols = landing.shape
    tr = _pick(rows, (64, 32, 16))

    def body(l_ref, o_ref):
        total = l_ref[0].astype(_F32)
        for s in range(1, n_slots):
            total = total + l_ref[s].astype(_F32)
        o_ref[...] = total

    return pl.pallas_call(
        body, name=name, grid=(rows // tr,),
        in_specs=[pl.BlockSpec((n_slots, tr, cols), lambda i: (0, i, 0))],
        out_specs=pl.BlockSpec((tr, cols), lambda i: (i, 0)),
        out_shape=jax.ShapeDtypeStruct((rows, cols), _F32),
        compiler_params=pltpu.CompilerParams(dimension_semantics=("arbitrary",)),
    )(landing)


def _adamw(name, w, g, m, v):
    rows, cols = w.shape
    tr = _pick(rows, (256, 128, 64, 32, 16, 8))

    def body(w_ref, g_ref, m_ref, v_ref, d_ref, nm_ref, nv_ref):
        g = g_ref[...]
        nm = ADAM_B1 * m_ref[...] + (1.0 - ADAM_B1) * g
        nv = ADAM_B2 * v_ref[...] + (1.0 - ADAM_B2) * (g * g)
        m_hat = nm / (1.0 - ADAM_B1 ** ADAM_STEP)
        v_hat = nv / (1.0 - ADAM_B2 ** ADAM_STEP)
        d_ref[...] = -ADAM_LR * (m_hat / (jnp.sqrt(v_hat) + ADAM_EPS) + ADAM_WD * w_ref[...])
        nm_ref[...] = nm
        nv_ref[...] = nv

    spec = pl.BlockSpec((tr, cols), lambda i: (i, 0))
    return pl.pallas_call(
        body, name=name, grid=(rows // tr,), in_specs=[spec] * 4, out_specs=[spec] * 3,
        out_shape=[jax.ShapeDtypeStruct((rows, cols), _F32)] * 3,
        compiler_params=pltpu.CompilerParams(dimension_semantics=("arbitrary",)),
    )(w, g, m, v)


def _pad_rows(a, rows):
    return jnp.pad(a, ((0, rows - a.shape[0]), (0, 0)))


def _pack(parts):
    rows, spans, at = [], [], 0
    for p in parts:
        p = p.reshape(-1)
        r = -(-p.shape[0] // V7X_LANES)
        rows.append(jnp.pad(p, (0, r * V7X_LANES - p.shape[0])).reshape(r, V7X_LANES))
        spans.append((at, r, p.shape[0]))
        at += r
    packed = jnp.concatenate(rows, axis=0)
    return _pad_rows(packed, -(-at // V7X_SUBLANES) * V7X_SUBLANES), spans


def _unpack(packed, spans, shapes):
    return [packed[at:at + r].reshape(-1)[:size].reshape(shape) for (at, r, size), shape in zip(spans, shapes)]


def kernel(x, positions, w_in, conv_w, sinks, g_attn, g_conv, w_out, ln1_g, ln1_b, w_gate, w_up, w_down, ln2_g, ln2_b, loss_target, m_w_in, m_conv_w, m_sinks, m_g_attn, m_g_conv, m_w_out, m_ln1_g, m_ln1_b, m_w_gate, m_w_up, m_w_down, m_ln2_g, m_ln2_b, v_w_in, v_conv_w, v_sinks, v_g_attn, v_g_conv, v_w_out, v_ln1_g, v_ln1_b, v_w_gate, v_w_up, v_w_down, v_ln2_g, v_ln2_b):
    _, s, d = x.shape
    d_ff = N_DEV * w_gate.shape[2]
    dm = _Dims(s, d, d_ff)
    aw, cw, nq, inw = dm.aw, dm.cw, dm.nq, dm.inw
    x2 = x[0]
    x_c = x2.astype(_CDT)
    pos = positions[0].reshape(s, 1)
    inv_freq = ROPE_THETA ** (-jnp.arange(0, ROT_DIM, 2, dtype=_F32) / ROT_DIM)
    invf = jnp.tile(inv_freq, V7X_LANES // (ROT_DIM // 2)).reshape(1, V7X_LANES)

    conv_cols = conv_w.shape[2]
    sh_in, sh_out = w_in[0].T.astype(_CDT), w_out[0].astype(_CDT)
    sh_gate, sh_up, sh_down = w_gate[0].T.astype(_CDT), w_up[0].T.astype(_CDT), w_down[0].astype(_CDT)
    r_in, r_out, r_ff = sh_in.shape[0], sh_out.shape[0], sh_gate.shape[0]
    h_ff = r_ff // 2
    assert h_ff % 16 == 0
    w_in_t, conv_all = _comm_kernel("gather_w_in", [_gather_op(
        [(sh_in, None, 0, r_in), (_pad_rows(conv_w[0], 16), None, 0, 16)])])
    conv_full = conv_all.reshape(N_DEV, 16, conv_cols)[:, :3, :].transpose(1, 0, 2).reshape(3, cw)
    conv_w8 = _pad_rows(conv_full, V7X_SUBLANES)

    tm = _pick(s, (1024, 512, 256, 128))
    tt = _pick(s, (512, 256, 128))
    tn_in = _pick(inw, (512, 256, 128))
    tn_ff = _pick(d_ff, (512, 256, 128))
    tk_ff = _pick(d_ff, (512, 256, 128))
    tk_in = _pick(inw, (1536, 1280, 1024, 512, 256, 128))
    tr = _pick(s, (512, 256, 128))
    tw_ff = _pick(d_ff, (1408, 512, 256, 128))
    tw_in = _pick(inw, (1536, 512, 256, 128))
    tw_d = _pick(d, (1024, 512))

    proj, w_out_f, w_gate_t = _matmul(
        "proj", [[(x_c, w_in_t, "nt")]], s, inw, d, tm, tn_in, d, [],
        [((s, inw), _F32, (tm, tn_in), _tile_ij)], _store_epilogue,
        comm=[_gather_op([(sh_out, None, 0, r_out), (sh_gate, None, 0, h_ff)])])
    mixed, attn, lse, y_conv, w_gate_t, w_up_t = _mixer_fwd(
        dm, proj, pos, invf, sinks, g_attn, g_conv, conv_w8,
        comm=[_gather_op([(sh_gate, w_gate_t, h_ff, r_ff), (sh_up, None, 0, h_ff)])])

    def ln1_epilogue(accs, ex, out, first):
        x_ref, g_ref, b_ref = ex
        h1, xhat, rstd = _ln_fwd(DEEPNORM_ALPHA * x_ref[...] + accs[0], g_ref[...], b_ref[...])
        out[0][...] = h1
        out[1][...] = h1.astype(_CDT)
        out[2][...] = xhat
        out[3][...] = rstd

    h1, h1_c, xhat1, rstd1, w_up_t = _matmul(
        "out_proj_ln1", [[(mixed, w_out_f, "nn")]], s, d, d, tr, d, _pick(d, (1024, 512)),
        [(x2, (tr, d), _row_i), (ln1_g, (1, d), _whole), (ln1_b, (1, d), _whole)],
        [((s, d), _F32, (tr, d), _row_i), ((s, d), _CDT, (tr, d), _row_i), ((s, d), _F32, (tr, d), _row_i),
         ((s, 1), _F32, (tr, 1), _row_i)], ln1_epilogue,
        comm=[_gather_op([(sh_up, w_up_t, h_ff, r_ff)])])

    def swiglu_epilogue(accs, ex, out, first):
        gate, up = accs
        out[0][...] = gate
        out[1][...] = up
        out[2][...] = (gate * jax.nn.sigmoid(gate) * up).astype(_CDT)

    gate, up, act, w_down_f = _matmul(
        "gate_up", [[(h1_c, w_gate_t, "nt")], [(h1_c, w_up_t, "nt")]], s, d_ff, d, tm, tn_ff, d, [],
        [((s, d_ff), _F32, (tm, tn_ff), _tile_ij), ((s, d_ff), _F32, (tm, tn_ff), _tile_ij),
         ((s, d_ff), _CDT, (tm, tn_ff), _tile_ij)], swiglu_epilogue,
        comm=[_gather_op([(sh_down, None, 0, r_ff)])])

    def ln2_loss_epilogue(accs, ex, out, first):
        h1_ref, tgt_ref, g_ref, b_ref = ex
        yv, xhat, rstd = _ln_fwd(DEEPNORM_ALPHA * h1_ref[...] + accs[0], g_ref[...], b_ref[...])
        err = yv - tgt_ref[...]
        dr2, dg, db = _ln_bwd(err * (1.0 / d), xhat, rstd, g_ref[...])
        out[0][...] = dr2
        out[1][...] = dr2.astype(_CDT)
        _accumulate(out[2], jnp.zeros(out[2].shape, _F32) + 0.5 * jnp.sum(err * err) * (1.0 / d), first)
        _accumulate(out[3], dg, first)
        _accumulate(out[4], db, first)

    dr2, dr2_c, loss_acc, d_ln2_g, d_ln2_b = _matmul(
        "down_ln2_loss", [[(act, w_down_f, "nn")]], s, d, d_ff, tr, d, tk_ff,
        [(h1, (tr, d), _row_i), (loss_target[0], (tr, d), _row_i), (ln2_g, (1, d), _whole), (ln2_b, (1, d), _whole)],
        [((s, d), _F32, (tr, d), _row_i), ((s, d), _CDT, (tr, d), _row_i),
         ((V7X_SUBLANES, V7X_LANES), _F32, (V7X_SUBLANES, V7X_LANES), _whole),
         ((1, d), _F32, (1, d), _whole), ((1, d), _F32, (1, d), _whole)], ln2_loss_epilogue)

    def swiglu_bwd_epilogue(accs, ex, out, first):
        gate_v, up_v = ex[0][...], ex[1][...]
        sig = jax.nn.sigmoid(gate_v)
        out[0][...] = (accs[0] * up_v * (sig * (1.0 + gate_v * (1.0 - sig)))).astype(_CDT)
        out[1][...] = (accs[0] * (gate_v * sig)).astype(_CDT)

    dgate, dup = _matmul(
        "dact", [[(dr2_c, w_down_f, "nt")]], s, d_ff, d, tm, tn_ff, d,
        [(gate, (tm, tn_ff), _tile_ij), (up, (tm, tn_ff), _tile_ij)],
        [((s, d_ff), _CDT, (tm, tn_ff), _tile_ij), ((s, d_ff), _CDT, (tm, tn_ff), _tile_ij)], swiglu_bwd_epilogue)
    (dw_down,) = _matmul("dw_down", [[(act, dr2_c, "tn")]], d_ff, d, s, tw_ff, d, tt, [],
                         [((d_ff, d), _CDT, (tw_ff, d), _row_i)], _store_epilogue)
    dw_gate_t, x_down = _matmul("dw_gate", [[(dgate, h1_c, "tn")]], d_ff, d, s, tw_ff, d, tt, [],
                                [((d_ff, d), _CDT, (tw_ff, d), _row_i)], _store_epilogue,
                                comm=[_exchange_op([dw_down])])
    q_down = _pair_sum("chip_sum_w_down", dw_down, x_down)
    dw_up_t, x_gate = _matmul("dw_up", [[(dup, h1_c, "tn")]], d_ff, d, s, tw_ff, d, tt, [],
                              [((d_ff, d), _CDT, (tw_ff, d), _row_i)], _store_epilogue,
                              comm=[_exchange_op([dw_gate_t])])
    q_gate = _pair_sum("chip_sum_w_gate", dw_gate_t, x_gate)

    def ln1_bwd_epilogue(accs, ex, out, first):
        dr2_ref, xhat_ref, rstd_ref, g_ref = ex
        dr1, dg, db = _ln_bwd(DEEPNORM_ALPHA * dr2_ref[...] + accs[0], xhat_ref[...], rstd_ref[...], g_ref[...])
        out[0][...] = dr1
        out[1][...] = dr1.astype(_CDT)
        _accumulate(out[2], dg, first)
        _accumulate(out[3], db, first)

    dr1, dr1_c, d_ln1_g, d_ln1_b, l_down, x_up = _matmul(
        "dh1_ln1_bwd", [[(dgate, w_gate_t, "nn"), (dup, w_up_t, "nn")]], s, d, d_ff, tr, d, tk_ff,
        [(dr2, (tr, d), _row_i), (xhat1, (tr, d), _row_i), (rstd1, (tr, 1), _row_i), (ln1_g, (1, d), _whole)],
        [((s, d), _F32, (tr, d), _row_i), ((s, d), _CDT, (tr, d), _row_i),
         ((1, d), _F32, (1, d), _whole), ((1, d), _F32, (1, d), _whole)], ln1_bwd_epilogue,
        comm=[_chip_send_op([(q_down, None, 0, r_ff)]), _exchange_op([dw_up_t])])
    q_up = _pair_sum("chip_sum_w_up", dw_up_t, x_up)
    tn_d = _pick(d, (512,))
    dmixed, l_gate = _matmul("dmixed", [[(dr1_c, w_out_f, "nt")]], s, d, d, tm, tn_d, d, [],
                             [((s, d), _F32, (tm, tn_d), _tile_ij)], _store_epilogue,
                             comm=[_chip_send_op([(q_gate, None, 0, h_ff)])])
    dw_out, l_gate = _matmul("dw_out", [[(mixed, dr1_c, "tn")]], d, d, s, tw_d, d, tt, [],
                             [((d, d), _CDT, (tw_d, d), _row_i)], _store_epilogue,
                             comm=[_chip_send_op([(q_gate, l_gate, h_ff, r_ff)])])
    dproj, d_g_attn, d_g_conv, d_sinks, d_conv8, l_up, x_out = _mixer_bwd(
        dm, proj, pos, invf, sinks, g_attn, g_conv, conv_w8, dmixed, attn, lse, y_conv,
        comm=[_chip_send_op([(q_up, None, 0, r_ff)]), _exchange_op([dw_out])])
    q_out = _pair_sum("chip_sum_w_out", dw_out, x_out)
    dw_in_t, l_out = _matmul("dw_in", [[(dproj, x_c, "tn")]], inw, d, s, tw_in, d, tt, [],
                             [((inw, d), _CDT, (tw_in, d), _row_i)], _store_epilogue,
                             comm=[_chip_send_op([(q_out, None, 0, r_out)])])
    (x_in,) = _comm_kernel("exchange_w_in", [_exchange_op([dw_in_t])])
    q_in = _pair_sum("chip_sum_w_in", dw_in_t, x_in)

    def dx_epilogue(accs, ex, out, first):
        out[0][...] = DEEPNORM_ALPHA * ex[0][...] + accs[0]

    grad_x, l_in = _matmul("dx", [[(dproj, w_in_t, "nn")]], s, d, inw, tr, d, tk_in,
                           [(dr1, (tr, d), _row_i)], [((s, d), _F32, (tr, d), _row_i)], dx_epilogue,
                           comm=[_chip_send_op([(q_in, None, 0, r_in)])])

    g_in_t, g_out, g_gate_t, g_up_t, g_down = [
        _sum_slots(f"sum_{nm}", ld) for nm, ld in zip(("w_in", "w_out", "w_gate", "w_up", "w_down"),
                                                      (l_in, l_out, l_gate, l_up, l_down))]
    small_parts = [d_conv8[:3], d_sinks, d_g_attn, d_g_conv, d_ln1_g, d_ln1_b, d_ln2_g, d_ln2_b]
    packed, spans = _pack(small_parts)
    reduced = _unpack(_all_reduce_small("reduce_small", packed), spans, [p.shape for p in small_parts])
    g_conv_full, g_sinks, g_g_attn, g_g_conv, g_ln1_g, g_ln1_b, g_ln2_g, g_ln2_b = reduced
    me = _linear(*_position())
    g_conv_w = lax.dynamic_slice(g_conv_full, (0, me * conv_cols), (3, conv_cols))
    loss = lax.psum(loss_acc[0, 0], ("x", "y", "c"))

    big = {"w_in": (w_in[0], g_in_t.T, m_w_in[0], v_w_in[0]), "w_out": (w_out[0], g_out, m_w_out[0], v_w_out[0]),
           "w_gate": (w_gate[0], g_gate_t.T, m_w_gate[0], v_w_gate[0]), "w_up": (w_up[0], g_up_t.T, m_w_up[0], v_w_up[0]),
           "w_down": (w_down[0], g_down, m_w_down[0], v_w_down[0])}
    res = {nm: (g,) + tuple(_adamw(f"adamw_{nm}", w, g, m, v)) for nm, (w, g, m, v) in big.items()}
    small_names = ["conv_w", "sinks", "g_attn", "g_conv", "ln1_g", "ln1_b", "ln2_g", "ln2_b"]
    small_w = [conv_w, sinks, g_attn, g_conv, ln1_g, ln1_b, ln2_g, ln2_b]
    small_g = [g_conv_w[None], g_sinks, g_g_attn, g_g_conv, g_ln1_g, g_ln1_b, g_ln2_g, g_ln2_b]
    small_m = [m_conv_w, m_sinks, m_g_attn, m_g_conv, m_ln1_g, m_ln1_b, m_ln2_g, m_ln2_b]
    small_v = [v_conv_w, v_sinks, v_g_attn, v_g_conv, v_ln1_g, v_ln1_b, v_ln2_g, v_ln2_b]
    pw, sp = _pack(small_w)
    pg, _ = _pack(small_g)
    pm, _ = _pack(small_m)
    pv, _ = _pack(small_v)
    shapes = [w.shape for w in small_w]
    sd, sm, sv = [_unpack(p, sp, shapes) for p in _adamw("adamw_small", pw, pg, pm, pv)]
    for i, nm in enumerate(small_names):
        res[nm] = (small_g[i].reshape(shapes[i]), sd[i], sm[i], sv[i])

    order = ["w_in", "conv_w", "sinks", "g_attn", "g_conv", "w_out", "ln1_g", "ln1_b", "w_gate", "w_up", "w_down", "ln2_g", "ln2_b"]

    def lead(a, nm):
        return a[None] if nm in big else a

    return (loss, grad_x[None],
            *[lead(res[nm][0], nm) for nm in order], *[lead(res[nm][1], nm) for nm in order],
            *[lead(res[nm][2], nm) for nm in order], *[lead(res[nm][3], nm) for nm in order])
```

```python
import functools

import jax
import jax.numpy as jnp
from jax import lax
from jax.experimental import pallas as pl
from jax.experimental.pallas import tpu as pltpu

_F32 = jnp.float32
_CDT = jnp.bfloat16

HEAD_DIM = 64
WINDOW = 128
N_KV_HEADS = 4
KV_WIDTH = N_KV_HEADS * HEAD_DIM
ROT_DIM = HEAD_DIM // 4
ROPE_THETA = 500000.0
ATTN_SCALE = HEAD_DIM ** -0.5
DEPTH = 1
DEEPNORM_ALPHA = (2 * DEPTH) ** 0.25
LN_EPS = 1e-5
RMS_EPS = 1e-6
ADAM_LR = 0.001
ADAM_B1 = 0.9
ADAM_B2 = 0.999
ADAM_EPS = 1e-08
ADAM_WD = 0.01
ADAM_STEP = 10
N_DEV = 8
MASKED = -1e30

V7X_VMEM_BYTES = 64 * 1024 * 1024
V7X_LANES = 128
V7X_SUBLANES = 8
_MESH = pl.DeviceIdType.MESH
_ANY = pl.BlockSpec(memory_space=pl.ANY)


def _vmem_limit(block_bytes, scratch_bytes=0):
    want = 2 * block_bytes + scratch_bytes + 16 * 1024 * 1024
    return int(min(max(want, 32 * 1024 * 1024), V7X_VMEM_BYTES - 8 * 1024 * 1024))


def _nbytes(shape, dtype):
    n = 1
    for s in shape:
        n *= s
    return n * jnp.dtype(dtype).itemsize


def _pick(n, candidates):
    for c in candidates:
        if n % c == 0:
            return c
    raise ValueError(f"no tile of {candidates} divides {n}")


_DOT_DIMS = {"nn": ((1,), (0,)), "nt": ((1,), (1,)), "tn": ((0,), (0,))}


def _dot(a, b, mode):
    return lax.dot_general(a.astype(_CDT), b.astype(_CDT), (_DOT_DIMS[mode], ((), ())),
                           preferred_element_type=_F32)


def _accumulate(ref, val, first):
    @pl.when(first)
    def _():
        ref[...] = val

    @pl.when(jnp.logical_not(first))
    def _():
        ref[...] += val


class _Comm:
    def __init__(self, inputs, outputs, aliases, sems, start, finish):
        self.inputs, self.outputs, self.aliases, self.sems = inputs, outputs, aliases, sems
        self.start, self.finish = start, finish


class _CommArgs:
    def __init__(self, comms, n_in_before, n_out_before):
        self.comms, self.operands, self.out_shape, self.aliases, self.sems, self.at = comms, [], [], {}, [], []
        for cm in comms:
            self.at.append((len(self.operands), len(self.out_shape), len(self.sems)))
            for i_in, i_out in cm.aliases.items():
                self.aliases[n_in_before + len(self.operands) + i_in] = n_out_before + len(self.out_shape) + i_out
            self.operands += cm.inputs
            self.out_shape += cm.outputs
            self.sems += cm.sems

    def _each(self, in_refs, out_refs, sem_refs):
        for cm, (i0, o0, s0) in zip(self.comms, self.at):
            yield cm, (in_refs[i0:i0 + len(cm.inputs)], out_refs[o0:o0 + len(cm.outputs)], sem_refs[s0:s0 + len(cm.sems)])

    def start(self, in_refs, out_refs, sem_refs):
        for cm, refs in self._each(in_refs, out_refs, sem_refs):
            cm.start(*refs)

    def finish(self, in_refs, out_refs, sem_refs):
        for cm, refs in self._each(in_refs, out_refs, sem_refs):
            cm.finish(*refs)


def _matmul(name, groups, m, n, k, tm, tn, tk, extras, outs, epilogue, comm=(), j_outer=False):
    assert m % tm == 0 and n % tn == 0 and k % tk == 0, (name, m, n, k, tm, tn, tk)
    nk = k // tk
    terms = [t for g in groups for t in g]
    operands, in_specs, block_bytes = [], [], 0

    def spec(blk, imap):
        return pl.BlockSpec(blk, (lambda g0, g1, kk: imap(g1, g0, kk)) if j_outer else imap)

    for a, b, mode in terms:
        assert a.shape == ((k, m) if mode == "tn" else (m, k)), (name, a.shape, mode)
        assert b.shape == ((n, k) if mode == "nt" else (k, n)), (name, b.shape, mode)
        if mode == "tn":
            a_blk, a_map = (tk, tm), (lambda i, j, kk: (kk, i))
        else:
            a_blk, a_map = (tm, tk), (lambda i, j, kk: (i, kk))
        if mode == "nt":
            b_blk, b_map = (tn, tk), (lambda i, j, kk: (j, kk))
        else:
            b_blk, b_map = (tk, tn), (lambda i, j, kk: (kk, j))
        operands += [a, b]
        in_specs += [spec(a_blk, a_map), spec(b_blk, b_map)]
        block_bytes += _nbytes(a_blk, a.dtype) + _nbytes(b_blk, b.dtype)
    for arr, blk, imap in extras:
        operands.append(arr)
        in_specs.append(spec(blk, lambda i, j, kk, imap=imap: imap(i, j)))
        block_bytes += _nbytes(blk, arr.dtype)
    out_shape, out_specs = [], []
    for shape, dtype, blk, imap in outs:
        out_shape.append(jax.ShapeDtypeStruct(shape, dtype))
        out_specs.append(spec(blk, lambda i, j, kk, imap=imap: imap(i, j)))
        block_bytes += _nbytes(blk, dtype)
    n_terms, n_extra, n_out, n_groups = len(terms), len(extras), len(outs), len(groups)
    scratch = [pltpu.VMEM((tm, tn), _F32) for _ in range(n_groups)] if nk > 1 else []
    ca = _CommArgs(list(comm), len(operands), n_out)
    n_cin, n_cout, n_acc = len(ca.operands), len(ca.out_shape), len(scratch)
    tiles = (m // tm, n // tn)
    grid = (tiles[1], tiles[0], nk) if j_outer else (tiles[0], tiles[1], nk)

    def body(*refs):
        refs = list(refs)
        term_refs = [refs.pop(0) for _ in range(2 * n_terms)]
        extra_refs = [refs.pop(0) for _ in range(n_extra)]
        cin_refs = [refs.pop(0) for _ in range(n_cin)]
        out_refs = [refs.pop(0) for _ in range(n_out)]
        cout_refs = [refs.pop(0) for _ in range(n_cout)]
        acc_refs = [refs.pop(0) for _ in range(n_acc)]
        sem_refs = refs
        g0, g1, kk = pl.program_id(0), pl.program_id(1), pl.program_id(2)
        first = jnp.logical_and(g0 == 0, g1 == 0)
        if comm:
            @pl.when(jnp.logical_and(first, kk == 0))
            def _():
                ca.start(cin_refs, cout_refs, sem_refs)
        partial, t = [], 0
        for g in groups:
            s = None
            for _, _, mode in g:
                d = _dot(term_refs[2 * t][...], term_refs[2 * t + 1][...], mode)
                s = d if s is None else s + d
                t += 1
            partial.append(s)
        if nk == 1:
            epilogue(partial, extra_refs, out_refs, first)
        else:
            for acc, p in zip(acc_refs, partial):
                _accumulate(acc, p, kk == 0)

            @pl.when(kk == nk - 1)
            def _():
                epilogue([acc[...] for acc in acc_refs], extra_refs, out_refs, first)
        if comm:
            @pl.when(jnp.logical_and(jnp.logical_and(g0 == grid[0] - 1, g1 == grid[1] - 1), kk == nk - 1))
            def _():
                ca.finish(cin_refs, cout_refs, sem_refs)

    res = pl.pallas_call(
        body, name=name, grid=grid,
        in_specs=in_specs + [_ANY] * n_cin, out_specs=out_specs + [_ANY] * n_cout,
        out_shape=out_shape + ca.out_shape, scratch_shapes=scratch + ca.sems, input_output_aliases=ca.aliases,
        compiler_params=pltpu.CompilerParams(
            dimension_semantics=("arbitrary", "arbitrary", "arbitrary"),
            vmem_limit_bytes=_vmem_limit(block_bytes, n_groups * tm * tn * 4 if nk > 1 else 0)),
    )(*operands, *ca.operands)
    return list(res[:n_out]) + list(res[n_out:])


def _store_epilogue(accs, extra_refs, out_refs, first):
    for acc, ref in zip(accs, out_refs):
        ref[...] = acc.astype(ref.dtype)


def _tile_ij(i, j):
    return (i, j)


def _row_i(i, j):
    return (i, 0)


def _whole(i, j):
    return (0, 0)


def _mean(v):
    return jnp.mean(v, axis=-1, keepdims=True)


def _ln_fwd(r, g, b):
    xc = r - _mean(r)
    rstd = lax.rsqrt(_mean(xc * xc) + LN_EPS)
    xhat = xc * rstd
    return xhat * g + b, xhat, rstd


def _ln_bwd(dy, xhat, rstd, g):
    dxh = dy * g
    dr = rstd * (dxh - _mean(dxh) - xhat * _mean(dxh * xhat))
    return dr, jnp.sum(dy * xhat, axis=0, keepdims=True), jnp.sum(dy, axis=0, keepdims=True)


def _rms_fwd(a, g):
    rstd = lax.rsqrt(_mean(a * a) + RMS_EPS)
    return a * rstd * g


def _rms_bwd(dm, a, g):
    rstd = lax.rsqrt(_mean(a * a) + RMS_EPS)
    nhat = a * rstd
    dn = dm * g
    da = rstd * (dn - nhat * _mean(dn * nhat))
    return da, jnp.sum(dm * nhat, axis=0, keepdims=True)


def _lane(shape):
    return lax.broadcasted_iota(jnp.int32, shape, 1)


def _row(shape):
    return lax.broadcasted_iota(jnp.int32, shape, 0)


def _rope_tables(pos, invf):
    ang = pos.astype(_F32) * invf
    lane = _lane(ang.shape)
    in_rot = (lane % HEAD_DIM) < ROT_DIM
    first = (lane % ROT_DIM) < ROT_DIM // 2
    cos = jnp.where(in_rot, jnp.cos(ang), 1.0)
    sin = jnp.sin(ang)
    sgn = jnp.where(in_rot, jnp.where(first, -sin, sin), 0.0)
    return cos, sgn


def _rope(t, cos, sgn, sign):
    half = ROT_DIM // 2
    first = (_lane(t.shape) % ROT_DIM) < half
    partner = jnp.where(first, pltpu.roll(t, V7X_LANES - half, 1), pltpu.roll(t, half, 1))
    return t * cos + partner * (sgn * sign)


def _dup_head(t, h):
    g = t[:, 128 * (h // 2):128 * (h // 2) + 128]
    r = pltpu.roll(g, HEAD_DIM, 1)
    lo = _lane(g.shape) < HEAD_DIM
    return jnp.where(lo, g, r) if h % 2 == 0 else jnp.where(lo, r, g)


def _fold_halves(t):
    return t + pltpu.roll(t, HEAD_DIM, 1)


def _halves(t):
    lo = _lane(t.shape) < HEAD_DIM
    zero = jnp.zeros_like(t)
    return jnp.where(lo, t, zero), jnp.where(lo, zero, t)


def _band_mask(n_keys, first_block):
    i = _row((WINDOW, n_keys))
    j = _lane((WINDOW, n_keys))
    valid = jnp.logical_and(j >= i + 1, j <= i + WINDOW)
    if first_block is not None:
        valid = jnp.logical_and(valid, jnp.logical_or(j >= WINDOW, jnp.logical_not(first_block)))
    return valid


def _shift_down(z, halo, k):
    rows = z.shape[0]
    out = pltpu.roll(z, k, 0)
    r = _row(z.shape)
    for t in range(k):
        out = jnp.where(r == t, halo[V7X_SUBLANES - k + t:V7X_SUBLANES - k + t + 1, :], out)
    del rows
    return out


def _shift_up(z, halo, k):
    rows = z.shape[0]
    out = pltpu.roll(z, rows - k, 0)
    r = _row(z.shape)
    for t in range(k):
        out = jnp.where(r == rows - k + t, halo[t:t + 1, :], out)
    return out


class _Dims:
    def __init__(self, s, d, d_ff):
        self.s, self.d, self.d_ff = s, d, d_ff
        self.aw = d // 2
        self.cw = d - self.aw
        self.nq = self.aw // HEAD_DIM
        self.group = self.nq // N_KV_HEADS
        assert self.group % 2 == 0, "a 128-lane pair of query heads must share its kv head"
        self.inw = self.aw + 2 * KV_WIDTH + 3 * self.cw
        self.o_k = self.aw
        self.o_v = self.aw + KV_WIDTH
        self.o_cg = self.aw + 2 * KV_WIDTH
        self.o_bg = self.o_cg + self.cw
        self.o_u = self.o_bg + self.cw
        self.nb = s // WINDOW
        assert s % WINDOW == 0


def _carrying(body, n_in, n_out, n_steps, ca):
    n_cin, n_cout = len(ca.operands), len(ca.out_shape)

    def wrapped(*refs):
        refs = list(refs)
        in_refs = [refs.pop(0) for _ in range(n_in)]
        cin_refs = [refs.pop(0) for _ in range(n_cin)]
        out_refs = [refs.pop(0) for _ in range(n_out)]
        cout_refs = [refs.pop(0) for _ in range(n_cout)]
        if ca.comms:
            @pl.when(pl.program_id(0) == 0)
            def _():
                ca.start(cin_refs, cout_refs, refs)
        body(*in_refs, *out_refs)
        if ca.comms:
            @pl.when(pl.program_id(0) == n_steps - 1)
            def _():
                ca.finish(cin_refs, cout_refs, refs)

    return wrapped


def _mixer_fwd(dm, proj, pos, invf, sinks, g_attn, g_conv, conv_w8, comm=()):
    s, d, aw, cw, nq, inw, nb = dm.s, dm.d, dm.aw, dm.cw, dm.nq, dm.inw, dm.nb

    def body(pp_ref, pc_ref, posp_ref, posc_ref, invf_ref, sinks_ref, ga_ref, gc_ref, cw_ref,
             mixed_ref, attn_ref, lse_ref, y_ref):
        n = pl.program_id(0)
        cos_c, sgn_c = _rope_tables(posc_ref[...], invf_ref[...])
        cos_p, sgn_p = _rope_tables(posp_ref[...], invf_ref[...])
        kk = jnp.concatenate(
            [jnp.concatenate([_rope(ref[:, dm.o_k + 128 * g:dm.o_k + 128 * g + 128], c, sg, 1.0)
                              for g in range(KV_WIDTH // 128)], axis=1)
             for ref, c, sg in ((pp_ref, cos_p, sgn_p), (pc_ref, cos_c, sgn_c))], axis=0)
        vv = jnp.concatenate([pp_ref[:, dm.o_v:dm.o_v + KV_WIDTH], pc_ref[:, dm.o_v:dm.o_v + KV_WIDTH]], axis=0)
        k2 = [_dup_head(kk, h).astype(_CDT) for h in range(N_KV_HEADS)]
        v2 = [_halves(_dup_head(vv, h).astype(_CDT)) for h in range(N_KV_HEADS)]
        valid = _band_mask(2 * WINDOW, n == 0)
        for j in range(nq // 2):
            h = (2 * j) // dm.group
            qp = _rope(pc_ref[:, 128 * j:128 * j + 128], cos_c, sgn_c, 1.0).astype(_CDT)
            out = None
            for half, qh in enumerate(_halves(qp)):
                hq = 2 * j + half
                sc = jnp.where(valid, _dot(qh, k2[h], "nt") * ATTN_SCALE, MASKED)
                sink = sinks_ref[0, hq]
                mx = jnp.maximum(jnp.max(sc, axis=1, keepdims=True), sink)
                p = jnp.exp(sc - mx)
                den = jnp.sum(p, axis=1, keepdims=True) + jnp.exp(sink - mx)
                o = _dot(p / den, v2[h][half], "nn")
                out = o if out is None else out + o
                lse_ref[:, hq:hq + 1] = mx + jnp.log(den)
            attn_ref[:, 128 * j:128 * j + 128] = out
        mixed_ref[:, 0:aw] = _rms_fwd(attn_ref[...], ga_ref[...]).astype(mixed_ref.dtype)

        z = pc_ref[:, dm.o_cg:dm.o_cg + cw] * pc_ref[:, dm.o_u:dm.o_u + cw]
        top = WINDOW - V7X_SUBLANES
        halo = pp_ref[top:WINDOW, dm.o_cg:dm.o_cg + cw] * pp_ref[top:WINDOW, dm.o_u:dm.o_u + cw]
        halo = jnp.where(n == 0, jnp.zeros_like(halo), halo)
        y = cw_ref[0:1, :] * _shift_down(z, halo, 2) + cw_ref[1:2, :] * _shift_down(z, halo, 1) + cw_ref[2:3, :] * z
        y_ref[...] = y
        conv = pc_ref[:, dm.o_bg:dm.o_bg + cw] * y
        mixed_ref[:, aw:d] = _rms_fwd(conv, gc_ref[...]).astype(mixed_ref.dtype)

    prev = lambda n: (jnp.maximum(n - 1, 0), 0)
    cur = lambda n: (n, 0)
    fixed = lambda n: (0, 0)
    blocks = 2 * WINDOW * inw * 4 + WINDOW * (d * 2 + aw * 4 + cw * 4 + nq * 4)
    ca = _CommArgs(list(comm), 9, 4)
    return pl.pallas_call(
        _carrying(body, 9, 4, nb, ca), name="mixer_fwd", grid=(nb,),
        in_specs=[pl.BlockSpec((WINDOW, inw), prev), pl.BlockSpec((WINDOW, inw), cur),
                  pl.BlockSpec((WINDOW, 1), prev), pl.BlockSpec((WINDOW, 1), cur),
                  pl.BlockSpec((1, V7X_LANES), fixed), pl.BlockSpec(memory_space=pltpu.SMEM),
                  pl.BlockSpec((1, aw), fixed), pl.BlockSpec((1, cw), fixed), pl.BlockSpec((V7X_SUBLANES, cw), fixed)]
        + [_ANY] * len(ca.operands),
        out_specs=[pl.BlockSpec((WINDOW, d), cur), pl.BlockSpec((WINDOW, aw), cur),
                   pl.BlockSpec((WINDOW, nq), cur), pl.BlockSpec((WINDOW, cw), cur)] + [_ANY] * len(ca.out_shape),
        out_shape=[jax.ShapeDtypeStruct((s, d), _CDT), jax.ShapeDtypeStruct((s, aw), _F32),
                   jax.ShapeDtypeStruct((s, nq), _F32), jax.ShapeDtypeStruct((s, cw), _F32)] + ca.out_shape,
        scratch_shapes=ca.sems, input_output_aliases=ca.aliases,
        compiler_params=pltpu.CompilerParams(dimension_semantics=("arbitrary",), vmem_limit_bytes=_vmem_limit(blocks)),
    )(proj, proj, pos, pos, invf, sinks, g_attn, g_conv, conv_w8, *ca.operands)


def _mixer_bwd(dm, proj, pos, invf, sinks, g_attn, g_conv, conv_w8, dmixed, attn, lse, y, comm=()):
    s, d, aw, cw, nq, inw, nb = dm.s, dm.d, dm.aw, dm.cw, dm.nq, dm.inw, dm.nb

    def body(pp_ref, pc_ref, pn_ref, posp_ref, posc_ref, posn_ref, dmc_ref, dmn_ref, ac_ref, an_ref,
             lsec_ref, lsen_ref, yc_ref, yn_ref, invf_ref, sinks_ref, ga_ref, gc_ref, cw_ref,
             dproj_ref, dga_ref, dgc_ref, dsinks_ref, dcw_ref):
        n = pl.program_id(0)
        first = n == 0
        has_next = n < nb - 1
        cos_p, sgn_p = _rope_tables(posp_ref[...], invf_ref[...])
        cos_c, sgn_c = _rope_tables(posc_ref[...], invf_ref[...])
        cos_n, sgn_n = _rope_tables(posn_ref[...], invf_ref[...])

        da_c, dga = _rms_bwd(dmc_ref[:, 0:aw], ac_ref[...], ga_ref[...])
        da_n, _ = _rms_bwd(dmn_ref[:, 0:aw], an_ref[...], ga_ref[...])
        _accumulate(dga_ref, dga, first)
        kk = jnp.concatenate(
            [jnp.concatenate([_rope(ref[:, dm.o_k + 128 * g:dm.o_k + 128 * g + 128], c, sg, 1.0)
                              for g in range(KV_WIDTH // 128)], axis=1)
             for ref, c, sg in ((pp_ref, cos_p, sgn_p), (pc_ref, cos_c, sgn_c))], axis=0)
        vv = jnp.concatenate([pp_ref[:, dm.o_v:dm.o_v + KV_WIDTH], pc_ref[:, dm.o_v:dm.o_v + KV_WIDTH]], axis=0)
        k2 = [_dup_head(kk, h).astype(_CDT) for h in range(N_KV_HEADS)]
        v2 = [_dup_head(vv, h).astype(_CDT) for h in range(N_KV_HEADS)]
        valid_c = _band_mask(2 * WINDOW, first)
        valid_n = jnp.logical_and(_band_mask(WINDOW, None), has_next)
        dk2 = [None] * N_KV_HEADS
        dv2 = [None] * N_KV_HEADS
        dsinks = jnp.zeros((1, nq), _F32)
        head_lane = _lane((1, nq))

        def tile(q_half, do_half, o_pair, da_pair, lse_col, keys, vals, valid, half):
            sc = _dot(q_half, keys, "nt") * ATTN_SCALE
            p = jnp.exp(jnp.where(valid, sc - lse_col, MASKED))
            lo = _lane(o_pair.shape) < HEAD_DIM
            mine = lo if half == 0 else jnp.logical_not(lo)
            delta = jnp.sum(jnp.where(mine, o_pair * da_pair, 0.0), axis=1, keepdims=True)
            dp = _dot(do_half, vals, "nt")
            return p, p * (dp - delta) * ATTN_SCALE, delta

        for j in range(nq // 2):
            h = (2 * j) // dm.group
            cols = slice(128 * j, 128 * j + 128)
            q_c = _halves(_rope(pc_ref[:, cols], cos_c, sgn_c, 1.0).astype(_CDT))
            q_n = _halves(_rope(pn_ref[:, cols], cos_n, sgn_n, 1.0).astype(_CDT))
            do_c = _halves(da_c[:, cols].astype(_CDT))
            do_n = _halves(da_n[:, cols].astype(_CDT))
            k_halves = _halves(k2[h])
            dq = None
            for half in range(2):
                hq = 2 * j + half
                p, ds, delta = tile(q_c[half], do_c[half], ac_ref[:, cols], da_c[:, cols], lsec_ref[:, hq:hq + 1],
                                    k2[h], v2[h], valid_c, half)
                ds = ds.astype(_CDT)
                t = _dot(ds, k_halves[half], "nn")
                dq = t if dq is None else dq + t
                dk = _dot(ds, q_c[half], "tn")[WINDOW:2 * WINDOW, :]
                dv = _dot(p.astype(_CDT), do_c[half], "tn")[WINDOW:2 * WINDOW, :]
                psink = jnp.exp(sinks_ref[0, hq] - lsec_ref[:, hq:hq + 1])
                dsinks = dsinks + jnp.where(head_lane == hq, -jnp.sum(psink * delta), 0.0)
                p, ds, _ = tile(q_n[half], do_n[half], an_ref[:, cols], da_n[:, cols], lsen_ref[:, hq:hq + 1],
                                k2[h][WINDOW:2 * WINDOW, :], v2[h][WINDOW:2 * WINDOW, :], valid_n, half)
                dk = dk + _dot(ds.astype(_CDT), q_n[half], "tn")
                dv = dv + _dot(p.astype(_CDT), do_n[half], "tn")
                dk2[h] = dk if dk2[h] is None else dk2[h] + dk
                dv2[h] = dv if dv2[h] is None else dv2[h] + dv
            dproj_ref[:, cols] = _rope(dq, cos_c, sgn_c, -1.0).astype(dproj_ref.dtype)
        _accumulate(dsinks_ref, dsinks, first)
        lo = _lane((WINDOW, 128)) < HEAD_DIM
        for g in range(KV_WIDTH // 128):
            dk = jnp.where(lo, _fold_halves(dk2[2 * g]), _fold_halves(dk2[2 * g + 1]))
            dv = jnp.where(lo, _fold_halves(dv2[2 * g]), _fold_halves(dv2[2 * g + 1]))
            dproj_ref[:, dm.o_k + 128 * g:dm.o_k + 128 * g + 128] = _rope(dk, cos_c, sgn_c, -1.0).astype(dproj_ref.dtype)
            dproj_ref[:, dm.o_v + 128 * g:dm.o_v + 128 * g + 128] = dv.astype(dproj_ref.dtype)

        bg = pc_ref[:, dm.o_bg:dm.o_bg + cw]
        yc = yc_ref[...]
        dconv, dgc = _rms_bwd(dmc_ref[:, aw:d], bg * yc, gc_ref[...])
        _accumulate(dgc_ref, dgc, first)
        dproj_ref[:, dm.o_bg:dm.o_bg + cw] = (dconv * yc).astype(dproj_ref.dtype)
        dy = dconv * bg
        bg_n = pn_ref[0:V7X_SUBLANES, dm.o_bg:dm.o_bg + cw]
        dconv_n, _ = _rms_bwd(dmn_ref[0:V7X_SUBLANES, aw:d], bg_n * yn_ref[...], gc_ref[...])
        halo = jnp.where(has_next, dconv_n * bg_n, 0.0)
        dy1 = _shift_up(dy, halo, 1)
        dy2 = _shift_up(dy, halo, 2)
        dz = cw_ref[2:3, :] * dy + cw_ref[1:2, :] * dy1 + cw_ref[0:1, :] * dy2
        cg = pc_ref[:, dm.o_cg:dm.o_cg + cw]
        u = pc_ref[:, dm.o_u:dm.o_u + cw]
        dproj_ref[:, dm.o_cg:dm.o_cg + cw] = (dz * u).astype(dproj_ref.dtype)
        dproj_ref[:, dm.o_u:dm.o_u + cw] = (dz * cg).astype(dproj_ref.dtype)
        z = cg * u
        dcw = jnp.concatenate(
            [jnp.sum(z * t, axis=0, keepdims=True) for t in (dy2, dy1, dy)]
            + [jnp.zeros((V7X_SUBLANES - 3, cw), _F32)], axis=0)
        _accumulate(dcw_ref, dcw, first)

    prev = lambda n: (jnp.maximum(n - 1, 0), 0)
    cur = lambda n: (n, 0)
    nxt = lambda n: (jnp.minimum(n + 1, nb - 1), 0)
    nxt8 = lambda n: (jnp.minimum((n + 1) * (WINDOW // V7X_SUBLANES), s // V7X_SUBLANES - 1), 0)
    fixed = lambda n: (0, 0)
    blocks = WINDOW * (3 * inw * 4 + 2 * d * 4 + 2 * aw * 4 + cw * 4 + inw * 2)
    ca = _CommArgs(list(comm), 19, 5)
    return pl.pallas_call(
        _carrying(body, 19, 5, nb, ca), name="mixer_bwd", grid=(nb,),
        in_specs=[pl.BlockSpec((WINDOW, inw), prev), pl.BlockSpec((WINDOW, inw), cur), pl.BlockSpec((WINDOW, inw), nxt),
                  pl.BlockSpec((WINDOW, 1), prev), pl.BlockSpec((WINDOW, 1), cur), pl.BlockSpec((WINDOW, 1), nxt),
                  pl.BlockSpec((WINDOW, d), cur), pl.BlockSpec((WINDOW, d), nxt),
                  pl.BlockSpec((WINDOW, aw), cur), pl.BlockSpec((WINDOW, aw), nxt),
                  pl.BlockSpec((WINDOW, nq), cur), pl.BlockSpec((WINDOW, nq), nxt),
                  pl.BlockSpec((WINDOW, cw), cur), pl.BlockSpec((V7X_SUBLANES, cw), nxt8),
                  pl.BlockSpec((1, V7X_LANES), fixed), pl.BlockSpec(memory_space=pltpu.SMEM),
                  pl.BlockSpec((1, aw), fixed), pl.BlockSpec((1, cw), fixed), pl.BlockSpec((V7X_SUBLANES, cw), fixed)]
        + [_ANY] * len(ca.operands),
        out_specs=[pl.BlockSpec((WINDOW, inw), cur), pl.BlockSpec((1, aw), fixed), pl.BlockSpec((1, cw), fixed),
                   pl.BlockSpec((1, nq), fixed), pl.BlockSpec((V7X_SUBLANES, cw), fixed)] + [_ANY] * len(ca.out_shape),
        out_shape=[jax.ShapeDtypeStruct((s, inw), _CDT), jax.ShapeDtypeStruct((1, aw), _F32),
                   jax.ShapeDtypeStruct((1, cw), _F32), jax.ShapeDtypeStruct((1, nq), _F32),
                   jax.ShapeDtypeStruct((V7X_SUBLANES, cw), _F32)] + ca.out_shape,
        scratch_shapes=ca.sems, input_output_aliases=ca.aliases,
        compiler_params=pltpu.CompilerParams(dimension_semantics=("arbitrary",), vmem_limit_bytes=_vmem_limit(blocks)),
    )(proj, proj, proj, pos, pos, pos, dmixed, dmixed, attn, attn, lse, lse, y, y, invf, sinks, g_attn, g_conv, conv_w8,
      *ca.operands)


def _position():
    return lax.axis_index("x"), lax.axis_index("y"), lax.axis_index("c")


def _linear(px, py, pc):
    return 4 * px + 2 * py + pc


def _comm_kernel(name, comm):
    ca = _CommArgs(list(comm), 0, 0)
    n_cin, n_cout = len(ca.operands), len(ca.out_shape)

    def body(*refs):
        cin, cout, sems = refs[:n_cin], refs[n_cin:n_cin + n_cout], refs[n_cin + n_cout:]
        ca.start(cin, cout, sems)
        ca.finish(cin, cout, sems)

    return pl.pallas_call(
        body, name=name, out_shape=ca.out_shape, in_specs=[_ANY] * n_cin, out_specs=[_ANY] * n_cout,
        scratch_shapes=ca.sems, input_output_aliases=ca.aliases,
    )(*ca.operands)


def _gather_op(units):
    n = len(units)
    inputs, outputs, aliases = [], [], {}
    for shard, _, _, _ in units:
        inputs.append(shard)
        outputs.append(jax.ShapeDtypeStruct((N_DEV * shard.shape[0], shard.shape[1]), shard.dtype))
    for u, (_, buf, _, _) in enumerate(units):
        if buf is not None:
            aliases[len(inputs)] = u
            inputs.append(buf)

    def plan(ins, outs, sems):
        send_sems, recv_sems, local_sems = sems
        x, y, c = _position()
        me, sibling = (x, y, c), (x, y, 1 - c)
        chips = [(1 - x, y), (x, 1 - y), (1 - x, 1 - y)]

        def rows(u, px, py, pc):
            shard, _, r0, r1 = units[u]
            return outs[u].at[pl.ds(pl.multiple_of(_linear(px, py, pc) * shard.shape[0] + r0, 16), r1 - r0), :]

        def own(u):
            _, _, r0, r1 = units[u]
            return ins[u].at[pl.ds(r0, r1 - r0), :]

        def copy(u, k, block, to, src=None):
            return pltpu.make_async_remote_copy(
                src_ref=rows(u, *block) if src is None else src, dst_ref=rows(u, *block),
                send_sem=send_sems.at[u, k], recv_sem=recv_sems.at[u, k], device_id=to, device_id_type=_MESH)

        mine = [pltpu.make_async_copy(own(u), rows(u, *me), local_sems.at[u]) for u in range(n)]
        first = []
        for u in range(n):
            first.append(copy(u, 0, me, sibling, src=own(u)))
            first += [copy(u, 1 + j, me, (*chip, c), src=own(u)) for j, chip in enumerate(chips)]
        passed = [[copy(u, 4 + j, (*chip, c), sibling) for j, chip in enumerate(chips)] for u in range(n)]
        landed = [[copy(u, 1 + j, (*chip, c), me) for j, chip in enumerate(chips)] for u in range(n)]
        rest = [[copy(u, 0, sibling, me)] + [copy(u, 4 + j, (*chip, 1 - c), me) for j, chip in enumerate(chips)]
                for u in range(n)]
        return mine, first, passed, landed, rest

    def start(ins, outs, sems):
        mine, first, _, _, _ = plan(ins, outs, sems)
        for cp in mine + first:
            cp.start()

    def finish(ins, outs, sems):
        mine, first, passed, landed, rest = plan(ins, outs, sems)
        for u in range(n):
            for arrived, onward in zip(landed[u], passed[u]):
                arrived.wait_recv()
                onward.start()
        for u in range(n):
            for cp in rest[u]:
                cp.wait_recv()
        for cp in first + [cp for row in passed for cp in row]:
            cp.wait_send()
        for cp in mine:
            cp.wait()

    sems = [pltpu.SemaphoreType.DMA((n, 7)), pltpu.SemaphoreType.DMA((n, 7)), pltpu.SemaphoreType.DMA((n,))]
    return _Comm(inputs, outputs, aliases, sems, start, finish)


def _peers(x, y, c):
    out = []
    for k in range(1, N_DEV):
        fx, fy, fc = (k >> 2) & 1, (k >> 1) & 1, k & 1
        out.append((1 - x if fx else x, 1 - y if fy else y, 1 - c if fc else c))
    return out


def _exchange_op(partials):
    n = len(partials)
    outputs = [jax.ShapeDtypeStruct((4, p.shape[0] // N_DEV, p.shape[1]), p.dtype) for p in partials]

    def plan(ins, outs, sems):
        send_sems, recv_sems = sems
        x, y, c = _position()
        out = []
        for a in range(n):
            r = outs[a].shape[1]
            for ch in range(4):
                out.append(pltpu.make_async_remote_copy(
                    src_ref=ins[a].at[pl.ds(pl.multiple_of((2 * ch + 1 - c) * r, 16), r), :], dst_ref=outs[a].at[ch],
                    send_sem=send_sems.at[a, ch], recv_sem=recv_sems.at[a, ch], device_id=(x, y, 1 - c),
                    device_id_type=_MESH))
        return out

    def start(ins, outs, sems):
        for cp in plan(ins, outs, sems):
            cp.start()

    def finish(ins, outs, sems):
        copies = plan(ins, outs, sems)
        for cp in copies:
            cp.wait_recv()
        for cp in copies:
            cp.wait_send()

    sems = [pltpu.SemaphoreType.DMA((n, 4)), pltpu.SemaphoreType.DMA((n, 4))]
    return _Comm(list(partials), outputs, {}, sems, start, finish)


def _chip_send_op(units):
    n = len(units)
    inputs, outputs, aliases = [], [], {}
    for q, _, _, _ in units:
        inputs.append(q)
        outputs.append(jax.ShapeDtypeStruct(q.shape, q.dtype))
    for u, (_, buf, _, _) in enumerate(units):
        if buf is not None:
            aliases[len(inputs)] = u
            inputs.append(buf)

    def plan(ins, outs, sems):
        send_sems, recv_sems, local_sems = sems
        x, y, c = _position()
        my_chip = 2 * x + y
        chips = [(1 - x, y), (x, 1 - y), (1 - x, 1 - y)]
        mine, sends, arrivals = [], [], []
        for u, (_, _, r0, r1) in enumerate(units):
            span = pl.ds(r0, r1 - r0)
            mine.append(pltpu.make_async_copy(ins[u].at[my_chip, span, :], outs[u].at[my_chip, span, :], local_sems.at[u]))
            for k, (px, py) in enumerate(chips):
                sends.append(pltpu.make_async_remote_copy(
                    src_ref=ins[u].at[2 * px + py, span, :], dst_ref=outs[u].at[my_chip, span, :],
                    send_sem=send_sems.at[u, k], recv_sem=recv_sems.at[u, k], device_id=(px, py, c), device_id_type=_MESH))
                arrivals.append(pltpu.make_async_remote_copy(
                    src_ref=ins[u].at[my_chip, span, :], dst_ref=outs[u].at[2 * px + py, span, :],
                    send_sem=send_sems.at[u, k], recv_sem=recv_sems.at[u, k], device_id=(px, py, c), device_id_type=_MESH))
        return mine, sends, arrivals

    def start(ins, outs, sems):
        mine, sends, _ = plan(ins, outs, sems)
        for cp in mine + sends:
            cp.start()

    def finish(ins, outs, sems):
        mine, sends, arrivals = plan(ins, outs, sems)
        for cp in arrivals:
            cp.wait_recv()
        for cp in sends:
            cp.wait_send()
        for cp in mine:
            cp.wait()

    sems = [pltpu.SemaphoreType.DMA((n, 3)), pltpu.SemaphoreType.DMA((n, 3)), pltpu.SemaphoreType.DMA((n,))]
    return _Comm(inputs, outputs, aliases, sems, start, finish)


def _pair_sum(name, partial, received):
    _, rows, cols = received.shape
    tr = _pick(rows, (352, 288, 256, 128, 64, 32, 16))
    p4 = partial.reshape(4, 2, rows, cols)
    kind = jnp.reshape(lax.axis_index("c"), (1,)).astype(jnp.int32)

    def body(kind_ref, p_ref, r_ref, o_ref):
        o_ref[0] = (p_ref[0, 0].astype(_F32) + r_ref[0].astype(_F32)).astype(o_ref.dtype)

    return pl.pallas_call(
        body, name=name,
        grid_spec=pltpu.PrefetchScalarGridSpec(
            num_scalar_prefetch=1, grid=(4, rows // tr),
            in_specs=[pl.BlockSpec((1, 1, tr, cols), lambda ch, i, kind_ref: (ch, kind_ref[0], i, 0)),
                      pl.BlockSpec((1, tr, cols), lambda ch, i, kind_ref: (ch, i, 0))],
            out_specs=pl.BlockSpec((1, tr, cols), lambda ch, i, kind_ref: (ch, i, 0))),
        out_shape=jax.ShapeDtypeStruct(received.shape, received.dtype),
        compiler_params=pltpu.CompilerParams(dimension_semantics=("arbitrary", "arbitrary")),
    )(kind, p4, received)


def _all_reduce_small(name, v):
    rows = v.shape[0]

    def body(v_ref, out_ref, land_ref, send_sems, recv_sems):
        x, y, c = _position()
        me = _linear(x, y, c)
        peers = _peers(x, y, c)
        land_ref[me] = v_ref[...]
        sends = [pltpu.make_async_remote_copy(
            src_ref=v_ref, dst_ref=land_ref.at[me], send_sem=send_sems.at[k], recv_sem=recv_sems.at[k],
            device_id=peer, device_id_type=_MESH) for k, peer in enumerate(peers)]
        for cp in sends:
            cp.start()
        for k, peer in enumerate(peers):
            pltpu.make_async_remote_copy(
                src_ref=v_ref, dst_ref=land_ref.at[_linear(*peer)], send_sem=send_sems.at[k], recv_sem=recv_sems.at[k],
                device_id=peer, device_id_type=_MESH).wait_recv()
        for cp in sends:
            cp.wait_send()
        total = land_ref[0]
        for s in range(1, N_DEV):
            total = total + land_ref[s]
        out_ref[...] = total

    return pl.pallas_call(
        body, name=name, out_shape=jax.ShapeDtypeStruct(v.shape, _F32),
        in_specs=[pl.BlockSpec(memory_space=pltpu.VMEM)], out_specs=pl.BlockSpec(memory_space=pltpu.VMEM),
        scratch_shapes=[pltpu.VMEM((N_DEV, rows, V7X_LANES), _F32), pltpu.SemaphoreType.DMA((7,)), pltpu.SemaphoreType.DMA((7,))],
    )(v)


def _adamw(name, w, slots, m, v):
    rows, cols = w.shape
    n_slots = slots.shape[0]
    tr = _pick(rows, (176, 144, 128, 64, 32, 16, 8))

    def body(w_ref, s_ref, m_ref, v_ref, g_ref, d_ref, nm_ref, nv_ref):
        g = s_ref[0].astype(_F32)
        for k in range(1, n_slots):
            g = g + s_ref[k].astype(_F32)
        nm = ADAM_B1 * m_ref[...] + (1.0 - ADAM_B1) * g
        nv = ADAM_B2 * v_ref[...] + (1.0 - ADAM_B2) * (g * g)
        m_hat = nm / (1.0 - ADAM_B1 ** ADAM_STEP)
        v_hat = nv / (1.0 - ADAM_B2 ** ADAM_STEP)
        g_ref[...] = g
        d_ref[...] = -ADAM_LR * (m_hat / (jnp.sqrt(v_hat) + ADAM_EPS) + ADAM_WD * w_ref[...])
        nm_ref[...] = nm
        nv_ref[...] = nv

    spec = pl.BlockSpec((tr, cols), lambda i: (i, 0))
    blocks = 7 * tr * cols * 4 + _nbytes((n_slots, tr, cols), slots.dtype)
    return pl.pallas_call(
        body, name=name, grid=(rows // tr,),
        in_specs=[spec, pl.BlockSpec((n_slots, tr, cols), lambda i: (0, i, 0)), spec, spec], out_specs=[spec] * 4,
        out_shape=[jax.ShapeDtypeStruct((rows, cols), _F32)] * 4,
        compiler_params=pltpu.CompilerParams(dimension_semantics=("arbitrary",), vmem_limit_bytes=_vmem_limit(blocks)),
    )(w, slots, m, v)


def _pad_rows(a, rows):
    return jnp.pad(a, ((0, rows - a.shape[0]), (0, 0)))


def _pack(parts):
    rows, spans, at = [], [], 0
    for p in parts:
        p = p.reshape(-1)
        r = -(-p.shape[0] // V7X_LANES)
        rows.append(jnp.pad(p, (0, r * V7X_LANES - p.shape[0])).reshape(r, V7X_LANES))
        spans.append((at, r, p.shape[0]))
        at += r
    packed = jnp.concatenate(rows, axis=0)
    return _pad_rows(packed, -(-at // V7X_SUBLANES) * V7X_SUBLANES), spans


def _unpack(packed, spans, shapes):
    return [packed[at:at + r].reshape(-1)[:size].reshape(shape) for (at, r, size), shape in zip(spans, shapes)]


def kernel(x, positions, w_in, conv_w, sinks, g_attn, g_conv, w_out, ln1_g, ln1_b, w_gate, w_up, w_down, ln2_g, ln2_b, loss_target, m_w_in, m_conv_w, m_sinks, m_g_attn, m_g_conv, m_w_out, m_ln1_g, m_ln1_b, m_w_gate, m_w_up, m_w_down, m_ln2_g, m_ln2_b, v_w_in, v_conv_w, v_sinks, v_g_attn, v_g_conv, v_w_out, v_ln1_g, v_ln1_b, v_w_gate, v_w_up, v_w_down, v_ln2_g, v_ln2_b):
    _, s, d = x.shape
    d_ff = N_DEV * w_gate.shape[2]
    dm = _Dims(s, d, d_ff)
    aw, cw, nq, inw = dm.aw, dm.cw, dm.nq, dm.inw
    x2 = x[0]
    x_c = x2.astype(_CDT)
    pos = positions[0].reshape(s, 1)
    inv_freq = ROPE_THETA ** (-jnp.arange(0, ROT_DIM, 2, dtype=_F32) / ROT_DIM)
    invf = jnp.tile(inv_freq, V7X_LANES // (ROT_DIM // 2)).reshape(1, V7X_LANES)

    conv_cols = conv_w.shape[2]
    sh_in, sh_out = w_in[0].T.astype(_CDT), w_out[0].astype(_CDT)
    sh_gate, sh_up, sh_down = w_gate[0].T.astype(_CDT), w_up[0].T.astype(_CDT), w_down[0].astype(_CDT)
    r_in, r_out, r_ff = sh_in.shape[0], sh_out.shape[0], sh_gate.shape[0]
    h_ff = r_ff // 2
    assert h_ff % 16 == 0
    w_in_t, conv_all = _comm_kernel("gather_w_in", [_gather_op(
        [(sh_in, None, 0, r_in), (_pad_rows(conv_w[0], 16), None, 0, 16)])])
    conv_full = conv_all.reshape(N_DEV, 16, conv_cols)[:, :3, :].transpose(1, 0, 2).reshape(3, cw)
    conv_w8 = _pad_rows(conv_full, V7X_SUBLANES)

    tm = _pick(s, (1024, 512, 256, 128))
    tn_in = _pick(inw, (512, 256, 128))
    tn_ff = _pick(d_ff, (512, 256, 128))
    tk_ff = _pick(d_ff, (512, 256, 128))
    tk_in = _pick(inw, (1536, 1280, 1024, 512, 256, 128))
    tr = _pick(s, (512, 256, 128))

    proj, w_out_f, w_gate_t = _matmul(
        "proj", [[(x_c, w_in_t, "nt")]], s, inw, d, tm, tn_in, d, [],
        [((s, inw), _F32, (tm, tn_in), _tile_ij)], _store_epilogue,
        comm=[_gather_op([(sh_out, None, 0, r_out), (sh_gate, None, 0, h_ff)])])
    mixed, attn, lse, y_conv, w_gate_t, w_up_t = _mixer_fwd(
        dm, proj, pos, invf, sinks, g_attn, g_conv, conv_w8,
        comm=[_gather_op([(sh_gate, w_gate_t, h_ff, r_ff), (sh_up, None, 0, h_ff)])])

    def ln1_epilogue(accs, ex, out, first):
        x_ref, g_ref, b_ref = ex
        h1, xhat, rstd = _ln_fwd(DEEPNORM_ALPHA * x_ref[...] + accs[0], g_ref[...], b_ref[...])
        out[0][...] = h1
        out[1][...] = h1.astype(_CDT)
        out[2][...] = xhat
        out[3][...] = rstd

    h1, h1_c, xhat1, rstd1, w_up_t = _matmul(
        "out_proj_ln1", [[(mixed, w_out_f, "nn")]], s, d, d, tr, d, _pick(d, (1024, 512)),
        [(x2, (tr, d), _row_i), (ln1_g, (1, d), _whole), (ln1_b, (1, d), _whole)],
        [((s, d), _F32, (tr, d), _row_i), ((s, d), _CDT, (tr, d), _row_i), ((s, d), _F32, (tr, d), _row_i),
         ((s, 1), _F32, (tr, 1), _row_i)], ln1_epilogue,
        comm=[_gather_op([(sh_up, w_up_t, h_ff, r_ff)])])

    def swiglu_epilogue(accs, ex, out, first):
        gate, up = accs
        out[0][...] = gate
        out[1][...] = up
        out[2][...] = (gate * jax.nn.sigmoid(gate) * up).astype(_CDT)

    gate, up, act, w_down_f = _matmul(
        "gate_up", [[(h1_c, w_gate_t, "nt")], [(h1_c, w_up_t, "nt")]], s, d_ff, d, tm, tn_ff, d, [],
        [((s, d_ff), _F32, (tm, tn_ff), _tile_ij), ((s, d_ff), _F32, (tm, tn_ff), _tile_ij),
         ((s, d_ff), _CDT, (tm, tn_ff), _tile_ij)], swiglu_epilogue,
        comm=[_gather_op([(sh_down, None, 0, r_ff)])])

    def ln2_loss_epilogue(accs, ex, out, first):
        h1_ref, tgt_ref, g_ref, b_ref = ex
        yv, xhat, rstd = _ln_fwd(DEEPNORM_ALPHA * h1_ref[...] + accs[0], g_ref[...], b_ref[...])
        err = yv - tgt_ref[...]
        dr2, dg, db = _ln_bwd(err * (1.0 / d), xhat, rstd, g_ref[...])
        out[0][...] = dr2
        out[1][...] = dr2.astype(_CDT)
        _accumulate(out[2], jnp.zeros(out[2].shape, _F32) + 0.5 * jnp.sum(err * err) * (1.0 / d), first)
        _accumulate(out[3], dg, first)
        _accumulate(out[4], db, first)

    dr2, dr2_c, loss_acc, d_ln2_g, d_ln2_b = _matmul(
        "down_ln2_loss", [[(act, w_down_f, "nn")]], s, d, d_ff, tr, d, tk_ff,
        [(h1, (tr, d), _row_i), (loss_target[0], (tr, d), _row_i), (ln2_g, (1, d), _whole), (ln2_b, (1, d), _whole)],
        [((s, d), _F32, (tr, d), _row_i), ((s, d), _CDT, (tr, d), _row_i),
         ((V7X_SUBLANES, V7X_LANES), _F32, (V7X_SUBLANES, V7X_LANES), _whole),
         ((1, d), _F32, (1, d), _whole), ((1, d), _F32, (1, d), _whole)], ln2_loss_epilogue)

    def swiglu_bwd_epilogue(accs, ex, out, first):
        gate_v, up_v = ex[0][...], ex[1][...]
        sig = jax.nn.sigmoid(gate_v)
        out[0][...] = (accs[0] * up_v * (sig * (1.0 + gate_v * (1.0 - sig)))).astype(_CDT)
        out[1][...] = (accs[0] * (gate_v * sig)).astype(_CDT)

    dgate, dup = _matmul(
        "dact", [[(dr2_c, w_down_f, "nt")]], s, d_ff, d, tm, tn_ff, d,
        [(gate, (tm, tn_ff), _tile_ij), (up, (tm, tn_ff), _tile_ij)],
        [((s, d_ff), _CDT, (tm, tn_ff), _tile_ij), ((s, d_ff), _CDT, (tm, tn_ff), _tile_ij)], swiglu_bwd_epilogue)
    def weight_grad(name, a, b, comm=()):
        rows = a.shape[1]
        tw, tn_w = _pick(rows, (512, 256, 128)), _pick(d, (1024, 512))
        return _matmul(name, [[(a, b, "tn")]], rows, d, s, tw, tn_w, s, [],
                       [((rows, d), _CDT, (tw, tn_w), _tile_ij)], _store_epilogue, comm=comm, j_outer=True)

    (dw_down,) = weight_grad("dw_down", act, dr2_c)
    dw_gate_t, x_down = weight_grad("dw_gate", dgate, h1_c, comm=[_exchange_op([dw_down])])
    q_down = _pair_sum("chip_sum_w_down", dw_down, x_down)
    dw_up_t, x_gate = weight_grad("dw_up", dup, h1_c, comm=[_exchange_op([dw_gate_t])])
    q_gate = _pair_sum("chip_sum_w_gate", dw_gate_t, x_gate)

    def ln1_bwd_epilogue(accs, ex, out, first):
        dr2_ref, xhat_ref, rstd_ref, g_ref = ex
        dr1, dg, db = _ln_bwd(DEEPNORM_ALPHA * dr2_ref[...] + accs[0], xhat_ref[...], rstd_ref[...], g_ref[...])
        out[0][...] = dr1
        out[1][...] = dr1.astype(_CDT)
        _accumulate(out[2], dg, first)
        _accumulate(out[3], db, first)

    dr1, dr1_c, d_ln1_g, d_ln1_b, l_down, x_up = _matmul(
        "dh1_ln1_bwd", [[(dgate, w_gate_t, "nn"), (dup, w_up_t, "nn")]], s, d, d_ff, tr, d, tk_ff,
        [(dr2, (tr, d), _row_i), (xhat1, (tr, d), _row_i), (rstd1, (tr, 1), _row_i), (ln1_g, (1, d), _whole)],
        [((s, d), _F32, (tr, d), _row_i), ((s, d), _CDT, (tr, d), _row_i),
         ((1, d), _F32, (1, d), _whole), ((1, d), _F32, (1, d), _whole)], ln1_bwd_epilogue,
        comm=[_chip_send_op([(q_down, None, 0, r_ff)]), _exchange_op([dw_up_t])])
    q_up = _pair_sum("chip_sum_w_up", dw_up_t, x_up)
    tn_d = _pick(d, (512,))
    dmixed, l_gate = _matmul("dmixed", [[(dr1_c, w_out_f, "nt")]], s, d, d, tm, tn_d, d, [],
                             [((s, d), _F32, (tm, tn_d), _tile_ij)], _store_epilogue,
                             comm=[_chip_send_op([(q_gate, None, 0, h_ff)])])
    dw_out, l_gate = weight_grad("dw_out", mixed, dr1_c, comm=[_chip_send_op([(q_gate, l_gate, h_ff, r_ff)])])
    dproj, d_g_attn, d_g_conv, d_sinks, d_conv8, l_up, x_out = _mixer_bwd(
        dm, proj, pos, invf, sinks, g_attn, g_conv, conv_w8, dmixed, attn, lse, y_conv,
        comm=[_chip_send_op([(q_up, None, 0, r_ff)]), _exchange_op([dw_out])])
    q_out = _pair_sum("chip_sum_w_out", dw_out, x_out)
    dw_in_t, l_out = weight_grad("dw_in", dproj, x_c, comm=[_chip_send_op([(q_out, None, 0, r_out)])])
    (x_in,) = _comm_kernel("exchange_w_in", [_exchange_op([dw_in_t])])
    q_in = _pair_sum("chip_sum_w_in", dw_in_t, x_in)

    def dx_epilogue(accs, ex, out, first):
        out[0][...] = DEEPNORM_ALPHA * ex[0][...] + accs[0]

    grad_x, l_in = _matmul("dx", [[(dproj, w_in_t, "nn")]], s, d, inw, tr, d, tk_in,
                           [(dr1, (tr, d), _row_i)], [((s, d), _F32, (tr, d), _row_i)], dx_epilogue,
                           comm=[_chip_send_op([(q_in, None, 0, r_in)])])

    small_parts = [d_conv8[:3], d_sinks, d_g_attn, d_g_conv, d_ln1_g, d_ln1_b, d_ln2_g, d_ln2_b]
    packed, spans = _pack(small_parts)
    reduced = _unpack(_all_reduce_small("reduce_small", packed), spans, [p.shape for p in small_parts])
    g_conv_full, g_sinks, g_g_attn, g_g_conv, g_ln1_g, g_ln1_b, g_ln2_g, g_ln2_b = reduced
    me = _linear(*_position())
    g_conv_w = lax.dynamic_slice(g_conv_full, (0, me * conv_cols), (3, conv_cols))
    loss = lax.psum(loss_acc[0, 0], ("x", "y", "c"))

    big = {"w_in": (w_in[0].T, l_in, m_w_in[0].T, v_w_in[0].T), "w_out": (w_out[0], l_out, m_w_out[0], v_w_out[0]),
           "w_gate": (w_gate[0].T, l_gate, m_w_gate[0].T, v_w_gate[0].T),
           "w_up": (w_up[0].T, l_up, m_w_up[0].T, v_w_up[0].T), "w_down": (w_down[0], l_down, m_w_down[0], v_w_down[0])}
    res = {nm: tuple(_adamw(f"adamw_{nm}", w, slots, m, v)) for nm, (w, slots, m, v) in big.items()}
    for nm in ("w_in", "w_gate", "w_up"):
        res[nm] = tuple(a.T for a in res[nm])
    small_names = ["conv_w", "sinks", "g_attn", "g_conv", "ln1_g", "ln1_b", "ln2_g", "ln2_b"]
    small_w = [conv_w, sinks, g_attn, g_conv, ln1_g, ln1_b, ln2_g, ln2_b]
    small_g = [g_conv_w[None], g_sinks, g_g_attn, g_g_conv, g_ln1_g, g_ln1_b, g_ln2_g, g_ln2_b]
    small_m = [m_conv_w, m_sinks, m_g_attn, m_g_conv, m_ln1_g, m_ln1_b, m_ln2_g, m_ln2_b]
    small_v = [v_conv_w, v_sinks, v_g_attn, v_g_conv, v_ln1_g, v_ln1_b, v_ln2_g, v_ln2_b]
    pw, sp = _pack(small_w)
    pg, _ = _pack(small_g)
    pm, _ = _pack(small_m)
    pv, _ = _pack(small_v)
    shapes = [w.shape for w in small_w]
    _, sd, sm, sv = [_unpack(p, sp, shapes) for p in _adamw("adamw_small", pw, pg[None], pm, pv)]
    for i, nm in enumerate(small_names):
        res[nm] = (small_g[i].reshape(shapes[i]), sd[i], sm[i], sv[i])

    order = ["w_in", "conv_w", "sinks", "g_attn", "g_conv", "w_out", "ln1_g", "ln1_b", "w_gate", "w_up", "w_down", "ln2_g", "ln2_b"]

    def lead(a, nm):
        return a[None] if nm in big else a

    return (loss, grad_x[None],
            *[lead(res[nm][0], nm) for nm in order], *[lead(res[nm][1], nm) for nm in order],
            *[lead(res[nm][2], nm) for nm in order], *[lead(res[nm][3], nm) for nm in order])
```

```python
import functools

import jax
import jax.numpy as jnp
from jax import lax
from jax.experimental import pallas as pl
from jax.experimental.pallas import tpu as pltpu

_F32 = jnp.float32
_CDT = jnp.bfloat16

HEAD_DIM = 64
WINDOW = 128
N_KV_HEADS = 4
KV_WIDTH = N_KV_HEADS * HEAD_DIM
ROT_DIM = HEAD_DIM // 4
ROPE_THETA = 500000.0
ATTN_SCALE = HEAD_DIM ** -0.5
DEPTH = 1
DEEPNORM_ALPHA = (2 * DEPTH) ** 0.25
LN_EPS = 1e-5
RMS_EPS = 1e-6
ADAM_LR = 0.001
ADAM_B1 = 0.9
ADAM_B2 = 0.999
ADAM_EPS = 1e-08
ADAM_WD = 0.01
ADAM_STEP = 10
N_DEV = 8
MASKED = -1e30

V7X_VMEM_BYTES = 64 * 1024 * 1024
V7X_LANES = 128
V7X_SUBLANES = 8
_MESH = pl.DeviceIdType.MESH
_ANY = pl.BlockSpec(memory_space=pl.ANY)


def _vmem_limit(block_bytes, scratch_bytes=0):
    want = 2 * block_bytes + scratch_bytes + 16 * 1024 * 1024
    return int(min(max(want, 32 * 1024 * 1024), V7X_VMEM_BYTES - 8 * 1024 * 1024))


def _nbytes(shape, dtype):
    n = 1
    for s in shape:
        n *= s
    return n * jnp.dtype(dtype).itemsize


def _pick(n, candidates):
    for c in candidates:
        if n % c == 0:
            return c
    raise ValueError(f"no tile of {candidates} divides {n}")


_DOT_DIMS = {"nn": ((1,), (0,)), "nt": ((1,), (1,)), "tn": ((0,), (0,))}


def _dot(a, b, mode):
    return lax.dot_general(a.astype(_CDT), b.astype(_CDT), (_DOT_DIMS[mode], ((), ())),
                           preferred_element_type=_F32)


def _accumulate(ref, val, first):
    @pl.when(first)
    def _():
        ref[...] = val

    @pl.when(jnp.logical_not(first))
    def _():
        ref[...] += val


class _Comm:
    def __init__(self, inputs, outputs, aliases, sems, start, finish):
        self.inputs, self.outputs, self.aliases, self.sems = inputs, outputs, aliases, sems
        self.start, self.finish = start, finish


class _CommArgs:
    def __init__(self, comms, n_in_before, n_out_before):
        self.comms, self.operands, self.out_shape, self.aliases, self.sems, self.at = comms, [], [], {}, [], []
        for cm in comms:
            self.at.append((len(self.operands), len(self.out_shape), len(self.sems)))
            for i_in, i_out in cm.aliases.items():
                self.aliases[n_in_before + len(self.operands) + i_in] = n_out_before + len(self.out_shape) + i_out
            self.operands += cm.inputs
            self.out_shape += cm.outputs
            self.sems += cm.sems

    def _each(self, in_refs, out_refs, sem_refs):
        for cm, (i0, o0, s0) in zip(self.comms, self.at):
            yield cm, (in_refs[i0:i0 + len(cm.inputs)], out_refs[o0:o0 + len(cm.outputs)], sem_refs[s0:s0 + len(cm.sems)])

    def start(self, in_refs, out_refs, sem_refs):
        for cm, refs in self._each(in_refs, out_refs, sem_refs):
            cm.start(*refs)

    def finish(self, in_refs, out_refs, sem_refs):
        for cm, refs in self._each(in_refs, out_refs, sem_refs):
            cm.finish(*refs)


def _matmul(name, groups, m, n, k, tm, tn, tk, extras, outs, epilogue, comm=(), j_outer=False, n_split=1):
    assert m % tm == 0 and n % tn == 0 and k % tk == 0, (name, m, n, k, tm, tn, tk)
    nk = k // tk
    assert n_split == 1 or (nk == 1 and tn % (n_split * V7X_LANES) == 0), (name, n_split)
    terms = [t for g in groups for t in g]
    operands, in_specs, block_bytes = [], [], 0

    def spec(blk, imap):
        return pl.BlockSpec(blk, (lambda g0, g1, kk: imap(g1, g0, kk)) if j_outer else imap)

    for a, b, mode in terms:
        assert a.shape == ((k, m) if mode == "tn" else (m, k)), (name, a.shape, mode)
        assert b.shape == ((n, k) if mode == "nt" else (k, n)), (name, b.shape, mode)
        if mode == "tn":
            a_blk, a_map = (tk, tm), (lambda i, j, kk: (kk, i))
        else:
            a_blk, a_map = (tm, tk), (lambda i, j, kk: (i, kk))
        if mode == "nt":
            b_blk, b_map = (tn, tk), (lambda i, j, kk: (j, kk))
        else:
            b_blk, b_map = (tk, tn), (lambda i, j, kk: (kk, j))
        operands += [a, b]
        in_specs += [spec(a_blk, a_map), spec(b_blk, b_map)]
        block_bytes += _nbytes(a_blk, a.dtype) + _nbytes(b_blk, b.dtype)
    for arr, blk, imap in extras:
        operands.append(arr)
        in_specs.append(spec(blk, lambda i, j, kk, imap=imap: imap(i, j)))
        block_bytes += _nbytes(blk, arr.dtype)
    out_shape, out_specs = [], []
    for shape, dtype, blk, imap in outs:
        out_shape.append(jax.ShapeDtypeStruct(shape, dtype))
        out_specs.append(spec(blk, lambda i, j, kk, imap=imap: imap(i, j)))
        block_bytes += _nbytes(blk, dtype)
    n_terms, n_extra, n_out, n_groups = len(terms), len(extras), len(outs), len(groups)
    scratch = [pltpu.VMEM((tm, tn), _F32) for _ in range(n_groups)] if nk > 1 else []
    ca = _CommArgs(list(comm), len(operands), n_out)
    n_cin, n_cout, n_acc = len(ca.operands), len(ca.out_shape), len(scratch)
    tiles = (m // tm, n // tn)
    grid = (tiles[1], tiles[0], nk) if j_outer else (tiles[0], tiles[1], nk)

    def body(*refs):
        refs = list(refs)
        term_refs = [refs.pop(0) for _ in range(2 * n_terms)]
        extra_refs = [refs.pop(0) for _ in range(n_extra)]
        cin_refs = [refs.pop(0) for _ in range(n_cin)]
        out_refs = [refs.pop(0) for _ in range(n_out)]
        cout_refs = [refs.pop(0) for _ in range(n_cout)]
        acc_refs = [refs.pop(0) for _ in range(n_acc)]
        sem_refs = refs
        g0, g1, kk = pl.program_id(0), pl.program_id(1), pl.program_id(2)
        first = jnp.logical_and(g0 == 0, g1 == 0)
        if comm:
            @pl.when(jnp.logical_and(first, kk == 0))
            def _():
                ca.start(cin_refs, cout_refs, sem_refs)
        def products(cols):
            partial, t = [], 0
            for g in groups:
                s = None
                for _, _, mode in g:
                    b_ref = term_refs[2 * t + 1]
                    b = b_ref[...] if cols is None else (b_ref[cols, :] if mode == "nt" else b_ref[:, cols])
                    d = _dot(term_refs[2 * t][...], b, mode)
                    s = d if s is None else s + d
                    t += 1
                partial.append(s)
            return partial

        if n_split > 1:
            width = tn // n_split
            for c in range(n_split):
                cols = pl.ds(c * width, width)
                view = lambda ref: ref.at[:, cols] if tuple(ref.shape) == (tm, tn) else ref
                epilogue(products(cols), [view(r) for r in extra_refs], [view(r) for r in out_refs], first)
        elif nk == 1:
            epilogue(products(None), extra_refs, out_refs, first)
        else:
            partial = products(None)
            for acc, p in zip(acc_refs, partial):
                _accumulate(acc, p, kk == 0)

            @pl.when(kk == nk - 1)
            def _():
                epilogue([acc[...] for acc in acc_refs], extra_refs, out_refs, first)
        if comm:
            @pl.when(jnp.logical_and(jnp.logical_and(g0 == grid[0] - 1, g1 == grid[1] - 1), kk == nk - 1))
            def _():
                ca.finish(cin_refs, cout_refs, sem_refs)

    res = pl.pallas_call(
        body, name=name, grid=grid,
        in_specs=in_specs + [_ANY] * n_cin, out_specs=out_specs + [_ANY] * n_cout,
        out_shape=out_shape + ca.out_shape, scratch_shapes=scratch + ca.sems, input_output_aliases=ca.aliases,
        compiler_params=pltpu.CompilerParams(
            dimension_semantics=("arbitrary", "arbitrary", "arbitrary"),
            vmem_limit_bytes=_vmem_limit(block_bytes, n_groups * tm * tn * 4 if nk > 1 else 0)),
    )(*operands, *ca.operands)
    return list(res[:n_out]) + list(res[n_out:])


def _store_epilogue(accs, extra_refs, out_refs, first):
    for acc, ref in zip(accs, out_refs):
        ref[...] = acc.astype(ref.dtype)


def _tile_ij(i, j):
    return (i, j)


def _row_i(i, j):
    return (i, 0)


def _whole(i, j):
    return (0, 0)


def _mean(v):
    return jnp.mean(v, axis=-1, keepdims=True)


def _ln_fwd(r, g, b):
    xc = r - _mean(r)
    rstd = lax.rsqrt(_mean(xc * xc) + LN_EPS)
    xhat = xc * rstd
    return xhat * g + b, xhat, rstd


def _ln_bwd(dy, xhat, rstd, g):
    dxh = dy * g
    dr = rstd * (dxh - _mean(dxh) - xhat * _mean(dxh * xhat))
    return dr, jnp.sum(dy * xhat, axis=0, keepdims=True), jnp.sum(dy, axis=0, keepdims=True)


def _rms_fwd(a, g):
    rstd = lax.rsqrt(_mean(a * a) + RMS_EPS)
    return a * rstd * g


def _rms_bwd(dm, a, g):
    rstd = lax.rsqrt(_mean(a * a) + RMS_EPS)
    nhat = a * rstd
    dn = dm * g
    da = rstd * (dn - nhat * _mean(dn * nhat))
    return da, jnp.sum(dm * nhat, axis=0, keepdims=True)


def _lane(shape):
    return lax.broadcasted_iota(jnp.int32, shape, 1)


def _row(shape):
    return lax.broadcasted_iota(jnp.int32, shape, 0)


def _rope_tables(pos, invf):
    ang = pos.astype(_F32) * invf
    lane = _lane(ang.shape)
    in_rot = (lane % HEAD_DIM) < ROT_DIM
    first = (lane % ROT_DIM) < ROT_DIM // 2
    cos = jnp.where(in_rot, jnp.cos(ang), 1.0)
    sin = jnp.sin(ang)
    sgn = jnp.where(in_rot, jnp.where(first, -sin, sin), 0.0)
    return cos, sgn


def _rope(t, cos, sgn, sign):
    half = ROT_DIM // 2
    first = (_lane(t.shape) % ROT_DIM) < half
    partner = jnp.where(first, pltpu.roll(t, V7X_LANES - half, 1), pltpu.roll(t, half, 1))
    return t * cos + partner * (sgn * sign)


def _dup_head(t, h):
    g = t[:, 128 * (h // 2):128 * (h // 2) + 128]
    r = pltpu.roll(g, HEAD_DIM, 1)
    lo = _lane(g.shape) < HEAD_DIM
    return jnp.where(lo, g, r) if h % 2 == 0 else jnp.where(lo, r, g)


def _fold_halves(t):
    return t + pltpu.roll(t, HEAD_DIM, 1)


def _halves(t):
    lo = _lane(t.shape) < HEAD_DIM
    zero = jnp.zeros_like(t)
    return jnp.where(lo, t, zero), jnp.where(lo, zero, t)


def _band_mask(n_keys, first_block):
    i = _row((WINDOW, n_keys))
    j = _lane((WINDOW, n_keys))
    valid = jnp.logical_and(j >= i + 1, j <= i + WINDOW)
    if first_block is not None:
        valid = jnp.logical_and(valid, jnp.logical_or(j >= WINDOW, jnp.logical_not(first_block)))
    return valid


def _shift_down(z, halo, k):
    rows = z.shape[0]
    out = pltpu.roll(z, k, 0)
    r = _row(z.shape)
    for t in range(k):
        out = jnp.where(r == t, halo[V7X_SUBLANES - k + t:V7X_SUBLANES - k + t + 1, :], out)
    del rows
    return out


def _shift_up(z, halo, k):
    rows = z.shape[0]
    out = pltpu.roll(z, rows - k, 0)
    r = _row(z.shape)
    for t in range(k):
        out = jnp.where(r == rows - k + t, halo[t:t + 1, :], out)
    return out


class _Dims:
    def __init__(self, s, d, d_ff):
        self.s, self.d, self.d_ff = s, d, d_ff
        self.aw = d // 2
        self.cw = d - self.aw
        self.nq = self.aw // HEAD_DIM
        self.group = self.nq // N_KV_HEADS
        assert self.group % 2 == 0, "a 128-lane pair of query heads must share its kv head"
        self.inw = self.aw + 2 * KV_WIDTH + 3 * self.cw
        self.o_k = self.aw
        self.o_v = self.aw + KV_WIDTH
        self.o_cg = self.aw + 2 * KV_WIDTH
        self.o_bg = self.o_cg + self.cw
        self.o_u = self.o_bg + self.cw
        self.nb = s // WINDOW
        assert s % WINDOW == 0


def _carrying(body, n_in, n_out, n_steps, ca):
    n_cin, n_cout = len(ca.operands), len(ca.out_shape)

    def wrapped(*refs):
        refs = list(refs)
        in_refs = [refs.pop(0) for _ in range(n_in)]
        cin_refs = [refs.pop(0) for _ in range(n_cin)]
        out_refs = [refs.pop(0) for _ in range(n_out)]
        cout_refs = [refs.pop(0) for _ in range(n_cout)]
        if ca.comms:
            @pl.when(pl.program_id(0) == 0)
            def _():
                ca.start(cin_refs, cout_refs, refs)
        body(*in_refs, *out_refs)
        if ca.comms:
            @pl.when(pl.program_id(0) == n_steps - 1)
            def _():
                ca.finish(cin_refs, cout_refs, refs)

    return wrapped


def _row_kernel(name, body, rows_in, vecs_in, rows_out, vecs_out, comm=()):
    s = rows_in[0].shape[0]
    tr = _pick(s, (256, 128))
    row = lambda a: pl.BlockSpec((tr, a[1] if isinstance(a, tuple) else a.shape[1]), lambda i: (i, 0))
    vec = lambda shape: pl.BlockSpec(tuple(shape), lambda i: (0, 0))
    n_in, n_out = len(rows_in) + len(vecs_in), len(rows_out) + len(vecs_out)
    ca = _CommArgs(list(comm), n_in, n_out)
    blocks = sum(_nbytes((tr, a.shape[1]), a.dtype) for a in rows_in) + sum(_nbytes((tr, sh[1]), dt) for sh, dt in rows_out)
    res = pl.pallas_call(
        _carrying(body, n_in, n_out, s // tr, ca), name=name, grid=(s // tr,),
        in_specs=[row(a) for a in rows_in] + [vec(v.shape) for v in vecs_in] + [_ANY] * len(ca.operands),
        out_specs=[row(sh) for sh, _ in rows_out] + [vec(sh) for sh, _ in vecs_out] + [_ANY] * len(ca.out_shape),
        out_shape=[jax.ShapeDtypeStruct(sh, dt) for sh, dt in list(rows_out) + list(vecs_out)] + ca.out_shape,
        scratch_shapes=ca.sems, input_output_aliases=ca.aliases,
        compiler_params=pltpu.CompilerParams(dimension_semantics=("arbitrary",), vmem_limit_bytes=_vmem_limit(blocks)),
    )(*rows_in, *vecs_in, *ca.operands)
    return list(res)


def _ln2_loss_bwd(r2, target, gain, bias, comm=()):
    s, d = r2.shape

    def body(r_ref, t_ref, g_ref, b_ref, dr_ref, drc_ref, loss_ref, dg_ref, db_ref):
        first = pl.program_id(0) == 0
        yv, xhat, rstd = _ln_fwd(r_ref[...], g_ref[...], b_ref[...])
        err = yv - t_ref[...]
        dr2, dg, db = _ln_bwd(err * (1.0 / d), xhat, rstd, g_ref[...])
        dr_ref[...] = dr2
        drc_ref[...] = dr2.astype(_CDT)
        _accumulate(loss_ref, jnp.zeros(loss_ref.shape, _F32) + 0.5 * jnp.sum(err * err) * (1.0 / d), first)
        _accumulate(dg_ref, dg, first)
        _accumulate(db_ref, db, first)

    return _row_kernel("ln2_loss_bwd", body, [r2, target], [gain, bias], [((s, d), _F32), ((s, d), _CDT)],
                       [((V7X_SUBLANES, V7X_LANES), _F32), ((1, d), _F32), ((1, d), _F32)], comm)


def _ln1_bwd_rows(dh1, xhat, rstd, gain, comm=()):
    s, d = dh1.shape

    def body(dh_ref, xhat_ref, rstd_ref, g_ref, dr_ref, drc_ref, dg_ref, db_ref):
        first = pl.program_id(0) == 0
        dr1, dg, db = _ln_bwd(dh_ref[...], xhat_ref[...], rstd_ref[...], g_ref[...])
        dr_ref[...] = dr1
        drc_ref[...] = dr1.astype(_CDT)
        _accumulate(dg_ref, dg, first)
        _accumulate(db_ref, db, first)

    return _row_kernel("ln1_bwd", body, [dh1, xhat, rstd], [gain], [((s, d), _F32), ((s, d), _CDT)],
                       [((1, d), _F32), ((1, d), _F32)], comm)


def _mixer_fwd(dm, proj, pos, invf, sinks, g_attn, g_conv, conv_w8, comm=()):
    s, d, aw, cw, nq, inw, nb = dm.s, dm.d, dm.aw, dm.cw, dm.nq, dm.inw, dm.nb

    def body(pp_ref, pc_ref, posp_ref, posc_ref, invf_ref, sinks_ref, ga_ref, gc_ref, cw_ref,
             mixed_ref, attn_ref, lse_ref, y_ref):
        n = pl.program_id(0)
        cos_c, sgn_c = _rope_tables(posc_ref[...], invf_ref[...])
        cos_p, sgn_p = _rope_tables(posp_ref[...], invf_ref[...])
        kk = jnp.concatenate(
            [jnp.concatenate([_rope(ref[:, dm.o_k + 128 * g:dm.o_k + 128 * g + 128], c, sg, 1.0)
                              for g in range(KV_WIDTH // 128)], axis=1)
             for ref, c, sg in ((pp_ref, cos_p, sgn_p), (pc_ref, cos_c, sgn_c))], axis=0)
        vv = jnp.concatenate([pp_ref[:, dm.o_v:dm.o_v + KV_WIDTH], pc_ref[:, dm.o_v:dm.o_v + KV_WIDTH]], axis=0)
        k2 = [_dup_head(kk, h).astype(_CDT) for h in range(N_KV_HEADS)]
        v2 = [_halves(_dup_head(vv, h).astype(_CDT)) for h in range(N_KV_HEADS)]
        valid = _band_mask(2 * WINDOW, n == 0)
        for j in range(nq // 2):
            h = (2 * j) // dm.group
            qp = _rope(pc_ref[:, 128 * j:128 * j + 128], cos_c, sgn_c, 1.0).astype(_CDT)
            out = None
            for half, qh in enumerate(_halves(qp)):
                hq = 2 * j + half
                sc = jnp.where(valid, _dot(qh, k2[h], "nt") * ATTN_SCALE, MASKED)
                sink = sinks_ref[0, hq]
                mx = jnp.maximum(jnp.max(sc, axis=1, keepdims=True), sink)
                p = jnp.exp(sc - mx)
                den = jnp.sum(p, axis=1, keepdims=True) + jnp.exp(sink - mx)
                o = _dot(p / den, v2[h][half], "nn")
                out = o if out is None else out + o
                lse_ref[:, hq:hq + 1] = mx + jnp.log(den)
            attn_ref[:, 128 * j:128 * j + 128] = out
        mixed_ref[:, 0:aw] = _rms_fwd(attn_ref[...], ga_ref[...]).astype(mixed_ref.dtype)

        z = pc_ref[:, dm.o_cg:dm.o_cg + cw] * pc_ref[:, dm.o_u:dm.o_u + cw]
        top = WINDOW - V7X_SUBLANES
        halo = pp_ref[top:WINDOW, dm.o_cg:dm.o_cg + cw] * pp_ref[top:WINDOW, dm.o_u:dm.o_u + cw]
        halo = jnp.where(n == 0, jnp.zeros_like(halo), halo)
        y = cw_ref[0:1, :] * _shift_down(z, halo, 2) + cw_ref[1:2, :] * _shift_down(z, halo, 1) + cw_ref[2:3, :] * z
        y_ref[...] = y
        conv = pc_ref[:, dm.o_bg:dm.o_bg + cw] * y
        mixed_ref[:, aw:d] = _rms_fwd(conv, gc_ref[...]).astype(mixed_ref.dtype)

    prev = lambda n: (jnp.maximum(n - 1, 0), 0)
    cur = lambda n: (n, 0)
    fixed = lambda n: (0, 0)
    blocks = 2 * WINDOW * inw * 4 + WINDOW * (d * 2 + aw * 4 + cw * 4 + nq * 4)
    ca = _CommArgs(list(comm), 9, 4)
    return pl.pallas_call(
        _carrying(body, 9, 4, nb, ca), name="mixer_fwd", grid=(nb,),
        in_specs=[pl.BlockSpec((WINDOW, inw), prev), pl.BlockSpec((WINDOW, inw), cur),
                  pl.BlockSpec((WINDOW, 1), prev), pl.BlockSpec((WINDOW, 1), cur),
                  pl.BlockSpec((1, V7X_LANES), fixed), pl.BlockSpec(memory_space=pltpu.SMEM),
                  pl.BlockSpec((1, aw), fixed), pl.BlockSpec((1, cw), fixed), pl.BlockSpec((V7X_SUBLANES, cw), fixed)]
        + [_ANY] * len(ca.operands),
        out_specs=[pl.BlockSpec((WINDOW, d), cur), pl.BlockSpec((WINDOW, aw), cur),
                   pl.BlockSpec((WINDOW, nq), cur), pl.BlockSpec((WINDOW, cw), cur)] + [_ANY] * len(ca.out_shape),
        out_shape=[jax.ShapeDtypeStruct((s, d), _CDT), jax.ShapeDtypeStruct((s, aw), _F32),
                   jax.ShapeDtypeStruct((s, nq), _F32), jax.ShapeDtypeStruct((s, cw), _F32)] + ca.out_shape,
        scratch_shapes=ca.sems, input_output_aliases=ca.aliases,
        compiler_params=pltpu.CompilerParams(dimension_semantics=("arbitrary",), vmem_limit_bytes=_vmem_limit(blocks)),
    )(proj, proj, pos, pos, invf, sinks, g_attn, g_conv, conv_w8, *ca.operands)


def _mixer_bwd(dm, proj, pos, invf, sinks, g_attn, g_conv, conv_w8, dmixed, attn, lse, y, comm=()):
    s, d, aw, cw, nq, inw, nb = dm.s, dm.d, dm.aw, dm.cw, dm.nq, dm.inw, dm.nb

    def body(pp_ref, pc_ref, pn_ref, posp_ref, posc_ref, posn_ref, dmc_ref, dmn_ref, ac_ref, an_ref,
             lsec_ref, lsen_ref, yc_ref, yn_ref, invf_ref, sinks_ref, ga_ref, gc_ref, cw_ref,
             dproj_ref, dga_ref, dgc_ref, dsinks_ref, dcw_ref):
        n = pl.program_id(0)
        first = n == 0
        has_next = n < nb - 1
        cos_p, sgn_p = _rope_tables(posp_ref[...], invf_ref[...])
        cos_c, sgn_c = _rope_tables(posc_ref[...], invf_ref[...])
        cos_n, sgn_n = _rope_tables(posn_ref[...], invf_ref[...])

        da_c, dga = _rms_bwd(dmc_ref[:, 0:aw], ac_ref[...], ga_ref[...])
        da_n, _ = _rms_bwd(dmn_ref[:, 0:aw], an_ref[...], ga_ref[...])
        _accumulate(dga_ref, dga, first)
        kk = jnp.concatenate(
            [jnp.concatenate([_rope(ref[:, dm.o_k + 128 * g:dm.o_k + 128 * g + 128], c, sg, 1.0)
                              for g in range(KV_WIDTH // 128)], axis=1)
             for ref, c, sg in ((pp_ref, cos_p, sgn_p), (pc_ref, cos_c, sgn_c))], axis=0)
        vv = jnp.concatenate([pp_ref[:, dm.o_v:dm.o_v + KV_WIDTH], pc_ref[:, dm.o_v:dm.o_v + KV_WIDTH]], axis=0)
        k2 = [_dup_head(kk, h).astype(_CDT) for h in range(N_KV_HEADS)]
        v2 = [_dup_head(vv, h).astype(_CDT) for h in range(N_KV_HEADS)]
        valid_c = _band_mask(2 * WINDOW, first)
        valid_n = jnp.logical_and(_band_mask(WINDOW, None), has_next)
        dk2 = [None] * N_KV_HEADS
        dv2 = [None] * N_KV_HEADS
        dsinks = jnp.zeros((1, nq), _F32)
        head_lane = _lane((1, nq))

        def tile(q_half, do_half, o_pair, da_pair, lse_col, keys, vals, valid, half):
            sc = _dot(q_half, keys, "nt") * ATTN_SCALE
            p = jnp.exp(jnp.where(valid, sc - lse_col, MASKED))
            lo = _lane(o_pair.shape) < HEAD_DIM
            mine = lo if half == 0 else jnp.logical_not(lo)
            delta = jnp.sum(jnp.where(mine, o_pair * da_pair, 0.0), axis=1, keepdims=True)
            dp = _dot(do_half, vals, "nt")
            return p, p * (dp - delta) * ATTN_SCALE, delta

        for j in range(nq // 2):
            h = (2 * j) // dm.group
            cols = slice(128 * j, 128 * j + 128)
            q_c = _halves(_rope(pc_ref[:, cols], cos_c, sgn_c, 1.0).astype(_CDT))
            q_n = _halves(_rope(pn_ref[:, cols], cos_n, sgn_n, 1.0).astype(_CDT))
            do_c = _halves(da_c[:, cols].astype(_CDT))
            do_n = _halves(da_n[:, cols].astype(_CDT))
            k_halves = _halves(k2[h])
            dq = None
            for half in range(2):
                hq = 2 * j + half
                p, ds, delta = tile(q_c[half], do_c[half], ac_ref[:, cols], da_c[:, cols], lsec_ref[:, hq:hq + 1],
                                    k2[h], v2[h], valid_c, half)
                ds = ds.astype(_CDT)
                t = _dot(ds, k_halves[half], "nn")
                dq = t if dq is None else dq + t
                dk = _dot(ds, q_c[half], "tn")[WINDOW:2 * WINDOW, :]
                dv = _dot(p.astype(_CDT), do_c[half], "tn")[WINDOW:2 * WINDOW, :]
                psink = jnp.exp(sinks_ref[0, hq] - lsec_ref[:, hq:hq + 1])
                dsinks = dsinks + jnp.where(head_lane == hq, -jnp.sum(psink * delta), 0.0)
                p, ds, _ = tile(q_n[half], do_n[half], an_ref[:, cols], da_n[:, cols], lsen_ref[:, hq:hq + 1],
                                k2[h][WINDOW:2 * WINDOW, :], v2[h][WINDOW:2 * WINDOW, :], valid_n, half)
                dk = dk + _dot(ds.astype(_CDT), q_n[half], "tn")
                dv = dv + _dot(p.astype(_CDT), do_n[half], "tn")
                dk2[h] = dk if dk2[h] is None else dk2[h] + dk
                dv2[h] = dv if dv2[h] is None else dv2[h] + dv
            dproj_ref[:, cols] = _rope(dq, cos_c, sgn_c, -1.0).astype(dproj_ref.dtype)
        _accumulate(dsinks_ref, dsinks, first)
        lo = _lane((WINDOW, 128)) < HEAD_DIM
        for g in range(KV_WIDTH // 128):
            dk = jnp.where(lo, _fold_halves(dk2[2 * g]), _fold_halves(dk2[2 * g + 1]))
            dv = jnp.where(lo, _fold_halves(dv2[2 * g]), _fold_halves(dv2[2 * g + 1]))
            dproj_ref[:, dm.o_k + 128 * g:dm.o_k + 128 * g + 128] = _rope(dk, cos_c, sgn_c, -1.0).astype(dproj_ref.dtype)
            dproj_ref[:, dm.o_v + 128 * g:dm.o_v + 128 * g + 128] = dv.astype(dproj_ref.dtype)

        bg = pc_ref[:, dm.o_bg:dm.o_bg + cw]
        yc = yc_ref[...]
        dconv, dgc = _rms_bwd(dmc_ref[:, aw:d], bg * yc, gc_ref[...])
        _accumulate(dgc_ref, dgc, first)
        dproj_ref[:, dm.o_bg:dm.o_bg + cw] = (dconv * yc).astype(dproj_ref.dtype)
        dy = dconv * bg
        bg_n = pn_ref[0:V7X_SUBLANES, dm.o_bg:dm.o_bg + cw]
        dconv_n, _ = _rms_bwd(dmn_ref[0:V7X_SUBLANES, aw:d], bg_n * yn_ref[...], gc_ref[...])
        halo = jnp.where(has_next, dconv_n * bg_n, 0.0)
        dy1 = _shift_up(dy, halo, 1)
        dy2 = _shift_up(dy, halo, 2)
        dz = cw_ref[2:3, :] * dy + cw_ref[1:2, :] * dy1 + cw_ref[0:1, :] * dy2
        cg = pc_ref[:, dm.o_cg:dm.o_cg + cw]
        u = pc_ref[:, dm.o_u:dm.o_u + cw]
        dproj_ref[:, dm.o_cg:dm.o_cg + cw] = (dz * u).astype(dproj_ref.dtype)
        dproj_ref[:, dm.o_u:dm.o_u + cw] = (dz * cg).astype(dproj_ref.dtype)
        z = cg * u
        dcw = jnp.concatenate(
            [jnp.sum(z * t, axis=0, keepdims=True) for t in (dy2, dy1, dy)]
            + [jnp.zeros((V7X_SUBLANES - 3, cw), _F32)], axis=0)
        _accumulate(dcw_ref, dcw, first)

    prev = lambda n: (jnp.maximum(n - 1, 0), 0)
    cur = lambda n: (n, 0)
    nxt = lambda n: (jnp.minimum(n + 1, nb - 1), 0)
    nxt8 = lambda n: (jnp.minimum((n + 1) * (WINDOW // V7X_SUBLANES), s // V7X_SUBLANES - 1), 0)
    fixed = lambda n: (0, 0)
    blocks = WINDOW * (3 * inw * 4 + 2 * d * 4 + 2 * aw * 4 + cw * 4 + inw * 2)
    ca = _CommArgs(list(comm), 19, 5)
    return pl.pallas_call(
        _carrying(body, 19, 5, nb, ca), name="mixer_bwd", grid=(nb,),
        in_specs=[pl.BlockSpec((WINDOW, inw), prev), pl.BlockSpec((WINDOW, inw), cur), pl.BlockSpec((WINDOW, inw), nxt),
                  pl.BlockSpec((WINDOW, 1), prev), pl.BlockSpec((WINDOW, 1), cur), pl.BlockSpec((WINDOW, 1), nxt),
                  pl.BlockSpec((WINDOW, d), cur), pl.BlockSpec((WINDOW, d), nxt),
                  pl.BlockSpec((WINDOW, aw), cur), pl.BlockSpec((WINDOW, aw), nxt),
                  pl.BlockSpec((WINDOW, nq), cur), pl.BlockSpec((WINDOW, nq), nxt),
                  pl.BlockSpec((WINDOW, cw), cur), pl.BlockSpec((V7X_SUBLANES, cw), nxt8),
                  pl.BlockSpec((1, V7X_LANES), fixed), pl.BlockSpec(memory_space=pltpu.SMEM),
                  pl.BlockSpec((1, aw), fixed), pl.BlockSpec((1, cw), fixed), pl.BlockSpec((V7X_SUBLANES, cw), fixed)]
        + [_ANY] * len(ca.operands),
        out_specs=[pl.BlockSpec((WINDOW, inw), cur), pl.BlockSpec((1, aw), fixed), pl.BlockSpec((1, cw), fixed),
                   pl.BlockSpec((1, nq), fixed), pl.BlockSpec((V7X_SUBLANES, cw), fixed)] + [_ANY] * len(ca.out_shape),
        out_shape=[jax.ShapeDtypeStruct((s, inw), _CDT), jax.ShapeDtypeStruct((1, aw), _F32),
                   jax.ShapeDtypeStruct((1, cw), _F32), jax.ShapeDtypeStruct((1, nq), _F32),
                   jax.ShapeDtypeStruct((V7X_SUBLANES, cw), _F32)] + ca.out_shape,
        scratch_shapes=ca.sems, input_output_aliases=ca.aliases,
        compiler_params=pltpu.CompilerParams(dimension_semantics=("arbitrary",), vmem_limit_bytes=_vmem_limit(blocks)),
    )(proj, proj, proj, pos, pos, pos, dmixed, dmixed, attn, attn, lse, lse, y, y, invf, sinks, g_attn, g_conv, conv_w8,
      *ca.operands)


def _position():
    return lax.axis_index("x"), lax.axis_index("y"), lax.axis_index("c")


def _linear(px, py, pc):
    return 4 * px + 2 * py + pc


def _comm_kernel(name, comm):
    ca = _CommArgs(list(comm), 0, 0)
    n_cin, n_cout = len(ca.operands), len(ca.out_shape)

    def body(*refs):
        cin, cout, sems = refs[:n_cin], refs[n_cin:n_cin + n_cout], refs[n_cin + n_cout:]
        ca.start(cin, cout, sems)
        ca.finish(cin, cout, sems)

    return pl.pallas_call(
        body, name=name, out_shape=ca.out_shape, in_specs=[_ANY] * n_cin, out_specs=[_ANY] * n_cout,
        scratch_shapes=ca.sems, input_output_aliases=ca.aliases,
    )(*ca.operands)


def _gather_op(units):
    n = len(units)
    inputs, outputs, aliases = [], [], {}
    for shard, _, _, _ in units:
        inputs.append(shard)
        outputs.append(jax.ShapeDtypeStruct((N_DEV * shard.shape[0], shard.shape[1]), shard.dtype))
    for u, (_, buf, _, _) in enumerate(units):
        if buf is not None:
            aliases[len(inputs)] = u
            inputs.append(buf)

    def plan(ins, outs, sems):
        send_sems, recv_sems, local_sems = sems
        x, y, c = _position()
        me, sibling = (x, y, c), (x, y, 1 - c)
        chips = [(1 - x, y), (x, 1 - y), (1 - x, 1 - y)]

        def rows(u, px, py, pc):
            shard, _, r0, r1 = units[u]
            return outs[u].at[pl.ds(pl.multiple_of(_linear(px, py, pc) * shard.shape[0] + r0, 16), r1 - r0), :]

        def own(u):
            _, _, r0, r1 = units[u]
            return ins[u].at[pl.ds(r0, r1 - r0), :]

        def copy(u, k, block, to, src=None):
            return pltpu.make_async_remote_copy(
                src_ref=rows(u, *block) if src is None else src, dst_ref=rows(u, *block),
                send_sem=send_sems.at[u, k], recv_sem=recv_sems.at[u, k], device_id=to, device_id_type=_MESH)

        mine = [pltpu.make_async_copy(own(u), rows(u, *me), local_sems.at[u]) for u in range(n)]
        first = []
        for u in range(n):
            first.append(copy(u, 0, me, sibling, src=own(u)))
            first += [copy(u, 1 + j, me, (*chip, c), src=own(u)) for j, chip in enumerate(chips)]
        passed = [[copy(u, 4 + j, (*chip, c), sibling) for j, chip in enumerate(chips)] for u in range(n)]
        landed = [[copy(u, 1 + j, (*chip, c), me) for j, chip in enumerate(chips)] for u in range(n)]
        rest = [[copy(u, 0, sibling, me)] + [copy(u, 4 + j, (*chip, 1 - c), me) for j, chip in enumerate(chips)]
                for u in range(n)]
        return mine, first, passed, landed, rest

    def start(ins, outs, sems):
        mine, first, _, _, _ = plan(ins, outs, sems)
        for cp in mine + first:
            cp.start()

    def finish(ins, outs, sems):
        mine, first, passed, landed, rest = plan(ins, outs, sems)
        for u in range(n):
            for arrived, onward in zip(landed[u], passed[u]):
                arrived.wait_recv()
                onward.start()
        for u in range(n):
            for cp in rest[u]:
                cp.wait_recv()
        for cp in first + [cp for row in passed for cp in row]:
            cp.wait_send()
        for cp in mine:
            cp.wait()

    sems = [pltpu.SemaphoreType.DMA((n, 7)), pltpu.SemaphoreType.DMA((n, 7)), pltpu.SemaphoreType.DMA((n,))]
    return _Comm(inputs, outputs, aliases, sems, start, finish)


def _peers(x, y, c):
    out = []
    for k in range(1, N_DEV):
        fx, fy, fc = (k >> 2) & 1, (k >> 1) & 1, k & 1
        out.append((1 - x if fx else x, 1 - y if fy else y, 1 - c if fc else c))
    return out


def _exchange_op(partials):
    n = len(partials)
    outputs = [jax.ShapeDtypeStruct((4, p.shape[0] // N_DEV, p.shape[1]), p.dtype) for p in partials]

    def plan(ins, outs, sems):
        send_sems, recv_sems = sems
        x, y, c = _position()
        out = []
        for a in range(n):
            r = outs[a].shape[1]
            for ch in range(4):
                out.append(pltpu.make_async_remote_copy(
                    src_ref=ins[a].at[pl.ds(pl.multiple_of((2 * ch + 1 - c) * r, 16), r), :], dst_ref=outs[a].at[ch],
                    send_sem=send_sems.at[a, ch], recv_sem=recv_sems.at[a, ch], device_id=(x, y, 1 - c),
                    device_id_type=_MESH))
        return out

    def start(ins, outs, sems):
        for cp in plan(ins, outs, sems):
            cp.start()

    def finish(ins, outs, sems):
        copies = plan(ins, outs, sems)
        for cp in copies:
            cp.wait_recv()
        for cp in copies:
            cp.wait_send()

    sems = [pltpu.SemaphoreType.DMA((n, 4)), pltpu.SemaphoreType.DMA((n, 4))]
    return _Comm(list(partials), outputs, {}, sems, start, finish)


def _chip_send_op(units):
    n = len(units)
    inputs, outputs, aliases = [], [], {}
    for q, _, _, _ in units:
        inputs.append(q)
        outputs.append(jax.ShapeDtypeStruct(q.shape, q.dtype))
    for u, (_, buf, _, _) in enumerate(units):
        if buf is not None:
            aliases[len(inputs)] = u
            inputs.append(buf)

    def plan(ins, outs, sems):
        send_sems, recv_sems, local_sems = sems
        x, y, c = _position()
        my_chip = 2 * x + y
        chips = [(1 - x, y), (x, 1 - y), (1 - x, 1 - y)]
        mine, sends, arrivals = [], [], []
        for u, (_, _, r0, r1) in enumerate(units):
            span = pl.ds(r0, r1 - r0)
            mine.append(pltpu.make_async_copy(ins[u].at[my_chip, span, :], outs[u].at[my_chip, span, :], local_sems.at[u]))
            for k, (px, py) in enumerate(chips):
                sends.append(pltpu.make_async_remote_copy(
                    src_ref=ins[u].at[2 * px + py, span, :], dst_ref=outs[u].at[my_chip, span, :],
                    send_sem=send_sems.at[u, k], recv_sem=recv_sems.at[u, k], device_id=(px, py, c), device_id_type=_MESH))
                arrivals.append(pltpu.make_async_remote_copy(
                    src_ref=ins[u].at[my_chip, span, :], dst_ref=outs[u].at[2 * px + py, span, :],
                    send_sem=send_sems.at[u, k], recv_sem=recv_sems.at[u, k], device_id=(px, py, c), device_id_type=_MESH))
        return mine, sends, arrivals

    def start(ins, outs, sems):
        mine, sends, _ = plan(ins, outs, sems)
        for cp in mine + sends:
            cp.start()

    def finish(ins, outs, sems):
        mine, sends, arrivals = plan(ins, outs, sems)
        for cp in arrivals:
            cp.wait_recv()
        for cp in sends:
            cp.wait_send()
        for cp in mine:
            cp.wait()

    sems = [pltpu.SemaphoreType.DMA((n, 3)), pltpu.SemaphoreType.DMA((n, 3)), pltpu.SemaphoreType.DMA((n,))]
    return _Comm(inputs, outputs, aliases, sems, start, finish)


def _pair_sum(name, partial, received):
    _, rows, cols = received.shape
    tr = _pick(rows, (352, 288, 256, 128, 64, 32, 16))
    p4 = partial.reshape(4, 2, rows, cols)
    kind = jnp.reshape(lax.axis_index("c"), (1,)).astype(jnp.int32)

    def body(kind_ref, p_ref, r_ref, o_ref):
        o_ref[0] = (p_ref[0, 0].astype(_F32) + r_ref[0].astype(_F32)).astype(o_ref.dtype)

    return pl.pallas_call(
        body, name=name,
        grid_spec=pltpu.PrefetchScalarGridSpec(
            num_scalar_prefetch=1, grid=(4, rows // tr),
            in_specs=[pl.BlockSpec((1, 1, tr, cols), lambda ch, i, kind_ref: (ch, kind_ref[0], i, 0)),
                      pl.BlockSpec((1, tr, cols), lambda ch, i, kind_ref: (ch, i, 0))],
            out_specs=pl.BlockSpec((1, tr, cols), lambda ch, i, kind_ref: (ch, i, 0))),
        out_shape=jax.ShapeDtypeStruct(received.shape, received.dtype),
        compiler_params=pltpu.CompilerParams(dimension_semantics=("arbitrary", "arbitrary")),
    )(kind, p4, received)


def _all_reduce_small(name, v):
    rows = v.shape[0]

    def body(v_ref, out_ref, land_ref, send_sems, recv_sems):
        x, y, c = _position()
        me = _linear(x, y, c)
        peers = _peers(x, y, c)
        land_ref[me] = v_ref[...]
        sends = [pltpu.make_async_remote_copy(
            src_ref=v_ref, dst_ref=land_ref.at[me], send_sem=send_sems.at[k], recv_sem=recv_sems.at[k],
            device_id=peer, device_id_type=_MESH) for k, peer in enumerate(peers)]
        for cp in sends:
            cp.start()
        for k, peer in enumerate(peers):
            pltpu.make_async_remote_copy(
                src_ref=v_ref, dst_ref=land_ref.at[_linear(*peer)], send_sem=send_sems.at[k], recv_sem=recv_sems.at[k],
                device_id=peer, device_id_type=_MESH).wait_recv()
        for cp in sends:
            cp.wait_send()
        total = land_ref[0]
        for s in range(1, N_DEV):
            total = total + land_ref[s]
        out_ref[...] = total

    return pl.pallas_call(
        body, name=name, out_shape=jax.ShapeDtypeStruct(v.shape, _F32),
        in_specs=[pl.BlockSpec(memory_space=pltpu.VMEM)], out_specs=pl.BlockSpec(memory_space=pltpu.VMEM),
        scratch_shapes=[pltpu.VMEM((N_DEV, rows, V7X_LANES), _F32), pltpu.SemaphoreType.DMA((7,)), pltpu.SemaphoreType.DMA((7,))],
    )(v)


def _adamw(name, w, slots, m, v):
    rows, cols = w.shape
    n_slots = slots.shape[0]
    tr = _pick(rows, (176, 144, 128, 64, 32, 16, 8))

    def body(w_ref, s_ref, m_ref, v_ref, g_ref, d_ref, nm_ref, nv_ref):
        g = s_ref[0].astype(_F32)
        for k in range(1, n_slots):
            g = g + s_ref[k].astype(_F32)
        nm = ADAM_B1 * m_ref[...] + (1.0 - ADAM_B1) * g
        nv = ADAM_B2 * v_ref[...] + (1.0 - ADAM_B2) * (g * g)
        m_hat = nm / (1.0 - ADAM_B1 ** ADAM_STEP)
        v_hat = nv / (1.0 - ADAM_B2 ** ADAM_STEP)
        g_ref[...] = g
        d_ref[...] = -ADAM_LR * (m_hat / (jnp.sqrt(v_hat) + ADAM_EPS) + ADAM_WD * w_ref[...])
        nm_ref[...] = nm
        nv_ref[...] = nv

    spec = pl.BlockSpec((tr, cols), lambda i: (i, 0))
    blocks = 7 * tr * cols * 4 + _nbytes((n_slots, tr, cols), slots.dtype)
    return pl.pallas_call(
        body, name=name, grid=(rows // tr,),
        in_specs=[spec, pl.BlockSpec((n_slots, tr, cols), lambda i: (0, i, 0)), spec, spec], out_specs=[spec] * 4,
        out_shape=[jax.ShapeDtypeStruct((rows, cols), _F32)] * 4,
        compiler_params=pltpu.CompilerParams(dimension_semantics=("arbitrary",), vmem_limit_bytes=_vmem_limit(blocks)),
    )(w, slots, m, v)


def _pad_rows(a, rows):
    return jnp.pad(a, ((0, rows - a.shape[0]), (0, 0)))


def _pack(parts):
    rows, spans, at = [], [], 0
    for p in parts:
        p = p.reshape(-1)
        r = -(-p.shape[0] // V7X_LANES)
        rows.append(jnp.pad(p, (0, r * V7X_LANES - p.shape[0])).reshape(r, V7X_LANES))
        spans.append((at, r, p.shape[0]))
        at += r
    packed = jnp.concatenate(rows, axis=0)
    return _pad_rows(packed, -(-at // V7X_SUBLANES) * V7X_SUBLANES), spans


def _unpack(packed, spans, shapes):
    return [packed[at:at + r].reshape(-1)[:size].reshape(shape) for (at, r, size), shape in zip(spans, shapes)]


def kernel(x, positions, w_in, conv_w, sinks, g_attn, g_conv, w_out, ln1_g, ln1_b, w_gate, w_up, w_down, ln2_g, ln2_b, loss_target, m_w_in, m_conv_w, m_sinks, m_g_attn, m_g_conv, m_w_out, m_ln1_g, m_ln1_b, m_w_gate, m_w_up, m_w_down, m_ln2_g, m_ln2_b, v_w_in, v_conv_w, v_sinks, v_g_attn, v_g_conv, v_w_out, v_ln1_g, v_ln1_b, v_w_gate, v_w_up, v_w_down, v_ln2_g, v_ln2_b):
    _, s, d = x.shape
    d_ff = N_DEV * w_gate.shape[2]
    dm = _Dims(s, d, d_ff)
    aw, cw, nq, inw = dm.aw, dm.cw, dm.nq, dm.inw
    x2 = x[0]
    x_c = x2.astype(_CDT)
    pos = positions[0].reshape(s, 1)
    inv_freq = ROPE_THETA ** (-jnp.arange(0, ROT_DIM, 2, dtype=_F32) / ROT_DIM)
    invf = jnp.tile(inv_freq, V7X_LANES // (ROT_DIM // 2)).reshape(1, V7X_LANES)

    conv_cols = conv_w.shape[2]
    sh_in, sh_out = w_in[0].T.astype(_CDT), w_out[0].astype(_CDT)
    sh_gate, sh_up, sh_down = w_gate[0].T.astype(_CDT), w_up[0].T.astype(_CDT), w_down[0].astype(_CDT)
    r_in, r_out, r_ff = sh_in.shape[0], sh_out.shape[0], sh_gate.shape[0]
    h_ff, q_ff = r_ff // 2, r_ff // 4
    assert q_ff % 16 == 0
    w_in_t, conv_all = _comm_kernel("gather_w_in", [_gather_op(
        [(sh_in, None, 0, r_in), (_pad_rows(conv_w[0], 16), None, 0, 16)])])
    conv_full = conv_all.reshape(N_DEV, 16, conv_cols)[:, :3, :].transpose(1, 0, 2).reshape(3, cw)
    conv_w8 = _pad_rows(conv_full, V7X_SUBLANES)

    tm = _pick(s, (1024, 512, 256, 128))
    tn_in = _pick(inw, (512, 256, 128))
    tn_ff = _pick(d_ff, (512, 256, 128))
    tk_in = _pick(inw, (1536, 1280, 1024, 512, 256, 128))
    tr = _pick(s, (512, 256, 128))

    proj, w_out_f, w_gate_t = _matmul(
        "proj", [[(x_c, w_in_t, "nt")]], s, inw, d, tm, tn_in, d, [],
        [((s, inw), _F32, (tm, tn_in), _tile_ij)], _store_epilogue,
        comm=[_gather_op([(sh_out, None, 0, r_out), (sh_gate, None, 0, q_ff)])])
    mixed, attn, lse, y_conv, w_gate_t, w_up_t = _mixer_fwd(
        dm, proj, pos, invf, sinks, g_attn, g_conv, conv_w8,
        comm=[_gather_op([(sh_gate, w_gate_t, q_ff, r_ff), (sh_up, None, 0, q_ff)])])

    def ln1_epilogue(accs, ex, out, first):
        x_ref, g_ref, b_ref = ex
        h1, xhat, rstd = _ln_fwd(DEEPNORM_ALPHA * x_ref[...] + accs[0], g_ref[...], b_ref[...])
        out[0][...] = h1
        out[1][...] = h1.astype(_CDT)
        out[2][...] = xhat
        out[3][...] = rstd

    h1, h1_c, xhat1, rstd1, w_up_t = _matmul(
        "out_proj_ln1", [[(mixed, w_out_f, "nn")]], s, d, d, tr, d, _pick(d, (1024, 512)),
        [(x2, (tr, d), _row_i), (ln1_g, (1, d), _whole), (ln1_b, (1, d), _whole)],
        [((s, d), _F32, (tr, d), _row_i), ((s, d), _CDT, (tr, d), _row_i), ((s, d), _F32, (tr, d), _row_i),
         ((s, 1), _F32, (tr, 1), _row_i)], ln1_epilogue,
        comm=[_gather_op([(sh_up, w_up_t, q_ff, h_ff)])])

    gate, w_up_t, w_down_f = _matmul(
        "gate", [[(h1_c, w_gate_t, "nt")]], s, d_ff, d, tm, tn_ff, d, [],
        [((s, d_ff), _F32, (tm, tn_ff), _tile_ij)], _store_epilogue,
        comm=[_gather_op([(sh_up, w_up_t, h_ff, r_ff), (sh_down, None, 0, q_ff)])])

    def swiglu_epilogue(accs, ex, out, first):
        gate_v = ex[0][...]
        out[0][...] = accs[0]
        out[1][...] = (gate_v * jax.nn.sigmoid(gate_v) * accs[0]).astype(_CDT)

    up, act, w_down_f = _matmul(
        "up_swiglu", [[(h1_c, w_up_t, "nt")]], s, d_ff, d, tm, tn_ff, d, [(gate, (tm, tn_ff), _tile_ij)],
        [((s, d_ff), _F32, (tm, tn_ff), _tile_ij), ((s, d_ff), _CDT, (tm, tn_ff), _tile_ij)], swiglu_epilogue,
        comm=[_gather_op([(sh_down, w_down_f, q_ff, r_ff)])], n_split=2)

    def residual_epilogue(accs, ex, out, first):
        out[0][...] = DEEPNORM_ALPHA * ex[0][...] + accs[0]

    tn_d = _pick(d, (512,))
    (r2,) = _matmul("down", [[(act, w_down_f, "nn")]], s, d, d_ff, tr, tn_d, d_ff, [(h1, (tr, tn_d), _tile_ij)],
                    [((s, d), _F32, (tr, tn_d), _tile_ij)], residual_epilogue)
    dr2, dr2_c, loss_acc, d_ln2_g, d_ln2_b = _ln2_loss_bwd(r2, loss_target[0], ln2_g, ln2_b)

    def swiglu_bwd_epilogue(accs, ex, out, first):
        gate_v, up_v = ex[0][...], ex[1][...]
        sig = jax.nn.sigmoid(gate_v)
        out[0][...] = (accs[0] * up_v * (sig * (1.0 + gate_v * (1.0 - sig)))).astype(_CDT)
        out[1][...] = (accs[0] * (gate_v * sig)).astype(_CDT)

    dgate, dup = _matmul(
        "dact", [[(dr2_c, w_down_f, "nt")]], s, d_ff, d, tm, tn_ff, d,
        [(gate, (tm, tn_ff), _tile_ij), (up, (tm, tn_ff), _tile_ij)],
        [((s, d_ff), _CDT, (tm, tn_ff), _tile_ij), ((s, d_ff), _CDT, (tm, tn_ff), _tile_ij)], swiglu_bwd_epilogue,
        n_split=2)
    def weight_grad(name, a, b, comm=()):
        rows = a.shape[1]
        tw, tn_w = _pick(rows, (512, 256, 128)), _pick(d, (1024, 512))
        return _matmul(name, [[(a, b, "tn")]], rows, d, s, tw, tn_w, s, [],
                       [((rows, d), _CDT, (tw, tn_w), _tile_ij)], _store_epilogue, comm=comm, j_outer=True)

    (dw_down,) = weight_grad("dw_down", act, dr2_c)
    dw_gate_t, x_down = weight_grad("dw_gate", dgate, h1_c, comm=[_exchange_op([dw_down])])
    q_down = _pair_sum("chip_sum_w_down", dw_down, x_down)
    dw_up_t, x_gate = weight_grad("dw_up", dup, h1_c, comm=[_exchange_op([dw_gate_t])])
    q_gate = _pair_sum("chip_sum_w_gate", dw_gate_t, x_gate)

    tn_h = _pick(d, (256,))
    dh1, l_down, x_up = _matmul(
        "dh1", [[(dgate, w_gate_t, "nn"), (dup, w_up_t, "nn")]], s, d, d_ff, tr, tn_h, d_ff,
        [(dr2, (tr, tn_h), _tile_ij)], [((s, d), _F32, (tr, tn_h), _tile_ij)], residual_epilogue,
        comm=[_chip_send_op([(q_down, None, 0, r_ff)]), _exchange_op([dw_up_t])])
    q_up = _pair_sum("chip_sum_w_up", dw_up_t, x_up)
    dr1, dr1_c, d_ln1_g, d_ln1_b = _ln1_bwd_rows(dh1, xhat1, rstd1, ln1_g)
    dmixed, l_gate = _matmul("dmixed", [[(dr1_c, w_out_f, "nt")]], s, d, d, tm, tn_d, d, [],
                             [((s, d), _F32, (tm, tn_d), _tile_ij)], _store_epilogue,
                             comm=[_chip_send_op([(q_gate, None, 0, h_ff)])])
    dw_out, l_gate = weight_grad("dw_out", mixed, dr1_c, comm=[_chip_send_op([(q_gate, l_gate, h_ff, r_ff)])])
    dproj, d_g_attn, d_g_conv, d_sinks, d_conv8, l_up, x_out = _mixer_bwd(
        dm, proj, pos, invf, sinks, g_attn, g_conv, conv_w8, dmixed, attn, lse, y_conv,
        comm=[_chip_send_op([(q_up, None, 0, r_ff)]), _exchange_op([dw_out])])
    q_out = _pair_sum("chip_sum_w_out", dw_out, x_out)
    dw_in_t, l_out = weight_grad("dw_in", dproj, x_c, comm=[_chip_send_op([(q_out, None, 0, r_out)])])
    (x_in,) = _comm_kernel("exchange_w_in", [_exchange_op([dw_in_t])])
    q_in = _pair_sum("chip_sum_w_in", dw_in_t, x_in)

    def dx_epilogue(accs, ex, out, first):
        out[0][...] = DEEPNORM_ALPHA * ex[0][...] + accs[0]

    grad_x, l_in = _matmul("dx", [[(dproj, w_in_t, "nn")]], s, d, inw, tr, d, tk_in,
                           [(dr1, (tr, d), _row_i)], [((s, d), _F32, (tr, d), _row_i)], dx_epilogue,
                           comm=[_chip_send_op([(q_in, None, 0, r_in)])])

    small_parts = [d_conv8[:3], d_sinks, d_g_attn, d_g_conv, d_ln1_g, d_ln1_b, d_ln2_g, d_ln2_b]
    packed, spans = _pack(small_parts)
    reduced = _unpack(_all_reduce_small("reduce_small", packed), spans, [p.shape for p in small_parts])
    g_conv_full, g_sinks, g_g_attn, g_g_conv, g_ln1_g, g_ln1_b, g_ln2_g, g_ln2_b = reduced
    me = _linear(*_position())
    g_conv_w = lax.dynamic_slice(g_conv_full, (0, me * conv_cols), (3, conv_cols))
    loss = lax.psum(loss_acc[0, 0], ("x", "y", "c"))

    big = {"w_in": (w_in[0].T, l_in, m_w_in[0].T, v_w_in[0].T), "w_out": (w_out[0], l_out, m_w_out[0], v_w_out[0]),
           "w_gate": (w_gate[0].T, l_gate, m_w_gate[0].T, v_w_gate[0].T),
           "w_up": (w_up[0].T, l_up, m_w_up[0].T, v_w_up[0].T), "w_down": (w_down[0], l_down, m_w_down[0], v_w_down[0])}
    res = {nm: tuple(_adamw(f"adamw_{nm}", w, slots, m, v)) for nm, (w, slots, m, v) in big.items()}
    for nm in ("w_in", "w_gate", "w_up"):
        res[nm] = tuple(a.T for a in res[nm])
    small_names = ["conv_w", "sinks", "g_attn", "g_conv", "ln1_g", "ln1_b", "ln2_g", "ln2_b"]
    small_w = [conv_w, sinks, g_attn, g_conv, ln1_g, ln1_b, ln2_g, ln2_b]
    small_g = [g_conv_w[None], g_sinks, g_g_attn, g_g_conv, g_ln1_g, g_ln1_b, g_ln2_g, g_ln2_b]
    small_m = [m_conv_w, m_sinks, m_g_attn, m_g_conv, m_ln1_g, m_ln1_b, m_ln2_g, m_ln2_b]
    small_v = [v_conv_w, v_sinks, v_g_attn, v_g_conv, v_ln1_g, v_ln1_b, v_ln2_g, v_ln2_b]
    pw, sp = _pack(small_w)
    pg, _ = _pack(small_g)
    pm, _ = _pack(small_m)
    pv, _ = _pack(small_v)
    shapes = [w.shape for w in small_w]
    _, sd, sm, sv = [_unpack(p, sp, shapes) for p in _adamw("adamw_small", pw, pg[None], pm, pv)]
    for i, nm in enumerate(small_names):
        res[nm] = (small_g[i].reshape(shapes[i]), sd[i], sm[i], sv[i])

    order = ["w_in", "conv_w", "sinks", "g_attn", "g_conv", "w_out", "ln1_g", "ln1_b", "w_gate", "w_up", "w_down", "ln2_g", "ln2_b"]

    def lead(a, nm):
        return a[None] if nm in big else a

    return (loss, grad_x[None],
            *[lead(res[nm][0], nm) for nm in order], *[lead(res[nm][1], nm) for nm in order],
            *[lead(res[nm][2], nm) for nm in order], *[lead(res[nm][3], nm) for nm in order])
```

```python
import functools

import jax
import jax.numpy as jnp
from jax import lax
from jax.experimental import pallas as pl
from jax.experimental.pallas import tpu as pltpu

_F32 = jnp.float32
_CDT = jnp.bfloat16

HEAD_DIM = 64
WINDOW = 128
N_KV_HEADS = 4
KV_WIDTH = N_KV_HEADS * HEAD_DIM
ROT_DIM = HEAD_DIM // 4
ROPE_THETA = 500000.0
ATTN_SCALE = HEAD_DIM ** -0.5
DEPTH = 1
DEEPNORM_ALPHA = (2 * DEPTH) ** 0.25
LN_EPS = 1e-5
RMS_EPS = 1e-6
ADAM_LR = 0.001
ADAM_B1 = 0.9
ADAM_B2 = 0.999
ADAM_EPS = 1e-08
ADAM_WD = 0.01
ADAM_STEP = 10
N_DEV = 8
MASKED = -1e30

V7X_VMEM_BYTES = 64 * 1024 * 1024
V7X_LANES = 128
V7X_SUBLANES = 8
_MESH = pl.DeviceIdType.MESH
_ANY = pl.BlockSpec(memory_space=pl.ANY)


def _vmem_limit(block_bytes, scratch_bytes=0):
    want = 2 * block_bytes + scratch_bytes + 16 * 1024 * 1024
    return int(min(max(want, 32 * 1024 * 1024), V7X_VMEM_BYTES - 8 * 1024 * 1024))


def _nbytes(shape, dtype):
    n = 1
    for s in shape:
        n *= s
    return n * jnp.dtype(dtype).itemsize


def _pick(n, candidates):
    for c in candidates:
        if n % c == 0:
            return c
    raise ValueError(f"no tile of {candidates} divides {n}")


_DOT_DIMS = {"nn": ((1,), (0,)), "nt": ((1,), (1,)), "tn": ((0,), (0,))}


def _dot(a, b, mode):
    return lax.dot_general(a.astype(_CDT), b.astype(_CDT), (_DOT_DIMS[mode], ((), ())),
                           preferred_element_type=_F32)


def _accumulate(ref, val, first):
    @pl.when(first)
    def _():
        ref[...] = val

    @pl.when(jnp.logical_not(first))
    def _():
        ref[...] += val


class _Comm:
    def __init__(self, inputs, outputs, aliases, sems, start, finish):
        self.inputs, self.outputs, self.aliases, self.sems = inputs, outputs, aliases, sems
        self.start, self.finish = start, finish


class _CommArgs:
    def __init__(self, comms, n_in_before, n_out_before):
        self.comms, self.operands, self.out_shape, self.aliases, self.sems, self.at = comms, [], [], {}, [], []
        for cm in comms:
            self.at.append((len(self.operands), len(self.out_shape), len(self.sems)))
            for i_in, i_out in cm.aliases.items():
                self.aliases[n_in_before + len(self.operands) + i_in] = n_out_before + len(self.out_shape) + i_out
            self.operands += cm.inputs
            self.out_shape += cm.outputs
            self.sems += cm.sems

    def _each(self, in_refs, out_refs, sem_refs):
        for cm, (i0, o0, s0) in zip(self.comms, self.at):
            yield cm, (in_refs[i0:i0 + len(cm.inputs)], out_refs[o0:o0 + len(cm.outputs)], sem_refs[s0:s0 + len(cm.sems)])

    def start(self, in_refs, out_refs, sem_refs):
        for cm, refs in self._each(in_refs, out_refs, sem_refs):
            cm.start(*refs)

    def finish(self, in_refs, out_refs, sem_refs):
        for cm, refs in self._each(in_refs, out_refs, sem_refs):
            cm.finish(*refs)


def _matmul(name, groups, m, n, k, tm, tn, tk, extras, outs, epilogue, comm=(), j_outer=False, n_split=1):
    assert m % tm == 0 and n % tn == 0 and k % tk == 0, (name, m, n, k, tm, tn, tk)
    nk = k // tk
    assert n_split == 1 or (nk == 1 and tn % (n_split * V7X_LANES) == 0), (name, n_split)
    terms = [t for g in groups for t in g]
    operands, in_specs, block_bytes = [], [], 0

    def spec(blk, imap):
        return pl.BlockSpec(blk, (lambda g0, g1, kk: imap(g1, g0, kk)) if j_outer else imap)

    for a, b, mode in terms:
        assert a.shape == ((k, m) if mode == "tn" else (m, k)), (name, a.shape, mode)
        assert b.shape == ((n, k) if mode == "nt" else (k, n)), (name, b.shape, mode)
        if mode == "tn":
            a_blk, a_map = (tk, tm), (lambda i, j, kk: (kk, i))
        else:
            a_blk, a_map = (tm, tk), (lambda i, j, kk: (i, kk))
        if mode == "nt":
            b_blk, b_map = (tn, tk), (lambda i, j, kk: (j, kk))
        else:
            b_blk, b_map = (tk, tn), (lambda i, j, kk: (kk, j))
        operands += [a, b]
        in_specs += [spec(a_blk, a_map), spec(b_blk, b_map)]
        block_bytes += _nbytes(a_blk, a.dtype) + _nbytes(b_blk, b.dtype)
    for arr, blk, imap in extras:
        operands.append(arr)
        in_specs.append(spec(blk, lambda i, j, kk, imap=imap: imap(i, j)))
        block_bytes += _nbytes(blk, arr.dtype)
    out_shape, out_specs = [], []
    for shape, dtype, blk, imap in outs:
        out_shape.append(jax.ShapeDtypeStruct(shape, dtype))
        out_specs.append(spec(blk, lambda i, j, kk, imap=imap: imap(i, j)))
        block_bytes += _nbytes(blk, dtype)
    n_terms, n_extra, n_out, n_groups = len(terms), len(extras), len(outs), len(groups)
    scratch = [pltpu.VMEM((tm, tn), _F32) for _ in range(n_groups)] if nk > 1 else []
    ca = _CommArgs(list(comm), len(operands), n_out)
    n_cin, n_cout, n_acc = len(ca.operands), len(ca.out_shape), len(scratch)
    tiles = (m // tm, n // tn)
    grid = (tiles[1], tiles[0], nk) if j_outer else (tiles[0], tiles[1], nk)

    def body(*refs):
        refs = list(refs)
        term_refs = [refs.pop(0) for _ in range(2 * n_terms)]
        extra_refs = [refs.pop(0) for _ in range(n_extra)]
        cin_refs = [refs.pop(0) for _ in range(n_cin)]
        out_refs = [refs.pop(0) for _ in range(n_out)]
        cout_refs = [refs.pop(0) for _ in range(n_cout)]
        acc_refs = [refs.pop(0) for _ in range(n_acc)]
        sem_refs = refs
        g0, g1, kk = pl.program_id(0), pl.program_id(1), pl.program_id(2)
        first = jnp.logical_and(g0 == 0, g1 == 0)
        if comm:
            @pl.when(jnp.logical_and(first, kk == 0))
            def _():
                ca.start(cin_refs, cout_refs, sem_refs)
        def products(cols):
            partial, t = [], 0
            for g in groups:
                s = None
                for _, _, mode in g:
                    b_ref = term_refs[2 * t + 1]
                    b = b_ref[...] if cols is None else (b_ref[cols, :] if mode == "nt" else b_ref[:, cols])
                    d = _dot(term_refs[2 * t][...], b, mode)
                    s = d if s is None else s + d
                    t += 1
                partial.append(s)
            return partial

        if n_split > 1:
            width = tn // n_split
            for c in range(n_split):
                cols = pl.ds(c * width, width)
                view = lambda ref: ref.at[:, cols] if tuple(ref.shape) == (tm, tn) else ref
                epilogue(products(cols), [view(r) for r in extra_refs], [view(r) for r in out_refs], first)
        elif nk == 1:
            epilogue(products(None), extra_refs, out_refs, first)
        else:
            partial = products(None)
            for acc, p in zip(acc_refs, partial):
                _accumulate(acc, p, kk == 0)

            @pl.when(kk == nk - 1)
            def _():
                epilogue([acc[...] for acc in acc_refs], extra_refs, out_refs, first)
        if comm:
            @pl.when(jnp.logical_and(jnp.logical_and(g0 == grid[0] - 1, g1 == grid[1] - 1), kk == nk - 1))
            def _():
                ca.finish(cin_refs, cout_refs, sem_refs)

    res = pl.pallas_call(
        body, name=name, grid=grid,
        in_specs=in_specs + [_ANY] * n_cin, out_specs=out_specs + [_ANY] * n_cout,
        out_shape=out_shape + ca.out_shape, scratch_shapes=scratch + ca.sems, input_output_aliases=ca.aliases,
        compiler_params=pltpu.CompilerParams(
            dimension_semantics=("arbitrary", "arbitrary", "arbitrary"),
            vmem_limit_bytes=_vmem_limit(block_bytes, n_groups * tm * tn * 4 if nk > 1 else 0)),
    )(*operands, *ca.operands)
    return list(res[:n_out]) + list(res[n_out:])


def _store_epilogue(accs, extra_refs, out_refs, first):
    for acc, ref in zip(accs, out_refs):
        ref[...] = acc.astype(ref.dtype)


def _tile_ij(i, j):
    return (i, j)


def _row_i(i, j):
    return (i, 0)


def _whole(i, j):
    return (0, 0)


def _mean(v):
    return jnp.mean(v, axis=-1, keepdims=True)


def _ln_fwd(r, g, b):
    xc = r - _mean(r)
    rstd = lax.rsqrt(_mean(xc * xc) + LN_EPS)
    xhat = xc * rstd
    return xhat * g + b, xhat, rstd


def _ln_bwd(dy, xhat, rstd, g):
    dxh = dy * g
    dr = rstd * (dxh - _mean(dxh) - xhat * _mean(dxh * xhat))
    return dr, jnp.sum(dy * xhat, axis=0, keepdims=True), jnp.sum(dy, axis=0, keepdims=True)


def _rms_fwd(a, g):
    rstd = lax.rsqrt(_mean(a * a) + RMS_EPS)
    return a * rstd * g


def _rms_bwd(dm, a, g):
    rstd = lax.rsqrt(_mean(a * a) + RMS_EPS)
    nhat = a * rstd
    dn = dm * g
    da = rstd * (dn - nhat * _mean(dn * nhat))
    return da, jnp.sum(dm * nhat, axis=0, keepdims=True)


def _lane(shape):
    return lax.broadcasted_iota(jnp.int32, shape, 1)


def _row(shape):
    return lax.broadcasted_iota(jnp.int32, shape, 0)


def _rope_tables(pos, invf):
    ang = pos.astype(_F32) * invf
    lane = _lane(ang.shape)
    in_rot = (lane % HEAD_DIM) < ROT_DIM
    first = (lane % ROT_DIM) < ROT_DIM // 2
    cos = jnp.where(in_rot, jnp.cos(ang), 1.0)
    sin = jnp.sin(ang)
    sgn = jnp.where(in_rot, jnp.where(first, -sin, sin), 0.0)
    return cos, sgn


def _rope(t, cos, sgn, sign):
    half = ROT_DIM // 2
    first = (_lane(t.shape) % ROT_DIM) < half
    partner = jnp.where(first, pltpu.roll(t, V7X_LANES - half, 1), pltpu.roll(t, half, 1))
    return t * cos + partner * (sgn * sign)


def _dup_head(t, h):
    g = t[:, 128 * (h // 2):128 * (h // 2) + 128]
    r = pltpu.roll(g, HEAD_DIM, 1)
    lo = _lane(g.shape) < HEAD_DIM
    return jnp.where(lo, g, r) if h % 2 == 0 else jnp.where(lo, r, g)


def _fold_halves(t):
    return t + pltpu.roll(t, HEAD_DIM, 1)


def _halves(t):
    lo = _lane(t.shape) < HEAD_DIM
    zero = jnp.zeros_like(t)
    return jnp.where(lo, t, zero), jnp.where(lo, zero, t)


def _band_mask(n_heads, n_keys, first_block):
    shape = (n_heads * WINDOW, n_keys)
    i = jnp.bitwise_and(_row(shape), WINDOW - 1)
    j = _lane(shape)
    valid = jnp.logical_and(j >= i + 1, j <= i + WINDOW)
    if first_block is not None:
        valid = jnp.logical_and(valid, jnp.logical_or(j >= WINDOW, jnp.logical_not(first_block)))
    return valid


def _stack_heads(pairs):
    return jnp.concatenate([half for t in pairs for half in _halves(t)], axis=0).astype(_CDT)


def _unstack_heads(t, n_pairs):
    lo = _lane((WINDOW, 128)) < HEAD_DIM
    return [jnp.where(lo, t[2 * WINDOW * i:2 * WINDOW * i + WINDOW], t[2 * WINDOW * i + WINDOW:2 * WINDOW * (i + 1)])
            for i in range(n_pairs)]


def _per_head(values, n_heads):
    block = jnp.right_shift(_row((n_heads * WINDOW, 1)), WINDOW.bit_length() - 1)
    out = jnp.zeros((n_heads * WINDOW, 1), _F32)
    for k, v in enumerate(values):
        out = jnp.where(block == k, v, out)
    return out


def _shift_down(z, halo, k):
    rows = z.shape[0]
    out = pltpu.roll(z, k, 0)
    r = _row(z.shape)
    for t in range(k):
        out = jnp.where(r == t, halo[V7X_SUBLANES - k + t:V7X_SUBLANES - k + t + 1, :], out)
    del rows
    return out


def _shift_up(z, halo, k):
    rows = z.shape[0]
    out = pltpu.roll(z, rows - k, 0)
    r = _row(z.shape)
    for t in range(k):
        out = jnp.where(r == rows - k + t, halo[t:t + 1, :], out)
    return out


class _Dims:
    def __init__(self, s, d, d_ff):
        self.s, self.d, self.d_ff = s, d, d_ff
        self.aw = d // 2
        self.cw = d - self.aw
        self.nq = self.aw // HEAD_DIM
        self.group = self.nq // N_KV_HEADS
        assert self.group % 2 == 0, "a 128-lane pair of query heads must share its kv head"
        self.inw = self.aw + 2 * KV_WIDTH + 3 * self.cw
        self.o_k = self.aw
        self.o_v = self.aw + KV_WIDTH
        self.o_cg = self.aw + 2 * KV_WIDTH
        self.o_bg = self.o_cg + self.cw
        self.o_u = self.o_bg + self.cw
        self.nb = s // WINDOW
        assert s % WINDOW == 0


def _carrying(body, n_in, n_out, n_steps, ca):
    n_cin, n_cout = len(ca.operands), len(ca.out_shape)

    def wrapped(*refs):
        refs = list(refs)
        in_refs = [refs.pop(0) for _ in range(n_in)]
        cin_refs = [refs.pop(0) for _ in range(n_cin)]
        out_refs = [refs.pop(0) for _ in range(n_out)]
        cout_refs = [refs.pop(0) for _ in range(n_cout)]
        if ca.comms:
            @pl.when(pl.program_id(0) == 0)
            def _():
                ca.start(cin_refs, cout_refs, refs)
        body(*in_refs, *out_refs)
        if ca.comms:
            @pl.when(pl.program_id(0) == n_steps - 1)
            def _():
                ca.finish(cin_refs, cout_refs, refs)

    return wrapped


def _row_kernel(name, body, rows_in, vecs_in, rows_out, vecs_out, comm=()):
    s = rows_in[0].shape[0]
    tr = _pick(s, (256, 128))
    row = lambda a: pl.BlockSpec((tr, a[1] if isinstance(a, tuple) else a.shape[1]), lambda i: (i, 0))
    vec = lambda shape: pl.BlockSpec(tuple(shape), lambda i: (0, 0))
    n_in, n_out = len(rows_in) + len(vecs_in), len(rows_out) + len(vecs_out)
    ca = _CommArgs(list(comm), n_in, n_out)
    blocks = sum(_nbytes((tr, a.shape[1]), a.dtype) for a in rows_in) + sum(_nbytes((tr, sh[1]), dt) for sh, dt in rows_out)
    res = pl.pallas_call(
        _carrying(body, n_in, n_out, s // tr, ca), name=name, grid=(s // tr,),
        in_specs=[row(a) for a in rows_in] + [vec(v.shape) for v in vecs_in] + [_ANY] * len(ca.operands),
        out_specs=[row(sh) for sh, _ in rows_out] + [vec(sh) for sh, _ in vecs_out] + [_ANY] * len(ca.out_shape),
        out_shape=[jax.ShapeDtypeStruct(sh, dt) for sh, dt in list(rows_out) + list(vecs_out)] + ca.out_shape,
        scratch_shapes=ca.sems, input_output_aliases=ca.aliases,
        compiler_params=pltpu.CompilerParams(dimension_semantics=("arbitrary",), vmem_limit_bytes=_vmem_limit(blocks)),
    )(*rows_in, *vecs_in, *ca.operands)
    return list(res)


def _ln2_loss_bwd(r2, target, gain, bias, comm=()):
    s, d = r2.shape

    def body(r_ref, t_ref, g_ref, b_ref, dr_ref, drc_ref, loss_ref, dg_ref, db_ref):
        first = pl.program_id(0) == 0
        yv, xhat, rstd = _ln_fwd(r_ref[...], g_ref[...], b_ref[...])
        err = yv - t_ref[...]
        dr2, dg, db = _ln_bwd(err * (1.0 / d), xhat, rstd, g_ref[...])
        dr_ref[...] = dr2
        drc_ref[...] = dr2.astype(_CDT)
        _accumulate(loss_ref, jnp.zeros(loss_ref.shape, _F32) + 0.5 * jnp.sum(err * err) * (1.0 / d), first)
        _accumulate(dg_ref, dg, first)
        _accumulate(db_ref, db, first)

    return _row_kernel("ln2_loss_bwd", body, [r2, target], [gain, bias], [((s, d), _F32), ((s, d), _CDT)],
                       [((V7X_SUBLANES, V7X_LANES), _F32), ((1, d), _F32), ((1, d), _F32)], comm)


def _ln1_bwd_rows(dh1, xhat, rstd, gain, comm=()):
    s, d = dh1.shape

    def body(dh_ref, xhat_ref, rstd_ref, g_ref, dr_ref, drc_ref, dg_ref, db_ref):
        first = pl.program_id(0) == 0
        dr1, dg, db = _ln_bwd(dh_ref[...], xhat_ref[...], rstd_ref[...], g_ref[...])
        dr_ref[...] = dr1
        drc_ref[...] = dr1.astype(_CDT)
        _accumulate(dg_ref, dg, first)
        _accumulate(db_ref, db, first)

    return _row_kernel("ln1_bwd", body, [dh1, xhat, rstd], [gain], [((s, d), _F32), ((s, d), _CDT)],
                       [((1, d), _F32), ((1, d), _F32)], comm)


def _mixer_fwd(dm, proj, pos, invf, sinks, g_attn, g_conv, conv_w8, comm=()):
    s, d, aw, cw, nq, inw, nb = dm.s, dm.d, dm.aw, dm.cw, dm.nq, dm.inw, dm.nb

    def body(pp_ref, pc_ref, posp_ref, posc_ref, invf_ref, sinks_ref, ga_ref, gc_ref, cw_ref,
             mixed_ref, attn_ref, lse_ref, y_ref):
        n = pl.program_id(0)
        cos_c, sgn_c = _rope_tables(posc_ref[...], invf_ref[...])
        cos_p, sgn_p = _rope_tables(posp_ref[...], invf_ref[...])
        kk = jnp.concatenate(
            [jnp.concatenate([_rope(ref[:, dm.o_k + 128 * g:dm.o_k + 128 * g + 128], c, sg, 1.0)
                              for g in range(KV_WIDTH // 128)], axis=1)
             for ref, c, sg in ((pp_ref, cos_p, sgn_p), (pc_ref, cos_c, sgn_c))], axis=0)
        vv = jnp.concatenate([pp_ref[:, dm.o_v:dm.o_v + KV_WIDTH], pc_ref[:, dm.o_v:dm.o_v + KV_WIDTH]], axis=0)
        group, pairs = dm.group, dm.group // 2
        valid = _band_mask(group, 2 * WINDOW, n == 0)
        for h in range(N_KV_HEADS):
            k2, v2 = _dup_head(kk, h).astype(_CDT), _dup_head(vv, h).astype(_CDT)
            q4 = _stack_heads([_rope(pc_ref[:, 128 * j:128 * j + 128], cos_c, sgn_c, 1.0)
                               for j in range(pairs * h, pairs * (h + 1))])
            sc = jnp.where(valid, _dot(q4, k2, "nt") * ATTN_SCALE, MASKED)
            sink = _per_head([sinks_ref[0, group * h + r] for r in range(group)], group)
            mx = jnp.maximum(jnp.max(sc, axis=1, keepdims=True), sink)
            p = jnp.exp(sc - mx)
            den = jnp.sum(p, axis=1, keepdims=True) + jnp.exp(sink - mx)
            out = _unstack_heads(_dot(p / den, v2, "nn"), pairs)
            lse = mx + jnp.log(den)
            for r in range(group):
                lse_ref[:, group * h + r:group * h + r + 1] = lse[WINDOW * r:WINDOW * (r + 1)]
            for i in range(pairs):
                j = pairs * h + i
                attn_ref[:, 128 * j:128 * j + 128] = out[i]
        mixed_ref[:, 0:aw] = _rms_fwd(attn_ref[...], ga_ref[...]).astype(mixed_ref.dtype)

        z = pc_ref[:, dm.o_cg:dm.o_cg + cw] * pc_ref[:, dm.o_u:dm.o_u + cw]
        top = WINDOW - V7X_SUBLANES
        halo = pp_ref[top:WINDOW, dm.o_cg:dm.o_cg + cw] * pp_ref[top:WINDOW, dm.o_u:dm.o_u + cw]
        halo = jnp.where(n == 0, jnp.zeros_like(halo), halo)
        y = cw_ref[0:1, :] * _shift_down(z, halo, 2) + cw_ref[1:2, :] * _shift_down(z, halo, 1) + cw_ref[2:3, :] * z
        y_ref[...] = y
        conv = pc_ref[:, dm.o_bg:dm.o_bg + cw] * y
        mixed_ref[:, aw:d] = _rms_fwd(conv, gc_ref[...]).astype(mixed_ref.dtype)

    prev = lambda n: (jnp.maximum(n - 1, 0), 0)
    cur = lambda n: (n, 0)
    fixed = lambda n: (0, 0)
    blocks = 2 * WINDOW * inw * 4 + WINDOW * (d * 2 + aw * 4 + cw * 4 + nq * 4)
    ca = _CommArgs(list(comm), 9, 4)
    return pl.pallas_call(
        _carrying(body, 9, 4, nb, ca), name="mixer_fwd", grid=(nb,),
        in_specs=[pl.BlockSpec((WINDOW, inw), prev), pl.BlockSpec((WINDOW, inw), cur),
                  pl.BlockSpec((WINDOW, 1), prev), pl.BlockSpec((WINDOW, 1), cur),
                  pl.BlockSpec((1, V7X_LANES), fixed), pl.BlockSpec(memory_space=pltpu.SMEM),
                  pl.BlockSpec((1, aw), fixed), pl.BlockSpec((1, cw), fixed), pl.BlockSpec((V7X_SUBLANES, cw), fixed)]
        + [_ANY] * len(ca.operands),
        out_specs=[pl.BlockSpec((WINDOW, d), cur), pl.BlockSpec((WINDOW, aw), cur),
                   pl.BlockSpec((WINDOW, nq), cur), pl.BlockSpec((WINDOW, cw), cur)] + [_ANY] * len(ca.out_shape),
        out_shape=[jax.ShapeDtypeStruct((s, d), _CDT), jax.ShapeDtypeStruct((s, aw), _F32),
                   jax.ShapeDtypeStruct((s, nq), _F32), jax.ShapeDtypeStruct((s, cw), _F32)] + ca.out_shape,
        scratch_shapes=ca.sems, input_output_aliases=ca.aliases,
        compiler_params=pltpu.CompilerParams(dimension_semantics=("arbitrary",), vmem_limit_bytes=_vmem_limit(blocks)),
    )(proj, proj, pos, pos, invf, sinks, g_attn, g_conv, conv_w8, *ca.operands)


def _mixer_bwd(dm, proj, pos, invf, sinks, g_attn, g_conv, conv_w8, dmixed, attn, lse, y, comm=()):
    s, d, aw, cw, nq, inw, nb = dm.s, dm.d, dm.aw, dm.cw, dm.nq, dm.inw, dm.nb

    def body(pp_ref, pc_ref, pn_ref, posp_ref, posc_ref, posn_ref, dmc_ref, dmn_ref, ac_ref, an_ref,
             lsec_ref, lsen_ref, yc_ref, yn_ref, invf_ref, sinks_ref, ga_ref, gc_ref, cw_ref,
             dproj_ref, dga_ref, dgc_ref, dsinks_ref, dcw_ref):
        n = pl.program_id(0)
        first = n == 0
        has_next = n < nb - 1
        cos_p, sgn_p = _rope_tables(posp_ref[...], invf_ref[...])
        cos_c, sgn_c = _rope_tables(posc_ref[...], invf_ref[...])
        cos_n, sgn_n = _rope_tables(posn_ref[...], invf_ref[...])

        da_c, dga = _rms_bwd(dmc_ref[:, 0:aw], ac_ref[...], ga_ref[...])
        da_n, _ = _rms_bwd(dmn_ref[:, 0:aw], an_ref[...], ga_ref[...])
        _accumulate(dga_ref, dga, first)
        kk = jnp.concatenate(
            [jnp.concatenate([_rope(ref[:, dm.o_k + 128 * g:dm.o_k + 128 * g + 128], c, sg, 1.0)
                              for g in range(KV_WIDTH // 128)], axis=1)
             for ref, c, sg in ((pp_ref, cos_p, sgn_p), (pc_ref, cos_c, sgn_c))], axis=0)
        vv = jnp.concatenate([pp_ref[:, dm.o_v:dm.o_v + KV_WIDTH], pc_ref[:, dm.o_v:dm.o_v + KV_WIDTH]], axis=0)
        group, pairs = dm.group, dm.group // 2
        valid_c = _band_mask(group, 2 * WINDOW, first)
        valid_n = jnp.logical_and(_band_mask(group, WINDOW, None), has_next)
        dk2, dv2 = [], []
        dsinks = jnp.zeros((1, nq), _F32)
        head_lane = _lane((1, nq))
        block = jnp.right_shift(_row((group * WINDOW, 1)), WINDOW.bit_length() - 1)

        def stacked(q_ref, cos, sgn, da, o_ref, lse_ref_, h):
            cols = [slice(128 * j, 128 * j + 128) for j in range(pairs * h, pairs * (h + 1))]
            q4 = _stack_heads([_rope(q_ref[:, c], cos, sgn, 1.0) for c in cols])
            do4 = _stack_heads([da[:, c] for c in cols])
            lo = _lane((WINDOW, 128)) < HEAD_DIM
            deltas = []
            for c in cols:
                prod = o_ref[:, c] * da[:, c]
                deltas += [jnp.sum(jnp.where(lo, prod, 0.0), axis=1, keepdims=True),
                           jnp.sum(jnp.where(lo, 0.0, prod), axis=1, keepdims=True)]
            lse4 = jnp.concatenate([lse_ref_[:, group * h + r:group * h + r + 1] for r in range(group)], axis=0)
            return q4, do4, lse4, jnp.concatenate(deltas, axis=0)

        def scores_bwd(q4, do4, lse4, delta4, keys, vals, valid):
            sc = _dot(q4, keys, "nt") * ATTN_SCALE
            p = jnp.exp(jnp.where(valid, sc - lse4, MASKED))
            return p.astype(_CDT), (p * (_dot(do4, vals, "nt") - delta4) * ATTN_SCALE).astype(_CDT)

        for h in range(N_KV_HEADS):
            k2, v2 = _dup_head(kk, h).astype(_CDT), _dup_head(vv, h).astype(_CDT)
            q4, do4, lse4, delta4 = stacked(pc_ref, cos_c, sgn_c, da_c, ac_ref, lsec_ref, h)
            p, ds = scores_bwd(q4, do4, lse4, delta4, k2, v2, valid_c)
            for i, dq in enumerate(_unstack_heads(_dot(ds, k2, "nn"), pairs)):
                j = pairs * h + i
                dproj_ref[:, 128 * j:128 * j + 128] = _rope(dq, cos_c, sgn_c, -1.0).astype(dproj_ref.dtype)
            dk = _dot(ds, q4, "tn")[WINDOW:2 * WINDOW, :]
            dv = _dot(p, do4, "tn")[WINDOW:2 * WINDOW, :]
            sink4 = _per_head([sinks_ref[0, group * h + r] for r in range(group)], group)
            loss_sink = jnp.exp(sink4 - lse4) * delta4
            for r in range(group):
                dsinks = dsinks + jnp.where(head_lane == group * h + r,
                                            -jnp.sum(jnp.where(block == r, loss_sink, 0.0)), 0.0)
            q4, do4, lse4, delta4 = stacked(pn_ref, cos_n, sgn_n, da_n, an_ref, lsen_ref, h)
            p, ds = scores_bwd(q4, do4, lse4, delta4, k2[WINDOW:2 * WINDOW, :], v2[WINDOW:2 * WINDOW, :], valid_n)
            dk2.append(dk + _dot(ds, q4, "tn"))
            dv2.append(dv + _dot(p, do4, "tn"))
        _accumulate(dsinks_ref, dsinks, first)
        lo = _lane((WINDOW, 128)) < HEAD_DIM
        for g in range(KV_WIDTH // 128):
            dk = jnp.where(lo, _fold_halves(dk2[2 * g]), _fold_halves(dk2[2 * g + 1]))
            dv = jnp.where(lo, _fold_halves(dv2[2 * g]), _fold_halves(dv2[2 * g + 1]))
            dproj_ref[:, dm.o_k + 128 * g:dm.o_k + 128 * g + 128] = _rope(dk, cos_c, sgn_c, -1.0).astype(dproj_ref.dtype)
            dproj_ref[:, dm.o_v + 128 * g:dm.o_v + 128 * g + 128] = dv.astype(dproj_ref.dtype)

        bg = pc_ref[:, dm.o_bg:dm.o_bg + cw]
        yc = yc_ref[...]
        dconv, dgc = _rms_bwd(dmc_ref[:, aw:d], bg * yc, gc_ref[...])
        _accumulate(dgc_ref, dgc, first)
        dproj_ref[:, dm.o_bg:dm.o_bg + cw] = (dconv * yc).astype(dproj_ref.dtype)
        dy = dconv * bg
        bg_n = pn_ref[0:V7X_SUBLANES, dm.o_bg:dm.o_bg + cw]
        dconv_n, _ = _rms_bwd(dmn_ref[0:V7X_SUBLANES, aw:d], bg_n * yn_ref[...], gc_ref[...])
        halo = jnp.where(has_next, dconv_n * bg_n, 0.0)
        dy1 = _shift_up(dy, halo, 1)
        dy2 = _shift_up(dy, halo, 2)
        dz = cw_ref[2:3, :] * dy + cw_ref[1:2, :] * dy1 + cw_ref[0:1, :] * dy2
        cg = pc_ref[:, dm.o_cg:dm.o_cg + cw]
        u = pc_ref[:, dm.o_u:dm.o_u + cw]
        dproj_ref[:, dm.o_cg:dm.o_cg + cw] = (dz * u).astype(dproj_ref.dtype)
        dproj_ref[:, dm.o_u:dm.o_u + cw] = (dz * cg).astype(dproj_ref.dtype)
        z = cg * u
        dcw = jnp.concatenate(
            [jnp.sum(z * t, axis=0, keepdims=True) for t in (dy2, dy1, dy)]
            + [jnp.zeros((V7X_SUBLANES - 3, cw), _F32)], axis=0)
        _accumulate(dcw_ref, dcw, first)

    prev = lambda n: (jnp.maximum(n - 1, 0), 0)
    cur = lambda n: (n, 0)
    nxt = lambda n: (jnp.minimum(n + 1, nb - 1), 0)
    nxt8 = lambda n: (jnp.minimum((n + 1) * (WINDOW // V7X_SUBLANES), s // V7X_SUBLANES - 1), 0)
    fixed = lambda n: (0, 0)
    blocks = WINDOW * (3 * inw * 4 + 2 * d * 4 + 2 * aw * 4 + cw * 4 + inw * 2)
    ca = _CommArgs(list(comm), 19, 5)
    return pl.pallas_call(
        _carrying(body, 19, 5, nb, ca), name="mixer_bwd", grid=(nb,),
        in_specs=[pl.BlockSpec((WINDOW, inw), prev), pl.BlockSpec((WINDOW, inw), cur), pl.BlockSpec((WINDOW, inw), nxt),
                  pl.BlockSpec((WINDOW, 1), prev), pl.BlockSpec((WINDOW, 1), cur), pl.BlockSpec((WINDOW, 1), nxt),
                  pl.BlockSpec((WINDOW, d), cur), pl.BlockSpec((WINDOW, d), nxt),
                  pl.BlockSpec((WINDOW, aw), cur), pl.BlockSpec((WINDOW, aw), nxt),
                  pl.BlockSpec((WINDOW, nq), cur), pl.BlockSpec((WINDOW, nq), nxt),
                  pl.BlockSpec((WINDOW, cw), cur), pl.BlockSpec((V7X_SUBLANES, cw), nxt8),
                  pl.BlockSpec((1, V7X_LANES), fixed), pl.BlockSpec(memory_space=pltpu.SMEM),
                  pl.BlockSpec((1, aw), fixed), pl.BlockSpec((1, cw), fixed), pl.BlockSpec((V7X_SUBLANES, cw), fixed)]
        + [_ANY] * len(ca.operands),
        out_specs=[pl.BlockSpec((WINDOW, inw), cur), pl.BlockSpec((1, aw), fixed), pl.BlockSpec((1, cw), fixed),
                   pl.BlockSpec((1, nq), fixed), pl.BlockSpec((V7X_SUBLANES, cw), fixed)] + [_ANY] * len(ca.out_shape),
        out_shape=[jax.ShapeDtypeStruct((s, inw), _CDT), jax.ShapeDtypeStruct((1, aw), _F32),
                   jax.ShapeDtypeStruct((1, cw), _F32), jax.ShapeDtypeStruct((1, nq), _F32),
                   jax.ShapeDtypeStruct((V7X_SUBLANES, cw), _F32)] + ca.out_shape,
        scratch_shapes=ca.sems, input_output_aliases=ca.aliases,
        compiler_params=pltpu.CompilerParams(dimension_semantics=("arbitrary",), vmem_limit_bytes=_vmem_limit(blocks)),
    )(proj, proj, proj, pos, pos, pos, dmixed, dmixed, attn, attn, lse, lse, y, y, invf, sinks, g_attn, g_conv, conv_w8,
      *ca.operands)


def _position():
    return lax.axis_index("x"), lax.axis_index("y"), lax.axis_index("c")


def _linear(px, py, pc):
    return 4 * px + 2 * py + pc


def _comm_kernel(name, comm):
    ca = _CommArgs(list(comm), 0, 0)
    n_cin, n_cout = len(ca.operands), len(ca.out_shape)

    def body(*refs):
        cin, cout, sems = refs[:n_cin], refs[n_cin:n_cin + n_cout], refs[n_cin + n_cout:]
        ca.start(cin, cout, sems)
        ca.finish(cin, cout, sems)

    return pl.pallas_call(
        body, name=name, out_shape=ca.out_shape, in_specs=[_ANY] * n_cin, out_specs=[_ANY] * n_cout,
        scratch_shapes=ca.sems, input_output_aliases=ca.aliases,
    )(*ca.operands)


def _gather_op(units):
    n = len(units)
    inputs, outputs, aliases = [], [], {}
    for shard, _, _, _ in units:
        inputs.append(shard)
        outputs.append(jax.ShapeDtypeStruct((N_DEV * shard.shape[0], shard.shape[1]), shard.dtype))
    for u, (_, buf, _, _) in enumerate(units):
        if buf is not None:
            aliases[len(inputs)] = u
            inputs.append(buf)

    def plan(ins, outs, sems):
        send_sems, recv_sems, local_sems = sems
        x, y, c = _position()
        me, sibling = (x, y, c), (x, y, 1 - c)
        chips = [(1 - x, y), (x, 1 - y), (1 - x, 1 - y)]

        def rows(u, px, py, pc):
            shard, _, r0, r1 = units[u]
            return outs[u].at[pl.ds(pl.multiple_of(_linear(px, py, pc) * shard.shape[0] + r0, 16), r1 - r0), :]

        def own(u):
            _, _, r0, r1 = units[u]
            return ins[u].at[pl.ds(r0, r1 - r0), :]

        def copy(u, k, block, to, src=None):
            return pltpu.make_async_remote_copy(
                src_ref=rows(u, *block) if src is None else src, dst_ref=rows(u, *block),
                send_sem=send_sems.at[u, k], recv_sem=recv_sems.at[u, k], device_id=to, device_id_type=_MESH)

        mine = [pltpu.make_async_copy(own(u), rows(u, *me), local_sems.at[u]) for u in range(n)]
        first = []
        for u in range(n):
            first.append(copy(u, 0, me, sibling, src=own(u)))
            first += [copy(u, 1 + j, me, (*chip, c), src=own(u)) for j, chip in enumerate(chips)]
        passed = [[copy(u, 4 + j, (*chip, c), sibling) for j, chip in enumerate(chips)] for u in range(n)]
        landed = [[copy(u, 1 + j, (*chip, c), me) for j, chip in enumerate(chips)] for u in range(n)]
        rest = [[copy(u, 0, sibling, me)] + [copy(u, 4 + j, (*chip, 1 - c), me) for j, chip in enumerate(chips)]
                for u in range(n)]
        return mine, first, passed, landed, rest

    def start(ins, outs, sems):
        mine, first, _, _, _ = plan(ins, outs, sems)
        for cp in mine + first:
            cp.start()

    def finish(ins, outs, sems):
        mine, first, passed, landed, rest = plan(ins, outs, sems)
        for u in range(n):
            for arrived, onward in zip(landed[u], passed[u]):
                arrived.wait_recv()
                onward.start()
        for u in range(n):
            for cp in rest[u]:
                cp.wait_recv()
        for cp in first + [cp for row in passed for cp in row]:
            cp.wait_send()
        for cp in mine:
            cp.wait()

    sems = [pltpu.SemaphoreType.DMA((n, 7)), pltpu.SemaphoreType.DMA((n, 7)), pltpu.SemaphoreType.DMA((n,))]
    return _Comm(inputs, outputs, aliases, sems, start, finish)


def _peers(x, y, c):
    out = []
    for k in range(1, N_DEV):
        fx, fy, fc = (k >> 2) & 1, (k >> 1) & 1, k & 1
        out.append((1 - x if fx else x, 1 - y if fy else y, 1 - c if fc else c))
    return out


def _exchange_op(partials):
    n = len(partials)
    outputs = [jax.ShapeDtypeStruct((4, p.shape[0] // N_DEV, p.shape[1]), p.dtype) for p in partials]

    def plan(ins, outs, sems):
        send_sems, recv_sems = sems
        x, y, c = _position()
        out = []
        for a in range(n):
            r = outs[a].shape[1]
            for ch in range(4):
                out.append(pltpu.make_async_remote_copy(
                    src_ref=ins[a].at[pl.ds(pl.multiple_of((2 * ch + 1 - c) * r, 16), r), :], dst_ref=outs[a].at[ch],
                    send_sem=send_sems.at[a, ch], recv_sem=recv_sems.at[a, ch], device_id=(x, y, 1 - c),
                    device_id_type=_MESH))
        return out

    def start(ins, outs, sems):
        for cp in plan(ins, outs, sems):
            cp.start()

    def finish(ins, outs, sems):
        copies = plan(ins, outs, sems)
        for cp in copies:
            cp.wait_recv()
        for cp in copies:
            cp.wait_send()

    sems = [pltpu.SemaphoreType.DMA((n, 4)), pltpu.SemaphoreType.DMA((n, 4))]
    return _Comm(list(partials), outputs, {}, sems, start, finish)


def _chip_send_op(units):
    n = len(units)
    inputs, outputs, aliases = [], [], {}
    for q, _, _, _ in units:
        inputs.append(q)
        outputs.append(jax.ShapeDtypeStruct(q.shape, q.dtype))
    for u, (_, buf, _, _) in enumerate(units):
        if buf is not None:
            aliases[len(inputs)] = u
            inputs.append(buf)

    def plan(ins, outs, sems):
        send_sems, recv_sems, local_sems = sems
        x, y, c = _position()
        my_chip = 2 * x + y
        chips = [(1 - x, y), (x, 1 - y), (1 - x, 1 - y)]
        mine, sends, arrivals = [], [], []
        for u, (_, _, r0, r1) in enumerate(units):
            span = pl.ds(r0, r1 - r0)
            mine.append(pltpu.make_async_copy(ins[u].at[my_chip, span, :], outs[u].at[my_chip, span, :], local_sems.at[u]))
            for k, (px, py) in enumerate(chips):
                sends.append(pltpu.make_async_remote_copy(
                    src_ref=ins[u].at[2 * px + py, span, :], dst_ref=outs[u].at[my_chip, span, :],
                    send_sem=send_sems.at[u, k], recv_sem=recv_sems.at[u, k], device_id=(px, py, c), device_id_type=_MESH))
                arrivals.append(pltpu.make_async_remote_copy(
                    src_ref=ins[u].at[my_chip, span, :], dst_ref=outs[u].at[2 * px + py, span, :],
                    send_sem=send_sems.at[u, k], recv_sem=recv_sems.at[u, k], device_id=(px, py, c), device_id_type=_MESH))
        return mine, sends, arrivals

    def start(ins, outs, sems):
        mine, sends, _ = plan(ins, outs, sems)
        for cp in mine + sends:
            cp.start()

    def finish(ins, outs, sems):
        mine, sends, arrivals = plan(ins, outs, sems)
        for cp in arrivals:
            cp.wait_recv()
        for cp in sends:
            cp.wait_send()
        for cp in mine:
            cp.wait()

    sems = [pltpu.SemaphoreType.DMA((n, 3)), pltpu.SemaphoreType.DMA((n, 3)), pltpu.SemaphoreType.DMA((n,))]
    return _Comm(inputs, outputs, aliases, sems, start, finish)


def _pair_sum(name, partial, received):
    _, rows, cols = received.shape
    tr = _pick(rows, (352, 288, 256, 128, 64, 32, 16))
    p4 = partial.reshape(4, 2, rows, cols)
    kind = jnp.reshape(lax.axis_index("c"), (1,)).astype(jnp.int32)

    def body(kind_ref, p_ref, r_ref, o_ref):
        o_ref[0] = (p_ref[0, 0].astype(_F32) + r_ref[0].astype(_F32)).astype(o_ref.dtype)

    return pl.pallas_call(
        body, name=name,
        grid_spec=pltpu.PrefetchScalarGridSpec(
            num_scalar_prefetch=1, grid=(4, rows // tr),
            in_specs=[pl.BlockSpec((1, 1, tr, cols), lambda ch, i, kind_ref: (ch, kind_ref[0], i, 0)),
                      pl.BlockSpec((1, tr, cols), lambda ch, i, kind_ref: (ch, i, 0))],
            out_specs=pl.BlockSpec((1, tr, cols), lambda ch, i, kind_ref: (ch, i, 0))),
        out_shape=jax.ShapeDtypeStruct(received.shape, received.dtype),
        compiler_params=pltpu.CompilerParams(dimension_semantics=("arbitrary", "arbitrary")),
    )(kind, p4, received)


def _all_reduce_small(name, v):
    rows = v.shape[0]

    def body(v_ref, out_ref, land_ref, send_sems, recv_sems):
        x, y, c = _position()
        me = _linear(x, y, c)
        peers = _peers(x, y, c)
        land_ref[me] = v_ref[...]
        sends = [pltpu.make_async_remote_copy(
            src_ref=v_ref, dst_ref=land_ref.at[me], send_sem=send_sems.at[k], recv_sem=recv_sems.at[k],
            device_id=peer, device_id_type=_MESH) for k, peer in enumerate(peers)]
        for cp in sends:
            cp.start()
        for k, peer in enumerate(peers):
            pltpu.make_async_remote_copy(
                src_ref=v_ref, dst_ref=land_ref.at[_linear(*peer)], send_sem=send_sems.at[k], recv_sem=recv_sems.at[k],
                device_id=peer, device_id_type=_MESH).wait_recv()
        for cp in sends:
            cp.wait_send()
        total = land_ref[0]
        for s in range(1, N_DEV):
            total = total + land_ref[s]
        out_ref[...] = total

    return pl.pallas_call(
        body, name=name, out_shape=jax.ShapeDtypeStruct(v.shape, _F32),
        in_specs=[pl.BlockSpec(memory_space=pltpu.VMEM)], out_specs=pl.BlockSpec(memory_space=pltpu.VMEM),
        scratch_shapes=[pltpu.VMEM((N_DEV, rows, V7X_LANES), _F32), pltpu.SemaphoreType.DMA((7,)), pltpu.SemaphoreType.DMA((7,))],
    )(v)


def _adamw(name, w, slots, m, v):
    rows, cols = w.shape
    n_slots = slots.shape[0]
    tr = _pick(rows, (176, 144, 128, 64, 32, 16, 8))

    def body(w_ref, s_ref, m_ref, v_ref, g_ref, d_ref, nm_ref, nv_ref):
        g = s_ref[0].astype(_F32)
        for k in range(1, n_slots):
            g = g + s_ref[k].astype(_F32)
        nm = ADAM_B1 * m_ref[...] + (1.0 - ADAM_B1) * g
        nv = ADAM_B2 * v_ref[...] + (1.0 - ADAM_B2) * (g * g)
        m_hat = nm / (1.0 - ADAM_B1 ** ADAM_STEP)
        v_hat = nv / (1.0 - ADAM_B2 ** ADAM_STEP)
        g_ref[...] = g
        d_ref[...] = -ADAM_LR * (m_hat / (jnp.sqrt(v_hat) + ADAM_EPS) + ADAM_WD * w_ref[...])
        nm_ref[...] = nm
        nv_ref[...] = nv

    spec = pl.BlockSpec((tr, cols), lambda i: (i, 0))
    blocks = 7 * tr * cols * 4 + _nbytes((n_slots, tr, cols), slots.dtype)
    return pl.pallas_call(
        body, name=name, grid=(rows // tr,),
        in_specs=[spec, pl.BlockSpec((n_slots, tr, cols), lambda i: (0, i, 0)), spec, spec], out_specs=[spec] * 4,
        out_shape=[jax.ShapeDtypeStruct((rows, cols), _F32)] * 4,
        compiler_params=pltpu.CompilerParams(dimension_semantics=("arbitrary",), vmem_limit_bytes=_vmem_limit(blocks)),
    )(w, slots, m, v)


def _pad_rows(a, rows):
    return jnp.pad(a, ((0, rows - a.shape[0]), (0, 0)))


def _pack(parts):
    rows, spans, at = [], [], 0
    for p in parts:
        p = p.reshape(-1)
        r = -(-p.shape[0] // V7X_LANES)
        rows.append(jnp.pad(p, (0, r * V7X_LANES - p.shape[0])).reshape(r, V7X_LANES))
        spans.append((at, r, p.shape[0]))
        at += r
    packed = jnp.concatenate(rows, axis=0)
    return _pad_rows(packed, -(-at // V7X_SUBLANES) * V7X_SUBLANES), spans


def _unpack(packed, spans, shapes):
    return [packed[at:at + r].reshape(-1)[:size].reshape(shape) for (at, r, size), shape in zip(spans, shapes)]


def kernel(x, positions, w_in, conv_w, sinks, g_attn, g_conv, w_out, ln1_g, ln1_b, w_gate, w_up, w_down, ln2_g, ln2_b, loss_target, m_w_in, m_conv_w, m_sinks, m_g_attn, m_g_conv, m_w_out, m_ln1_g, m_ln1_b, m_w_gate, m_w_up, m_w_down, m_ln2_g, m_ln2_b, v_w_in, v_conv_w, v_sinks, v_g_attn, v_g_conv, v_w_out, v_ln1_g, v_ln1_b, v_w_gate, v_w_up, v_w_down, v_ln2_g, v_ln2_b):
    _, s, d = x.shape
    d_ff = N_DEV * w_gate.shape[2]
    dm = _Dims(s, d, d_ff)
    aw, cw, nq, inw = dm.aw, dm.cw, dm.nq, dm.inw
    x2 = x[0]
    x_c = x2.astype(_CDT)
    pos = positions[0].reshape(s, 1)
    inv_freq = ROPE_THETA ** (-jnp.arange(0, ROT_DIM, 2, dtype=_F32) / ROT_DIM)
    invf = jnp.tile(inv_freq, V7X_LANES // (ROT_DIM // 2)).reshape(1, V7X_LANES)

    conv_cols = conv_w.shape[2]
    sh_in, sh_out = w_in[0].T.astype(_CDT), w_out[0].astype(_CDT)
    sh_gate, sh_up, sh_down = w_gate[0].T.astype(_CDT), w_up[0].T.astype(_CDT), w_down[0].astype(_CDT)
    r_in, r_out, r_ff = sh_in.shape[0], sh_out.shape[0], sh_gate.shape[0]
    h_ff, q_ff = r_ff // 2, r_ff // 4
    assert q_ff % 16 == 0
    w_in_t, conv_all = _comm_kernel("gather_w_in", [_gather_op(
        [(sh_in, None, 0, r_in), (_pad_rows(conv_w[0], 16), None, 0, 16)])])
    conv_full = conv_all.reshape(N_DEV, 16, conv_cols)[:, :3, :].transpose(1, 0, 2).reshape(3, cw)
    conv_w8 = _pad_rows(conv_full, V7X_SUBLANES)

    tm = _pick(s, (1024, 512, 256, 128))
    tn_in = _pick(inw, (512, 256, 128))
    tn_ff = _pick(d_ff, (512, 256, 128))
    tk_in = _pick(inw, (1536, 1280, 1024, 512, 256, 128))
    tr = _pick(s, (512, 256, 128))

    proj, w_out_f, w_gate_t = _matmul(
        "proj", [[(x_c, w_in_t, "nt")]], s, inw, d, tm, tn_in, d, [],
        [((s, inw), _F32, (tm, tn_in), _tile_ij)], _store_epilogue,
        comm=[_gather_op([(sh_out, None, 0, r_out), (sh_gate, None, 0, q_ff)])])
    mixed, attn, lse, y_conv, w_gate_t, w_up_t = _mixer_fwd(
        dm, proj, pos, invf, sinks, g_attn, g_conv, conv_w8,
        comm=[_gather_op([(sh_gate, w_gate_t, q_ff, r_ff), (sh_up, None, 0, q_ff)])])

    def ln1_epilogue(accs, ex, out, first):
        x_ref, g_ref, b_ref = ex
        h1, xhat, rstd = _ln_fwd(DEEPNORM_ALPHA * x_ref[...] + accs[0], g_ref[...], b_ref[...])
        out[0][...] = h1
        out[1][...] = h1.astype(_CDT)
        out[2][...] = xhat
        out[3][...] = rstd

    h1, h1_c, xhat1, rstd1, w_up_t = _matmul(
        "out_proj_ln1", [[(mixed, w_out_f, "nn")]], s, d, d, tr, d, _pick(d, (1024, 512)),
        [(x2, (tr, d), _row_i), (ln1_g, (1, d), _whole), (ln1_b, (1, d), _whole)],
        [((s, d), _F32, (tr, d), _row_i), ((s, d), _CDT, (tr, d), _row_i), ((s, d), _F32, (tr, d), _row_i),
         ((s, 1), _F32, (tr, 1), _row_i)], ln1_epilogue,
        comm=[_gather_op([(sh_up, w_up_t, q_ff, h_ff)])])

    gate, w_up_t, w_down_f = _matmul(
        "gate", [[(h1_c, w_gate_t, "nt")]], s, d_ff, d, tm, tn_ff, d, [],
        [((s, d_ff), _F32, (tm, tn_ff), _tile_ij)], _store_epilogue,
        comm=[_gather_op([(sh_up, w_up_t, h_ff, r_ff), (sh_down, None, 0, q_ff)])])

    def swiglu_epilogue(accs, ex, out, first):
        gate_v = ex[0][...]
        out[0][...] = accs[0]
        out[1][...] = (gate_v * jax.nn.sigmoid(gate_v) * accs[0]).astype(_CDT)

    up, act, w_down_f = _matmul(
        "up_swiglu", [[(h1_c, w_up_t, "nt")]], s, d_ff, d, tm, tn_ff, d, [(gate, (tm, tn_ff), _tile_ij)],
        [((s, d_ff), _F32, (tm, tn_ff), _tile_ij), ((s, d_ff), _CDT, (tm, tn_ff), _tile_ij)], swiglu_epilogue,
        comm=[_gather_op([(sh_down, w_down_f, q_ff, r_ff)])], n_split=2)

    def residual_epilogue(accs, ex, out, first):
        out[0][...] = DEEPNORM_ALPHA * ex[0][...] + accs[0]

    tn_d = _pick(d, (512,))
    (r2,) = _matmul("down", [[(act, w_down_f, "nn")]], s, d, d_ff, tr, tn_d, d_ff, [(h1, (tr, tn_d), _tile_ij)],
                    [((s, d), _F32, (tr, tn_d), _tile_ij)], residual_epilogue)
    dr2, dr2_c, loss_acc, d_ln2_g, d_ln2_b = _ln2_loss_bwd(r2, loss_target[0], ln2_g, ln2_b)

    def swiglu_bwd_epilogue(accs, ex, out, first):
        gate_v, up_v = ex[0][...], ex[1][...]
        sig = jax.nn.sigmoid(gate_v)
        out[0][...] = (accs[0] * up_v * (sig * (1.0 + gate_v * (1.0 - sig)))).astype(_CDT)
        out[1][...] = (accs[0] * (gate_v * sig)).astype(_CDT)

    dgate, dup = _matmul(
        "dact", [[(dr2_c, w_down_f, "nt")]], s, d_ff, d, tm, tn_ff, d,
        [(gate, (tm, tn_ff), _tile_ij), (up, (tm, tn_ff), _tile_ij)],
        [((s, d_ff), _CDT, (tm, tn_ff), _tile_ij), ((s, d_ff), _CDT, (tm, tn_ff), _tile_ij)], swiglu_bwd_epilogue,
        n_split=2)
    def weight_grad(name, a, b, comm=()):
        rows = a.shape[1]
        tw, tn_w = _pick(rows, (512, 256, 128)), _pick(d, (1024, 512))
        return _matmul(name, [[(a, b, "tn")]], rows, d, s, tw, tn_w, s, [],
                       [((rows, d), _CDT, (tw, tn_w), _tile_ij)], _store_epilogue, comm=comm, j_outer=True)

    (dw_down,) = weight_grad("dw_down", act, dr2_c)
    dw_gate_t, x_down = weight_grad("dw_gate", dgate, h1_c, comm=[_exchange_op([dw_down])])
    q_down = _pair_sum("chip_sum_w_down", dw_down, x_down)
    dw_up_t, x_gate = weight_grad("dw_up", dup, h1_c, comm=[_exchange_op([dw_gate_t])])
    q_gate = _pair_sum("chip_sum_w_gate", dw_gate_t, x_gate)

    tn_h = _pick(d, (256,))
    dh1, l_down, x_up = _matmul(
        "dh1", [[(dgate, w_gate_t, "nn"), (dup, w_up_t, "nn")]], s, d, d_ff, tr, tn_h, d_ff,
        [(dr2, (tr, tn_h), _tile_ij)], [((s, d), _F32, (tr, tn_h), _tile_ij)], residual_epilogue,
        comm=[_chip_send_op([(q_down, None, 0, r_ff)]), _exchange_op([dw_up_t])])
    q_up = _pair_sum("chip_sum_w_up", dw_up_t, x_up)
    dr1, dr1_c, d_ln1_g, d_ln1_b = _ln1_bwd_rows(dh1, xhat1, rstd1, ln1_g)
    dmixed, l_gate = _matmul("dmixed", [[(dr1_c, w_out_f, "nt")]], s, d, d, tm, tn_d, d, [],
                             [((s, d), _F32, (tm, tn_d), _tile_ij)], _store_epilogue,
                             comm=[_chip_send_op([(q_gate, None, 0, h_ff)])])
    dw_out, l_gate = weight_grad("dw_out", mixed, dr1_c, comm=[_chip_send_op([(q_gate, l_gate, h_ff, r_ff)])])
    dproj, d_g_attn, d_g_conv, d_sinks, d_conv8, l_up, x_out = _mixer_bwd(
        dm, proj, pos, invf, sinks, g_attn, g_conv, conv_w8, dmixed, attn, lse, y_conv,
        comm=[_chip_send_op([(q_up, None, 0, r_ff)]), _exchange_op([dw_out])])
    q_out = _pair_sum("chip_sum_w_out", dw_out, x_out)
    dw_in_t, l_out = weight_grad("dw_in", dproj, x_c, comm=[_chip_send_op([(q_out, None, 0, r_out)])])
    (x_in,) = _comm_kernel("exchange_w_in", [_exchange_op([dw_in_t])])
    q_in = _pair_sum("chip_sum_w_in", dw_in_t, x_in)

    def dx_epilogue(accs, ex, out, first):
        out[0][...] = DEEPNORM_ALPHA * ex[0][...] + accs[0]

    grad_x, l_in = _matmul("dx", [[(dproj, w_in_t, "nn")]], s, d, inw, tr, d, tk_in,
                           [(dr1, (tr, d), _row_i)], [((s, d), _F32, (tr, d), _row_i)], dx_epilogue,
                           comm=[_chip_send_op([(q_in, None, 0, r_in)])])

    small_parts = [d_conv8[:3], d_sinks, d_g_attn, d_g_conv, d_ln1_g, d_ln1_b, d_ln2_g, d_ln2_b]
    packed, spans = _pack(small_parts)
    reduced = _unpack(_all_reduce_small("reduce_small", packed), spans, [p.shape for p in small_parts])
    g_conv_full, g_sinks, g_g_attn, g_g_conv, g_ln1_g, g_ln1_b, g_ln2_g, g_ln2_b = reduced
    me = _linear(*_position())
    g_conv_w = lax.dynamic_slice(g_conv_full, (0, me * conv_cols), (3, conv_cols))
    loss = lax.psum(loss_acc[0, 0], ("x", "y", "c"))

    big = {"w_in": (w_in[0].T, l_in, m_w_in[0].T, v_w_in[0].T), "w_out": (w_out[0], l_out, m_w_out[0], v_w_out[0]),
           "w_gate": (w_gate[0].T, l_gate, m_w_gate[0].T, v_w_gate[0].T),
           "w_up": (w_up[0].T, l_up, m_w_up[0].T, v_w_up[0].T), "w_down": (w_down[0], l_down, m_w_down[0], v_w_down[0])}
    res = {nm: tuple(_adamw(f"adamw_{nm}", w, slots, m, v)) for nm, (w, slots, m, v) in big.items()}
    for nm in ("w_in", "w_gate", "w_up"):
        res[nm] = tuple(a.T for a in res[nm])
    small_names = ["conv_w", "sinks", "g_attn", "g_conv", "ln1_g", "ln1_b", "ln2_g", "ln2_b"]
    small_w = [conv_w, sinks, g_attn, g_conv, ln1_g, ln1_b, ln2_g, ln2_b]
    small_g = [g_conv_w[None], g_sinks, g_g_attn, g_g_conv, g_ln1_g, g_ln1_b, g_ln2_g, g_ln2_b]
    small_m = [m_conv_w, m_sinks, m_g_attn, m_g_conv, m_ln1_g, m_ln1_b, m_ln2_g, m_ln2_b]
    small_v = [v_conv_w, v_sinks, v_g_attn, v_g_conv, v_ln1_g, v_ln1_b, v_ln2_g, v_ln2_b]
    pw, sp = _pack(small_w)
    pg, _ = _pack(small_g)
    pm, _ = _pack(small_m)
    pv, _ = _pack(small_v)
    shapes = [w.shape for w in small_w]
    _, sd, sm, sv = [_unpack(p, sp, shapes) for p in _adamw("adamw_small", pw, pg[None], pm, pv)]
    for i, nm in enumerate(small_names):
        res[nm] = (small_g[i].reshape(shapes[i]), sd[i], sm[i], sv[i])

    order = ["w_in", "conv_w", "sinks", "g_attn", "g_conv", "w_out", "ln1_g", "ln1_b", "w_gate", "w_up", "w_down", "ln2_g", "ln2_b"]

    def lead(a, nm):
        return a[None] if nm in big else a

    return (loss, grad_x[None],
            *[lead(res[nm][0], nm) for nm in order], *[lead(res[nm][1], nm) for nm in order],
            *[lead(res[nm][2], nm) for nm in order], *[lead(res[nm][3], nm) for nm in order])
```

```python
import functools

import jax
import jax.numpy as jnp
from jax import lax
from jax.experimental import pallas as pl
from jax.experimental.pallas import tpu as pltpu

_F32 = jnp.float32
_CDT = jnp.bfloat16

HEAD_DIM = 64
WINDOW = 128
N_KV_HEADS = 4
KV_WIDTH = N_KV_HEADS * HEAD_DIM
ROT_DIM = HEAD_DIM // 4
ROPE_THETA = 500000.0
ATTN_SCALE = HEAD_DIM ** -0.5
DEPTH = 1
DEEPNORM_ALPHA = (2 * DEPTH) ** 0.25
LN_EPS = 1e-5
RMS_EPS = 1e-6
ADAM_LR = 0.001
ADAM_B1 = 0.9
ADAM_B2 = 0.999
ADAM_EPS = 1e-08
ADAM_WD = 0.01
ADAM_STEP = 10
N_DEV = 8
MASKED = -1e30

V7X_VMEM_BYTES = 64 * 1024 * 1024
V7X_LANES = 128
V7X_SUBLANES = 8
_MESH = pl.DeviceIdType.MESH
_ANY = pl.BlockSpec(memory_space=pl.ANY)


def _vmem_limit(block_bytes, scratch_bytes=0):
    want = 2 * block_bytes + scratch_bytes + 16 * 1024 * 1024
    return int(min(max(want, 32 * 1024 * 1024), V7X_VMEM_BYTES - 8 * 1024 * 1024))


def _nbytes(shape, dtype):
    n = 1
    for s in shape:
        n *= s
    return n * jnp.dtype(dtype).itemsize


def _pick(n, candidates):
    for c in candidates:
        if n % c == 0:
            return c
    raise ValueError(f"no tile of {candidates} divides {n}")


_DOT_DIMS = {"nn": ((1,), (0,)), "nt": ((1,), (1,)), "tn": ((0,), (0,))}


def _dot(a, b, mode):
    return lax.dot_general(a.astype(_CDT), b.astype(_CDT), (_DOT_DIMS[mode], ((), ())),
                           preferred_element_type=_F32)


def _accumulate(ref, val, first):
    @pl.when(first)
    def _():
        ref[...] = val

    @pl.when(jnp.logical_not(first))
    def _():
        ref[...] += val


class _Comm:
    def __init__(self, inputs, outputs, aliases, sems, start, finish):
        self.inputs, self.outputs, self.aliases, self.sems = inputs, outputs, aliases, sems
        self.start, self.finish = start, finish


class _CommArgs:
    def __init__(self, comms, n_in_before, n_out_before):
        self.comms, self.operands, self.out_shape, self.aliases, self.sems, self.at = comms, [], [], {}, [], []
        for cm in comms:
            self.at.append((len(self.operands), len(self.out_shape), len(self.sems)))
            for i_in, i_out in cm.aliases.items():
                self.aliases[n_in_before + len(self.operands) + i_in] = n_out_before + len(self.out_shape) + i_out
            self.operands += cm.inputs
            self.out_shape += cm.outputs
            self.sems += cm.sems

    def _each(self, in_refs, out_refs, sem_refs):
        for cm, (i0, o0, s0) in zip(self.comms, self.at):
            yield cm, (in_refs[i0:i0 + len(cm.inputs)], out_refs[o0:o0 + len(cm.outputs)], sem_refs[s0:s0 + len(cm.sems)])

    def start(self, in_refs, out_refs, sem_refs):
        for cm, refs in self._each(in_refs, out_refs, sem_refs):
            cm.start(*refs)

    def finish(self, in_refs, out_refs, sem_refs):
        for cm, refs in self._each(in_refs, out_refs, sem_refs):
            cm.finish(*refs)


def _matmul(name, groups, m, n, k, tm, tn, tk, extras, outs, epilogue, comm=(), j_outer=False, n_split=1):
    assert m % tm == 0 and n % tn == 0 and k % tk == 0, (name, m, n, k, tm, tn, tk)
    nk = k // tk
    assert n_split == 1 or (nk == 1 and tn % (n_split * V7X_LANES) == 0), (name, n_split)
    terms = [t for g in groups for t in g]
    operands, in_specs, block_bytes = [], [], 0

    def spec(blk, imap):
        return pl.BlockSpec(blk, (lambda g0, g1, kk: imap(g1, g0, kk)) if j_outer else imap)

    for a, b, mode in terms:
        assert a.shape == ((k, m) if mode == "tn" else (m, k)), (name, a.shape, mode)
        assert b.shape == ((n, k) if mode == "nt" else (k, n)), (name, b.shape, mode)
        if mode == "tn":
            a_blk, a_map = (tk, tm), (lambda i, j, kk: (kk, i))
        else:
            a_blk, a_map = (tm, tk), (lambda i, j, kk: (i, kk))
        if mode == "nt":
            b_blk, b_map = (tn, tk), (lambda i, j, kk: (j, kk))
        else:
            b_blk, b_map = (tk, tn), (lambda i, j, kk: (kk, j))
        operands += [a, b]
        in_specs += [spec(a_blk, a_map), spec(b_blk, b_map)]
        block_bytes += _nbytes(a_blk, a.dtype) + _nbytes(b_blk, b.dtype)
    for arr, blk, imap in extras:
        operands.append(arr)
        in_specs.append(spec(blk, lambda i, j, kk, imap=imap: imap(i, j)))
        block_bytes += _nbytes(blk, arr.dtype)
    out_shape, out_specs = [], []
    for shape, dtype, blk, imap in outs:
        out_shape.append(jax.ShapeDtypeStruct(shape, dtype))
        out_specs.append(spec(blk, lambda i, j, kk, imap=imap: imap(i, j)))
        block_bytes += _nbytes(blk, dtype)
    n_terms, n_extra, n_out, n_groups = len(terms), len(extras), len(outs), len(groups)
    scratch = [pltpu.VMEM((tm, tn), _F32) for _ in range(n_groups)] if nk > 1 else []
    ca = _CommArgs(list(comm), len(operands), n_out)
    n_cin, n_cout, n_acc = len(ca.operands), len(ca.out_shape), len(scratch)
    tiles = (m // tm, n // tn)
    grid = (tiles[1], tiles[0], nk) if j_outer else (tiles[0], tiles[1], nk)

    def body(*refs):
        refs = list(refs)
        term_refs = [refs.pop(0) for _ in range(2 * n_terms)]
        extra_refs = [refs.pop(0) for _ in range(n_extra)]
        cin_refs = [refs.pop(0) for _ in range(n_cin)]
        out_refs = [refs.pop(0) for _ in range(n_out)]
        cout_refs = [refs.pop(0) for _ in range(n_cout)]
        acc_refs = [refs.pop(0) for _ in range(n_acc)]
        sem_refs = refs
        g0, g1, kk = pl.program_id(0), pl.program_id(1), pl.program_id(2)
        first = jnp.logical_and(g0 == 0, g1 == 0)
        if comm:
            @pl.when(jnp.logical_and(first, kk == 0))
            def _():
                ca.start(cin_refs, cout_refs, sem_refs)
        def products(cols):
            partial, t = [], 0
            for g in groups:
                s = None
                for _, _, mode in g:
                    b_ref = term_refs[2 * t + 1]
                    b = b_ref[...] if cols is None else (b_ref[cols, :] if mode == "nt" else b_ref[:, cols])
                    d = _dot(term_refs[2 * t][...], b, mode)
                    s = d if s is None else s + d
                    t += 1
                partial.append(s)
            return partial

        if n_split > 1:
            width = tn // n_split
            chunk = lambda c: pl.ds(c * width, width)
            ahead = products(chunk(0))
            for c in range(n_split):
                done, cols = ahead, chunk(c)
                if c + 1 < n_split:
                    ahead = products(chunk(c + 1))
                view = lambda ref: ref.at[:, cols] if tuple(ref.shape) == (tm, tn) else ref
                epilogue(done, [view(r) for r in extra_refs], [view(r) for r in out_refs], first)
        elif nk == 1:
            epilogue(products(None), extra_refs, out_refs, first)
        else:
            partial = products(None)
            for acc, p in zip(acc_refs, partial):
                _accumulate(acc, p, kk == 0)

            @pl.when(kk == nk - 1)
            def _():
                epilogue([acc[...] for acc in acc_refs], extra_refs, out_refs, first)
        if comm:
            @pl.when(jnp.logical_and(jnp.logical_and(g0 == grid[0] - 1, g1 == grid[1] - 1), kk == nk - 1))
            def _():
                ca.finish(cin_refs, cout_refs, sem_refs)

    res = pl.pallas_call(
        body, name=name, grid=grid,
        in_specs=in_specs + [_ANY] * n_cin, out_specs=out_specs + [_ANY] * n_cout,
        out_shape=out_shape + ca.out_shape, scratch_shapes=scratch + ca.sems, input_output_aliases=ca.aliases,
        compiler_params=pltpu.CompilerParams(
            dimension_semantics=("arbitrary", "arbitrary", "arbitrary"),
            vmem_limit_bytes=_vmem_limit(block_bytes, n_groups * tm * tn * 4 if nk > 1 else 0)),
    )(*operands, *ca.operands)
    return list(res[:n_out]) + list(res[n_out:])


def _store_epilogue(accs, extra_refs, out_refs, first):
    for acc, ref in zip(accs, out_refs):
        ref[...] = acc.astype(ref.dtype)


def _tile_ij(i, j):
    return (i, j)


def _row_i(i, j):
    return (i, 0)


def _whole(i, j):
    return (0, 0)


def _mean(v):
    return jnp.mean(v, axis=-1, keepdims=True)


def _ln_fwd(r, g, b):
    xc = r - _mean(r)
    rstd = lax.rsqrt(_mean(xc * xc) + LN_EPS)
    xhat = xc * rstd
    return xhat * g + b, xhat, rstd


def _ln_bwd(dy, xhat, rstd, g):
    dxh = dy * g
    dr = rstd * (dxh - _mean(dxh) - xhat * _mean(dxh * xhat))
    return dr, jnp.sum(dy * xhat, axis=0, keepdims=True), jnp.sum(dy, axis=0, keepdims=True)


def _rms_fwd(a, g):
    rstd = lax.rsqrt(_mean(a * a) + RMS_EPS)
    return a * rstd * g


def _rms_bwd(dm, a, g):
    rstd = lax.rsqrt(_mean(a * a) + RMS_EPS)
    nhat = a * rstd
    dn = dm * g
    da = rstd * (dn - nhat * _mean(dn * nhat))
    return da, jnp.sum(dm * nhat, axis=0, keepdims=True)


def _lane(shape):
    return lax.broadcasted_iota(jnp.int32, shape, 1)


def _row(shape):
    return lax.broadcasted_iota(jnp.int32, shape, 0)


def _rope_tables(pos, invf):
    ang = pos.astype(_F32) * invf
    lane = _lane(ang.shape)
    in_rot = (lane % HEAD_DIM) < ROT_DIM
    first = (lane % ROT_DIM) < ROT_DIM // 2
    cos = jnp.where(in_rot, jnp.cos(ang), 1.0)
    sin = jnp.sin(ang)
    sgn = jnp.where(in_rot, jnp.where(first, -sin, sin), 0.0)
    return cos, sgn


def _rope(t, cos, sgn, sign):
    half = ROT_DIM // 2
    first = (_lane(t.shape) % ROT_DIM) < half
    partner = jnp.where(first, pltpu.roll(t, V7X_LANES - half, 1), pltpu.roll(t, half, 1))
    return t * cos + partner * (sgn * sign)


def _dup_head(t, h):
    g = t[:, 128 * (h // 2):128 * (h // 2) + 128]
    r = pltpu.roll(g, HEAD_DIM, 1)
    lo = _lane(g.shape) < HEAD_DIM
    return jnp.where(lo, g, r) if h % 2 == 0 else jnp.where(lo, r, g)


def _fold_halves(t):
    return t + pltpu.roll(t, HEAD_DIM, 1)


def _halves(t):
    lo = _lane(t.shape) < HEAD_DIM
    zero = jnp.zeros_like(t)
    return jnp.where(lo, t, zero), jnp.where(lo, zero, t)


def _band_mask(n_heads, n_keys, first_block):
    shape = (n_heads * WINDOW, n_keys)
    i = jnp.bitwise_and(_row(shape), WINDOW - 1)
    j = _lane(shape)
    valid = jnp.logical_and(j >= i + 1, j <= i + WINDOW)
    if first_block is not None:
        valid = jnp.logical_and(valid, jnp.logical_or(j >= WINDOW, jnp.logical_not(first_block)))
    return valid


def _stack_heads(pairs):
    return jnp.concatenate([half for t in pairs for half in _halves(t)], axis=0).astype(_CDT)


def _unstack_heads(t, n_pairs):
    lo = _lane((WINDOW, 128)) < HEAD_DIM
    return [jnp.where(lo, t[2 * WINDOW * i:2 * WINDOW * i + WINDOW], t[2 * WINDOW * i + WINDOW:2 * WINDOW * (i + 1)])
            for i in range(n_pairs)]


def _per_head(values):
    return jnp.concatenate([jnp.full((WINDOW, 1), v, _F32) for v in values], axis=0)


def _shift_down(z, halo, k):
    rows = z.shape[0]
    out = pltpu.roll(z, k, 0)
    r = _row(z.shape)
    for t in range(k):
        out = jnp.where(r == t, halo[V7X_SUBLANES - k + t:V7X_SUBLANES - k + t + 1, :], out)
    del rows
    return out


def _shift_up(z, halo, k):
    rows = z.shape[0]
    out = pltpu.roll(z, rows - k, 0)
    r = _row(z.shape)
    for t in range(k):
        out = jnp.where(r == rows - k + t, halo[t:t + 1, :], out)
    return out


class _Dims:
    def __init__(self, s, d, d_ff):
        self.s, self.d, self.d_ff = s, d, d_ff
        self.aw = d // 2
        self.cw = d - self.aw
        self.nq = self.aw // HEAD_DIM
        self.group = self.nq // N_KV_HEADS
        assert self.group % 2 == 0, "a 128-lane pair of query heads must share its kv head"
        self.inw = self.aw + 2 * KV_WIDTH + 3 * self.cw
        self.o_k = self.aw
        self.o_v = self.aw + KV_WIDTH
        self.o_cg = self.aw + 2 * KV_WIDTH
        self.o_bg = self.o_cg + self.cw
        self.o_u = self.o_bg + self.cw
        self.nb = s // WINDOW
        assert s % WINDOW == 0


def _carrying(body, n_in, n_out, n_steps, ca):
    n_cin, n_cout = len(ca.operands), len(ca.out_shape)

    def wrapped(*refs):
        refs = list(refs)
        in_refs = [refs.pop(0) for _ in range(n_in)]
        cin_refs = [refs.pop(0) for _ in range(n_cin)]
        out_refs = [refs.pop(0) for _ in range(n_out)]
        cout_refs = [refs.pop(0) for _ in range(n_cout)]
        if ca.comms:
            @pl.when(pl.program_id(0) == 0)
            def _():
                ca.start(cin_refs, cout_refs, refs)
        body(*in_refs, *out_refs)
        if ca.comms:
            @pl.when(pl.program_id(0) == n_steps - 1)
            def _():
                ca.finish(cin_refs, cout_refs, refs)

    return wrapped


def _row_kernel(name, body, rows_in, vecs_in, rows_out, vecs_out, comm=()):
    s = rows_in[0].shape[0]
    tr = _pick(s, (256, 128))
    row = lambda a: pl.BlockSpec((tr, a[1] if isinstance(a, tuple) else a.shape[1]), lambda i: (i, 0))
    vec = lambda shape: pl.BlockSpec(tuple(shape), lambda i: (0, 0))
    n_in, n_out = len(rows_in) + len(vecs_in), len(rows_out) + len(vecs_out)
    ca = _CommArgs(list(comm), n_in, n_out)
    blocks = sum(_nbytes((tr, a.shape[1]), a.dtype) for a in rows_in) + sum(_nbytes((tr, sh[1]), dt) for sh, dt in rows_out)
    res = pl.pallas_call(
        _carrying(body, n_in, n_out, s // tr, ca), name=name, grid=(s // tr,),
        in_specs=[row(a) for a in rows_in] + [vec(v.shape) for v in vecs_in] + [_ANY] * len(ca.operands),
        out_specs=[row(sh) for sh, _ in rows_out] + [vec(sh) for sh, _ in vecs_out] + [_ANY] * len(ca.out_shape),
        out_shape=[jax.ShapeDtypeStruct(sh, dt) for sh, dt in list(rows_out) + list(vecs_out)] + ca.out_shape,
        scratch_shapes=ca.sems, input_output_aliases=ca.aliases,
        compiler_params=pltpu.CompilerParams(dimension_semantics=("arbitrary",), vmem_limit_bytes=_vmem_limit(blocks)),
    )(*rows_in, *vecs_in, *ca.operands)
    return list(res)


def _ln2_loss_bwd(r2, target, gain, bias, comm=()):
    s, d = r2.shape

    def body(r_ref, t_ref, g_ref, b_ref, dr_ref, drc_ref, loss_ref, dg_ref, db_ref):
        first = pl.program_id(0) == 0
        yv, xhat, rstd = _ln_fwd(r_ref[...], g_ref[...], b_ref[...])
        err = yv - t_ref[...]
        dr2, dg, db = _ln_bwd(err * (1.0 / d), xhat, rstd, g_ref[...])
        dr_ref[...] = dr2
        drc_ref[...] = dr2.astype(_CDT)
        _accumulate(loss_ref, jnp.zeros(loss_ref.shape, _F32) + 0.5 * jnp.sum(err * err) * (1.0 / d), first)
        _accumulate(dg_ref, dg, first)
        _accumulate(db_ref, db, first)

    return _row_kernel("ln2_loss_bwd", body, [r2, target], [gain, bias], [((s, d), _F32), ((s, d), _CDT)],
                       [((V7X_SUBLANES, V7X_LANES), _F32), ((1, d), _F32), ((1, d), _F32)], comm)


def _ln1_fwd_rows(r1, gain, bias, comm=()):
    s, d = r1.shape

    def body(r_ref, g_ref, b_ref, h_ref, hc_ref, xhat_ref, rstd_ref):
        h1, xhat, rstd = _ln_fwd(r_ref[...], g_ref[...], b_ref[...])
        h_ref[...] = h1
        hc_ref[...] = h1.astype(_CDT)
        xhat_ref[...] = xhat
        rstd_ref[...] = rstd

    return _row_kernel("ln1", body, [r1], [gain, bias],
                       [((s, d), _F32), ((s, d), _CDT), ((s, d), _F32), ((s, 1), _F32)], [], comm)


def _ln1_bwd_rows(dh1, xhat, rstd, gain, comm=()):
    s, d = dh1.shape

    def body(dh_ref, xhat_ref, rstd_ref, g_ref, dr_ref, drc_ref, dg_ref, db_ref):
        first = pl.program_id(0) == 0
        dr1, dg, db = _ln_bwd(dh_ref[...], xhat_ref[...], rstd_ref[...], g_ref[...])
        dr_ref[...] = dr1
        drc_ref[...] = dr1.astype(_CDT)
        _accumulate(dg_ref, dg, first)
        _accumulate(db_ref, db, first)

    return _row_kernel("ln1_bwd", body, [dh1, xhat, rstd], [gain], [((s, d), _F32), ((s, d), _CDT)],
                       [((1, d), _F32), ((1, d), _F32)], comm)


def _mixer_fwd(dm, proj, pos, invf, sinks, g_attn, g_conv, conv_w8, comm=()):
    s, d, aw, cw, nq, inw, nb = dm.s, dm.d, dm.aw, dm.cw, dm.nq, dm.inw, dm.nb

    def body(pp_ref, pc_ref, posp_ref, posc_ref, invf_ref, sinks_ref, ga_ref, gc_ref, cw_ref,
             mixed_ref, attn_ref, lse_ref, y_ref):
        n = pl.program_id(0)
        cos_c, sgn_c = _rope_tables(posc_ref[...], invf_ref[...])
        cos_p, sgn_p = _rope_tables(posp_ref[...], invf_ref[...])
        kk = jnp.concatenate(
            [jnp.concatenate([_rope(ref[:, dm.o_k + 128 * g:dm.o_k + 128 * g + 128], c, sg, 1.0)
                              for g in range(KV_WIDTH // 128)], axis=1)
             for ref, c, sg in ((pp_ref, cos_p, sgn_p), (pc_ref, cos_c, sgn_c))], axis=0)
        vv = jnp.concatenate([pp_ref[:, dm.o_v:dm.o_v + KV_WIDTH], pc_ref[:, dm.o_v:dm.o_v + KV_WIDTH]], axis=0)
        group, pairs = dm.group, dm.group // 2
        valid = _band_mask(group, 2 * WINDOW, n == 0)
        for h in range(N_KV_HEADS):
            k2, v2 = _dup_head(kk, h).astype(_CDT), _dup_head(vv, h).astype(_CDT)
            q4 = _stack_heads([_rope(pc_ref[:, 128 * j:128 * j + 128], cos_c, sgn_c, 1.0)
                               for j in range(pairs * h, pairs * (h + 1))])
            sc = jnp.where(valid, _dot(q4, k2, "nt") * ATTN_SCALE, MASKED)
            sink = _per_head([sinks_ref[0, group * h + r] for r in range(group)])
            mx = jnp.maximum(jnp.max(sc, axis=1, keepdims=True), sink)
            p = jnp.exp(sc - mx)
            den = jnp.sum(p, axis=1, keepdims=True) + jnp.exp(sink - mx)
            out = _unstack_heads(_dot(p / den, v2, "nn"), pairs)
            lse = mx + jnp.log(den)
            for r in range(group):
                lse_ref[:, group * h + r:group * h + r + 1] = lse[WINDOW * r:WINDOW * (r + 1)]
            for i in range(pairs):
                j = pairs * h + i
                attn_ref[:, 128 * j:128 * j + 128] = out[i]
        mixed_ref[:, 0:aw] = _rms_fwd(attn_ref[...], ga_ref[...]).astype(mixed_ref.dtype)

        z = pc_ref[:, dm.o_cg:dm.o_cg + cw] * pc_ref[:, dm.o_u:dm.o_u + cw]
        top = WINDOW - V7X_SUBLANES
        halo = pp_ref[top:WINDOW, dm.o_cg:dm.o_cg + cw] * pp_ref[top:WINDOW, dm.o_u:dm.o_u + cw]
        halo = jnp.where(n == 0, jnp.zeros_like(halo), halo)
        y = cw_ref[0:1, :] * _shift_down(z, halo, 2) + cw_ref[1:2, :] * _shift_down(z, halo, 1) + cw_ref[2:3, :] * z
        y_ref[...] = y
        conv = pc_ref[:, dm.o_bg:dm.o_bg + cw] * y
        mixed_ref[:, aw:d] = _rms_fwd(conv, gc_ref[...]).astype(mixed_ref.dtype)

    prev = lambda n: (jnp.maximum(n - 1, 0), 0)
    cur = lambda n: (n, 0)
    fixed = lambda n: (0, 0)
    blocks = 2 * WINDOW * inw * 4 + WINDOW * (d * 2 + aw * 4 + cw * 4 + nq * 4)
    ca = _CommArgs(list(comm), 9, 4)
    return pl.pallas_call(
        _carrying(body, 9, 4, nb, ca), name="mixer_fwd", grid=(nb,),
        in_specs=[pl.BlockSpec((WINDOW, inw), prev), pl.BlockSpec((WINDOW, inw), cur),
                  pl.BlockSpec((WINDOW, 1), prev), pl.BlockSpec((WINDOW, 1), cur),
                  pl.BlockSpec((1, V7X_LANES), fixed), pl.BlockSpec(memory_space=pltpu.SMEM),
                  pl.BlockSpec((1, aw), fixed), pl.BlockSpec((1, cw), fixed), pl.BlockSpec((V7X_SUBLANES, cw), fixed)]
        + [_ANY] * len(ca.operands),
        out_specs=[pl.BlockSpec((WINDOW, d), cur), pl.BlockSpec((WINDOW, aw), cur),
                   pl.BlockSpec((WINDOW, nq), cur), pl.BlockSpec((WINDOW, cw), cur)] + [_ANY] * len(ca.out_shape),
        out_shape=[jax.ShapeDtypeStruct((s, d), _CDT), jax.ShapeDtypeStruct((s, aw), _F32),
                   jax.ShapeDtypeStruct((s, nq), _F32), jax.ShapeDtypeStruct((s, cw), _F32)] + ca.out_shape,
        scratch_shapes=ca.sems, input_output_aliases=ca.aliases,
        compiler_params=pltpu.CompilerParams(dimension_semantics=("arbitrary",), vmem_limit_bytes=_vmem_limit(blocks)),
    )(proj, proj, pos, pos, invf, sinks, g_attn, g_conv, conv_w8, *ca.operands)


def _mixer_bwd(dm, proj, pos, invf, sinks, g_attn, g_conv, conv_w8, dmixed, attn, lse, y, comm=()):
    s, d, aw, cw, nq, inw, nb = dm.s, dm.d, dm.aw, dm.cw, dm.nq, dm.inw, dm.nb

    def body(pp_ref, pc_ref, pn_ref, posp_ref, posc_ref, posn_ref, dmc_ref, dmn_ref, ac_ref, an_ref,
             lsec_ref, lsen_ref, yc_ref, yn_ref, invf_ref, sinks_ref, ga_ref, gc_ref, cw_ref,
             dproj_ref, dga_ref, dgc_ref, dsinks_ref, dcw_ref):
        n = pl.program_id(0)
        first = n == 0
        has_next = n < nb - 1
        cos_p, sgn_p = _rope_tables(posp_ref[...], invf_ref[...])
        cos_c, sgn_c = _rope_tables(posc_ref[...], invf_ref[...])
        cos_n, sgn_n = _rope_tables(posn_ref[...], invf_ref[...])

        da_c, dga = _rms_bwd(dmc_ref[:, 0:aw], ac_ref[...], ga_ref[...])
        da_n, _ = _rms_bwd(dmn_ref[:, 0:aw], an_ref[...], ga_ref[...])
        _accumulate(dga_ref, dga, first)
        kk = jnp.concatenate(
            [jnp.concatenate([_rope(ref[:, dm.o_k + 128 * g:dm.o_k + 128 * g + 128], c, sg, 1.0)
                              for g in range(KV_WIDTH // 128)], axis=1)
             for ref, c, sg in ((pp_ref, cos_p, sgn_p), (pc_ref, cos_c, sgn_c))], axis=0)
        vv = jnp.concatenate([pp_ref[:, dm.o_v:dm.o_v + KV_WIDTH], pc_ref[:, dm.o_v:dm.o_v + KV_WIDTH]], axis=0)
        group, pairs = dm.group, dm.group // 2
        valid_c = _band_mask(group, 2 * WINDOW, first)
        valid_n = jnp.logical_and(_band_mask(group, WINDOW, None), has_next)
        dk2, dv2 = [], []
        dsinks = jnp.zeros((1, nq), _F32)
        head_lane = _lane((1, nq))

        def stacked(q_ref, cos, sgn, da, o_ref, lse_ref_, h):
            cols = [slice(128 * j, 128 * j + 128) for j in range(pairs * h, pairs * (h + 1))]
            q4 = _stack_heads([_rope(q_ref[:, c], cos, sgn, 1.0) for c in cols])
            do4 = _stack_heads([da[:, c] for c in cols])
            lo = _lane((WINDOW, 128)) < HEAD_DIM
            deltas = []
            for c in cols:
                prod = o_ref[:, c] * da[:, c]
                deltas += [jnp.sum(jnp.where(lo, prod, 0.0), axis=1, keepdims=True),
                           jnp.sum(jnp.where(lo, 0.0, prod), axis=1, keepdims=True)]
            lse4 = jnp.concatenate([lse_ref_[:, group * h + r:group * h + r + 1] for r in range(group)], axis=0)
            return q4, do4, lse4, jnp.concatenate(deltas, axis=0)

        def scores_bwd(q4, do4, lse4, delta4, keys, vals, valid):
            sc = _dot(q4, keys, "nt") * ATTN_SCALE
            p = jnp.exp(jnp.where(valid, sc - lse4, MASKED))
            return p.astype(_CDT), (p * (_dot(do4, vals, "nt") - delta4) * ATTN_SCALE).astype(_CDT)

        for h in range(N_KV_HEADS):
            k2, v2 = _dup_head(kk, h).astype(_CDT), _dup_head(vv, h).astype(_CDT)
            q4, do4, lse4, delta4 = stacked(pc_ref, cos_c, sgn_c, da_c, ac_ref, lsec_ref, h)
            p, ds = scores_bwd(q4, do4, lse4, delta4, k2, v2, valid_c)
            for i, dq in enumerate(_unstack_heads(_dot(ds, k2, "nn"), pairs)):
                j = pairs * h + i
                dproj_ref[:, 128 * j:128 * j + 128] = _rope(dq, cos_c, sgn_c, -1.0).astype(dproj_ref.dtype)
            dk = _dot(ds, q4, "tn")[WINDOW:2 * WINDOW, :]
            dv = _dot(p, do4, "tn")[WINDOW:2 * WINDOW, :]
            sink4 = _per_head([sinks_ref[0, group * h + r] for r in range(group)])
            loss_sink = jnp.exp(sink4 - lse4) * delta4
            for r in range(group):
                dsinks = dsinks + jnp.where(head_lane == group * h + r,
                                            -jnp.sum(loss_sink[WINDOW * r:WINDOW * (r + 1)]), 0.0)
            q4, do4, lse4, delta4 = stacked(pn_ref, cos_n, sgn_n, da_n, an_ref, lsen_ref, h)
            p, ds = scores_bwd(q4, do4, lse4, delta4, k2[WINDOW:2 * WINDOW, :], v2[WINDOW:2 * WINDOW, :], valid_n)
            dk2.append(dk + _dot(ds, q4, "tn"))
            dv2.append(dv + _dot(p, do4, "tn"))
        _accumulate(dsinks_ref, dsinks, first)
        lo = _lane((WINDOW, 128)) < HEAD_DIM
        for g in range(KV_WIDTH // 128):
            dk = jnp.where(lo, _fold_halves(dk2[2 * g]), _fold_halves(dk2[2 * g + 1]))
            dv = jnp.where(lo, _fold_halves(dv2[2 * g]), _fold_halves(dv2[2 * g + 1]))
            dproj_ref[:, dm.o_k + 128 * g:dm.o_k + 128 * g + 128] = _rope(dk, cos_c, sgn_c, -1.0).astype(dproj_ref.dtype)
            dproj_ref[:, dm.o_v + 128 * g:dm.o_v + 128 * g + 128] = dv.astype(dproj_ref.dtype)

        bg = pc_ref[:, dm.o_bg:dm.o_bg + cw]
        yc = yc_ref[...]
        dconv, dgc = _rms_bwd(dmc_ref[:, aw:d], bg * yc, gc_ref[...])
        _accumulate(dgc_ref, dgc, first)
        dproj_ref[:, dm.o_bg:dm.o_bg + cw] = (dconv * yc).astype(dproj_ref.dtype)
        dy = dconv * bg
        bg_n = pn_ref[0:V7X_SUBLANES, dm.o_bg:dm.o_bg + cw]
        dconv_n, _ = _rms_bwd(dmn_ref[0:V7X_SUBLANES, aw:d], bg_n * yn_ref[...], gc_ref[...])
        halo = jnp.where(has_next, dconv_n * bg_n, 0.0)
        dy1 = _shift_up(dy, halo, 1)
        dy2 = _shift_up(dy, halo, 2)
        dz = cw_ref[2:3, :] * dy + cw_ref[1:2, :] * dy1 + cw_ref[0:1, :] * dy2
        cg = pc_ref[:, dm.o_cg:dm.o_cg + cw]
        u = pc_ref[:, dm.o_u:dm.o_u + cw]
        dproj_ref[:, dm.o_cg:dm.o_cg + cw] = (dz * u).astype(dproj_ref.dtype)
        dproj_ref[:, dm.o_u:dm.o_u + cw] = (dz * cg).astype(dproj_ref.dtype)
        z = cg * u
        dcw = jnp.concatenate(
            [jnp.sum(z * t, axis=0, keepdims=True) for t in (dy2, dy1, dy)]
            + [jnp.zeros((V7X_SUBLANES - 3, cw), _F32)], axis=0)
        _accumulate(dcw_ref, dcw, first)

    prev = lambda n: (jnp.maximum(n - 1, 0), 0)
    cur = lambda n: (n, 0)
    nxt = lambda n: (jnp.minimum(n + 1, nb - 1), 0)
    nxt8 = lambda n: (jnp.minimum((n + 1) * (WINDOW // V7X_SUBLANES), s // V7X_SUBLANES - 1), 0)
    fixed = lambda n: (0, 0)
    blocks = WINDOW * (3 * inw * 4 + 2 * d * 4 + 2 * aw * 4 + cw * 4 + inw * 2)
    ca = _CommArgs(list(comm), 19, 5)
    return pl.pallas_call(
        _carrying(body, 19, 5, nb, ca), name="mixer_bwd", grid=(nb,),
        in_specs=[pl.BlockSpec((WINDOW, inw), prev), pl.BlockSpec((WINDOW, inw), cur), pl.BlockSpec((WINDOW, inw), nxt),
                  pl.BlockSpec((WINDOW, 1), prev), pl.BlockSpec((WINDOW, 1), cur), pl.BlockSpec((WINDOW, 1), nxt),
                  pl.BlockSpec((WINDOW, d), cur), pl.BlockSpec((WINDOW, d), nxt),
                  pl.BlockSpec((WINDOW, aw), cur), pl.BlockSpec((WINDOW, aw), nxt),
                  pl.BlockSpec((WINDOW, nq), cur), pl.BlockSpec((WINDOW, nq), nxt),
                  pl.BlockSpec((WINDOW, cw), cur), pl.BlockSpec((V7X_SUBLANES, cw), nxt8),
                  pl.BlockSpec((1, V7X_LANES), fixed), pl.BlockSpec(memory_space=pltpu.SMEM),
                  pl.BlockSpec((1, aw), fixed), pl.BlockSpec((1, cw), fixed), pl.BlockSpec((V7X_SUBLANES, cw), fixed)]
        + [_ANY] * len(ca.operands),
        out_specs=[pl.BlockSpec((WINDOW, inw), cur), pl.BlockSpec((1, aw), fixed), pl.BlockSpec((1, cw), fixed),
                   pl.BlockSpec((1, nq), fixed), pl.BlockSpec((V7X_SUBLANES, cw), fixed)] + [_ANY] * len(ca.out_shape),
        out_shape=[jax.ShapeDtypeStruct((s, inw), _CDT), jax.ShapeDtypeStruct((1, aw), _F32),
                   jax.ShapeDtypeStruct((1, cw), _F32), jax.ShapeDtypeStruct((1, nq), _F32),
                   jax.ShapeDtypeStruct((V7X_SUBLANES, cw), _F32)] + ca.out_shape,
        scratch_shapes=ca.sems, input_output_aliases=ca.aliases,
        compiler_params=pltpu.CompilerParams(dimension_semantics=("arbitrary",), vmem_limit_bytes=_vmem_limit(blocks)),
    )(proj, proj, proj, pos, pos, pos, dmixed, dmixed, attn, attn, lse, lse, y, y, invf, sinks, g_attn, g_conv, conv_w8,
      *ca.operands)


def _position():
    return lax.axis_index("x"), lax.axis_index("y"), lax.axis_index("c")


def _linear(px, py, pc):
    return 4 * px + 2 * py + pc


def _comm_kernel(name, comm):
    ca = _CommArgs(list(comm), 0, 0)
    n_cin, n_cout = len(ca.operands), len(ca.out_shape)

    def body(*refs):
        cin, cout, sems = refs[:n_cin], refs[n_cin:n_cin + n_cout], refs[n_cin + n_cout:]
        ca.start(cin, cout, sems)
        ca.finish(cin, cout, sems)

    return pl.pallas_call(
        body, name=name, out_shape=ca.out_shape, in_specs=[_ANY] * n_cin, out_specs=[_ANY] * n_cout,
        scratch_shapes=ca.sems, input_output_aliases=ca.aliases,
    )(*ca.operands)


def _gather_op(units):
    n = len(units)
    inputs, outputs, aliases = [], [], {}
    for shard, _, _, _ in units:
        inputs.append(shard)
        outputs.append(jax.ShapeDtypeStruct((N_DEV * shard.shape[0], shard.shape[1]), shard.dtype))
    for u, (_, buf, _, _) in enumerate(units):
        if buf is not None:
            aliases[len(inputs)] = u
            inputs.append(buf)

    def plan(ins, outs, sems):
        send_sems, recv_sems, local_sems = sems
        x, y, c = _position()
        me, sibling = (x, y, c), (x, y, 1 - c)
        chips = [(1 - x, y), (x, 1 - y), (1 - x, 1 - y)]

        def rows(u, px, py, pc):
            shard, _, r0, r1 = units[u]
            return outs[u].at[pl.ds(pl.multiple_of(_linear(px, py, pc) * shard.shape[0] + r0, 16), r1 - r0), :]

        def own(u):
            _, _, r0, r1 = units[u]
            return ins[u].at[pl.ds(r0, r1 - r0), :]

        def copy(u, k, block, to, src=None):
            return pltpu.make_async_remote_copy(
                src_ref=rows(u, *block) if src is None else src, dst_ref=rows(u, *block),
                send_sem=send_sems.at[u, k], recv_sem=recv_sems.at[u, k], device_id=to, device_id_type=_MESH)

        mine = [pltpu.make_async_copy(own(u), rows(u, *me), local_sems.at[u]) for u in range(n)]
        first = []
        for u in range(n):
            first.append(copy(u, 0, me, sibling, src=own(u)))
            first += [copy(u, 1 + j, me, (*chip, c), src=own(u)) for j, chip in enumerate(chips)]
        passed = [[copy(u, 4 + j, (*chip, c), sibling) for j, chip in enumerate(chips)] for u in range(n)]
        landed = [[copy(u, 1 + j, (*chip, c), me) for j, chip in enumerate(chips)] for u in range(n)]
        rest = [[copy(u, 0, sibling, me)] + [copy(u, 4 + j, (*chip, 1 - c), me) for j, chip in enumerate(chips)]
                for u in range(n)]
        return mine, first, passed, landed, rest

    def start(ins, outs, sems):
        mine, first, _, _, _ = plan(ins, outs, sems)
        for cp in mine + first:
            cp.start()

    def finish(ins, outs, sems):
        mine, first, passed, landed, rest = plan(ins, outs, sems)
        for u in range(n):
            for arrived, onward in zip(landed[u], passed[u]):
                arrived.wait_recv()
                onward.start()
        for u in range(n):
            for cp in rest[u]:
                cp.wait_recv()
        for cp in first + [cp for row in passed for cp in row]:
            cp.wait_send()
        for cp in mine:
            cp.wait()

    sems = [pltpu.SemaphoreType.DMA((n, 7)), pltpu.SemaphoreType.DMA((n, 7)), pltpu.SemaphoreType.DMA((n,))]
    return _Comm(inputs, outputs, aliases, sems, start, finish)


def _peers(x, y, c):
    out = []
    for k in range(1, N_DEV):
        fx, fy, fc = (k >> 2) & 1, (k >> 1) & 1, k & 1
        out.append((1 - x if fx else x, 1 - y if fy else y, 1 - c if fc else c))
    return out


def _exchange_op(partials):
    n = len(partials)
    outputs = [jax.ShapeDtypeStruct((4, p.shape[0] // N_DEV, p.shape[1]), p.dtype) for p in partials]

    def plan(ins, outs, sems):
        send_sems, recv_sems = sems
        x, y, c = _position()
        out = []
        for a in range(n):
            r = outs[a].shape[1]
            for ch in range(4):
                out.append(pltpu.make_async_remote_copy(
                    src_ref=ins[a].at[pl.ds(pl.multiple_of((2 * ch + 1 - c) * r, 16), r), :], dst_ref=outs[a].at[ch],
                    send_sem=send_sems.at[a, ch], recv_sem=recv_sems.at[a, ch], device_id=(x, y, 1 - c),
                    device_id_type=_MESH))
        return out

    def start(ins, outs, sems):
        for cp in plan(ins, outs, sems):
            cp.start()

    def finish(ins, outs, sems):
        copies = plan(ins, outs, sems)
        for cp in copies:
            cp.wait_recv()
        for cp in copies:
            cp.wait_send()

    sems = [pltpu.SemaphoreType.DMA((n, 4)), pltpu.SemaphoreType.DMA((n, 4))]
    return _Comm(list(partials), outputs, {}, sems, start, finish)


def _chip_send_op(units):
    n = len(units)
    inputs, outputs, aliases = [], [], {}
    for q, _, _, _ in units:
        inputs.append(q)
        outputs.append(jax.ShapeDtypeStruct(q.shape, q.dtype))
    for u, (_, buf, _, _) in enumerate(units):
        if buf is not None:
            aliases[len(inputs)] = u
            inputs.append(buf)

    def plan(ins, outs, sems):
        send_sems, recv_sems, local_sems = sems
        x, y, c = _position()
        my_chip = 2 * x + y
        chips = [(1 - x, y), (x, 1 - y), (1 - x, 1 - y)]
        mine, sends, arrivals = [], [], []
        for u, (_, _, r0, r1) in enumerate(units):
            span = pl.ds(r0, r1 - r0)
            mine.append(pltpu.make_async_copy(ins[u].at[my_chip, span, :], outs[u].at[my_chip, span, :], local_sems.at[u]))
            for k, (px, py) in enumerate(chips):
                sends.append(pltpu.make_async_remote_copy(
                    src_ref=ins[u].at[2 * px + py, span, :], dst_ref=outs[u].at[my_chip, span, :],
                    send_sem=send_sems.at[u, k], recv_sem=recv_sems.at[u, k], device_id=(px, py, c), device_id_type=_MESH))
                arrivals.append(pltpu.make_async_remote_copy(
                    src_ref=ins[u].at[my_chip, span, :], dst_ref=outs[u].at[2 * px + py, span, :],
                    send_sem=send_sems.at[u, k], recv_sem=recv_sems.at[u, k], device_id=(px, py, c), device_id_type=_MESH))
        return mine, sends, arrivals

    def start(ins, outs, sems):
        mine, sends, _ = plan(ins, outs, sems)
        for cp in mine + sends:
            cp.start()

    def finish(ins, outs, sems):
        mine, sends, arrivals = plan(ins, outs, sems)
        for cp in arrivals:
            cp.wait_recv()
        for cp in sends:
            cp.wait_send()
        for cp in mine:
            cp.wait()

    sems = [pltpu.SemaphoreType.DMA((n, 3)), pltpu.SemaphoreType.DMA((n, 3)), pltpu.SemaphoreType.DMA((n,))]
    return _Comm(inputs, outputs, aliases, sems, start, finish)


def _pair_sum(name, partial, received):
    _, rows, cols = received.shape
    tr = _pick(rows, (352, 288, 256, 128, 64, 32, 16))
    p4 = partial.reshape(4, 2, rows, cols)
    kind = jnp.reshape(lax.axis_index("c"), (1,)).astype(jnp.int32)

    def body(kind_ref, p_ref, r_ref, o_ref):
        o_ref[0] = (p_ref[0, 0].astype(_F32) + r_ref[0].astype(_F32)).astype(o_ref.dtype)

    return pl.pallas_call(
        body, name=name,
        grid_spec=pltpu.PrefetchScalarGridSpec(
            num_scalar_prefetch=1, grid=(4, rows // tr),
            in_specs=[pl.BlockSpec((1, 1, tr, cols), lambda ch, i, kind_ref: (ch, kind_ref[0], i, 0)),
                      pl.BlockSpec((1, tr, cols), lambda ch, i, kind_ref: (ch, i, 0))],
            out_specs=pl.BlockSpec((1, tr, cols), lambda ch, i, kind_ref: (ch, i, 0))),
        out_shape=jax.ShapeDtypeStruct(received.shape, received.dtype),
        compiler_params=pltpu.CompilerParams(dimension_semantics=("arbitrary", "arbitrary")),
    )(kind, p4, received)


def _all_reduce_small(name, v):
    rows = v.shape[0]

    def body(v_ref, out_ref, land_ref, send_sems, recv_sems):
        x, y, c = _position()
        me = _linear(x, y, c)
        peers = _peers(x, y, c)
        land_ref[me] = v_ref[...]
        sends = [pltpu.make_async_remote_copy(
            src_ref=v_ref, dst_ref=land_ref.at[me], send_sem=send_sems.at[k], recv_sem=recv_sems.at[k],
            device_id=peer, device_id_type=_MESH) for k, peer in enumerate(peers)]
        for cp in sends:
            cp.start()
        for k, peer in enumerate(peers):
            pltpu.make_async_remote_copy(
                src_ref=v_ref, dst_ref=land_ref.at[_linear(*peer)], send_sem=send_sems.at[k], recv_sem=recv_sems.at[k],
                device_id=peer, device_id_type=_MESH).wait_recv()
        for cp in sends:
            cp.wait_send()
        total = land_ref[0]
        for s in range(1, N_DEV):
            total = total + land_ref[s]
        out_ref[...] = total

    return pl.pallas_call(
        body, name=name, out_shape=jax.ShapeDtypeStruct(v.shape, _F32),
        in_specs=[pl.BlockSpec(memory_space=pltpu.VMEM)], out_specs=pl.BlockSpec(memory_space=pltpu.VMEM),
        scratch_shapes=[pltpu.VMEM((N_DEV, rows, V7X_LANES), _F32), pltpu.SemaphoreType.DMA((7,)), pltpu.SemaphoreType.DMA((7,))],
    )(v)


def _adamw(name, w, slots, m, v):
    rows, cols = w.shape
    n_slots = slots.shape[0]
    tr = _pick(rows, (176, 144, 128, 64, 32, 16, 8))

    def body(w_ref, s_ref, m_ref, v_ref, g_ref, d_ref, nm_ref, nv_ref):
        g = s_ref[0].astype(_F32)
        for k in range(1, n_slots):
            g = g + s_ref[k].astype(_F32)
        nm = ADAM_B1 * m_ref[...] + (1.0 - ADAM_B1) * g
        nv = ADAM_B2 * v_ref[...] + (1.0 - ADAM_B2) * (g * g)
        m_hat = nm / (1.0 - ADAM_B1 ** ADAM_STEP)
        v_hat = nv / (1.0 - ADAM_B2 ** ADAM_STEP)
        g_ref[...] = g
        d_ref[...] = -ADAM_LR * (m_hat / (jnp.sqrt(v_hat) + ADAM_EPS) + ADAM_WD * w_ref[...])
        nm_ref[...] = nm
        nv_ref[...] = nv

    spec = pl.BlockSpec((tr, cols), lambda i: (i, 0))
    blocks = 7 * tr * cols * 4 + _nbytes((n_slots, tr, cols), slots.dtype)
    return pl.pallas_call(
        body, name=name, grid=(rows // tr,),
        in_specs=[spec, pl.BlockSpec((n_slots, tr, cols), lambda i: (0, i, 0)), spec, spec], out_specs=[spec] * 4,
        out_shape=[jax.ShapeDtypeStruct((rows, cols), _F32)] * 4,
        compiler_params=pltpu.CompilerParams(dimension_semantics=("arbitrary",), vmem_limit_bytes=_vmem_limit(blocks)),
    )(w, slots, m, v)


def _pad_rows(a, rows):
    return jnp.pad(a, ((0, rows - a.shape[0]), (0, 0)))


def _pack(parts):
    rows, spans, at = [], [], 0
    for p in parts:
        p = p.reshape(-1)
        r = -(-p.shape[0] // V7X_LANES)
        rows.append(jnp.pad(p, (0, r * V7X_LANES - p.shape[0])).reshape(r, V7X_LANES))
        spans.append((at, r, p.shape[0]))
        at += r
    packed = jnp.concatenate(rows, axis=0)
    return _pad_rows(packed, -(-at // V7X_SUBLANES) * V7X_SUBLANES), spans


def _unpack(packed, spans, shapes):
    return [packed[at:at + r].reshape(-1)[:size].reshape(shape) for (at, r, size), shape in zip(spans, shapes)]


def kernel(x, positions, w_in, conv_w, sinks, g_attn, g_conv, w_out, ln1_g, ln1_b, w_gate, w_up, w_down, ln2_g, ln2_b, loss_target, m_w_in, m_conv_w, m_sinks, m_g_attn, m_g_conv, m_w_out, m_ln1_g, m_ln1_b, m_w_gate, m_w_up, m_w_down, m_ln2_g, m_ln2_b, v_w_in, v_conv_w, v_sinks, v_g_attn, v_g_conv, v_w_out, v_ln1_g, v_ln1_b, v_w_gate, v_w_up, v_w_down, v_ln2_g, v_ln2_b):
    _, s, d = x.shape
    d_ff = N_DEV * w_gate.shape[2]
    dm = _Dims(s, d, d_ff)
    aw, cw, nq, inw = dm.aw, dm.cw, dm.nq, dm.inw
    x2 = x[0]
    pos = positions[0].reshape(s, 1)
    inv_freq = ROPE_THETA ** (-jnp.arange(0, ROT_DIM, 2, dtype=_F32) / ROT_DIM)
    invf = jnp.tile(inv_freq, V7X_LANES // (ROT_DIM // 2)).reshape(1, V7X_LANES)

    conv_cols = conv_w.shape[2]
    sh_in, sh_out = w_in[0].T.astype(_CDT), w_out[0].astype(_CDT)
    sh_gate, sh_up, sh_down = w_gate[0].T.astype(_CDT), w_up[0].T.astype(_CDT), w_down[0].astype(_CDT)
    r_in, r_out, r_ff = sh_in.shape[0], sh_out.shape[0], sh_gate.shape[0]
    q_ff = r_ff // 4
    assert q_ff % 16 == 0
    def cast_body(x_ref, o_ref):
        o_ref[...] = x_ref[...].astype(_CDT)

    x_c, w_in_t, conv_all = _row_kernel("cast_x_gather_w_in", cast_body, [x2], [], [((s, d), _CDT)], [], comm=[
        _gather_op([(sh_in, None, 0, r_in), (_pad_rows(conv_w[0], 16), None, 0, 16)])])
    conv_full = conv_all.reshape(N_DEV, 16, conv_cols)[:, :3, :].transpose(1, 0, 2).reshape(3, cw)
    conv_w8 = _pad_rows(conv_full, V7X_SUBLANES)

    tm = _pick(s, (1024, 512, 256, 128))
    tn_in = _pick(inw, (512, 256, 128))
    tn_ff = _pick(d_ff, (512, 256, 128))
    tr = _pick(s, (512, 256, 128))

    proj, w_out_f, w_gate_t = _matmul(
        "proj", [[(x_c, w_in_t, "nt")]], s, inw, d, tm, tn_in, d, [],
        [((s, inw), _F32, (tm, tn_in), _tile_ij)], _store_epilogue,
        comm=[_gather_op([(sh_out, None, 0, r_out), (sh_gate, None, 0, q_ff)])])
    mixed, attn, lse, y_conv, w_gate_t = _mixer_fwd(
        dm, proj, pos, invf, sinks, g_attn, g_conv, conv_w8,
        comm=[_gather_op([(sh_gate, w_gate_t, q_ff, r_ff)])])

    def residual_epilogue(accs, ex, out, first):
        out[0][...] = DEEPNORM_ALPHA * ex[0][...] + accs[0]

    tn_d = _pick(d, (512,))
    r1, w_up_t = _matmul(
        "out_proj", [[(mixed, w_out_f, "nn")]], s, d, d, tm, tn_d, d, [(x2, (tm, tn_d), _tile_ij)],
        [((s, d), _F32, (tm, tn_d), _tile_ij)], residual_epilogue, comm=[_gather_op([(sh_up, None, 0, q_ff)])])
    h1, h1_c, xhat1, rstd1 = _ln1_fwd_rows(r1, ln1_g, ln1_b)

    gate, w_up_t = _matmul(
        "gate", [[(h1_c, w_gate_t, "nt")]], s, d_ff, d, tm, tn_ff, d, [],
        [((s, d_ff), _F32, (tm, tn_ff), _tile_ij)], _store_epilogue,
        comm=[_gather_op([(sh_up, w_up_t, q_ff, r_ff)])])

    def swiglu_epilogue(accs, ex, out, first):
        gate_v = ex[0][...]
        out[0][...] = accs[0]
        out[1][...] = (gate_v * jax.nn.sigmoid(gate_v) * accs[0]).astype(_CDT)

    up, act, w_down_f = _matmul(
        "up_swiglu", [[(h1_c, w_up_t, "nt")]], s, d_ff, d, tm, tn_ff, d, [(gate, (tm, tn_ff), _tile_ij)],
        [((s, d_ff), _F32, (tm, tn_ff), _tile_ij), ((s, d_ff), _CDT, (tm, tn_ff), _tile_ij)], swiglu_epilogue,
        comm=[_gather_op([(sh_down, None, 0, r_ff)])], n_split=2)

    (r2,) = _matmul("down", [[(act, w_down_f, "nn")]], s, d, d_ff, tr, tn_d, d_ff, [(h1, (tr, tn_d), _tile_ij)],
                    [((s, d), _F32, (tr, tn_d), _tile_ij)], residual_epilogue)
    dr2, dr2_c, loss_acc, d_ln2_g, d_ln2_b = _ln2_loss_bwd(r2, loss_target[0], ln2_g, ln2_b)

    def swiglu_bwd_epilogue(accs, ex, out, first):
        gate_v, up_v = ex[0][...], ex[1][...]
        sig = jax.nn.sigmoid(gate_v)
        out[0][...] = (accs[0] * up_v * (sig * (1.0 + gate_v * (1.0 - sig)))).astype(_CDT)
        out[1][...] = (accs[0] * (gate_v * sig)).astype(_CDT)

    dgate, dup = _matmul(
        "dact", [[(dr2_c, w_down_f, "nt")]], s, d_ff, d, tm, tn_ff, d,
        [(gate, (tm, tn_ff), _tile_ij), (up, (tm, tn_ff), _tile_ij)],
        [((s, d_ff), _CDT, (tm, tn_ff), _tile_ij), ((s, d_ff), _CDT, (tm, tn_ff), _tile_ij)], swiglu_bwd_epilogue,
        n_split=2)
    def weight_grad(name, a, b, comm=()):
        rows = a.shape[1]
        tw, tn_w = _pick(rows, (512, 256, 128)), _pick(d, (1024, 512))
        return _matmul(name, [[(a, b, "tn")]], rows, d, s, tw, tn_w, s, [],
                       [((rows, d), _CDT, (tw, tn_w), _tile_ij)], _store_epilogue, comm=comm, j_outer=True)

    (dw_down,) = weight_grad("dw_down", act, dr2_c)
    dw_gate_t, x_down = weight_grad("dw_gate", dgate, h1_c, comm=[_exchange_op([dw_down])])
    q_down = _pair_sum("chip_sum_w_down", dw_down, x_down)
    dw_up_t, l_down, x_gate = weight_grad(
        "dw_up", dup, h1_c, comm=[_chip_send_op([(q_down, None, 0, r_ff)]), _exchange_op([dw_gate_t])])
    q_gate = _pair_sum("chip_sum_w_gate", dw_gate_t, x_gate)

    tn_h = _pick(d, (256,))
    dh1, l_gate, x_up = _matmul(
        "dh1", [[(dgate, w_gate_t, "nn"), (dup, w_up_t, "nn")]], s, d, d_ff, tr, tn_h, d_ff,
        [(dr2, (tr, tn_h), _tile_ij)], [((s, d), _F32, (tr, tn_h), _tile_ij)], residual_epilogue,
        comm=[_chip_send_op([(q_gate, None, 0, r_ff)]), _exchange_op([dw_up_t])])
    q_up = _pair_sum("chip_sum_w_up", dw_up_t, x_up)
    dr1, dr1_c, d_ln1_g, d_ln1_b = _ln1_bwd_rows(dh1, xhat1, rstd1, ln1_g)
    (dmixed,) = _matmul("dmixed", [[(dr1_c, w_out_f, "nt")]], s, d, d, tm, tn_d, d, [],
                        [((s, d), _F32, (tm, tn_d), _tile_ij)], _store_epilogue)
    (dw_out,) = weight_grad("dw_out", mixed, dr1_c)
    dproj, d_g_attn, d_g_conv, d_sinks, d_conv8, l_up, x_out = _mixer_bwd(
        dm, proj, pos, invf, sinks, g_attn, g_conv, conv_w8, dmixed, attn, lse, y_conv,
        comm=[_chip_send_op([(q_up, None, 0, r_ff)]), _exchange_op([dw_out])])
    q_out = _pair_sum("chip_sum_w_out", dw_out, x_out)
    dw_in_t, l_out = weight_grad("dw_in", dproj, x_c, comm=[_chip_send_op([(q_out, None, 0, r_out)])])
    (x_in,) = _comm_kernel("exchange_w_in", [_exchange_op([dw_in_t])])
    q_in = _pair_sum("chip_sum_w_in", dw_in_t, x_in)

    grad_x, l_in = _matmul("dx", [[(dproj, w_in_t, "nn")]], s, d, inw, tr, tn_d, inw,
                           [(dr1, (tr, tn_d), _tile_ij)], [((s, d), _F32, (tr, tn_d), _tile_ij)], residual_epilogue,
                           comm=[_chip_send_op([(q_in, None, 0, r_in)])])

    small_parts = [d_conv8[:3], d_sinks, d_g_attn, d_g_conv, d_ln1_g, d_ln1_b, d_ln2_g, d_ln2_b]
    packed, spans = _pack(small_parts)
    reduced = _unpack(_all_reduce_small("reduce_small", packed), spans, [p.shape for p in small_parts])
    g_conv_full, g_sinks, g_g_attn, g_g_conv, g_ln1_g, g_ln1_b, g_ln2_g, g_ln2_b = reduced
    me = _linear(*_position())
    g_conv_w = lax.dynamic_slice(g_conv_full, (0, me * conv_cols), (3, conv_cols))
    loss = lax.psum(loss_acc[0, 0], ("x", "y", "c"))

    big = {"w_in": (w_in[0].T, l_in, m_w_in[0].T, v_w_in[0].T), "w_out": (w_out[0], l_out, m_w_out[0], v_w_out[0]),
           "w_gate": (w_gate[0].T, l_gate, m_w_gate[0].T, v_w_gate[0].T),
           "w_up": (w_up[0].T, l_up, m_w_up[0].T, v_w_up[0].T), "w_down": (w_down[0], l_down, m_w_down[0], v_w_down[0])}
    res = {nm: tuple(_adamw(f"adamw_{nm}", w, slots, m, v)) for nm, (w, slots, m, v) in big.items()}
    for nm in ("w_in", "w_gate", "w_up"):
        res[nm] = tuple(a.T for a in res[nm])
    small_names = ["conv_w", "sinks", "g_attn", "g_conv", "ln1_g", "ln1_b", "ln2_g", "ln2_b"]
    small_w = [conv_w, sinks, g_attn, g_conv, ln1_g, ln1_b, ln2_g, ln2_b]
    small_g = [g_conv_w[None], g_sinks, g_g_attn, g_g_conv, g_ln1_g, g_ln1_b, g_ln2_g, g_ln2_b]
    small_m = [m_conv_w, m_sinks, m_g_attn, m_g_conv, m_ln1_g, m_ln1_b, m_ln2_g, m_ln2_b]
    small_v = [v_conv_w, v_sinks, v_g_attn, v_g_conv, v_ln1_g, v_ln1_b, v_ln2_g, v_ln2_b]
    pw, sp = _pack(small_w)
    pg, _ = _pack(small_g)
    pm, _ = _pack(small_m)
    pv, _ = _pack(small_v)
    shapes = [w.shape for w in small_w]
    _, sd, sm, sv = [_unpack(p, sp, shapes) for p in _adamw("adamw_small", pw, pg[None], pm, pv)]
    for i, nm in enumerate(small_names):
        res[nm] = (small_g[i].reshape(shapes[i]), sd[i], sm[i], sv[i])

    order = ["w_in", "conv_w", "sinks", "g_attn", "g_conv", "w_out", "ln1_g", "ln1_b", "w_gate", "w_up", "w_down", "ln2_g", "ln2_b"]

    def lead(a, nm):
        return a[None] if nm in big else a

    return (loss, grad_x[None],
            *[lead(res[nm][0], nm) for nm in order], *[lead(res[nm][1], nm) for nm in order],
            *[lead(res[nm][2], nm) for nm in order], *[lead(res[nm][3], nm) for nm in order])
```

```python
import functools

import jax
import jax.numpy as jnp
from jax import lax
from jax.experimental import pallas as pl
from jax.experimental.pallas import tpu as pltpu

_F32 = jnp.float32
_CDT = jnp.bfloat16

HEAD_DIM = 64
WINDOW = 128
N_KV_HEADS = 4
KV_WIDTH = N_KV_HEADS * HEAD_DIM
ROT_DIM = HEAD_DIM // 4
ROPE_THETA = 500000.0
ATTN_SCALE = HEAD_DIM ** -0.5
DEPTH = 1
DEEPNORM_ALPHA = (2 * DEPTH) ** 0.25
LN_EPS = 1e-5
RMS_EPS = 1e-6
ADAM_LR = 0.001
ADAM_B1 = 0.9
ADAM_B2 = 0.999
ADAM_EPS = 1e-08
ADAM_WD = 0.01
ADAM_STEP = 10
N_DEV = 8
MASKED = -1e30

V7X_VMEM_BYTES = 64 * 1024 * 1024
V7X_LANES = 128
V7X_SUBLANES = 8
_MESH = pl.DeviceIdType.MESH
_ANY = pl.BlockSpec(memory_space=pl.ANY)


def _vmem_limit(block_bytes, scratch_bytes=0):
    want = 2 * block_bytes + scratch_bytes + 16 * 1024 * 1024
    return int(min(max(want, 32 * 1024 * 1024), V7X_VMEM_BYTES - 8 * 1024 * 1024))


def _nbytes(shape, dtype):
    n = 1
    for s in shape:
        n *= s
    return n * jnp.dtype(dtype).itemsize


def _pick(n, candidates):
    for c in candidates:
        if n % c == 0:
            return c
    raise ValueError(f"no tile of {candidates} divides {n}")


_DOT_DIMS = {"nn": ((1,), (0,)), "nt": ((1,), (1,)), "tn": ((0,), (0,))}


def _dot(a, b, mode):
    return lax.dot_general(a.astype(_CDT), b.astype(_CDT), (_DOT_DIMS[mode], ((), ())),
                           preferred_element_type=_F32)


def _accumulate(ref, val, first):
    @pl.when(first)
    def _():
        ref[...] = val

    @pl.when(jnp.logical_not(first))
    def _():
        ref[...] += val


class _Comm:
    def __init__(self, inputs, outputs, aliases, sems, start, finish):
        self.inputs, self.outputs, self.aliases, self.sems = inputs, outputs, aliases, sems
        self.start, self.finish = start, finish


class _CommArgs:
    def __init__(self, comms, n_in_before, n_out_before):
        self.comms, self.operands, self.out_shape, self.aliases, self.sems, self.at = comms, [], [], {}, [], []
        for cm in comms:
            self.at.append((len(self.operands), len(self.out_shape), len(self.sems)))
            for i_in, i_out in cm.aliases.items():
                self.aliases[n_in_before + len(self.operands) + i_in] = n_out_before + len(self.out_shape) + i_out
            self.operands += cm.inputs
            self.out_shape += cm.outputs
            self.sems += cm.sems

    def _each(self, in_refs, out_refs, sem_refs):
        for cm, (i0, o0, s0) in zip(self.comms, self.at):
            yield cm, (in_refs[i0:i0 + len(cm.inputs)], out_refs[o0:o0 + len(cm.outputs)], sem_refs[s0:s0 + len(cm.sems)])

    def start(self, in_refs, out_refs, sem_refs):
        for cm, refs in self._each(in_refs, out_refs, sem_refs):
            cm.start(*refs)

    def finish(self, in_refs, out_refs, sem_refs):
        for cm, refs in self._each(in_refs, out_refs, sem_refs):
            cm.finish(*refs)


def _matmul(name, groups, m, n, k, tm, tn, tk, extras, outs, epilogue, comm=(), j_outer=False, n_split=1):
    assert m % tm == 0 and n % tn == 0 and k % tk == 0, (name, m, n, k, tm, tn, tk)
    nk = k // tk
    assert n_split == 1 or (nk == 1 and tn % (n_split * V7X_LANES) == 0), (name, n_split)
    terms = [t for g in groups for t in g]
    operands, in_specs, block_bytes = [], [], 0

    def spec(blk, imap):
        return pl.BlockSpec(blk, (lambda g0, g1, kk: imap(g1, g0, kk)) if j_outer else imap)

    for a, b, mode in terms:
        assert a.shape == ((k, m) if mode == "tn" else (m, k)), (name, a.shape, mode)
        assert b.shape == ((n, k) if mode == "nt" else (k, n)), (name, b.shape, mode)
        if mode == "tn":
            a_blk, a_map = (tk, tm), (lambda i, j, kk: (kk, i))
        else:
            a_blk, a_map = (tm, tk), (lambda i, j, kk: (i, kk))
        if mode == "nt":
            b_blk, b_map = (tn, tk), (lambda i, j, kk: (j, kk))
        else:
            b_blk, b_map = (tk, tn), (lambda i, j, kk: (kk, j))
        operands += [a, b]
        in_specs += [spec(a_blk, a_map), spec(b_blk, b_map)]
        block_bytes += _nbytes(a_blk, a.dtype) + _nbytes(b_blk, b.dtype)
    for arr, blk, imap in extras:
        operands.append(arr)
        in_specs.append(spec(blk, lambda i, j, kk, imap=imap: imap(i, j)))
        block_bytes += _nbytes(blk, arr.dtype)
    out_shape, out_specs = [], []
    for shape, dtype, blk, imap in outs:
        out_shape.append(jax.ShapeDtypeStruct(shape, dtype))
        out_specs.append(spec(blk, lambda i, j, kk, imap=imap: imap(i, j)))
        block_bytes += _nbytes(blk, dtype)
    n_terms, n_extra, n_out, n_groups = len(terms), len(extras), len(outs), len(groups)
    scratch = [pltpu.VMEM((tm, tn), _F32) for _ in range(n_groups)] if nk > 1 else []
    ca = _CommArgs(list(comm), len(operands), n_out)
    n_cin, n_cout, n_acc = len(ca.operands), len(ca.out_shape), len(scratch)
    tiles = (m // tm, n // tn)
    grid = (tiles[1], tiles[0], nk) if j_outer else (tiles[0], tiles[1], nk)

    def body(*refs):
        refs = list(refs)
        term_refs = [refs.pop(0) for _ in range(2 * n_terms)]
        extra_refs = [refs.pop(0) for _ in range(n_extra)]
        cin_refs = [refs.pop(0) for _ in range(n_cin)]
        out_refs = [refs.pop(0) for _ in range(n_out)]
        cout_refs = [refs.pop(0) for _ in range(n_cout)]
        acc_refs = [refs.pop(0) for _ in range(n_acc)]
        sem_refs = refs
        g0, g1, kk = pl.program_id(0), pl.program_id(1), pl.program_id(2)
        first = jnp.logical_and(g0 == 0, g1 == 0)
        if comm:
            @pl.when(jnp.logical_and(first, kk == 0))
            def _():
                ca.start(cin_refs, cout_refs, sem_refs)
        def products(cols):
            partial, t = [], 0
            for g in groups:
                s = None
                for _, _, mode in g:
                    b_ref = term_refs[2 * t + 1]
                    b = b_ref[...] if cols is None else (b_ref[cols, :] if mode == "nt" else b_ref[:, cols])
                    d = _dot(term_refs[2 * t][...], b, mode)
                    s = d if s is None else s + d
                    t += 1
                partial.append(s)
            return partial

        if n_split > 1:
            width = tn // n_split
            chunk = lambda c: pl.ds(c * width, width)
            ahead = products(chunk(0))
            for c in range(n_split):
                done, cols = ahead, chunk(c)
                if c + 1 < n_split:
                    ahead = products(chunk(c + 1))
                view = lambda ref: ref.at[:, cols] if tuple(ref.shape) == (tm, tn) else ref
                epilogue(done, [view(r) for r in extra_refs], [view(r) for r in out_refs], first)
        elif nk == 1:
            epilogue(products(None), extra_refs, out_refs, first)
        else:
            partial = products(None)
            for acc, p in zip(acc_refs, partial):
                _accumulate(acc, p, kk == 0)

            @pl.when(kk == nk - 1)
            def _():
                epilogue([acc[...] for acc in acc_refs], extra_refs, out_refs, first)
        if comm:
            @pl.when(jnp.logical_and(jnp.logical_and(g0 == grid[0] - 1, g1 == grid[1] - 1), kk == nk - 1))
            def _():
                ca.finish(cin_refs, cout_refs, sem_refs)

    res = pl.pallas_call(
        body, name=name, grid=grid,
        in_specs=in_specs + [_ANY] * n_cin, out_specs=out_specs + [_ANY] * n_cout,
        out_shape=out_shape + ca.out_shape, scratch_shapes=scratch + ca.sems, input_output_aliases=ca.aliases,
        compiler_params=pltpu.CompilerParams(
            dimension_semantics=("arbitrary", "arbitrary", "arbitrary"),
            vmem_limit_bytes=_vmem_limit(block_bytes, n_groups * tm * tn * 4 if nk > 1 else 0)),
    )(*operands, *ca.operands)
    return list(res[:n_out]) + list(res[n_out:])


def _store_epilogue(accs, extra_refs, out_refs, first):
    for acc, ref in zip(accs, out_refs):
        ref[...] = acc.astype(ref.dtype)


def _tile_ij(i, j):
    return (i, j)


def _row_i(i, j):
    return (i, 0)


def _whole(i, j):
    return (0, 0)


def _mean(v):
    return jnp.mean(v, axis=-1, keepdims=True)


def _ln_fwd(r, g, b):
    xc = r - _mean(r)
    rstd = lax.rsqrt(_mean(xc * xc) + LN_EPS)
    xhat = xc * rstd
    return xhat * g + b, xhat, rstd


def _ln_bwd(dy, xhat, rstd, g):
    dxh = dy * g
    dr = rstd * (dxh - _mean(dxh) - xhat * _mean(dxh * xhat))
    return dr, jnp.sum(dy * xhat, axis=0, keepdims=True), jnp.sum(dy, axis=0, keepdims=True)


def _rms_fwd(a, g):
    rstd = lax.rsqrt(_mean(a * a) + RMS_EPS)
    return a * rstd * g


def _rms_bwd(dm, a, g):
    rstd = lax.rsqrt(_mean(a * a) + RMS_EPS)
    nhat = a * rstd
    dn = dm * g
    da = rstd * (dn - nhat * _mean(dn * nhat))
    return da, jnp.sum(dm * nhat, axis=0, keepdims=True)


def _lane(shape):
    return lax.broadcasted_iota(jnp.int32, shape, 1)


def _row(shape):
    return lax.broadcasted_iota(jnp.int32, shape, 0)


def _rope_tables(pos, invf):
    ang = pos.astype(_F32) * invf
    lane = _lane(ang.shape)
    in_rot = (lane % HEAD_DIM) < ROT_DIM
    first = (lane % ROT_DIM) < ROT_DIM // 2
    cos = jnp.where(in_rot, jnp.cos(ang), 1.0)
    sin = jnp.sin(ang)
    sgn = jnp.where(in_rot, jnp.where(first, -sin, sin), 0.0)
    return cos, sgn


def _rope(t, cos, sgn, sign):
    half = ROT_DIM // 2
    first = (_lane(t.shape) % ROT_DIM) < half
    partner = jnp.where(first, pltpu.roll(t, V7X_LANES - half, 1), pltpu.roll(t, half, 1))
    return t * cos + partner * (sgn * sign)


def _dup_head(t, h):
    g = t[:, 128 * (h // 2):128 * (h // 2) + 128]
    r = pltpu.roll(g, HEAD_DIM, 1)
    lo = _lane(g.shape) < HEAD_DIM
    return jnp.where(lo, g, r) if h % 2 == 0 else jnp.where(lo, r, g)


def _fold_halves(t):
    return t + pltpu.roll(t, HEAD_DIM, 1)


def _halves(t):
    lo = _lane(t.shape) < HEAD_DIM
    zero = jnp.zeros_like(t)
    return jnp.where(lo, t, zero), jnp.where(lo, zero, t)


def _band_mask(n_heads, n_keys, first_block):
    shape = (n_heads * WINDOW, n_keys)
    i = jnp.bitwise_and(_row(shape), WINDOW - 1)
    j = _lane(shape)
    valid = jnp.logical_and(j >= i + 1, j <= i + WINDOW)
    if first_block is not None:
        valid = jnp.logical_and(valid, jnp.logical_or(j >= WINDOW, jnp.logical_not(first_block)))
    return valid


def _stack_heads(pairs):
    return jnp.concatenate([half for t in pairs for half in _halves(t)], axis=0).astype(_CDT)


def _unstack_heads(t, n_pairs):
    lo = _lane((WINDOW, 128)) < HEAD_DIM
    return [jnp.where(lo, t[2 * WINDOW * i:2 * WINDOW * i + WINDOW], t[2 * WINDOW * i + WINDOW:2 * WINDOW * (i + 1)])
            for i in range(n_pairs)]


def _per_head(values):
    n_rows = len(values) * WINDOW
    block = jnp.right_shift(_row((n_rows, 1)), WINDOW.bit_length() - 1)
    out = jnp.zeros((n_rows, 1), _F32)
    for k, v in enumerate(values):
        out = jnp.where(block == k, v, out)
    return out


def _shift_down(z, halo, k):
    rows = z.shape[0]
    out = pltpu.roll(z, k, 0)
    r = _row(z.shape)
    for t in range(k):
        out = jnp.where(r == t, halo[V7X_SUBLANES - k + t:V7X_SUBLANES - k + t + 1, :], out)
    del rows
    return out


def _shift_up(z, halo, k):
    rows = z.shape[0]
    out = pltpu.roll(z, rows - k, 0)
    r = _row(z.shape)
    for t in range(k):
        out = jnp.where(r == rows - k + t, halo[t:t + 1, :], out)
    return out


class _Dims:
    def __init__(self, s, d, d_ff):
        self.s, self.d, self.d_ff = s, d, d_ff
        self.aw = d // 2
        self.cw = d - self.aw
        self.nq = self.aw // HEAD_DIM
        self.group = self.nq // N_KV_HEADS
        assert self.group % 2 == 0, "a 128-lane pair of query heads must share its kv head"
        self.inw = self.aw + 2 * KV_WIDTH + 3 * self.cw
        self.o_k = self.aw
        self.o_v = self.aw + KV_WIDTH
        self.o_cg = self.aw + 2 * KV_WIDTH
        self.o_bg = self.o_cg + self.cw
        self.o_u = self.o_bg + self.cw
        self.nb = s // WINDOW
        assert s % WINDOW == 0


def _carrying(body, n_in, n_out, n_steps, ca):
    n_cin, n_cout = len(ca.operands), len(ca.out_shape)

    def wrapped(*refs):
        refs = list(refs)
        in_refs = [refs.pop(0) for _ in range(n_in)]
        cin_refs = [refs.pop(0) for _ in range(n_cin)]
        out_refs = [refs.pop(0) for _ in range(n_out)]
        cout_refs = [refs.pop(0) for _ in range(n_cout)]
        if ca.comms:
            @pl.when(pl.program_id(0) == 0)
            def _():
                ca.start(cin_refs, cout_refs, refs)
        body(*in_refs, *out_refs)
        if ca.comms:
            @pl.when(pl.program_id(0) == n_steps - 1)
            def _():
                ca.finish(cin_refs, cout_refs, refs)

    return wrapped


def _row_kernel(name, body, rows_in, vecs_in, rows_out, vecs_out, comm=()):
    s = rows_in[0].shape[0]
    tr = _pick(s, (256, 128))
    row = lambda a: pl.BlockSpec((tr, a[1] if isinstance(a, tuple) else a.shape[1]), lambda i: (i, 0))
    vec = lambda shape: pl.BlockSpec(tuple(shape), lambda i: (0, 0))
    n_in, n_out = len(rows_in) + len(vecs_in), len(rows_out) + len(vecs_out)
    ca = _CommArgs(list(comm), n_in, n_out)
    blocks = sum(_nbytes((tr, a.shape[1]), a.dtype) for a in rows_in) + sum(_nbytes((tr, sh[1]), dt) for sh, dt in rows_out)
    res = pl.pallas_call(
        _carrying(body, n_in, n_out, s // tr, ca), name=name, grid=(s // tr,),
        in_specs=[row(a) for a in rows_in] + [vec(v.shape) for v in vecs_in] + [_ANY] * len(ca.operands),
        out_specs=[row(sh) for sh, _ in rows_out] + [vec(sh) for sh, _ in vecs_out] + [_ANY] * len(ca.out_shape),
        out_shape=[jax.ShapeDtypeStruct(sh, dt) for sh, dt in list(rows_out) + list(vecs_out)] + ca.out_shape,
        scratch_shapes=ca.sems, input_output_aliases=ca.aliases,
        compiler_params=pltpu.CompilerParams(dimension_semantics=("arbitrary",), vmem_limit_bytes=_vmem_limit(blocks)),
    )(*rows_in, *vecs_in, *ca.operands)
    return list(res)


def _ln2_loss_bwd(r2, target, gain, bias, comm=()):
    s, d = r2.shape

    def body(r_ref, t_ref, g_ref, b_ref, dr_ref, drc_ref, loss_ref, dg_ref, db_ref):
        first = pl.program_id(0) == 0
        yv, xhat, rstd = _ln_fwd(r_ref[...], g_ref[...], b_ref[...])
        err = yv - t_ref[...]
        dr2, dg, db = _ln_bwd(err * (1.0 / d), xhat, rstd, g_ref[...])
        dr_ref[...] = dr2
        drc_ref[...] = dr2.astype(_CDT)
        _accumulate(loss_ref, jnp.zeros(loss_ref.shape, _F32) + 0.5 * jnp.sum(err * err) * (1.0 / d), first)
        _accumulate(dg_ref, dg, first)
        _accumulate(db_ref, db, first)

    return _row_kernel("ln2_loss_bwd", body, [r2, target], [gain, bias], [((s, d), _F32), ((s, d), _CDT)],
                       [((V7X_SUBLANES, V7X_LANES), _F32), ((1, d), _F32), ((1, d), _F32)], comm)


def _ln1_fwd_rows(r1, gain, bias, comm=()):
    s, d = r1.shape

    def body(r_ref, g_ref, b_ref, h_ref, hc_ref, xhat_ref, rstd_ref):
        h1, xhat, rstd = _ln_fwd(r_ref[...], g_ref[...], b_ref[...])
        h_ref[...] = h1
        hc_ref[...] = h1.astype(_CDT)
        xhat_ref[...] = xhat
        rstd_ref[...] = rstd

    return _row_kernel("ln1", body, [r1], [gain, bias],
                       [((s, d), _F32), ((s, d), _CDT), ((s, d), _F32), ((s, 1), _F32)], [], comm)


def _ln1_bwd_rows(dh1, xhat, rstd, gain, comm=()):
    s, d = dh1.shape

    def body(dh_ref, xhat_ref, rstd_ref, g_ref, dr_ref, drc_ref, dg_ref, db_ref):
        first = pl.program_id(0) == 0
        dr1, dg, db = _ln_bwd(dh_ref[...], xhat_ref[...], rstd_ref[...], g_ref[...])
        dr_ref[...] = dr1
        drc_ref[...] = dr1.astype(_CDT)
        _accumulate(dg_ref, dg, first)
        _accumulate(db_ref, db, first)

    return _row_kernel("ln1_bwd", body, [dh1, xhat, rstd], [gain], [((s, d), _F32), ((s, d), _CDT)],
                       [((1, d), _F32), ((1, d), _F32)], comm)


def _mixer_fwd(dm, proj, pos, invf, sinks, g_attn, g_conv, conv_w8, comm=()):
    s, d, aw, cw, nq, inw, nb = dm.s, dm.d, dm.aw, dm.cw, dm.nq, dm.inw, dm.nb

    def body(pp_ref, pc_ref, posp_ref, posc_ref, invf_ref, sinks_ref, ga_ref, gc_ref, cw_ref,
             mixed_ref, attn_ref, lse_ref, y_ref):
        n = pl.program_id(0)
        cos_c, sgn_c = _rope_tables(posc_ref[...], invf_ref[...])
        cos_p, sgn_p = _rope_tables(posp_ref[...], invf_ref[...])
        kk = jnp.concatenate(
            [jnp.concatenate([_rope(ref[:, dm.o_k + 128 * g:dm.o_k + 128 * g + 128], c, sg, 1.0)
                              for g in range(KV_WIDTH // 128)], axis=1)
             for ref, c, sg in ((pp_ref, cos_p, sgn_p), (pc_ref, cos_c, sgn_c))], axis=0)
        vv = jnp.concatenate([pp_ref[:, dm.o_v:dm.o_v + KV_WIDTH], pc_ref[:, dm.o_v:dm.o_v + KV_WIDTH]], axis=0)
        group, pairs = dm.group, dm.group // 2
        valid = _band_mask(group, 2 * WINDOW, n == 0)
        for h in range(N_KV_HEADS):
            k2, v2 = _dup_head(kk, h).astype(_CDT), _dup_head(vv, h).astype(_CDT)
            q4 = _stack_heads([_rope(pc_ref[:, 128 * j:128 * j + 128], cos_c, sgn_c, 1.0)
                               for j in range(pairs * h, pairs * (h + 1))])
            sc = jnp.where(valid, _dot(q4, k2, "nt") * ATTN_SCALE, MASKED)
            sink = _per_head([sinks_ref[0, group * h + r] for r in range(group)])
            mx = jnp.maximum(jnp.max(sc, axis=1, keepdims=True), sink)
            p = jnp.exp(sc - mx)
            den = jnp.sum(p, axis=1, keepdims=True) + jnp.exp(sink - mx)
            out = _unstack_heads(_dot(p / den, v2, "nn"), pairs)
            lse = mx + jnp.log(den)
            for r in range(group):
                lse_ref[:, group * h + r:group * h + r + 1] = lse[WINDOW * r:WINDOW * (r + 1)]
            for i in range(pairs):
                j = pairs * h + i
                attn_ref[:, 128 * j:128 * j + 128] = out[i]
        mixed_ref[:, 0:aw] = _rms_fwd(attn_ref[...], ga_ref[...]).astype(mixed_ref.dtype)

        z = pc_ref[:, dm.o_cg:dm.o_cg + cw] * pc_ref[:, dm.o_u:dm.o_u + cw]
        top = WINDOW - V7X_SUBLANES
        halo = pp_ref[top:WINDOW, dm.o_cg:dm.o_cg + cw] * pp_ref[top:WINDOW, dm.o_u:dm.o_u + cw]
        halo = jnp.where(n == 0, jnp.zeros_like(halo), halo)
        y = cw_ref[0:1, :] * _shift_down(z, halo, 2) + cw_ref[1:2, :] * _shift_down(z, halo, 1) + cw_ref[2:3, :] * z
        y_ref[...] = y
        conv = pc_ref[:, dm.o_bg:dm.o_bg + cw] * y
        mixed_ref[:, aw:d] = _rms_fwd(conv, gc_ref[...]).astype(mixed_ref.dtype)

    prev = lambda n: (jnp.maximum(n - 1, 0), 0)
    cur = lambda n: (n, 0)
    fixed = lambda n: (0, 0)
    blocks = 2 * WINDOW * inw * 4 + WINDOW * (d * 2 + aw * 4 + cw * 4 + nq * 4)
    ca = _CommArgs(list(comm), 9, 4)
    return pl.pallas_call(
        _carrying(body, 9, 4, nb, ca), name="mixer_fwd", grid=(nb,),
        in_specs=[pl.BlockSpec((WINDOW, inw), prev), pl.BlockSpec((WINDOW, inw), cur),
                  pl.BlockSpec((WINDOW, 1), prev), pl.BlockSpec((WINDOW, 1), cur),
                  pl.BlockSpec((1, V7X_LANES), fixed), pl.BlockSpec(memory_space=pltpu.SMEM),
                  pl.BlockSpec((1, aw), fixed), pl.BlockSpec((1, cw), fixed), pl.BlockSpec((V7X_SUBLANES, cw), fixed)]
        + [_ANY] * len(ca.operands),
        out_specs=[pl.BlockSpec((WINDOW, d), cur), pl.BlockSpec((WINDOW, aw), cur),
                   pl.BlockSpec((WINDOW, nq), cur), pl.BlockSpec((WINDOW, cw), cur)] + [_ANY] * len(ca.out_shape),
        out_shape=[jax.ShapeDtypeStruct((s, d), _CDT), jax.ShapeDtypeStruct((s, aw), _F32),
                   jax.ShapeDtypeStruct((s, nq), _F32), jax.ShapeDtypeStruct((s, cw), _F32)] + ca.out_shape,
        scratch_shapes=ca.sems, input_output_aliases=ca.aliases,
        compiler_params=pltpu.CompilerParams(dimension_semantics=("arbitrary",), vmem_limit_bytes=_vmem_limit(blocks)),
    )(proj, proj, pos, pos, invf, sinks, g_attn, g_conv, conv_w8, *ca.operands)


def _mixer_bwd(dm, proj, pos, invf, sinks, g_attn, g_conv, conv_w8, dmixed, attn, lse, y, comm=()):
    s, d, aw, cw, nq, inw, nb = dm.s, dm.d, dm.aw, dm.cw, dm.nq, dm.inw, dm.nb

    def body(pp_ref, pc_ref, pn_ref, posp_ref, posc_ref, posn_ref, dmc_ref, dmn_ref, ac_ref, an_ref,
             lsec_ref, lsen_ref, yc_ref, yn_ref, invf_ref, sinks_ref, ga_ref, gc_ref, cw_ref,
             dproj_ref, dga_ref, dgc_ref, dsinks_ref, dcw_ref):
        n = pl.program_id(0)
        first = n == 0
        has_next = n < nb - 1
        cos_p, sgn_p = _rope_tables(posp_ref[...], invf_ref[...])
        cos_c, sgn_c = _rope_tables(posc_ref[...], invf_ref[...])
        cos_n, sgn_n = _rope_tables(posn_ref[...], invf_ref[...])

        da_c, dga = _rms_bwd(dmc_ref[:, 0:aw], ac_ref[...], ga_ref[...])
        da_n, _ = _rms_bwd(dmn_ref[:, 0:aw], an_ref[...], ga_ref[...])
        _accumulate(dga_ref, dga, first)
        kk = jnp.concatenate(
            [jnp.concatenate([_rope(ref[:, dm.o_k + 128 * g:dm.o_k + 128 * g + 128], c, sg, 1.0)
                              for g in range(KV_WIDTH // 128)], axis=1)
             for ref, c, sg in ((pp_ref, cos_p, sgn_p), (pc_ref, cos_c, sgn_c))], axis=0)
        vv = jnp.concatenate([pp_ref[:, dm.o_v:dm.o_v + KV_WIDTH], pc_ref[:, dm.o_v:dm.o_v + KV_WIDTH]], axis=0)
        group, pairs = dm.group, dm.group // 2
        valid_c = _band_mask(group, 2 * WINDOW, first)
        valid_n = jnp.logical_and(_band_mask(group, WINDOW, None), has_next)
        dk2, dv2 = [], []
        dsinks = jnp.zeros((1, nq), _F32)
        head_lane = _lane((1, nq))

        def stacked(q_ref, cos, sgn, da, o_ref, lse_ref_, h):
            cols = [slice(128 * j, 128 * j + 128) for j in range(pairs * h, pairs * (h + 1))]
            q4 = _stack_heads([_rope(q_ref[:, c], cos, sgn, 1.0) for c in cols])
            do4 = _stack_heads([da[:, c] for c in cols])
            lo = _lane((WINDOW, 128)) < HEAD_DIM
            deltas = []
            for c in cols:
                prod = o_ref[:, c] * da[:, c]
                deltas += [jnp.sum(jnp.where(lo, prod, 0.0), axis=1, keepdims=True),
                           jnp.sum(jnp.where(lo, 0.0, prod), axis=1, keepdims=True)]
            lse4 = jnp.concatenate([lse_ref_[:, group * h + r:group * h + r + 1] for r in range(group)], axis=0)
            return q4, do4, lse4, jnp.concatenate(deltas, axis=0)

        def scores_bwd(q4, do4, lse4, delta4, keys, vals, valid):
            sc = _dot(q4, keys, "nt") * ATTN_SCALE
            p = jnp.exp(jnp.where(valid, sc - lse4, MASKED))
            return p.astype(_CDT), (p * (_dot(do4, vals, "nt") - delta4) * ATTN_SCALE).astype(_CDT)

        for h in range(N_KV_HEADS):
            k2, v2 = _dup_head(kk, h).astype(_CDT), _dup_head(vv, h).astype(_CDT)
            q4, do4, lse4, delta4 = stacked(pc_ref, cos_c, sgn_c, da_c, ac_ref, lsec_ref, h)
            p, ds = scores_bwd(q4, do4, lse4, delta4, k2, v2, valid_c)
            for i, dq in enumerate(_unstack_heads(_dot(ds, k2, "nn"), pairs)):
                j = pairs * h + i
                dproj_ref[:, 128 * j:128 * j + 128] = _rope(dq, cos_c, sgn_c, -1.0).astype(dproj_ref.dtype)
            dk = _dot(ds, q4, "tn")[WINDOW:2 * WINDOW, :]
            dv = _dot(p, do4, "tn")[WINDOW:2 * WINDOW, :]
            sink4 = _per_head([sinks_ref[0, group * h + r] for r in range(group)])
            loss_sink = jnp.exp(sink4 - lse4) * delta4
            for r in range(group):
                dsinks = dsinks + jnp.where(head_lane == group * h + r,
                                            -jnp.sum(loss_sink[WINDOW * r:WINDOW * (r + 1)]), 0.0)
            q4, do4, lse4, delta4 = stacked(pn_ref, cos_n, sgn_n, da_n, an_ref, lsen_ref, h)
            p, ds = scores_bwd(q4, do4, lse4, delta4, k2[WINDOW:2 * WINDOW, :], v2[WINDOW:2 * WINDOW, :], valid_n)
            dk2.append(dk + _dot(ds, q4, "tn"))
            dv2.append(dv + _dot(p, do4, "tn"))
        _accumulate(dsinks_ref, dsinks, first)
        lo = _lane((WINDOW, 128)) < HEAD_DIM
        for g in range(KV_WIDTH // 128):
            dk = jnp.where(lo, _fold_halves(dk2[2 * g]), _fold_halves(dk2[2 * g + 1]))
            dv = jnp.where(lo, _fold_halves(dv2[2 * g]), _fold_halves(dv2[2 * g + 1]))
            dproj_ref[:, dm.o_k + 128 * g:dm.o_k + 128 * g + 128] = _rope(dk, cos_c, sgn_c, -1.0).astype(dproj_ref.dtype)
            dproj_ref[:, dm.o_v + 128 * g:dm.o_v + 128 * g + 128] = dv.astype(dproj_ref.dtype)

        bg = pc_ref[:, dm.o_bg:dm.o_bg + cw]
        yc = yc_ref[...]
        dconv, dgc = _rms_bwd(dmc_ref[:, aw:d], bg * yc, gc_ref[...])
        _accumulate(dgc_ref, dgc, first)
        dproj_ref[:, dm.o_bg:dm.o_bg + cw] = (dconv * yc).astype(dproj_ref.dtype)
        dy = dconv * bg
        bg_n = pn_ref[0:V7X_SUBLANES, dm.o_bg:dm.o_bg + cw]
        dconv_n, _ = _rms_bwd(dmn_ref[0:V7X_SUBLANES, aw:d], bg_n * yn_ref[...], gc_ref[...])
        halo = jnp.where(has_next, dconv_n * bg_n, 0.0)
        dy1 = _shift_up(dy, halo, 1)
        dy2 = _shift_up(dy, halo, 2)
        dz = cw_ref[2:3, :] * dy + cw_ref[1:2, :] * dy1 + cw_ref[0:1, :] * dy2
        cg = pc_ref[:, dm.o_cg:dm.o_cg + cw]
        u = pc_ref[:, dm.o_u:dm.o_u + cw]
        dproj_ref[:, dm.o_cg:dm.o_cg + cw] = (dz * u).astype(dproj_ref.dtype)
        dproj_ref[:, dm.o_u:dm.o_u + cw] = (dz * cg).astype(dproj_ref.dtype)
        z = cg * u
        dcw = jnp.concatenate(
            [jnp.sum(z * t, axis=0, keepdims=True) for t in (dy2, dy1, dy)]
            + [jnp.zeros((V7X_SUBLANES - 3, cw), _F32)], axis=0)
        _accumulate(dcw_ref, dcw, first)

    prev = lambda n: (jnp.maximum(n - 1, 0), 0)
    cur = lambda n: (n, 0)
    nxt = lambda n: (jnp.minimum(n + 1, nb - 1), 0)
    nxt8 = lambda n: (jnp.minimum((n + 1) * (WINDOW // V7X_SUBLANES), s // V7X_SUBLANES - 1), 0)
    fixed = lambda n: (0, 0)
    blocks = WINDOW * (3 * inw * 4 + 2 * d * 4 + 2 * aw * 4 + cw * 4 + inw * 2)
    ca = _CommArgs(list(comm), 19, 5)
    return pl.pallas_call(
        _carrying(body, 19, 5, nb, ca), name="mixer_bwd", grid=(nb,),
        in_specs=[pl.BlockSpec((WINDOW, inw), prev), pl.BlockSpec((WINDOW, inw), cur), pl.BlockSpec((WINDOW, inw), nxt),
                  pl.BlockSpec((WINDOW, 1), prev), pl.BlockSpec((WINDOW, 1), cur), pl.BlockSpec((WINDOW, 1), nxt),
                  pl.BlockSpec((WINDOW, d), cur), pl.BlockSpec((WINDOW, d), nxt),
                  pl.BlockSpec((WINDOW, aw), cur), pl.BlockSpec((WINDOW, aw), nxt),
                  pl.BlockSpec((WINDOW, nq), cur), pl.BlockSpec((WINDOW, nq), nxt),
                  pl.BlockSpec((WINDOW, cw), cur), pl.BlockSpec((V7X_SUBLANES, cw), nxt8),
                  pl.BlockSpec((1, V7X_LANES), fixed), pl.BlockSpec(memory_space=pltpu.SMEM),
                  pl.BlockSpec((1, aw), fixed), pl.BlockSpec((1, cw), fixed), pl.BlockSpec((V7X_SUBLANES, cw), fixed)]
        + [_ANY] * len(ca.operands),
        out_specs=[pl.BlockSpec((WINDOW, inw), cur), pl.BlockSpec((1, aw), fixed), pl.BlockSpec((1, cw), fixed),
                   pl.BlockSpec((1, nq), fixed), pl.BlockSpec((V7X_SUBLANES, cw), fixed)] + [_ANY] * len(ca.out_shape),
        out_shape=[jax.ShapeDtypeStruct((s, inw), _CDT), jax.ShapeDtypeStruct((1, aw), _F32),
                   jax.ShapeDtypeStruct((1, cw), _F32), jax.ShapeDtypeStruct((1, nq), _F32),
                   jax.ShapeDtypeStruct((V7X_SUBLANES, cw), _F32)] + ca.out_shape,
        scratch_shapes=ca.sems, input_output_aliases=ca.aliases,
        compiler_params=pltpu.CompilerParams(dimension_semantics=("arbitrary",), vmem_limit_bytes=_vmem_limit(blocks)),
    )(proj, proj, proj, pos, pos, pos, dmixed, dmixed, attn, attn, lse, lse, y, y, invf, sinks, g_attn, g_conv, conv_w8,
      *ca.operands)


def _position():
    return lax.axis_index("x"), lax.axis_index("y"), lax.axis_index("c")


def _linear(px, py, pc):
    return 4 * px + 2 * py + pc


def _comm_kernel(name, comm):
    ca = _CommArgs(list(comm), 0, 0)
    n_cin, n_cout = len(ca.operands), len(ca.out_shape)

    def body(*refs):
        cin, cout, sems = refs[:n_cin], refs[n_cin:n_cin + n_cout], refs[n_cin + n_cout:]
        ca.start(cin, cout, sems)
        ca.finish(cin, cout, sems)

    return pl.pallas_call(
        body, name=name, out_shape=ca.out_shape, in_specs=[_ANY] * n_cin, out_specs=[_ANY] * n_cout,
        scratch_shapes=ca.sems, input_output_aliases=ca.aliases,
    )(*ca.operands)


def _gather_op(units):
    n = len(units)
    inputs, outputs, aliases = [], [], {}
    for shard, _, _, _ in units:
        inputs.append(shard)
        outputs.append(jax.ShapeDtypeStruct((N_DEV * shard.shape[0], shard.shape[1]), shard.dtype))
    for u, (_, buf, _, _) in enumerate(units):
        if buf is not None:
            aliases[len(inputs)] = u
            inputs.append(buf)

    def plan(ins, outs, sems):
        send_sems, recv_sems, local_sems = sems
        x, y, c = _position()
        me, sibling = (x, y, c), (x, y, 1 - c)
        chips = [(1 - x, y), (x, 1 - y), (1 - x, 1 - y)]

        def rows(u, px, py, pc):
            shard, _, r0, r1 = units[u]
            return outs[u].at[pl.ds(pl.multiple_of(_linear(px, py, pc) * shard.shape[0] + r0, 16), r1 - r0), :]

        def own(u):
            _, _, r0, r1 = units[u]
            return ins[u].at[pl.ds(r0, r1 - r0), :]

        def copy(u, k, block, to, src=None):
            return pltpu.make_async_remote_copy(
                src_ref=rows(u, *block) if src is None else src, dst_ref=rows(u, *block),
                send_sem=send_sems.at[u, k], recv_sem=recv_sems.at[u, k], device_id=to, device_id_type=_MESH)

        mine = [pltpu.make_async_copy(own(u), rows(u, *me), local_sems.at[u]) for u in range(n)]
        first = []
        for u in range(n):
            first.append(copy(u, 0, me, sibling, src=own(u)))
            first += [copy(u, 1 + j, me, (*chip, c), src=own(u)) for j, chip in enumerate(chips)]
        passed = [[copy(u, 4 + j, (*chip, c), sibling) for j, chip in enumerate(chips)] for u in range(n)]
        landed = [[copy(u, 1 + j, (*chip, c), me) for j, chip in enumerate(chips)] for u in range(n)]
        rest = [[copy(u, 0, sibling, me)] + [copy(u, 4 + j, (*chip, 1 - c), me) for j, chip in enumerate(chips)]
                for u in range(n)]
        return mine, first, passed, landed, rest

    def start(ins, outs, sems):
        mine, first, _, _, _ = plan(ins, outs, sems)
        for cp in mine + first:
            cp.start()

    def finish(ins, outs, sems):
        mine, first, passed, landed, rest = plan(ins, outs, sems)
        for u in range(n):
            for arrived, onward in zip(landed[u], passed[u]):
                arrived.wait_recv()
                onward.start()
        for u in range(n):
            for cp in rest[u]:
                cp.wait_recv()
        for cp in first + [cp for row in passed for cp in row]:
            cp.wait_send()
        for cp in mine:
            cp.wait()

    sems = [pltpu.SemaphoreType.DMA((n, 7)), pltpu.SemaphoreType.DMA((n, 7)), pltpu.SemaphoreType.DMA((n,))]
    return _Comm(inputs, outputs, aliases, sems, start, finish)


def _peers(x, y, c):
    out = []
    for k in range(1, N_DEV):
        fx, fy, fc = (k >> 2) & 1, (k >> 1) & 1, k & 1
        out.append((1 - x if fx else x, 1 - y if fy else y, 1 - c if fc else c))
    return out


def _exchange_op(partials):
    n = len(partials)
    outputs = [jax.ShapeDtypeStruct((4, p.shape[0] // N_DEV, p.shape[1]), p.dtype) for p in partials]

    def plan(ins, outs, sems):
        send_sems, recv_sems = sems
        x, y, c = _position()
        out = []
        for a in range(n):
            r = outs[a].shape[1]
            for ch in range(4):
                out.append(pltpu.make_async_remote_copy(
                    src_ref=ins[a].at[pl.ds(pl.multiple_of((2 * ch + 1 - c) * r, 16), r), :], dst_ref=outs[a].at[ch],
                    send_sem=send_sems.at[a, ch], recv_sem=recv_sems.at[a, ch], device_id=(x, y, 1 - c),
                    device_id_type=_MESH))
        return out

    def start(ins, outs, sems):
        for cp in plan(ins, outs, sems):
            cp.start()

    def finish(ins, outs, sems):
        copies = plan(ins, outs, sems)
        for cp in copies:
            cp.wait_recv()
        for cp in copies:
            cp.wait_send()

    sems = [pltpu.SemaphoreType.DMA((n, 4)), pltpu.SemaphoreType.DMA((n, 4))]
    return _Comm(list(partials), outputs, {}, sems, start, finish)


def _chip_send_op(units):
    n = len(units)
    inputs, outputs, aliases = [], [], {}
    for q, _, _, _ in units:
        inputs.append(q)
        outputs.append(jax.ShapeDtypeStruct(q.shape, q.dtype))
    for u, (_, buf, _, _) in enumerate(units):
        if buf is not None:
            aliases[len(inputs)] = u
            inputs.append(buf)

    def plan(ins, outs, sems):
        send_sems, recv_sems, local_sems = sems
        x, y, c = _position()
        my_chip = 2 * x + y
        chips = [(1 - x, y), (x, 1 - y), (1 - x, 1 - y)]
        mine, sends, arrivals = [], [], []
        for u, (_, _, r0, r1) in enumerate(units):
            span = pl.ds(r0, r1 - r0)
            mine.append(pltpu.make_async_copy(ins[u].at[my_chip, span, :], outs[u].at[my_chip, span, :], local_sems.at[u]))
            for k, (px, py) in enumerate(chips):
                sends.append(pltpu.make_async_remote_copy(
                    src_ref=ins[u].at[2 * px + py, span, :], dst_ref=outs[u].at[my_chip, span, :],
                    send_sem=send_sems.at[u, k], recv_sem=recv_sems.at[u, k], device_id=(px, py, c), device_id_type=_MESH))
                arrivals.append(pltpu.make_async_remote_copy(
                    src_ref=ins[u].at[my_chip, span, :], dst_ref=outs[u].at[2 * px + py, span, :],
                    send_sem=send_sems.at[u, k], recv_sem=recv_sems.at[u, k], device_id=(px, py, c), device_id_type=_MESH))
        return mine, sends, arrivals

    def start(ins, outs, sems):
        mine, sends, _ = plan(ins, outs, sems)
        for cp in mine + sends:
            cp.start()

    def finish(ins, outs, sems):
        mine, sends, arrivals = plan(ins, outs, sems)
        for cp in arrivals:
            cp.wait_recv()
        for cp in sends:
            cp.wait_send()
        for cp in mine:
            cp.wait()

    sems = [pltpu.SemaphoreType.DMA((n, 3)), pltpu.SemaphoreType.DMA((n, 3)), pltpu.SemaphoreType.DMA((n,))]
    return _Comm(inputs, outputs, aliases, sems, start, finish)


def _pair_sum(name, partial, received):
    _, rows, cols = received.shape
    tr = _pick(rows, (352, 288, 256, 128, 64, 32, 16))
    p4 = partial.reshape(4, 2, rows, cols)
    kind = jnp.reshape(lax.axis_index("c"), (1,)).astype(jnp.int32)

    def body(kind_ref, p_ref, r_ref, o_ref):
        o_ref[0] = (p_ref[0, 0].astype(_F32) + r_ref[0].astype(_F32)).astype(o_ref.dtype)

    return pl.pallas_call(
        body, name=name,
        grid_spec=pltpu.PrefetchScalarGridSpec(
            num_scalar_prefetch=1, grid=(4, rows // tr),
            in_specs=[pl.BlockSpec((1, 1, tr, cols), lambda ch, i, kind_ref: (ch, kind_ref[0], i, 0)),
                      pl.BlockSpec((1, tr, cols), lambda ch, i, kind_ref: (ch, i, 0))],
            out_specs=pl.BlockSpec((1, tr, cols), lambda ch, i, kind_ref: (ch, i, 0))),
        out_shape=jax.ShapeDtypeStruct(received.shape, received.dtype),
        compiler_params=pltpu.CompilerParams(dimension_semantics=("arbitrary", "arbitrary")),
    )(kind, p4, received)


def _all_reduce_small(name, v):
    rows = v.shape[0]

    def body(v_ref, out_ref, land_ref, send_sems, recv_sems):
        x, y, c = _position()
        me = _linear(x, y, c)
        peers = _peers(x, y, c)
        land_ref[me] = v_ref[...]
        sends = [pltpu.make_async_remote_copy(
            src_ref=v_ref, dst_ref=land_ref.at[me], send_sem=send_sems.at[k], recv_sem=recv_sems.at[k],
            device_id=peer, device_id_type=_MESH) for k, peer in enumerate(peers)]
        for cp in sends:
            cp.start()
        for k, peer in enumerate(peers):
            pltpu.make_async_remote_copy(
                src_ref=v_ref, dst_ref=land_ref.at[_linear(*peer)], send_sem=send_sems.at[k], recv_sem=recv_sems.at[k],
                device_id=peer, device_id_type=_MESH).wait_recv()
        for cp in sends:
            cp.wait_send()
        total = land_ref[0]
        for s in range(1, N_DEV):
            total = total + land_ref[s]
        out_ref[...] = total

    return pl.pallas_call(
        body, name=name, out_shape=jax.ShapeDtypeStruct(v.shape, _F32),
        in_specs=[pl.BlockSpec(memory_space=pltpu.VMEM)], out_specs=pl.BlockSpec(memory_space=pltpu.VMEM),
        scratch_shapes=[pltpu.VMEM((N_DEV, rows, V7X_LANES), _F32), pltpu.SemaphoreType.DMA((7,)), pltpu.SemaphoreType.DMA((7,))],
    )(v)


def _adamw(name, w, slots, m, v):
    rows, cols = w.shape
    n_slots = slots.shape[0]
    tr = _pick(rows, (176, 144, 128, 64, 32, 16, 8))

    def body(w_ref, s_ref, m_ref, v_ref, g_ref, d_ref, nm_ref, nv_ref):
        g = s_ref[0].astype(_F32)
        for k in range(1, n_slots):
            g = g + s_ref[k].astype(_F32)
        nm = ADAM_B1 * m_ref[...] + (1.0 - ADAM_B1) * g
        nv = ADAM_B2 * v_ref[...] + (1.0 - ADAM_B2) * (g * g)
        m_hat = nm / (1.0 - ADAM_B1 ** ADAM_STEP)
        v_hat = nv / (1.0 - ADAM_B2 ** ADAM_STEP)
        g_ref[...] = g
        d_ref[...] = -ADAM_LR * (m_hat / (jnp.sqrt(v_hat) + ADAM_EPS) + ADAM_WD * w_ref[...])
        nm_ref[...] = nm
        nv_ref[...] = nv

    spec = pl.BlockSpec((tr, cols), lambda i: (i, 0))
    blocks = 7 * tr * cols * 4 + _nbytes((n_slots, tr, cols), slots.dtype)
    return pl.pallas_call(
        body, name=name, grid=(rows // tr,),
        in_specs=[spec, pl.BlockSpec((n_slots, tr, cols), lambda i: (0, i, 0)), spec, spec], out_specs=[spec] * 4,
        out_shape=[jax.ShapeDtypeStruct((rows, cols), _F32)] * 4,
        compiler_params=pltpu.CompilerParams(dimension_semantics=("arbitrary",), vmem_limit_bytes=_vmem_limit(blocks)),
    )(w, slots, m, v)


def _pad_rows(a, rows):
    return jnp.pad(a, ((0, rows - a.shape[0]), (0, 0)))


def _pack(parts):
    rows, spans, at = [], [], 0
    for p in parts:
        p = p.reshape(-1)
        r = -(-p.shape[0] // V7X_LANES)
        rows.append(jnp.pad(p, (0, r * V7X_LANES - p.shape[0])).reshape(r, V7X_LANES))
        spans.append((at, r, p.shape[0]))
        at += r
    packed = jnp.concatenate(rows, axis=0)
    return _pad_rows(packed, -(-at // V7X_SUBLANES) * V7X_SUBLANES), spans


def _unpack(packed, spans, shapes):
    return [packed[at:at + r].reshape(-1)[:size].reshape(shape) for (at, r, size), shape in zip(spans, shapes)]


def kernel(x, positions, w_in, conv_w, sinks, g_attn, g_conv, w_out, ln1_g, ln1_b, w_gate, w_up, w_down, ln2_g, ln2_b, loss_target, m_w_in, m_conv_w, m_sinks, m_g_attn, m_g_conv, m_w_out, m_ln1_g, m_ln1_b, m_w_gate, m_w_up, m_w_down, m_ln2_g, m_ln2_b, v_w_in, v_conv_w, v_sinks, v_g_attn, v_g_conv, v_w_out, v_ln1_g, v_ln1_b, v_w_gate, v_w_up, v_w_down, v_ln2_g, v_ln2_b):
    _, s, d = x.shape
    d_ff = N_DEV * w_gate.shape[2]
    dm = _Dims(s, d, d_ff)
    aw, cw, nq, inw = dm.aw, dm.cw, dm.nq, dm.inw
    x2 = x[0]
    pos = positions[0].reshape(s, 1)
    inv_freq = ROPE_THETA ** (-jnp.arange(0, ROT_DIM, 2, dtype=_F32) / ROT_DIM)
    invf = jnp.tile(inv_freq, V7X_LANES // (ROT_DIM // 2)).reshape(1, V7X_LANES)

    conv_cols = conv_w.shape[2]
    sh_in, sh_out = w_in[0].T.astype(_CDT), w_out[0].astype(_CDT)
    sh_gate, sh_up, sh_down = w_gate[0].T.astype(_CDT), w_up[0].T.astype(_CDT), w_down[0].astype(_CDT)
    r_in, r_out, r_ff = sh_in.shape[0], sh_out.shape[0], sh_gate.shape[0]
    q_ff = r_ff // 4
    assert q_ff % 16 == 0
    def cast_body(x_ref, o_ref):
        o_ref[...] = x_ref[...].astype(_CDT)

    x_c, w_in_t, conv_all = _row_kernel("cast_x_gather_w_in", cast_body, [x2], [], [((s, d), _CDT)], [], comm=[
        _gather_op([(sh_in, None, 0, r_in), (_pad_rows(conv_w[0], 16), None, 0, 16)])])
    conv_full = conv_all.reshape(N_DEV, 16, conv_cols)[:, :3, :].transpose(1, 0, 2).reshape(3, cw)
    conv_w8 = _pad_rows(conv_full, V7X_SUBLANES)

    tm = _pick(s, (1024, 512, 256, 128))
    tn_in = _pick(inw, (512, 256, 128))
    tn_ff = _pick(d_ff, (512, 256, 128))
    tr = _pick(s, (512, 256, 128))

    proj, w_out_f, w_gate_t = _matmul(
        "proj", [[(x_c, w_in_t, "nt")]], s, inw, d, tm, tn_in, d, [],
        [((s, inw), _F32, (tm, tn_in), _tile_ij)], _store_epilogue,
        comm=[_gather_op([(sh_out, None, 0, r_out), (sh_gate, None, 0, 2 * q_ff)])])
    mixed, attn, lse, y_conv, w_gate_t, w_up_t = _mixer_fwd(
        dm, proj, pos, invf, sinks, g_attn, g_conv, conv_w8,
        comm=[_gather_op([(sh_gate, w_gate_t, 2 * q_ff, r_ff), (sh_up, None, 0, 2 * q_ff)])])

    def residual_epilogue(accs, ex, out, first):
        out[0][...] = DEEPNORM_ALPHA * ex[0][...] + accs[0]

    tn_d = _pick(d, (512,))
    r1, w_up_t = _matmul(
        "out_proj", [[(mixed, w_out_f, "nn")]], s, d, d, tm, tn_d, d, [(x2, (tm, tn_d), _tile_ij)],
        [((s, d), _F32, (tm, tn_d), _tile_ij)], residual_epilogue,
        comm=[_gather_op([(sh_up, w_up_t, 2 * q_ff, 3 * q_ff)])])
    h1, h1_c, xhat1, rstd1, w_up_t = _ln1_fwd_rows(
        r1, ln1_g, ln1_b, comm=[_gather_op([(sh_up, w_up_t, 3 * q_ff, r_ff)])])

    def swiglu_epilogue(accs, ex, out, first):
        gate_v, up_v = accs
        out[0][...] = gate_v
        out[1][...] = up_v
        out[2][...] = (gate_v * jax.nn.sigmoid(gate_v) * up_v).astype(_CDT)

    gate, up, act, w_down_f = _matmul(
        "gate_up", [[(h1_c, w_gate_t, "nt")], [(h1_c, w_up_t, "nt")]], s, d_ff, d, tm, tn_ff, d, [],
        [((s, d_ff), _F32, (tm, tn_ff), _tile_ij), ((s, d_ff), _F32, (tm, tn_ff), _tile_ij),
         ((s, d_ff), _CDT, (tm, tn_ff), _tile_ij)], swiglu_epilogue,
        comm=[_gather_op([(sh_down, None, 0, r_ff)])])

    (r2,) = _matmul("down", [[(act, w_down_f, "nn")]], s, d, d_ff, tr, tn_d, d_ff, [(h1, (tr, tn_d), _tile_ij)],
                    [((s, d), _F32, (tr, tn_d), _tile_ij)], residual_epilogue)
    dr2, dr2_c, loss_acc, d_ln2_g, d_ln2_b = _ln2_loss_bwd(r2, loss_target[0], ln2_g, ln2_b)

    def swiglu_bwd_epilogue(accs, ex, out, first):
        gate_v, up_v = ex[0][...], ex[1][...]
        sig = jax.nn.sigmoid(gate_v)
        out[0][...] = (accs[0] * up_v * (sig * (1.0 + gate_v * (1.0 - sig)))).astype(_CDT)
        out[1][...] = (accs[0] * (gate_v * sig)).astype(_CDT)

    dgate, dup = _matmul(
        "dact", [[(dr2_c, w_down_f, "nt")]], s, d_ff, d, tm, tn_ff, d,
        [(gate, (tm, tn_ff), _tile_ij), (up, (tm, tn_ff), _tile_ij)],
        [((s, d_ff), _CDT, (tm, tn_ff), _tile_ij), ((s, d_ff), _CDT, (tm, tn_ff), _tile_ij)], swiglu_bwd_epilogue,
        n_split=2)
    def weight_grad(name, a, b, comm=()):
        rows = a.shape[1]
        tw, tn_w = _pick(rows, (512, 256, 128)), _pick(d, (1024, 512))
        return _matmul(name, [[(a, b, "tn")]], rows, d, s, tw, tn_w, s, [],
                       [((rows, d), _CDT, (tw, tn_w), _tile_ij)], _store_epilogue, comm=comm, j_outer=True)

    (dw_down,) = weight_grad("dw_down", act, dr2_c)
    dw_gate_t, x_down = weight_grad("dw_gate", dgate, h1_c, comm=[_exchange_op([dw_down])])
    q_down = _pair_sum("chip_sum_w_down", dw_down, x_down)
    dw_up_t, l_down, x_gate = weight_grad(
        "dw_up", dup, h1_c, comm=[_chip_send_op([(q_down, None, 0, 2 * q_ff)]), _exchange_op([dw_gate_t])])
    q_gate = _pair_sum("chip_sum_w_gate", dw_gate_t, x_gate)

    tn_h = _pick(d, (256,))
    dh1, l_down, l_gate, x_up = _matmul(
        "dh1", [[(dgate, w_gate_t, "nn"), (dup, w_up_t, "nn")]], s, d, d_ff, tr, tn_h, d_ff,
        [(dr2, (tr, tn_h), _tile_ij)], [((s, d), _F32, (tr, tn_h), _tile_ij)], residual_epilogue,
        comm=[_chip_send_op([(q_down, l_down, 2 * q_ff, r_ff), (q_gate, None, 0, r_ff)]), _exchange_op([dw_up_t])])
    q_up = _pair_sum("chip_sum_w_up", dw_up_t, x_up)
    dr1, dr1_c, d_ln1_g, d_ln1_b = _ln1_bwd_rows(dh1, xhat1, rstd1, ln1_g)
    (dmixed,) = _matmul("dmixed", [[(dr1_c, w_out_f, "nt")]], s, d, d, tm, tn_d, d, [],
                        [((s, d), _F32, (tm, tn_d), _tile_ij)], _store_epilogue)
    (dw_out,) = weight_grad("dw_out", mixed, dr1_c)
    dproj, d_g_attn, d_g_conv, d_sinks, d_conv8, l_up, x_out = _mixer_bwd(
        dm, proj, pos, invf, sinks, g_attn, g_conv, conv_w8, dmixed, attn, lse, y_conv,
        comm=[_chip_send_op([(q_up, None, 0, r_ff)]), _exchange_op([dw_out])])
    q_out = _pair_sum("chip_sum_w_out", dw_out, x_out)
    dw_in_t, l_out = weight_grad("dw_in", dproj, x_c, comm=[_chip_send_op([(q_out, None, 0, r_out)])])
    (x_in,) = _comm_kernel("exchange_w_in", [_exchange_op([dw_in_t])])
    q_in = _pair_sum("chip_sum_w_in", dw_in_t, x_in)

    grad_x, l_in = _matmul("dx", [[(dproj, w_in_t, "nn")]], s, d, inw, tr, tn_d, inw,
                           [(dr1, (tr, tn_d), _tile_ij)], [((s, d), _F32, (tr, tn_d), _tile_ij)], residual_epilogue,
                           comm=[_chip_send_op([(q_in, None, 0, r_in)])])

    small_parts = [d_conv8[:3], d_sinks, d_g_attn, d_g_conv, d_ln1_g, d_ln1_b, d_ln2_g, d_ln2_b]
    packed, spans = _pack(small_parts)
    reduced = _unpack(_all_reduce_small("reduce_small", packed), spans, [p.shape for p in small_parts])
    g_conv_full, g_sinks, g_g_attn, g_g_conv, g_ln1_g, g_ln1_b, g_ln2_g, g_ln2_b = reduced
    me = _linear(*_position())
    g_conv_w = lax.dynamic_slice(g_conv_full, (0, me * conv_cols), (3, conv_cols))
    loss = lax.psum(loss_acc[0, 0], ("x", "y", "c"))

    big = {"w_in": (w_in[0].T, l_in, m_w_in[0].T, v_w_in[0].T), "w_out": (w_out[0], l_out, m_w_out[0], v_w_out[0]),
           "w_gate": (w_gate[0].T, l_gate, m_w_gate[0].T, v_w_gate[0].T),
           "w_up": (w_up[0].T, l_up, m_w_up[0].T, v_w_up[0].T), "w_down": (w_down[0], l_down, m_w_down[0], v_w_down[0])}
    res = {nm: tuple(_adamw(f"adamw_{nm}", w, slots, m, v)) for nm, (w, slots, m, v) in big.items()}
    for nm in ("w_in", "w_gate", "w_up"):
        res[nm] = tuple(a.T for a in res[nm])
    small_names = ["conv_w", "sinks", "g_attn", "g_conv", "ln1_g", "ln1_b", "ln2_g", "ln2_b"]
    small_w = [conv_w, sinks, g_attn, g_conv, ln1_g, ln1_b, ln2_g, ln2_b]
    small_g = [g_conv_w[None], g_sinks, g_g_attn, g_g_conv, g_ln1_g, g_ln1_b, g_ln2_g, g_ln2_b]
    small_m = [m_conv_w, m_sinks, m_g_attn, m_g_conv, m_ln1_g, m_ln1_b, m_ln2_g, m_ln2_b]
    small_v = [v_conv_w, v_sinks, v_g_attn, v_g_conv, v_ln1_g, v_ln1_b, v_ln2_g, v_ln2_b]
    pw, sp = _pack(small_w)
    pg, _ = _pack(small_g)
    pm, _ = _pack(small_m)
    pv, _ = _pack(small_v)
    shapes = [w.shape for w in small_w]
    _, sd, sm, sv = [_unpack(p, sp, shapes) for p in _adamw("adamw_small", pw, pg[None], pm, pv)]
    for i, nm in enumerate(small_names):
        res[nm] = (small_g[i].reshape(shapes[i]), sd[i], sm[i], sv[i])

    order = ["w_in", "conv_w", "sinks", "g_attn", "g_conv", "w_out", "ln1_g", "ln1_b", "w_gate", "w_up", "w_down", "ln2_g", "ln2_b"]

    def lead(a, nm):
        return a[None] if nm in big else a

    return (loss, grad_x[None],
            *[lead(res[nm][0], nm) for nm in order], *[lead(res[nm][1], nm) for nm in order],
            *[lead(res[nm][2], nm) for nm in order], *[lead(res[nm][3], nm) for nm in order])
```

```python
import functools

import jax
import jax.numpy as jnp
from jax import lax
from jax.experimental import pallas as pl
from jax.experimental.pallas import tpu as pltpu

_F32 = jnp.float32
_CDT = jnp.bfloat16

HEAD_DIM = 64
WINDOW = 128
N_KV_HEADS = 4
KV_WIDTH = N_KV_HEADS * HEAD_DIM
ROT_DIM = HEAD_DIM // 4
ROPE_THETA = 500000.0
ATTN_SCALE = HEAD_DIM ** -0.5
DEPTH = 1
DEEPNORM_ALPHA = (2 * DEPTH) ** 0.25
LN_EPS = 1e-5
RMS_EPS = 1e-6
ADAM_LR = 0.001
ADAM_B1 = 0.9
ADAM_B2 = 0.999
ADAM_EPS = 1e-08
ADAM_WD = 0.01
ADAM_STEP = 10
N_DEV = 8
MASKED = -1e30

V7X_VMEM_BYTES = 64 * 1024 * 1024
V7X_LANES = 128
V7X_SUBLANES = 8
_MESH = pl.DeviceIdType.MESH
_ANY = pl.BlockSpec(memory_space=pl.ANY)


def _vmem_limit(block_bytes, scratch_bytes=0):
    want = 2 * block_bytes + scratch_bytes + 16 * 1024 * 1024
    return int(min(max(want, 32 * 1024 * 1024), V7X_VMEM_BYTES - 8 * 1024 * 1024))


def _nbytes(shape, dtype):
    n = 1
    for s in shape:
        n *= s
    return n * jnp.dtype(dtype).itemsize


def _pick(n, candidates):
    for c in candidates:
        if n % c == 0:
            return c
    raise ValueError(f"no tile of {candidates} divides {n}")


_DOT_DIMS = {"nn": ((1,), (0,)), "nt": ((1,), (1,)), "tn": ((0,), (0,))}


def _dot(a, b, mode):
    return lax.dot_general(a.astype(_CDT), b.astype(_CDT), (_DOT_DIMS[mode], ((), ())),
                           preferred_element_type=_F32)


def _accumulate(ref, val, first):
    @pl.when(first)
    def _():
        ref[...] = val

    @pl.when(jnp.logical_not(first))
    def _():
        ref[...] += val


class _Comm:
    def __init__(self, inputs, outputs, aliases, sems, start, finish):
        self.inputs, self.outputs, self.aliases, self.sems = inputs, outputs, aliases, sems
        self.start, self.finish = start, finish


class _CommArgs:
    def __init__(self, comms, n_in_before, n_out_before):
        self.comms, self.operands, self.out_shape, self.aliases, self.sems, self.at = comms, [], [], {}, [], []
        for cm in comms:
            self.at.append((len(self.operands), len(self.out_shape), len(self.sems)))
            for i_in, i_out in cm.aliases.items():
                self.aliases[n_in_before + len(self.operands) + i_in] = n_out_before + len(self.out_shape) + i_out
            self.operands += cm.inputs
            self.out_shape += cm.outputs
            self.sems += cm.sems

    def _each(self, in_refs, out_refs, sem_refs):
        for cm, (i0, o0, s0) in zip(self.comms, self.at):
            yield cm, (in_refs[i0:i0 + len(cm.inputs)], out_refs[o0:o0 + len(cm.outputs)], sem_refs[s0:s0 + len(cm.sems)])

    def start(self, in_refs, out_refs, sem_refs):
        for cm, refs in self._each(in_refs, out_refs, sem_refs):
            cm.start(*refs)

    def finish(self, in_refs, out_refs, sem_refs):
        for cm, refs in self._each(in_refs, out_refs, sem_refs):
            cm.finish(*refs)


def _matmul(name, groups, m, n, k, tm, tn, tk, extras, outs, epilogue, comm=(), j_outer=False, n_split=1):
    assert m % tm == 0 and n % tn == 0 and k % tk == 0, (name, m, n, k, tm, tn, tk)
    nk = k // tk
    assert n_split == 1 or (nk == 1 and tn % (n_split * V7X_LANES) == 0), (name, n_split)
    terms = [t for g in groups for t in g]
    operands, in_specs, block_bytes = [], [], 0

    def spec(blk, imap):
        return pl.BlockSpec(blk, (lambda g0, g1, kk: imap(g1, g0, kk)) if j_outer else imap)

    for a, b, mode in terms:
        assert a.shape == ((k, m) if mode == "tn" else (m, k)), (name, a.shape, mode)
        assert b.shape == ((n, k) if mode == "nt" else (k, n)), (name, b.shape, mode)
        if mode == "tn":
            a_blk, a_map = (tk, tm), (lambda i, j, kk: (kk, i))
        else:
            a_blk, a_map = (tm, tk), (lambda i, j, kk: (i, kk))
        if mode == "nt":
            b_blk, b_map = (tn, tk), (lambda i, j, kk: (j, kk))
        else:
            b_blk, b_map = (tk, tn), (lambda i, j, kk: (kk, j))
        operands += [a, b]
        in_specs += [spec(a_blk, a_map), spec(b_blk, b_map)]
        block_bytes += _nbytes(a_blk, a.dtype) + _nbytes(b_blk, b.dtype)
    for arr, blk, imap in extras:
        operands.append(arr)
        in_specs.append(spec(blk, lambda i, j, kk, imap=imap: imap(i, j)))
        block_bytes += _nbytes(blk, arr.dtype)
    out_shape, out_specs = [], []
    for shape, dtype, blk, imap in outs:
        out_shape.append(jax.ShapeDtypeStruct(shape, dtype))
        out_specs.append(spec(blk, lambda i, j, kk, imap=imap: imap(i, j)))
        block_bytes += _nbytes(blk, dtype)
    n_terms, n_extra, n_out, n_groups = len(terms), len(extras), len(outs), len(groups)
    scratch = [pltpu.VMEM((tm, tn), _F32) for _ in range(n_groups)] if nk > 1 else []
    ca = _CommArgs(list(comm), len(operands), n_out)
    n_cin, n_cout, n_acc = len(ca.operands), len(ca.out_shape), len(scratch)
    tiles = (m // tm, n // tn)
    grid = (tiles[1], tiles[0], nk) if j_outer else (tiles[0], tiles[1], nk)

    def body(*refs):
        refs = list(refs)
        term_refs = [refs.pop(0) for _ in range(2 * n_terms)]
        extra_refs = [refs.pop(0) for _ in range(n_extra)]
        cin_refs = [refs.pop(0) for _ in range(n_cin)]
        out_refs = [refs.pop(0) for _ in range(n_out)]
        cout_refs = [refs.pop(0) for _ in range(n_cout)]
        acc_refs = [refs.pop(0) for _ in range(n_acc)]
        sem_refs = refs
        g0, g1, kk = pl.program_id(0), pl.program_id(1), pl.program_id(2)
        first = jnp.logical_and(g0 == 0, g1 == 0)
        if comm:
            @pl.when(jnp.logical_and(first, kk == 0))
            def _():
                ca.start(cin_refs, cout_refs, sem_refs)
        def products(cols):
            partial, t = [], 0
            for g in groups:
                s = None
                for _, _, mode in g:
                    b_ref = term_refs[2 * t + 1]
                    b = b_ref[...] if cols is None else (b_ref[cols, :] if mode == "nt" else b_ref[:, cols])
                    d = _dot(term_refs[2 * t][...], b, mode)
                    s = d if s is None else s + d
                    t += 1
                partial.append(s)
            return partial

        if n_split > 1:
            width = tn // n_split
            chunk = lambda c: pl.ds(c * width, width)
            ahead = products(chunk(0))
            for c in range(n_split):
                done, cols = ahead, chunk(c)
                if c + 1 < n_split:
                    ahead = products(chunk(c + 1))
                view = lambda ref: ref.at[:, cols] if tuple(ref.shape) == (tm, tn) else ref
                epilogue(done, [view(r) for r in extra_refs], [view(r) for r in out_refs], first)
        elif nk == 1:
            epilogue(products(None), extra_refs, out_refs, first)
        else:
            partial = products(None)
            for acc, p in zip(acc_refs, partial):
                _accumulate(acc, p, kk == 0)

            @pl.when(kk == nk - 1)
            def _():
                epilogue([acc[...] for acc in acc_refs], extra_refs, out_refs, first)
        if comm:
            @pl.when(jnp.logical_and(jnp.logical_and(g0 == grid[0] - 1, g1 == grid[1] - 1), kk == nk - 1))
            def _():
                ca.finish(cin_refs, cout_refs, sem_refs)

    res = pl.pallas_call(
        body, name=name, grid=grid,
        in_specs=in_specs + [_ANY] * n_cin, out_specs=out_specs + [_ANY] * n_cout,
        out_shape=out_shape + ca.out_shape, scratch_shapes=scratch + ca.sems, input_output_aliases=ca.aliases,
        compiler_params=pltpu.CompilerParams(
            dimension_semantics=("arbitrary", "arbitrary", "arbitrary"),
            vmem_limit_bytes=_vmem_limit(block_bytes, n_groups * tm * tn * 4 if nk > 1 else 0)),
    )(*operands, *ca.operands)
    return list(res[:n_out]) + list(res[n_out:])


def _store_epilogue(accs, extra_refs, out_refs, first):
    for acc, ref in zip(accs, out_refs):
        ref[...] = acc.astype(ref.dtype)


def _tile_ij(i, j):
    return (i, j)


def _row_i(i, j):
    return (i, 0)


def _whole(i, j):
    return (0, 0)


def _mean(v):
    return jnp.mean(v, axis=-1, keepdims=True)


def _ln_fwd(r, g, b):
    xc = r - _mean(r)
    rstd = lax.rsqrt(_mean(xc * xc) + LN_EPS)
    xhat = xc * rstd
    return xhat * g + b, xhat, rstd


def _ln_bwd(dy, xhat, rstd, g):
    dxh = dy * g
    dr = rstd * (dxh - _mean(dxh) - xhat * _mean(dxh * xhat))
    return dr, jnp.sum(dy * xhat, axis=0, keepdims=True), jnp.sum(dy, axis=0, keepdims=True)


def _rms_fwd(a, g):
    rstd = lax.rsqrt(_mean(a * a) + RMS_EPS)
    return a * rstd * g


def _rms_bwd(dm, a, g):
    rstd = lax.rsqrt(_mean(a * a) + RMS_EPS)
    nhat = a * rstd
    dn = dm * g
    da = rstd * (dn - nhat * _mean(dn * nhat))
    return da, jnp.sum(dm * nhat, axis=0, keepdims=True)


def _lane(shape):
    return lax.broadcasted_iota(jnp.int32, shape, 1)


def _row(shape):
    return lax.broadcasted_iota(jnp.int32, shape, 0)


def _rope_tables(pos, invf):
    ang = pos.astype(_F32) * invf
    lane = _lane(ang.shape)
    in_rot = (lane % HEAD_DIM) < ROT_DIM
    first = (lane % ROT_DIM) < ROT_DIM // 2
    cos = jnp.where(in_rot, jnp.cos(ang), 1.0)
    sin = jnp.sin(ang)
    sgn = jnp.where(in_rot, jnp.where(first, -sin, sin), 0.0)
    return cos, sgn


def _rope(t, cos, sgn, sign):
    half = ROT_DIM // 2
    first = (_lane(t.shape) % ROT_DIM) < half
    partner = jnp.where(first, pltpu.roll(t, V7X_LANES - half, 1), pltpu.roll(t, half, 1))
    return t * cos + partner * (sgn * sign)


def _dup_head(t, h):
    g = t[:, 128 * (h // 2):128 * (h // 2) + 128]
    r = pltpu.roll(g, HEAD_DIM, 1)
    lo = _lane(g.shape) < HEAD_DIM
    return jnp.where(lo, g, r) if h % 2 == 0 else jnp.where(lo, r, g)


def _fold_halves(t):
    return t + pltpu.roll(t, HEAD_DIM, 1)


def _halves(t):
    lo = _lane(t.shape) < HEAD_DIM
    zero = jnp.zeros_like(t)
    return jnp.where(lo, t, zero), jnp.where(lo, zero, t)


def _band_mask(n_heads, n_keys, first_block):
    shape = (n_heads * WINDOW, n_keys)
    i = jnp.bitwise_and(_row(shape), WINDOW - 1)
    j = _lane(shape)
    valid = jnp.logical_and(j >= i + 1, j <= i + WINDOW)
    if first_block is not None:
        valid = jnp.logical_and(valid, jnp.logical_or(j >= WINDOW, jnp.logical_not(first_block)))
    return valid


def _stack_heads(pairs):
    return jnp.concatenate([half for t in pairs for half in _halves(t)], axis=0).astype(_CDT)


def _unstack_heads(t, n_pairs):
    lo = _lane((WINDOW, 128)) < HEAD_DIM
    return [jnp.where(lo, t[2 * WINDOW * i:2 * WINDOW * i + WINDOW], t[2 * WINDOW * i + WINDOW:2 * WINDOW * (i + 1)])
            for i in range(n_pairs)]


def _per_head(values):
    n_rows = len(values) * WINDOW
    block = jnp.right_shift(_row((n_rows, 1)), WINDOW.bit_length() - 1)
    out = jnp.zeros((n_rows, 1), _F32)
    for k, v in enumerate(values):
        out = jnp.where(block == k, v, out)
    return out


def _shift_down(z, halo, k):
    rows = z.shape[0]
    out = pltpu.roll(z, k, 0)
    r = _row(z.shape)
    for t in range(k):
        out = jnp.where(r == t, halo[V7X_SUBLANES - k + t:V7X_SUBLANES - k + t + 1, :], out)
    del rows
    return out


def _shift_up(z, halo, k):
    rows = z.shape[0]
    out = pltpu.roll(z, rows - k, 0)
    r = _row(z.shape)
    for t in range(k):
        out = jnp.where(r == rows - k + t, halo[t:t + 1, :], out)
    return out


class _Dims:
    def __init__(self, s, d, d_ff):
        self.s, self.d, self.d_ff = s, d, d_ff
        self.aw = d // 2
        self.cw = d - self.aw
        self.nq = self.aw // HEAD_DIM
        self.group = self.nq // N_KV_HEADS
        assert self.group % 2 == 0, "a 128-lane pair of query heads must share its kv head"
        self.inw = self.aw + 2 * KV_WIDTH + 3 * self.cw
        self.o_k = self.aw
        self.o_v = self.aw + KV_WIDTH
        self.o_cg = self.aw + 2 * KV_WIDTH
        self.o_bg = self.o_cg + self.cw
        self.o_u = self.o_bg + self.cw
        self.nb = s // WINDOW
        assert s % WINDOW == 0


def _carrying(body, n_in, n_out, n_steps, ca, n_scratch=0):
    n_cin, n_cout = len(ca.operands), len(ca.out_shape)

    def wrapped(*refs):
        refs = list(refs)
        in_refs = [refs.pop(0) for _ in range(n_in)]
        cin_refs = [refs.pop(0) for _ in range(n_cin)]
        out_refs = [refs.pop(0) for _ in range(n_out)]
        cout_refs = [refs.pop(0) for _ in range(n_cout)]
        scratch_refs = [refs.pop(0) for _ in range(n_scratch)]
        if ca.comms:
            @pl.when(pl.program_id(0) == 0)
            def _():
                ca.start(cin_refs, cout_refs, refs)
        body(*in_refs, *out_refs, *scratch_refs)
        if ca.comms:
            @pl.when(pl.program_id(0) == n_steps - 1)
            def _():
                ca.finish(cin_refs, cout_refs, refs)

    return wrapped


def _row_kernel(name, body, rows_in, vecs_in, rows_out, vecs_out, comm=()):
    s = rows_in[0].shape[0]
    tr = _pick(s, (256, 128))
    row = lambda a: pl.BlockSpec((tr, a[1] if isinstance(a, tuple) else a.shape[1]), lambda i: (i, 0))
    vec = lambda shape: pl.BlockSpec(tuple(shape), lambda i: (0, 0))
    n_in, n_out = len(rows_in) + len(vecs_in), len(rows_out) + len(vecs_out)
    ca = _CommArgs(list(comm), n_in, n_out)
    blocks = sum(_nbytes((tr, a.shape[1]), a.dtype) for a in rows_in) + sum(_nbytes((tr, sh[1]), dt) for sh, dt in rows_out)
    res = pl.pallas_call(
        _carrying(body, n_in, n_out, s // tr, ca), name=name, grid=(s // tr,),
        in_specs=[row(a) for a in rows_in] + [vec(v.shape) for v in vecs_in] + [_ANY] * len(ca.operands),
        out_specs=[row(sh) for sh, _ in rows_out] + [vec(sh) for sh, _ in vecs_out] + [_ANY] * len(ca.out_shape),
        out_shape=[jax.ShapeDtypeStruct(sh, dt) for sh, dt in list(rows_out) + list(vecs_out)] + ca.out_shape,
        scratch_shapes=ca.sems, input_output_aliases=ca.aliases,
        compiler_params=pltpu.CompilerParams(dimension_semantics=("arbitrary",), vmem_limit_bytes=_vmem_limit(blocks)),
    )(*rows_in, *vecs_in, *ca.operands)
    return list(res)


def _ln2_loss_bwd(r2, target, gain, bias, comm=()):
    s, d = r2.shape

    def body(r_ref, t_ref, g_ref, b_ref, dr_ref, drc_ref, loss_ref, dg_ref, db_ref):
        first = pl.program_id(0) == 0
        yv, xhat, rstd = _ln_fwd(r_ref[...], g_ref[...], b_ref[...])
        err = yv - t_ref[...]
        dr2, dg, db = _ln_bwd(err * (1.0 / d), xhat, rstd, g_ref[...])
        dr_ref[...] = dr2
        drc_ref[...] = dr2.astype(_CDT)
        _accumulate(loss_ref, jnp.zeros(loss_ref.shape, _F32) + 0.5 * jnp.sum(err * err) * (1.0 / d), first)
        _accumulate(dg_ref, dg, first)
        _accumulate(db_ref, db, first)

    return _row_kernel("ln2_loss_bwd", body, [r2, target], [gain, bias], [((s, d), _F32), ((s, d), _CDT)],
                       [((V7X_SUBLANES, V7X_LANES), _F32), ((1, d), _F32), ((1, d), _F32)], comm)


def _ln1_fwd_rows(r1, gain, bias, comm=()):
    s, d = r1.shape

    def body(r_ref, g_ref, b_ref, h_ref, hc_ref, xhat_ref, rstd_ref):
        h1, xhat, rstd = _ln_fwd(r_ref[...], g_ref[...], b_ref[...])
        h_ref[...] = h1
        hc_ref[...] = h1.astype(_CDT)
        xhat_ref[...] = xhat
        rstd_ref[...] = rstd

    return _row_kernel("ln1", body, [r1], [gain, bias],
                       [((s, d), _F32), ((s, d), _CDT), ((s, d), _F32), ((s, 1), _F32)], [], comm)


def _ln1_bwd_rows(dh1, xhat, rstd, gain, comm=()):
    s, d = dh1.shape

    def body(dh_ref, xhat_ref, rstd_ref, g_ref, dr_ref, drc_ref, dg_ref, db_ref):
        first = pl.program_id(0) == 0
        dr1, dg, db = _ln_bwd(dh_ref[...], xhat_ref[...], rstd_ref[...], g_ref[...])
        dr_ref[...] = dr1
        drc_ref[...] = dr1.astype(_CDT)
        _accumulate(dg_ref, dg, first)
        _accumulate(db_ref, db, first)

    return _row_kernel("ln1_bwd", body, [dh1, xhat, rstd], [gain], [((s, d), _F32), ((s, d), _CDT)],
                       [((1, d), _F32), ((1, d), _F32)], comm)


def _mixer_fwd(dm, proj, pos, invf, sinks, g_attn, g_conv, conv_w8, comm=()):
    s, d, aw, cw, nq, inw, nb = dm.s, dm.d, dm.aw, dm.cw, dm.nq, dm.inw, dm.nb

    def body(pp_ref, pc_ref, posp_ref, posc_ref, invf_ref, sinks_ref, ga_ref, gc_ref, cw_ref,
             mixed_ref, attn_ref, lse_ref, y_ref):
        n = pl.program_id(0)
        cos_c, sgn_c = _rope_tables(posc_ref[...], invf_ref[...])
        cos_p, sgn_p = _rope_tables(posp_ref[...], invf_ref[...])
        kk = jnp.concatenate(
            [jnp.concatenate([_rope(ref[:, dm.o_k + 128 * g:dm.o_k + 128 * g + 128], c, sg, 1.0)
                              for g in range(KV_WIDTH // 128)], axis=1)
             for ref, c, sg in ((pp_ref, cos_p, sgn_p), (pc_ref, cos_c, sgn_c))], axis=0)
        vv = jnp.concatenate([pp_ref[:, dm.o_v:dm.o_v + KV_WIDTH], pc_ref[:, dm.o_v:dm.o_v + KV_WIDTH]], axis=0)
        group, pairs = dm.group, dm.group // 2
        valid = _band_mask(group, 2 * WINDOW, n == 0)
        for h in range(N_KV_HEADS):
            k2, v2 = _dup_head(kk, h).astype(_CDT), _dup_head(vv, h).astype(_CDT)
            q4 = _stack_heads([_rope(pc_ref[:, 128 * j:128 * j + 128], cos_c, sgn_c, 1.0)
                               for j in range(pairs * h, pairs * (h + 1))])
            sc = jnp.where(valid, _dot(q4, k2, "nt") * ATTN_SCALE, MASKED)
            sink = _per_head([sinks_ref[0, group * h + r] for r in range(group)])
            mx = jnp.maximum(jnp.max(sc, axis=1, keepdims=True), sink)
            p = jnp.exp(sc - mx)
            den = jnp.sum(p, axis=1, keepdims=True) + jnp.exp(sink - mx)
            out = _unstack_heads(_dot(p / den, v2, "nn"), pairs)
            lse = mx + jnp.log(den)
            for r in range(group):
                lse_ref[:, group * h + r:group * h + r + 1] = lse[WINDOW * r:WINDOW * (r + 1)]
            for i in range(pairs):
                j = pairs * h + i
                attn_ref[:, 128 * j:128 * j + 128] = out[i]
        mixed_ref[:, 0:aw] = _rms_fwd(attn_ref[...], ga_ref[...]).astype(mixed_ref.dtype)

        z = pc_ref[:, dm.o_cg:dm.o_cg + cw] * pc_ref[:, dm.o_u:dm.o_u + cw]
        top = WINDOW - V7X_SUBLANES
        halo = pp_ref[top:WINDOW, dm.o_cg:dm.o_cg + cw] * pp_ref[top:WINDOW, dm.o_u:dm.o_u + cw]
        halo = jnp.where(n == 0, jnp.zeros_like(halo), halo)
        y = cw_ref[0:1, :] * _shift_down(z, halo, 2) + cw_ref[1:2, :] * _shift_down(z, halo, 1) + cw_ref[2:3, :] * z
        y_ref[...] = y
        conv = pc_ref[:, dm.o_bg:dm.o_bg + cw] * y
        mixed_ref[:, aw:d] = _rms_fwd(conv, gc_ref[...]).astype(mixed_ref.dtype)

    prev = lambda n: (jnp.maximum(n - 1, 0), 0)
    cur = lambda n: (n, 0)
    fixed = lambda n: (0, 0)
    blocks = 2 * WINDOW * inw * 4 + WINDOW * (d * 2 + aw * 4 + cw * 4 + nq * 4)
    ca = _CommArgs(list(comm), 9, 4)
    return pl.pallas_call(
        _carrying(body, 9, 4, nb, ca), name="mixer_fwd", grid=(nb,),
        in_specs=[pl.BlockSpec((WINDOW, inw), prev), pl.BlockSpec((WINDOW, inw), cur),
                  pl.BlockSpec((WINDOW, 1), prev), pl.BlockSpec((WINDOW, 1), cur),
                  pl.BlockSpec((1, V7X_LANES), fixed), pl.BlockSpec(memory_space=pltpu.SMEM),
                  pl.BlockSpec((1, aw), fixed), pl.BlockSpec((1, cw), fixed), pl.BlockSpec((V7X_SUBLANES, cw), fixed)]
        + [_ANY] * len(ca.operands),
        out_specs=[pl.BlockSpec((WINDOW, d), cur), pl.BlockSpec((WINDOW, aw), cur),
                   pl.BlockSpec((WINDOW, nq), cur), pl.BlockSpec((WINDOW, cw), cur)] + [_ANY] * len(ca.out_shape),
        out_shape=[jax.ShapeDtypeStruct((s, d), _CDT), jax.ShapeDtypeStruct((s, aw), _F32),
                   jax.ShapeDtypeStruct((s, nq), _F32), jax.ShapeDtypeStruct((s, cw), _F32)] + ca.out_shape,
        scratch_shapes=ca.sems, input_output_aliases=ca.aliases,
        compiler_params=pltpu.CompilerParams(dimension_semantics=("arbitrary",), vmem_limit_bytes=_vmem_limit(blocks)),
    )(proj, proj, pos, pos, invf, sinks, g_attn, g_conv, conv_w8, *ca.operands)


def _patch_columns(name, a, part, offset):
    s, pw = part.shape
    assert offset % pw == 0 and pw % V7X_LANES == 0
    tr = _pick(s, (512, 256, 128))

    def body(a_ref, p_ref, o_ref):
        del a_ref
        o_ref[...] = p_ref[...]

    return pl.pallas_call(
        body, name=name, grid=(s // tr,),
        in_specs=[_ANY, pl.BlockSpec((tr, pw), lambda i: (i, 0))],
        out_specs=pl.BlockSpec((tr, pw), lambda i: (i, offset // pw)),
        out_shape=jax.ShapeDtypeStruct(a.shape, a.dtype), input_output_aliases={0: 0},
        compiler_params=pltpu.CompilerParams(dimension_semantics=("arbitrary",)),
    )(a, part)


def _mixer_bwd(dm, proj, pos, invf, sinks, g_attn, g_conv, conv_w8, dmixed, attn, lse, y, comm=()):
    s, d, aw, cw, nq, inw, nb = dm.s, dm.d, dm.aw, dm.cw, dm.nq, dm.inw, dm.nb

    def body(pp_ref, pc_ref, pn_ref, posp_ref, posc_ref, dmc_ref, dmn_ref, ac_ref,
             lsec_ref, yc_ref, yn_ref, invf_ref, sinks_ref, ga_ref, gc_ref, cw_ref,
             dproj_ref, dkv_ref, dga_ref, dgc_ref, dsinks_ref, dcw_ref, dk_carry, dv_carry):
        n = pl.program_id(0)
        first = n == 0
        live = n < nb
        has_next = n < nb - 1
        cos_p, sgn_p = _rope_tables(posp_ref[...], invf_ref[...])
        cos_c, sgn_c = _rope_tables(posc_ref[...], invf_ref[...])

        @pl.when(first)
        def _():
            dk_carry[...] = jnp.zeros(dk_carry.shape, _F32)
            dv_carry[...] = jnp.zeros(dv_carry.shape, _F32)

        def write_kv(dk2, dv2, cos, sgn):
            lo = _lane((WINDOW, 128)) < HEAD_DIM
            for g in range(KV_WIDTH // 128):
                dk = jnp.where(lo, _fold_halves(dk2[2 * g]), _fold_halves(dk2[2 * g + 1]))
                dv = jnp.where(lo, _fold_halves(dv2[2 * g]), _fold_halves(dv2[2 * g + 1]))
                dkv_ref[:, 128 * g:128 * g + 128] = _rope(dk, cos, sgn, -1.0).astype(dkv_ref.dtype)
                dkv_ref[:, KV_WIDTH + 128 * g:KV_WIDTH + 128 * g + 128] = dv.astype(dkv_ref.dtype)

        @pl.when(jnp.logical_not(live))
        def _():
            write_kv([dk_carry[h] for h in range(N_KV_HEADS)], [dv_carry[h] for h in range(N_KV_HEADS)], cos_c, sgn_c)

        @pl.when(live)
        def _():
            block_step(pp_ref, pc_ref, pn_ref, dmc_ref, dmn_ref, ac_ref, lsec_ref, yc_ref, yn_ref, sinks_ref, ga_ref,
                       gc_ref, cw_ref, dproj_ref, dga_ref, dgc_ref, dsinks_ref, dcw_ref, dk_carry, dv_carry,
                       first, has_next, cos_p, sgn_p, cos_c, sgn_c, write_kv)

    def block_step(pp_ref, pc_ref, pn_ref, dmc_ref, dmn_ref, ac_ref, lsec_ref, yc_ref, yn_ref, sinks_ref, ga_ref,
                   gc_ref, cw_ref, dproj_ref, dga_ref, dgc_ref, dsinks_ref, dcw_ref, dk_carry, dv_carry,
                   first, has_next, cos_p, sgn_p, cos_c, sgn_c, write_kv):
        da_c, dga = _rms_bwd(dmc_ref[:, 0:aw], ac_ref[...], ga_ref[...])
        _accumulate(dga_ref, dga, first)
        kk = jnp.concatenate(
            [jnp.concatenate([_rope(ref[:, dm.o_k + 128 * g:dm.o_k + 128 * g + 128], c, sg, 1.0)
                              for g in range(KV_WIDTH // 128)], axis=1)
             for ref, c, sg in ((pp_ref, cos_p, sgn_p), (pc_ref, cos_c, sgn_c))], axis=0)
        vv = jnp.concatenate([pp_ref[:, dm.o_v:dm.o_v + KV_WIDTH], pc_ref[:, dm.o_v:dm.o_v + KV_WIDTH]], axis=0)
        group, pairs = dm.group, dm.group // 2
        valid_c = _band_mask(group, 2 * WINDOW, first)
        dk_prev, dv_prev = [], []
        dsinks = jnp.zeros((1, nq), _F32)
        head_lane = _lane((1, nq))

        def stacked(q_ref, cos, sgn, da, o_ref, lse_ref_, h):
            cols = [slice(128 * j, 128 * j + 128) for j in range(pairs * h, pairs * (h + 1))]
            q4 = _stack_heads([_rope(q_ref[:, c], cos, sgn, 1.0) for c in cols])
            do4 = _stack_heads([da[:, c] for c in cols])
            lo = _lane((WINDOW, 128)) < HEAD_DIM
            deltas = []
            for c in cols:
                prod = o_ref[:, c] * da[:, c]
                deltas += [jnp.sum(jnp.where(lo, prod, 0.0), axis=1, keepdims=True),
                           jnp.sum(jnp.where(lo, 0.0, prod), axis=1, keepdims=True)]
            lse4 = jnp.concatenate([lse_ref_[:, group * h + r:group * h + r + 1] for r in range(group)], axis=0)
            return q4, do4, lse4, jnp.concatenate(deltas, axis=0)

        def scores_bwd(q4, do4, lse4, delta4, keys, vals, valid):
            sc = _dot(q4, keys, "nt") * ATTN_SCALE
            p = jnp.exp(jnp.where(valid, sc - lse4, MASKED))
            return p.astype(_CDT), (p * (_dot(do4, vals, "nt") - delta4) * ATTN_SCALE).astype(_CDT)

        for h in range(N_KV_HEADS):
            k2, v2 = _dup_head(kk, h).astype(_CDT), _dup_head(vv, h).astype(_CDT)
            q4, do4, lse4, delta4 = stacked(pc_ref, cos_c, sgn_c, da_c, ac_ref, lsec_ref, h)
            p, ds = scores_bwd(q4, do4, lse4, delta4, k2, v2, valid_c)
            for i, dq in enumerate(_unstack_heads(_dot(ds, k2, "nn"), pairs)):
                j = pairs * h + i
                dproj_ref[:, 128 * j:128 * j + 128] = _rope(dq, cos_c, sgn_c, -1.0).astype(dproj_ref.dtype)
            dk = _dot(ds, q4, "tn")
            dv = _dot(p, do4, "tn")
            dk_prev.append(dk_carry[h] + dk[0:WINDOW])
            dv_prev.append(dv_carry[h] + dv[0:WINDOW])
            dk_carry[h] = dk[WINDOW:2 * WINDOW]
            dv_carry[h] = dv[WINDOW:2 * WINDOW]
            sink4 = _per_head([sinks_ref[0, group * h + r] for r in range(group)])
            loss_sink = jnp.exp(sink4 - lse4) * delta4
            for r in range(group):
                dsinks = dsinks + jnp.where(head_lane == group * h + r,
                                            -jnp.sum(loss_sink[WINDOW * r:WINDOW * (r + 1)]), 0.0)
        _accumulate(dsinks_ref, dsinks, first)
        write_kv(dk_prev, dv_prev, cos_p, sgn_p)

        bg = pc_ref[:, dm.o_bg:dm.o_bg + cw]
        yc = yc_ref[...]
        dconv, dgc = _rms_bwd(dmc_ref[:, aw:d], bg * yc, gc_ref[...])
        _accumulate(dgc_ref, dgc, first)
        dproj_ref[:, dm.o_bg:dm.o_bg + cw] = (dconv * yc).astype(dproj_ref.dtype)
        dy = dconv * bg
        bg_n = pn_ref[:, dm.o_bg:dm.o_bg + cw]
        dconv_n, _ = _rms_bwd(dmn_ref[:, aw:d], bg_n * yn_ref[...], gc_ref[...])
        halo = jnp.where(has_next, dconv_n * bg_n, 0.0)
        dy1 = _shift_up(dy, halo, 1)
        dy2 = _shift_up(dy, halo, 2)
        dz = cw_ref[2:3, :] * dy + cw_ref[1:2, :] * dy1 + cw_ref[0:1, :] * dy2
        cg = pc_ref[:, dm.o_cg:dm.o_cg + cw]
        u = pc_ref[:, dm.o_u:dm.o_u + cw]
        dproj_ref[:, dm.o_cg:dm.o_cg + cw] = (dz * u).astype(dproj_ref.dtype)
        dproj_ref[:, dm.o_u:dm.o_u + cw] = (dz * cg).astype(dproj_ref.dtype)
        z = cg * u
        dcw = jnp.concatenate(
            [jnp.sum(z * t, axis=0, keepdims=True) for t in (dy2, dy1, dy)]
            + [jnp.zeros((V7X_SUBLANES - 3, cw), _F32)], axis=0)
        _accumulate(dcw_ref, dcw, first)

    at = lambda n: jnp.minimum(n, nb - 1)
    prev = lambda n: (jnp.maximum(at(n) - 1, 0), 0)
    cur = lambda n: (at(n), 0)
    done = lambda n: (jnp.maximum(n - 1, 0), 0)
    nxt8 = lambda n: (jnp.minimum((at(n) + 1) * (WINDOW // V7X_SUBLANES), s // V7X_SUBLANES - 1), 0)
    fixed = lambda n: (0, 0)
    blocks = WINDOW * (2 * inw * 4 + d * 4 + aw * 4 + cw * 4 + inw * 2 + 2 * KV_WIDTH * 2)
    carry = [pltpu.VMEM((N_KV_HEADS, WINDOW, 128), _F32), pltpu.VMEM((N_KV_HEADS, WINDOW, 128), _F32)]
    n_in, n_out = 16, 6
    ca = _CommArgs(list(comm), n_in, n_out)
    return pl.pallas_call(
        _carrying(body, n_in, n_out, nb + 1, ca, n_scratch=len(carry)), name="mixer_bwd", grid=(nb + 1,),
        in_specs=[pl.BlockSpec((WINDOW, inw), prev), pl.BlockSpec((WINDOW, inw), cur), pl.BlockSpec((V7X_SUBLANES, inw), nxt8),
                  pl.BlockSpec((WINDOW, 1), prev), pl.BlockSpec((WINDOW, 1), cur),
                  pl.BlockSpec((WINDOW, d), cur), pl.BlockSpec((V7X_SUBLANES, d), nxt8),
                  pl.BlockSpec((WINDOW, aw), cur), pl.BlockSpec((WINDOW, nq), cur),
                  pl.BlockSpec((WINDOW, cw), cur), pl.BlockSpec((V7X_SUBLANES, cw), nxt8),
                  pl.BlockSpec((1, V7X_LANES), fixed), pl.BlockSpec(memory_space=pltpu.SMEM),
                  pl.BlockSpec((1, aw), fixed), pl.BlockSpec((1, cw), fixed), pl.BlockSpec((V7X_SUBLANES, cw), fixed)]
        + [_ANY] * len(ca.operands),
        out_specs=[pl.BlockSpec((WINDOW, inw), cur), pl.BlockSpec((WINDOW, 2 * KV_WIDTH), done),
                   pl.BlockSpec((1, aw), fixed), pl.BlockSpec((1, cw), fixed),
                   pl.BlockSpec((1, nq), fixed), pl.BlockSpec((V7X_SUBLANES, cw), fixed)] + [_ANY] * len(ca.out_shape),
        out_shape=[jax.ShapeDtypeStruct((s, inw), _CDT), jax.ShapeDtypeStruct((s, 2 * KV_WIDTH), _CDT),
                   jax.ShapeDtypeStruct((1, aw), _F32), jax.ShapeDtypeStruct((1, cw), _F32),
                   jax.ShapeDtypeStruct((1, nq), _F32), jax.ShapeDtypeStruct((V7X_SUBLANES, cw), _F32)] + ca.out_shape,
        scratch_shapes=carry + ca.sems, input_output_aliases=ca.aliases,
        compiler_params=pltpu.CompilerParams(dimension_semantics=("arbitrary",), vmem_limit_bytes=_vmem_limit(blocks)),
    )(proj, proj, proj, pos, pos, dmixed, dmixed, attn, lse, y, y, invf, sinks, g_attn, g_conv, conv_w8, *ca.operands)


def _position():
    return lax.axis_index("x"), lax.axis_index("y"), lax.axis_index("c")


def _linear(px, py, pc):
    return 4 * px + 2 * py + pc


def _comm_kernel(name, comm):
    ca = _CommArgs(list(comm), 0, 0)
    n_cin, n_cout = len(ca.operands), len(ca.out_shape)

    def body(*refs):
        cin, cout, sems = refs[:n_cin], refs[n_cin:n_cin + n_cout], refs[n_cin + n_cout:]
        ca.start(cin, cout, sems)
        ca.finish(cin, cout, sems)

    return pl.pallas_call(
        body, name=name, out_shape=ca.out_shape, in_specs=[_ANY] * n_cin, out_specs=[_ANY] * n_cout,
        scratch_shapes=ca.sems, input_output_aliases=ca.aliases,
    )(*ca.operands)


def _gather_op(units):
    n = len(units)
    inputs, outputs, aliases = [], [], {}
    for shard, _, _, _ in units:
        inputs.append(shard)
        outputs.append(jax.ShapeDtypeStruct((N_DEV * shard.shape[0], shard.shape[1]), shard.dtype))
    for u, (_, buf, _, _) in enumerate(units):
        if buf is not None:
            aliases[len(inputs)] = u
            inputs.append(buf)

    def plan(ins, outs, sems):
        send_sems, recv_sems, local_sems = sems
        x, y, c = _position()
        me, sibling = (x, y, c), (x, y, 1 - c)
        chips = [(1 - x, y), (x, 1 - y), (1 - x, 1 - y)]

        def rows(u, px, py, pc):
            shard, _, r0, r1 = units[u]
            return outs[u].at[pl.ds(pl.multiple_of(_linear(px, py, pc) * shard.shape[0] + r0, 16), r1 - r0), :]

        def own(u):
            _, _, r0, r1 = units[u]
            return ins[u].at[pl.ds(r0, r1 - r0), :]

        def copy(u, k, block, to, src=None):
            return pltpu.make_async_remote_copy(
                src_ref=rows(u, *block) if src is None else src, dst_ref=rows(u, *block),
                send_sem=send_sems.at[u, k], recv_sem=recv_sems.at[u, k], device_id=to, device_id_type=_MESH)

        mine = [pltpu.make_async_copy(own(u), rows(u, *me), local_sems.at[u]) for u in range(n)]
        first = []
        for u in range(n):
            first.append(copy(u, 0, me, sibling, src=own(u)))
            first += [copy(u, 1 + j, me, (*chip, c), src=own(u)) for j, chip in enumerate(chips)]
        passed = [[copy(u, 4 + j, (*chip, c), sibling) for j, chip in enumerate(chips)] for u in range(n)]
        landed = [[copy(u, 1 + j, (*chip, c), me) for j, chip in enumerate(chips)] for u in range(n)]
        rest = [[copy(u, 0, sibling, me)] + [copy(u, 4 + j, (*chip, 1 - c), me) for j, chip in enumerate(chips)]
                for u in range(n)]
        return mine, first, passed, landed, rest

    def start(ins, outs, sems):
        mine, first, _, _, _ = plan(ins, outs, sems)
        for cp in mine + first:
            cp.start()

    def finish(ins, outs, sems):
        mine, first, passed, landed, rest = plan(ins, outs, sems)
        for u in range(n):
            for arrived, onward in zip(landed[u], passed[u]):
                arrived.wait_recv()
                onward.start()
        for u in range(n):
            for cp in rest[u]:
                cp.wait_recv()
        for cp in first + [cp for row in passed for cp in row]:
            cp.wait_send()
        for cp in mine:
            cp.wait()

    sems = [pltpu.SemaphoreType.DMA((n, 7)), pltpu.SemaphoreType.DMA((n, 7)), pltpu.SemaphoreType.DMA((n,))]
    return _Comm(inputs, outputs, aliases, sems, start, finish)


def _peers(x, y, c):
    out = []
    for k in range(1, N_DEV):
        fx, fy, fc = (k >> 2) & 1, (k >> 1) & 1, k & 1
        out.append((1 - x if fx else x, 1 - y if fy else y, 1 - c if fc else c))
    return out


def _exchange_op(partials):
    n = len(partials)
    outputs = [jax.ShapeDtypeStruct((4, p.shape[0] // N_DEV, p.shape[1]), p.dtype) for p in partials]

    def plan(ins, outs, sems):
        send_sems, recv_sems = sems
        x, y, c = _position()
        out = []
        for a in range(n):
            r = outs[a].shape[1]
            for ch in range(4):
                out.append(pltpu.make_async_remote_copy(
                    src_ref=ins[a].at[pl.ds(pl.multiple_of((2 * ch + 1 - c) * r, 16), r), :], dst_ref=outs[a].at[ch],
                    send_sem=send_sems.at[a, ch], recv_sem=recv_sems.at[a, ch], device_id=(x, y, 1 - c),
                    device_id_type=_MESH))
        return out

    def start(ins, outs, sems):
        for cp in plan(ins, outs, sems):
            cp.start()

    def finish(ins, outs, sems):
        copies = plan(ins, outs, sems)
        for cp in copies:
            cp.wait_recv()
        for cp in copies:
            cp.wait_send()

    sems = [pltpu.SemaphoreType.DMA((n, 4)), pltpu.SemaphoreType.DMA((n, 4))]
    return _Comm(list(partials), outputs, {}, sems, start, finish)


def _chip_send_op(units):
    n = len(units)
    inputs, outputs, aliases = [], [], {}
    for q, _, _, _ in units:
        inputs.append(q)
        outputs.append(jax.ShapeDtypeStruct(q.shape, q.dtype))
    for u, (_, buf, _, _) in enumerate(units):
        if buf is not None:
            aliases[len(inputs)] = u
            inputs.append(buf)

    def plan(ins, outs, sems):
        send_sems, recv_sems, local_sems = sems
        x, y, c = _position()
        my_chip = 2 * x + y
        chips = [(1 - x, y), (x, 1 - y), (1 - x, 1 - y)]
        mine, sends, arrivals = [], [], []
        for u, (_, _, r0, r1) in enumerate(units):
            span = pl.ds(r0, r1 - r0)
            mine.append(pltpu.make_async_copy(ins[u].at[my_chip, span, :], outs[u].at[my_chip, span, :], local_sems.at[u]))
            for k, (px, py) in enumerate(chips):
                sends.append(pltpu.make_async_remote_copy(
                    src_ref=ins[u].at[2 * px + py, span, :], dst_ref=outs[u].at[my_chip, span, :],
                    send_sem=send_sems.at[u, k], recv_sem=recv_sems.at[u, k], device_id=(px, py, c), device_id_type=_MESH))
                arrivals.append(pltpu.make_async_remote_copy(
                    src_ref=ins[u].at[my_chip, span, :], dst_ref=outs[u].at[2 * px + py, span, :],
                    send_sem=send_sems.at[u, k], recv_sem=recv_sems.at[u, k], device_id=(px, py, c), device_id_type=_MESH))
        return mine, sends, arrivals

    def start(ins, outs, sems):
        mine, sends, _ = plan(ins, outs, sems)
        for cp in mine + sends:
            cp.start()

    def finish(ins, outs, sems):
        mine, sends, arrivals = plan(ins, outs, sems)
        for cp in arrivals:
            cp.wait_recv()
        for cp in sends:
            cp.wait_send()
        for cp in mine:
            cp.wait()

    sems = [pltpu.SemaphoreType.DMA((n, 3)), pltpu.SemaphoreType.DMA((n, 3)), pltpu.SemaphoreType.DMA((n,))]
    return _Comm(inputs, outputs, aliases, sems, start, finish)


def _pair_sum(name, partial, received):
    _, rows, cols = received.shape
    tr = _pick(rows, (352, 288, 256, 128, 64, 32, 16))
    p4 = partial.reshape(4, 2, rows, cols)
    kind = jnp.reshape(lax.axis_index("c"), (1,)).astype(jnp.int32)

    def body(kind_ref, p_ref, r_ref, o_ref):
        o_ref[0] = (p_ref[0, 0].astype(_F32) + r_ref[0].astype(_F32)).astype(o_ref.dtype)

    return pl.pallas_call(
        body, name=name,
        grid_spec=pltpu.PrefetchScalarGridSpec(
            num_scalar_prefetch=1, grid=(4, rows // tr),
            in_specs=[pl.BlockSpec((1, 1, tr, cols), lambda ch, i, kind_ref: (ch, kind_ref[0], i, 0)),
                      pl.BlockSpec((1, tr, cols), lambda ch, i, kind_ref: (ch, i, 0))],
            out_specs=pl.BlockSpec((1, tr, cols), lambda ch, i, kind_ref: (ch, i, 0))),
        out_shape=jax.ShapeDtypeStruct(received.shape, received.dtype),
        compiler_params=pltpu.CompilerParams(dimension_semantics=("arbitrary", "arbitrary")),
    )(kind, p4, received)


def _all_reduce_small(name, v):
    rows = v.shape[0]

    def body(v_ref, out_ref, land_ref, send_sems, recv_sems):
        x, y, c = _position()
        me = _linear(x, y, c)
        peers = _peers(x, y, c)
        land_ref[me] = v_ref[...]
        sends = [pltpu.make_async_remote_copy(
            src_ref=v_ref, dst_ref=land_ref.at[me], send_sem=send_sems.at[k], recv_sem=recv_sems.at[k],
            device_id=peer, device_id_type=_MESH) for k, peer in enumerate(peers)]
        for cp in sends:
            cp.start()
        for k, peer in enumerate(peers):
            pltpu.make_async_remote_copy(
                src_ref=v_ref, dst_ref=land_ref.at[_linear(*peer)], send_sem=send_sems.at[k], recv_sem=recv_sems.at[k],
                device_id=peer, device_id_type=_MESH).wait_recv()
        for cp in sends:
            cp.wait_send()
        total = land_ref[0]
        for s in range(1, N_DEV):
            total = total + land_ref[s]
        out_ref[...] = total

    return pl.pallas_call(
        body, name=name, out_shape=jax.ShapeDtypeStruct(v.shape, _F32),
        in_specs=[pl.BlockSpec(memory_space=pltpu.VMEM)], out_specs=pl.BlockSpec(memory_space=pltpu.VMEM),
        scratch_shapes=[pltpu.VMEM((N_DEV, rows, V7X_LANES), _F32), pltpu.SemaphoreType.DMA((7,)), pltpu.SemaphoreType.DMA((7,))],
    )(v)


def _adamw(name, w, slots, m, v):
    rows, cols = w.shape
    n_slots = slots.shape[0]
    tr = _pick(rows, (176, 144, 128, 64, 32, 16, 8))

    def body(w_ref, s_ref, m_ref, v_ref, g_ref, d_ref, nm_ref, nv_ref):
        g = s_ref[0].astype(_F32)
        for k in range(1, n_slots):
            g = g + s_ref[k].astype(_F32)
        nm = ADAM_B1 * m_ref[...] + (1.0 - ADAM_B1) * g
        nv = ADAM_B2 * v_ref[...] + (1.0 - ADAM_B2) * (g * g)
        m_hat = nm / (1.0 - ADAM_B1 ** ADAM_STEP)
        v_hat = nv / (1.0 - ADAM_B2 ** ADAM_STEP)
        g_ref[...] = g
        d_ref[...] = -ADAM_LR * (m_hat / (jnp.sqrt(v_hat) + ADAM_EPS) + ADAM_WD * w_ref[...])
        nm_ref[...] = nm
        nv_ref[...] = nv

    spec = pl.BlockSpec((tr, cols), lambda i: (i, 0))
    blocks = 7 * tr * cols * 4 + _nbytes((n_slots, tr, cols), slots.dtype)
    return pl.pallas_call(
        body, name=name, grid=(rows // tr,),
        in_specs=[spec, pl.BlockSpec((n_slots, tr, cols), lambda i: (0, i, 0)), spec, spec], out_specs=[spec] * 4,
        out_shape=[jax.ShapeDtypeStruct((rows, cols), _F32)] * 4,
        compiler_params=pltpu.CompilerParams(dimension_semantics=("arbitrary",), vmem_limit_bytes=_vmem_limit(blocks)),
    )(w, slots, m, v)


def _pad_rows(a, rows):
    return jnp.pad(a, ((0, rows - a.shape[0]), (0, 0)))


def _pack(parts):
    rows, spans, at = [], [], 0
    for p in parts:
        p = p.reshape(-1)
        r = -(-p.shape[0] // V7X_LANES)
        rows.append(jnp.pad(p, (0, r * V7X_LANES - p.shape[0])).reshape(r, V7X_LANES))
        spans.append((at, r, p.shape[0]))
        at += r
    packed = jnp.concatenate(rows, axis=0)
    return _pad_rows(packed, -(-at // V7X_SUBLANES) * V7X_SUBLANES), spans


def _unpack(packed, spans, shapes):
    return [packed[at:at + r].reshape(-1)[:size].reshape(shape) for (at, r, size), shape in zip(spans, shapes)]


def kernel(x, positions, w_in, conv_w, sinks, g_attn, g_conv, w_out, ln1_g, ln1_b, w_gate, w_up, w_down, ln2_g, ln2_b, loss_target, m_w_in, m_conv_w, m_sinks, m_g_attn, m_g_conv, m_w_out, m_ln1_g, m_ln1_b, m_w_gate, m_w_up, m_w_down, m_ln2_g, m_ln2_b, v_w_in, v_conv_w, v_sinks, v_g_attn, v_g_conv, v_w_out, v_ln1_g, v_ln1_b, v_w_gate, v_w_up, v_w_down, v_ln2_g, v_ln2_b):
    _, s, d = x.shape
    d_ff = N_DEV * w_gate.shape[2]
    dm = _Dims(s, d, d_ff)
    aw, cw, nq, inw = dm.aw, dm.cw, dm.nq, dm.inw
    x2 = x[0]
    pos = positions[0].reshape(s, 1)
    inv_freq = ROPE_THETA ** (-jnp.arange(0, ROT_DIM, 2, dtype=_F32) / ROT_DIM)
    invf = jnp.tile(inv_freq, V7X_LANES // (ROT_DIM // 2)).reshape(1, V7X_LANES)

    conv_cols = conv_w.shape[2]
    sh_in, sh_out = w_in[0].T.astype(_CDT), w_out[0].astype(_CDT)
    sh_gate, sh_up, sh_down = w_gate[0].T.astype(_CDT), w_up[0].T.astype(_CDT), w_down[0].astype(_CDT)
    r_in, r_out, r_ff = sh_in.shape[0], sh_out.shape[0], sh_gate.shape[0]
    q_ff = r_ff // 4
    assert q_ff % 16 == 0
    def cast_body(x_ref, o_ref):
        o_ref[...] = x_ref[...].astype(_CDT)

    x_c, w_in_t, conv_all = _row_kernel("cast_x_gather_w_in", cast_body, [x2], [], [((s, d), _CDT)], [], comm=[
        _gather_op([(sh_in, None, 0, r_in), (_pad_rows(conv_w[0], 16), None, 0, 16)])])
    conv_full = conv_all.reshape(N_DEV, 16, conv_cols)[:, :3, :].transpose(1, 0, 2).reshape(3, cw)
    conv_w8 = _pad_rows(conv_full, V7X_SUBLANES)

    tm = _pick(s, (1024, 512, 256, 128))
    tn_in = _pick(inw, (512, 256, 128))
    tn_ff = _pick(d_ff, (512, 256, 128))
    tr = _pick(s, (512, 256, 128))

    proj, w_out_f, w_gate_t = _matmul(
        "proj", [[(x_c, w_in_t, "nt")]], s, inw, d, tm, tn_in, d, [],
        [((s, inw), _F32, (tm, tn_in), _tile_ij)], _store_epilogue,
        comm=[_gather_op([(sh_out, None, 0, r_out), (sh_gate, None, 0, 2 * q_ff)])])
    mixed, attn, lse, y_conv, w_gate_t, w_up_t = _mixer_fwd(
        dm, proj, pos, invf, sinks, g_attn, g_conv, conv_w8,
        comm=[_gather_op([(sh_gate, w_gate_t, 2 * q_ff, r_ff), (sh_up, None, 0, 2 * q_ff)])])

    def residual_epilogue(accs, ex, out, first):
        out[0][...] = DEEPNORM_ALPHA * ex[0][...] + accs[0]

    tn_d = _pick(d, (512,))
    r1, w_up_t = _matmul(
        "out_proj", [[(mixed, w_out_f, "nn")]], s, d, d, tm, tn_d, d, [(x2, (tm, tn_d), _tile_ij)],
        [((s, d), _F32, (tm, tn_d), _tile_ij)], residual_epilogue,
        comm=[_gather_op([(sh_up, w_up_t, 2 * q_ff, 3 * q_ff)])])
    h1, h1_c, xhat1, rstd1, w_up_t = _ln1_fwd_rows(
        r1, ln1_g, ln1_b, comm=[_gather_op([(sh_up, w_up_t, 3 * q_ff, r_ff)])])

    def swiglu_epilogue(accs, ex, out, first):
        gate_v, up_v = accs
        out[0][...] = gate_v
        out[1][...] = up_v
        out[2][...] = (gate_v * jax.nn.sigmoid(gate_v) * up_v).astype(_CDT)

    gate, up, act, w_down_f = _matmul(
        "gate_up", [[(h1_c, w_gate_t, "nt")], [(h1_c, w_up_t, "nt")]], s, d_ff, d, tm, tn_ff, d, [],
        [((s, d_ff), _F32, (tm, tn_ff), _tile_ij), ((s, d_ff), _F32, (tm, tn_ff), _tile_ij),
         ((s, d_ff), _CDT, (tm, tn_ff), _tile_ij)], swiglu_epilogue,
        comm=[_gather_op([(sh_down, None, 0, r_ff)])])

    (r2,) = _matmul("down", [[(act, w_down_f, "nn")]], s, d, d_ff, tr, tn_d, d_ff, [(h1, (tr, tn_d), _tile_ij)],
                    [((s, d), _F32, (tr, tn_d), _tile_ij)], residual_epilogue)
    dr2, dr2_c, loss_acc, d_ln2_g, d_ln2_b = _ln2_loss_bwd(r2, loss_target[0], ln2_g, ln2_b)

    def swiglu_bwd_epilogue(accs, ex, out, first):
        gate_v, up_v = ex[0][...], ex[1][...]
        sig = jax.nn.sigmoid(gate_v)
        out[0][...] = (accs[0] * up_v * (sig * (1.0 + gate_v * (1.0 - sig)))).astype(_CDT)
        out[1][...] = (accs[0] * (gate_v * sig)).astype(_CDT)

    dgate, dup = _matmul(
        "dact", [[(dr2_c, w_down_f, "nt")]], s, d_ff, d, tm, tn_ff, d,
        [(gate, (tm, tn_ff), _tile_ij), (up, (tm, tn_ff), _tile_ij)],
        [((s, d_ff), _CDT, (tm, tn_ff), _tile_ij), ((s, d_ff), _CDT, (tm, tn_ff), _tile_ij)], swiglu_bwd_epilogue,
        n_split=2)
    def weight_grad(name, a, b, comm=()):
        rows = a.shape[1]
        tw, tn_w = _pick(rows, (512, 256, 128)), _pick(d, (1024, 512))
        return _matmul(name, [[(a, b, "tn")]], rows, d, s, tw, tn_w, s, [],
                       [((rows, d), _CDT, (tw, tn_w), _tile_ij)], _store_epilogue, comm=comm, j_outer=True)

    (dw_down,) = weight_grad("dw_down", act, dr2_c)
    dw_gate_t, x_down = weight_grad("dw_gate", dgate, h1_c, comm=[_exchange_op([dw_down])])
    q_down = _pair_sum("chip_sum_w_down", dw_down, x_down)
    dw_up_t, l_down, x_gate = weight_grad(
        "dw_up", dup, h1_c, comm=[_chip_send_op([(q_down, None, 0, 2 * q_ff)]), _exchange_op([dw_gate_t])])
    q_gate = _pair_sum("chip_sum_w_gate", dw_gate_t, x_gate)

    tn_h = _pick(d, (256,))
    dh1, l_down, l_gate, x_up = _matmul(
        "dh1", [[(dgate, w_gate_t, "nn"), (dup, w_up_t, "nn")]], s, d, d_ff, tr, tn_h, d_ff,
        [(dr2, (tr, tn_h), _tile_ij)], [((s, d), _F32, (tr, tn_h), _tile_ij)], residual_epilogue,
        comm=[_chip_send_op([(q_down, l_down, 2 * q_ff, r_ff), (q_gate, None, 0, r_ff)]), _exchange_op([dw_up_t])])
    q_up = _pair_sum("chip_sum_w_up", dw_up_t, x_up)
    dr1, dr1_c, d_ln1_g, d_ln1_b = _ln1_bwd_rows(dh1, xhat1, rstd1, ln1_g)
    (dmixed,) = _matmul("dmixed", [[(dr1_c, w_out_f, "nt")]], s, d, d, tm, tn_d, d, [],
                        [((s, d), _F32, (tm, tn_d), _tile_ij)], _store_epilogue)
    (dw_out,) = weight_grad("dw_out", mixed, dr1_c)
    dproj, dkv, d_g_attn, d_g_conv, d_sinks, d_conv8, l_up, x_out = _mixer_bwd(
        dm, proj, pos, invf, sinks, g_attn, g_conv, conv_w8, dmixed, attn, lse, y_conv,
        comm=[_chip_send_op([(q_up, None, 0, r_ff)]), _exchange_op([dw_out])])
    dproj = _patch_columns("dproj_kv", dproj, dkv, dm.o_k)
    q_out = _pair_sum("chip_sum_w_out", dw_out, x_out)
    dw_in_t, l_out = weight_grad("dw_in", dproj, x_c, comm=[_chip_send_op([(q_out, None, 0, r_out)])])
    (x_in,) = _comm_kernel("exchange_w_in", [_exchange_op([dw_in_t])])
    q_in = _pair_sum("chip_sum_w_in", dw_in_t, x_in)

    grad_x, l_in = _matmul("dx", [[(dproj, w_in_t, "nn")]], s, d, inw, tr, tn_d, inw,
                           [(dr1, (tr, tn_d), _tile_ij)], [((s, d), _F32, (tr, tn_d), _tile_ij)], residual_epilogue,
                           comm=[_chip_send_op([(q_in, None, 0, r_in)])])

    small_parts = [d_conv8[:3], d_sinks, d_g_attn, d_g_conv, d_ln1_g, d_ln1_b, d_ln2_g, d_ln2_b]
    packed, spans = _pack(small_parts)
    reduced = _unpack(_all_reduce_small("reduce_small", packed), spans, [p.shape for p in small_parts])
    g_conv_full, g_sinks, g_g_attn, g_g_conv, g_ln1_g, g_ln1_b, g_ln2_g, g_ln2_b = reduced
    me = _linear(*_position())
    g_conv_w = lax.dynamic_slice(g_conv_full, (0, me * conv_cols), (3, conv_cols))
    loss = lax.psum(loss_acc[0, 0], ("x", "y", "c"))

    big = {"w_in": (w_in[0].T, l_in, m_w_in[0].T, v_w_in[0].T), "w_out": (w_out[0], l_out, m_w_out[0], v_w_out[0]),
           "w_gate": (w_gate[0].T, l_gate, m_w_gate[0].T, v_w_gate[0].T),
           "w_up": (w_up[0].T, l_up, m_w_up[0].T, v_w_up[0].T), "w_down": (w_down[0], l_down, m_w_down[0], v_w_down[0])}
    res = {nm: tuple(_adamw(f"adamw_{nm}", w, slots, m, v)) for nm, (w, slots, m, v) in big.items()}
    for nm in ("w_in", "w_gate", "w_up"):
        res[nm] = tuple(a.T for a in res[nm])
    small_names = ["conv_w", "sinks", "g_attn", "g_conv", "ln1_g", "ln1_b", "ln2_g", "ln2_b"]
    small_w = [conv_w, sinks, g_attn, g_conv, ln1_g, ln1_b, ln2_g, ln2_b]
    small_g = [g_conv_w[None], g_sinks, g_g_attn, g_g_conv, g_ln1_g, g_ln1_b, g_ln2_g, g_ln2_b]
    small_m = [m_conv_w, m_sinks, m_g_attn, m_g_conv, m_ln1_g, m_ln1_b, m_ln2_g, m_ln2_b]
    small_v = [v_conv_w, v_sinks, v_g_attn, v_g_conv, v_ln1_g, v_ln1_b, v_ln2_g, v_ln2_b]
    pw, sp = _pack(small_w)
    pg, _ = _pack(small_g)
    pm, _ = _pack(small_m)
    pv, _ = _pack(small_v)
    shapes = [w.shape for w in small_w]
    _, sd, sm, sv = [_unpack(p, sp, shapes) for p in _adamw("adamw_small", pw, pg[None], pm, pv)]
    for i, nm in enumerate(small_names):
        res[nm] = (small_g[i].reshape(shapes[i]), sd[i], sm[i], sv[i])

    order = ["w_in", "conv_w", "sinks", "g_attn", "g_conv", "w_out", "ln1_g", "ln1_b", "w_gate", "w_up", "w_down", "ln2_g", "ln2_b"]

    def lead(a, nm):
        return a[None] if nm in big else a

    return (loss, grad_x[None],
            *[lead(res[nm][0], nm) for nm in order], *[lead(res[nm][1], nm) for nm in order],
            *[lead(res[nm][2], nm) for nm in order], *[lead(res[nm][3], nm) for nm in order])
```

```python
import functools

import jax
import jax.numpy as jnp
from jax import lax
from jax.experimental import pallas as pl
from jax.experimental.pallas import tpu as pltpu

_F32 = jnp.float32
_CDT = jnp.bfloat16

HEAD_DIM = 64
WINDOW = 128
N_KV_HEADS = 4
KV_WIDTH = N_KV_HEADS * HEAD_DIM
ROT_DIM = HEAD_DIM // 4
ROPE_THETA = 500000.0
ATTN_SCALE = HEAD_DIM ** -0.5
DEPTH = 1
DEEPNORM_ALPHA = (2 * DEPTH) ** 0.25
LN_EPS = 1e-5
RMS_EPS = 1e-6
ADAM_LR = 0.001
ADAM_B1 = 0.9
ADAM_B2 = 0.999
ADAM_EPS = 1e-08
ADAM_WD = 0.01
ADAM_STEP = 10
N_DEV = 8
MASKED = -1e30

V7X_VMEM_BYTES = 64 * 1024 * 1024
V7X_LANES = 128
V7X_SUBLANES = 8
_MESH = pl.DeviceIdType.MESH
_ANY = pl.BlockSpec(memory_space=pl.ANY)


def _vmem_limit(block_bytes, scratch_bytes=0):
    want = 2 * block_bytes + scratch_bytes + 16 * 1024 * 1024
    return int(min(max(want, 32 * 1024 * 1024), V7X_VMEM_BYTES - 8 * 1024 * 1024))


def _nbytes(shape, dtype):
    n = 1
    for s in shape:
        n *= s
    return n * jnp.dtype(dtype).itemsize


def _pick(n, candidates):
    for c in candidates:
        if n % c == 0:
            return c
    raise ValueError(f"no tile of {candidates} divides {n}")


_DOT_DIMS = {"nn": ((1,), (0,)), "nt": ((1,), (1,)), "tn": ((0,), (0,))}


def _dot(a, b, mode):
    return lax.dot_general(a.astype(_CDT), b.astype(_CDT), (_DOT_DIMS[mode], ((), ())),
                           preferred_element_type=_F32)


def _accumulate(ref, val, first):
    @pl.when(first)
    def _():
        ref[...] = val

    @pl.when(jnp.logical_not(first))
    def _():
        ref[...] += val


class _Comm:
    def __init__(self, inputs, outputs, aliases, sems, start, finish, middle=None):
        self.inputs, self.outputs, self.aliases, self.sems = inputs, outputs, aliases, sems
        self.start, self.finish, self.middle = start, finish, middle


class _CommArgs:
    def __init__(self, comms, n_in_before, n_out_before):
        self.comms, self.operands, self.out_shape, self.aliases, self.sems, self.at = comms, [], [], {}, [], []
        for cm in comms:
            self.at.append((len(self.operands), len(self.out_shape), len(self.sems)))
            for i_in, i_out in cm.aliases.items():
                self.aliases[n_in_before + len(self.operands) + i_in] = n_out_before + len(self.out_shape) + i_out
            self.operands += cm.inputs
            self.out_shape += cm.outputs
            self.sems += cm.sems

    def _each(self, in_refs, out_refs, sem_refs):
        for cm, (i0, o0, s0) in zip(self.comms, self.at):
            yield cm, (in_refs[i0:i0 + len(cm.inputs)], out_refs[o0:o0 + len(cm.outputs)], sem_refs[s0:s0 + len(cm.sems)])

    def start(self, in_refs, out_refs, sem_refs):
        for cm, refs in self._each(in_refs, out_refs, sem_refs):
            cm.start(*refs)

    def finish(self, in_refs, out_refs, sem_refs):
        for cm, refs in self._each(in_refs, out_refs, sem_refs):
            cm.finish(*refs)

    @property
    def has_middle(self):
        return any(cm.middle is not None for cm in self.comms)

    def middle(self, in_refs, out_refs, sem_refs):
        for cm, refs in self._each(in_refs, out_refs, sem_refs):
            if cm.middle is not None:
                cm.middle(*refs)


def _matmul(name, groups, m, n, k, tm, tn, tk, extras, outs, epilogue, comm=(), j_outer=False, n_split=1):
    assert m % tm == 0 and n % tn == 0 and k % tk == 0, (name, m, n, k, tm, tn, tk)
    nk = k // tk
    assert n_split == 1 or (nk == 1 and tn % (n_split * V7X_LANES) == 0), (name, n_split)
    terms = [t for g in groups for t in g]
    operands, in_specs, block_bytes = [], [], 0

    def spec(blk, imap):
        return pl.BlockSpec(blk, (lambda g0, g1, kk: imap(g1, g0, kk)) if j_outer else imap)

    for a, b, mode in terms:
        assert a.shape == ((k, m) if mode == "tn" else (m, k)), (name, a.shape, mode)
        assert b.shape == ((n, k) if mode == "nt" else (k, n)), (name, b.shape, mode)
        if mode == "tn":
            a_blk, a_map = (tk, tm), (lambda i, j, kk: (kk, i))
        else:
            a_blk, a_map = (tm, tk), (lambda i, j, kk: (i, kk))
        if mode == "nt":
            b_blk, b_map = (tn, tk), (lambda i, j, kk: (j, kk))
        else:
            b_blk, b_map = (tk, tn), (lambda i, j, kk: (kk, j))
        operands += [a, b]
        in_specs += [spec(a_blk, a_map), spec(b_blk, b_map)]
        block_bytes += _nbytes(a_blk, a.dtype) + _nbytes(b_blk, b.dtype)
    for arr, blk, imap in extras:
        operands.append(arr)
        in_specs.append(spec(blk, lambda i, j, kk, imap=imap: imap(i, j)))
        block_bytes += _nbytes(blk, arr.dtype)
    out_shape, out_specs = [], []
    for shape, dtype, blk, imap in outs:
        out_shape.append(jax.ShapeDtypeStruct(shape, dtype))
        out_specs.append(spec(blk, lambda i, j, kk, imap=imap: imap(i, j)))
        block_bytes += _nbytes(blk, dtype)
    n_terms, n_extra, n_out, n_groups = len(terms), len(extras), len(outs), len(groups)
    scratch = [pltpu.VMEM((tm, tn), _F32) for _ in range(n_groups)] if nk > 1 else []
    ca = _CommArgs(list(comm), len(operands), n_out)
    n_cin, n_cout, n_acc = len(ca.operands), len(ca.out_shape), len(scratch)
    tiles = (m // tm, n // tn)
    grid = (tiles[1], tiles[0], nk) if j_outer else (tiles[0], tiles[1], nk)

    def body(*refs):
        refs = list(refs)
        term_refs = [refs.pop(0) for _ in range(2 * n_terms)]
        extra_refs = [refs.pop(0) for _ in range(n_extra)]
        cin_refs = [refs.pop(0) for _ in range(n_cin)]
        out_refs = [refs.pop(0) for _ in range(n_out)]
        cout_refs = [refs.pop(0) for _ in range(n_cout)]
        acc_refs = [refs.pop(0) for _ in range(n_acc)]
        sem_refs = refs
        g0, g1, kk = pl.program_id(0), pl.program_id(1), pl.program_id(2)
        first = jnp.logical_and(g0 == 0, g1 == 0)
        if comm:
            @pl.when(jnp.logical_and(first, kk == 0))
            def _():
                ca.start(cin_refs, cout_refs, sem_refs)
        if ca.has_middle:
            step = (g0 * grid[1] + g1) * nk + kk

            @pl.when(step == (grid[0] * grid[1] * nk) // 2)
            def _():
                ca.middle(cin_refs, cout_refs, sem_refs)
        def products(cols):
            partial, t = [], 0
            for g in groups:
                s = None
                for _, _, mode in g:
                    b_ref = term_refs[2 * t + 1]
                    b = b_ref[...] if cols is None else (b_ref[cols, :] if mode == "nt" else b_ref[:, cols])
                    d = _dot(term_refs[2 * t][...], b, mode)
                    s = d if s is None else s + d
                    t += 1
                partial.append(s)
            return partial

        if n_split > 1:
            width = tn // n_split
            chunk = lambda c: pl.ds(c * width, width)
            ahead = products(chunk(0))
            for c in range(n_split):
                done, cols = ahead, chunk(c)
                if c + 1 < n_split:
                    ahead = products(chunk(c + 1))
                view = lambda ref: ref.at[:, cols] if tuple(ref.shape) == (tm, tn) else ref
                epilogue(done, [view(r) for r in extra_refs], [view(r) for r in out_refs], first)
        elif nk == 1:
            epilogue(products(None), extra_refs, out_refs, first)
        else:
            partial = products(None)
            for acc, p in zip(acc_refs, partial):
                _accumulate(acc, p, kk == 0)

            @pl.when(kk == nk - 1)
            def _():
                epilogue([acc[...] for acc in acc_refs], extra_refs, out_refs, first)
        if comm:
            @pl.when(jnp.logical_and(jnp.logical_and(g0 == grid[0] - 1, g1 == grid[1] - 1), kk == nk - 1))
            def _():
                ca.finish(cin_refs, cout_refs, sem_refs)

    res = pl.pallas_call(
        body, name=name, grid=grid,
        in_specs=in_specs + [_ANY] * n_cin, out_specs=out_specs + [_ANY] * n_cout,
        out_shape=out_shape + ca.out_shape, scratch_shapes=scratch + ca.sems, input_output_aliases=ca.aliases,
        compiler_params=pltpu.CompilerParams(
            dimension_semantics=("arbitrary", "arbitrary", "arbitrary"),
            vmem_limit_bytes=_vmem_limit(block_bytes, n_groups * tm * tn * 4 if nk > 1 else 0)),
    )(*operands, *ca.operands)
    return list(res[:n_out]) + list(res[n_out:])


def _store_epilogue(accs, extra_refs, out_refs, first):
    for acc, ref in zip(accs, out_refs):
        ref[...] = acc.astype(ref.dtype)


def _tile_ij(i, j):
    return (i, j)


def _row_i(i, j):
    return (i, 0)


def _whole(i, j):
    return (0, 0)


def _mean(v):
    return jnp.mean(v, axis=-1, keepdims=True)


def _ln_fwd(r, g, b):
    xc = r - _mean(r)
    rstd = lax.rsqrt(_mean(xc * xc) + LN_EPS)
    xhat = xc * rstd
    return xhat * g + b, xhat, rstd


def _ln_bwd(dy, xhat, rstd, g):
    dxh = dy * g
    dr = rstd * (dxh - _mean(dxh) - xhat * _mean(dxh * xhat))
    return dr, jnp.sum(dy * xhat, axis=0, keepdims=True), jnp.sum(dy, axis=0, keepdims=True)


def _rms_fwd(a, g):
    rstd = lax.rsqrt(_mean(a * a) + RMS_EPS)
    return a * rstd * g


def _rms_bwd(dm, a, g):
    rstd = lax.rsqrt(_mean(a * a) + RMS_EPS)
    nhat = a * rstd
    dn = dm * g
    da = rstd * (dn - nhat * _mean(dn * nhat))
    return da, jnp.sum(dm * nhat, axis=0, keepdims=True)


def _lane(shape):
    return lax.broadcasted_iota(jnp.int32, shape, 1)


def _row(shape):
    return lax.broadcasted_iota(jnp.int32, shape, 0)


def _rope_tables(pos, invf):
    ang = pos.astype(_F32) * invf
    lane = _lane(ang.shape)
    in_rot = (lane % HEAD_DIM) < ROT_DIM
    first = (lane % ROT_DIM) < ROT_DIM // 2
    cos = jnp.where(in_rot, jnp.cos(ang), 1.0)
    sin = jnp.sin(ang)
    sgn = jnp.where(in_rot, jnp.where(first, -sin, sin), 0.0)
    return cos, sgn


def _rope(t, cos, sgn, sign):
    half = ROT_DIM // 2
    first = (_lane(t.shape) % ROT_DIM) < half
    partner = jnp.where(first, pltpu.roll(t, V7X_LANES - half, 1), pltpu.roll(t, half, 1))
    return t * cos + partner * (sgn * sign)


def _dup_head(t, h):
    g = t[:, 128 * (h // 2):128 * (h // 2) + 128]
    r = pltpu.roll(g, HEAD_DIM, 1)
    lo = _lane(g.shape) < HEAD_DIM
    return jnp.where(lo, g, r) if h % 2 == 0 else jnp.where(lo, r, g)


def _fold_halves(t):
    return t + pltpu.roll(t, HEAD_DIM, 1)


def _halves(t):
    lo = _lane(t.shape) < HEAD_DIM
    zero = jnp.zeros_like(t)
    return jnp.where(lo, t, zero), jnp.where(lo, zero, t)


def _band_mask(n_heads, n_keys, first_block):
    shape = (n_heads * WINDOW, n_keys)
    i = jnp.bitwise_and(_row(shape), WINDOW - 1)
    j = _lane(shape)
    valid = jnp.logical_and(j >= i + 1, j <= i + WINDOW)
    if first_block is not None:
        valid = jnp.logical_and(valid, jnp.logical_or(j >= WINDOW, jnp.logical_not(first_block)))
    return valid


def _stack_heads(pairs):
    return jnp.concatenate([half for t in pairs for half in _halves(t)], axis=0).astype(_CDT)


def _unstack_heads(t, n_pairs):
    lo = _lane((WINDOW, 128)) < HEAD_DIM
    return [jnp.where(lo, t[2 * WINDOW * i:2 * WINDOW * i + WINDOW], t[2 * WINDOW * i + WINDOW:2 * WINDOW * (i + 1)])
            for i in range(n_pairs)]


def _per_head(values):
    n_rows = len(values) * WINDOW
    block = jnp.right_shift(_row((n_rows, 1)), WINDOW.bit_length() - 1)
    out = jnp.zeros((n_rows, 1), _F32)
    for k, v in enumerate(values):
        out = jnp.where(block == k, v, out)
    return out


def _shift_down(z, halo, k):
    rows = z.shape[0]
    out = pltpu.roll(z, k, 0)
    r = _row(z.shape)
    for t in range(k):
        out = jnp.where(r == t, halo[V7X_SUBLANES - k + t:V7X_SUBLANES - k + t + 1, :], out)
    del rows
    return out


def _shift_up(z, halo, k):
    rows = z.shape[0]
    out = pltpu.roll(z, rows - k, 0)
    r = _row(z.shape)
    for t in range(k):
        out = jnp.where(r == rows - k + t, halo[t:t + 1, :], out)
    return out


class _Dims:
    def __init__(self, s, d, d_ff):
        self.s, self.d, self.d_ff = s, d, d_ff
        self.aw = d // 2
        self.cw = d - self.aw
        self.nq = self.aw // HEAD_DIM
        self.group = self.nq // N_KV_HEADS
        assert self.group % 2 == 0, "a 128-lane pair of query heads must share its kv head"
        self.inw = self.aw + 2 * KV_WIDTH + 3 * self.cw
        self.o_k = self.aw
        self.o_v = self.aw + KV_WIDTH
        self.o_cg = self.aw + 2 * KV_WIDTH
        self.o_bg = self.o_cg + self.cw
        self.o_u = self.o_bg + self.cw
        self.nb = s // WINDOW
        assert s % WINDOW == 0


def _carrying(body, n_in, n_out, n_steps, ca, n_scratch=0):
    n_cin, n_cout = len(ca.operands), len(ca.out_shape)

    def wrapped(*refs):
        refs = list(refs)
        in_refs = [refs.pop(0) for _ in range(n_in)]
        cin_refs = [refs.pop(0) for _ in range(n_cin)]
        out_refs = [refs.pop(0) for _ in range(n_out)]
        cout_refs = [refs.pop(0) for _ in range(n_cout)]
        scratch_refs = [refs.pop(0) for _ in range(n_scratch)]
        if ca.comms:
            @pl.when(pl.program_id(0) == 0)
            def _():
                ca.start(cin_refs, cout_refs, refs)
        if ca.has_middle:
            @pl.when(pl.program_id(0) == n_steps // 2)
            def _():
                ca.middle(cin_refs, cout_refs, refs)
        body(*in_refs, *out_refs, *scratch_refs)
        if ca.comms:
            @pl.when(pl.program_id(0) == n_steps - 1)
            def _():
                ca.finish(cin_refs, cout_refs, refs)

    return wrapped


def _row_kernel(name, body, rows_in, vecs_in, rows_out, vecs_out, comm=()):
    s = rows_in[0].shape[0]
    tr = _pick(s, (256, 128))
    row = lambda a: pl.BlockSpec((tr, a[1] if isinstance(a, tuple) else a.shape[1]), lambda i: (i, 0))
    vec = lambda shape: pl.BlockSpec(tuple(shape), lambda i: (0, 0))
    n_in, n_out = len(rows_in) + len(vecs_in), len(rows_out) + len(vecs_out)
    ca = _CommArgs(list(comm), n_in, n_out)
    blocks = sum(_nbytes((tr, a.shape[1]), a.dtype) for a in rows_in) + sum(_nbytes((tr, sh[1]), dt) for sh, dt in rows_out)
    res = pl.pallas_call(
        _carrying(body, n_in, n_out, s // tr, ca), name=name, grid=(s // tr,),
        in_specs=[row(a) for a in rows_in] + [vec(v.shape) for v in vecs_in] + [_ANY] * len(ca.operands),
        out_specs=[row(sh) for sh, _ in rows_out] + [vec(sh) for sh, _ in vecs_out] + [_ANY] * len(ca.out_shape),
        out_shape=[jax.ShapeDtypeStruct(sh, dt) for sh, dt in list(rows_out) + list(vecs_out)] + ca.out_shape,
        scratch_shapes=ca.sems, input_output_aliases=ca.aliases,
        compiler_params=pltpu.CompilerParams(dimension_semantics=("arbitrary",), vmem_limit_bytes=_vmem_limit(blocks)),
    )(*rows_in, *vecs_in, *ca.operands)
    return list(res)


def _ln2_loss_bwd(r2, target, gain, bias, comm=()):
    s, d = r2.shape

    def body(r_ref, t_ref, g_ref, b_ref, dr_ref, drc_ref, loss_ref, dg_ref, db_ref):
        first = pl.program_id(0) == 0
        yv, xhat, rstd = _ln_fwd(r_ref[...], g_ref[...], b_ref[...])
        err = yv - t_ref[...]
        dr2, dg, db = _ln_bwd(err * (1.0 / d), xhat, rstd, g_ref[...])
        dr_ref[...] = dr2
        drc_ref[...] = dr2.astype(_CDT)
        _accumulate(loss_ref, jnp.zeros(loss_ref.shape, _F32) + 0.5 * jnp.sum(err * err) * (1.0 / d), first)
        _accumulate(dg_ref, dg, first)
        _accumulate(db_ref, db, first)

    return _row_kernel("ln2_loss_bwd", body, [r2, target], [gain, bias], [((s, d), _F32), ((s, d), _CDT)],
                       [((V7X_SUBLANES, V7X_LANES), _F32), ((1, d), _F32), ((1, d), _F32)], comm)


def _ln1_fwd_rows(r1, gain, bias, comm=()):
    s, d = r1.shape

    def body(r_ref, g_ref, b_ref, h_ref, hc_ref, xhat_ref, rstd_ref):
        h1, xhat, rstd = _ln_fwd(r_ref[...], g_ref[...], b_ref[...])
        h_ref[...] = h1
        hc_ref[...] = h1.astype(_CDT)
        xhat_ref[...] = xhat
        rstd_ref[...] = rstd

    return _row_kernel("ln1", body, [r1], [gain, bias],
                       [((s, d), _F32), ((s, d), _CDT), ((s, d), _F32), ((s, 1), _F32)], [], comm)


def _ln1_bwd_rows(dh1, xhat, rstd, gain, comm=()):
    s, d = dh1.shape

    def body(dh_ref, xhat_ref, rstd_ref, g_ref, dr_ref, drc_ref, dg_ref, db_ref):
        first = pl.program_id(0) == 0
        dr1, dg, db = _ln_bwd(dh_ref[...], xhat_ref[...], rstd_ref[...], g_ref[...])
        dr_ref[...] = dr1
        drc_ref[...] = dr1.astype(_CDT)
        _accumulate(dg_ref, dg, first)
        _accumulate(db_ref, db, first)

    return _row_kernel("ln1_bwd", body, [dh1, xhat, rstd], [gain], [((s, d), _F32), ((s, d), _CDT)],
                       [((1, d), _F32), ((1, d), _F32)], comm)


def _mixer_fwd(dm, proj, pos, invf, sinks, g_attn, g_conv, conv_w8, comm=()):
    s, d, aw, cw, nq, inw, nb = dm.s, dm.d, dm.aw, dm.cw, dm.nq, dm.inw, dm.nb

    def body(pp_ref, pc_ref, posp_ref, posc_ref, invf_ref, sinks_ref, ga_ref, gc_ref, cw_ref,
             mixed_ref, attn_ref, lse_ref, y_ref):
        n = pl.program_id(0)
        cos_c, sgn_c = _rope_tables(posc_ref[...], invf_ref[...])
        cos_p, sgn_p = _rope_tables(posp_ref[...], invf_ref[...])
        kk = jnp.concatenate(
            [jnp.concatenate([_rope(ref[:, dm.o_k + 128 * g:dm.o_k + 128 * g + 128], c, sg, 1.0)
                              for g in range(KV_WIDTH // 128)], axis=1)
             for ref, c, sg in ((pp_ref, cos_p, sgn_p), (pc_ref, cos_c, sgn_c))], axis=0)
        vv = jnp.concatenate([pp_ref[:, dm.o_v:dm.o_v + KV_WIDTH], pc_ref[:, dm.o_v:dm.o_v + KV_WIDTH]], axis=0)
        group, pairs = dm.group, dm.group // 2
        valid = _band_mask(group, 2 * WINDOW, n == 0)
        for h in range(N_KV_HEADS):
            k2, v2 = _dup_head(kk, h).astype(_CDT), _dup_head(vv, h).astype(_CDT)
            q4 = _stack_heads([_rope(pc_ref[:, 128 * j:128 * j + 128], cos_c, sgn_c, 1.0)
                               for j in range(pairs * h, pairs * (h + 1))])
            sc = jnp.where(valid, _dot(q4, k2, "nt") * ATTN_SCALE, MASKED)
            sink = _per_head([sinks_ref[0, group * h + r] for r in range(group)])
            mx = jnp.maximum(jnp.max(sc, axis=1, keepdims=True), sink)
            p = jnp.exp(sc - mx)
            den = jnp.sum(p, axis=1, keepdims=True) + jnp.exp(sink - mx)
            out = _unstack_heads(_dot(p / den, v2, "nn"), pairs)
            lse = mx + jnp.log(den)
            for r in range(group):
                lse_ref[:, group * h + r:group * h + r + 1] = lse[WINDOW * r:WINDOW * (r + 1)]
            for i in range(pairs):
                j = pairs * h + i
                attn_ref[:, 128 * j:128 * j + 128] = out[i]
        mixed_ref[:, 0:aw] = _rms_fwd(attn_ref[...], ga_ref[...]).astype(mixed_ref.dtype)

        z = pc_ref[:, dm.o_cg:dm.o_cg + cw] * pc_ref[:, dm.o_u:dm.o_u + cw]
        top = WINDOW - V7X_SUBLANES
        halo = pp_ref[top:WINDOW, dm.o_cg:dm.o_cg + cw] * pp_ref[top:WINDOW, dm.o_u:dm.o_u + cw]
        halo = jnp.where(n == 0, jnp.zeros_like(halo), halo)
        y = cw_ref[0:1, :] * _shift_down(z, halo, 2) + cw_ref[1:2, :] * _shift_down(z, halo, 1) + cw_ref[2:3, :] * z
        y_ref[...] = y
        conv = pc_ref[:, dm.o_bg:dm.o_bg + cw] * y
        mixed_ref[:, aw:d] = _rms_fwd(conv, gc_ref[...]).astype(mixed_ref.dtype)

    prev = lambda n: (jnp.maximum(n - 1, 0), 0)
    cur = lambda n: (n, 0)
    fixed = lambda n: (0, 0)
    blocks = 2 * WINDOW * inw * 4 + WINDOW * (d * 2 + aw * 4 + cw * 4 + nq * 4)
    ca = _CommArgs(list(comm), 9, 4)
    return pl.pallas_call(
        _carrying(body, 9, 4, nb, ca), name="mixer_fwd", grid=(nb,),
        in_specs=[pl.BlockSpec((WINDOW, inw), prev), pl.BlockSpec((WINDOW, inw), cur),
                  pl.BlockSpec((WINDOW, 1), prev), pl.BlockSpec((WINDOW, 1), cur),
                  pl.BlockSpec((1, V7X_LANES), fixed), pl.BlockSpec(memory_space=pltpu.SMEM),
                  pl.BlockSpec((1, aw), fixed), pl.BlockSpec((1, cw), fixed), pl.BlockSpec((V7X_SUBLANES, cw), fixed)]
        + [_ANY] * len(ca.operands),
        out_specs=[pl.BlockSpec((WINDOW, d), cur), pl.BlockSpec((WINDOW, aw), cur),
                   pl.BlockSpec((WINDOW, nq), cur), pl.BlockSpec((WINDOW, cw), cur)] + [_ANY] * len(ca.out_shape),
        out_shape=[jax.ShapeDtypeStruct((s, d), _CDT), jax.ShapeDtypeStruct((s, aw), _F32),
                   jax.ShapeDtypeStruct((s, nq), _F32), jax.ShapeDtypeStruct((s, cw), _F32)] + ca.out_shape,
        scratch_shapes=ca.sems, input_output_aliases=ca.aliases,
        compiler_params=pltpu.CompilerParams(dimension_semantics=("arbitrary",), vmem_limit_bytes=_vmem_limit(blocks)),
    )(proj, proj, pos, pos, invf, sinks, g_attn, g_conv, conv_w8, *ca.operands)


def _patch_columns(name, a, part, offset):
    s, pw = part.shape
    assert offset % pw == 0 and pw % V7X_LANES == 0
    tr = _pick(s, (512, 256, 128))

    def body(a_ref, p_ref, o_ref):
        del a_ref
        o_ref[...] = p_ref[...]

    return pl.pallas_call(
        body, name=name, grid=(s // tr,),
        in_specs=[_ANY, pl.BlockSpec((tr, pw), lambda i: (i, 0))],
        out_specs=pl.BlockSpec((tr, pw), lambda i: (i, offset // pw)),
        out_shape=jax.ShapeDtypeStruct(a.shape, a.dtype), input_output_aliases={0: 0},
        compiler_params=pltpu.CompilerParams(dimension_semantics=("arbitrary",)),
    )(a, part)


def _mixer_bwd(dm, proj, pos, invf, sinks, g_attn, g_conv, conv_w8, dmixed, attn, lse, y, comm=()):
    s, d, aw, cw, nq, inw, nb = dm.s, dm.d, dm.aw, dm.cw, dm.nq, dm.inw, dm.nb

    def body(pp_ref, pc_ref, pn_ref, posp_ref, posc_ref, dmc_ref, dmn_ref, ac_ref,
             lsec_ref, yc_ref, yn_ref, invf_ref, sinks_ref, ga_ref, gc_ref, cw_ref,
             dproj_ref, dkv_ref, dga_ref, dgc_ref, dsinks_ref, dcw_ref, dk_carry, dv_carry):
        n = pl.program_id(0)
        first = n == 0
        live = n < nb
        has_next = n < nb - 1
        cos_p, sgn_p = _rope_tables(posp_ref[...], invf_ref[...])
        cos_c, sgn_c = _rope_tables(posc_ref[...], invf_ref[...])

        @pl.when(first)
        def _():
            dk_carry[...] = jnp.zeros(dk_carry.shape, _F32)
            dv_carry[...] = jnp.zeros(dv_carry.shape, _F32)

        def write_kv(dk2, dv2, cos, sgn):
            lo = _lane((WINDOW, 128)) < HEAD_DIM
            for g in range(KV_WIDTH // 128):
                dk = jnp.where(lo, _fold_halves(dk2[2 * g]), _fold_halves(dk2[2 * g + 1]))
                dv = jnp.where(lo, _fold_halves(dv2[2 * g]), _fold_halves(dv2[2 * g + 1]))
                dkv_ref[:, 128 * g:128 * g + 128] = _rope(dk, cos, sgn, -1.0).astype(dkv_ref.dtype)
                dkv_ref[:, KV_WIDTH + 128 * g:KV_WIDTH + 128 * g + 128] = dv.astype(dkv_ref.dtype)

        @pl.when(jnp.logical_not(live))
        def _():
            write_kv([dk_carry[h] for h in range(N_KV_HEADS)], [dv_carry[h] for h in range(N_KV_HEADS)], cos_c, sgn_c)

        @pl.when(live)
        def _():
            block_step(pp_ref, pc_ref, pn_ref, dmc_ref, dmn_ref, ac_ref, lsec_ref, yc_ref, yn_ref, sinks_ref, ga_ref,
                       gc_ref, cw_ref, dproj_ref, dga_ref, dgc_ref, dsinks_ref, dcw_ref, dk_carry, dv_carry,
                       first, has_next, cos_p, sgn_p, cos_c, sgn_c, write_kv)

    def block_step(pp_ref, pc_ref, pn_ref, dmc_ref, dmn_ref, ac_ref, lsec_ref, yc_ref, yn_ref, sinks_ref, ga_ref,
                   gc_ref, cw_ref, dproj_ref, dga_ref, dgc_ref, dsinks_ref, dcw_ref, dk_carry, dv_carry,
                   first, has_next, cos_p, sgn_p, cos_c, sgn_c, write_kv):
        da_c, dga = _rms_bwd(dmc_ref[:, 0:aw], ac_ref[...], ga_ref[...])
        _accumulate(dga_ref, dga, first)
        kk = jnp.concatenate(
            [jnp.concatenate([_rope(ref[:, dm.o_k + 128 * g:dm.o_k + 128 * g + 128], c, sg, 1.0)
                              for g in range(KV_WIDTH // 128)], axis=1)
             for ref, c, sg in ((pp_ref, cos_p, sgn_p), (pc_ref, cos_c, sgn_c))], axis=0)
        vv = jnp.concatenate([pp_ref[:, dm.o_v:dm.o_v + KV_WIDTH], pc_ref[:, dm.o_v:dm.o_v + KV_WIDTH]], axis=0)
        group, pairs = dm.group, dm.group // 2
        valid_c = _band_mask(group, 2 * WINDOW, first)
        dk_prev, dv_prev = [], []
        dsinks = jnp.zeros((1, nq), _F32)
        head_lane = _lane((1, nq))

        def stacked(q_ref, cos, sgn, da, o_ref, lse_ref_, h):
            cols = [slice(128 * j, 128 * j + 128) for j in range(pairs * h, pairs * (h + 1))]
            q4 = _stack_heads([_rope(q_ref[:, c], cos, sgn, 1.0) for c in cols])
            do4 = _stack_heads([da[:, c] for c in cols])
            lo = _lane((WINDOW, 128)) < HEAD_DIM
            deltas = []
            for c in cols:
                prod = o_ref[:, c] * da[:, c]
                deltas += [jnp.sum(jnp.where(lo, prod, 0.0), axis=1, keepdims=True),
                           jnp.sum(jnp.where(lo, 0.0, prod), axis=1, keepdims=True)]
            lse4 = jnp.concatenate([lse_ref_[:, group * h + r:group * h + r + 1] for r in range(group)], axis=0)
            return q4, do4, lse4, jnp.concatenate(deltas, axis=0)

        def scores_bwd(q4, do4, lse4, delta4, keys, vals, valid):
            sc = _dot(q4, keys, "nt") * ATTN_SCALE
            p = jnp.exp(jnp.where(valid, sc - lse4, MASKED))
            return p.astype(_CDT), (p * (_dot(do4, vals, "nt") - delta4) * ATTN_SCALE).astype(_CDT)

        for h in range(N_KV_HEADS):
            k2, v2 = _dup_head(kk, h).astype(_CDT), _dup_head(vv, h).astype(_CDT)
            q4, do4, lse4, delta4 = stacked(pc_ref, cos_c, sgn_c, da_c, ac_ref, lsec_ref, h)
            p, ds = scores_bwd(q4, do4, lse4, delta4, k2, v2, valid_c)
            for i, dq in enumerate(_unstack_heads(_dot(ds, k2, "nn"), pairs)):
                j = pairs * h + i
                dproj_ref[:, 128 * j:128 * j + 128] = _rope(dq, cos_c, sgn_c, -1.0).astype(dproj_ref.dtype)
            dk = _dot(ds, q4, "tn")
            dv = _dot(p, do4, "tn")
            dk_prev.append(dk_carry[h] + dk[0:WINDOW])
            dv_prev.append(dv_carry[h] + dv[0:WINDOW])
            dk_carry[h] = dk[WINDOW:2 * WINDOW]
            dv_carry[h] = dv[WINDOW:2 * WINDOW]
            sink4 = _per_head([sinks_ref[0, group * h + r] for r in range(group)])
            loss_sink = jnp.exp(sink4 - lse4) * delta4
            for r in range(group):
                dsinks = dsinks + jnp.where(head_lane == group * h + r,
                                            -jnp.sum(loss_sink[WINDOW * r:WINDOW * (r + 1)]), 0.0)
        _accumulate(dsinks_ref, dsinks, first)
        write_kv(dk_prev, dv_prev, cos_p, sgn_p)

        bg = pc_ref[:, dm.o_bg:dm.o_bg + cw]
        yc = yc_ref[...]
        dconv, dgc = _rms_bwd(dmc_ref[:, aw:d], bg * yc, gc_ref[...])
        _accumulate(dgc_ref, dgc, first)
        dproj_ref[:, dm.o_bg:dm.o_bg + cw] = (dconv * yc).astype(dproj_ref.dtype)
        dy = dconv * bg
        bg_n = pn_ref[:, dm.o_bg:dm.o_bg + cw]
        dconv_n, _ = _rms_bwd(dmn_ref[:, aw:d], bg_n * yn_ref[...], gc_ref[...])
        halo = jnp.where(has_next, dconv_n * bg_n, 0.0)
        dy1 = _shift_up(dy, halo, 1)
        dy2 = _shift_up(dy, halo, 2)
        dz = cw_ref[2:3, :] * dy + cw_ref[1:2, :] * dy1 + cw_ref[0:1, :] * dy2
        cg = pc_ref[:, dm.o_cg:dm.o_cg + cw]
        u = pc_ref[:, dm.o_u:dm.o_u + cw]
        dproj_ref[:, dm.o_cg:dm.o_cg + cw] = (dz * u).astype(dproj_ref.dtype)
        dproj_ref[:, dm.o_u:dm.o_u + cw] = (dz * cg).astype(dproj_ref.dtype)
        z = cg * u
        dcw = jnp.concatenate(
            [jnp.sum(z * t, axis=0, keepdims=True) for t in (dy2, dy1, dy)]
            + [jnp.zeros((V7X_SUBLANES - 3, cw), _F32)], axis=0)
        _accumulate(dcw_ref, dcw, first)

    at = lambda n: jnp.minimum(n, nb - 1)
    prev = lambda n: (jnp.maximum(at(n) - 1, 0), 0)
    cur = lambda n: (at(n), 0)
    done = lambda n: (jnp.maximum(n - 1, 0), 0)
    nxt8 = lambda n: (jnp.minimum((at(n) + 1) * (WINDOW // V7X_SUBLANES), s // V7X_SUBLANES - 1), 0)
    fixed = lambda n: (0, 0)
    blocks = WINDOW * (2 * inw * 4 + d * 4 + aw * 4 + cw * 4 + inw * 2 + 2 * KV_WIDTH * 2)
    carry = [pltpu.VMEM((N_KV_HEADS, WINDOW, 128), _F32), pltpu.VMEM((N_KV_HEADS, WINDOW, 128), _F32)]
    n_in, n_out = 16, 6
    ca = _CommArgs(list(comm), n_in, n_out)
    return pl.pallas_call(
        _carrying(body, n_in, n_out, nb + 1, ca, n_scratch=len(carry)), name="mixer_bwd", grid=(nb + 1,),
        in_specs=[pl.BlockSpec((WINDOW, inw), prev), pl.BlockSpec((WINDOW, inw), cur), pl.BlockSpec((V7X_SUBLANES, inw), nxt8),
                  pl.BlockSpec((WINDOW, 1), prev), pl.BlockSpec((WINDOW, 1), cur),
                  pl.BlockSpec((WINDOW, d), cur), pl.BlockSpec((V7X_SUBLANES, d), nxt8),
                  pl.BlockSpec((WINDOW, aw), cur), pl.BlockSpec((WINDOW, nq), cur),
                  pl.BlockSpec((WINDOW, cw), cur), pl.BlockSpec((V7X_SUBLANES, cw), nxt8),
                  pl.BlockSpec((1, V7X_LANES), fixed), pl.BlockSpec(memory_space=pltpu.SMEM),
                  pl.BlockSpec((1, aw), fixed), pl.BlockSpec((1, cw), fixed), pl.BlockSpec((V7X_SUBLANES, cw), fixed)]
        + [_ANY] * len(ca.operands),
        out_specs=[pl.BlockSpec((WINDOW, inw), cur), pl.BlockSpec((WINDOW, 2 * KV_WIDTH), done),
                   pl.BlockSpec((1, aw), fixed), pl.BlockSpec((1, cw), fixed),
                   pl.BlockSpec((1, nq), fixed), pl.BlockSpec((V7X_SUBLANES, cw), fixed)] + [_ANY] * len(ca.out_shape),
        out_shape=[jax.ShapeDtypeStruct((s, inw), _CDT), jax.ShapeDtypeStruct((s, 2 * KV_WIDTH), _CDT),
                   jax.ShapeDtypeStruct((1, aw), _F32), jax.ShapeDtypeStruct((1, cw), _F32),
                   jax.ShapeDtypeStruct((1, nq), _F32), jax.ShapeDtypeStruct((V7X_SUBLANES, cw), _F32)] + ca.out_shape,
        scratch_shapes=carry + ca.sems, input_output_aliases=ca.aliases,
        compiler_params=pltpu.CompilerParams(dimension_semantics=("arbitrary",), vmem_limit_bytes=_vmem_limit(blocks)),
    )(proj, proj, proj, pos, pos, dmixed, dmixed, attn, lse, y, y, invf, sinks, g_attn, g_conv, conv_w8, *ca.operands)


def _position():
    return lax.axis_index("x"), lax.axis_index("y"), lax.axis_index("c")


def _linear(px, py, pc):
    return 4 * px + 2 * py + pc


def _comm_kernel(name, comm):
    ca = _CommArgs(list(comm), 0, 0)
    n_cin, n_cout = len(ca.operands), len(ca.out_shape)

    def body(*refs):
        cin, cout, sems = refs[:n_cin], refs[n_cin:n_cin + n_cout], refs[n_cin + n_cout:]
        ca.start(cin, cout, sems)
        ca.middle(cin, cout, sems)
        ca.finish(cin, cout, sems)

    return pl.pallas_call(
        body, name=name, out_shape=ca.out_shape, in_specs=[_ANY] * n_cin, out_specs=[_ANY] * n_cout,
        scratch_shapes=ca.sems, input_output_aliases=ca.aliases,
    )(*ca.operands)


def _gather_op(units):
    n = len(units)
    inputs, outputs, aliases = [], [], {}
    for shard, _, _, _ in units:
        inputs.append(shard)
        outputs.append(jax.ShapeDtypeStruct((N_DEV * shard.shape[0], shard.shape[1]), shard.dtype))
    for u, (_, buf, _, _) in enumerate(units):
        if buf is not None:
            aliases[len(inputs)] = u
            inputs.append(buf)

    def plan(ins, outs, sems, north):
        send_sems, recv_sems, local_sems = sems
        x, y, c = _position()
        me, sibling = (x, y, c), (x, y, 1 - c)
        xn, yn, dg = (1 - x, y), (x, 1 - y), (1 - x, 1 - y)
        via, to, k_via, k_other = (yn, xn, 2, 1) if north else (xn, yn, 1, 2)

        def rows(u, px, py, pc):
            shard, _, r0, r1 = units[u]
            return outs[u].at[pl.ds(pl.multiple_of(_linear(px, py, pc) * shard.shape[0] + r0, 16), r1 - r0), :]

        def own(u):
            _, _, r0, r1 = units[u]
            return ins[u].at[pl.ds(r0, r1 - r0), :]

        def copy(u, k, block, to_, src=None):
            return pltpu.make_async_remote_copy(
                src_ref=rows(u, *block) if src is None else src, dst_ref=rows(u, *block),
                send_sem=send_sems.at[u, k], recv_sem=recv_sems.at[u, k], device_id=to_, device_id_type=_MESH)

        us = range(n)
        return dict(
            mine=[pltpu.make_async_copy(own(u), rows(u, *me), local_sems.at[u]) for u in us],
            first=[cp for u in us for cp in (copy(u, 0, me, sibling, src=own(u)), copy(u, 1, me, (*xn, c), src=own(u)),
                                             copy(u, 2, me, (*yn, c), src=own(u)))],
            relay=[copy(u, 3, (*via, c), (*to, c)) for u in us],
            arrived={1: [copy(u, 1, (*xn, c), me) for u in us], 2: [copy(u, 2, (*yn, c), me) for u in us],
                     3: [copy(u, 3, (*dg, c), me) for u in us]},
            passed={1: [copy(u, 4, (*xn, c), sibling) for u in us], 2: [copy(u, 5, (*yn, c), sibling) for u in us],
                    3: [copy(u, 6, (*dg, c), sibling) for u in us]},
            rest=[cp for u in us for cp in (copy(u, 0, sibling, me), copy(u, 4, (*xn, 1 - c), me),
                                            copy(u, 5, (*yn, 1 - c), me), copy(u, 6, (*dg, 1 - c), me))],
            k_via=k_via, k_other=k_other)

    def land(p, k):
        for arrived, onward in zip(p["arrived"][k], p["passed"][k]):
            arrived.wait_recv()
            onward.start()

    def by_core(fn):
        c = lax.axis_index("c")
        for north in (True, False):
            pl.when(c == (1 if north else 0))(functools.partial(fn, north))

    def start(ins, outs, sems):
        p = plan(ins, outs, sems, True)
        for cp in p["mine"] + p["first"]:
            cp.start()

    def middle(ins, outs, sems):
        def go(north):
            p = plan(ins, outs, sems, north)
            land(p, p["k_via"])
            for cp in p["relay"]:
                cp.start()
        by_core(go)

    def finish(ins, outs, sems):
        def go(north):
            p = plan(ins, outs, sems, north)
            land(p, p["k_other"])
            land(p, 3)
            for cp in p["rest"]:
                cp.wait_recv()
            for cp in p["first"] + p["relay"] + [cp for k in (1, 2, 3) for cp in p["passed"][k]]:
                cp.wait_send()
            for cp in p["mine"]:
                cp.wait()
        by_core(go)

    sems = [pltpu.SemaphoreType.DMA((n, 7)), pltpu.SemaphoreType.DMA((n, 7)), pltpu.SemaphoreType.DMA((n,))]
    return _Comm(inputs, outputs, aliases, sems, start, finish, middle)


def _peers(x, y, c):
    out = []
    for k in range(1, N_DEV):
        fx, fy, fc = (k >> 2) & 1, (k >> 1) & 1, k & 1
        out.append((1 - x if fx else x, 1 - y if fy else y, 1 - c if fc else c))
    return out


def _exchange_op(partials):
    n = len(partials)
    outputs = [jax.ShapeDtypeStruct((4, p.shape[0] // N_DEV, p.shape[1]), p.dtype) for p in partials]

    def plan(ins, outs, sems):
        send_sems, recv_sems = sems
        x, y, c = _position()
        out = []
        for a in range(n):
            r = outs[a].shape[1]
            for ch in range(4):
                out.append(pltpu.make_async_remote_copy(
                    src_ref=ins[a].at[pl.ds(pl.multiple_of((2 * ch + 1 - c) * r, 16), r), :], dst_ref=outs[a].at[ch],
                    send_sem=send_sems.at[a, ch], recv_sem=recv_sems.at[a, ch], device_id=(x, y, 1 - c),
                    device_id_type=_MESH))
        return out

    def start(ins, outs, sems):
        for cp in plan(ins, outs, sems):
            cp.start()

    def finish(ins, outs, sems):
        copies = plan(ins, outs, sems)
        for cp in copies:
            cp.wait_recv()
        for cp in copies:
            cp.wait_send()

    sems = [pltpu.SemaphoreType.DMA((n, 4)), pltpu.SemaphoreType.DMA((n, 4))]
    return _Comm(list(partials), outputs, {}, sems, start, finish)


def _chip_send_op(units):
    n = len(units)
    inputs, outputs, aliases = [], [], {}
    for q, _, _, _ in units:
        inputs.append(q)
        outputs.append(jax.ShapeDtypeStruct(q.shape, q.dtype))
    for u, (_, buf, _, _) in enumerate(units):
        if buf is not None:
            aliases[len(inputs)] = u
            inputs.append(buf)

    def plan(ins, outs, sems):
        send_sems, recv_sems, local_sems = sems
        x, y, c = _position()
        my_chip = 2 * x + y
        chips = [(1 - x, y), (x, 1 - y), (1 - x, 1 - y)]
        mine, sends, arrivals = [], [], []
        for u, (_, _, r0, r1) in enumerate(units):
            span = pl.ds(r0, r1 - r0)
            mine.append(pltpu.make_async_copy(ins[u].at[my_chip, span, :], outs[u].at[my_chip, span, :], local_sems.at[u]))
            for k, (px, py) in enumerate(chips):
                sends.append(pltpu.make_async_remote_copy(
                    src_ref=ins[u].at[2 * px + py, span, :], dst_ref=outs[u].at[my_chip, span, :],
                    send_sem=send_sems.at[u, k], recv_sem=recv_sems.at[u, k], device_id=(px, py, c), device_id_type=_MESH))
                arrivals.append(pltpu.make_async_remote_copy(
                    src_ref=ins[u].at[my_chip, span, :], dst_ref=outs[u].at[2 * px + py, span, :],
                    send_sem=send_sems.at[u, k], recv_sem=recv_sems.at[u, k], device_id=(px, py, c), device_id_type=_MESH))
        return mine, sends, arrivals

    def start(ins, outs, sems):
        mine, sends, _ = plan(ins, outs, sems)
        for cp in mine + sends:
            cp.start()

    def finish(ins, outs, sems):
        mine, sends, arrivals = plan(ins, outs, sems)
        for cp in arrivals:
            cp.wait_recv()
        for cp in sends:
            cp.wait_send()
        for cp in mine:
            cp.wait()

    sems = [pltpu.SemaphoreType.DMA((n, 3)), pltpu.SemaphoreType.DMA((n, 3)), pltpu.SemaphoreType.DMA((n,))]
    return _Comm(inputs, outputs, aliases, sems, start, finish)


def _pair_sum(name, partial, received):
    _, rows, cols = received.shape
    tr = _pick(rows, (352, 288, 256, 128, 64, 32, 16))
    p4 = partial.reshape(4, 2, rows, cols)
    kind = jnp.reshape(lax.axis_index("c"), (1,)).astype(jnp.int32)

    def body(kind_ref, p_ref, r_ref, o_ref):
        o_ref[0] = (p_ref[0, 0].astype(_F32) + r_ref[0].astype(_F32)).astype(o_ref.dtype)

    return pl.pallas_call(
        body, name=name,
        grid_spec=pltpu.PrefetchScalarGridSpec(
            num_scalar_prefetch=1, grid=(4, rows // tr),
            in_specs=[pl.BlockSpec((1, 1, tr, cols), lambda ch, i, kind_ref: (ch, kind_ref[0], i, 0)),
                      pl.BlockSpec((1, tr, cols), lambda ch, i, kind_ref: (ch, i, 0))],
            out_specs=pl.BlockSpec((1, tr, cols), lambda ch, i, kind_ref: (ch, i, 0))),
        out_shape=jax.ShapeDtypeStruct(received.shape, received.dtype),
        compiler_params=pltpu.CompilerParams(dimension_semantics=("arbitrary", "arbitrary")),
    )(kind, p4, received)


def _all_reduce_small(name, v):
    rows = v.shape[0]

    def body(v_ref, out_ref, land_ref, send_sems, recv_sems):
        x, y, c = _position()
        me = _linear(x, y, c)
        peers = _peers(x, y, c)
        land_ref[me] = v_ref[...]
        sends = [pltpu.make_async_remote_copy(
            src_ref=v_ref, dst_ref=land_ref.at[me], send_sem=send_sems.at[k], recv_sem=recv_sems.at[k],
            device_id=peer, device_id_type=_MESH) for k, peer in enumerate(peers)]
        for cp in sends:
            cp.start()
        for k, peer in enumerate(peers):
            pltpu.make_async_remote_copy(
                src_ref=v_ref, dst_ref=land_ref.at[_linear(*peer)], send_sem=send_sems.at[k], recv_sem=recv_sems.at[k],
                device_id=peer, device_id_type=_MESH).wait_recv()
        for cp in sends:
            cp.wait_send()
        total = land_ref[0]
        for s in range(1, N_DEV):
            total = total + land_ref[s]
        out_ref[...] = total

    return pl.pallas_call(
        body, name=name, out_shape=jax.ShapeDtypeStruct(v.shape, _F32),
        in_specs=[pl.BlockSpec(memory_space=pltpu.VMEM)], out_specs=pl.BlockSpec(memory_space=pltpu.VMEM),
        scratch_shapes=[pltpu.VMEM((N_DEV, rows, V7X_LANES), _F32), pltpu.SemaphoreType.DMA((7,)), pltpu.SemaphoreType.DMA((7,))],
    )(v)


def _adamw(name, w, slots, m, v):
    rows, cols = w.shape
    n_slots = slots.shape[0]
    tr = _pick(rows, (176, 144, 128, 64, 32, 16, 8))

    def body(w_ref, s_ref, m_ref, v_ref, g_ref, d_ref, nm_ref, nv_ref):
        g = s_ref[0].astype(_F32)
        for k in range(1, n_slots):
            g = g + s_ref[k].astype(_F32)
        nm = ADAM_B1 * m_ref[...] + (1.0 - ADAM_B1) * g
        nv = ADAM_B2 * v_ref[...] + (1.0 - ADAM_B2) * (g * g)
        m_hat = nm / (1.0 - ADAM_B1 ** ADAM_STEP)
        v_hat = nv / (1.0 - ADAM_B2 ** ADAM_STEP)
        g_ref[...] = g
        d_ref[...] = -ADAM_LR * (m_hat / (jnp.sqrt(v_hat) + ADAM_EPS) + ADAM_WD * w_ref[...])
        nm_ref[...] = nm
        nv_ref[...] = nv

    spec = pl.BlockSpec((tr, cols), lambda i: (i, 0))
    blocks = 7 * tr * cols * 4 + _nbytes((n_slots, tr, cols), slots.dtype)
    return pl.pallas_call(
        body, name=name, grid=(rows // tr,),
        in_specs=[spec, pl.BlockSpec((n_slots, tr, cols), lambda i: (0, i, 0)), spec, spec], out_specs=[spec] * 4,
        out_shape=[jax.ShapeDtypeStruct((rows, cols), _F32)] * 4,
        compiler_params=pltpu.CompilerParams(dimension_semantics=("arbitrary",), vmem_limit_bytes=_vmem_limit(blocks)),
    )(w, slots, m, v)


def _pad_rows(a, rows):
    return jnp.pad(a, ((0, rows - a.shape[0]), (0, 0)))


def _pack(parts):
    rows, spans, at = [], [], 0
    for p in parts:
        p = p.reshape(-1)
        r = -(-p.shape[0] // V7X_LANES)
        rows.append(jnp.pad(p, (0, r * V7X_LANES - p.shape[0])).reshape(r, V7X_LANES))
        spans.append((at, r, p.shape[0]))
        at += r
    packed = jnp.concatenate(rows, axis=0)
    return _pad_rows(packed, -(-at // V7X_SUBLANES) * V7X_SUBLANES), spans


def _unpack(packed, spans, shapes):
    return [packed[at:at + r].reshape(-1)[:size].reshape(shape) for (at, r, size), shape in zip(spans, shapes)]


def kernel(x, positions, w_in, conv_w, sinks, g_attn, g_conv, w_out, ln1_g, ln1_b, w_gate, w_up, w_down, ln2_g, ln2_b, loss_target, m_w_in, m_conv_w, m_sinks, m_g_attn, m_g_conv, m_w_out, m_ln1_g, m_ln1_b, m_w_gate, m_w_up, m_w_down, m_ln2_g, m_ln2_b, v_w_in, v_conv_w, v_sinks, v_g_attn, v_g_conv, v_w_out, v_ln1_g, v_ln1_b, v_w_gate, v_w_up, v_w_down, v_ln2_g, v_ln2_b):
    _, s, d = x.shape
    d_ff = N_DEV * w_gate.shape[2]
    dm = _Dims(s, d, d_ff)
    aw, cw, nq, inw = dm.aw, dm.cw, dm.nq, dm.inw
    x2 = x[0]
    pos = positions[0].reshape(s, 1)
    inv_freq = ROPE_THETA ** (-jnp.arange(0, ROT_DIM, 2, dtype=_F32) / ROT_DIM)
    invf = jnp.tile(inv_freq, V7X_LANES // (ROT_DIM // 2)).reshape(1, V7X_LANES)

    conv_cols = conv_w.shape[2]
    sh_in, sh_out = w_in[0].T.astype(_CDT), w_out[0].astype(_CDT)
    sh_gate, sh_up, sh_down = w_gate[0].T.astype(_CDT), w_up[0].T.astype(_CDT), w_down[0].astype(_CDT)
    r_in, r_out, r_ff = sh_in.shape[0], sh_out.shape[0], sh_gate.shape[0]
    q_ff = r_ff // 4
    assert q_ff % 16 == 0
    def cast_body(x_ref, o_ref):
        o_ref[...] = x_ref[...].astype(_CDT)

    x_c, w_in_t, conv_all = _row_kernel("cast_x_gather_w_in", cast_body, [x2], [], [((s, d), _CDT)], [], comm=[
        _gather_op([(sh_in, None, 0, r_in), (_pad_rows(conv_w[0], 16), None, 0, 16)])])
    conv_full = conv_all.reshape(N_DEV, 16, conv_cols)[:, :3, :].transpose(1, 0, 2).reshape(3, cw)
    conv_w8 = _pad_rows(conv_full, V7X_SUBLANES)

    tm = _pick(s, (1024, 512, 256, 128))
    tn_in = _pick(inw, (512, 256, 128))
    tn_ff = _pick(d_ff, (512, 256, 128))
    tr = _pick(s, (512, 256, 128))

    proj, w_out_f, w_gate_t = _matmul(
        "proj", [[(x_c, w_in_t, "nt")]], s, inw, d, tm, tn_in, d, [],
        [((s, inw), _F32, (tm, tn_in), _tile_ij)], _store_epilogue,
        comm=[_gather_op([(sh_out, None, 0, r_out), (sh_gate, None, 0, 2 * q_ff)])])
    mixed, attn, lse, y_conv, w_gate_t, w_up_t = _mixer_fwd(
        dm, proj, pos, invf, sinks, g_attn, g_conv, conv_w8,
        comm=[_gather_op([(sh_gate, w_gate_t, 2 * q_ff, r_ff), (sh_up, None, 0, 2 * q_ff)])])

    def residual_epilogue(accs, ex, out, first):
        out[0][...] = DEEPNORM_ALPHA * ex[0][...] + accs[0]

    tn_d = _pick(d, (512,))
    r1, w_up_t = _matmul(
        "out_proj", [[(mixed, w_out_f, "nn")]], s, d, d, tm, tn_d, d, [(x2, (tm, tn_d), _tile_ij)],
        [((s, d), _F32, (tm, tn_d), _tile_ij)], residual_epilogue,
        comm=[_gather_op([(sh_up, w_up_t, 2 * q_ff, 3 * q_ff)])])
    h1, h1_c, xhat1, rstd1, w_up_t = _ln1_fwd_rows(
        r1, ln1_g, ln1_b, comm=[_gather_op([(sh_up, w_up_t, 3 * q_ff, r_ff)])])

    def swiglu_epilogue(accs, ex, out, first):
        gate_v, up_v = accs
        out[0][...] = gate_v
        out[1][...] = up_v
        out[2][...] = (gate_v * jax.nn.sigmoid(gate_v) * up_v).astype(_CDT)

    gate, up, act, w_down_f = _matmul(
        "gate_up", [[(h1_c, w_gate_t, "nt")], [(h1_c, w_up_t, "nt")]], s, d_ff, d, tm, tn_ff, d, [],
        [((s, d_ff), _F32, (tm, tn_ff), _tile_ij), ((s, d_ff), _F32, (tm, tn_ff), _tile_ij),
         ((s, d_ff), _CDT, (tm, tn_ff), _tile_ij)], swiglu_epilogue,
        comm=[_gather_op([(sh_down, None, 0, r_ff)])])

    (r2,) = _matmul("down", [[(act, w_down_f, "nn")]], s, d, d_ff, tr, tn_d, d_ff, [(h1, (tr, tn_d), _tile_ij)],
                    [((s, d), _F32, (tr, tn_d), _tile_ij)], residual_epilogue)
    dr2, dr2_c, loss_acc, d_ln2_g, d_ln2_b = _ln2_loss_bwd(r2, loss_target[0], ln2_g, ln2_b)

    def swiglu_bwd_epilogue(accs, ex, out, first):
        gate_v, up_v = ex[0][...], ex[1][...]
        sig = jax.nn.sigmoid(gate_v)
        out[0][...] = (accs[0] * up_v * (sig * (1.0 + gate_v * (1.0 - sig)))).astype(_CDT)
        out[1][...] = (accs[0] * (gate_v * sig)).astype(_CDT)

    dgate, dup = _matmul(
        "dact", [[(dr2_c, w_down_f, "nt")]], s, d_ff, d, tm, tn_ff, d,
        [(gate, (tm, tn_ff), _tile_ij), (up, (tm, tn_ff), _tile_ij)],
        [((s, d_ff), _CDT, (tm, tn_ff), _tile_ij), ((s, d_ff), _CDT, (tm, tn_ff), _tile_ij)], swiglu_bwd_epilogue,
        n_split=2)
    def weight_grad(name, a, b, comm=()):
        rows = a.shape[1]
        tw, tn_w = _pick(rows, (512, 256, 128)), _pick(d, (1024, 512))
        return _matmul(name, [[(a, b, "tn")]], rows, d, s, tw, tn_w, s, [],
                       [((rows, d), _CDT, (tw, tn_w), _tile_ij)], _store_epilogue, comm=comm, j_outer=True)

    (dw_down,) = weight_grad("dw_down", act, dr2_c)
    dw_gate_t, x_down = weight_grad("dw_gate", dgate, h1_c, comm=[_exchange_op([dw_down])])
    q_down = _pair_sum("chip_sum_w_down", dw_down, x_down)
    dw_up_t, l_down, x_gate = weight_grad(
        "dw_up", dup, h1_c, comm=[_chip_send_op([(q_down, None, 0, 2 * q_ff)]), _exchange_op([dw_gate_t])])
    q_gate = _pair_sum("chip_sum_w_gate", dw_gate_t, x_gate)

    tn_h = _pick(d, (256,))
    dh1, l_down, l_gate, x_up = _matmul(
        "dh1", [[(dgate, w_gate_t, "nn"), (dup, w_up_t, "nn")]], s, d, d_ff, tr, tn_h, d_ff,
        [(dr2, (tr, tn_h), _tile_ij)], [((s, d), _F32, (tr, tn_h), _tile_ij)], residual_epilogue,
        comm=[_chip_send_op([(q_down, l_down, 2 * q_ff, r_ff), (q_gate, None, 0, r_ff)]), _exchange_op([dw_up_t])])
    q_up = _pair_sum("chip_sum_w_up", dw_up_t, x_up)
    dr1, dr1_c, d_ln1_g, d_ln1_b = _ln1_bwd_rows(dh1, xhat1, rstd1, ln1_g)
    (dmixed,) = _matmul("dmixed", [[(dr1_c, w_out_f, "nt")]], s, d, d, tm, tn_d, d, [],
                        [((s, d), _F32, (tm, tn_d), _tile_ij)], _store_epilogue)
    (dw_out,) = weight_grad("dw_out", mixed, dr1_c)
    dproj, dkv, d_g_attn, d_g_conv, d_sinks, d_conv8, l_up, x_out = _mixer_bwd(
        dm, proj, pos, invf, sinks, g_attn, g_conv, conv_w8, dmixed, attn, lse, y_conv,
        comm=[_chip_send_op([(q_up, None, 0, r_ff)]), _exchange_op([dw_out])])
    dproj = _patch_columns("dproj_kv", dproj, dkv, dm.o_k)
    q_out = _pair_sum("chip_sum_w_out", dw_out, x_out)
    dw_in_t, l_out = weight_grad("dw_in", dproj, x_c, comm=[_chip_send_op([(q_out, None, 0, r_out)])])
    (x_in,) = _comm_kernel("exchange_w_in", [_exchange_op([dw_in_t])])
    q_in = _pair_sum("chip_sum_w_in", dw_in_t, x_in)

    grad_x, l_in = _matmul("dx", [[(dproj, w_in_t, "nn")]], s, d, inw, tr, tn_d, inw,
                           [(dr1, (tr, tn_d), _tile_ij)], [((s, d), _F32, (tr, tn_d), _tile_ij)], residual_epilogue,
                           comm=[_chip_send_op([(q_in, None, 0, r_in)])])

    small_parts = [d_conv8[:3], d_sinks, d_g_attn, d_g_conv, d_ln1_g, d_ln1_b, d_ln2_g, d_ln2_b]
    packed, spans = _pack(small_parts)
    reduced = _unpack(_all_reduce_small("reduce_small", packed), spans, [p.shape for p in small_parts])
    g_conv_full, g_sinks, g_g_attn, g_g_conv, g_ln1_g, g_ln1_b, g_ln2_g, g_ln2_b = reduced
    me = _linear(*_position())
    g_conv_w = lax.dynamic_slice(g_conv_full, (0, me * conv_cols), (3, conv_cols))
    loss = lax.psum(loss_acc[0, 0], ("x", "y", "c"))

    big = {"w_in": (w_in[0].T, l_in, m_w_in[0].T, v_w_in[0].T), "w_out": (w_out[0], l_out, m_w_out[0], v_w_out[0]),
           "w_gate": (w_gate[0].T, l_gate, m_w_gate[0].T, v_w_gate[0].T),
           "w_up": (w_up[0].T, l_up, m_w_up[0].T, v_w_up[0].T), "w_down": (w_down[0], l_down, m_w_down[0], v_w_down[0])}
    res = {nm: tuple(_adamw(f"adamw_{nm}", w, slots, m, v)) for nm, (w, slots, m, v) in big.items()}
    for nm in ("w_in", "w_gate", "w_up"):
        res[nm] = tuple(a.T for a in res[nm])
    small_names = ["conv_w", "sinks", "g_attn", "g_conv", "ln1_g", "ln1_b", "ln2_g", "ln2_b"]
    small_w = [conv_w, sinks, g_attn, g_conv, ln1_g, ln1_b, ln2_g, ln2_b]
    small_g = [g_conv_w[None], g_sinks, g_g_attn, g_g_conv, g_ln1_g, g_ln1_b, g_ln2_g, g_ln2_b]
    small_m = [m_conv_w, m_sinks, m_g_attn, m_g_conv, m_ln1_g, m_ln1_b, m_ln2_g, m_ln2_b]
    small_v = [v_conv_w, v_sinks, v_g_attn, v_g_conv, v_ln1_g, v_ln1_b, v_ln2_g, v_ln2_b]
    pw, sp = _pack(small_w)
    pg, _ = _pack(small_g)
    pm, _ = _pack(small_m)
    pv, _ = _pack(small_v)
    shapes = [w.shape for w in small_w]
    _, sd, sm, sv = [_unpack(p, sp, shapes) for p in _adamw("adamw_small", pw, pg[None], pm, pv)]
    for i, nm in enumerate(small_names):
        res[nm] = (small_g[i].reshape(shapes[i]), sd[i], sm[i], sv[i])

    order = ["w_in", "conv_w", "sinks", "g_attn", "g_conv", "w_out", "ln1_g", "ln1_b", "w_gate", "w_up", "w_down", "ln2_g", "ln2_b"]

    def lead(a, nm):
        return a[None] if nm in big else a

    return (loss, grad_x[None],
            *[lead(res[nm][0], nm) for nm in order], *[lead(res[nm][1], nm) for nm in order],
            *[lead(res[nm][2], nm) for nm in order], *[lead(res[nm][3], nm) for nm in order])
```

```python
import functools

import jax
import jax.numpy as jnp
from jax import lax
from jax.experimental import pallas as pl
from jax.experimental.pallas import tpu as pltpu

_F32 = jnp.float32
_CDT = jnp.bfloat16

HEAD_DIM = 64
WINDOW = 128
N_KV_HEADS = 4
KV_WIDTH = N_KV_HEADS * HEAD_DIM
ROT_DIM = HEAD_DIM // 4
ROPE_THETA = 500000.0
ATTN_SCALE = HEAD_DIM ** -0.5
DEPTH = 1
DEEPNORM_ALPHA = (2 * DEPTH) ** 0.25
LN_EPS = 1e-5
RMS_EPS = 1e-6
ADAM_LR = 0.001
ADAM_B1 = 0.9
ADAM_B2 = 0.999
ADAM_EPS = 1e-08
ADAM_WD = 0.01
ADAM_STEP = 10
N_DEV = 8
MASKED = -1e30

V7X_VMEM_BYTES = 64 * 1024 * 1024
V7X_LANES = 128
V7X_SUBLANES = 8
_MESH = pl.DeviceIdType.MESH
_ANY = pl.BlockSpec(memory_space=pl.ANY)


def _vmem_limit(block_bytes, scratch_bytes=0):
    want = 2 * block_bytes + scratch_bytes + 16 * 1024 * 1024
    return int(min(max(want, 32 * 1024 * 1024), V7X_VMEM_BYTES - 8 * 1024 * 1024))


def _nbytes(shape, dtype):
    n = 1
    for s in shape:
        n *= s
    return n * jnp.dtype(dtype).itemsize


def _pick(n, candidates):
    for c in candidates:
        if n % c == 0:
            return c
    raise ValueError(f"no tile of {candidates} divides {n}")


_DOT_DIMS = {"nn": ((1,), (0,)), "nt": ((1,), (1,)), "tn": ((0,), (0,))}


def _dot(a, b, mode):
    return lax.dot_general(a.astype(_CDT), b.astype(_CDT), (_DOT_DIMS[mode], ((), ())),
                           preferred_element_type=_F32)


def _accumulate(ref, val, first):
    @pl.when(first)
    def _():
        ref[...] = val

    @pl.when(jnp.logical_not(first))
    def _():
        ref[...] += val


class _Comm:
    def __init__(self, inputs, outputs, aliases, sems, start, finish, middle=None):
        self.inputs, self.outputs, self.aliases, self.sems = inputs, outputs, aliases, sems
        self.start, self.finish, self.middle = start, finish, middle


def _middle_step(n_steps):
    return (2 * n_steps) // 3


class _CommArgs:
    def __init__(self, comms, n_in_before, n_out_before):
        self.comms, self.operands, self.out_shape, self.aliases, self.sems, self.at = comms, [], [], {}, [], []
        for cm in comms:
            self.at.append((len(self.operands), len(self.out_shape), len(self.sems)))
            for i_in, i_out in cm.aliases.items():
                self.aliases[n_in_before + len(self.operands) + i_in] = n_out_before + len(self.out_shape) + i_out
            self.operands += cm.inputs
            self.out_shape += cm.outputs
            self.sems += cm.sems

    def _each(self, in_refs, out_refs, sem_refs):
        for cm, (i0, o0, s0) in zip(self.comms, self.at):
            yield cm, (in_refs[i0:i0 + len(cm.inputs)], out_refs[o0:o0 + len(cm.outputs)], sem_refs[s0:s0 + len(cm.sems)])

    def start(self, in_refs, out_refs, sem_refs):
        for cm, refs in self._each(in_refs, out_refs, sem_refs):
            cm.start(*refs)

    def finish(self, in_refs, out_refs, sem_refs):
        for cm, refs in self._each(in_refs, out_refs, sem_refs):
            cm.finish(*refs)

    @property
    def has_middle(self):
        return any(cm.middle is not None for cm in self.comms)

    def middle(self, in_refs, out_refs, sem_refs):
        for cm, refs in self._each(in_refs, out_refs, sem_refs):
            if cm.middle is not None:
                cm.middle(*refs)


def _matmul(name, groups, m, n, k, tm, tn, tk, extras, outs, epilogue, comm=(), j_outer=False, n_split=1):
    assert m % tm == 0 and n % tn == 0 and k % tk == 0, (name, m, n, k, tm, tn, tk)
    nk = k // tk
    assert n_split == 1 or (nk == 1 and tn % (n_split * V7X_LANES) == 0), (name, n_split)
    terms = [t for g in groups for t in g]
    operands, in_specs, block_bytes = [], [], 0

    def spec(blk, imap):
        return pl.BlockSpec(blk, (lambda g0, g1, kk: imap(g1, g0, kk)) if j_outer else imap)

    for a, b, mode in terms:
        assert a.shape == ((k, m) if mode == "tn" else (m, k)), (name, a.shape, mode)
        assert b.shape == ((n, k) if mode == "nt" else (k, n)), (name, b.shape, mode)
        if mode == "tn":
            a_blk, a_map = (tk, tm), (lambda i, j, kk: (kk, i))
        else:
            a_blk, a_map = (tm, tk), (lambda i, j, kk: (i, kk))
        if mode == "nt":
            b_blk, b_map = (tn, tk), (lambda i, j, kk: (j, kk))
        else:
            b_blk, b_map = (tk, tn), (lambda i, j, kk: (kk, j))
        operands += [a, b]
        in_specs += [spec(a_blk, a_map), spec(b_blk, b_map)]
        block_bytes += _nbytes(a_blk, a.dtype) + _nbytes(b_blk, b.dtype)
    for arr, blk, imap in extras:
        operands.append(arr)
        in_specs.append(spec(blk, lambda i, j, kk, imap=imap: imap(i, j)))
        block_bytes += _nbytes(blk, arr.dtype)
    out_shape, out_specs = [], []
    for shape, dtype, blk, imap in outs:
        out_shape.append(jax.ShapeDtypeStruct(shape, dtype))
        out_specs.append(spec(blk, lambda i, j, kk, imap=imap: imap(i, j)))
        block_bytes += _nbytes(blk, dtype)
    n_terms, n_extra, n_out, n_groups = len(terms), len(extras), len(outs), len(groups)
    scratch = [pltpu.VMEM((tm, tn), _F32) for _ in range(n_groups)] if nk > 1 else []
    ca = _CommArgs(list(comm), len(operands), n_out)
    n_cin, n_cout, n_acc = len(ca.operands), len(ca.out_shape), len(scratch)
    tiles = (m // tm, n // tn)
    grid = (tiles[1], tiles[0], nk) if j_outer else (tiles[0], tiles[1], nk)

    def body(*refs):
        refs = list(refs)
        term_refs = [refs.pop(0) for _ in range(2 * n_terms)]
        extra_refs = [refs.pop(0) for _ in range(n_extra)]
        cin_refs = [refs.pop(0) for _ in range(n_cin)]
        out_refs = [refs.pop(0) for _ in range(n_out)]
        cout_refs = [refs.pop(0) for _ in range(n_cout)]
        acc_refs = [refs.pop(0) for _ in range(n_acc)]
        sem_refs = refs
        g0, g1, kk = pl.program_id(0), pl.program_id(1), pl.program_id(2)
        first = jnp.logical_and(g0 == 0, g1 == 0)
        if comm:
            @pl.when(jnp.logical_and(first, kk == 0))
            def _():
                ca.start(cin_refs, cout_refs, sem_refs)
        if ca.has_middle:
            step = (g0 * grid[1] + g1) * nk + kk

            @pl.when(step == _middle_step(grid[0] * grid[1] * nk))
            def _():
                ca.middle(cin_refs, cout_refs, sem_refs)
        def products(cols):
            partial, t = [], 0
            for g in groups:
                s = None
                for _, _, mode in g:
                    b_ref = term_refs[2 * t + 1]
                    b = b_ref[...] if cols is None else (b_ref[cols, :] if mode == "nt" else b_ref[:, cols])
                    d = _dot(term_refs[2 * t][...], b, mode)
                    s = d if s is None else s + d
                    t += 1
                partial.append(s)
            return partial

        if n_split > 1:
            width = tn // n_split
            chunk = lambda c: pl.ds(c * width, width)
            ahead = products(chunk(0))
            for c in range(n_split):
                done, cols = ahead, chunk(c)
                if c + 1 < n_split:
                    ahead = products(chunk(c + 1))
                view = lambda ref: ref.at[:, cols] if tuple(ref.shape) == (tm, tn) else ref
                epilogue(done, [view(r) for r in extra_refs], [view(r) for r in out_refs], first)
        elif nk == 1:
            epilogue(products(None), extra_refs, out_refs, first)
        else:
            partial = products(None)
            for acc, p in zip(acc_refs, partial):
                _accumulate(acc, p, kk == 0)

            @pl.when(kk == nk - 1)
            def _():
                epilogue([acc[...] for acc in acc_refs], extra_refs, out_refs, first)
        if comm:
            @pl.when(jnp.logical_and(jnp.logical_and(g0 == grid[0] - 1, g1 == grid[1] - 1), kk == nk - 1))
            def _():
                ca.finish(cin_refs, cout_refs, sem_refs)

    res = pl.pallas_call(
        body, name=name, grid=grid,
        in_specs=in_specs + [_ANY] * n_cin, out_specs=out_specs + [_ANY] * n_cout,
        out_shape=out_shape + ca.out_shape, scratch_shapes=scratch + ca.sems, input_output_aliases=ca.aliases,
        compiler_params=pltpu.CompilerParams(
            dimension_semantics=("arbitrary", "arbitrary", "arbitrary"),
            vmem_limit_bytes=_vmem_limit(block_bytes, n_groups * tm * tn * 4 if nk > 1 else 0)),
    )(*operands, *ca.operands)
    return list(res[:n_out]) + list(res[n_out:])


def _store_epilogue(accs, extra_refs, out_refs, first):
    for acc, ref in zip(accs, out_refs):
        ref[...] = acc.astype(ref.dtype)


def _tile_ij(i, j):
    return (i, j)


def _row_i(i, j):
    return (i, 0)


def _whole(i, j):
    return (0, 0)


def _mean(v):
    return jnp.mean(v, axis=-1, keepdims=True)


def _ln_fwd(r, g, b):
    xc = r - _mean(r)
    rstd = lax.rsqrt(_mean(xc * xc) + LN_EPS)
    xhat = xc * rstd
    return xhat * g + b, xhat, rstd


def _ln_bwd(dy, xhat, rstd, g):
    dxh = dy * g
    dr = rstd * (dxh - _mean(dxh) - xhat * _mean(dxh * xhat))
    return dr, jnp.sum(dy * xhat, axis=0, keepdims=True), jnp.sum(dy, axis=0, keepdims=True)


def _rms_fwd(a, g):
    rstd = lax.rsqrt(_mean(a * a) + RMS_EPS)
    return a * rstd * g


def _rms_bwd(dm, a, g):
    rstd = lax.rsqrt(_mean(a * a) + RMS_EPS)
    nhat = a * rstd
    dn = dm * g
    da = rstd * (dn - nhat * _mean(dn * nhat))
    return da, jnp.sum(dm * nhat, axis=0, keepdims=True)


def _lane(shape):
    return lax.broadcasted_iota(jnp.int32, shape, 1)


def _row(shape):
    return lax.broadcasted_iota(jnp.int32, shape, 0)


def _rope_tables(pos, invf):
    ang = pos.astype(_F32) * invf
    lane = _lane(ang.shape)
    in_rot = (lane % HEAD_DIM) < ROT_DIM
    first = (lane % ROT_DIM) < ROT_DIM // 2
    cos = jnp.where(in_rot, jnp.cos(ang), 1.0)
    sin = jnp.sin(ang)
    sgn = jnp.where(in_rot, jnp.where(first, -sin, sin), 0.0)
    return cos, sgn


def _rope(t, cos, sgn, sign):
    half = ROT_DIM // 2
    first = (_lane(t.shape) % ROT_DIM) < half
    partner = jnp.where(first, pltpu.roll(t, V7X_LANES - half, 1), pltpu.roll(t, half, 1))
    return t * cos + partner * (sgn * sign)


def _dup_head(t, h):
    g = t[:, 128 * (h // 2):128 * (h // 2) + 128]
    r = pltpu.roll(g, HEAD_DIM, 1)
    lo = _lane(g.shape) < HEAD_DIM
    return jnp.where(lo, g, r) if h % 2 == 0 else jnp.where(lo, r, g)


def _fold_halves(t):
    return t + pltpu.roll(t, HEAD_DIM, 1)


def _halves(t):
    lo = _lane(t.shape) < HEAD_DIM
    zero = jnp.zeros_like(t)
    return jnp.where(lo, t, zero), jnp.where(lo, zero, t)


def _band_mask(n_heads, n_keys, first_block):
    shape = (n_heads * WINDOW, n_keys)
    i = jnp.bitwise_and(_row(shape), WINDOW - 1)
    j = _lane(shape)
    valid = jnp.logical_and(j >= i + 1, j <= i + WINDOW)
    if first_block is not None:
        valid = jnp.logical_and(valid, jnp.logical_or(j >= WINDOW, jnp.logical_not(first_block)))
    return valid


def _stack_heads(pairs):
    return jnp.concatenate([half for t in pairs for half in _halves(t)], axis=0).astype(_CDT)


def _unstack_heads(t, n_pairs):
    lo = _lane((WINDOW, 128)) < HEAD_DIM
    return [jnp.where(lo, t[2 * WINDOW * i:2 * WINDOW * i + WINDOW], t[2 * WINDOW * i + WINDOW:2 * WINDOW * (i + 1)])
            for i in range(n_pairs)]


def _per_head(values):
    n_rows = len(values) * WINDOW
    block = jnp.right_shift(_row((n_rows, 1)), WINDOW.bit_length() - 1)
    out = jnp.zeros((n_rows, 1), _F32)
    for k, v in enumerate(values):
        out = jnp.where(block == k, v, out)
    return out


def _shift_down(z, halo, k):
    rows = z.shape[0]
    out = pltpu.roll(z, k, 0)
    r = _row(z.shape)
    for t in range(k):
        out = jnp.where(r == t, halo[V7X_SUBLANES - k + t:V7X_SUBLANES - k + t + 1, :], out)
    del rows
    return out


def _shift_up(z, halo, k):
    rows = z.shape[0]
    out = pltpu.roll(z, rows - k, 0)
    r = _row(z.shape)
    for t in range(k):
        out = jnp.where(r == rows - k + t, halo[t:t + 1, :], out)
    return out


class _Dims:
    def __init__(self, s, d, d_ff):
        self.s, self.d, self.d_ff = s, d, d_ff
        self.aw = d // 2
        self.cw = d - self.aw
        self.nq = self.aw // HEAD_DIM
        self.group = self.nq // N_KV_HEADS
        assert self.group % 2 == 0, "a 128-lane pair of query heads must share its kv head"
        self.inw = self.aw + 2 * KV_WIDTH + 3 * self.cw
        self.o_k = self.aw
        self.o_v = self.aw + KV_WIDTH
        self.o_cg = self.aw + 2 * KV_WIDTH
        self.o_bg = self.o_cg + self.cw
        self.o_u = self.o_bg + self.cw
        self.nb = s // WINDOW
        assert s % WINDOW == 0


def _carrying(body, n_in, n_out, n_steps, ca, n_scratch=0):
    n_cin, n_cout = len(ca.operands), len(ca.out_shape)

    def wrapped(*refs):
        refs = list(refs)
        in_refs = [refs.pop(0) for _ in range(n_in)]
        cin_refs = [refs.pop(0) for _ in range(n_cin)]
        out_refs = [refs.pop(0) for _ in range(n_out)]
        cout_refs = [refs.pop(0) for _ in range(n_cout)]
        scratch_refs = [refs.pop(0) for _ in range(n_scratch)]
        if ca.comms:
            @pl.when(pl.program_id(0) == 0)
            def _():
                ca.start(cin_refs, cout_refs, refs)
        if ca.has_middle:
            @pl.when(pl.program_id(0) == _middle_step(n_steps))
            def _():
                ca.middle(cin_refs, cout_refs, refs)
        body(*in_refs, *out_refs, *scratch_refs)
        if ca.comms:
            @pl.when(pl.program_id(0) == n_steps - 1)
            def _():
                ca.finish(cin_refs, cout_refs, refs)

    return wrapped


def _row_kernel(name, body, rows_in, vecs_in, rows_out, vecs_out, comm=()):
    s = rows_in[0].shape[0]
    tr = _pick(s, (256, 128))
    row = lambda a: pl.BlockSpec((tr, a[1] if isinstance(a, tuple) else a.shape[1]), lambda i: (i, 0))
    vec = lambda shape: pl.BlockSpec(tuple(shape), lambda i: (0, 0))
    n_in, n_out = len(rows_in) + len(vecs_in), len(rows_out) + len(vecs_out)
    ca = _CommArgs(list(comm), n_in, n_out)
    blocks = sum(_nbytes((tr, a.shape[1]), a.dtype) for a in rows_in) + sum(_nbytes((tr, sh[1]), dt) for sh, dt in rows_out)
    res = pl.pallas_call(
        _carrying(body, n_in, n_out, s // tr, ca), name=name, grid=(s // tr,),
        in_specs=[row(a) for a in rows_in] + [vec(v.shape) for v in vecs_in] + [_ANY] * len(ca.operands),
        out_specs=[row(sh) for sh, _ in rows_out] + [vec(sh) for sh, _ in vecs_out] + [_ANY] * len(ca.out_shape),
        out_shape=[jax.ShapeDtypeStruct(sh, dt) for sh, dt in list(rows_out) + list(vecs_out)] + ca.out_shape,
        scratch_shapes=ca.sems, input_output_aliases=ca.aliases,
        compiler_params=pltpu.CompilerParams(dimension_semantics=("arbitrary",), vmem_limit_bytes=_vmem_limit(blocks)),
    )(*rows_in, *vecs_in, *ca.operands)
    return list(res)


def _ln2_loss_bwd(r2, target, gain, bias, comm=()):
    s, d = r2.shape

    def body(r_ref, t_ref, g_ref, b_ref, dr_ref, drc_ref, loss_ref, dg_ref, db_ref):
        first = pl.program_id(0) == 0
        yv, xhat, rstd = _ln_fwd(r_ref[...], g_ref[...], b_ref[...])
        err = yv - t_ref[...]
        dr2, dg, db = _ln_bwd(err * (1.0 / d), xhat, rstd, g_ref[...])
        dr_ref[...] = dr2
        drc_ref[...] = dr2.astype(_CDT)
        _accumulate(loss_ref, jnp.zeros(loss_ref.shape, _F32) + 0.5 * jnp.sum(err * err) * (1.0 / d), first)
        _accumulate(dg_ref, dg, first)
        _accumulate(db_ref, db, first)

    return _row_kernel("ln2_loss_bwd", body, [r2, target], [gain, bias], [((s, d), _F32), ((s, d), _CDT)],
                       [((V7X_SUBLANES, V7X_LANES), _F32), ((1, d), _F32), ((1, d), _F32)], comm)


def _ln1_fwd_rows(r1, gain, bias, comm=()):
    s, d = r1.shape

    def body(r_ref, g_ref, b_ref, h_ref, hc_ref, xhat_ref, rstd_ref):
        h1, xhat, rstd = _ln_fwd(r_ref[...], g_ref[...], b_ref[...])
        h_ref[...] = h1
        hc_ref[...] = h1.astype(_CDT)
        xhat_ref[...] = xhat
        rstd_ref[...] = rstd

    return _row_kernel("ln1", body, [r1], [gain, bias],
                       [((s, d), _F32), ((s, d), _CDT), ((s, d), _F32), ((s, 1), _F32)], [], comm)


def _ln1_bwd_rows(dh1, xhat, rstd, gain, comm=()):
    s, d = dh1.shape

    def body(dh_ref, xhat_ref, rstd_ref, g_ref, dr_ref, drc_ref, dg_ref, db_ref):
        first = pl.program_id(0) == 0
        dr1, dg, db = _ln_bwd(dh_ref[...], xhat_ref[...], rstd_ref[...], g_ref[...])
        dr_ref[...] = dr1
        drc_ref[...] = dr1.astype(_CDT)
        _accumulate(dg_ref, dg, first)
        _accumulate(db_ref, db, first)

    return _row_kernel("ln1_bwd", body, [dh1, xhat, rstd], [gain], [((s, d), _F32), ((s, d), _CDT)],
                       [((1, d), _F32), ((1, d), _F32)], comm)


def _mixer_fwd(dm, proj, rope, sinks, g_attn, g_conv, conv_w8, comm=()):
    s, d, aw, cw, nq, inw, nb = dm.s, dm.d, dm.aw, dm.cw, dm.nq, dm.inw, dm.nb

    def body(pp_ref, pc_ref, ropep_ref, ropec_ref, sinks_ref, ga_ref, gc_ref, cw_ref,
             mixed_ref, attn_ref, lse_ref, y_ref):
        n = pl.program_id(0)
        cos_c, sgn_c = ropec_ref[:, 0:V7X_LANES], ropec_ref[:, V7X_LANES:2 * V7X_LANES]
        cos_p, sgn_p = ropep_ref[:, 0:V7X_LANES], ropep_ref[:, V7X_LANES:2 * V7X_LANES]
        kk = jnp.concatenate(
            [jnp.concatenate([_rope(ref[:, dm.o_k + 128 * g:dm.o_k + 128 * g + 128], c, sg, 1.0)
                              for g in range(KV_WIDTH // 128)], axis=1)
             for ref, c, sg in ((pp_ref, cos_p, sgn_p), (pc_ref, cos_c, sgn_c))], axis=0)
        vv = jnp.concatenate([pp_ref[:, dm.o_v:dm.o_v + KV_WIDTH], pc_ref[:, dm.o_v:dm.o_v + KV_WIDTH]], axis=0)
        group, pairs = dm.group, dm.group // 2
        valid = _band_mask(group, 2 * WINDOW, n == 0)
        for h in range(N_KV_HEADS):
            k2, v2 = _dup_head(kk, h).astype(_CDT), _dup_head(vv, h).astype(_CDT)
            q4 = _stack_heads([_rope(pc_ref[:, 128 * j:128 * j + 128], cos_c, sgn_c, 1.0)
                               for j in range(pairs * h, pairs * (h + 1))])
            sc = jnp.where(valid, _dot(q4, k2, "nt") * ATTN_SCALE, MASKED)
            sink = _per_head([sinks_ref[0, group * h + r] for r in range(group)])
            mx = jnp.maximum(jnp.max(sc, axis=1, keepdims=True), sink)
            p = jnp.exp(sc - mx)
            den = jnp.sum(p, axis=1, keepdims=True) + jnp.exp(sink - mx)
            out = _unstack_heads(_dot(p / den, v2, "nn"), pairs)
            lse = mx + jnp.log(den)
            for r in range(group):
                lse_ref[:, group * h + r:group * h + r + 1] = lse[WINDOW * r:WINDOW * (r + 1)]
            for i in range(pairs):
                j = pairs * h + i
                attn_ref[:, 128 * j:128 * j + 128] = out[i]
        mixed_ref[:, 0:aw] = _rms_fwd(attn_ref[...], ga_ref[...]).astype(mixed_ref.dtype)

        z = pc_ref[:, dm.o_cg:dm.o_cg + cw] * pc_ref[:, dm.o_u:dm.o_u + cw]
        top = WINDOW - V7X_SUBLANES
        halo = pp_ref[top:WINDOW, dm.o_cg:dm.o_cg + cw] * pp_ref[top:WINDOW, dm.o_u:dm.o_u + cw]
        halo = jnp.where(n == 0, jnp.zeros_like(halo), halo)
        y = cw_ref[0:1, :] * _shift_down(z, halo, 2) + cw_ref[1:2, :] * _shift_down(z, halo, 1) + cw_ref[2:3, :] * z
        y_ref[...] = y
        conv = pc_ref[:, dm.o_bg:dm.o_bg + cw] * y
        mixed_ref[:, aw:d] = _rms_fwd(conv, gc_ref[...]).astype(mixed_ref.dtype)

    prev = lambda n: (jnp.maximum(n - 1, 0), 0)
    cur = lambda n: (n, 0)
    fixed = lambda n: (0, 0)
    blocks = 2 * WINDOW * inw * 4 + WINDOW * (d * 2 + aw * 4 + cw * 4 + nq * 4)
    ca = _CommArgs(list(comm), 8, 4)
    return pl.pallas_call(
        _carrying(body, 8, 4, nb, ca), name="mixer_fwd", grid=(nb,),
        in_specs=[pl.BlockSpec((WINDOW, inw), prev), pl.BlockSpec((WINDOW, inw), cur),
                  pl.BlockSpec((WINDOW, 2 * V7X_LANES), prev), pl.BlockSpec((WINDOW, 2 * V7X_LANES), cur),
                  pl.BlockSpec(memory_space=pltpu.SMEM),
                  pl.BlockSpec((1, aw), fixed), pl.BlockSpec((1, cw), fixed), pl.BlockSpec((V7X_SUBLANES, cw), fixed)]
        + [_ANY] * len(ca.operands),
        out_specs=[pl.BlockSpec((WINDOW, d), cur), pl.BlockSpec((WINDOW, aw), cur),
                   pl.BlockSpec((WINDOW, nq), cur), pl.BlockSpec((WINDOW, cw), cur)] + [_ANY] * len(ca.out_shape),
        out_shape=[jax.ShapeDtypeStruct((s, d), _CDT), jax.ShapeDtypeStruct((s, aw), _F32),
                   jax.ShapeDtypeStruct((s, nq), _F32), jax.ShapeDtypeStruct((s, cw), _F32)] + ca.out_shape,
        scratch_shapes=ca.sems, input_output_aliases=ca.aliases,
        compiler_params=pltpu.CompilerParams(dimension_semantics=("arbitrary",), vmem_limit_bytes=_vmem_limit(blocks)),
    )(proj, proj, rope, rope, sinks, g_attn, g_conv, conv_w8, *ca.operands)


def _patch_columns(name, a, part, offset):
    s, pw = part.shape
    assert offset % pw == 0 and pw % V7X_LANES == 0
    tr = _pick(s, (512, 256, 128))

    def body(a_ref, p_ref, o_ref):
        del a_ref
        o_ref[...] = p_ref[...]

    return pl.pallas_call(
        body, name=name, grid=(s // tr,),
        in_specs=[_ANY, pl.BlockSpec((tr, pw), lambda i: (i, 0))],
        out_specs=pl.BlockSpec((tr, pw), lambda i: (i, offset // pw)),
        out_shape=jax.ShapeDtypeStruct(a.shape, a.dtype), input_output_aliases={0: 0},
        compiler_params=pltpu.CompilerParams(dimension_semantics=("arbitrary",)),
    )(a, part)


def _mixer_bwd(dm, proj, rope, sinks, g_attn, g_conv, conv_w8, dmixed, attn, lse, y, comm=()):
    s, d, aw, cw, nq, inw, nb = dm.s, dm.d, dm.aw, dm.cw, dm.nq, dm.inw, dm.nb

    def body(pp_ref, pc_ref, pn_ref, ropep_ref, ropec_ref, dmc_ref, dmn_ref, ac_ref,
             lsec_ref, yc_ref, yn_ref, sinks_ref, ga_ref, gc_ref, cw_ref,
             dproj_ref, dkv_ref, dga_ref, dgc_ref, dsinks_ref, dcw_ref, dk_carry, dv_carry):
        n = pl.program_id(0)
        first = n == 0
        live = n < nb
        has_next = n < nb - 1
        cos_p, sgn_p = ropep_ref[:, 0:V7X_LANES], ropep_ref[:, V7X_LANES:2 * V7X_LANES]
        cos_c, sgn_c = ropec_ref[:, 0:V7X_LANES], ropec_ref[:, V7X_LANES:2 * V7X_LANES]

        @pl.when(first)
        def _():
            dk_carry[...] = jnp.zeros(dk_carry.shape, _F32)
            dv_carry[...] = jnp.zeros(dv_carry.shape, _F32)

        def write_kv(dk2, dv2, cos, sgn):
            lo = _lane((WINDOW, 128)) < HEAD_DIM
            for g in range(KV_WIDTH // 128):
                dk = jnp.where(lo, _fold_halves(dk2[2 * g]), _fold_halves(dk2[2 * g + 1]))
                dv = jnp.where(lo, _fold_halves(dv2[2 * g]), _fold_halves(dv2[2 * g + 1]))
                dkv_ref[:, 128 * g:128 * g + 128] = _rope(dk, cos, sgn, -1.0).astype(dkv_ref.dtype)
                dkv_ref[:, KV_WIDTH + 128 * g:KV_WIDTH + 128 * g + 128] = dv.astype(dkv_ref.dtype)

        @pl.when(jnp.logical_not(live))
        def _():
            write_kv([dk_carry[h] for h in range(N_KV_HEADS)], [dv_carry[h] for h in range(N_KV_HEADS)], cos_c, sgn_c)

        @pl.when(live)
        def _():
            block_step(pp_ref, pc_ref, pn_ref, dmc_ref, dmn_ref, ac_ref, lsec_ref, yc_ref, yn_ref, sinks_ref, ga_ref,
                       gc_ref, cw_ref, dproj_ref, dga_ref, dgc_ref, dsinks_ref, dcw_ref, dk_carry, dv_carry,
                       first, has_next, cos_p, sgn_p, cos_c, sgn_c, write_kv)

    def block_step(pp_ref, pc_ref, pn_ref, dmc_ref, dmn_ref, ac_ref, lsec_ref, yc_ref, yn_ref, sinks_ref, ga_ref,
                   gc_ref, cw_ref, dproj_ref, dga_ref, dgc_ref, dsinks_ref, dcw_ref, dk_carry, dv_carry,
                   first, has_next, cos_p, sgn_p, cos_c, sgn_c, write_kv):
        da_c, dga = _rms_bwd(dmc_ref[:, 0:aw], ac_ref[...], ga_ref[...])
        _accumulate(dga_ref, dga, first)
        kk = jnp.concatenate(
            [jnp.concatenate([_rope(ref[:, dm.o_k + 128 * g:dm.o_k + 128 * g + 128], c, sg, 1.0)
                              for g in range(KV_WIDTH // 128)], axis=1)
             for ref, c, sg in ((pp_ref, cos_p, sgn_p), (pc_ref, cos_c, sgn_c))], axis=0)
        vv = jnp.concatenate([pp_ref[:, dm.o_v:dm.o_v + KV_WIDTH], pc_ref[:, dm.o_v:dm.o_v + KV_WIDTH]], axis=0)
        group, pairs = dm.group, dm.group // 2
        valid_c = _band_mask(group, 2 * WINDOW, first)
        dk_prev, dv_prev = [], []
        dsinks = jnp.zeros((1, nq), _F32)
        head_lane = _lane((1, nq))

        def stacked(q_ref, cos, sgn, da, o_ref, lse_ref_, h):
            cols = [slice(128 * j, 128 * j + 128) for j in range(pairs * h, pairs * (h + 1))]
            q4 = _stack_heads([_rope(q_ref[:, c], cos, sgn, 1.0) for c in cols])
            do4 = _stack_heads([da[:, c] for c in cols])
            lo = _lane((WINDOW, 128)) < HEAD_DIM
            deltas = []
            for c in cols:
                prod = o_ref[:, c] * da[:, c]
                deltas += [jnp.sum(jnp.where(lo, prod, 0.0), axis=1, keepdims=True),
                           jnp.sum(jnp.where(lo, 0.0, prod), axis=1, keepdims=True)]
            lse4 = jnp.concatenate([lse_ref_[:, group * h + r:group * h + r + 1] for r in range(group)], axis=0)
            return q4, do4, lse4, jnp.concatenate(deltas, axis=0)

        def scores_bwd(q4, do4, lse4, delta4, keys, vals, valid):
            sc = _dot(q4, keys, "nt") * ATTN_SCALE
            p = jnp.exp(jnp.where(valid, sc - lse4, MASKED))
            return p.astype(_CDT), (p * (_dot(do4, vals, "nt") - delta4) * ATTN_SCALE).astype(_CDT)

        for h in range(N_KV_HEADS):
            k2, v2 = _dup_head(kk, h).astype(_CDT), _dup_head(vv, h).astype(_CDT)
            q4, do4, lse4, delta4 = stacked(pc_ref, cos_c, sgn_c, da_c, ac_ref, lsec_ref, h)
            p, ds = scores_bwd(q4, do4, lse4, delta4, k2, v2, valid_c)
            for i, dq in enumerate(_unstack_heads(_dot(ds, k2, "nn"), pairs)):
                j = pairs * h + i
                dproj_ref[:, 128 * j:128 * j + 128] = _rope(dq, cos_c, sgn_c, -1.0).astype(dproj_ref.dtype)
            dk = _dot(ds, q4, "tn")
            dv = _dot(p, do4, "tn")
            dk_prev.append(dk_carry[h] + dk[0:WINDOW])
            dv_prev.append(dv_carry[h] + dv[0:WINDOW])
            dk_carry[h] = dk[WINDOW:2 * WINDOW]
            dv_carry[h] = dv[WINDOW:2 * WINDOW]
            sink4 = _per_head([sinks_ref[0, group * h + r] for r in range(group)])
            loss_sink = jnp.exp(sink4 - lse4) * delta4
            for r in range(group):
                dsinks = dsinks + jnp.where(head_lane == group * h + r,
                                            -jnp.sum(loss_sink[WINDOW * r:WINDOW * (r + 1)]), 0.0)
        _accumulate(dsinks_ref, dsinks, first)
        write_kv(dk_prev, dv_prev, cos_p, sgn_p)

        bg = pc_ref[:, dm.o_bg:dm.o_bg + cw]
        yc = yc_ref[...]
        dconv, dgc = _rms_bwd(dmc_ref[:, aw:d], bg * yc, gc_ref[...])
        _accumulate(dgc_ref, dgc, first)
        dproj_ref[:, dm.o_bg:dm.o_bg + cw] = (dconv * yc).astype(dproj_ref.dtype)
        dy = dconv * bg
        bg_n = pn_ref[:, dm.o_bg:dm.o_bg + cw]
        dconv_n, _ = _rms_bwd(dmn_ref[:, aw:d], bg_n * yn_ref[...], gc_ref[...])
        halo = jnp.where(has_next, dconv_n * bg_n, 0.0)
        dy1 = _shift_up(dy, halo, 1)
        dy2 = _shift_up(dy, halo, 2)
        dz = cw_ref[2:3, :] * dy + cw_ref[1:2, :] * dy1 + cw_ref[0:1, :] * dy2
        cg = pc_ref[:, dm.o_cg:dm.o_cg + cw]
        u = pc_ref[:, dm.o_u:dm.o_u + cw]
        dproj_ref[:, dm.o_cg:dm.o_cg + cw] = (dz * u).astype(dproj_ref.dtype)
        dproj_ref[:, dm.o_u:dm.o_u + cw] = (dz * cg).astype(dproj_ref.dtype)
        z = cg * u
        dcw = jnp.concatenate(
            [jnp.sum(z * t, axis=0, keepdims=True) for t in (dy2, dy1, dy)]
            + [jnp.zeros((V7X_SUBLANES - 3, cw), _F32)], axis=0)
        _accumulate(dcw_ref, dcw, first)

    at = lambda n: jnp.minimum(n, nb - 1)
    prev = lambda n: (jnp.maximum(at(n) - 1, 0), 0)
    cur = lambda n: (at(n), 0)
    done = lambda n: (jnp.maximum(n - 1, 0), 0)
    nxt8 = lambda n: (jnp.minimum((at(n) + 1) * (WINDOW // V7X_SUBLANES), s // V7X_SUBLANES - 1), 0)
    fixed = lambda n: (0, 0)
    blocks = WINDOW * (2 * inw * 4 + d * 4 + aw * 4 + cw * 4 + inw * 2 + 2 * KV_WIDTH * 2)
    carry = [pltpu.VMEM((N_KV_HEADS, WINDOW, 128), _F32), pltpu.VMEM((N_KV_HEADS, WINDOW, 128), _F32)]
    n_in, n_out = 15, 6
    ca = _CommArgs(list(comm), n_in, n_out)
    return pl.pallas_call(
        _carrying(body, n_in, n_out, nb + 1, ca, n_scratch=len(carry)), name="mixer_bwd", grid=(nb + 1,),
        in_specs=[pl.BlockSpec((WINDOW, inw), prev), pl.BlockSpec((WINDOW, inw), cur), pl.BlockSpec((V7X_SUBLANES, inw), nxt8),
                  pl.BlockSpec((WINDOW, 2 * V7X_LANES), prev), pl.BlockSpec((WINDOW, 2 * V7X_LANES), cur),
                  pl.BlockSpec((WINDOW, d), cur), pl.BlockSpec((V7X_SUBLANES, d), nxt8),
                  pl.BlockSpec((WINDOW, aw), cur), pl.BlockSpec((WINDOW, nq), cur),
                  pl.BlockSpec((WINDOW, cw), cur), pl.BlockSpec((V7X_SUBLANES, cw), nxt8),
                  pl.BlockSpec(memory_space=pltpu.SMEM),
                  pl.BlockSpec((1, aw), fixed), pl.BlockSpec((1, cw), fixed), pl.BlockSpec((V7X_SUBLANES, cw), fixed)]
        + [_ANY] * len(ca.operands),
        out_specs=[pl.BlockSpec((WINDOW, inw), cur), pl.BlockSpec((WINDOW, 2 * KV_WIDTH), done),
                   pl.BlockSpec((1, aw), fixed), pl.BlockSpec((1, cw), fixed),
                   pl.BlockSpec((1, nq), fixed), pl.BlockSpec((V7X_SUBLANES, cw), fixed)] + [_ANY] * len(ca.out_shape),
        out_shape=[jax.ShapeDtypeStruct((s, inw), _CDT), jax.ShapeDtypeStruct((s, 2 * KV_WIDTH), _CDT),
                   jax.ShapeDtypeStruct((1, aw), _F32), jax.ShapeDtypeStruct((1, cw), _F32),
                   jax.ShapeDtypeStruct((1, nq), _F32), jax.ShapeDtypeStruct((V7X_SUBLANES, cw), _F32)] + ca.out_shape,
        scratch_shapes=carry + ca.sems, input_output_aliases=ca.aliases,
        compiler_params=pltpu.CompilerParams(dimension_semantics=("arbitrary",), vmem_limit_bytes=_vmem_limit(blocks)),
    )(proj, proj, proj, rope, rope, dmixed, dmixed, attn, lse, y, y, sinks, g_attn, g_conv, conv_w8, *ca.operands)


def _position():
    return lax.axis_index("x"), lax.axis_index("y"), lax.axis_index("c")


def _linear(px, py, pc):
    return 4 * px + 2 * py + pc


def _comm_kernel(name, comm):
    ca = _CommArgs(list(comm), 0, 0)
    n_cin, n_cout = len(ca.operands), len(ca.out_shape)

    def body(*refs):
        cin, cout, sems = refs[:n_cin], refs[n_cin:n_cin + n_cout], refs[n_cin + n_cout:]
        ca.start(cin, cout, sems)
        ca.middle(cin, cout, sems)
        ca.finish(cin, cout, sems)

    return pl.pallas_call(
        body, name=name, out_shape=ca.out_shape, in_specs=[_ANY] * n_cin, out_specs=[_ANY] * n_cout,
        scratch_shapes=ca.sems, input_output_aliases=ca.aliases,
    )(*ca.operands)


def _gather_op(units):
    n = len(units)
    inputs, outputs, aliases = [], [], {}
    for shard, _, _, _ in units:
        inputs.append(shard)
        outputs.append(jax.ShapeDtypeStruct((N_DEV * shard.shape[0], shard.shape[1]), shard.dtype))
    for u, (_, buf, _, _) in enumerate(units):
        if buf is not None:
            aliases[len(inputs)] = u
            inputs.append(buf)

    def plan(ins, outs, sems, north):
        send_sems, recv_sems, local_sems = sems
        x, y, c = _position()
        me, sibling = (x, y, c), (x, y, 1 - c)
        xn, yn, dg = (1 - x, y), (x, 1 - y), (1 - x, 1 - y)
        via, to, k_via, k_other = (yn, xn, 2, 1) if north else (xn, yn, 1, 2)

        def rows(u, px, py, pc):
            shard, _, r0, r1 = units[u]
            return outs[u].at[pl.ds(pl.multiple_of(_linear(px, py, pc) * shard.shape[0] + r0, 16), r1 - r0), :]

        def own(u):
            _, _, r0, r1 = units[u]
            return ins[u].at[pl.ds(r0, r1 - r0), :]

        def copy(u, k, block, to_, src=None):
            return pltpu.make_async_remote_copy(
                src_ref=rows(u, *block) if src is None else src, dst_ref=rows(u, *block),
                send_sem=send_sems.at[u, k], recv_sem=recv_sems.at[u, k], device_id=to_, device_id_type=_MESH)

        us = range(n)
        return dict(
            mine=[pltpu.make_async_copy(own(u), rows(u, *me), local_sems.at[u]) for u in us],
            first=[cp for u in us for cp in (copy(u, 0, me, sibling, src=own(u)), copy(u, 1, me, (*xn, c), src=own(u)),
                                             copy(u, 2, me, (*yn, c), src=own(u)))],
            relay=[copy(u, 3, (*via, c), (*to, c)) for u in us],
            arrived={1: [copy(u, 1, (*xn, c), me) for u in us], 2: [copy(u, 2, (*yn, c), me) for u in us],
                     3: [copy(u, 3, (*dg, c), me) for u in us]},
            passed={1: [copy(u, 4, (*xn, c), sibling) for u in us], 2: [copy(u, 5, (*yn, c), sibling) for u in us],
                    3: [copy(u, 6, (*dg, c), sibling) for u in us]},
            rest=[cp for u in us for cp in (copy(u, 0, sibling, me), copy(u, 4, (*xn, 1 - c), me),
                                            copy(u, 5, (*yn, 1 - c), me), copy(u, 6, (*dg, 1 - c), me))],
            k_via=k_via, k_other=k_other)

    def land(p, k):
        for arrived, onward in zip(p["arrived"][k], p["passed"][k]):
            arrived.wait_recv()
            onward.start()

    def by_core(fn):
        c = lax.axis_index("c")
        for north in (True, False):
            pl.when(c == (1 if north else 0))(functools.partial(fn, north))

    def start(ins, outs, sems):
        p = plan(ins, outs, sems, True)
        for cp in p["mine"] + p["first"]:
            cp.start()

    def middle(ins, outs, sems):
        def go(north):
            p = plan(ins, outs, sems, north)
            land(p, p["k_via"])
            for cp in p["relay"]:
                cp.start()
        by_core(go)

    def finish(ins, outs, sems):
        def go(north):
            p = plan(ins, outs, sems, north)
            land(p, p["k_other"])
            land(p, 3)
            for cp in p["rest"]:
                cp.wait_recv()
            for cp in p["first"] + p["relay"] + [cp for k in (1, 2, 3) for cp in p["passed"][k]]:
                cp.wait_send()
            for cp in p["mine"]:
                cp.wait()
        by_core(go)

    sems = [pltpu.SemaphoreType.DMA((n, 7)), pltpu.SemaphoreType.DMA((n, 7)), pltpu.SemaphoreType.DMA((n,))]
    return _Comm(inputs, outputs, aliases, sems, start, finish, middle)


def _peers(x, y, c):
    out = []
    for k in range(1, N_DEV):
        fx, fy, fc = (k >> 2) & 1, (k >> 1) & 1, k & 1
        out.append((1 - x if fx else x, 1 - y if fy else y, 1 - c if fc else c))
    return out


def _exchange_op(partials):
    n = len(partials)
    outputs = [jax.ShapeDtypeStruct((4, p.shape[0] // N_DEV, p.shape[1]), p.dtype) for p in partials]

    def plan(ins, outs, sems):
        send_sems, recv_sems = sems
        x, y, c = _position()
        out = []
        for a in range(n):
            r = outs[a].shape[1]
            for ch in range(4):
                out.append(pltpu.make_async_remote_copy(
                    src_ref=ins[a].at[pl.ds(pl.multiple_of((2 * ch + 1 - c) * r, 16), r), :], dst_ref=outs[a].at[ch],
                    send_sem=send_sems.at[a, ch], recv_sem=recv_sems.at[a, ch], device_id=(x, y, 1 - c),
                    device_id_type=_MESH))
        return out

    def start(ins, outs, sems):
        for cp in plan(ins, outs, sems):
            cp.start()

    def finish(ins, outs, sems):
        copies = plan(ins, outs, sems)
        for cp in copies:
            cp.wait_recv()
        for cp in copies:
            cp.wait_send()

    sems = [pltpu.SemaphoreType.DMA((n, 4)), pltpu.SemaphoreType.DMA((n, 4))]
    return _Comm(list(partials), outputs, {}, sems, start, finish)


def _chip_send_op(units):
    n = len(units)
    inputs, outputs, aliases = [], [], {}
    for q, _, _, _ in units:
        inputs.append(q)
        outputs.append(jax.ShapeDtypeStruct(q.shape, q.dtype))
    for u, (_, buf, _, _) in enumerate(units):
        if buf is not None:
            aliases[len(inputs)] = u
            inputs.append(buf)

    def plan(ins, outs, sems):
        send_sems, recv_sems, local_sems = sems
        x, y, c = _position()
        my_chip = 2 * x + y
        chips = [(1 - x, y), (x, 1 - y), (1 - x, 1 - y)]
        mine, sends, arrivals = [], [], []
        for u, (_, _, r0, r1) in enumerate(units):
            span = pl.ds(r0, r1 - r0)
            mine.append(pltpu.make_async_copy(ins[u].at[my_chip, span, :], outs[u].at[my_chip, span, :], local_sems.at[u]))
            for k, (px, py) in enumerate(chips):
                sends.append(pltpu.make_async_remote_copy(
                    src_ref=ins[u].at[2 * px + py, span, :], dst_ref=outs[u].at[my_chip, span, :],
                    send_sem=send_sems.at[u, k], recv_sem=recv_sems.at[u, k], device_id=(px, py, c), device_id_type=_MESH))
                arrivals.append(pltpu.make_async_remote_copy(
                    src_ref=ins[u].at[my_chip, span, :], dst_ref=outs[u].at[2 * px + py, span, :],
                    send_sem=send_sems.at[u, k], recv_sem=recv_sems.at[u, k], device_id=(px, py, c), device_id_type=_MESH))
        return mine, sends, arrivals

    def start(ins, outs, sems):
        mine, sends, _ = plan(ins, outs, sems)
        for cp in mine + sends:
            cp.start()

    def finish(ins, outs, sems):
        mine, sends, arrivals = plan(ins, outs, sems)
        for cp in arrivals:
            cp.wait_recv()
        for cp in sends:
            cp.wait_send()
        for cp in mine:
            cp.wait()

    sems = [pltpu.SemaphoreType.DMA((n, 3)), pltpu.SemaphoreType.DMA((n, 3)), pltpu.SemaphoreType.DMA((n,))]
    return _Comm(inputs, outputs, aliases, sems, start, finish)


def _pair_sum(name, partial, received):
    _, rows, cols = received.shape
    tr = _pick(rows, (352, 288, 256, 128, 64, 32, 16))
    p4 = partial.reshape(4, 2, rows, cols)
    kind = jnp.reshape(lax.axis_index("c"), (1,)).astype(jnp.int32)

    def body(kind_ref, p_ref, r_ref, o_ref):
        o_ref[0] = (p_ref[0, 0].astype(_F32) + r_ref[0].astype(_F32)).astype(o_ref.dtype)

    return pl.pallas_call(
        body, name=name,
        grid_spec=pltpu.PrefetchScalarGridSpec(
            num_scalar_prefetch=1, grid=(4, rows // tr),
            in_specs=[pl.BlockSpec((1, 1, tr, cols), lambda ch, i, kind_ref: (ch, kind_ref[0], i, 0)),
                      pl.BlockSpec((1, tr, cols), lambda ch, i, kind_ref: (ch, i, 0))],
            out_specs=pl.BlockSpec((1, tr, cols), lambda ch, i, kind_ref: (ch, i, 0))),
        out_shape=jax.ShapeDtypeStruct(received.shape, received.dtype),
        compiler_params=pltpu.CompilerParams(dimension_semantics=("arbitrary", "arbitrary")),
    )(kind, p4, received)


def _all_reduce_small(name, v):
    rows = v.shape[0]

    def body(v_ref, out_ref, land_ref, send_sems, recv_sems):
        x, y, c = _position()
        me = _linear(x, y, c)
        peers = _peers(x, y, c)
        land_ref[me] = v_ref[...]
        sends = [pltpu.make_async_remote_copy(
            src_ref=v_ref, dst_ref=land_ref.at[me], send_sem=send_sems.at[k], recv_sem=recv_sems.at[k],
            device_id=peer, device_id_type=_MESH) for k, peer in enumerate(peers)]
        for cp in sends:
            cp.start()
        for k, peer in enumerate(peers):
            pltpu.make_async_remote_copy(
                src_ref=v_ref, dst_ref=land_ref.at[_linear(*peer)], send_sem=send_sems.at[k], recv_sem=recv_sems.at[k],
                device_id=peer, device_id_type=_MESH).wait_recv()
        for cp in sends:
            cp.wait_send()
        total = land_ref[0]
        for s in range(1, N_DEV):
            total = total + land_ref[s]
        out_ref[...] = total

    return pl.pallas_call(
        body, name=name, out_shape=jax.ShapeDtypeStruct(v.shape, _F32),
        in_specs=[pl.BlockSpec(memory_space=pltpu.VMEM)], out_specs=pl.BlockSpec(memory_space=pltpu.VMEM),
        scratch_shapes=[pltpu.VMEM((N_DEV, rows, V7X_LANES), _F32), pltpu.SemaphoreType.DMA((7,)), pltpu.SemaphoreType.DMA((7,))],
    )(v)


def _adamw(name, w, slots, m, v):
    rows, cols = w.shape
    n_slots = slots.shape[0]
    tr = _pick(rows, (176, 144, 128, 64, 32, 16, 8))

    def body(w_ref, s_ref, m_ref, v_ref, g_ref, d_ref, nm_ref, nv_ref):
        g = s_ref[0].astype(_F32)
        for k in range(1, n_slots):
            g = g + s_ref[k].astype(_F32)
        nm = ADAM_B1 * m_ref[...] + (1.0 - ADAM_B1) * g
        nv = ADAM_B2 * v_ref[...] + (1.0 - ADAM_B2) * (g * g)
        m_hat = nm / (1.0 - ADAM_B1 ** ADAM_STEP)
        v_hat = nv / (1.0 - ADAM_B2 ** ADAM_STEP)
        g_ref[...] = g
        d_ref[...] = -ADAM_LR * (m_hat / (jnp.sqrt(v_hat) + ADAM_EPS) + ADAM_WD * w_ref[...])
        nm_ref[...] = nm
        nv_ref[...] = nv

    spec = pl.BlockSpec((tr, cols), lambda i: (i, 0))
    blocks = 7 * tr * cols * 4 + _nbytes((n_slots, tr, cols), slots.dtype)
    return pl.pallas_call(
        body, name=name, grid=(rows // tr,),
        in_specs=[spec, pl.BlockSpec((n_slots, tr, cols), lambda i: (0, i, 0)), spec, spec], out_specs=[spec] * 4,
        out_shape=[jax.ShapeDtypeStruct((rows, cols), _F32)] * 4,
        compiler_params=pltpu.CompilerParams(dimension_semantics=("arbitrary",), vmem_limit_bytes=_vmem_limit(blocks)),
    )(w, slots, m, v)


def _pad_rows(a, rows):
    return jnp.pad(a, ((0, rows - a.shape[0]), (0, 0)))


def _pack(parts):
    rows, spans, at = [], [], 0
    for p in parts:
        p = p.reshape(-1)
        r = -(-p.shape[0] // V7X_LANES)
        rows.append(jnp.pad(p, (0, r * V7X_LANES - p.shape[0])).reshape(r, V7X_LANES))
        spans.append((at, r, p.shape[0]))
        at += r
    packed = jnp.concatenate(rows, axis=0)
    return _pad_rows(packed, -(-at // V7X_SUBLANES) * V7X_SUBLANES), spans


def _unpack(packed, spans, shapes):
    return [packed[at:at + r].reshape(-1)[:size].reshape(shape) for (at, r, size), shape in zip(spans, shapes)]


def kernel(x, positions, w_in, conv_w, sinks, g_attn, g_conv, w_out, ln1_g, ln1_b, w_gate, w_up, w_down, ln2_g, ln2_b, loss_target, m_w_in, m_conv_w, m_sinks, m_g_attn, m_g_conv, m_w_out, m_ln1_g, m_ln1_b, m_w_gate, m_w_up, m_w_down, m_ln2_g, m_ln2_b, v_w_in, v_conv_w, v_sinks, v_g_attn, v_g_conv, v_w_out, v_ln1_g, v_ln1_b, v_w_gate, v_w_up, v_w_down, v_ln2_g, v_ln2_b):
    _, s, d = x.shape
    d_ff = N_DEV * w_gate.shape[2]
    dm = _Dims(s, d, d_ff)
    aw, cw, nq, inw = dm.aw, dm.cw, dm.nq, dm.inw
    x2 = x[0]
    pos = positions[0].reshape(s, 1)
    inv_freq = ROPE_THETA ** (-jnp.arange(0, ROT_DIM, 2, dtype=_F32) / ROT_DIM)
    invf = jnp.tile(inv_freq, V7X_LANES // (ROT_DIM // 2)).reshape(1, V7X_LANES)

    conv_cols = conv_w.shape[2]
    sh_in, sh_out = w_in[0].T.astype(_CDT), w_out[0].astype(_CDT)
    sh_gate, sh_up, sh_down = w_gate[0].T.astype(_CDT), w_up[0].T.astype(_CDT), w_down[0].astype(_CDT)
    r_in, r_out, r_ff = sh_in.shape[0], sh_out.shape[0], sh_gate.shape[0]
    q_ff = r_ff // 4
    assert q_ff % 16 == 0
    def prepare_body(x_ref, pos_ref, invf_ref, xc_ref, rope_ref):
        xc_ref[...] = x_ref[...].astype(_CDT)
        cos, sgn = _rope_tables(pos_ref[...], invf_ref[...])
        rope_ref[:, 0:V7X_LANES] = cos
        rope_ref[:, V7X_LANES:2 * V7X_LANES] = sgn

    x_c, rope, w_in_t, conv_all = _row_kernel(
        "prepare_gather_w_in", prepare_body, [x2, pos], [invf], [((s, d), _CDT), ((s, 2 * V7X_LANES), _F32)], [],
        comm=[_gather_op([(sh_in, None, 0, r_in), (_pad_rows(conv_w[0], 16), None, 0, 16)])])
    conv_full = conv_all.reshape(N_DEV, 16, conv_cols)[:, :3, :].transpose(1, 0, 2).reshape(3, cw)
    conv_w8 = _pad_rows(conv_full, V7X_SUBLANES)

    tm = _pick(s, (1024, 512, 256, 128))
    tn_in = _pick(inw, (512, 256, 128))
    tn_ff = _pick(d_ff, (512, 256, 128))
    tr = _pick(s, (512, 256, 128))

    proj, w_out_f, w_gate_t = _matmul(
        "proj", [[(x_c, w_in_t, "nt")]], s, inw, d, tm, tn_in, d, [],
        [((s, inw), _F32, (tm, tn_in), _tile_ij)], _store_epilogue,
        comm=[_gather_op([(sh_out, None, 0, r_out), (sh_gate, None, 0, 2 * q_ff)])])
    mixed, attn, lse, y_conv, w_gate_t, w_up_t = _mixer_fwd(
        dm, proj, rope, sinks, g_attn, g_conv, conv_w8,
        comm=[_gather_op([(sh_gate, w_gate_t, 2 * q_ff, r_ff), (sh_up, None, 0, 2 * q_ff)])])

    def residual_epilogue(accs, ex, out, first):
        out[0][...] = DEEPNORM_ALPHA * ex[0][...] + accs[0]

    tn_d = _pick(d, (512,))
    r1, w_up_t = _matmul(
        "out_proj", [[(mixed, w_out_f, "nn")]], s, d, d, tm, tn_d, d, [(x2, (tm, tn_d), _tile_ij)],
        [((s, d), _F32, (tm, tn_d), _tile_ij)], residual_epilogue,
        comm=[_gather_op([(sh_up, w_up_t, 2 * q_ff, 3 * q_ff)])])
    h1, h1_c, xhat1, rstd1, w_up_t = _ln1_fwd_rows(
        r1, ln1_g, ln1_b, comm=[_gather_op([(sh_up, w_up_t, 3 * q_ff, r_ff)])])

    def swiglu_epilogue(accs, ex, out, first):
        gate_v, up_v = accs
        out[0][...] = gate_v
        out[1][...] = up_v
        out[2][...] = (gate_v * jax.nn.sigmoid(gate_v) * up_v).astype(_CDT)

    gate, up, act, w_down_f = _matmul(
        "gate_up", [[(h1_c, w_gate_t, "nt")], [(h1_c, w_up_t, "nt")]], s, d_ff, d, tm, tn_ff, d, [],
        [((s, d_ff), _F32, (tm, tn_ff), _tile_ij), ((s, d_ff), _F32, (tm, tn_ff), _tile_ij),
         ((s, d_ff), _CDT, (tm, tn_ff), _tile_ij)], swiglu_epilogue,
        comm=[_gather_op([(sh_down, None, 0, r_ff)])])

    (r2,) = _matmul("down", [[(act, w_down_f, "nn")]], s, d, d_ff, tm, tn_d, d_ff, [(h1, (tm, tn_d), _tile_ij)],
                    [((s, d), _F32, (tm, tn_d), _tile_ij)], residual_epilogue)
    dr2, dr2_c, loss_acc, d_ln2_g, d_ln2_b = _ln2_loss_bwd(r2, loss_target[0], ln2_g, ln2_b)

    def swiglu_bwd_epilogue(accs, ex, out, first):
        gate_v, up_v = ex[0][...], ex[1][...]
        sig = jax.nn.sigmoid(gate_v)
        out[0][...] = (accs[0] * up_v * (sig * (1.0 + gate_v * (1.0 - sig)))).astype(_CDT)
        out[1][...] = (accs[0] * (gate_v * sig)).astype(_CDT)

    dgate, dup = _matmul(
        "dact", [[(dr2_c, w_down_f, "nt")]], s, d_ff, d, tm, tn_ff, d,
        [(gate, (tm, tn_ff), _tile_ij), (up, (tm, tn_ff), _tile_ij)],
        [((s, d_ff), _CDT, (tm, tn_ff), _tile_ij), ((s, d_ff), _CDT, (tm, tn_ff), _tile_ij)], swiglu_bwd_epilogue,
        n_split=2)
    def weight_grad(name, a, b, comm=()):
        rows = a.shape[1]
        tw, tn_w = _pick(rows, (512, 256, 128)), _pick(d, (1024, 512))
        return _matmul(name, [[(a, b, "tn")]], rows, d, s, tw, tn_w, s, [],
                       [((rows, d), _CDT, (tw, tn_w), _tile_ij)], _store_epilogue, comm=comm, j_outer=True)

    (dw_down,) = weight_grad("dw_down", act, dr2_c)
    dw_gate_t, x_down = weight_grad("dw_gate", dgate, h1_c, comm=[_exchange_op([dw_down])])
    q_down = _pair_sum("chip_sum_w_down", dw_down, x_down)
    dw_up_t, l_down, x_gate = weight_grad(
        "dw_up", dup, h1_c, comm=[_chip_send_op([(q_down, None, 0, 2 * q_ff)]), _exchange_op([dw_gate_t])])
    q_gate = _pair_sum("chip_sum_w_gate", dw_gate_t, x_gate)

    tn_h = _pick(d, (512,))
    dh1, l_down, l_gate, x_up = _matmul(
        "dh1", [[(dgate, w_gate_t, "nn"), (dup, w_up_t, "nn")]], s, d, d_ff, tr, tn_h, d_ff,
        [(dr2, (tr, tn_h), _tile_ij)], [((s, d), _F32, (tr, tn_h), _tile_ij)], residual_epilogue,
        comm=[_chip_send_op([(q_down, l_down, 2 * q_ff, r_ff), (q_gate, None, 0, r_ff)]), _exchange_op([dw_up_t])])
    q_up = _pair_sum("chip_sum_w_up", dw_up_t, x_up)
    dr1, dr1_c, d_ln1_g, d_ln1_b = _ln1_bwd_rows(dh1, xhat1, rstd1, ln1_g)
    (dmixed,) = _matmul("dmixed", [[(dr1_c, w_out_f, "nt")]], s, d, d, tm, tn_d, d, [],
                        [((s, d), _F32, (tm, tn_d), _tile_ij)], _store_epilogue)
    (dw_out,) = weight_grad("dw_out", mixed, dr1_c)
    dproj, dkv, d_g_attn, d_g_conv, d_sinks, d_conv8, l_up, x_out = _mixer_bwd(
        dm, proj, rope, sinks, g_attn, g_conv, conv_w8, dmixed, attn, lse, y_conv,
        comm=[_chip_send_op([(q_up, None, 0, r_ff)]), _exchange_op([dw_out])])
    dproj = _patch_columns("dproj_kv", dproj, dkv, dm.o_k)
    q_out = _pair_sum("chip_sum_w_out", dw_out, x_out)
    dw_in_t, l_out = weight_grad("dw_in", dproj, x_c, comm=[_chip_send_op([(q_out, None, 0, r_out)])])
    (x_in,) = _comm_kernel("exchange_w_in", [_exchange_op([dw_in_t])])
    q_in = _pair_sum("chip_sum_w_in", dw_in_t, x_in)

    grad_x, l_in = _matmul("dx", [[(dproj, w_in_t, "nn")]], s, d, inw, tm, tn_d, inw,
                           [(dr1, (tm, tn_d), _tile_ij)], [((s, d), _F32, (tm, tn_d), _tile_ij)], residual_epilogue,
                           comm=[_chip_send_op([(q_in, None, 0, r_in)])])

    small_parts = [d_conv8[:3], d_sinks, d_g_attn, d_g_conv, d_ln1_g, d_ln1_b, d_ln2_g, d_ln2_b]
    packed, spans = _pack(small_parts)
    reduced = _unpack(_all_reduce_small("reduce_small", packed), spans, [p.shape for p in small_parts])
    g_conv_full, g_sinks, g_g_attn, g_g_conv, g_ln1_g, g_ln1_b, g_ln2_g, g_ln2_b = reduced
    me = _linear(*_position())
    g_conv_w = lax.dynamic_slice(g_conv_full, (0, me * conv_cols), (3, conv_cols))
    loss = lax.psum(loss_acc[0, 0], ("x", "y", "c"))

    big = {"w_in": (w_in[0].T, l_in, m_w_in[0].T, v_w_in[0].T), "w_out": (w_out[0], l_out, m_w_out[0], v_w_out[0]),
           "w_gate": (w_gate[0].T, l_gate, m_w_gate[0].T, v_w_gate[0].T),
           "w_up": (w_up[0].T, l_up, m_w_up[0].T, v_w_up[0].T), "w_down": (w_down[0], l_down, m_w_down[0], v_w_down[0])}
    res = {nm: tuple(_adamw(f"adamw_{nm}", w, slots, m, v)) for nm, (w, slots, m, v) in big.items()}
    for nm in ("w_in", "w_gate", "w_up"):
        res[nm] = tuple(a.T for a in res[nm])
    small_names = ["conv_w", "sinks", "g_attn", "g_conv", "ln1_g", "ln1_b", "ln2_g", "ln2_b"]
    small_w = [conv_w, sinks, g_attn, g_conv, ln1_g, ln1_b, ln2_g, ln2_b]
    small_g = [g_conv_w[None], g_sinks, g_g_attn, g_g_conv, g_ln1_g, g_ln1_b, g_ln2_g, g_ln2_b]
    small_m = [m_conv_w, m_sinks, m_g_attn, m_g_conv, m_ln1_g, m_ln1_b, m_ln2_g, m_ln2_b]
    small_v = [v_conv_w, v_sinks, v_g_attn, v_g_conv, v_ln1_g, v_ln1_b, v_ln2_g, v_ln2_b]
    pw, sp = _pack(small_w)
    pg, _ = _pack(small_g)
    pm, _ = _pack(small_m)
    pv, _ = _pack(small_v)
    shapes = [w.shape for w in small_w]
    _, sd, sm, sv = [_unpack(p, sp, shapes) for p in _adamw("adamw_small", pw, pg[None], pm, pv)]
    for i, nm in enumerate(small_names):
        res[nm] = (small_g[i].reshape(shapes[i]), sd[i], sm[i], sv[i])

    order = ["w_in", "conv_w", "sinks", "g_attn", "g_conv", "w_out", "ln1_g", "ln1_b", "w_gate", "w_up", "w_down", "ln2_g", "ln2_b"]

    def lead(a, nm):
        return a[None] if nm in big else a

    return (loss, grad_x[None],
            *[lead(res[nm][0], nm) for nm in order], *[lead(res[nm][1], nm) for nm in order],
            *[lead(res[nm][2], nm) for nm in order], *[lead(res[nm][3], nm) for nm in order])
```

```python
import functools

import jax
import jax.numpy as jnp
from jax import lax
from jax.experimental import pallas as pl
from jax.experimental.pallas import tpu as pltpu

_F32 = jnp.float32
_CDT = jnp.bfloat16

HEAD_DIM = 64
WINDOW = 128
N_KV_HEADS = 4
KV_WIDTH = N_KV_HEADS * HEAD_DIM
ROT_DIM = HEAD_DIM // 4
ROPE_THETA = 500000.0
ATTN_SCALE = HEAD_DIM ** -0.5
DEPTH = 1
DEEPNORM_ALPHA = (2 * DEPTH) ** 0.25
LN_EPS = 1e-5
RMS_EPS = 1e-6
ADAM_LR = 0.001
ADAM_B1 = 0.9
ADAM_B2 = 0.999
ADAM_EPS = 1e-08
ADAM_WD = 0.01
ADAM_STEP = 10
N_DEV = 8
MASKED = -1e30

MIB = 1024 * 1024
V7X_VMEM_BYTES = 64 * MIB
V7X_LANES = 128
V7X_SUBLANES = 8
BODY_TEMPORARIES_BYTES = 16 * MIB
VMEM_LIMIT_FLOOR_BYTES = 32 * MIB
VMEM_LIMIT_CEILING_BYTES = V7X_VMEM_BYTES - 8 * MIB
_MESH = pl.DeviceIdType.MESH
_ANY = pl.BlockSpec(memory_space=pl.ANY)


def _vmem_limit(block_bytes, scratch_bytes=0):
    want = 2 * block_bytes + scratch_bytes + BODY_TEMPORARIES_BYTES
    return int(min(max(want, VMEM_LIMIT_FLOOR_BYTES), VMEM_LIMIT_CEILING_BYTES))


def _nbytes(shape, dtype):
    n = 1
    for s in shape:
        n *= s
    return n * jnp.dtype(dtype).itemsize


def _pick(n, candidates):
    for c in candidates:
        if n % c == 0:
            return c
    raise ValueError(f"no tile of {candidates} divides {n}")


_DOT_DIMS = {"nn": ((1,), (0,)), "nt": ((1,), (1,)), "tn": ((0,), (0,))}


def _dot(a, b, mode):
    return lax.dot_general(a.astype(_CDT), b.astype(_CDT), (_DOT_DIMS[mode], ((), ())),
                           preferred_element_type=_F32)


def _accumulate(ref, val, first):
    @pl.when(first)
    def _():
        ref[...] = val

    @pl.when(jnp.logical_not(first))
    def _():
        ref[...] += val


class _Comm:
    def __init__(self, inputs, outputs, aliases, sems, start, finish, middle=None):
        self.inputs, self.outputs, self.aliases, self.sems = inputs, outputs, aliases, sems
        self.start, self.finish, self.middle = start, finish, middle


def _middle_step(n_steps):
    return (2 * n_steps) // 3


class _CommArgs:
    def __init__(self, comms, n_in_before, n_out_before):
        self.comms, self.operands, self.out_shape, self.aliases, self.sems, self.at = comms, [], [], {}, [], []
        for cm in comms:
            self.at.append((len(self.operands), len(self.out_shape), len(self.sems)))
            for i_in, i_out in cm.aliases.items():
                self.aliases[n_in_before + len(self.operands) + i_in] = n_out_before + len(self.out_shape) + i_out
            self.operands += cm.inputs
            self.out_shape += cm.outputs
            self.sems += cm.sems

    def _each(self, in_refs, out_refs, sem_refs):
        for cm, (i0, o0, s0) in zip(self.comms, self.at):
            yield cm, (in_refs[i0:i0 + len(cm.inputs)], out_refs[o0:o0 + len(cm.outputs)], sem_refs[s0:s0 + len(cm.sems)])

    def start(self, in_refs, out_refs, sem_refs):
        for cm, refs in self._each(in_refs, out_refs, sem_refs):
            cm.start(*refs)

    def finish(self, in_refs, out_refs, sem_refs):
        for cm, refs in self._each(in_refs, out_refs, sem_refs):
            cm.finish(*refs)

    @property
    def has_middle(self):
        return any(cm.middle is not None for cm in self.comms)

    def middle(self, in_refs, out_refs, sem_refs):
        for cm, refs in self._each(in_refs, out_refs, sem_refs):
            if cm.middle is not None:
                cm.middle(*refs)


def _matmul(name, groups, m, n, k, tm, tn, tk, extras, outs, epilogue, comm=(), j_outer=False):
    assert m % tm == 0 and n % tn == 0 and k % tk == 0, (name, m, n, k, tm, tn, tk)
    nk = k // tk
    terms = [t for g in groups for t in g]
    operands, in_specs, block_bytes = [], [], 0

    def spec(blk, imap):
        return pl.BlockSpec(blk, (lambda g0, g1, kk: imap(g1, g0, kk)) if j_outer else imap)

    for a, b, mode in terms:
        assert a.shape == ((k, m) if mode == "tn" else (m, k)), (name, a.shape, mode)
        assert b.shape == ((n, k) if mode == "nt" else (k, n)), (name, b.shape, mode)
        if mode == "tn":
            a_blk, a_map = (tk, tm), (lambda i, j, kk: (kk, i))
        else:
            a_blk, a_map = (tm, tk), (lambda i, j, kk: (i, kk))
        if mode == "nt":
            b_blk, b_map = (tn, tk), (lambda i, j, kk: (j, kk))
        else:
            b_blk, b_map = (tk, tn), (lambda i, j, kk: (kk, j))
        operands += [a, b]
        in_specs += [spec(a_blk, a_map), spec(b_blk, b_map)]
        block_bytes += _nbytes(a_blk, a.dtype) + _nbytes(b_blk, b.dtype)
    for arr, blk, imap in extras:
        operands.append(arr)
        in_specs.append(spec(blk, lambda i, j, kk, imap=imap: imap(i, j)))
        block_bytes += _nbytes(blk, arr.dtype)
    out_shape, out_specs = [], []
    for shape, dtype, blk, imap in outs:
        out_shape.append(jax.ShapeDtypeStruct(shape, dtype))
        out_specs.append(spec(blk, lambda i, j, kk, imap=imap: imap(i, j)))
        block_bytes += _nbytes(blk, dtype)
    n_terms, n_extra, n_out, n_groups = len(terms), len(extras), len(outs), len(groups)
    scratch = [pltpu.VMEM((tm, tn), _F32) for _ in range(n_groups)] if nk > 1 else []
    ca = _CommArgs(list(comm), len(operands), n_out)
    n_cin, n_cout, n_acc = len(ca.operands), len(ca.out_shape), len(scratch)
    tiles = (m // tm, n // tn)
    grid = (tiles[1], tiles[0], nk) if j_outer else (tiles[0], tiles[1], nk)

    def body(*refs):
        refs = list(refs)
        term_refs = [refs.pop(0) for _ in range(2 * n_terms)]
        extra_refs = [refs.pop(0) for _ in range(n_extra)]
        cin_refs = [refs.pop(0) for _ in range(n_cin)]
        out_refs = [refs.pop(0) for _ in range(n_out)]
        cout_refs = [refs.pop(0) for _ in range(n_cout)]
        acc_refs = [refs.pop(0) for _ in range(n_acc)]
        sem_refs = refs
        g0, g1, kk = pl.program_id(0), pl.program_id(1), pl.program_id(2)
        first = jnp.logical_and(g0 == 0, g1 == 0)
        if comm:
            @pl.when(jnp.logical_and(first, kk == 0))
            def _():
                ca.start(cin_refs, cout_refs, sem_refs)
        if ca.has_middle:
            step = (g0 * grid[1] + g1) * nk + kk

            @pl.when(step == _middle_step(grid[0] * grid[1] * nk))
            def _():
                ca.middle(cin_refs, cout_refs, sem_refs)
        partial, t = [], 0
        for g in groups:
            s = None
            for _, _, mode in g:
                d = _dot(term_refs[2 * t][...], term_refs[2 * t + 1][...], mode)
                s = d if s is None else s + d
                t += 1
            partial.append(s)
        if nk == 1:
            epilogue(partial, extra_refs, out_refs, first)
        else:
            for acc, p in zip(acc_refs, partial):
                _accumulate(acc, p, kk == 0)

            @pl.when(kk == nk - 1)
            def _():
                epilogue([acc[...] for acc in acc_refs], extra_refs, out_refs, first)
        if comm:
            @pl.when(jnp.logical_and(jnp.logical_and(g0 == grid[0] - 1, g1 == grid[1] - 1), kk == nk - 1))
            def _():
                ca.finish(cin_refs, cout_refs, sem_refs)

    res = pl.pallas_call(
        body, name=name, grid=grid,
        in_specs=in_specs + [_ANY] * n_cin, out_specs=out_specs + [_ANY] * n_cout,
        out_shape=out_shape + ca.out_shape, scratch_shapes=scratch + ca.sems, input_output_aliases=ca.aliases,
        compiler_params=pltpu.CompilerParams(
            dimension_semantics=("arbitrary", "arbitrary", "arbitrary"),
            vmem_limit_bytes=_vmem_limit(block_bytes, n_groups * tm * tn * 4 if nk > 1 else 0)),
    )(*operands, *ca.operands)
    return list(res[:n_out]) + list(res[n_out:])


def _store_epilogue(accs, extra_refs, out_refs, first):
    for acc, ref in zip(accs, out_refs):
        ref[...] = acc.astype(ref.dtype)


def _tile_ij(i, j):
    return (i, j)


def _row_i(i, j):
    return (i, 0)


def _whole(i, j):
    return (0, 0)


def _mean(v):
    return jnp.mean(v, axis=-1, keepdims=True)


def _ln_fwd(r, g, b):
    xc = r - _mean(r)
    rstd = lax.rsqrt(_mean(xc * xc) + LN_EPS)
    xhat = xc * rstd
    return xhat * g + b, xhat, rstd


def _ln_bwd(dy, xhat, rstd, g):
    dxh = dy * g
    dr = rstd * (dxh - _mean(dxh) - xhat * _mean(dxh * xhat))
    return dr, jnp.sum(dy * xhat, axis=0, keepdims=True), jnp.sum(dy, axis=0, keepdims=True)


def _rms_fwd(a, g):
    rstd = lax.rsqrt(_mean(a * a) + RMS_EPS)
    return a * rstd * g


def _rms_bwd(dm, a, g):
    rstd = lax.rsqrt(_mean(a * a) + RMS_EPS)
    nhat = a * rstd
    dn = dm * g
    da = rstd * (dn - nhat * _mean(dn * nhat))
    return da, jnp.sum(dm * nhat, axis=0, keepdims=True)


def _lane(shape):
    return lax.broadcasted_iota(jnp.int32, shape, 1)


def _row(shape):
    return lax.broadcasted_iota(jnp.int32, shape, 0)


def _rope_tables(pos, invf):
    ang = pos.astype(_F32) * invf
    lane = _lane(ang.shape)
    in_rot = (lane % HEAD_DIM) < ROT_DIM
    first = (lane % ROT_DIM) < ROT_DIM // 2
    cos = jnp.where(in_rot, jnp.cos(ang), 1.0)
    sin = jnp.sin(ang)
    sgn = jnp.where(in_rot, jnp.where(first, -sin, sin), 0.0)
    return cos, sgn


def _rope(t, cos, sgn, sign):
    half = ROT_DIM // 2
    first = (_lane(t.shape) % ROT_DIM) < half
    partner = jnp.where(first, pltpu.roll(t, V7X_LANES - half, 1), pltpu.roll(t, half, 1))
    return t * cos + partner * (sgn * sign)


def _dup_head(t, h):
    g = t[:, 128 * (h // 2):128 * (h // 2) + 128]
    r = pltpu.roll(g, HEAD_DIM, 1)
    lo = _lane(g.shape) < HEAD_DIM
    return jnp.where(lo, g, r) if h % 2 == 0 else jnp.where(lo, r, g)


def _fold_halves(t):
    return t + pltpu.roll(t, HEAD_DIM, 1)


def _halves(t):
    lo = _lane(t.shape) < HEAD_DIM
    zero = jnp.zeros_like(t)
    return jnp.where(lo, t, zero), jnp.where(lo, zero, t)


def _band_mask(n_heads, n_keys, first_block):
    shape = (n_heads * WINDOW, n_keys)
    i = jnp.bitwise_and(_row(shape), WINDOW - 1)
    j = _lane(shape)
    valid = jnp.logical_and(j >= i + 1, j <= i + WINDOW)
    if first_block is not None:
        valid = jnp.logical_and(valid, jnp.logical_or(j >= WINDOW, jnp.logical_not(first_block)))
    return valid


def _stack_heads(pairs):
    return jnp.concatenate([half for t in pairs for half in _halves(t)], axis=0).astype(_CDT)


def _unstack_heads(t, n_pairs):
    lo = _lane((WINDOW, 128)) < HEAD_DIM
    return [jnp.where(lo, t[2 * WINDOW * i:2 * WINDOW * i + WINDOW], t[2 * WINDOW * i + WINDOW:2 * WINDOW * (i + 1)])
            for i in range(n_pairs)]


def _per_head(values):
    n_rows = len(values) * WINDOW
    block = jnp.right_shift(_row((n_rows, 1)), WINDOW.bit_length() - 1)
    out = jnp.zeros((n_rows, 1), _F32)
    for k, v in enumerate(values):
        out = jnp.where(block == k, v, out)
    return out


def _shift_down(z, halo, k):
    out = pltpu.roll(z, k, 0)
    r = _row(z.shape)
    for t in range(k):
        out = jnp.where(r == t, halo[V7X_SUBLANES - k + t:V7X_SUBLANES - k + t + 1, :], out)
    return out


def _shift_up(z, halo, k):
    rows = z.shape[0]
    out = pltpu.roll(z, rows - k, 0)
    r = _row(z.shape)
    for t in range(k):
        out = jnp.where(r == rows - k + t, halo[t:t + 1, :], out)
    return out


class _Dims:
    def __init__(self, s, d, d_ff):
        self.s, self.d, self.d_ff = s, d, d_ff
        self.aw = d // 2
        self.cw = d - self.aw
        self.nq = self.aw // HEAD_DIM
        self.group = self.nq // N_KV_HEADS
        assert self.group % 2 == 0, "a 128-lane pair of query heads must share its kv head"
        self.inw = self.aw + 2 * KV_WIDTH + 3 * self.cw
        self.o_k = self.aw
        self.o_v = self.aw + KV_WIDTH
        self.o_cg = self.aw + 2 * KV_WIDTH
        self.o_bg = self.o_cg + self.cw
        self.o_u = self.o_bg + self.cw
        self.nb = s // WINDOW
        assert s % WINDOW == 0


def _carrying(body, n_in, n_out, n_steps, ca, n_scratch=0):
    n_cin, n_cout = len(ca.operands), len(ca.out_shape)

    def wrapped(*refs):
        refs = list(refs)
        in_refs = [refs.pop(0) for _ in range(n_in)]
        cin_refs = [refs.pop(0) for _ in range(n_cin)]
        out_refs = [refs.pop(0) for _ in range(n_out)]
        cout_refs = [refs.pop(0) for _ in range(n_cout)]
        scratch_refs = [refs.pop(0) for _ in range(n_scratch)]
        if ca.comms:
            @pl.when(pl.program_id(0) == 0)
            def _():
                ca.start(cin_refs, cout_refs, refs)
        if ca.has_middle:
            @pl.when(pl.program_id(0) == _middle_step(n_steps))
            def _():
                ca.middle(cin_refs, cout_refs, refs)
        body(*in_refs, *out_refs, *scratch_refs)
        if ca.comms:
            @pl.when(pl.program_id(0) == n_steps - 1)
            def _():
                ca.finish(cin_refs, cout_refs, refs)

    return wrapped


def _row_kernel(name, body, rows_in, vecs_in, rows_out, vecs_out, comm=()):
    s = rows_in[0].shape[0]
    tr = _pick(s, (256, 128))
    row = lambda a: pl.BlockSpec((tr, a[1] if isinstance(a, tuple) else a.shape[1]), lambda i: (i, 0))
    vec = lambda shape: pl.BlockSpec(tuple(shape), lambda i: (0, 0))
    n_in, n_out = len(rows_in) + len(vecs_in), len(rows_out) + len(vecs_out)
    ca = _CommArgs(list(comm), n_in, n_out)
    blocks = sum(_nbytes((tr, a.shape[1]), a.dtype) for a in rows_in) + sum(_nbytes((tr, sh[1]), dt) for sh, dt in rows_out)
    res = pl.pallas_call(
        _carrying(body, n_in, n_out, s // tr, ca), name=name, grid=(s // tr,),
        in_specs=[row(a) for a in rows_in] + [vec(v.shape) for v in vecs_in] + [_ANY] * len(ca.operands),
        out_specs=[row(sh) for sh, _ in rows_out] + [vec(sh) for sh, _ in vecs_out] + [_ANY] * len(ca.out_shape),
        out_shape=[jax.ShapeDtypeStruct(sh, dt) for sh, dt in list(rows_out) + list(vecs_out)] + ca.out_shape,
        scratch_shapes=ca.sems, input_output_aliases=ca.aliases,
        compiler_params=pltpu.CompilerParams(dimension_semantics=("arbitrary",), vmem_limit_bytes=_vmem_limit(blocks)),
    )(*rows_in, *vecs_in, *ca.operands)
    return list(res)


def _ln2_loss_bwd(r2, target, gain, bias, comm=()):
    s, d = r2.shape

    def body(r_ref, t_ref, g_ref, b_ref, dr_ref, drc_ref, loss_ref, dg_ref, db_ref):
        first = pl.program_id(0) == 0
        yv, xhat, rstd = _ln_fwd(r_ref[...], g_ref[...], b_ref[...])
        err = yv - t_ref[...]
        dr2, dg, db = _ln_bwd(err * (1.0 / d), xhat, rstd, g_ref[...])
        dr_ref[...] = dr2
        drc_ref[...] = dr2.astype(_CDT)
        _accumulate(loss_ref, jnp.zeros(loss_ref.shape, _F32) + 0.5 * jnp.sum(err * err) * (1.0 / d), first)
        _accumulate(dg_ref, dg, first)
        _accumulate(db_ref, db, first)

    return _row_kernel("ln2_loss_bwd", body, [r2, target], [gain, bias], [((s, d), _F32), ((s, d), _CDT)],
                       [((V7X_SUBLANES, V7X_LANES), _F32), ((1, d), _F32), ((1, d), _F32)], comm)


def _ln1_fwd_rows(r1, gain, bias, comm=()):
    s, d = r1.shape

    def body(r_ref, g_ref, b_ref, h_ref, hc_ref, xhat_ref, rstd_ref):
        h1, xhat, rstd = _ln_fwd(r_ref[...], g_ref[...], b_ref[...])
        h_ref[...] = h1
        hc_ref[...] = h1.astype(_CDT)
        xhat_ref[...] = xhat
        rstd_ref[...] = rstd

    return _row_kernel("ln1", body, [r1], [gain, bias],
                       [((s, d), _F32), ((s, d), _CDT), ((s, d), _F32), ((s, 1), _F32)], [], comm)


def _ln1_bwd_rows(dh1, xhat, rstd, gain, comm=()):
    s, d = dh1.shape

    def body(dh_ref, xhat_ref, rstd_ref, g_ref, dr_ref, drc_ref, dg_ref, db_ref):
        first = pl.program_id(0) == 0
        dr1, dg, db = _ln_bwd(dh_ref[...], xhat_ref[...], rstd_ref[...], g_ref[...])
        dr_ref[...] = dr1
        drc_ref[...] = dr1.astype(_CDT)
        _accumulate(dg_ref, dg, first)
        _accumulate(db_ref, db, first)

    return _row_kernel("ln1_bwd", body, [dh1, xhat, rstd], [gain], [((s, d), _F32), ((s, d), _CDT)],
                       [((1, d), _F32), ((1, d), _F32)], comm)


def _mixer_fwd(dm, proj, rope, sinks, g_attn, g_conv, conv_w8, comm=()):
    s, d, aw, cw, nq, inw, nb = dm.s, dm.d, dm.aw, dm.cw, dm.nq, dm.inw, dm.nb

    def body(pp_ref, pc_ref, ropep_ref, ropec_ref, sinks_ref, ga_ref, gc_ref, cw_ref,
             mixed_ref, attn_ref, lse_ref, y_ref):
        n = pl.program_id(0)
        cos_c, sgn_c = ropec_ref[:, 0:V7X_LANES], ropec_ref[:, V7X_LANES:2 * V7X_LANES]
        cos_p, sgn_p = ropep_ref[:, 0:V7X_LANES], ropep_ref[:, V7X_LANES:2 * V7X_LANES]
        kk = jnp.concatenate(
            [jnp.concatenate([_rope(ref[:, dm.o_k + 128 * g:dm.o_k + 128 * g + 128], c, sg, 1.0)
                              for g in range(KV_WIDTH // 128)], axis=1)
             for ref, c, sg in ((pp_ref, cos_p, sgn_p), (pc_ref, cos_c, sgn_c))], axis=0)
        vv = jnp.concatenate([pp_ref[:, dm.o_v:dm.o_v + KV_WIDTH], pc_ref[:, dm.o_v:dm.o_v + KV_WIDTH]], axis=0)
        group, pairs = dm.group, dm.group // 2
        valid = _band_mask(group, 2 * WINDOW, n == 0)
        for h in range(N_KV_HEADS):
            k2, v2 = _dup_head(kk, h).astype(_CDT), _dup_head(vv, h).astype(_CDT)
            q4 = _stack_heads([_rope(pc_ref[:, 128 * j:128 * j + 128], cos_c, sgn_c, 1.0)
                               for j in range(pairs * h, pairs * (h + 1))])
            sc = jnp.where(valid, _dot(q4, k2, "nt") * ATTN_SCALE, MASKED)
            sink = _per_head([sinks_ref[0, group * h + r] for r in range(group)])
            mx = jnp.maximum(jnp.max(sc, axis=1, keepdims=True), sink)
            p = jnp.exp(sc - mx)
            den = jnp.sum(p, axis=1, keepdims=True) + jnp.exp(sink - mx)
            out = _unstack_heads(_dot(p / den, v2, "nn"), pairs)
            lse = mx + jnp.log(den)
            for r in range(group):
                lse_ref[:, group * h + r:group * h + r + 1] = lse[WINDOW * r:WINDOW * (r + 1)]
            for i in range(pairs):
                j = pairs * h + i
                attn_ref[:, 128 * j:128 * j + 128] = out[i]
        mixed_ref[:, 0:aw] = _rms_fwd(attn_ref[...], ga_ref[...]).astype(mixed_ref.dtype)

        z = pc_ref[:, dm.o_cg:dm.o_cg + cw] * pc_ref[:, dm.o_u:dm.o_u + cw]
        top = WINDOW - V7X_SUBLANES
        halo = pp_ref[top:WINDOW, dm.o_cg:dm.o_cg + cw] * pp_ref[top:WINDOW, dm.o_u:dm.o_u + cw]
        halo = jnp.where(n == 0, jnp.zeros_like(halo), halo)
        y = cw_ref[0:1, :] * _shift_down(z, halo, 2) + cw_ref[1:2, :] * _shift_down(z, halo, 1) + cw_ref[2:3, :] * z
        y_ref[...] = y
        conv = pc_ref[:, dm.o_bg:dm.o_bg + cw] * y
        mixed_ref[:, aw:d] = _rms_fwd(conv, gc_ref[...]).astype(mixed_ref.dtype)

    prev = lambda n: (jnp.maximum(n - 1, 0), 0)
    cur = lambda n: (n, 0)
    fixed = lambda n: (0, 0)
    blocks = 2 * WINDOW * inw * 4 + WINDOW * (d * 2 + aw * 4 + cw * 4 + nq * 4)
    ca = _CommArgs(list(comm), 8, 4)
    return pl.pallas_call(
        _carrying(body, 8, 4, nb, ca), name="mixer_fwd", grid=(nb,),
        in_specs=[pl.BlockSpec((WINDOW, inw), prev), pl.BlockSpec((WINDOW, inw), cur),
                  pl.BlockSpec((WINDOW, 2 * V7X_LANES), prev), pl.BlockSpec((WINDOW, 2 * V7X_LANES), cur),
                  pl.BlockSpec(memory_space=pltpu.SMEM),
                  pl.BlockSpec((1, aw), fixed), pl.BlockSpec((1, cw), fixed), pl.BlockSpec((V7X_SUBLANES, cw), fixed)]
        + [_ANY] * len(ca.operands),
        out_specs=[pl.BlockSpec((WINDOW, d), cur), pl.BlockSpec((WINDOW, aw), cur),
                   pl.BlockSpec((WINDOW, nq), cur), pl.BlockSpec((WINDOW, cw), cur)] + [_ANY] * len(ca.out_shape),
        out_shape=[jax.ShapeDtypeStruct((s, d), _CDT), jax.ShapeDtypeStruct((s, aw), _F32),
                   jax.ShapeDtypeStruct((s, nq), _F32), jax.ShapeDtypeStruct((s, cw), _F32)] + ca.out_shape,
        scratch_shapes=ca.sems, input_output_aliases=ca.aliases,
        compiler_params=pltpu.CompilerParams(dimension_semantics=("arbitrary",), vmem_limit_bytes=_vmem_limit(blocks)),
    )(proj, proj, rope, rope, sinks, g_attn, g_conv, conv_w8, *ca.operands)


def _patch_columns(name, a, part, offset):
    s, pw = part.shape
    assert offset % pw == 0 and pw % V7X_LANES == 0
    tr = _pick(s, (512, 256, 128))

    def body(a_ref, p_ref, o_ref):
        del a_ref
        o_ref[...] = p_ref[...]

    return pl.pallas_call(
        body, name=name, grid=(s // tr,),
        in_specs=[_ANY, pl.BlockSpec((tr, pw), lambda i: (i, 0))],
        out_specs=pl.BlockSpec((tr, pw), lambda i: (i, offset // pw)),
        out_shape=jax.ShapeDtypeStruct(a.shape, a.dtype), input_output_aliases={0: 0},
        compiler_params=pltpu.CompilerParams(dimension_semantics=("arbitrary",)),
    )(a, part)


def _mixer_bwd(dm, proj, rope, sinks, g_attn, g_conv, conv_w8, dmixed, attn, lse, y, comm=()):
    s, d, aw, cw, nq, inw, nb = dm.s, dm.d, dm.aw, dm.cw, dm.nq, dm.inw, dm.nb

    def body(pp_ref, pc_ref, pn_ref, ropep_ref, ropec_ref, dmc_ref, dmn_ref, ac_ref,
             lsec_ref, yc_ref, yn_ref, sinks_ref, ga_ref, gc_ref, cw_ref,
             dproj_ref, dkv_ref, dga_ref, dgc_ref, dsinks_ref, dcw_ref, dk_carry, dv_carry):
        n = pl.program_id(0)
        first = n == 0
        live = n < nb
        has_next = n < nb - 1
        cos_p, sgn_p = ropep_ref[:, 0:V7X_LANES], ropep_ref[:, V7X_LANES:2 * V7X_LANES]
        cos_c, sgn_c = ropec_ref[:, 0:V7X_LANES], ropec_ref[:, V7X_LANES:2 * V7X_LANES]

        @pl.when(first)
        def _():
            dk_carry[...] = jnp.zeros(dk_carry.shape, _F32)
            dv_carry[...] = jnp.zeros(dv_carry.shape, _F32)

        def write_kv(dk2, dv2, cos, sgn):
            lo = _lane((WINDOW, 128)) < HEAD_DIM
            for g in range(KV_WIDTH // 128):
                dk = jnp.where(lo, _fold_halves(dk2[2 * g]), _fold_halves(dk2[2 * g + 1]))
                dv = jnp.where(lo, _fold_halves(dv2[2 * g]), _fold_halves(dv2[2 * g + 1]))
                dkv_ref[:, 128 * g:128 * g + 128] = _rope(dk, cos, sgn, -1.0).astype(dkv_ref.dtype)
                dkv_ref[:, KV_WIDTH + 128 * g:KV_WIDTH + 128 * g + 128] = dv.astype(dkv_ref.dtype)

        @pl.when(jnp.logical_not(live))
        def _():
            write_kv([dk_carry[h] for h in range(N_KV_HEADS)], [dv_carry[h] for h in range(N_KV_HEADS)], cos_c, sgn_c)

        @pl.when(live)
        def _():
            block_step(pp_ref, pc_ref, pn_ref, dmc_ref, dmn_ref, ac_ref, lsec_ref, yc_ref, yn_ref, sinks_ref, ga_ref,
                       gc_ref, cw_ref, dproj_ref, dga_ref, dgc_ref, dsinks_ref, dcw_ref, dk_carry, dv_carry,
                       first, has_next, cos_p, sgn_p, cos_c, sgn_c, write_kv)

    def block_step(pp_ref, pc_ref, pn_ref, dmc_ref, dmn_ref, ac_ref, lsec_ref, yc_ref, yn_ref, sinks_ref, ga_ref,
                   gc_ref, cw_ref, dproj_ref, dga_ref, dgc_ref, dsinks_ref, dcw_ref, dk_carry, dv_carry,
                   first, has_next, cos_p, sgn_p, cos_c, sgn_c, write_kv):
        da_c, dga = _rms_bwd(dmc_ref[:, 0:aw], ac_ref[...], ga_ref[...])
        _accumulate(dga_ref, dga, first)
        kk = jnp.concatenate(
            [jnp.concatenate([_rope(ref[:, dm.o_k + 128 * g:dm.o_k + 128 * g + 128], c, sg, 1.0)
                              for g in range(KV_WIDTH // 128)], axis=1)
             for ref, c, sg in ((pp_ref, cos_p, sgn_p), (pc_ref, cos_c, sgn_c))], axis=0)
        vv = jnp.concatenate([pp_ref[:, dm.o_v:dm.o_v + KV_WIDTH], pc_ref[:, dm.o_v:dm.o_v + KV_WIDTH]], axis=0)
        group, pairs = dm.group, dm.group // 2
        valid_c = _band_mask(group, 2 * WINDOW, first)
        dk_prev, dv_prev = [], []
        dsinks = jnp.zeros((1, nq), _F32)
        head_lane = _lane((1, nq))

        def stacked(q_ref, cos, sgn, da, o_ref, lse_ref_, h):
            cols = [slice(128 * j, 128 * j + 128) for j in range(pairs * h, pairs * (h + 1))]
            q4 = _stack_heads([_rope(q_ref[:, c], cos, sgn, 1.0) for c in cols])
            do4 = _stack_heads([da[:, c] for c in cols])
            lo = _lane((WINDOW, 128)) < HEAD_DIM
            deltas = []
            for c in cols:
                prod = o_ref[:, c] * da[:, c]
                deltas += [jnp.sum(jnp.where(lo, prod, 0.0), axis=1, keepdims=True),
                           jnp.sum(jnp.where(lo, 0.0, prod), axis=1, keepdims=True)]
            lse4 = jnp.concatenate([lse_ref_[:, group * h + r:group * h + r + 1] for r in range(group)], axis=0)
            return q4, do4, lse4, jnp.concatenate(deltas, axis=0)

        def scores_bwd(q4, do4, lse4, delta4, keys, vals, valid):
            sc = _dot(q4, keys, "nt") * ATTN_SCALE
            p = jnp.exp(jnp.where(valid, sc - lse4, MASKED))
            return p.astype(_CDT), (p * (_dot(do4, vals, "nt") - delta4) * ATTN_SCALE).astype(_CDT)

        for h in range(N_KV_HEADS):
            k2, v2 = _dup_head(kk, h).astype(_CDT), _dup_head(vv, h).astype(_CDT)
            q4, do4, lse4, delta4 = stacked(pc_ref, cos_c, sgn_c, da_c, ac_ref, lsec_ref, h)
            p, ds = scores_bwd(q4, do4, lse4, delta4, k2, v2, valid_c)
            for i, dq in enumerate(_unstack_heads(_dot(ds, k2, "nn"), pairs)):
                j = pairs * h + i
                dproj_ref[:, 128 * j:128 * j + 128] = _rope(dq, cos_c, sgn_c, -1.0).astype(dproj_ref.dtype)
            dk = _dot(ds, q4, "tn")
            dv = _dot(p, do4, "tn")
            dk_prev.append(dk_carry[h] + dk[0:WINDOW])
            dv_prev.append(dv_carry[h] + dv[0:WINDOW])
            dk_carry[h] = dk[WINDOW:2 * WINDOW]
            dv_carry[h] = dv[WINDOW:2 * WINDOW]
            sink4 = _per_head([sinks_ref[0, group * h + r] for r in range(group)])
            loss_sink = jnp.exp(sink4 - lse4) * delta4
            for r in range(group):
                dsinks = dsinks + jnp.where(head_lane == group * h + r,
                                            -jnp.sum(loss_sink[WINDOW * r:WINDOW * (r + 1)]), 0.0)
        _accumulate(dsinks_ref, dsinks, first)
        write_kv(dk_prev, dv_prev, cos_p, sgn_p)

        bg = pc_ref[:, dm.o_bg:dm.o_bg + cw]
        yc = yc_ref[...]
        dconv, dgc = _rms_bwd(dmc_ref[:, aw:d], bg * yc, gc_ref[...])
        _accumulate(dgc_ref, dgc, first)
        dproj_ref[:, dm.o_bg:dm.o_bg + cw] = (dconv * yc).astype(dproj_ref.dtype)
        dy = dconv * bg
        bg_n = pn_ref[:, dm.o_bg:dm.o_bg + cw]
        dconv_n, _ = _rms_bwd(dmn_ref[:, aw:d], bg_n * yn_ref[...], gc_ref[...])
        halo = jnp.where(has_next, dconv_n * bg_n, 0.0)
        dy1 = _shift_up(dy, halo, 1)
        dy2 = _shift_up(dy, halo, 2)
        dz = cw_ref[2:3, :] * dy + cw_ref[1:2, :] * dy1 + cw_ref[0:1, :] * dy2
        cg = pc_ref[:, dm.o_cg:dm.o_cg + cw]
        u = pc_ref[:, dm.o_u:dm.o_u + cw]
        dproj_ref[:, dm.o_cg:dm.o_cg + cw] = (dz * u).astype(dproj_ref.dtype)
        dproj_ref[:, dm.o_u:dm.o_u + cw] = (dz * cg).astype(dproj_ref.dtype)
        z = cg * u
        dcw = jnp.concatenate(
            [jnp.sum(z * t, axis=0, keepdims=True) for t in (dy2, dy1, dy)]
            + [jnp.zeros((V7X_SUBLANES - 3, cw), _F32)], axis=0)
        _accumulate(dcw_ref, dcw, first)

    at = lambda n: jnp.minimum(n, nb - 1)
    prev = lambda n: (jnp.maximum(at(n) - 1, 0), 0)
    cur = lambda n: (at(n), 0)
    done = lambda n: (jnp.maximum(n - 1, 0), 0)
    nxt8 = lambda n: (jnp.minimum((at(n) + 1) * (WINDOW // V7X_SUBLANES), s // V7X_SUBLANES - 1), 0)
    fixed = lambda n: (0, 0)
    blocks = WINDOW * (2 * inw * 4 + d * 4 + aw * 4 + cw * 4 + inw * 2 + 2 * KV_WIDTH * 2)
    carry = [pltpu.VMEM((N_KV_HEADS, WINDOW, 128), _F32), pltpu.VMEM((N_KV_HEADS, WINDOW, 128), _F32)]
    n_in, n_out = 15, 6
    ca = _CommArgs(list(comm), n_in, n_out)
    return pl.pallas_call(
        _carrying(body, n_in, n_out, nb + 1, ca, n_scratch=len(carry)), name="mixer_bwd", grid=(nb + 1,),
        in_specs=[pl.BlockSpec((WINDOW, inw), prev), pl.BlockSpec((WINDOW, inw), cur), pl.BlockSpec((V7X_SUBLANES, inw), nxt8),
                  pl.BlockSpec((WINDOW, 2 * V7X_LANES), prev), pl.BlockSpec((WINDOW, 2 * V7X_LANES), cur),
                  pl.BlockSpec((WINDOW, d), cur), pl.BlockSpec((V7X_SUBLANES, d), nxt8),
                  pl.BlockSpec((WINDOW, aw), cur), pl.BlockSpec((WINDOW, nq), cur),
                  pl.BlockSpec((WINDOW, cw), cur), pl.BlockSpec((V7X_SUBLANES, cw), nxt8),
                  pl.BlockSpec(memory_space=pltpu.SMEM),
                  pl.BlockSpec((1, aw), fixed), pl.BlockSpec((1, cw), fixed), pl.BlockSpec((V7X_SUBLANES, cw), fixed)]
        + [_ANY] * len(ca.operands),
        out_specs=[pl.BlockSpec((WINDOW, inw), cur), pl.BlockSpec((WINDOW, 2 * KV_WIDTH), done),
                   pl.BlockSpec((1, aw), fixed), pl.BlockSpec((1, cw), fixed),
                   pl.BlockSpec((1, nq), fixed), pl.BlockSpec((V7X_SUBLANES, cw), fixed)] + [_ANY] * len(ca.out_shape),
        out_shape=[jax.ShapeDtypeStruct((s, inw), _CDT), jax.ShapeDtypeStruct((s, 2 * KV_WIDTH), _CDT),
                   jax.ShapeDtypeStruct((1, aw), _F32), jax.ShapeDtypeStruct((1, cw), _F32),
                   jax.ShapeDtypeStruct((1, nq), _F32), jax.ShapeDtypeStruct((V7X_SUBLANES, cw), _F32)] + ca.out_shape,
        scratch_shapes=carry + ca.sems, input_output_aliases=ca.aliases,
        compiler_params=pltpu.CompilerParams(dimension_semantics=("arbitrary",), vmem_limit_bytes=_vmem_limit(blocks)),
    )(proj, proj, proj, rope, rope, dmixed, dmixed, attn, lse, y, y, sinks, g_attn, g_conv, conv_w8, *ca.operands)


def _position():
    return lax.axis_index("x"), lax.axis_index("y"), lax.axis_index("c")


def _linear(px, py, pc):
    return 4 * px + 2 * py + pc


def _comm_kernel(name, comm):
    ca = _CommArgs(list(comm), 0, 0)
    n_cin, n_cout = len(ca.operands), len(ca.out_shape)

    def body(*refs):
        cin, cout, sems = refs[:n_cin], refs[n_cin:n_cin + n_cout], refs[n_cin + n_cout:]
        ca.start(cin, cout, sems)
        ca.middle(cin, cout, sems)
        ca.finish(cin, cout, sems)

    return pl.pallas_call(
        body, name=name, out_shape=ca.out_shape, in_specs=[_ANY] * n_cin, out_specs=[_ANY] * n_cout,
        scratch_shapes=ca.sems, input_output_aliases=ca.aliases,
    )(*ca.operands)


def _gather_op(units):
    n = len(units)
    inputs, outputs, aliases = [], [], {}
    for shard, _, _, _ in units:
        inputs.append(shard)
        outputs.append(jax.ShapeDtypeStruct((N_DEV * shard.shape[0], shard.shape[1]), shard.dtype))
    for u, (_, buf, _, _) in enumerate(units):
        if buf is not None:
            aliases[len(inputs)] = u
            inputs.append(buf)

    def plan(ins, outs, sems, north):
        send_sems, recv_sems, local_sems = sems
        x, y, c = _position()
        me, sibling = (x, y, c), (x, y, 1 - c)
        xn, yn, dg = (1 - x, y), (x, 1 - y), (1 - x, 1 - y)
        via, to, k_via, k_other = (yn, xn, 2, 1) if north else (xn, yn, 1, 2)

        def rows(u, px, py, pc):
            shard, _, r0, r1 = units[u]
            return outs[u].at[pl.ds(pl.multiple_of(_linear(px, py, pc) * shard.shape[0] + r0, 16), r1 - r0), :]

        def own(u):
            _, _, r0, r1 = units[u]
            return ins[u].at[pl.ds(r0, r1 - r0), :]

        def copy(u, k, block, to_, src=None):
            return pltpu.make_async_remote_copy(
                src_ref=rows(u, *block) if src is None else src, dst_ref=rows(u, *block),
                send_sem=send_sems.at[u, k], recv_sem=recv_sems.at[u, k], device_id=to_, device_id_type=_MESH)

        us = range(n)
        return dict(
            mine=[pltpu.make_async_copy(own(u), rows(u, *me), local_sems.at[u]) for u in us],
            first=[cp for u in us for cp in (copy(u, 0, me, sibling, src=own(u)), copy(u, 1, me, (*xn, c), src=own(u)),
                                             copy(u, 2, me, (*yn, c), src=own(u)))],
            relay=[copy(u, 3, (*via, c), (*to, c)) for u in us],
            arrived={1: [copy(u, 1, (*xn, c), me) for u in us], 2: [copy(u, 2, (*yn, c), me) for u in us],
                     3: [copy(u, 3, (*dg, c), me) for u in us]},
            passed={1: [copy(u, 4, (*xn, c), sibling) for u in us], 2: [copy(u, 5, (*yn, c), sibling) for u in us],
                    3: [copy(u, 6, (*dg, c), sibling) for u in us]},
            rest=[cp for u in us for cp in (copy(u, 0, sibling, me), copy(u, 4, (*xn, 1 - c), me),
                                            copy(u, 5, (*yn, 1 - c), me), copy(u, 6, (*dg, 1 - c), me))],
            k_via=k_via, k_other=k_other)

    def land(p, k):
        for arrived, onward in zip(p["arrived"][k], p["passed"][k]):
            arrived.wait_recv()
            onward.start()

    def by_core(fn):
        c = lax.axis_index("c")
        for north in (True, False):
            pl.when(c == (1 if north else 0))(functools.partial(fn, north))

    def start(ins, outs, sems):
        p = plan(ins, outs, sems, True)
        for cp in p["mine"] + p["first"]:
            cp.start()

    def middle(ins, outs, sems):
        def go(north):
            p = plan(ins, outs, sems, north)
            land(p, p["k_via"])
            for cp in p["relay"]:
                cp.start()
        by_core(go)

    def finish(ins, outs, sems):
        def go(north):
            p = plan(ins, outs, sems, north)
            land(p, p["k_other"])
            land(p, 3)
            for cp in p["rest"]:
                cp.wait_recv()
            for cp in p["first"] + p["relay"] + [cp for k in (1, 2, 3) for cp in p["passed"][k]]:
                cp.wait_send()
            for cp in p["mine"]:
                cp.wait()
        by_core(go)

    sems = [pltpu.SemaphoreType.DMA((n, 7)), pltpu.SemaphoreType.DMA((n, 7)), pltpu.SemaphoreType.DMA((n,))]
    return _Comm(inputs, outputs, aliases, sems, start, finish, middle)


def _peers(x, y, c):
    out = []
    for k in range(1, N_DEV):
        fx, fy, fc = (k >> 2) & 1, (k >> 1) & 1, k & 1
        out.append((1 - x if fx else x, 1 - y if fy else y, 1 - c if fc else c))
    return out


def _exchange_op(partials):
    n = len(partials)
    outputs = [jax.ShapeDtypeStruct((4, p.shape[0] // N_DEV, p.shape[1]), p.dtype) for p in partials]

    def plan(ins, outs, sems):
        send_sems, recv_sems = sems
        x, y, c = _position()
        out = []
        for a in range(n):
            r = outs[a].shape[1]
            for ch in range(4):
                out.append(pltpu.make_async_remote_copy(
                    src_ref=ins[a].at[pl.ds(pl.multiple_of((2 * ch + 1 - c) * r, 16), r), :], dst_ref=outs[a].at[ch],
                    send_sem=send_sems.at[a, ch], recv_sem=recv_sems.at[a, ch], device_id=(x, y, 1 - c),
                    device_id_type=_MESH))
        return out

    def start(ins, outs, sems):
        for cp in plan(ins, outs, sems):
            cp.start()

    def finish(ins, outs, sems):
        copies = plan(ins, outs, sems)
        for cp in copies:
            cp.wait_recv()
        for cp in copies:
            cp.wait_send()

    sems = [pltpu.SemaphoreType.DMA((n, 4)), pltpu.SemaphoreType.DMA((n, 4))]
    return _Comm(list(partials), outputs, {}, sems, start, finish)


def _chip_send_op(units):
    n = len(units)
    inputs, outputs, aliases = [], [], {}
    for q, _, _, _ in units:
        inputs.append(q)
        outputs.append(jax.ShapeDtypeStruct(q.shape, q.dtype))
    for u, (_, buf, _, _) in enumerate(units):
        if buf is not None:
            aliases[len(inputs)] = u
            inputs.append(buf)

    def plan(ins, outs, sems):
        send_sems, recv_sems, local_sems = sems
        x, y, c = _position()
        my_chip = 2 * x + y
        chips = [(1 - x, y), (x, 1 - y), (1 - x, 1 - y)]
        mine, sends, arrivals = [], [], []
        for u, (_, _, r0, r1) in enumerate(units):
            span = pl.ds(r0, r1 - r0)
            mine.append(pltpu.make_async_copy(ins[u].at[my_chip, span, :], outs[u].at[my_chip, span, :], local_sems.at[u]))
            for k, (px, py) in enumerate(chips):
                sends.append(pltpu.make_async_remote_copy(
                    src_ref=ins[u].at[2 * px + py, span, :], dst_ref=outs[u].at[my_chip, span, :],
                    send_sem=send_sems.at[u, k], recv_sem=recv_sems.at[u, k], device_id=(px, py, c), device_id_type=_MESH))
                arrivals.append(pltpu.make_async_remote_copy(
                    src_ref=ins[u].at[my_chip, span, :], dst_ref=outs[u].at[2 * px + py, span, :],
                    send_sem=send_sems.at[u, k], recv_sem=recv_sems.at[u, k], device_id=(px, py, c), device_id_type=_MESH))
        return mine, sends, arrivals

    def start(ins, outs, sems):
        mine, sends, _ = plan(ins, outs, sems)
        for cp in mine + sends:
            cp.start()

    def finish(ins, outs, sems):
        mine, sends, arrivals = plan(ins, outs, sems)
        for cp in arrivals:
            cp.wait_recv()
        for cp in sends:
            cp.wait_send()
        for cp in mine:
            cp.wait()

    sems = [pltpu.SemaphoreType.DMA((n, 3)), pltpu.SemaphoreType.DMA((n, 3)), pltpu.SemaphoreType.DMA((n,))]
    return _Comm(inputs, outputs, aliases, sems, start, finish)


def _pair_sum(name, partial, received):
    _, rows, cols = received.shape
    tr = _pick(rows, (352, 288, 256, 128, 64, 32, 16))
    p4 = partial.reshape(4, 2, rows, cols)
    kind = jnp.reshape(lax.axis_index("c"), (1,)).astype(jnp.int32)

    def body(kind_ref, p_ref, r_ref, o_ref):
        o_ref[0] = (p_ref[0, 0].astype(_F32) + r_ref[0].astype(_F32)).astype(o_ref.dtype)

    return pl.pallas_call(
        body, name=name,
        grid_spec=pltpu.PrefetchScalarGridSpec(
            num_scalar_prefetch=1, grid=(4, rows // tr),
            in_specs=[pl.BlockSpec((1, 1, tr, cols), lambda ch, i, kind_ref: (ch, kind_ref[0], i, 0)),
                      pl.BlockSpec((1, tr, cols), lambda ch, i, kind_ref: (ch, i, 0))],
            out_specs=pl.BlockSpec((1, tr, cols), lambda ch, i, kind_ref: (ch, i, 0))),
        out_shape=jax.ShapeDtypeStruct(received.shape, received.dtype),
        compiler_params=pltpu.CompilerParams(dimension_semantics=("arbitrary", "arbitrary")),
    )(kind, p4, received)


def _all_reduce_small(name, v):
    rows = v.shape[0]

    def body(v_ref, out_ref, land_ref, send_sems, recv_sems):
        x, y, c = _position()
        me = _linear(x, y, c)
        peers = _peers(x, y, c)
        land_ref[me] = v_ref[...]
        sends = [pltpu.make_async_remote_copy(
            src_ref=v_ref, dst_ref=land_ref.at[me], send_sem=send_sems.at[k], recv_sem=recv_sems.at[k],
            device_id=peer, device_id_type=_MESH) for k, peer in enumerate(peers)]
        for cp in sends:
            cp.start()
        for k, peer in enumerate(peers):
            pltpu.make_async_remote_copy(
                src_ref=v_ref, dst_ref=land_ref.at[_linear(*peer)], send_sem=send_sems.at[k], recv_sem=recv_sems.at[k],
                device_id=peer, device_id_type=_MESH).wait_recv()
        for cp in sends:
            cp.wait_send()
        total = land_ref[0]
        for s in range(1, N_DEV):
            total = total + land_ref[s]
        out_ref[...] = total

    return pl.pallas_call(
        body, name=name, out_shape=jax.ShapeDtypeStruct(v.shape, _F32),
        in_specs=[pl.BlockSpec(memory_space=pltpu.VMEM)], out_specs=pl.BlockSpec(memory_space=pltpu.VMEM),
        scratch_shapes=[pltpu.VMEM((N_DEV, rows, V7X_LANES), _F32), pltpu.SemaphoreType.DMA((7,)), pltpu.SemaphoreType.DMA((7,))],
    )(v)


def _adamw(name, w, slots, m, v):
    rows, cols = w.shape
    n_slots = slots.shape[0]
    tr = _pick(rows, (176, 144, 128, 64, 32, 16, 8))

    def body(w_ref, s_ref, m_ref, v_ref, g_ref, d_ref, nm_ref, nv_ref):
        g = s_ref[0].astype(_F32)
        for k in range(1, n_slots):
            g = g + s_ref[k].astype(_F32)
        nm = ADAM_B1 * m_ref[...] + (1.0 - ADAM_B1) * g
        nv = ADAM_B2 * v_ref[...] + (1.0 - ADAM_B2) * (g * g)
        m_hat = nm / (1.0 - ADAM_B1 ** ADAM_STEP)
        v_hat = nv / (1.0 - ADAM_B2 ** ADAM_STEP)
        g_ref[...] = g
        d_ref[...] = -ADAM_LR * (m_hat / (jnp.sqrt(v_hat) + ADAM_EPS) + ADAM_WD * w_ref[...])
        nm_ref[...] = nm
        nv_ref[...] = nv

    spec = pl.BlockSpec((tr, cols), lambda i: (i, 0))
    blocks = 7 * tr * cols * 4 + _nbytes((n_slots, tr, cols), slots.dtype)
    return pl.pallas_call(
        body, name=name, grid=(rows // tr,),
        in_specs=[spec, pl.BlockSpec((n_slots, tr, cols), lambda i: (0, i, 0)), spec, spec], out_specs=[spec] * 4,
        out_shape=[jax.ShapeDtypeStruct((rows, cols), _F32)] * 4,
        compiler_params=pltpu.CompilerParams(dimension_semantics=("arbitrary",), vmem_limit_bytes=_vmem_limit(blocks)),
    )(w, slots, m, v)


def _pad_rows(a, rows):
    return jnp.pad(a, ((0, rows - a.shape[0]), (0, 0)))


def _pack(parts):
    rows, spans, at = [], [], 0
    for p in parts:
        p = p.reshape(-1)
        r = -(-p.shape[0] // V7X_LANES)
        rows.append(jnp.pad(p, (0, r * V7X_LANES - p.shape[0])).reshape(r, V7X_LANES))
        spans.append((at, r, p.shape[0]))
        at += r
    packed = jnp.concatenate(rows, axis=0)
    return _pad_rows(packed, -(-at // V7X_SUBLANES) * V7X_SUBLANES), spans


def _unpack(packed, spans, shapes):
    return [packed[at:at + r].reshape(-1)[:size].reshape(shape) for (at, r, size), shape in zip(spans, shapes)]


def kernel(x, positions, w_in, conv_w, sinks, g_attn, g_conv, w_out, ln1_g, ln1_b, w_gate, w_up, w_down, ln2_g, ln2_b, loss_target, m_w_in, m_conv_w, m_sinks, m_g_attn, m_g_conv, m_w_out, m_ln1_g, m_ln1_b, m_w_gate, m_w_up, m_w_down, m_ln2_g, m_ln2_b, v_w_in, v_conv_w, v_sinks, v_g_attn, v_g_conv, v_w_out, v_ln1_g, v_ln1_b, v_w_gate, v_w_up, v_w_down, v_ln2_g, v_ln2_b):
    _, s, d = x.shape
    d_ff = N_DEV * w_gate.shape[2]
    dm = _Dims(s, d, d_ff)
    aw, cw, nq, inw = dm.aw, dm.cw, dm.nq, dm.inw
    x2 = x[0]
    pos = positions[0].reshape(s, 1)
    inv_freq = ROPE_THETA ** (-jnp.arange(0, ROT_DIM, 2, dtype=_F32) / ROT_DIM)
    invf = jnp.tile(inv_freq, V7X_LANES // (ROT_DIM // 2)).reshape(1, V7X_LANES)

    conv_cols = conv_w.shape[2]
    sh_in, sh_out = w_in[0].T.astype(_CDT), w_out[0].astype(_CDT)
    sh_gate, sh_up, sh_down = w_gate[0].T.astype(_CDT), w_up[0].T.astype(_CDT), w_down[0].astype(_CDT)
    r_in, r_out, r_ff = sh_in.shape[0], sh_out.shape[0], sh_gate.shape[0]
    q_ff = r_ff // 4
    assert q_ff % 16 == 0
    def prepare_body(x_ref, pos_ref, invf_ref, xc_ref, rope_ref):
        xc_ref[...] = x_ref[...].astype(_CDT)
        cos, sgn = _rope_tables(pos_ref[...], invf_ref[...])
        rope_ref[:, 0:V7X_LANES] = cos
        rope_ref[:, V7X_LANES:2 * V7X_LANES] = sgn

    x_c, rope, w_in_t, conv_all = _row_kernel(
        "prepare_gather_w_in", prepare_body, [x2, pos], [invf], [((s, d), _CDT), ((s, 2 * V7X_LANES), _F32)], [],
        comm=[_gather_op([(sh_in, None, 0, r_in), (_pad_rows(conv_w[0], 16), None, 0, 16)])])
    conv_full = conv_all.reshape(N_DEV, 16, conv_cols)[:, :3, :].transpose(1, 0, 2).reshape(3, cw)
    conv_w8 = _pad_rows(conv_full, V7X_SUBLANES)

    tm = _pick(s, (1024, 512, 256, 128))
    tn_in = _pick(inw, (512, 256, 128))
    tn_ff = _pick(d_ff, (512, 256, 128))
    tr = _pick(s, (512, 256, 128))

    proj, w_out_f, w_gate_t = _matmul(
        "proj", [[(x_c, w_in_t, "nt")]], s, inw, d, tm, tn_in, d, [],
        [((s, inw), _F32, (tm, tn_in), _tile_ij)], _store_epilogue,
        comm=[_gather_op([(sh_out, None, 0, r_out), (sh_gate, None, 0, 2 * q_ff)])])
    mixed, attn, lse, y_conv, w_gate_t, w_up_t = _mixer_fwd(
        dm, proj, rope, sinks, g_attn, g_conv, conv_w8,
        comm=[_gather_op([(sh_gate, w_gate_t, 2 * q_ff, r_ff), (sh_up, None, 0, 2 * q_ff)])])

    def residual_epilogue(accs, ex, out, first):
        out[0][...] = DEEPNORM_ALPHA * ex[0][...] + accs[0]

    tn_d = _pick(d, (512,))
    r1, w_up_t = _matmul(
        "out_proj", [[(mixed, w_out_f, "nn")]], s, d, d, tm, tn_d, d, [(x2, (tm, tn_d), _tile_ij)],
        [((s, d), _F32, (tm, tn_d), _tile_ij)], residual_epilogue,
        comm=[_gather_op([(sh_up, w_up_t, 2 * q_ff, 3 * q_ff)])])
    h1, h1_c, xhat1, rstd1, w_up_t = _ln1_fwd_rows(
        r1, ln1_g, ln1_b, comm=[_gather_op([(sh_up, w_up_t, 3 * q_ff, r_ff)])])

    def swiglu_epilogue(accs, ex, out, first):
        gate_v, up_v = accs
        out[0][...] = gate_v
        out[1][...] = up_v
        out[2][...] = (gate_v * jax.nn.sigmoid(gate_v) * up_v).astype(_CDT)

    gate, up, act, w_down_f = _matmul(
        "gate_up", [[(h1_c, w_gate_t, "nt")], [(h1_c, w_up_t, "nt")]], s, d_ff, d, tm, tn_ff, d, [],
        [((s, d_ff), _F32, (tm, tn_ff), _tile_ij), ((s, d_ff), _F32, (tm, tn_ff), _tile_ij),
         ((s, d_ff), _CDT, (tm, tn_ff), _tile_ij)], swiglu_epilogue,
        comm=[_gather_op([(sh_down, None, 0, r_ff)])])

    (r2,) = _matmul("down", [[(act, w_down_f, "nn")]], s, d, d_ff, tm, tn_d, d_ff, [(h1, (tm, tn_d), _tile_ij)],
                    [((s, d), _F32, (tm, tn_d), _tile_ij)], residual_epilogue)
    dr2, dr2_c, loss_acc, d_ln2_g, d_ln2_b = _ln2_loss_bwd(r2, loss_target[0], ln2_g, ln2_b)

    def swiglu_bwd_epilogue(accs, ex, out, first):
        gate_v, up_v = ex[0][...], ex[1][...]
        sig = jax.nn.sigmoid(gate_v)
        out[0][...] = (accs[0] * up_v * (sig * (1.0 + gate_v * (1.0 - sig)))).astype(_CDT)
        out[1][...] = (accs[0] * (gate_v * sig)).astype(_CDT)

    dgate, dup = _matmul(
        "dact", [[(dr2_c, w_down_f, "nt")]], s, d_ff, d, tm, tn_ff, d,
        [(gate, (tm, tn_ff), _tile_ij), (up, (tm, tn_ff), _tile_ij)],
        [((s, d_ff), _CDT, (tm, tn_ff), _tile_ij), ((s, d_ff), _CDT, (tm, tn_ff), _tile_ij)], swiglu_bwd_epilogue)
    def weight_grad(name, a, b, comm=()):
        rows = a.shape[1]
        tw, tn_w = _pick(rows, (512, 256, 128)), _pick(d, (1024, 512))
        return _matmul(name, [[(a, b, "tn")]], rows, d, s, tw, tn_w, s, [],
                       [((rows, d), _CDT, (tw, tn_w), _tile_ij)], _store_epilogue, comm=comm, j_outer=True)

    (dw_down,) = weight_grad("dw_down", act, dr2_c)
    dw_gate_t, x_down = weight_grad("dw_gate", dgate, h1_c, comm=[_exchange_op([dw_down])])
    q_down = _pair_sum("chip_sum_w_down", dw_down, x_down)
    dw_up_t, l_down, x_gate = weight_grad(
        "dw_up", dup, h1_c, comm=[_chip_send_op([(q_down, None, 0, 2 * q_ff)]), _exchange_op([dw_gate_t])])
    q_gate = _pair_sum("chip_sum_w_gate", dw_gate_t, x_gate)

    tn_h = _pick(d, (512,))
    dh1, l_down, l_gate, x_up = _matmul(
        "dh1", [[(dgate, w_gate_t, "nn"), (dup, w_up_t, "nn")]], s, d, d_ff, tr, tn_h, d_ff,
        [(dr2, (tr, tn_h), _tile_ij)], [((s, d), _F32, (tr, tn_h), _tile_ij)], residual_epilogue,
        comm=[_chip_send_op([(q_down, l_down, 2 * q_ff, r_ff), (q_gate, None, 0, r_ff)]), _exchange_op([dw_up_t])])
    q_up = _pair_sum("chip_sum_w_up", dw_up_t, x_up)
    dr1, dr1_c, d_ln1_g, d_ln1_b = _ln1_bwd_rows(dh1, xhat1, rstd1, ln1_g)
    (dmixed,) = _matmul("dmixed", [[(dr1_c, w_out_f, "nt")]], s, d, d, tm, tn_d, d, [],
                        [((s, d), _F32, (tm, tn_d), _tile_ij)], _store_epilogue)
    (dw_out,) = weight_grad("dw_out", mixed, dr1_c)
    dproj, dkv, d_g_attn, d_g_conv, d_sinks, d_conv8, l_up, x_out = _mixer_bwd(
        dm, proj, rope, sinks, g_attn, g_conv, conv_w8, dmixed, attn, lse, y_conv,
        comm=[_chip_send_op([(q_up, None, 0, r_ff)]), _exchange_op([dw_out])])
    dproj = _patch_columns("dproj_kv", dproj, dkv, dm.o_k)
    q_out = _pair_sum("chip_sum_w_out", dw_out, x_out)
    dw_in_t, l_out = weight_grad("dw_in", dproj, x_c, comm=[_chip_send_op([(q_out, None, 0, r_out)])])
    (x_in,) = _comm_kernel("exchange_w_in", [_exchange_op([dw_in_t])])
    q_in = _pair_sum("chip_sum_w_in", dw_in_t, x_in)

    grad_x, l_in = _matmul("dx", [[(dproj, w_in_t, "nn")]], s, d, inw, tm, tn_d, inw,
                           [(dr1, (tm, tn_d), _tile_ij)], [((s, d), _F32, (tm, tn_d), _tile_ij)], residual_epilogue,
                           comm=[_chip_send_op([(q_in, None, 0, r_in)])])

    small_parts = [d_conv8[:3], d_sinks, d_g_attn, d_g_conv, d_ln1_g, d_ln1_b, d_ln2_g, d_ln2_b, loss_acc[0:1, 0:1]]
    packed, spans = _pack(small_parts)
    reduced = _unpack(_all_reduce_small("reduce_small", packed), spans, [p.shape for p in small_parts])
    g_conv_full, g_sinks, g_g_attn, g_g_conv, g_ln1_g, g_ln1_b, g_ln2_g, g_ln2_b, loss_sum = reduced
    me = _linear(*_position())
    g_conv_w = lax.dynamic_slice(g_conv_full, (0, me * conv_cols), (3, conv_cols))
    loss = loss_sum[0, 0]

    big = {"w_in": (w_in[0].T, l_in, m_w_in[0].T, v_w_in[0].T), "w_out": (w_out[0], l_out, m_w_out[0], v_w_out[0]),
           "w_gate": (w_gate[0].T, l_gate, m_w_gate[0].T, v_w_gate[0].T),
           "w_up": (w_up[0].T, l_up, m_w_up[0].T, v_w_up[0].T), "w_down": (w_down[0], l_down, m_w_down[0], v_w_down[0])}
    res = {nm: tuple(_adamw(f"adamw_{nm}", w, slots, m, v)) for nm, (w, slots, m, v) in big.items()}
    for nm in ("w_in", "w_gate", "w_up"):
        res[nm] = tuple(a.T for a in res[nm])
    small_names = ["conv_w", "sinks", "g_attn", "g_conv", "ln1_g", "ln1_b", "ln2_g", "ln2_b"]
    small_w = [conv_w, sinks, g_attn, g_conv, ln1_g, ln1_b, ln2_g, ln2_b]
    small_g = [g_conv_w[None], g_sinks, g_g_attn, g_g_conv, g_ln1_g, g_ln1_b, g_ln2_g, g_ln2_b]
    small_m = [m_conv_w, m_sinks, m_g_attn, m_g_conv, m_ln1_g, m_ln1_b, m_ln2_g, m_ln2_b]
    small_v = [v_conv_w, v_sinks, v_g_attn, v_g_conv, v_ln1_g, v_ln1_b, v_ln2_g, v_ln2_b]
    pw, sp = _pack(small_w)
    pg, _ = _pack(small_g)
    pm, _ = _pack(small_m)
    pv, _ = _pack(small_v)
    shapes = [w.shape for w in small_w]
    _, sd, sm, sv = [_unpack(p, sp, shapes) for p in _adamw("adamw_small", pw, pg[None], pm, pv)]
    for i, nm in enumerate(small_names):
        res[nm] = (small_g[i].reshape(shapes[i]), sd[i], sm[i], sv[i])

    order = ["w_in", "conv_w", "sinks", "g_attn", "g_conv", "w_out", "ln1_g", "ln1_b", "w_gate", "w_up", "w_down", "ln2_g", "ln2_b"]

    def lead(a, nm):
        return a[None] if nm in big else a

    return (loss, grad_x[None],
            *[lead(res[nm][0], nm) for nm in order], *[lead(res[nm][1], nm) for nm in order],
            *[lead(res[nm][2], nm) for nm in order], *[lead(res[nm][3], nm) for nm in order])
```

```python
import functools

import jax
import jax.numpy as jnp
from jax import lax
from jax.experimental import pallas as pl
from jax.experimental.pallas import tpu as pltpu

_F32 = jnp.float32
_CDT = jnp.bfloat16

HEAD_DIM = 64
WINDOW = 128
N_KV_HEADS = 4
KV_WIDTH = N_KV_HEADS * HEAD_DIM
ROT_DIM = HEAD_DIM // 4
ROPE_THETA = 500000.0
ATTN_SCALE = HEAD_DIM ** -0.5
DEPTH = 1
DEEPNORM_ALPHA = (2 * DEPTH) ** 0.25
LN_EPS = 1e-5
RMS_EPS = 1e-6
ADAM_LR = 0.001
ADAM_B1 = 0.9
ADAM_B2 = 0.999
ADAM_EPS = 1e-08
ADAM_WD = 0.01
ADAM_STEP = 10
N_DEV = 8
MASKED = -1e30

MIB = 1024 * 1024
V7X_VMEM_BYTES = 64 * MIB
V7X_LANES = 128
V7X_SUBLANES = 8
BODY_TEMPORARIES_BYTES = 16 * MIB
VMEM_LIMIT_FLOOR_BYTES = 32 * MIB
VMEM_LIMIT_CEILING_BYTES = V7X_VMEM_BYTES - 8 * MIB
_MESH = pl.DeviceIdType.MESH
_ANY = pl.BlockSpec(memory_space=pl.ANY)


def _vmem_limit(block_bytes, scratch_bytes=0):
    want = 2 * block_bytes + scratch_bytes + BODY_TEMPORARIES_BYTES
    return int(min(max(want, VMEM_LIMIT_FLOOR_BYTES), VMEM_LIMIT_CEILING_BYTES))


def _nbytes(shape, dtype):
    n = 1
    for s in shape:
        n *= s
    return n * jnp.dtype(dtype).itemsize


def _pick(n, candidates):
    for c in candidates:
        if n % c == 0:
            return c
    raise ValueError(f"no tile of {candidates} divides {n}")


_DOT_DIMS = {"nn": ((1,), (0,)), "nt": ((1,), (1,)), "tn": ((0,), (0,))}


def _dot(a, b, mode):
    return lax.dot_general(a.astype(_CDT), b.astype(_CDT), (_DOT_DIMS[mode], ((), ())),
                           preferred_element_type=_F32)


def _accumulate(ref, val, first):
    @pl.when(first)
    def _():
        ref[...] = val

    @pl.when(jnp.logical_not(first))
    def _():
        ref[...] += val


class _Comm:
    def __init__(self, inputs, outputs, aliases, sems, start, finish, middle=None):
        self.inputs, self.outputs, self.aliases, self.sems = inputs, outputs, aliases, sems
        self.start, self.finish, self.middle = start, finish, middle


def _middle_step(n_steps):
    return (2 * n_steps) // 3


class _CommArgs:
    def __init__(self, comms, n_in_before, n_out_before):
        self.comms, self.operands, self.out_shape, self.aliases, self.sems, self.at = comms, [], [], {}, [], []
        for cm in comms:
            self.at.append((len(self.operands), len(self.out_shape), len(self.sems)))
            for i_in, i_out in cm.aliases.items():
                self.aliases[n_in_before + len(self.operands) + i_in] = n_out_before + len(self.out_shape) + i_out
            self.operands += cm.inputs
            self.out_shape += cm.outputs
            self.sems += cm.sems

    def _each(self, in_refs, out_refs, sem_refs):
        for cm, (i0, o0, s0) in zip(self.comms, self.at):
            yield cm, (in_refs[i0:i0 + len(cm.inputs)], out_refs[o0:o0 + len(cm.outputs)], sem_refs[s0:s0 + len(cm.sems)])

    def start(self, in_refs, out_refs, sem_refs):
        for cm, refs in self._each(in_refs, out_refs, sem_refs):
            cm.start(*refs)

    def finish(self, in_refs, out_refs, sem_refs):
        for cm, refs in self._each(in_refs, out_refs, sem_refs):
            cm.finish(*refs)

    @property
    def has_middle(self):
        return any(cm.middle is not None for cm in self.comms)

    def middle(self, in_refs, out_refs, sem_refs):
        for cm, refs in self._each(in_refs, out_refs, sem_refs):
            if cm.middle is not None:
                cm.middle(*refs)


def _matmul(name, groups, m, n, k, tm, tn, tk, extras, outs, epilogue, comm=(), j_outer=False):
    assert m % tm == 0 and n % tn == 0 and k % tk == 0, (name, m, n, k, tm, tn, tk)
    nk = k // tk
    terms = [t for g in groups for t in g]
    operands, in_specs, block_bytes = [], [], 0

    def spec(blk, imap):
        return pl.BlockSpec(blk, (lambda g0, g1, kk: imap(g1, g0, kk)) if j_outer else imap)

    for a, b, mode in terms:
        assert a.shape == ((k, m) if mode == "tn" else (m, k)), (name, a.shape, mode)
        assert b.shape == ((n, k) if mode == "nt" else (k, n)), (name, b.shape, mode)
        if mode == "tn":
            a_blk, a_map = (tk, tm), (lambda i, j, kk: (kk, i))
        else:
            a_blk, a_map = (tm, tk), (lambda i, j, kk: (i, kk))
        if mode == "nt":
            b_blk, b_map = (tn, tk), (lambda i, j, kk: (j, kk))
        else:
            b_blk, b_map = (tk, tn), (lambda i, j, kk: (kk, j))
        operands += [a, b]
        in_specs += [spec(a_blk, a_map), spec(b_blk, b_map)]
        block_bytes += _nbytes(a_blk, a.dtype) + _nbytes(b_blk, b.dtype)
    for arr, blk, imap in extras:
        operands.append(arr)
        in_specs.append(spec(blk, lambda i, j, kk, imap=imap: imap(i, j)))
        block_bytes += _nbytes(blk, arr.dtype)
    out_shape, out_specs = [], []
    for shape, dtype, blk, imap in outs:
        out_shape.append(jax.ShapeDtypeStruct(shape, dtype))
        out_specs.append(spec(blk, lambda i, j, kk, imap=imap: imap(i, j)))
        block_bytes += _nbytes(blk, dtype)
    n_terms, n_extra, n_out, n_groups = len(terms), len(extras), len(outs), len(groups)
    scratch = [pltpu.VMEM((tm, tn), _F32) for _ in range(n_groups)] if nk > 1 else []
    ca = _CommArgs(list(comm), len(operands), n_out)
    n_cin, n_cout, n_acc = len(ca.operands), len(ca.out_shape), len(scratch)
    tiles = (m // tm, n // tn)
    grid = (tiles[1], tiles[0], nk) if j_outer else (tiles[0], tiles[1], nk)

    def body(*refs):
        refs = list(refs)
        term_refs = [refs.pop(0) for _ in range(2 * n_terms)]
        extra_refs = [refs.pop(0) for _ in range(n_extra)]
        cin_refs = [refs.pop(0) for _ in range(n_cin)]
        out_refs = [refs.pop(0) for _ in range(n_out)]
        cout_refs = [refs.pop(0) for _ in range(n_cout)]
        acc_refs = [refs.pop(0) for _ in range(n_acc)]
        sem_refs = refs
        g0, g1, kk = pl.program_id(0), pl.program_id(1), pl.program_id(2)
        first = jnp.logical_and(g0 == 0, g1 == 0)
        if comm:
            @pl.when(jnp.logical_and(first, kk == 0))
            def _():
                ca.start(cin_refs, cout_refs, sem_refs)
        if ca.has_middle:
            step = (g0 * grid[1] + g1) * nk + kk

            @pl.when(step == _middle_step(grid[0] * grid[1] * nk))
            def _():
                ca.middle(cin_refs, cout_refs, sem_refs)
        partial, t = [], 0
        for g in groups:
            s = None
            for _, _, mode in g:
                d = _dot(term_refs[2 * t][...], term_refs[2 * t + 1][...], mode)
                s = d if s is None else s + d
                t += 1
            partial.append(s)
        if nk == 1:
            epilogue(partial, extra_refs, out_refs, first)
        else:
            for acc, p in zip(acc_refs, partial):
                _accumulate(acc, p, kk == 0)

            @pl.when(kk == nk - 1)
            def _():
                epilogue([acc[...] for acc in acc_refs], extra_refs, out_refs, first)
        if comm:
            @pl.when(jnp.logical_and(jnp.logical_and(g0 == grid[0] - 1, g1 == grid[1] - 1), kk == nk - 1))
            def _():
                ca.finish(cin_refs, cout_refs, sem_refs)

    res = pl.pallas_call(
        body, name=name, grid=grid,
        in_specs=in_specs + [_ANY] * n_cin, out_specs=out_specs + [_ANY] * n_cout,
        out_shape=out_shape + ca.out_shape, scratch_shapes=scratch + ca.sems, input_output_aliases=ca.aliases,
        compiler_params=pltpu.CompilerParams(
            dimension_semantics=("arbitrary", "arbitrary", "arbitrary"),
            vmem_limit_bytes=_vmem_limit(block_bytes, n_groups * tm * tn * 4 if nk > 1 else 0)),
    )(*operands, *ca.operands)
    return list(res[:n_out]) + list(res[n_out:])


def _store_epilogue(accs, extra_refs, out_refs, first):
    for acc, ref in zip(accs, out_refs):
        ref[...] = acc.astype(ref.dtype)


def _tile_ij(i, j):
    return (i, j)


def _row_i(i, j):
    return (i, 0)


def _whole(i, j):
    return (0, 0)


def _mean(v):
    return jnp.mean(v, axis=-1, keepdims=True)


def _ln_fwd(r, g, b):
    xc = r - _mean(r)
    rstd = lax.rsqrt(_mean(xc * xc) + LN_EPS)
    xhat = xc * rstd
    return xhat * g + b, xhat, rstd


def _ln_bwd(dy, xhat, rstd, g):
    dxh = dy * g
    dr = rstd * (dxh - _mean(dxh) - xhat * _mean(dxh * xhat))
    return dr, jnp.sum(dy * xhat, axis=0, keepdims=True), jnp.sum(dy, axis=0, keepdims=True)


def _rms_fwd(a, g):
    rstd = lax.rsqrt(_mean(a * a) + RMS_EPS)
    return a * rstd * g


def _rms_bwd(dm, a, g):
    rstd = lax.rsqrt(_mean(a * a) + RMS_EPS)
    nhat = a * rstd
    dn = dm * g
    da = rstd * (dn - nhat * _mean(dn * nhat))
    return da, jnp.sum(dm * nhat, axis=0, keepdims=True)


def _lane(shape):
    return lax.broadcasted_iota(jnp.int32, shape, 1)


def _row(shape):
    return lax.broadcasted_iota(jnp.int32, shape, 0)


def _rope_tables(pos, invf):
    ang = pos.astype(_F32) * invf
    lane = _lane(ang.shape)
    in_rot = (lane % HEAD_DIM) < ROT_DIM
    first = (lane % ROT_DIM) < ROT_DIM // 2
    cos = jnp.where(in_rot, jnp.cos(ang), 1.0)
    sin = jnp.sin(ang)
    sgn = jnp.where(in_rot, jnp.where(first, -sin, sin), 0.0)
    return cos, sgn


def _rope(t, cos, sgn, sign):
    half = ROT_DIM // 2
    first = (_lane(t.shape) % ROT_DIM) < half
    partner = jnp.where(first, pltpu.roll(t, V7X_LANES - half, 1), pltpu.roll(t, half, 1))
    return t * cos + partner * (sgn * sign)


def _dup_head(t, h):
    g = t[:, 128 * (h // 2):128 * (h // 2) + 128]
    r = pltpu.roll(g, HEAD_DIM, 1)
    lo = _lane(g.shape) < HEAD_DIM
    return jnp.where(lo, g, r) if h % 2 == 0 else jnp.where(lo, r, g)


def _fold_halves(t):
    return t + pltpu.roll(t, HEAD_DIM, 1)


def _halves(t):
    lo = _lane(t.shape) < HEAD_DIM
    zero = jnp.zeros_like(t)
    return jnp.where(lo, t, zero), jnp.where(lo, zero, t)


def _band_mask(n_heads, n_keys, first_block):
    shape = (n_heads * WINDOW, n_keys)
    i = jnp.bitwise_and(_row(shape), WINDOW - 1)
    j = _lane(shape)
    valid = jnp.logical_and(j >= i + 1, j <= i + WINDOW)
    if first_block is not None:
        valid = jnp.logical_and(valid, jnp.logical_or(j >= WINDOW, jnp.logical_not(first_block)))
    return valid


def _stack_heads(pairs):
    return jnp.concatenate([half for t in pairs for half in _halves(t)], axis=0).astype(_CDT)


def _unstack_heads(t, n_pairs):
    lo = _lane((WINDOW, 128)) < HEAD_DIM
    return [jnp.where(lo, t[2 * WINDOW * i:2 * WINDOW * i + WINDOW], t[2 * WINDOW * i + WINDOW:2 * WINDOW * (i + 1)])
            for i in range(n_pairs)]


def _per_head(values):
    n_rows = len(values) * WINDOW
    block = jnp.right_shift(_row((n_rows, 1)), WINDOW.bit_length() - 1)
    out = jnp.zeros((n_rows, 1), _F32)
    for k, v in enumerate(values):
        out = jnp.where(block == k, v, out)
    return out


def _shift_down(z, halo, k):
    out = pltpu.roll(z, k, 0)
    r = _row(z.shape)
    for t in range(k):
        out = jnp.where(r == t, halo[V7X_SUBLANES - k + t:V7X_SUBLANES - k + t + 1, :], out)
    return out


def _shift_up(z, halo, k):
    rows = z.shape[0]
    out = pltpu.roll(z, rows - k, 0)
    r = _row(z.shape)
    for t in range(k):
        out = jnp.where(r == rows - k + t, halo[t:t + 1, :], out)
    return out


class _Dims:
    def __init__(self, s, d, d_ff):
        self.s, self.d, self.d_ff = s, d, d_ff
        self.aw = d // 2
        self.cw = d - self.aw
        self.nq = self.aw // HEAD_DIM
        self.group = self.nq // N_KV_HEADS
        assert self.group % 2 == 0, "a 128-lane pair of query heads must share its kv head"
        self.inw = self.aw + 2 * KV_WIDTH + 3 * self.cw
        self.o_k = self.aw
        self.o_v = self.aw + KV_WIDTH
        self.o_cg = self.aw + 2 * KV_WIDTH
        self.o_bg = self.o_cg + self.cw
        self.o_u = self.o_bg + self.cw
        self.nb = s // WINDOW
        assert s % WINDOW == 0


def _carrying(body, n_in, n_out, n_steps, ca, n_scratch=0):
    n_cin, n_cout = len(ca.operands), len(ca.out_shape)

    def wrapped(*refs):
        refs = list(refs)
        in_refs = [refs.pop(0) for _ in range(n_in)]
        cin_refs = [refs.pop(0) for _ in range(n_cin)]
        out_refs = [refs.pop(0) for _ in range(n_out)]
        cout_refs = [refs.pop(0) for _ in range(n_cout)]
        scratch_refs = [refs.pop(0) for _ in range(n_scratch)]
        if ca.comms:
            @pl.when(pl.program_id(0) == 0)
            def _():
                ca.start(cin_refs, cout_refs, refs)
        if ca.has_middle:
            @pl.when(pl.program_id(0) == _middle_step(n_steps))
            def _():
                ca.middle(cin_refs, cout_refs, refs)
        body(*in_refs, *out_refs, *scratch_refs)
        if ca.comms:
            @pl.when(pl.program_id(0) == n_steps - 1)
            def _():
                ca.finish(cin_refs, cout_refs, refs)

    return wrapped


def _row_kernel(name, body, rows_in, vecs_in, rows_out, vecs_out, comm=()):
    s = rows_in[0].shape[0]
    tr = _pick(s, (256, 128))
    row = lambda a: pl.BlockSpec((tr, a[1] if isinstance(a, tuple) else a.shape[1]), lambda i: (i, 0))
    vec = lambda shape: pl.BlockSpec(tuple(shape), lambda i: (0, 0))
    n_in, n_out = len(rows_in) + len(vecs_in), len(rows_out) + len(vecs_out)
    ca = _CommArgs(list(comm), n_in, n_out)
    blocks = sum(_nbytes((tr, a.shape[1]), a.dtype) for a in rows_in) + sum(_nbytes((tr, sh[1]), dt) for sh, dt in rows_out)
    res = pl.pallas_call(
        _carrying(body, n_in, n_out, s // tr, ca), name=name, grid=(s // tr,),
        in_specs=[row(a) for a in rows_in] + [vec(v.shape) for v in vecs_in] + [_ANY] * len(ca.operands),
        out_specs=[row(sh) for sh, _ in rows_out] + [vec(sh) for sh, _ in vecs_out] + [_ANY] * len(ca.out_shape),
        out_shape=[jax.ShapeDtypeStruct(sh, dt) for sh, dt in list(rows_out) + list(vecs_out)] + ca.out_shape,
        scratch_shapes=ca.sems, input_output_aliases=ca.aliases,
        compiler_params=pltpu.CompilerParams(dimension_semantics=("arbitrary",), vmem_limit_bytes=_vmem_limit(blocks)),
    )(*rows_in, *vecs_in, *ca.operands)
    return list(res)


def _ln2_loss_bwd(r2, target, gain, bias, comm=()):
    s, d = r2.shape

    def body(r_ref, t_ref, g_ref, b_ref, dr_ref, drc_ref, loss_ref, dg_ref, db_ref):
        first = pl.program_id(0) == 0
        yv, xhat, rstd = _ln_fwd(r_ref[...], g_ref[...], b_ref[...])
        err = yv - t_ref[...]
        dr2, dg, db = _ln_bwd(err * (1.0 / d), xhat, rstd, g_ref[...])
        dr_ref[...] = dr2
        drc_ref[...] = dr2.astype(_CDT)
        _accumulate(loss_ref, jnp.zeros(loss_ref.shape, _F32) + 0.5 * jnp.sum(err * err) * (1.0 / d), first)
        _accumulate(dg_ref, dg, first)
        _accumulate(db_ref, db, first)

    return _row_kernel("ln2_loss_bwd", body, [r2, target], [gain, bias], [((s, d), _F32), ((s, d), _CDT)],
                       [((V7X_SUBLANES, V7X_LANES), _F32), ((1, d), _F32), ((1, d), _F32)], comm)


def _ln1_fwd_rows(r1, gain, bias, comm=()):
    s, d = r1.shape

    def body(r_ref, g_ref, b_ref, h_ref, hc_ref, xhat_ref, rstd_ref):
        h1, xhat, rstd = _ln_fwd(r_ref[...], g_ref[...], b_ref[...])
        h_ref[...] = h1
        hc_ref[...] = h1.astype(_CDT)
        xhat_ref[...] = xhat
        rstd_ref[...] = rstd

    return _row_kernel("ln1", body, [r1], [gain, bias],
                       [((s, d), _F32), ((s, d), _CDT), ((s, d), _F32), ((s, 1), _F32)], [], comm)


def _ln1_bwd_rows(dh1, xhat, rstd, gain, comm=()):
    s, d = dh1.shape

    def body(dh_ref, xhat_ref, rstd_ref, g_ref, dr_ref, drc_ref, dg_ref, db_ref):
        first = pl.program_id(0) == 0
        dr1, dg, db = _ln_bwd(dh_ref[...], xhat_ref[...], rstd_ref[...], g_ref[...])
        dr_ref[...] = dr1
        drc_ref[...] = dr1.astype(_CDT)
        _accumulate(dg_ref, dg, first)
        _accumulate(db_ref, db, first)

    return _row_kernel("ln1_bwd", body, [dh1, xhat, rstd], [gain], [((s, d), _F32), ((s, d), _CDT)],
                       [((1, d), _F32), ((1, d), _F32)], comm)


def _mixer_fwd(dm, proj, rope, sinks, g_attn, g_conv, conv_w8, comm=()):
    s, d, aw, cw, nq, inw, nb = dm.s, dm.d, dm.aw, dm.cw, dm.nq, dm.inw, dm.nb

    def body(pp_ref, pc_ref, ropep_ref, ropec_ref, sinks_ref, ga_ref, gc_ref, cw_ref,
             mixed_ref, attn_ref, lse_ref, y_ref):
        n = pl.program_id(0)
        cos_c, sgn_c = ropec_ref[:, 0:V7X_LANES], ropec_ref[:, V7X_LANES:2 * V7X_LANES]
        cos_p, sgn_p = ropep_ref[:, 0:V7X_LANES], ropep_ref[:, V7X_LANES:2 * V7X_LANES]
        kk = jnp.concatenate(
            [jnp.concatenate([_rope(ref[:, dm.o_k + 128 * g:dm.o_k + 128 * g + 128], c, sg, 1.0)
                              for g in range(KV_WIDTH // 128)], axis=1)
             for ref, c, sg in ((pp_ref, cos_p, sgn_p), (pc_ref, cos_c, sgn_c))], axis=0)
        vv = jnp.concatenate([pp_ref[:, dm.o_v:dm.o_v + KV_WIDTH], pc_ref[:, dm.o_v:dm.o_v + KV_WIDTH]], axis=0)
        group, pairs = dm.group, dm.group // 2
        valid = _band_mask(group, 2 * WINDOW, n == 0)
        for h in range(N_KV_HEADS):
            k2, v2 = _dup_head(kk, h).astype(_CDT), _dup_head(vv, h).astype(_CDT)
            q4 = _stack_heads([_rope(pc_ref[:, 128 * j:128 * j + 128], cos_c, sgn_c, 1.0)
                               for j in range(pairs * h, pairs * (h + 1))])
            sc = jnp.where(valid, _dot(q4, k2, "nt") * ATTN_SCALE, MASKED)
            sink = _per_head([sinks_ref[0, group * h + r] for r in range(group)])
            mx = jnp.maximum(jnp.max(sc, axis=1, keepdims=True), sink)
            p = jnp.exp(sc - mx)
            den = jnp.sum(p, axis=1, keepdims=True) + jnp.exp(sink - mx)
            out = _unstack_heads(_dot(p / den, v2, "nn"), pairs)
            lse = mx + jnp.log(den)
            for r in range(group):
                lse_ref[:, group * h + r:group * h + r + 1] = lse[WINDOW * r:WINDOW * (r + 1)]
            for i in range(pairs):
                j = pairs * h + i
                attn_ref[:, 128 * j:128 * j + 128] = out[i]
        mixed_ref[:, 0:aw] = _rms_fwd(attn_ref[...], ga_ref[...]).astype(mixed_ref.dtype)

        z = pc_ref[:, dm.o_cg:dm.o_cg + cw] * pc_ref[:, dm.o_u:dm.o_u + cw]
        top = WINDOW - V7X_SUBLANES
        halo = pp_ref[top:WINDOW, dm.o_cg:dm.o_cg + cw] * pp_ref[top:WINDOW, dm.o_u:dm.o_u + cw]
        halo = jnp.where(n == 0, jnp.zeros_like(halo), halo)
        y = cw_ref[0:1, :] * _shift_down(z, halo, 2) + cw_ref[1:2, :] * _shift_down(z, halo, 1) + cw_ref[2:3, :] * z
        y_ref[...] = y
        conv = pc_ref[:, dm.o_bg:dm.o_bg + cw] * y
        mixed_ref[:, aw:d] = _rms_fwd(conv, gc_ref[...]).astype(mixed_ref.dtype)

    prev = lambda n: (jnp.maximum(n - 1, 0), 0)
    cur = lambda n: (n, 0)
    fixed = lambda n: (0, 0)
    blocks = 2 * WINDOW * inw * 4 + WINDOW * (d * 2 + aw * 4 + cw * 4 + nq * 4)
    ca = _CommArgs(list(comm), 8, 4)
    return pl.pallas_call(
        _carrying(body, 8, 4, nb, ca), name="mixer_fwd", grid=(nb,),
        in_specs=[pl.BlockSpec((WINDOW, inw), prev), pl.BlockSpec((WINDOW, inw), cur),
                  pl.BlockSpec((WINDOW, 2 * V7X_LANES), prev), pl.BlockSpec((WINDOW, 2 * V7X_LANES), cur),
                  pl.BlockSpec(memory_space=pltpu.SMEM),
                  pl.BlockSpec((1, aw), fixed), pl.BlockSpec((1, cw), fixed), pl.BlockSpec((V7X_SUBLANES, cw), fixed)]
        + [_ANY] * len(ca.operands),
        out_specs=[pl.BlockSpec((WINDOW, d), cur), pl.BlockSpec((WINDOW, aw), cur),
                   pl.BlockSpec((WINDOW, nq), cur), pl.BlockSpec((WINDOW, cw), cur)] + [_ANY] * len(ca.out_shape),
        out_shape=[jax.ShapeDtypeStruct((s, d), _CDT), jax.ShapeDtypeStruct((s, aw), _F32),
                   jax.ShapeDtypeStruct((s, nq), _F32), jax.ShapeDtypeStruct((s, cw), _F32)] + ca.out_shape,
        scratch_shapes=ca.sems, input_output_aliases=ca.aliases,
        compiler_params=pltpu.CompilerParams(dimension_semantics=("arbitrary",), vmem_limit_bytes=_vmem_limit(blocks)),
    )(proj, proj, rope, rope, sinks, g_attn, g_conv, conv_w8, *ca.operands)


def _patch_columns(name, a, part, offset):
    s, pw = part.shape
    assert offset % pw == 0 and pw % V7X_LANES == 0
    tr = _pick(s, (512, 256, 128))

    def body(a_ref, p_ref, o_ref):
        del a_ref
        o_ref[...] = p_ref[...]

    return pl.pallas_call(
        body, name=name, grid=(s // tr,),
        in_specs=[_ANY, pl.BlockSpec((tr, pw), lambda i: (i, 0))],
        out_specs=pl.BlockSpec((tr, pw), lambda i: (i, offset // pw)),
        out_shape=jax.ShapeDtypeStruct(a.shape, a.dtype), input_output_aliases={0: 0},
        compiler_params=pltpu.CompilerParams(dimension_semantics=("arbitrary",)),
    )(a, part)


def _mixer_bwd(dm, proj, rope, sinks, g_attn, g_conv, conv_w8, dmixed, attn, lse, y, comm=()):
    s, d, aw, cw, nq, inw, nb = dm.s, dm.d, dm.aw, dm.cw, dm.nq, dm.inw, dm.nb

    def body(pp_ref, pc_ref, pn_ref, ropep_ref, ropec_ref, dmc_ref, dmn_ref, ac_ref,
             lsec_ref, yc_ref, yn_ref, sinks_ref, ga_ref, gc_ref, cw_ref,
             dproj_ref, dkv_ref, dga_ref, dgc_ref, dsinks_ref, dcw_ref, dk_carry, dv_carry):
        n = pl.program_id(0)
        first = n == 0
        live = n < nb
        has_next = n < nb - 1
        cos_p, sgn_p = ropep_ref[:, 0:V7X_LANES], ropep_ref[:, V7X_LANES:2 * V7X_LANES]
        cos_c, sgn_c = ropec_ref[:, 0:V7X_LANES], ropec_ref[:, V7X_LANES:2 * V7X_LANES]

        @pl.when(first)
        def _():
            dk_carry[...] = jnp.zeros(dk_carry.shape, _F32)
            dv_carry[...] = jnp.zeros(dv_carry.shape, _F32)

        def write_kv(dk2, dv2, cos, sgn):
            lo = _lane((WINDOW, 128)) < HEAD_DIM
            for g in range(KV_WIDTH // 128):
                dk = jnp.where(lo, _fold_halves(dk2[2 * g]), _fold_halves(dk2[2 * g + 1]))
                dv = jnp.where(lo, _fold_halves(dv2[2 * g]), _fold_halves(dv2[2 * g + 1]))
                dkv_ref[:, 128 * g:128 * g + 128] = _rope(dk, cos, sgn, -1.0).astype(dkv_ref.dtype)
                dkv_ref[:, KV_WIDTH + 128 * g:KV_WIDTH + 128 * g + 128] = dv.astype(dkv_ref.dtype)

        @pl.when(jnp.logical_not(live))
        def _():
            write_kv([dk_carry[h] for h in range(N_KV_HEADS)], [dv_carry[h] for h in range(N_KV_HEADS)], cos_c, sgn_c)

        @pl.when(live)
        def _():
            block_step(pp_ref, pc_ref, pn_ref, dmc_ref, dmn_ref, ac_ref, lsec_ref, yc_ref, yn_ref, sinks_ref, ga_ref,
                       gc_ref, cw_ref, dproj_ref, dga_ref, dgc_ref, dsinks_ref, dcw_ref, dk_carry, dv_carry,
                       first, has_next, cos_p, sgn_p, cos_c, sgn_c, write_kv)

    def block_step(pp_ref, pc_ref, pn_ref, dmc_ref, dmn_ref, ac_ref, lsec_ref, yc_ref, yn_ref, sinks_ref, ga_ref,
                   gc_ref, cw_ref, dproj_ref, dga_ref, dgc_ref, dsinks_ref, dcw_ref, dk_carry, dv_carry,
                   first, has_next, cos_p, sgn_p, cos_c, sgn_c, write_kv):
        da_c, dga = _rms_bwd(dmc_ref[:, 0:aw], ac_ref[...], ga_ref[...])
        _accumulate(dga_ref, dga, first)
        kk = jnp.concatenate(
            [jnp.concatenate([_rope(ref[:, dm.o_k + 128 * g:dm.o_k + 128 * g + 128], c, sg, 1.0)
                              for g in range(KV_WIDTH // 128)], axis=1)
             for ref, c, sg in ((pp_ref, cos_p, sgn_p), (pc_ref, cos_c, sgn_c))], axis=0)
        vv = jnp.concatenate([pp_ref[:, dm.o_v:dm.o_v + KV_WIDTH], pc_ref[:, dm.o_v:dm.o_v + KV_WIDTH]], axis=0)
        group, pairs = dm.group, dm.group // 2
        valid_c = _band_mask(group, 2 * WINDOW, first)
        dk_prev, dv_prev = [], []
        dsinks = jnp.zeros((1, nq), _F32)
        head_lane = _lane((1, nq))

        def stacked(q_ref, cos, sgn, da, o_ref, lse_ref_, h):
            cols = [slice(128 * j, 128 * j + 128) for j in range(pairs * h, pairs * (h + 1))]
            q4 = _stack_heads([_rope(q_ref[:, c], cos, sgn, 1.0) for c in cols])
            do4 = _stack_heads([da[:, c] for c in cols])
            lo = _lane((WINDOW, 128)) < HEAD_DIM
            deltas = []
            for c in cols:
                prod = o_ref[:, c] * da[:, c]
                deltas += [jnp.sum(jnp.where(lo, prod, 0.0), axis=1, keepdims=True),
                           jnp.sum(jnp.where(lo, 0.0, prod), axis=1, keepdims=True)]
            lse4 = jnp.concatenate([lse_ref_[:, group * h + r:group * h + r + 1] for r in range(group)], axis=0)
            return q4, do4, lse4, jnp.concatenate(deltas, axis=0)

        def scores_bwd(q4, do4, lse4, delta4, keys, vals, valid):
            sc = _dot(q4, keys, "nt") * ATTN_SCALE
            p = jnp.exp(jnp.where(valid, sc - lse4, MASKED))
            return p.astype(_CDT), (p * (_dot(do4, vals, "nt") - delta4) * ATTN_SCALE).astype(_CDT)

        for h in range(N_KV_HEADS):
            k2, v2 = _dup_head(kk, h).astype(_CDT), _dup_head(vv, h).astype(_CDT)
            q4, do4, lse4, delta4 = stacked(pc_ref, cos_c, sgn_c, da_c, ac_ref, lsec_ref, h)
            p, ds = scores_bwd(q4, do4, lse4, delta4, k2, v2, valid_c)
            for i, dq in enumerate(_unstack_heads(_dot(ds, k2, "nn"), pairs)):
                j = pairs * h + i
                dproj_ref[:, 128 * j:128 * j + 128] = _rope(dq, cos_c, sgn_c, -1.0).astype(dproj_ref.dtype)
            dk = _dot(ds, q4, "tn")
            dv = _dot(p, do4, "tn")
            dk_prev.append(dk_carry[h] + dk[0:WINDOW])
            dv_prev.append(dv_carry[h] + dv[0:WINDOW])
            dk_carry[h] = dk[WINDOW:2 * WINDOW]
            dv_carry[h] = dv[WINDOW:2 * WINDOW]
            sink4 = _per_head([sinks_ref[0, group * h + r] for r in range(group)])
            loss_sink = jnp.exp(sink4 - lse4) * delta4
            for r in range(group):
                dsinks = dsinks + jnp.where(head_lane == group * h + r,
                                            -jnp.sum(loss_sink[WINDOW * r:WINDOW * (r + 1)]), 0.0)
        _accumulate(dsinks_ref, dsinks, first)
        write_kv(dk_prev, dv_prev, cos_p, sgn_p)

        bg = pc_ref[:, dm.o_bg:dm.o_bg + cw]
        yc = yc_ref[...]
        dconv, dgc = _rms_bwd(dmc_ref[:, aw:d], bg * yc, gc_ref[...])
        _accumulate(dgc_ref, dgc, first)
        dproj_ref[:, dm.o_bg:dm.o_bg + cw] = (dconv * yc).astype(dproj_ref.dtype)
        dy = dconv * bg
        bg_n = pn_ref[:, dm.o_bg:dm.o_bg + cw]
        dconv_n, _ = _rms_bwd(dmn_ref[:, aw:d], bg_n * yn_ref[...], gc_ref[...])
        halo = jnp.where(has_next, dconv_n * bg_n, 0.0)
        dy1 = _shift_up(dy, halo, 1)
        dy2 = _shift_up(dy, halo, 2)
        dz = cw_ref[2:3, :] * dy + cw_ref[1:2, :] * dy1 + cw_ref[0:1, :] * dy2
        cg = pc_ref[:, dm.o_cg:dm.o_cg + cw]
        u = pc_ref[:, dm.o_u:dm.o_u + cw]
        dproj_ref[:, dm.o_cg:dm.o_cg + cw] = (dz * u).astype(dproj_ref.dtype)
        dproj_ref[:, dm.o_u:dm.o_u + cw] = (dz * cg).astype(dproj_ref.dtype)
        z = cg * u
        dcw = jnp.concatenate(
            [jnp.sum(z * t, axis=0, keepdims=True) for t in (dy2, dy1, dy)]
            + [jnp.zeros((V7X_SUBLANES - 3, cw), _F32)], axis=0)
        _accumulate(dcw_ref, dcw, first)

    at = lambda n: jnp.minimum(n, nb - 1)
    prev = lambda n: (jnp.maximum(at(n) - 1, 0), 0)
    cur = lambda n: (at(n), 0)
    done = lambda n: (jnp.maximum(n - 1, 0), 0)
    nxt8 = lambda n: (jnp.minimum((at(n) + 1) * (WINDOW // V7X_SUBLANES), s // V7X_SUBLANES - 1), 0)
    fixed = lambda n: (0, 0)
    blocks = WINDOW * (2 * inw * 4 + d * 4 + aw * 4 + cw * 4 + inw * 2 + 2 * KV_WIDTH * 2)
    carry = [pltpu.VMEM((N_KV_HEADS, WINDOW, 128), _F32), pltpu.VMEM((N_KV_HEADS, WINDOW, 128), _F32)]
    n_in, n_out = 15, 6
    ca = _CommArgs(list(comm), n_in, n_out)
    return pl.pallas_call(
        _carrying(body, n_in, n_out, nb + 1, ca, n_scratch=len(carry)), name="mixer_bwd", grid=(nb + 1,),
        in_specs=[pl.BlockSpec((WINDOW, inw), prev), pl.BlockSpec((WINDOW, inw), cur), pl.BlockSpec((V7X_SUBLANES, inw), nxt8),
                  pl.BlockSpec((WINDOW, 2 * V7X_LANES), prev), pl.BlockSpec((WINDOW, 2 * V7X_LANES), cur),
                  pl.BlockSpec((WINDOW, d), cur), pl.BlockSpec((V7X_SUBLANES, d), nxt8),
                  pl.BlockSpec((WINDOW, aw), cur), pl.BlockSpec((WINDOW, nq), cur),
                  pl.BlockSpec((WINDOW, cw), cur), pl.BlockSpec((V7X_SUBLANES, cw), nxt8),
                  pl.BlockSpec(memory_space=pltpu.SMEM),
                  pl.BlockSpec((1, aw), fixed), pl.BlockSpec((1, cw), fixed), pl.BlockSpec((V7X_SUBLANES, cw), fixed)]
        + [_ANY] * len(ca.operands),
        out_specs=[pl.BlockSpec((WINDOW, inw), cur), pl.BlockSpec((WINDOW, 2 * KV_WIDTH), done),
                   pl.BlockSpec((1, aw), fixed), pl.BlockSpec((1, cw), fixed),
                   pl.BlockSpec((1, nq), fixed), pl.BlockSpec((V7X_SUBLANES, cw), fixed)] + [_ANY] * len(ca.out_shape),
        out_shape=[jax.ShapeDtypeStruct((s, inw), _CDT), jax.ShapeDtypeStruct((s, 2 * KV_WIDTH), _CDT),
                   jax.ShapeDtypeStruct((1, aw), _F32), jax.ShapeDtypeStruct((1, cw), _F32),
                   jax.ShapeDtypeStruct((1, nq), _F32), jax.ShapeDtypeStruct((V7X_SUBLANES, cw), _F32)] + ca.out_shape,
        scratch_shapes=carry + ca.sems, input_output_aliases=ca.aliases,
        compiler_params=pltpu.CompilerParams(dimension_semantics=("arbitrary",), vmem_limit_bytes=_vmem_limit(blocks)),
    )(proj, proj, proj, rope, rope, dmixed, dmixed, attn, lse, y, y, sinks, g_attn, g_conv, conv_w8, *ca.operands)


def _position():
    return lax.axis_index("x"), lax.axis_index("y"), lax.axis_index("c")


def _linear(px, py, pc):
    return 4 * px + 2 * py + pc


def _comm_kernel(name, comm):
    ca = _CommArgs(list(comm), 0, 0)
    n_cin, n_cout = len(ca.operands), len(ca.out_shape)

    def body(*refs):
        cin, cout, sems = refs[:n_cin], refs[n_cin:n_cin + n_cout], refs[n_cin + n_cout:]
        ca.start(cin, cout, sems)
        ca.middle(cin, cout, sems)
        ca.finish(cin, cout, sems)

    return pl.pallas_call(
        body, name=name, out_shape=ca.out_shape, in_specs=[_ANY] * n_cin, out_specs=[_ANY] * n_cout,
        scratch_shapes=ca.sems, input_output_aliases=ca.aliases,
    )(*ca.operands)


def _gather_op(units):
    n = len(units)
    inputs, outputs, aliases = [], [], {}
    for shard, _, _, _ in units:
        inputs.append(shard)
        outputs.append(jax.ShapeDtypeStruct((N_DEV * shard.shape[0], shard.shape[1]), shard.dtype))
    for u, (_, buf, _, _) in enumerate(units):
        if buf is not None:
            aliases[len(inputs)] = u
            inputs.append(buf)

    def plan(ins, outs, sems, north):
        send_sems, recv_sems, local_sems = sems
        x, y, c = _position()
        me, sibling = (x, y, c), (x, y, 1 - c)
        xn, yn, dg = (1 - x, y), (x, 1 - y), (1 - x, 1 - y)
        via, to, k_via, k_other = (yn, xn, 2, 1) if north else (xn, yn, 1, 2)

        def rows(u, px, py, pc):
            shard, _, r0, r1 = units[u]
            return outs[u].at[pl.ds(pl.multiple_of(_linear(px, py, pc) * shard.shape[0] + r0, 16), r1 - r0), :]

        def own(u):
            _, _, r0, r1 = units[u]
            return ins[u].at[pl.ds(r0, r1 - r0), :]

        def copy(u, k, block, to_, src=None):
            return pltpu.make_async_remote_copy(
                src_ref=rows(u, *block) if src is None else src, dst_ref=rows(u, *block),
                send_sem=send_sems.at[u, k], recv_sem=recv_sems.at[u, k], device_id=to_, device_id_type=_MESH)

        us = range(n)
        return dict(
            mine=[pltpu.make_async_copy(own(u), rows(u, *me), local_sems.at[u]) for u in us],
            first=[cp for u in us for cp in (copy(u, 0, me, sibling, src=own(u)), copy(u, 1, me, (*xn, c), src=own(u)),
                                             copy(u, 2, me, (*yn, c), src=own(u)))],
            relay=[copy(u, 3, (*via, c), (*to, c)) for u in us],
            arrived={1: [copy(u, 1, (*xn, c), me) for u in us], 2: [copy(u, 2, (*yn, c), me) for u in us],
                     3: [copy(u, 3, (*dg, c), me) for u in us]},
            passed={1: [copy(u, 4, (*xn, c), sibling) for u in us], 2: [copy(u, 5, (*yn, c), sibling) for u in us],
                    3: [copy(u, 6, (*dg, c), sibling) for u in us]},
            rest=[cp for u in us for cp in (copy(u, 0, sibling, me), copy(u, 4, (*xn, 1 - c), me),
                                            copy(u, 5, (*yn, 1 - c), me), copy(u, 6, (*dg, 1 - c), me))],
            k_via=k_via, k_other=k_other)

    def land(p, k):
        for arrived, onward in zip(p["arrived"][k], p["passed"][k]):
            arrived.wait_recv()
            onward.start()

    def by_core(fn):
        c = lax.axis_index("c")
        for north in (True, False):
            pl.when(c == (1 if north else 0))(functools.partial(fn, north))

    def start(ins, outs, sems):
        p = plan(ins, outs, sems, True)
        for cp in p["mine"] + p["first"]:
            cp.start()

    def middle(ins, outs, sems):
        def go(north):
            p = plan(ins, outs, sems, north)
            land(p, p["k_via"])
            for cp in p["relay"]:
                cp.start()
            land(p, p["k_other"])
        by_core(go)

    def finish(ins, outs, sems):
        def go(north):
            p = plan(ins, outs, sems, north)
            land(p, 3)
            for cp in p["rest"]:
                cp.wait_recv()
            for cp in p["first"] + p["relay"] + [cp for k in (1, 2, 3) for cp in p["passed"][k]]:
                cp.wait_send()
            for cp in p["mine"]:
                cp.wait()
        by_core(go)

    sems = [pltpu.SemaphoreType.DMA((n, 7)), pltpu.SemaphoreType.DMA((n, 7)), pltpu.SemaphoreType.DMA((n,))]
    return _Comm(inputs, outputs, aliases, sems, start, finish, middle)


def _peers(x, y, c):
    out = []
    for k in range(1, N_DEV):
        fx, fy, fc = (k >> 2) & 1, (k >> 1) & 1, k & 1
        out.append((1 - x if fx else x, 1 - y if fy else y, 1 - c if fc else c))
    return out


def _exchange_op(partials):
    n = len(partials)
    outputs = [jax.ShapeDtypeStruct((4, p.shape[0] // N_DEV, p.shape[1]), p.dtype) for p in partials]

    def plan(ins, outs, sems):
        send_sems, recv_sems = sems
        x, y, c = _position()
        out = []
        for a in range(n):
            r = outs[a].shape[1]
            for ch in range(4):
                out.append(pltpu.make_async_remote_copy(
                    src_ref=ins[a].at[pl.ds(pl.multiple_of((2 * ch + 1 - c) * r, 16), r), :], dst_ref=outs[a].at[ch],
                    send_sem=send_sems.at[a, ch], recv_sem=recv_sems.at[a, ch], device_id=(x, y, 1 - c),
                    device_id_type=_MESH))
        return out

    def start(ins, outs, sems):
        for cp in plan(ins, outs, sems):
            cp.start()

    def finish(ins, outs, sems):
        copies = plan(ins, outs, sems)
        for cp in copies:
            cp.wait_recv()
        for cp in copies:
            cp.wait_send()

    sems = [pltpu.SemaphoreType.DMA((n, 4)), pltpu.SemaphoreType.DMA((n, 4))]
    return _Comm(list(partials), outputs, {}, sems, start, finish)


def _chip_send_op(units):
    n = len(units)
    inputs, outputs, aliases = [], [], {}
    for q, _, _, _ in units:
        inputs.append(q)
        outputs.append(jax.ShapeDtypeStruct(q.shape, q.dtype))
    for u, (_, buf, _, _) in enumerate(units):
        if buf is not None:
            aliases[len(inputs)] = u
            inputs.append(buf)

    def plan(ins, outs, sems):
        send_sems, recv_sems, local_sems = sems
        x, y, c = _position()
        my_chip = 2 * x + y
        chips = [(1 - x, y), (x, 1 - y), (1 - x, 1 - y)]
        mine, sends, arrivals = [], [], []
        for u, (_, _, r0, r1) in enumerate(units):
            span = pl.ds(r0, r1 - r0)
            mine.append(pltpu.make_async_copy(ins[u].at[my_chip, span, :], outs[u].at[my_chip, span, :], local_sems.at[u]))
            for k, (px, py) in enumerate(chips):
                sends.append(pltpu.make_async_remote_copy(
                    src_ref=ins[u].at[2 * px + py, span, :], dst_ref=outs[u].at[my_chip, span, :],
                    send_sem=send_sems.at[u, k], recv_sem=recv_sems.at[u, k], device_id=(px, py, c), device_id_type=_MESH))
                arrivals.append(pltpu.make_async_remote_copy(
                    src_ref=ins[u].at[my_chip, span, :], dst_ref=outs[u].at[2 * px + py, span, :],
                    send_sem=send_sems.at[u, k], recv_sem=recv_sems.at[u, k], device_id=(px, py, c), device_id_type=_MESH))
        return mine, sends, arrivals

    def start(ins, outs, sems):
        mine, sends, _ = plan(ins, outs, sems)
        for cp in mine + sends:
            cp.start()

    def finish(ins, outs, sems):
        mine, sends, arrivals = plan(ins, outs, sems)
        for cp in arrivals:
            cp.wait_recv()
        for cp in sends:
            cp.wait_send()
        for cp in mine:
            cp.wait()

    sems = [pltpu.SemaphoreType.DMA((n, 3)), pltpu.SemaphoreType.DMA((n, 3)), pltpu.SemaphoreType.DMA((n,))]
    return _Comm(inputs, outputs, aliases, sems, start, finish)


def _pair_sum(name, partial, received):
    _, rows, cols = received.shape
    tr = _pick(rows, (352, 288, 256, 128, 64, 32, 16))
    p4 = partial.reshape(4, 2, rows, cols)
    kind = jnp.reshape(lax.axis_index("c"), (1,)).astype(jnp.int32)

    def body(kind_ref, p_ref, r_ref, o_ref):
        o_ref[0] = (p_ref[0, 0].astype(_F32) + r_ref[0].astype(_F32)).astype(o_ref.dtype)

    return pl.pallas_call(
        body, name=name,
        grid_spec=pltpu.PrefetchScalarGridSpec(
            num_scalar_prefetch=1, grid=(4, rows // tr),
            in_specs=[pl.BlockSpec((1, 1, tr, cols), lambda ch, i, kind_ref: (ch, kind_ref[0], i, 0)),
                      pl.BlockSpec((1, tr, cols), lambda ch, i, kind_ref: (ch, i, 0))],
            out_specs=pl.BlockSpec((1, tr, cols), lambda ch, i, kind_ref: (ch, i, 0))),
        out_shape=jax.ShapeDtypeStruct(received.shape, received.dtype),
        compiler_params=pltpu.CompilerParams(dimension_semantics=("arbitrary", "arbitrary")),
    )(kind, p4, received)


def _all_reduce_small(name, v):
    rows = v.shape[0]

    def body(v_ref, out_ref, land_ref, send_sems, recv_sems):
        x, y, c = _position()
        me = _linear(x, y, c)
        peers = _peers(x, y, c)
        land_ref[me] = v_ref[...]
        sends = [pltpu.make_async_remote_copy(
            src_ref=v_ref, dst_ref=land_ref.at[me], send_sem=send_sems.at[k], recv_sem=recv_sems.at[k],
            device_id=peer, device_id_type=_MESH) for k, peer in enumerate(peers)]
        for cp in sends:
            cp.start()
        for k, peer in enumerate(peers):
            pltpu.make_async_remote_copy(
                src_ref=v_ref, dst_ref=land_ref.at[_linear(*peer)], send_sem=send_sems.at[k], recv_sem=recv_sems.at[k],
                device_id=peer, device_id_type=_MESH).wait_recv()
        for cp in sends:
            cp.wait_send()
        total = land_ref[0]
        for s in range(1, N_DEV):
            total = total + land_ref[s]
        out_ref[...] = total

    return pl.pallas_call(
        body, name=name, out_shape=jax.ShapeDtypeStruct(v.shape, _F32),
        in_specs=[pl.BlockSpec(memory_space=pltpu.VMEM)], out_specs=pl.BlockSpec(memory_space=pltpu.VMEM),
        scratch_shapes=[pltpu.VMEM((N_DEV, rows, V7X_LANES), _F32), pltpu.SemaphoreType.DMA((7,)), pltpu.SemaphoreType.DMA((7,))],
    )(v)


def _adamw(name, w, slots, m, v):
    rows, cols = w.shape
    n_slots = slots.shape[0]
    tr = _pick(rows, (176, 144, 128, 64, 32, 16, 8))

    def body(w_ref, s_ref, m_ref, v_ref, g_ref, d_ref, nm_ref, nv_ref):
        g = s_ref[0].astype(_F32)
        for k in range(1, n_slots):
            g = g + s_ref[k].astype(_F32)
        nm = ADAM_B1 * m_ref[...] + (1.0 - ADAM_B1) * g
        nv = ADAM_B2 * v_ref[...] + (1.0 - ADAM_B2) * (g * g)
        m_hat = nm / (1.0 - ADAM_B1 ** ADAM_STEP)
        v_hat = nv / (1.0 - ADAM_B2 ** ADAM_STEP)
        g_ref[...] = g
        d_ref[...] = -ADAM_LR * (m_hat / (jnp.sqrt(v_hat) + ADAM_EPS) + ADAM_WD * w_ref[...])
        nm_ref[...] = nm
        nv_ref[...] = nv

    spec = pl.BlockSpec((tr, cols), lambda i: (i, 0))
    blocks = 7 * tr * cols * 4 + _nbytes((n_slots, tr, cols), slots.dtype)
    return pl.pallas_call(
        body, name=name, grid=(rows // tr,),
        in_specs=[spec, pl.BlockSpec((n_slots, tr, cols), lambda i: (0, i, 0)), spec, spec], out_specs=[spec] * 4,
        out_shape=[jax.ShapeDtypeStruct((rows, cols), _F32)] * 4,
        compiler_params=pltpu.CompilerParams(dimension_semantics=("arbitrary",), vmem_limit_bytes=_vmem_limit(blocks)),
    )(w, slots, m, v)


def _pad_rows(a, rows):
    return jnp.pad(a, ((0, rows - a.shape[0]), (0, 0)))


def _pack(parts):
    rows, spans, at = [], [], 0
    for p in parts:
        p = p.reshape(-1)
        r = -(-p.shape[0] // V7X_LANES)
        rows.append(jnp.pad(p, (0, r * V7X_LANES - p.shape[0])).reshape(r, V7X_LANES))
        spans.append((at, r, p.shape[0]))
        at += r
    packed = jnp.concatenate(rows, axis=0)
    return _pad_rows(packed, -(-at // V7X_SUBLANES) * V7X_SUBLANES), spans


def _unpack(packed, spans, shapes):
    return [packed[at:at + r].reshape(-1)[:size].reshape(shape) for (at, r, size), shape in zip(spans, shapes)]


def kernel(x, positions, w_in, conv_w, sinks, g_attn, g_conv, w_out, ln1_g, ln1_b, w_gate, w_up, w_down, ln2_g, ln2_b, loss_target, m_w_in, m_conv_w, m_sinks, m_g_attn, m_g_conv, m_w_out, m_ln1_g, m_ln1_b, m_w_gate, m_w_up, m_w_down, m_ln2_g, m_ln2_b, v_w_in, v_conv_w, v_sinks, v_g_attn, v_g_conv, v_w_out, v_ln1_g, v_ln1_b, v_w_gate, v_w_up, v_w_down, v_ln2_g, v_ln2_b):
    _, s, d = x.shape
    d_ff = N_DEV * w_gate.shape[2]
    dm = _Dims(s, d, d_ff)
    aw, cw, nq, inw = dm.aw, dm.cw, dm.nq, dm.inw
    x2 = x[0]
    pos = positions[0].reshape(s, 1)
    inv_freq = ROPE_THETA ** (-jnp.arange(0, ROT_DIM, 2, dtype=_F32) / ROT_DIM)
    invf = jnp.tile(inv_freq, V7X_LANES // (ROT_DIM // 2)).reshape(1, V7X_LANES)

    conv_cols = conv_w.shape[2]
    sh_in, sh_out = w_in[0].T.astype(_CDT), w_out[0].astype(_CDT)
    sh_gate, sh_up, sh_down = w_gate[0].T.astype(_CDT), w_up[0].T.astype(_CDT), w_down[0].astype(_CDT)
    r_in, r_out, r_ff = sh_in.shape[0], sh_out.shape[0], sh_gate.shape[0]
    q_ff = r_ff // 4
    assert q_ff % 16 == 0
    def prepare_body(x_ref, pos_ref, invf_ref, xc_ref, rope_ref):
        xc_ref[...] = x_ref[...].astype(_CDT)
        cos, sgn = _rope_tables(pos_ref[...], invf_ref[...])
        rope_ref[:, 0:V7X_LANES] = cos
        rope_ref[:, V7X_LANES:2 * V7X_LANES] = sgn

    x_c, rope, w_in_t, conv_all = _row_kernel(
        "prepare_gather_w_in", prepare_body, [x2, pos], [invf], [((s, d), _CDT), ((s, 2 * V7X_LANES), _F32)], [],
        comm=[_gather_op([(sh_in, None, 0, r_in), (_pad_rows(conv_w[0], 16), None, 0, 16)])])
    conv_full = conv_all.reshape(N_DEV, 16, conv_cols)[:, :3, :].transpose(1, 0, 2).reshape(3, cw)
    conv_w8 = _pad_rows(conv_full, V7X_SUBLANES)

    tm = _pick(s, (1024, 512, 256, 128))
    tn_in = _pick(inw, (512, 256, 128))
    tn_ff = _pick(d_ff, (512, 256, 128))
    tr = _pick(s, (512, 256, 128))

    proj, w_out_f, w_gate_t = _matmul(
        "proj", [[(x_c, w_in_t, "nt")]], s, inw, d, tm, tn_in, d, [],
        [((s, inw), _F32, (tm, tn_in), _tile_ij)], _store_epilogue,
        comm=[_gather_op([(sh_out, None, 0, r_out), (sh_gate, None, 0, 2 * q_ff)])])
    mixed, attn, lse, y_conv, w_gate_t, w_up_t = _mixer_fwd(
        dm, proj, rope, sinks, g_attn, g_conv, conv_w8,
        comm=[_gather_op([(sh_gate, w_gate_t, 2 * q_ff, r_ff), (sh_up, None, 0, 2 * q_ff)])])

    def residual_epilogue(accs, ex, out, first):
        out[0][...] = DEEPNORM_ALPHA * ex[0][...] + accs[0]

    tn_d = _pick(d, (512,))
    r1, w_up_t = _matmul(
        "out_proj", [[(mixed, w_out_f, "nn")]], s, d, d, tm, tn_d, d, [(x2, (tm, tn_d), _tile_ij)],
        [((s, d), _F32, (tm, tn_d), _tile_ij)], residual_epilogue,
        comm=[_gather_op([(sh_up, w_up_t, 2 * q_ff, 3 * q_ff)])])
    h1, h1_c, xhat1, rstd1, w_up_t = _ln1_fwd_rows(
        r1, ln1_g, ln1_b, comm=[_gather_op([(sh_up, w_up_t, 3 * q_ff, r_ff)])])

    def swiglu_epilogue(accs, ex, out, first):
        gate_v, up_v = accs
        out[0][...] = gate_v
        out[1][...] = up_v
        out[2][...] = (gate_v * jax.nn.sigmoid(gate_v) * up_v).astype(_CDT)

    gate, up, act, w_down_f = _matmul(
        "gate_up", [[(h1_c, w_gate_t, "nt")], [(h1_c, w_up_t, "nt")]], s, d_ff, d, tm, tn_ff, d, [],
        [((s, d_ff), _F32, (tm, tn_ff), _tile_ij), ((s, d_ff), _F32, (tm, tn_ff), _tile_ij),
         ((s, d_ff), _CDT, (tm, tn_ff), _tile_ij)], swiglu_epilogue,
        comm=[_gather_op([(sh_down, None, 0, r_ff)])])

    (r2,) = _matmul("down", [[(act, w_down_f, "nn")]], s, d, d_ff, tm, tn_d, d_ff, [(h1, (tm, tn_d), _tile_ij)],
                    [((s, d), _F32, (tm, tn_d), _tile_ij)], residual_epilogue)
    dr2, dr2_c, loss_acc, d_ln2_g, d_ln2_b = _ln2_loss_bwd(r2, loss_target[0], ln2_g, ln2_b)

    def swiglu_bwd_epilogue(accs, ex, out, first):
        gate_v, up_v = ex[0][...], ex[1][...]
        sig = jax.nn.sigmoid(gate_v)
        out[0][...] = (accs[0] * up_v * (sig * (1.0 + gate_v * (1.0 - sig)))).astype(_CDT)
        out[1][...] = (accs[0] * (gate_v * sig)).astype(_CDT)

    dgate, dup = _matmul(
        "dact", [[(dr2_c, w_down_f, "nt")]], s, d_ff, d, tm, tn_ff, d,
        [(gate, (tm, tn_ff), _tile_ij), (up, (tm, tn_ff), _tile_ij)],
        [((s, d_ff), _CDT, (tm, tn_ff), _tile_ij), ((s, d_ff), _CDT, (tm, tn_ff), _tile_ij)], swiglu_bwd_epilogue)
    def weight_grad(name, a, b, comm=()):
        rows = a.shape[1]
        tw, tn_w = _pick(rows, (512, 256, 128)), _pick(d, (1024, 512))
        return _matmul(name, [[(a, b, "tn")]], rows, d, s, tw, tn_w, s, [],
                       [((rows, d), _CDT, (tw, tn_w), _tile_ij)], _store_epilogue, comm=comm, j_outer=True)

    (dw_down,) = weight_grad("dw_down", act, dr2_c)
    dw_gate_t, x_down = weight_grad("dw_gate", dgate, h1_c, comm=[_exchange_op([dw_down])])
    q_down = _pair_sum("chip_sum_w_down", dw_down, x_down)
    dw_up_t, l_down, x_gate = weight_grad(
        "dw_up", dup, h1_c, comm=[_chip_send_op([(q_down, None, 0, 2 * q_ff)]), _exchange_op([dw_gate_t])])
    q_gate = _pair_sum("chip_sum_w_gate", dw_gate_t, x_gate)

    tn_h = _pick(d, (512,))
    dh1, l_down, l_gate, x_up = _matmul(
        "dh1", [[(dgate, w_gate_t, "nn"), (dup, w_up_t, "nn")]], s, d, d_ff, tr, tn_h, d_ff,
        [(dr2, (tr, tn_h), _tile_ij)], [((s, d), _F32, (tr, tn_h), _tile_ij)], residual_epilogue,
        comm=[_chip_send_op([(q_down, l_down, 2 * q_ff, r_ff), (q_gate, None, 0, r_ff)]), _exchange_op([dw_up_t])])
    q_up = _pair_sum("chip_sum_w_up", dw_up_t, x_up)
    dr1, dr1_c, d_ln1_g, d_ln1_b = _ln1_bwd_rows(dh1, xhat1, rstd1, ln1_g)
    (dmixed,) = _matmul("dmixed", [[(dr1_c, w_out_f, "nt")]], s, d, d, tm, tn_d, d, [],
                        [((s, d), _F32, (tm, tn_d), _tile_ij)], _store_epilogue)
    (dw_out,) = weight_grad("dw_out", mixed, dr1_c)
    dproj, dkv, d_g_attn, d_g_conv, d_sinks, d_conv8, l_up, x_out = _mixer_bwd(
        dm, proj, rope, sinks, g_attn, g_conv, conv_w8, dmixed, attn, lse, y_conv,
        comm=[_chip_send_op([(q_up, None, 0, r_ff)]), _exchange_op([dw_out])])
    dproj = _patch_columns("dproj_kv", dproj, dkv, dm.o_k)
    q_out = _pair_sum("chip_sum_w_out", dw_out, x_out)
    dw_in_t, l_out = weight_grad("dw_in", dproj, x_c, comm=[_chip_send_op([(q_out, None, 0, r_out)])])
    (x_in,) = _comm_kernel("exchange_w_in", [_exchange_op([dw_in_t])])
    q_in = _pair_sum("chip_sum_w_in", dw_in_t, x_in)

    grad_x, l_in = _matmul("dx", [[(dproj, w_in_t, "nn")]], s, d, inw, tm, tn_d, inw,
                           [(dr1, (tm, tn_d), _tile_ij)], [((s, d), _F32, (tm, tn_d), _tile_ij)], residual_epilogue,
                           comm=[_chip_send_op([(q_in, None, 0, r_in)])])

    small_parts = [d_conv8[:3], d_sinks, d_g_attn, d_g_conv, d_ln1_g, d_ln1_b, d_ln2_g, d_ln2_b, loss_acc[0:1, 0:1]]
    packed, spans = _pack(small_parts)
    reduced = _unpack(_all_reduce_small("reduce_small", packed), spans, [p.shape for p in small_parts])
    g_conv_full, g_sinks, g_g_attn, g_g_conv, g_ln1_g, g_ln1_b, g_ln2_g, g_ln2_b, loss_sum = reduced
    me = _linear(*_position())
    g_conv_w = lax.dynamic_slice(g_conv_full, (0, me * conv_cols), (3, conv_cols))
    loss = loss_sum[0, 0]

    big = {"w_in": (w_in[0].T, l_in, m_w_in[0].T, v_w_in[0].T), "w_out": (w_out[0], l_out, m_w_out[0], v_w_out[0]),
           "w_gate": (w_gate[0].T, l_gate, m_w_gate[0].T, v_w_gate[0].T),
           "w_up": (w_up[0].T, l_up, m_w_up[0].T, v_w_up[0].T), "w_down": (w_down[0], l_down, m_w_down[0], v_w_down[0])}
    res = {nm: tuple(_adamw(f"adamw_{nm}", w, slots, m, v)) for nm, (w, slots, m, v) in big.items()}
    for nm in ("w_in", "w_gate", "w_up"):
        res[nm] = tuple(a.T for a in res[nm])
    small_names = ["conv_w", "sinks", "g_attn", "g_conv", "ln1_g", "ln1_b", "ln2_g", "ln2_b"]
    small_w = [conv_w, sinks, g_attn, g_conv, ln1_g, ln1_b, ln2_g, ln2_b]
    small_g = [g_conv_w[None], g_sinks, g_g_attn, g_g_conv, g_ln1_g, g_ln1_b, g_ln2_g, g_ln2_b]
    small_m = [m_conv_w, m_sinks, m_g_attn, m_g_conv, m_ln1_g, m_ln1_b, m_ln2_g, m_ln2_b]
    small_v = [v_conv_w, v_sinks, v_g_attn, v_g_conv, v_ln1_g, v_ln1_b, v_ln2_g, v_ln2_b]
    pw, sp = _pack(small_w)
    pg, _ = _pack(small_g)
    pm, _ = _pack(small_m)
    pv, _ = _pack(small_v)
    shapes = [w.shape for w in small_w]
    _, sd, sm, sv = [_unpack(p, sp, shapes) for p in _adamw("adamw_small", pw, pg[None], pm, pv)]
    for i, nm in enumerate(small_names):
        res[nm] = (small_g[i].reshape(shapes[i]), sd[i], sm[i], sv[i])

    order = ["w_in", "conv_w", "sinks", "g_attn", "g_conv", "w_out", "ln1_g", "ln1_b", "w_gate", "w_up", "w_down", "ln2_g", "ln2_b"]

    def lead(a, nm):
        return a[None] if nm in big else a

    return (loss, grad_x[None],
            *[lead(res[nm][0], nm) for nm in order], *[lead(res[nm][1], nm) for nm in order],
            *[lead(res[nm][2], nm) for nm in order], *[lead(res[nm][3], nm) for nm in order])
```

```python
import functools

import jax
import jax.numpy as jnp
from jax import lax
from jax.experimental import pallas as pl
from jax.experimental.pallas import tpu as pltpu

_F32 = jnp.float32
_CDT = jnp.bfloat16

HEAD_DIM = 64
WINDOW = 128
N_KV_HEADS = 4
KV_WIDTH = N_KV_HEADS * HEAD_DIM
ROT_DIM = HEAD_DIM // 4
ROPE_THETA = 500000.0
ATTN_SCALE = HEAD_DIM ** -0.5
DEPTH = 1
DEEPNORM_ALPHA = (2 * DEPTH) ** 0.25
LN_EPS = 1e-5
RMS_EPS = 1e-6
ADAM_LR = 0.001
ADAM_B1 = 0.9
ADAM_B2 = 0.999
ADAM_EPS = 1e-08
ADAM_WD = 0.01
ADAM_STEP = 10
N_DEV = 8
MASKED = -1e30

MIB = 1024 * 1024
V7X_VMEM_BYTES = 64 * MIB
V7X_LANES = 128
V7X_SUBLANES = 8
BODY_TEMPORARIES_BYTES = 16 * MIB
VMEM_LIMIT_FLOOR_BYTES = 32 * MIB
VMEM_LIMIT_CEILING_BYTES = V7X_VMEM_BYTES - 8 * MIB
_MESH = pl.DeviceIdType.MESH
_ANY = pl.BlockSpec(memory_space=pl.ANY)


def _vmem_limit(block_bytes, scratch_bytes=0):
    want = 2 * block_bytes + scratch_bytes + BODY_TEMPORARIES_BYTES
    return int(min(max(want, VMEM_LIMIT_FLOOR_BYTES), VMEM_LIMIT_CEILING_BYTES))


def _nbytes(shape, dtype):
    n = 1
    for s in shape:
        n *= s
    return n * jnp.dtype(dtype).itemsize


def _pick(n, candidates):
    for c in candidates:
        if n % c == 0:
            return c
    raise ValueError(f"no tile of {candidates} divides {n}")


_DOT_DIMS = {"nn": ((1,), (0,)), "nt": ((1,), (1,)), "tn": ((0,), (0,))}


def _dot(a, b, mode):
    return lax.dot_general(a.astype(_CDT), b.astype(_CDT), (_DOT_DIMS[mode], ((), ())),
                           preferred_element_type=_F32)


def _accumulate(ref, val, first):
    @pl.when(first)
    def _():
        ref[...] = val

    @pl.when(jnp.logical_not(first))
    def _():
        ref[...] += val


class _Comm:
    def __init__(self, inputs, outputs, aliases, sems, start, finish, middle=None):
        self.inputs, self.outputs, self.aliases, self.sems = inputs, outputs, aliases, sems
        self.start, self.finish, self.middle = start, finish, middle


def _middle_step(n_steps):
    return (2 * n_steps) // 3


class _CommArgs:
    def __init__(self, comms, n_in_before, n_out_before):
        self.comms, self.operands, self.out_shape, self.aliases, self.sems, self.at = comms, [], [], {}, [], []
        for cm in comms:
            self.at.append((len(self.operands), len(self.out_shape), len(self.sems)))
            for i_in, i_out in cm.aliases.items():
                self.aliases[n_in_before + len(self.operands) + i_in] = n_out_before + len(self.out_shape) + i_out
            self.operands += cm.inputs
            self.out_shape += cm.outputs
            self.sems += cm.sems

    def _each(self, in_refs, out_refs, sem_refs):
        for cm, (i0, o0, s0) in zip(self.comms, self.at):
            yield cm, (in_refs[i0:i0 + len(cm.inputs)], out_refs[o0:o0 + len(cm.outputs)], sem_refs[s0:s0 + len(cm.sems)])

    def start(self, in_refs, out_refs, sem_refs):
        for cm, refs in self._each(in_refs, out_refs, sem_refs):
            cm.start(*refs)

    def finish(self, in_refs, out_refs, sem_refs):
        for cm, refs in self._each(in_refs, out_refs, sem_refs):
            cm.finish(*refs)

    @property
    def has_middle(self):
        return any(cm.middle is not None for cm in self.comms)

    def middle(self, in_refs, out_refs, sem_refs):
        for cm, refs in self._each(in_refs, out_refs, sem_refs):
            if cm.middle is not None:
                cm.middle(*refs)


def _matmul(name, groups, m, n, k, tm, tn, tk, extras, outs, epilogue, comm=(), j_outer=False):
    assert m % tm == 0 and n % tn == 0 and k % tk == 0, (name, m, n, k, tm, tn, tk)
    nk = k // tk
    terms = [t for g in groups for t in g]
    operands, in_specs, block_bytes = [], [], 0

    def spec(blk, imap):
        return pl.BlockSpec(blk, (lambda g0, g1, kk: imap(g1, g0, kk)) if j_outer else imap)

    for a, b, mode in terms:
        assert a.shape == ((k, m) if mode == "tn" else (m, k)), (name, a.shape, mode)
        assert b.shape == ((n, k) if mode == "nt" else (k, n)), (name, b.shape, mode)
        if mode == "tn":
            a_blk, a_map = (tk, tm), (lambda i, j, kk: (kk, i))
        else:
            a_blk, a_map = (tm, tk), (lambda i, j, kk: (i, kk))
        if mode == "nt":
            b_blk, b_map = (tn, tk), (lambda i, j, kk: (j, kk))
        else:
            b_blk, b_map = (tk, tn), (lambda i, j, kk: (kk, j))
        operands += [a, b]
        in_specs += [spec(a_blk, a_map), spec(b_blk, b_map)]
        block_bytes += _nbytes(a_blk, a.dtype) + _nbytes(b_blk, b.dtype)
    for arr, blk, imap in extras:
        operands.append(arr)
        in_specs.append(spec(blk, lambda i, j, kk, imap=imap: imap(i, j)))
        block_bytes += _nbytes(blk, arr.dtype)
    out_shape, out_specs = [], []
    for shape, dtype, blk, imap in outs:
        out_shape.append(jax.ShapeDtypeStruct(shape, dtype))
        out_specs.append(spec(blk, lambda i, j, kk, imap=imap: imap(i, j)))
        block_bytes += _nbytes(blk, dtype)
    n_terms, n_extra, n_out, n_groups = len(terms), len(extras), len(outs), len(groups)
    scratch = [pltpu.VMEM((tm, tn), _F32) for _ in range(n_groups)] if nk > 1 else []
    ca = _CommArgs(list(comm), len(operands), n_out)
    n_cin, n_cout, n_acc = len(ca.operands), len(ca.out_shape), len(scratch)
    tiles = (m // tm, n // tn)
    grid = (tiles[1], tiles[0], nk) if j_outer else (tiles[0], tiles[1], nk)

    def body(*refs):
        refs = list(refs)
        term_refs = [refs.pop(0) for _ in range(2 * n_terms)]
        extra_refs = [refs.pop(0) for _ in range(n_extra)]
        cin_refs = [refs.pop(0) for _ in range(n_cin)]
        out_refs = [refs.pop(0) for _ in range(n_out)]
        cout_refs = [refs.pop(0) for _ in range(n_cout)]
        acc_refs = [refs.pop(0) for _ in range(n_acc)]
        sem_refs = refs
        g0, g1, kk = pl.program_id(0), pl.program_id(1), pl.program_id(2)
        first = jnp.logical_and(g0 == 0, g1 == 0)
        if comm:
            @pl.when(jnp.logical_and(first, kk == 0))
            def _():
                ca.start(cin_refs, cout_refs, sem_refs)
        if ca.has_middle:
            step = (g0 * grid[1] + g1) * nk + kk

            @pl.when(step == _middle_step(grid[0] * grid[1] * nk))
            def _():
                ca.middle(cin_refs, cout_refs, sem_refs)
        partial, t = [], 0
        for g in groups:
            s = None
            for _, _, mode in g:
                d = _dot(term_refs[2 * t][...], term_refs[2 * t + 1][...], mode)
                s = d if s is None else s + d
                t += 1
            partial.append(s)
        if nk == 1:
            epilogue(partial, extra_refs, out_refs, first)
        else:
            for acc, p in zip(acc_refs, partial):
                _accumulate(acc, p, kk == 0)

            @pl.when(kk == nk - 1)
            def _():
                epilogue([acc[...] for acc in acc_refs], extra_refs, out_refs, first)
        if comm:
            @pl.when(jnp.logical_and(jnp.logical_and(g0 == grid[0] - 1, g1 == grid[1] - 1), kk == nk - 1))
            def _():
                ca.finish(cin_refs, cout_refs, sem_refs)

    res = pl.pallas_call(
        body, name=name, grid=grid,
        in_specs=in_specs + [_ANY] * n_cin, out_specs=out_specs + [_ANY] * n_cout,
        out_shape=out_shape + ca.out_shape, scratch_shapes=scratch + ca.sems, input_output_aliases=ca.aliases,
        compiler_params=pltpu.CompilerParams(
            dimension_semantics=("arbitrary", "arbitrary", "arbitrary"),
            vmem_limit_bytes=_vmem_limit(block_bytes, n_groups * tm * tn * 4 if nk > 1 else 0)),
    )(*operands, *ca.operands)
    return list(res[:n_out]) + list(res[n_out:])


def _store_epilogue(accs, extra_refs, out_refs, first):
    for acc, ref in zip(accs, out_refs):
        ref[...] = acc.astype(ref.dtype)


def _tile_ij(i, j):
    return (i, j)


def _row_i(i, j):
    return (i, 0)


def _whole(i, j):
    return (0, 0)


def _mean(v):
    return jnp.mean(v, axis=-1, keepdims=True)


def _ln_fwd(r, g, b):
    xc = r - _mean(r)
    rstd = lax.rsqrt(_mean(xc * xc) + LN_EPS)
    xhat = xc * rstd
    return xhat * g + b, xhat, rstd


def _ln_bwd(dy, xhat, rstd, g):
    dxh = dy * g
    dr = rstd * (dxh - _mean(dxh) - xhat * _mean(dxh * xhat))
    return dr, jnp.sum(dy * xhat, axis=0, keepdims=True), jnp.sum(dy, axis=0, keepdims=True)


def _rms_fwd(a, g):
    rstd = lax.rsqrt(_mean(a * a) + RMS_EPS)
    return a * rstd * g


def _rms_bwd(dm, a, g):
    rstd = lax.rsqrt(_mean(a * a) + RMS_EPS)
    nhat = a * rstd
    dn = dm * g
    da = rstd * (dn - nhat * _mean(dn * nhat))
    return da, jnp.sum(dm * nhat, axis=0, keepdims=True)


def _lane(shape):
    return lax.broadcasted_iota(jnp.int32, shape, 1)


def _row(shape):
    return lax.broadcasted_iota(jnp.int32, shape, 0)


def _rope_tables(pos, invf):
    ang = pos.astype(_F32) * invf
    lane = _lane(ang.shape)
    in_rot = (lane % HEAD_DIM) < ROT_DIM
    first = (lane % ROT_DIM) < ROT_DIM // 2
    cos = jnp.where(in_rot, jnp.cos(ang), 1.0)
    sin = jnp.sin(ang)
    sgn = jnp.where(in_rot, jnp.where(first, -sin, sin), 0.0)
    return cos, sgn


def _rope(t, cos, sgn, sign):
    half = ROT_DIM // 2
    first = (_lane(t.shape) % ROT_DIM) < half
    partner = jnp.where(first, pltpu.roll(t, V7X_LANES - half, 1), pltpu.roll(t, half, 1))
    return t * cos + partner * (sgn * sign)


def _dup_head(t, h):
    g = t[:, 128 * (h // 2):128 * (h // 2) + 128]
    r = pltpu.roll(g, HEAD_DIM, 1)
    lo = _lane(g.shape) < HEAD_DIM
    return jnp.where(lo, g, r) if h % 2 == 0 else jnp.where(lo, r, g)


def _fold_halves(t):
    return t + pltpu.roll(t, HEAD_DIM, 1)


def _halves(t):
    lo = _lane(t.shape) < HEAD_DIM
    zero = jnp.zeros_like(t)
    return jnp.where(lo, t, zero), jnp.where(lo, zero, t)


def _band_mask(n_heads, n_keys, first_block):
    shape = (n_heads * WINDOW, n_keys)
    i = jnp.bitwise_and(_row(shape), WINDOW - 1)
    j = _lane(shape)
    valid = jnp.logical_and(j >= i + 1, j <= i + WINDOW)
    if first_block is not None:
        valid = jnp.logical_and(valid, jnp.logical_or(j >= WINDOW, jnp.logical_not(first_block)))
    return valid


def _stack_heads(pairs):
    return jnp.concatenate([half for t in pairs for half in _halves(t)], axis=0).astype(_CDT)


def _unstack_heads(t, n_pairs):
    lo = _lane((WINDOW, 128)) < HEAD_DIM
    return [jnp.where(lo, t[2 * WINDOW * i:2 * WINDOW * i + WINDOW], t[2 * WINDOW * i + WINDOW:2 * WINDOW * (i + 1)])
            for i in range(n_pairs)]


def _per_head(values):
    n_rows = len(values) * WINDOW
    block = jnp.right_shift(_row((n_rows, 1)), WINDOW.bit_length() - 1)
    out = jnp.zeros((n_rows, 1), _F32)
    for k, v in enumerate(values):
        out = jnp.where(block == k, v, out)
    return out


def _shift_down(z, halo, k):
    out = pltpu.roll(z, k, 0)
    r = _row(z.shape)
    for t in range(k):
        out = jnp.where(r == t, halo[V7X_SUBLANES - k + t:V7X_SUBLANES - k + t + 1, :], out)
    return out


def _shift_up(z, halo, k):
    rows = z.shape[0]
    out = pltpu.roll(z, rows - k, 0)
    r = _row(z.shape)
    for t in range(k):
        out = jnp.where(r == rows - k + t, halo[t:t + 1, :], out)
    return out


class _Dims:
    def __init__(self, s, d, d_ff):
        self.s, self.d, self.d_ff = s, d, d_ff
        self.aw = d // 2
        self.cw = d - self.aw
        self.nq = self.aw // HEAD_DIM
        self.group = self.nq // N_KV_HEADS
        assert self.group % 2 == 0, "a 128-lane pair of query heads must share its kv head"
        self.inw = self.aw + 2 * KV_WIDTH + 3 * self.cw
        self.o_k = self.aw
        self.o_v = self.aw + KV_WIDTH
        self.o_cg = self.aw + 2 * KV_WIDTH
        self.o_bg = self.o_cg + self.cw
        self.o_u = self.o_bg + self.cw
        self.nb = s // WINDOW
        assert s % WINDOW == 0


def _carrying(body, n_in, n_out, n_steps, ca, n_scratch=0):
    n_cin, n_cout = len(ca.operands), len(ca.out_shape)

    def wrapped(*refs):
        refs = list(refs)
        in_refs = [refs.pop(0) for _ in range(n_in)]
        cin_refs = [refs.pop(0) for _ in range(n_cin)]
        out_refs = [refs.pop(0) for _ in range(n_out)]
        cout_refs = [refs.pop(0) for _ in range(n_cout)]
        scratch_refs = [refs.pop(0) for _ in range(n_scratch)]
        if ca.comms:
            @pl.when(pl.program_id(0) == 0)
            def _():
                ca.start(cin_refs, cout_refs, refs)
        if ca.has_middle:
            @pl.when(pl.program_id(0) == _middle_step(n_steps))
            def _():
                ca.middle(cin_refs, cout_refs, refs)
        body(*in_refs, *out_refs, *scratch_refs)
        if ca.comms:
            @pl.when(pl.program_id(0) == n_steps - 1)
            def _():
                ca.finish(cin_refs, cout_refs, refs)

    return wrapped


def _row_kernel(name, body, rows_in, vecs_in, rows_out, vecs_out, comm=()):
    s = rows_in[0].shape[0]
    tr = _pick(s, (256, 128))
    row = lambda a: pl.BlockSpec((tr, a[1] if isinstance(a, tuple) else a.shape[1]), lambda i: (i, 0))
    vec = lambda shape: pl.BlockSpec(tuple(shape), lambda i: (0, 0))
    n_in, n_out = len(rows_in) + len(vecs_in), len(rows_out) + len(vecs_out)
    ca = _CommArgs(list(comm), n_in, n_out)
    blocks = sum(_nbytes((tr, a.shape[1]), a.dtype) for a in rows_in) + sum(_nbytes((tr, sh[1]), dt) for sh, dt in rows_out)
    res = pl.pallas_call(
        _carrying(body, n_in, n_out, s // tr, ca), name=name, grid=(s // tr,),
        in_specs=[row(a) for a in rows_in] + [vec(v.shape) for v in vecs_in] + [_ANY] * len(ca.operands),
        out_specs=[row(sh) for sh, _ in rows_out] + [vec(sh) for sh, _ in vecs_out] + [_ANY] * len(ca.out_shape),
        out_shape=[jax.ShapeDtypeStruct(sh, dt) for sh, dt in list(rows_out) + list(vecs_out)] + ca.out_shape,
        scratch_shapes=ca.sems, input_output_aliases=ca.aliases,
        compiler_params=pltpu.CompilerParams(dimension_semantics=("arbitrary",), vmem_limit_bytes=_vmem_limit(blocks)),
    )(*rows_in, *vecs_in, *ca.operands)
    return list(res)


def _ln2_loss_bwd(r2, target, gain, bias, comm=()):
    s, d = r2.shape

    def body(r_ref, t_ref, g_ref, b_ref, dr_ref, drc_ref, loss_ref, dg_ref, db_ref):
        first = pl.program_id(0) == 0
        yv, xhat, rstd = _ln_fwd(r_ref[...], g_ref[...], b_ref[...])
        err = yv - t_ref[...]
        dr2, dg, db = _ln_bwd(err * (1.0 / d), xhat, rstd, g_ref[...])
        dr_ref[...] = dr2
        drc_ref[...] = dr2.astype(_CDT)
        _accumulate(loss_ref, jnp.zeros(loss_ref.shape, _F32) + 0.5 * jnp.sum(err * err) * (1.0 / d), first)
        _accumulate(dg_ref, dg, first)
        _accumulate(db_ref, db, first)

    return _row_kernel("ln2_loss_bwd", body, [r2, target], [gain, bias], [((s, d), _F32), ((s, d), _CDT)],
                       [((V7X_SUBLANES, V7X_LANES), _F32), ((1, d), _F32), ((1, d), _F32)], comm)


def _ln1_fwd_rows(r1, gain, bias, comm=()):
    s, d = r1.shape

    def body(r_ref, g_ref, b_ref, h_ref, hc_ref):
        h1, _, _ = _ln_fwd(r_ref[...], g_ref[...], b_ref[...])
        h_ref[...] = h1
        hc_ref[...] = h1.astype(_CDT)

    return _row_kernel("ln1", body, [r1], [gain, bias], [((s, d), _F32), ((s, d), _CDT)], [], comm)


def _ln1_bwd_rows(dh1, r1, gain, comm=()):
    s, d = dh1.shape

    def body(dh_ref, r_ref, g_ref, dr_ref, drc_ref, dg_ref, db_ref):
        first = pl.program_id(0) == 0
        _, xhat, rstd = _ln_fwd(r_ref[...], g_ref[...], 0.0)
        dr1, dg, db = _ln_bwd(dh_ref[...], xhat, rstd, g_ref[...])
        dr_ref[...] = dr1
        drc_ref[...] = dr1.astype(_CDT)
        _accumulate(dg_ref, dg, first)
        _accumulate(db_ref, db, first)

    return _row_kernel("ln1_bwd", body, [dh1, r1], [gain], [((s, d), _F32), ((s, d), _CDT)],
                       [((1, d), _F32), ((1, d), _F32)], comm)


def _mixer_fwd(dm, proj, rope, sinks, g_attn, g_conv, conv_w8, comm=()):
    s, d, aw, cw, nq, inw, nb = dm.s, dm.d, dm.aw, dm.cw, dm.nq, dm.inw, dm.nb

    def body(pp_ref, pc_ref, ropep_ref, ropec_ref, sinks_ref, ga_ref, gc_ref, cw_ref,
             mixed_ref, attn_ref, lse_ref, y_ref):
        n = pl.program_id(0)
        cos_c, sgn_c = ropec_ref[:, 0:V7X_LANES], ropec_ref[:, V7X_LANES:2 * V7X_LANES]
        cos_p, sgn_p = ropep_ref[:, 0:V7X_LANES], ropep_ref[:, V7X_LANES:2 * V7X_LANES]
        kk = jnp.concatenate(
            [jnp.concatenate([_rope(ref[:, dm.o_k + 128 * g:dm.o_k + 128 * g + 128], c, sg, 1.0)
                              for g in range(KV_WIDTH // 128)], axis=1)
             for ref, c, sg in ((pp_ref, cos_p, sgn_p), (pc_ref, cos_c, sgn_c))], axis=0)
        vv = jnp.concatenate([pp_ref[:, dm.o_v:dm.o_v + KV_WIDTH], pc_ref[:, dm.o_v:dm.o_v + KV_WIDTH]], axis=0)
        group, pairs = dm.group, dm.group // 2
        valid = _band_mask(group, 2 * WINDOW, n == 0)
        for h in range(N_KV_HEADS):
            k2, v2 = _dup_head(kk, h).astype(_CDT), _dup_head(vv, h).astype(_CDT)
            q4 = _stack_heads([_rope(pc_ref[:, 128 * j:128 * j + 128], cos_c, sgn_c, 1.0)
                               for j in range(pairs * h, pairs * (h + 1))])
            sc = jnp.where(valid, _dot(q4, k2, "nt") * ATTN_SCALE, MASKED)
            sink = _per_head([sinks_ref[0, group * h + r] for r in range(group)])
            mx = jnp.maximum(jnp.max(sc, axis=1, keepdims=True), sink)
            p = jnp.exp(sc - mx)
            den = jnp.sum(p, axis=1, keepdims=True) + jnp.exp(sink - mx)
            out = _unstack_heads(_dot(p / den, v2, "nn"), pairs)
            lse = mx + jnp.log(den)
            for r in range(group):
                lse_ref[:, group * h + r:group * h + r + 1] = lse[WINDOW * r:WINDOW * (r + 1)]
            for i in range(pairs):
                j = pairs * h + i
                attn_ref[:, 128 * j:128 * j + 128] = out[i]
        mixed_ref[:, 0:aw] = _rms_fwd(attn_ref[...], ga_ref[...]).astype(mixed_ref.dtype)

        z = pc_ref[:, dm.o_cg:dm.o_cg + cw] * pc_ref[:, dm.o_u:dm.o_u + cw]
        top = WINDOW - V7X_SUBLANES
        halo = pp_ref[top:WINDOW, dm.o_cg:dm.o_cg + cw] * pp_ref[top:WINDOW, dm.o_u:dm.o_u + cw]
        halo = jnp.where(n == 0, jnp.zeros_like(halo), halo)
        y = cw_ref[0:1, :] * _shift_down(z, halo, 2) + cw_ref[1:2, :] * _shift_down(z, halo, 1) + cw_ref[2:3, :] * z
        y_ref[...] = y
        conv = pc_ref[:, dm.o_bg:dm.o_bg + cw] * y
        mixed_ref[:, aw:d] = _rms_fwd(conv, gc_ref[...]).astype(mixed_ref.dtype)

    prev = lambda n: (jnp.maximum(n - 1, 0), 0)
    cur = lambda n: (n, 0)
    fixed = lambda n: (0, 0)
    blocks = 2 * WINDOW * inw * 4 + WINDOW * (d * 2 + aw * 4 + cw * 4 + nq * 4)
    ca = _CommArgs(list(comm), 8, 4)
    return pl.pallas_call(
        _carrying(body, 8, 4, nb, ca), name="mixer_fwd", grid=(nb,),
        in_specs=[pl.BlockSpec((WINDOW, inw), prev), pl.BlockSpec((WINDOW, inw), cur),
                  pl.BlockSpec((WINDOW, 2 * V7X_LANES), prev), pl.BlockSpec((WINDOW, 2 * V7X_LANES), cur),
                  pl.BlockSpec(memory_space=pltpu.SMEM),
                  pl.BlockSpec((1, aw), fixed), pl.BlockSpec((1, cw), fixed), pl.BlockSpec((V7X_SUBLANES, cw), fixed)]
        + [_ANY] * len(ca.operands),
        out_specs=[pl.BlockSpec((WINDOW, d), cur), pl.BlockSpec((WINDOW, aw), cur),
                   pl.BlockSpec((WINDOW, nq), cur), pl.BlockSpec((WINDOW, cw), cur)] + [_ANY] * len(ca.out_shape),
        out_shape=[jax.ShapeDtypeStruct((s, d), _CDT), jax.ShapeDtypeStruct((s, aw), _F32),
                   jax.ShapeDtypeStruct((s, nq), _F32), jax.ShapeDtypeStruct((s, cw), _F32)] + ca.out_shape,
        scratch_shapes=ca.sems, input_output_aliases=ca.aliases,
        compiler_params=pltpu.CompilerParams(dimension_semantics=("arbitrary",), vmem_limit_bytes=_vmem_limit(blocks)),
    )(proj, proj, rope, rope, sinks, g_attn, g_conv, conv_w8, *ca.operands)


def _patch_columns(name, a, part, offset):
    s, pw = part.shape
    assert offset % pw == 0 and pw % V7X_LANES == 0
    tr = _pick(s, (512, 256, 128))

    def body(a_ref, p_ref, o_ref):
        del a_ref
        o_ref[...] = p_ref[...]

    return pl.pallas_call(
        body, name=name, grid=(s // tr,),
        in_specs=[_ANY, pl.BlockSpec((tr, pw), lambda i: (i, 0))],
        out_specs=pl.BlockSpec((tr, pw), lambda i: (i, offset // pw)),
        out_shape=jax.ShapeDtypeStruct(a.shape, a.dtype), input_output_aliases={0: 0},
        compiler_params=pltpu.CompilerParams(dimension_semantics=("arbitrary",)),
    )(a, part)


def _mixer_bwd(dm, proj, rope, sinks, g_attn, g_conv, conv_w8, dmixed, attn, lse, y, comm=()):
    s, d, aw, cw, nq, inw, nb = dm.s, dm.d, dm.aw, dm.cw, dm.nq, dm.inw, dm.nb

    def body(pp_ref, pc_ref, pn_ref, ropep_ref, ropec_ref, dmc_ref, dmn_ref, ac_ref,
             lsec_ref, yc_ref, yn_ref, sinks_ref, ga_ref, gc_ref, cw_ref,
             dproj_ref, dkv_ref, dga_ref, dgc_ref, dsinks_ref, dcw_ref, dk_carry, dv_carry):
        n = pl.program_id(0)
        first = n == 0
        live = n < nb
        has_next = n < nb - 1
        cos_p, sgn_p = ropep_ref[:, 0:V7X_LANES], ropep_ref[:, V7X_LANES:2 * V7X_LANES]
        cos_c, sgn_c = ropec_ref[:, 0:V7X_LANES], ropec_ref[:, V7X_LANES:2 * V7X_LANES]

        @pl.when(first)
        def _():
            dk_carry[...] = jnp.zeros(dk_carry.shape, _F32)
            dv_carry[...] = jnp.zeros(dv_carry.shape, _F32)

        def write_kv(dk2, dv2, cos, sgn):
            lo = _lane((WINDOW, 128)) < HEAD_DIM
            for g in range(KV_WIDTH // 128):
                dk = jnp.where(lo, _fold_halves(dk2[2 * g]), _fold_halves(dk2[2 * g + 1]))
                dv = jnp.where(lo, _fold_halves(dv2[2 * g]), _fold_halves(dv2[2 * g + 1]))
                dkv_ref[:, 128 * g:128 * g + 128] = _rope(dk, cos, sgn, -1.0).astype(dkv_ref.dtype)
                dkv_ref[:, KV_WIDTH + 128 * g:KV_WIDTH + 128 * g + 128] = dv.astype(dkv_ref.dtype)

        @pl.when(jnp.logical_not(live))
        def _():
            write_kv([dk_carry[h] for h in range(N_KV_HEADS)], [dv_carry[h] for h in range(N_KV_HEADS)], cos_c, sgn_c)

        @pl.when(live)
        def _():
            block_step(pp_ref, pc_ref, pn_ref, dmc_ref, dmn_ref, ac_ref, lsec_ref, yc_ref, yn_ref, sinks_ref, ga_ref,
                       gc_ref, cw_ref, dproj_ref, dga_ref, dgc_ref, dsinks_ref, dcw_ref, dk_carry, dv_carry,
                       first, has_next, cos_p, sgn_p, cos_c, sgn_c, write_kv)

    def block_step(pp_ref, pc_ref, pn_ref, dmc_ref, dmn_ref, ac_ref, lsec_ref, yc_ref, yn_ref, sinks_ref, ga_ref,
                   gc_ref, cw_ref, dproj_ref, dga_ref, dgc_ref, dsinks_ref, dcw_ref, dk_carry, dv_carry,
                   first, has_next, cos_p, sgn_p, cos_c, sgn_c, write_kv):
        da_c, dga = _rms_bwd(dmc_ref[:, 0:aw], ac_ref[...], ga_ref[...])
        _accumulate(dga_ref, dga, first)
        kk = jnp.concatenate(
            [jnp.concatenate([_rope(ref[:, dm.o_k + 128 * g:dm.o_k + 128 * g + 128], c, sg, 1.0)
                              for g in range(KV_WIDTH // 128)], axis=1)
             for ref, c, sg in ((pp_ref, cos_p, sgn_p), (pc_ref, cos_c, sgn_c))], axis=0)
        vv = jnp.concatenate([pp_ref[:, dm.o_v:dm.o_v + KV_WIDTH], pc_ref[:, dm.o_v:dm.o_v + KV_WIDTH]], axis=0)
        group, pairs = dm.group, dm.group // 2
        valid_c = _band_mask(group, 2 * WINDOW, first)
        dk_prev, dv_prev = [], []
        dsinks = jnp.zeros((1, nq), _F32)
        head_lane = _lane((1, nq))

        def stacked(q_ref, cos, sgn, da, o_ref, lse_ref_, h):
            cols = [slice(128 * j, 128 * j + 128) for j in range(pairs * h, pairs * (h + 1))]
            q4 = _stack_heads([_rope(q_ref[:, c], cos, sgn, 1.0) for c in cols])
            do4 = _stack_heads([da[:, c] for c in cols])
            lo = _lane((WINDOW, 128)) < HEAD_DIM
            deltas = []
            for c in cols:
                prod = o_ref[:, c] * da[:, c]
                deltas += [jnp.sum(jnp.where(lo, prod, 0.0), axis=1, keepdims=True),
                           jnp.sum(jnp.where(lo, 0.0, prod), axis=1, keepdims=True)]
            lse4 = jnp.concatenate([lse_ref_[:, group * h + r:group * h + r + 1] for r in range(group)], axis=0)
            return q4, do4, lse4, jnp.concatenate(deltas, axis=0)

        def scores_bwd(q4, do4, lse4, delta4, keys, vals, valid):
            sc = _dot(q4, keys, "nt") * ATTN_SCALE
            p = jnp.exp(jnp.where(valid, sc - lse4, MASKED))
            return p.astype(_CDT), (p * (_dot(do4, vals, "nt") - delta4) * ATTN_SCALE).astype(_CDT)

        for h in range(N_KV_HEADS):
            k2, v2 = _dup_head(kk, h).astype(_CDT), _dup_head(vv, h).astype(_CDT)
            q4, do4, lse4, delta4 = stacked(pc_ref, cos_c, sgn_c, da_c, ac_ref, lsec_ref, h)
            p, ds = scores_bwd(q4, do4, lse4, delta4, k2, v2, valid_c)
            for i, dq in enumerate(_unstack_heads(_dot(ds, k2, "nn"), pairs)):
                j = pairs * h + i
                dproj_ref[:, 128 * j:128 * j + 128] = _rope(dq, cos_c, sgn_c, -1.0).astype(dproj_ref.dtype)
            dk = _dot(ds, q4, "tn")
            dv = _dot(p, do4, "tn")
            dk_prev.append(dk_carry[h] + dk[0:WINDOW])
            dv_prev.append(dv_carry[h] + dv[0:WINDOW])
            dk_carry[h] = dk[WINDOW:2 * WINDOW]
            dv_carry[h] = dv[WINDOW:2 * WINDOW]
            sink4 = _per_head([sinks_ref[0, group * h + r] for r in range(group)])
            loss_sink = jnp.exp(sink4 - lse4) * delta4
            for r in range(group):
                dsinks = dsinks + jnp.where(head_lane == group * h + r,
                                            -jnp.sum(loss_sink[WINDOW * r:WINDOW * (r + 1)]), 0.0)
        _accumulate(dsinks_ref, dsinks, first)
        write_kv(dk_prev, dv_prev, cos_p, sgn_p)

        bg = pc_ref[:, dm.o_bg:dm.o_bg + cw]
        yc = yc_ref[...]
        dconv, dgc = _rms_bwd(dmc_ref[:, aw:d], bg * yc, gc_ref[...])
        _accumulate(dgc_ref, dgc, first)
        dproj_ref[:, dm.o_bg:dm.o_bg + cw] = (dconv * yc).astype(dproj_ref.dtype)
        dy = dconv * bg
        bg_n = pn_ref[:, dm.o_bg:dm.o_bg + cw]
        dconv_n, _ = _rms_bwd(dmn_ref[:, aw:d], bg_n * yn_ref[...], gc_ref[...])
        halo = jnp.where(has_next, dconv_n * bg_n, 0.0)
        dy1 = _shift_up(dy, halo, 1)
        dy2 = _shift_up(dy, halo, 2)
        dz = cw_ref[2:3, :] * dy + cw_ref[1:2, :] * dy1 + cw_ref[0:1, :] * dy2
        cg = pc_ref[:, dm.o_cg:dm.o_cg + cw]
        u = pc_ref[:, dm.o_u:dm.o_u + cw]
        dproj_ref[:, dm.o_cg:dm.o_cg + cw] = (dz * u).astype(dproj_ref.dtype)
        dproj_ref[:, dm.o_u:dm.o_u + cw] = (dz * cg).astype(dproj_ref.dtype)
        z = cg * u
        dcw = jnp.concatenate(
            [jnp.sum(z * t, axis=0, keepdims=True) for t in (dy2, dy1, dy)]
            + [jnp.zeros((V7X_SUBLANES - 3, cw), _F32)], axis=0)
        _accumulate(dcw_ref, dcw, first)

    at = lambda n: jnp.minimum(n, nb - 1)
    prev = lambda n: (jnp.maximum(at(n) - 1, 0), 0)
    cur = lambda n: (at(n), 0)
    done = lambda n: (jnp.maximum(n - 1, 0), 0)
    nxt8 = lambda n: (jnp.minimum((at(n) + 1) * (WINDOW // V7X_SUBLANES), s // V7X_SUBLANES - 1), 0)
    fixed = lambda n: (0, 0)
    blocks = WINDOW * (2 * inw * 4 + d * 4 + aw * 4 + cw * 4 + inw * 2 + 2 * KV_WIDTH * 2)
    carry = [pltpu.VMEM((N_KV_HEADS, WINDOW, 128), _F32), pltpu.VMEM((N_KV_HEADS, WINDOW, 128), _F32)]
    n_in, n_out = 15, 6
    ca = _CommArgs(list(comm), n_in, n_out)
    return pl.pallas_call(
        _carrying(body, n_in, n_out, nb + 1, ca, n_scratch=len(carry)), name="mixer_bwd", grid=(nb + 1,),
        in_specs=[pl.BlockSpec((WINDOW, inw), prev), pl.BlockSpec((WINDOW, inw), cur), pl.BlockSpec((V7X_SUBLANES, inw), nxt8),
                  pl.BlockSpec((WINDOW, 2 * V7X_LANES), prev), pl.BlockSpec((WINDOW, 2 * V7X_LANES), cur),
                  pl.BlockSpec((WINDOW, d), cur), pl.BlockSpec((V7X_SUBLANES, d), nxt8),
                  pl.BlockSpec((WINDOW, aw), cur), pl.BlockSpec((WINDOW, nq), cur),
                  pl.BlockSpec((WINDOW, cw), cur), pl.BlockSpec((V7X_SUBLANES, cw), nxt8),
                  pl.BlockSpec(memory_space=pltpu.SMEM),
                  pl.BlockSpec((1, aw), fixed), pl.BlockSpec((1, cw), fixed), pl.BlockSpec((V7X_SUBLANES, cw), fixed)]
        + [_ANY] * len(ca.operands),
        out_specs=[pl.BlockSpec((WINDOW, inw), cur), pl.BlockSpec((WINDOW, 2 * KV_WIDTH), done),
                   pl.BlockSpec((1, aw), fixed), pl.BlockSpec((1, cw), fixed),
                   pl.BlockSpec((1, nq), fixed), pl.BlockSpec((V7X_SUBLANES, cw), fixed)] + [_ANY] * len(ca.out_shape),
        out_shape=[jax.ShapeDtypeStruct((s, inw), _CDT), jax.ShapeDtypeStruct((s, 2 * KV_WIDTH), _CDT),
                   jax.ShapeDtypeStruct((1, aw), _F32), jax.ShapeDtypeStruct((1, cw), _F32),
                   jax.ShapeDtypeStruct((1, nq), _F32), jax.ShapeDtypeStruct((V7X_SUBLANES, cw), _F32)] + ca.out_shape,
        scratch_shapes=carry + ca.sems, input_output_aliases=ca.aliases,
        compiler_params=pltpu.CompilerParams(dimension_semantics=("arbitrary",), vmem_limit_bytes=_vmem_limit(blocks)),
    )(proj, proj, proj, rope, rope, dmixed, dmixed, attn, lse, y, y, sinks, g_attn, g_conv, conv_w8, *ca.operands)


def _position():
    return lax.axis_index("x"), lax.axis_index("y"), lax.axis_index("c")


def _linear(px, py, pc):
    return 4 * px + 2 * py + pc


def _comm_kernel(name, comm):
    ca = _CommArgs(list(comm), 0, 0)
    n_cin, n_cout = len(ca.operands), len(ca.out_shape)

    def body(*refs):
        cin, cout, sems = refs[:n_cin], refs[n_cin:n_cin + n_cout], refs[n_cin + n_cout:]
        ca.start(cin, cout, sems)
        ca.middle(cin, cout, sems)
        ca.finish(cin, cout, sems)

    return pl.pallas_call(
        body, name=name, out_shape=ca.out_shape, in_specs=[_ANY] * n_cin, out_specs=[_ANY] * n_cout,
        scratch_shapes=ca.sems, input_output_aliases=ca.aliases,
    )(*ca.operands)


def _gather_op(units):
    n = len(units)
    inputs, outputs, aliases = [], [], {}
    for shard, _, _, _ in units:
        inputs.append(shard)
        outputs.append(jax.ShapeDtypeStruct((N_DEV * shard.shape[0], shard.shape[1]), shard.dtype))
    for u, (_, buf, _, _) in enumerate(units):
        if buf is not None:
            aliases[len(inputs)] = u
            inputs.append(buf)

    def plan(ins, outs, sems, north):
        send_sems, recv_sems, local_sems = sems
        x, y, c = _position()
        me, sibling = (x, y, c), (x, y, 1 - c)
        xn, yn, dg = (1 - x, y), (x, 1 - y), (1 - x, 1 - y)
        via, to, k_via, k_other = (yn, xn, 2, 1) if north else (xn, yn, 1, 2)

        def rows(u, px, py, pc):
            shard, _, r0, r1 = units[u]
            return outs[u].at[pl.ds(pl.multiple_of(_linear(px, py, pc) * shard.shape[0] + r0, 16), r1 - r0), :]

        def own(u):
            _, _, r0, r1 = units[u]
            return ins[u].at[pl.ds(r0, r1 - r0), :]

        def copy(u, k, block, to_, src=None):
            return pltpu.make_async_remote_copy(
                src_ref=rows(u, *block) if src is None else src, dst_ref=rows(u, *block),
                send_sem=send_sems.at[u, k], recv_sem=recv_sems.at[u, k], device_id=to_, device_id_type=_MESH)

        us = range(n)
        return dict(
            mine=[pltpu.make_async_copy(own(u), rows(u, *me), local_sems.at[u]) for u in us],
            first=[cp for u in us for cp in (copy(u, 0, me, sibling, src=own(u)), copy(u, 1, me, (*xn, c), src=own(u)),
                                             copy(u, 2, me, (*yn, c), src=own(u)))],
            relay=[copy(u, 3, (*via, c), (*to, c)) for u in us],
            arrived={1: [copy(u, 1, (*xn, c), me) for u in us], 2: [copy(u, 2, (*yn, c), me) for u in us],
                     3: [copy(u, 3, (*dg, c), me) for u in us]},
            passed={1: [copy(u, 4, (*xn, c), sibling) for u in us], 2: [copy(u, 5, (*yn, c), sibling) for u in us],
                    3: [copy(u, 6, (*dg, c), sibling) for u in us]},
            rest=[cp for u in us for cp in (copy(u, 0, sibling, me), copy(u, 4, (*xn, 1 - c), me),
                                            copy(u, 5, (*yn, 1 - c), me), copy(u, 6, (*dg, 1 - c), me))],
            k_via=k_via, k_other=k_other)

    def land(p, k):
        for arrived, onward in zip(p["arrived"][k], p["passed"][k]):
            arrived.wait_recv()
            onward.start()

    def by_core(fn):
        c = lax.axis_index("c")
        for north in (True, False):
            pl.when(c == (1 if north else 0))(functools.partial(fn, north))

    def start(ins, outs, sems):
        p = plan(ins, outs, sems, True)
        for cp in p["mine"] + p["first"]:
            cp.start()

    def middle(ins, outs, sems):
        def go(north):
            p = plan(ins, outs, sems, north)
            land(p, p["k_via"])
            for cp in p["relay"]:
                cp.start()
            land(p, p["k_other"])
        by_core(go)

    def finish(ins, outs, sems):
        def go(north):
            p = plan(ins, outs, sems, north)
            land(p, 3)
            for cp in p["rest"]:
                cp.wait_recv()
            for cp in p["first"] + p["relay"] + [cp for k in (1, 2, 3) for cp in p["passed"][k]]:
                cp.wait_send()
            for cp in p["mine"]:
                cp.wait()
        by_core(go)

    sems = [pltpu.SemaphoreType.DMA((n, 7)), pltpu.SemaphoreType.DMA((n, 7)), pltpu.SemaphoreType.DMA((n,))]
    return _Comm(inputs, outputs, aliases, sems, start, finish, middle)


def _peers(x, y, c):
    out = []
    for k in range(1, N_DEV):
        fx, fy, fc = (k >> 2) & 1, (k >> 1) & 1, k & 1
        out.append((1 - x if fx else x, 1 - y if fy else y, 1 - c if fc else c))
    return out


def _exchange_op(partials):
    n = len(partials)
    outputs = [jax.ShapeDtypeStruct((4, p.shape[0] // N_DEV, p.shape[1]), p.dtype) for p in partials]

    def plan(ins, outs, sems):
        send_sems, recv_sems = sems
        x, y, c = _position()
        out = []
        for a in range(n):
            r = outs[a].shape[1]
            for ch in range(4):
                out.append(pltpu.make_async_remote_copy(
                    src_ref=ins[a].at[pl.ds(pl.multiple_of((2 * ch + 1 - c) * r, 16), r), :], dst_ref=outs[a].at[ch],
                    send_sem=send_sems.at[a, ch], recv_sem=recv_sems.at[a, ch], device_id=(x, y, 1 - c),
                    device_id_type=_MESH))
        return out

    def start(ins, outs, sems):
        for cp in plan(ins, outs, sems):
            cp.start()

    def finish(ins, outs, sems):
        copies = plan(ins, outs, sems)
        for cp in copies:
            cp.wait_recv()
        for cp in copies:
            cp.wait_send()

    sems = [pltpu.SemaphoreType.DMA((n, 4)), pltpu.SemaphoreType.DMA((n, 4))]
    return _Comm(list(partials), outputs, {}, sems, start, finish)


def _chip_send_op(units):
    n = len(units)
    inputs, outputs, aliases = [], [], {}
    for q, _, _, _ in units:
        inputs.append(q)
        outputs.append(jax.ShapeDtypeStruct(q.shape, q.dtype))
    for u, (_, buf, _, _) in enumerate(units):
        if buf is not None:
            aliases[len(inputs)] = u
            inputs.append(buf)

    def plan(ins, outs, sems):
        send_sems, recv_sems, local_sems = sems
        x, y, c = _position()
        my_chip = 2 * x + y
        chips = [(1 - x, y), (x, 1 - y), (1 - x, 1 - y)]
        mine, sends, arrivals = [], [], []
        for u, (_, _, r0, r1) in enumerate(units):
            span = pl.ds(r0, r1 - r0)
            mine.append(pltpu.make_async_copy(ins[u].at[my_chip, span, :], outs[u].at[my_chip, span, :], local_sems.at[u]))
            for k, (px, py) in enumerate(chips):
                sends.append(pltpu.make_async_remote_copy(
                    src_ref=ins[u].at[2 * px + py, span, :], dst_ref=outs[u].at[my_chip, span, :],
                    send_sem=send_sems.at[u, k], recv_sem=recv_sems.at[u, k], device_id=(px, py, c), device_id_type=_MESH))
                arrivals.append(pltpu.make_async_remote_copy(
                    src_ref=ins[u].at[my_chip, span, :], dst_ref=outs[u].at[2 * px + py, span, :],
                    send_sem=send_sems.at[u, k], recv_sem=recv_sems.at[u, k], device_id=(px, py, c), device_id_type=_MESH))
        return mine, sends, arrivals

    def start(ins, outs, sems):
        mine, sends, _ = plan(ins, outs, sems)
        for cp in mine + sends:
            cp.start()

    def finish(ins, outs, sems):
        mine, sends, arrivals = plan(ins, outs, sems)
        for cp in arrivals:
            cp.wait_recv()
        for cp in sends:
            cp.wait_send()
        for cp in mine:
            cp.wait()

    sems = [pltpu.SemaphoreType.DMA((n, 3)), pltpu.SemaphoreType.DMA((n, 3)), pltpu.SemaphoreType.DMA((n,))]
    return _Comm(inputs, outputs, aliases, sems, start, finish)


def _pair_sum(name, partial, received):
    _, rows, cols = received.shape
    tr = _pick(rows, (352, 288, 256, 128, 64, 32, 16))
    p4 = partial.reshape(4, 2, rows, cols)
    kind = jnp.reshape(lax.axis_index("c"), (1,)).astype(jnp.int32)

    def body(kind_ref, p_ref, r_ref, o_ref):
        o_ref[0] = (p_ref[0, 0].astype(_F32) + r_ref[0].astype(_F32)).astype(o_ref.dtype)

    return pl.pallas_call(
        body, name=name,
        grid_spec=pltpu.PrefetchScalarGridSpec(
            num_scalar_prefetch=1, grid=(4, rows // tr),
            in_specs=[pl.BlockSpec((1, 1, tr, cols), lambda ch, i, kind_ref: (ch, kind_ref[0], i, 0)),
                      pl.BlockSpec((1, tr, cols), lambda ch, i, kind_ref: (ch, i, 0))],
            out_specs=pl.BlockSpec((1, tr, cols), lambda ch, i, kind_ref: (ch, i, 0))),
        out_shape=jax.ShapeDtypeStruct(received.shape, received.dtype),
        compiler_params=pltpu.CompilerParams(dimension_semantics=("arbitrary", "arbitrary")),
    )(kind, p4, received)


def _all_reduce_small(name, v):
    rows = v.shape[0]

    def body(v_ref, out_ref, land_ref, send_sems, recv_sems):
        x, y, c = _position()
        me = _linear(x, y, c)
        peers = _peers(x, y, c)
        land_ref[me] = v_ref[...]
        sends = [pltpu.make_async_remote_copy(
            src_ref=v_ref, dst_ref=land_ref.at[me], send_sem=send_sems.at[k], recv_sem=recv_sems.at[k],
            device_id=peer, device_id_type=_MESH) for k, peer in enumerate(peers)]
        for cp in sends:
            cp.start()
        for k, peer in enumerate(peers):
            pltpu.make_async_remote_copy(
                src_ref=v_ref, dst_ref=land_ref.at[_linear(*peer)], send_sem=send_sems.at[k], recv_sem=recv_sems.at[k],
                device_id=peer, device_id_type=_MESH).wait_recv()
        for cp in sends:
            cp.wait_send()
        total = land_ref[0]
        for s in range(1, N_DEV):
            total = total + land_ref[s]
        out_ref[...] = total

    return pl.pallas_call(
        body, name=name, out_shape=jax.ShapeDtypeStruct(v.shape, _F32),
        in_specs=[pl.BlockSpec(memory_space=pltpu.VMEM)], out_specs=pl.BlockSpec(memory_space=pltpu.VMEM),
        scratch_shapes=[pltpu.VMEM((N_DEV, rows, V7X_LANES), _F32), pltpu.SemaphoreType.DMA((7,)), pltpu.SemaphoreType.DMA((7,))],
    )(v)


def _adamw(name, w, slots, m, v):
    rows, cols = w.shape
    n_slots = slots.shape[0]
    tr = _pick(rows, (176, 144, 128, 64, 32, 16, 8))

    def body(w_ref, s_ref, m_ref, v_ref, g_ref, d_ref, nm_ref, nv_ref):
        g = s_ref[0].astype(_F32)
        for k in range(1, n_slots):
            g = g + s_ref[k].astype(_F32)
        nm = ADAM_B1 * m_ref[...] + (1.0 - ADAM_B1) * g
        nv = ADAM_B2 * v_ref[...] + (1.0 - ADAM_B2) * (g * g)
        m_hat = nm / (1.0 - ADAM_B1 ** ADAM_STEP)
        v_hat = nv / (1.0 - ADAM_B2 ** ADAM_STEP)
        g_ref[...] = g
        d_ref[...] = -ADAM_LR * (m_hat / (jnp.sqrt(v_hat) + ADAM_EPS) + ADAM_WD * w_ref[...])
        nm_ref[...] = nm
        nv_ref[...] = nv

    spec = pl.BlockSpec((tr, cols), lambda i: (i, 0))
    blocks = 7 * tr * cols * 4 + _nbytes((n_slots, tr, cols), slots.dtype)
    return pl.pallas_call(
        body, name=name, grid=(rows // tr,),
        in_specs=[spec, pl.BlockSpec((n_slots, tr, cols), lambda i: (0, i, 0)), spec, spec], out_specs=[spec] * 4,
        out_shape=[jax.ShapeDtypeStruct((rows, cols), _F32)] * 4,
        compiler_params=pltpu.CompilerParams(dimension_semantics=("arbitrary",), vmem_limit_bytes=_vmem_limit(blocks)),
    )(w, slots, m, v)


def _pad_rows(a, rows):
    return jnp.pad(a, ((0, rows - a.shape[0]), (0, 0)))


def _pack(parts):
    rows, spans, at = [], [], 0
    for p in parts:
        p = p.reshape(-1)
        r = -(-p.shape[0] // V7X_LANES)
        rows.append(jnp.pad(p, (0, r * V7X_LANES - p.shape[0])).reshape(r, V7X_LANES))
        spans.append((at, r, p.shape[0]))
        at += r
    packed = jnp.concatenate(rows, axis=0)
    return _pad_rows(packed, -(-at // V7X_SUBLANES) * V7X_SUBLANES), spans


def _unpack(packed, spans, shapes):
    return [packed[at:at + r].reshape(-1)[:size].reshape(shape) for (at, r, size), shape in zip(spans, shapes)]


def kernel(x, positions, w_in, conv_w, sinks, g_attn, g_conv, w_out, ln1_g, ln1_b, w_gate, w_up, w_down, ln2_g, ln2_b, loss_target, m_w_in, m_conv_w, m_sinks, m_g_attn, m_g_conv, m_w_out, m_ln1_g, m_ln1_b, m_w_gate, m_w_up, m_w_down, m_ln2_g, m_ln2_b, v_w_in, v_conv_w, v_sinks, v_g_attn, v_g_conv, v_w_out, v_ln1_g, v_ln1_b, v_w_gate, v_w_up, v_w_down, v_ln2_g, v_ln2_b):
    _, s, d = x.shape
    d_ff = N_DEV * w_gate.shape[2]
    dm = _Dims(s, d, d_ff)
    aw, cw, nq, inw = dm.aw, dm.cw, dm.nq, dm.inw
    x2 = x[0]
    pos = positions[0].reshape(s, 1)
    inv_freq = ROPE_THETA ** (-jnp.arange(0, ROT_DIM, 2, dtype=_F32) / ROT_DIM)
    invf = jnp.tile(inv_freq, V7X_LANES // (ROT_DIM // 2)).reshape(1, V7X_LANES)

    conv_cols = conv_w.shape[2]
    sh_in, sh_out = w_in[0].T.astype(_CDT), w_out[0].astype(_CDT)
    sh_gate, sh_up, sh_down = w_gate[0].T.astype(_CDT), w_up[0].T.astype(_CDT), w_down[0].astype(_CDT)
    r_in, r_out, r_ff = sh_in.shape[0], sh_out.shape[0], sh_gate.shape[0]
    q_ff = r_ff // 4
    assert q_ff % 16 == 0
    def prepare_body(x_ref, pos_ref, invf_ref, xc_ref, rope_ref):
        xc_ref[...] = x_ref[...].astype(_CDT)
        cos, sgn = _rope_tables(pos_ref[...], invf_ref[...])
        rope_ref[:, 0:V7X_LANES] = cos
        rope_ref[:, V7X_LANES:2 * V7X_LANES] = sgn

    x_c, rope, w_in_t, conv_all = _row_kernel(
        "prepare_gather_w_in", prepare_body, [x2, pos], [invf], [((s, d), _CDT), ((s, 2 * V7X_LANES), _F32)], [],
        comm=[_gather_op([(sh_in, None, 0, r_in), (_pad_rows(conv_w[0], 16), None, 0, 16)])])
    conv_full = conv_all.reshape(N_DEV, 16, conv_cols)[:, :3, :].transpose(1, 0, 2).reshape(3, cw)
    conv_w8 = _pad_rows(conv_full, V7X_SUBLANES)

    tm = _pick(s, (1024, 512, 256, 128))
    tm2 = _pick(s, (2048, 1024, 512, 256, 128))
    tn_in = _pick(inw, (512, 256, 128))
    tn_ff = _pick(d_ff, (512, 256, 128))
    tr = _pick(s, (512, 256, 128))

    proj, w_out_f, w_gate_t = _matmul(
        "proj", [[(x_c, w_in_t, "nt")]], s, inw, d, tm2, tn_in, d, [],
        [((s, inw), _F32, (tm2, tn_in), _tile_ij)], _store_epilogue,
        comm=[_gather_op([(sh_out, None, 0, r_out), (sh_gate, None, 0, 2 * q_ff)])])
    mixed, attn, lse, y_conv, w_gate_t, w_up_t = _mixer_fwd(
        dm, proj, rope, sinks, g_attn, g_conv, conv_w8,
        comm=[_gather_op([(sh_gate, w_gate_t, 2 * q_ff, r_ff), (sh_up, None, 0, 2 * q_ff)])])

    def residual_epilogue(accs, ex, out, first):
        out[0][...] = DEEPNORM_ALPHA * ex[0][...] + accs[0]

    tn_d = _pick(d, (512,))
    r1, w_up_t = _matmul(
        "out_proj", [[(mixed, w_out_f, "nn")]], s, d, d, tm, tn_d, d, [(x2, (tm, tn_d), _tile_ij)],
        [((s, d), _F32, (tm, tn_d), _tile_ij)], residual_epilogue,
        comm=[_gather_op([(sh_up, w_up_t, 2 * q_ff, 3 * q_ff)])])
    h1, h1_c, w_up_t = _ln1_fwd_rows(r1, ln1_g, ln1_b, comm=[_gather_op([(sh_up, w_up_t, 3 * q_ff, r_ff)])])

    def swiglu_epilogue(accs, ex, out, first):
        gate_v, up_v = accs
        out[0][...] = gate_v
        out[1][...] = up_v
        out[2][...] = (gate_v * jax.nn.sigmoid(gate_v) * up_v).astype(_CDT)

    gate, up, act, w_down_f = _matmul(
        "gate_up", [[(h1_c, w_gate_t, "nt")], [(h1_c, w_up_t, "nt")]], s, d_ff, d, tm, tn_ff, d, [],
        [((s, d_ff), _F32, (tm, tn_ff), _tile_ij), ((s, d_ff), _F32, (tm, tn_ff), _tile_ij),
         ((s, d_ff), _CDT, (tm, tn_ff), _tile_ij)], swiglu_epilogue,
        comm=[_gather_op([(sh_down, None, 0, r_ff)])])

    (r2,) = _matmul("down", [[(act, w_down_f, "nn")]], s, d, d_ff, tm, tn_d, d_ff, [(h1, (tm, tn_d), _tile_ij)],
                    [((s, d), _F32, (tm, tn_d), _tile_ij)], residual_epilogue)
    dr2, dr2_c, loss_acc, d_ln2_g, d_ln2_b = _ln2_loss_bwd(r2, loss_target[0], ln2_g, ln2_b)

    def swiglu_bwd_epilogue(accs, ex, out, first):
        gate_v, up_v = ex[0][...], ex[1][...]
        sig = jax.nn.sigmoid(gate_v)
        out[0][...] = (accs[0] * up_v * (sig * (1.0 + gate_v * (1.0 - sig)))).astype(_CDT)
        out[1][...] = (accs[0] * (gate_v * sig)).astype(_CDT)

    dgate, dup = _matmul(
        "dact", [[(dr2_c, w_down_f, "nt")]], s, d_ff, d, tm2, tn_ff, d,
        [(gate, (tm2, tn_ff), _tile_ij), (up, (tm2, tn_ff), _tile_ij)],
        [((s, d_ff), _CDT, (tm2, tn_ff), _tile_ij), ((s, d_ff), _CDT, (tm2, tn_ff), _tile_ij)], swiglu_bwd_epilogue)
    def weight_grad(name, a, b, comm=()):
        rows = a.shape[1]
        tw, tn_w = _pick(rows, (512, 256, 128)), _pick(d, (1024, 512))
        return _matmul(name, [[(a, b, "tn")]], rows, d, s, tw, tn_w, s, [],
                       [((rows, d), _CDT, (tw, tn_w), _tile_ij)], _store_epilogue, comm=comm, j_outer=True)

    (dw_down,) = weight_grad("dw_down", act, dr2_c)
    dw_gate_t, x_down = weight_grad("dw_gate", dgate, h1_c, comm=[_exchange_op([dw_down])])
    q_down = _pair_sum("chip_sum_w_down", dw_down, x_down)
    dw_up_t, l_down, x_gate = weight_grad(
        "dw_up", dup, h1_c, comm=[_chip_send_op([(q_down, None, 0, 2 * q_ff)]), _exchange_op([dw_gate_t])])
    q_gate = _pair_sum("chip_sum_w_gate", dw_gate_t, x_gate)

    tn_h = _pick(d, (512,))
    dh1, l_down, l_gate, x_up = _matmul(
        "dh1", [[(dgate, w_gate_t, "nn"), (dup, w_up_t, "nn")]], s, d, d_ff, tr, tn_h, d_ff,
        [(dr2, (tr, tn_h), _tile_ij)], [((s, d), _F32, (tr, tn_h), _tile_ij)], residual_epilogue,
        comm=[_chip_send_op([(q_down, l_down, 2 * q_ff, r_ff), (q_gate, None, 0, r_ff)]), _exchange_op([dw_up_t])])
    q_up = _pair_sum("chip_sum_w_up", dw_up_t, x_up)
    dr1, dr1_c, d_ln1_g, d_ln1_b = _ln1_bwd_rows(dh1, r1, ln1_g)
    (dmixed,) = _matmul("dmixed", [[(dr1_c, w_out_f, "nt")]], s, d, d, tm2, tn_d, d, [],
                        [((s, d), _F32, (tm2, tn_d), _tile_ij)], _store_epilogue)
    (dw_out,) = weight_grad("dw_out", mixed, dr1_c)
    dproj, dkv, d_g_attn, d_g_conv, d_sinks, d_conv8, l_up, x_out = _mixer_bwd(
        dm, proj, rope, sinks, g_attn, g_conv, conv_w8, dmixed, attn, lse, y_conv,
        comm=[_chip_send_op([(q_up, None, 0, r_ff)]), _exchange_op([dw_out])])
    dproj = _patch_columns("dproj_kv", dproj, dkv, dm.o_k)
    q_out = _pair_sum("chip_sum_w_out", dw_out, x_out)
    dw_in_t, l_out = weight_grad("dw_in", dproj, x_c, comm=[_chip_send_op([(q_out, None, 0, r_out)])])
    (x_in,) = _comm_kernel("exchange_w_in", [_exchange_op([dw_in_t])])
    q_in = _pair_sum("chip_sum_w_in", dw_in_t, x_in)

    grad_x, l_in = _matmul("dx", [[(dproj, w_in_t, "nn")]], s, d, inw, tm, tn_d, inw,
                           [(dr1, (tm, tn_d), _tile_ij)], [((s, d), _F32, (tm, tn_d), _tile_ij)], residual_epilogue,
                           comm=[_chip_send_op([(q_in, None, 0, r_in)])])

    small_parts = [d_conv8[:3], d_sinks, d_g_attn, d_g_conv, d_ln1_g, d_ln1_b, d_ln2_g, d_ln2_b, loss_acc[0:1, 0:1]]
    packed, spans = _pack(small_parts)
    reduced = _unpack(_all_reduce_small("reduce_small", packed), spans, [p.shape for p in small_parts])
    g_conv_full, g_sinks, g_g_attn, g_g_conv, g_ln1_g, g_ln1_b, g_ln2_g, g_ln2_b, loss_sum = reduced
    me = _linear(*_position())
    g_conv_w = lax.dynamic_slice(g_conv_full, (0, me * conv_cols), (3, conv_cols))
    loss = loss_sum[0, 0]

    big = {"w_in": (w_in[0].T, l_in, m_w_in[0].T, v_w_in[0].T), "w_out": (w_out[0], l_out, m_w_out[0], v_w_out[0]),
           "w_gate": (w_gate[0].T, l_gate, m_w_gate[0].T, v_w_gate[0].T),
           "w_up": (w_up[0].T, l_up, m_w_up[0].T, v_w_up[0].T), "w_down": (w_down[0], l_down, m_w_down[0], v_w_down[0])}
    res = {nm: tuple(_adamw(f"adamw_{nm}", w, slots, m, v)) for nm, (w, slots, m, v) in big.items()}
    for nm in ("w_in", "w_gate", "w_up"):
        res[nm] = tuple(a.T for a in res[nm])
    small_names = ["conv_w", "sinks", "g_attn", "g_conv", "ln1_g", "ln1_b", "ln2_g", "ln2_b"]
    small_w = [conv_w, sinks, g_attn, g_conv, ln1_g, ln1_b, ln2_g, ln2_b]
    small_g = [g_conv_w[None], g_sinks, g_g_attn, g_g_conv, g_ln1_g, g_ln1_b, g_ln2_g, g_ln2_b]
    small_m = [m_conv_w, m_sinks, m_g_attn, m_g_conv, m_ln1_g, m_ln1_b, m_ln2_g, m_ln2_b]
    small_v = [v_conv_w, v_sinks, v_g_attn, v_g_conv, v_ln1_g, v_ln1_b, v_ln2_g, v_ln2_b]
    pw, sp = _pack(small_w)
    pg, _ = _pack(small_g)
    pm, _ = _pack(small_m)
    pv, _ = _pack(small_v)
    shapes = [w.shape for w in small_w]
    _, sd, sm, sv = [_unpack(p, sp, shapes) for p in _adamw("adamw_small", pw, pg[None], pm, pv)]
    for i, nm in enumerate(small_names):
        res[nm] = (small_g[i].reshape(shapes[i]), sd[i], sm[i], sv[i])

    order = ["w_in", "conv_w", "sinks", "g_attn", "g_conv", "w_out", "ln1_g", "ln1_b", "w_gate", "w_up", "w_down", "ln2_g", "ln2_b"]

    def lead(a, nm):
        return a[None] if nm in big else a

    return (loss, grad_x[None],
            *[lead(res[nm][0], nm) for nm in order], *[lead(res[nm][1], nm) for nm in order],
            *[lead(res[nm][2], nm) for nm in order], *[lead(res[nm][3], nm) for nm in order])
```

```python
import functools

import jax
import jax.numpy as jnp
from jax import lax
from jax.experimental import pallas as pl
from jax.experimental.pallas import tpu as pltpu

_F32 = jnp.float32
_CDT = jnp.bfloat16

HEAD_DIM = 64
WINDOW = 128
N_KV_HEADS = 4
KV_WIDTH = N_KV_HEADS * HEAD_DIM
ROT_DIM = HEAD_DIM // 4
ROPE_THETA = 500000.0
ATTN_SCALE = HEAD_DIM ** -0.5
DEPTH = 1
DEEPNORM_ALPHA = (2 * DEPTH) ** 0.25
LN_EPS = 1e-5
RMS_EPS = 1e-6
ADAM_LR = 0.001
ADAM_B1 = 0.9
ADAM_B2 = 0.999
ADAM_EPS = 1e-08
ADAM_WD = 0.01
ADAM_STEP = 10
N_DEV = 8
MASKED = -1e30

MIB = 1024 * 1024
V7X_VMEM_BYTES = 64 * MIB
V7X_LANES = 128
V7X_SUBLANES = 8
BODY_TEMPORARIES_BYTES = 16 * MIB
VMEM_LIMIT_FLOOR_BYTES = 32 * MIB
VMEM_LIMIT_CEILING_BYTES = V7X_VMEM_BYTES - 8 * MIB
_MESH = pl.DeviceIdType.MESH
_ANY = pl.BlockSpec(memory_space=pl.ANY)


def _vmem_limit(block_bytes, scratch_bytes=0):
    want = 2 * block_bytes + scratch_bytes + BODY_TEMPORARIES_BYTES
    return int(min(max(want, VMEM_LIMIT_FLOOR_BYTES), VMEM_LIMIT_CEILING_BYTES))


def _nbytes(shape, dtype):
    n = 1
    for s in shape:
        n *= s
    return n * jnp.dtype(dtype).itemsize


def _pick(n, candidates):
    for c in candidates:
        if n % c == 0:
            return c
    raise ValueError(f"no tile of {candidates} divides {n}")


_DOT_DIMS = {"nn": ((1,), (0,)), "nt": ((1,), (1,)), "tn": ((0,), (0,))}


def _dot(a, b, mode):
    return lax.dot_general(a.astype(_CDT), b.astype(_CDT), (_DOT_DIMS[mode], ((), ())),
                           preferred_element_type=_F32)


def _accumulate(ref, val, first):
    @pl.when(first)
    def _():
        ref[...] = val

    @pl.when(jnp.logical_not(first))
    def _():
        ref[...] += val


class _Comm:
    def __init__(self, inputs, outputs, aliases, sems, start, finish, middle=None):
        self.inputs, self.outputs, self.aliases, self.sems = inputs, outputs, aliases, sems
        self.start, self.finish, self.middle = start, finish, middle


def _middle_step(n_steps):
    return (2 * n_steps) // 3


class _CommArgs:
    def __init__(self, comms, n_in_before, n_out_before):
        self.comms, self.operands, self.out_shape, self.aliases, self.sems, self.at = comms, [], [], {}, [], []
        for cm in comms:
            self.at.append((len(self.operands), len(self.out_shape), len(self.sems)))
            for i_in, i_out in cm.aliases.items():
                self.aliases[n_in_before + len(self.operands) + i_in] = n_out_before + len(self.out_shape) + i_out
            self.operands += cm.inputs
            self.out_shape += cm.outputs
            self.sems += cm.sems

    def _each(self, in_refs, out_refs, sem_refs):
        for cm, (i0, o0, s0) in zip(self.comms, self.at):
            yield cm, (in_refs[i0:i0 + len(cm.inputs)], out_refs[o0:o0 + len(cm.outputs)], sem_refs[s0:s0 + len(cm.sems)])

    def start(self, in_refs, out_refs, sem_refs):
        for cm, refs in self._each(in_refs, out_refs, sem_refs):
            cm.start(*refs)

    def finish(self, in_refs, out_refs, sem_refs):
        for cm, refs in self._each(in_refs, out_refs, sem_refs):
            cm.finish(*refs)

    @property
    def has_middle(self):
        return any(cm.middle is not None for cm in self.comms)

    def middle(self, in_refs, out_refs, sem_refs):
        for cm, refs in self._each(in_refs, out_refs, sem_refs):
            if cm.middle is not None:
                cm.middle(*refs)


def _matmul(name, groups, m, n, k, tm, tn, tk, extras, outs, epilogue, comm=(), j_outer=False):
    assert m % tm == 0 and n % tn == 0 and k % tk == 0, (name, m, n, k, tm, tn, tk)
    nk = k // tk
    terms = [t for g in groups for t in g]
    operands, in_specs, block_bytes = [], [], 0

    def spec(blk, imap):
        return pl.BlockSpec(blk, (lambda g0, g1, kk: imap(g1, g0, kk)) if j_outer else imap)

    for a, b, mode in terms:
        assert a.shape == ((k, m) if mode == "tn" else (m, k)), (name, a.shape, mode)
        assert b.shape == ((n, k) if mode == "nt" else (k, n)), (name, b.shape, mode)
        if mode == "tn":
            a_blk, a_map = (tk, tm), (lambda i, j, kk: (kk, i))
        else:
            a_blk, a_map = (tm, tk), (lambda i, j, kk: (i, kk))
        if mode == "nt":
            b_blk, b_map = (tn, tk), (lambda i, j, kk: (j, kk))
        else:
            b_blk, b_map = (tk, tn), (lambda i, j, kk: (kk, j))
        operands += [a, b]
        in_specs += [spec(a_blk, a_map), spec(b_blk, b_map)]
        block_bytes += _nbytes(a_blk, a.dtype) + _nbytes(b_blk, b.dtype)
    for arr, blk, imap in extras:
        operands.append(arr)
        in_specs.append(spec(blk, lambda i, j, kk, imap=imap: imap(i, j)))
        block_bytes += _nbytes(blk, arr.dtype)
    out_shape, out_specs = [], []
    for shape, dtype, blk, imap in outs:
        out_shape.append(jax.ShapeDtypeStruct(shape, dtype))
        out_specs.append(spec(blk, lambda i, j, kk, imap=imap: imap(i, j)))
        block_bytes += _nbytes(blk, dtype)
    n_terms, n_extra, n_out, n_groups = len(terms), len(extras), len(outs), len(groups)
    scratch = [pltpu.VMEM((tm, tn), _F32) for _ in range(n_groups)] if nk > 1 else []
    ca = _CommArgs(list(comm), len(operands), n_out)
    n_cin, n_cout, n_acc = len(ca.operands), len(ca.out_shape), len(scratch)
    tiles = (m // tm, n // tn)
    grid = (tiles[1], tiles[0], nk) if j_outer else (tiles[0], tiles[1], nk)

    def body(*refs):
        refs = list(refs)
        term_refs = [refs.pop(0) for _ in range(2 * n_terms)]
        extra_refs = [refs.pop(0) for _ in range(n_extra)]
        cin_refs = [refs.pop(0) for _ in range(n_cin)]
        out_refs = [refs.pop(0) for _ in range(n_out)]
        cout_refs = [refs.pop(0) for _ in range(n_cout)]
        acc_refs = [refs.pop(0) for _ in range(n_acc)]
        sem_refs = refs
        g0, g1, kk = pl.program_id(0), pl.program_id(1), pl.program_id(2)
        first = jnp.logical_and(g0 == 0, g1 == 0)
        if comm:
            @pl.when(jnp.logical_and(first, kk == 0))
            def _():
                ca.start(cin_refs, cout_refs, sem_refs)
        if ca.has_middle:
            step = (g0 * grid[1] + g1) * nk + kk

            @pl.when(step == _middle_step(grid[0] * grid[1] * nk))
            def _():
                ca.middle(cin_refs, cout_refs, sem_refs)
        partial, t = [], 0
        for g in groups:
            s = None
            for _, _, mode in g:
                d = _dot(term_refs[2 * t][...], term_refs[2 * t + 1][...], mode)
                s = d if s is None else s + d
                t += 1
            partial.append(s)
        if nk == 1:
            epilogue(partial, extra_refs, out_refs, first)
        else:
            for acc, p in zip(acc_refs, partial):
                _accumulate(acc, p, kk == 0)

            @pl.when(kk == nk - 1)
            def _():
                epilogue([acc[...] for acc in acc_refs], extra_refs, out_refs, first)
        if comm:
            @pl.when(jnp.logical_and(jnp.logical_and(g0 == grid[0] - 1, g1 == grid[1] - 1), kk == nk - 1))
            def _():
                ca.finish(cin_refs, cout_refs, sem_refs)

    res = pl.pallas_call(
        body, name=name, grid=grid,
        in_specs=in_specs + [_ANY] * n_cin, out_specs=out_specs + [_ANY] * n_cout,
        out_shape=out_shape + ca.out_shape, scratch_shapes=scratch + ca.sems, input_output_aliases=ca.aliases,
        compiler_params=pltpu.CompilerParams(
            dimension_semantics=("arbitrary", "arbitrary", "arbitrary"),
            vmem_limit_bytes=_vmem_limit(block_bytes, n_groups * tm * tn * 4 if nk > 1 else 0)),
    )(*operands, *ca.operands)
    return list(res[:n_out]) + list(res[n_out:])


def _store_epilogue(accs, extra_refs, out_refs, first):
    for acc, ref in zip(accs, out_refs):
        ref[...] = acc.astype(ref.dtype)


def _tile_ij(i, j):
    return (i, j)


def _row_i(i, j):
    return (i, 0)


def _whole(i, j):
    return (0, 0)


def _mean(v):
    return jnp.mean(v, axis=-1, keepdims=True)


def _ln_fwd(r, g, b):
    xc = r - _mean(r)
    rstd = lax.rsqrt(_mean(xc * xc) + LN_EPS)
    xhat = xc * rstd
    return xhat * g + b, xhat, rstd


def _ln_bwd(dy, xhat, rstd, g):
    dxh = dy * g
    dr = rstd * (dxh - _mean(dxh) - xhat * _mean(dxh * xhat))
    return dr, jnp.sum(dy * xhat, axis=0, keepdims=True), jnp.sum(dy, axis=0, keepdims=True)


def _rms_fwd(a, g):
    rstd = lax.rsqrt(_mean(a * a) + RMS_EPS)
    return a * rstd * g


def _rms_bwd(dm, a, g):
    rstd = lax.rsqrt(_mean(a * a) + RMS_EPS)
    nhat = a * rstd
    dn = dm * g
    da = rstd * (dn - nhat * _mean(dn * nhat))
    return da, jnp.sum(dm * nhat, axis=0, keepdims=True)


def _lane(shape):
    return lax.broadcasted_iota(jnp.int32, shape, 1)


def _row(shape):
    return lax.broadcasted_iota(jnp.int32, shape, 0)


def _rope_tables(pos, invf):
    ang = pos.astype(_F32) * invf
    lane = _lane(ang.shape)
    in_rot = (lane % HEAD_DIM) < ROT_DIM
    first = (lane % ROT_DIM) < ROT_DIM // 2
    cos = jnp.where(in_rot, jnp.cos(ang), 1.0)
    sin = jnp.sin(ang)
    sgn = jnp.where(in_rot, jnp.where(first, -sin, sin), 0.0)
    return cos, sgn


def _rope(t, cos, sgn, sign):
    half = ROT_DIM // 2
    first = (_lane(t.shape) % ROT_DIM) < half
    partner = jnp.where(first, pltpu.roll(t, V7X_LANES - half, 1), pltpu.roll(t, half, 1))
    return t * cos + partner * (sgn * sign)


def _dup_head(t, h):
    g = t[:, 128 * (h // 2):128 * (h // 2) + 128]
    r = pltpu.roll(g, HEAD_DIM, 1)
    lo = _lane(g.shape) < HEAD_DIM
    return jnp.where(lo, g, r) if h % 2 == 0 else jnp.where(lo, r, g)


def _fold_halves(t):
    return t + pltpu.roll(t, HEAD_DIM, 1)


def _halves(t):
    lo = _lane(t.shape) < HEAD_DIM
    zero = jnp.zeros_like(t)
    return jnp.where(lo, t, zero), jnp.where(lo, zero, t)


def _band_mask(n_heads, n_keys, first_block):
    shape = (n_heads * WINDOW, n_keys)
    i = jnp.bitwise_and(_row(shape), WINDOW - 1)
    j = _lane(shape)
    valid = jnp.logical_and(j >= i + 1, j <= i + WINDOW)
    if first_block is not None:
        valid = jnp.logical_and(valid, jnp.logical_or(j >= WINDOW, jnp.logical_not(first_block)))
    return valid


def _stack_heads(pairs):
    return jnp.concatenate([half for t in pairs for half in _halves(t)], axis=0).astype(_CDT)


def _unstack_heads(t, n_pairs):
    lo = _lane((WINDOW, 128)) < HEAD_DIM
    return [jnp.where(lo, t[2 * WINDOW * i:2 * WINDOW * i + WINDOW], t[2 * WINDOW * i + WINDOW:2 * WINDOW * (i + 1)])
            for i in range(n_pairs)]


def _per_head(values):
    n_rows = len(values) * WINDOW
    block = jnp.right_shift(_row((n_rows, 1)), WINDOW.bit_length() - 1)
    out = jnp.zeros((n_rows, 1), _F32)
    for k, v in enumerate(values):
        out = jnp.where(block == k, v, out)
    return out


def _shift_down(z, halo, k):
    out = pltpu.roll(z, k, 0)
    r = _row(z.shape)
    for t in range(k):
        out = jnp.where(r == t, halo[V7X_SUBLANES - k + t:V7X_SUBLANES - k + t + 1, :], out)
    return out


def _shift_up(z, halo, k):
    rows = z.shape[0]
    out = pltpu.roll(z, rows - k, 0)
    r = _row(z.shape)
    for t in range(k):
        out = jnp.where(r == rows - k + t, halo[t:t + 1, :], out)
    return out


class _Dims:
    def __init__(self, s, d, d_ff):
        self.s, self.d, self.d_ff = s, d, d_ff
        self.aw = d // 2
        self.cw = d - self.aw
        self.nq = self.aw // HEAD_DIM
        self.group = self.nq // N_KV_HEADS
        assert self.group % 2 == 0, "a 128-lane pair of query heads must share its kv head"
        self.inw = self.aw + 2 * KV_WIDTH + 3 * self.cw
        self.o_k = self.aw
        self.o_v = self.aw + KV_WIDTH
        self.o_cg = self.aw + 2 * KV_WIDTH
        self.o_bg = self.o_cg + self.cw
        self.o_u = self.o_bg + self.cw
        self.nb = s // WINDOW
        assert s % WINDOW == 0


def _carrying(body, n_in, n_out, n_steps, ca, n_scratch=0):
    n_cin, n_cout = len(ca.operands), len(ca.out_shape)

    def wrapped(*refs):
        refs = list(refs)
        in_refs = [refs.pop(0) for _ in range(n_in)]
        cin_refs = [refs.pop(0) for _ in range(n_cin)]
        out_refs = [refs.pop(0) for _ in range(n_out)]
        cout_refs = [refs.pop(0) for _ in range(n_cout)]
        scratch_refs = [refs.pop(0) for _ in range(n_scratch)]
        if ca.comms:
            @pl.when(pl.program_id(0) == 0)
            def _():
                ca.start(cin_refs, cout_refs, refs)
        if ca.has_middle:
            @pl.when(pl.program_id(0) == _middle_step(n_steps))
            def _():
                ca.middle(cin_refs, cout_refs, refs)
        body(*in_refs, *out_refs, *scratch_refs)
        if ca.comms:
            @pl.when(pl.program_id(0) == n_steps - 1)
            def _():
                ca.finish(cin_refs, cout_refs, refs)

    return wrapped


def _row_kernel(name, body, rows_in, vecs_in, rows_out, vecs_out, comm=()):
    s = rows_in[0].shape[0]
    tr = _pick(s, (256, 128))
    row = lambda a: pl.BlockSpec((tr, a[1] if isinstance(a, tuple) else a.shape[1]), lambda i: (i, 0))
    vec = lambda shape: pl.BlockSpec(tuple(shape), lambda i: (0, 0))
    n_in, n_out = len(rows_in) + len(vecs_in), len(rows_out) + len(vecs_out)
    ca = _CommArgs(list(comm), n_in, n_out)
    blocks = sum(_nbytes((tr, a.shape[1]), a.dtype) for a in rows_in) + sum(_nbytes((tr, sh[1]), dt) for sh, dt in rows_out)
    res = pl.pallas_call(
        _carrying(body, n_in, n_out, s // tr, ca), name=name, grid=(s // tr,),
        in_specs=[row(a) for a in rows_in] + [vec(v.shape) for v in vecs_in] + [_ANY] * len(ca.operands),
        out_specs=[row(sh) for sh, _ in rows_out] + [vec(sh) for sh, _ in vecs_out] + [_ANY] * len(ca.out_shape),
        out_shape=[jax.ShapeDtypeStruct(sh, dt) for sh, dt in list(rows_out) + list(vecs_out)] + ca.out_shape,
        scratch_shapes=ca.sems, input_output_aliases=ca.aliases,
        compiler_params=pltpu.CompilerParams(dimension_semantics=("arbitrary",), vmem_limit_bytes=_vmem_limit(blocks)),
    )(*rows_in, *vecs_in, *ca.operands)
    return list(res)


def _ln2_loss_bwd(r2, target, gain, bias, comm=()):
    s, d = r2.shape

    def body(r_ref, t_ref, g_ref, b_ref, dr_ref, drc_ref, loss_ref, dg_ref, db_ref):
        first = pl.program_id(0) == 0
        yv, xhat, rstd = _ln_fwd(r_ref[...], g_ref[...], b_ref[...])
        err = yv - t_ref[...]
        dr2, dg, db = _ln_bwd(err * (1.0 / d), xhat, rstd, g_ref[...])
        dr_ref[...] = dr2
        drc_ref[...] = dr2.astype(_CDT)
        _accumulate(loss_ref, jnp.zeros(loss_ref.shape, _F32) + 0.5 * jnp.sum(err * err) * (1.0 / d), first)
        _accumulate(dg_ref, dg, first)
        _accumulate(db_ref, db, first)

    return _row_kernel("ln2_loss_bwd", body, [r2, target], [gain, bias], [((s, d), _F32), ((s, d), _CDT)],
                       [((V7X_SUBLANES, V7X_LANES), _F32), ((1, d), _F32), ((1, d), _F32)], comm)


def _ln1_fwd_rows(r1, gain, bias, comm=()):
    s, d = r1.shape

    def body(r_ref, g_ref, b_ref, h_ref, hc_ref):
        h1, _, _ = _ln_fwd(r_ref[...], g_ref[...], b_ref[...])
        h_ref[...] = h1
        hc_ref[...] = h1.astype(_CDT)

    return _row_kernel("ln1", body, [r1], [gain, bias], [((s, d), _F32), ((s, d), _CDT)], [], comm)


def _ln1_bwd_rows(dh1, r1, gain, comm=()):
    s, d = dh1.shape

    def body(dh_ref, r_ref, g_ref, dr_ref, drc_ref, dg_ref, db_ref):
        first = pl.program_id(0) == 0
        _, xhat, rstd = _ln_fwd(r_ref[...], g_ref[...], 0.0)
        dr1, dg, db = _ln_bwd(dh_ref[...], xhat, rstd, g_ref[...])
        dr_ref[...] = dr1
        drc_ref[...] = dr1.astype(_CDT)
        _accumulate(dg_ref, dg, first)
        _accumulate(db_ref, db, first)

    return _row_kernel("ln1_bwd", body, [dh1, r1], [gain], [((s, d), _F32), ((s, d), _CDT)],
                       [((1, d), _F32), ((1, d), _F32)], comm)


def _mixer_fwd(dm, proj, rope, sinks, g_attn, g_conv, conv_w8, comm=()):
    s, d, aw, cw, nq, inw, nb = dm.s, dm.d, dm.aw, dm.cw, dm.nq, dm.inw, dm.nb

    def body(pp_ref, pc_ref, ropep_ref, ropec_ref, sinks_ref, ga_ref, gc_ref, cw_ref,
             mixed_ref, attn_ref, lse_ref, y_ref):
        n = pl.program_id(0)
        cos_c, sgn_c = ropec_ref[:, 0:V7X_LANES], ropec_ref[:, V7X_LANES:2 * V7X_LANES]
        cos_p, sgn_p = ropep_ref[:, 0:V7X_LANES], ropep_ref[:, V7X_LANES:2 * V7X_LANES]
        kk = jnp.concatenate(
            [jnp.concatenate([_rope(ref[:, dm.o_k + 128 * g:dm.o_k + 128 * g + 128], c, sg, 1.0)
                              for g in range(KV_WIDTH // 128)], axis=1)
             for ref, c, sg in ((pp_ref, cos_p, sgn_p), (pc_ref, cos_c, sgn_c))], axis=0)
        vv = jnp.concatenate([pp_ref[:, dm.o_v:dm.o_v + KV_WIDTH], pc_ref[:, dm.o_v:dm.o_v + KV_WIDTH]], axis=0)
        group, pairs = dm.group, dm.group // 2
        valid = _band_mask(group, 2 * WINDOW, n == 0)
        for h in range(N_KV_HEADS):
            k2, v2 = _dup_head(kk, h).astype(_CDT), _dup_head(vv, h).astype(_CDT)
            q4 = _stack_heads([_rope(pc_ref[:, 128 * j:128 * j + 128], cos_c, sgn_c, 1.0)
                               for j in range(pairs * h, pairs * (h + 1))])
            sc = jnp.where(valid, _dot(q4, k2, "nt") * ATTN_SCALE, MASKED)
            sink = _per_head([sinks_ref[0, group * h + r] for r in range(group)])
            mx = jnp.maximum(jnp.max(sc, axis=1, keepdims=True), sink)
            p = jnp.exp(sc - mx)
            den = jnp.sum(p, axis=1, keepdims=True) + jnp.exp(sink - mx)
            out = _unstack_heads(_dot(p / den, v2, "nn"), pairs)
            lse = mx + jnp.log(den)
            for r in range(group):
                lse_ref[:, group * h + r:group * h + r + 1] = lse[WINDOW * r:WINDOW * (r + 1)]
            for i in range(pairs):
                j = pairs * h + i
                attn_ref[:, 128 * j:128 * j + 128] = out[i]
        mixed_ref[:, 0:aw] = _rms_fwd(attn_ref[...], ga_ref[...]).astype(mixed_ref.dtype)

        z = pc_ref[:, dm.o_cg:dm.o_cg + cw] * pc_ref[:, dm.o_u:dm.o_u + cw]
        top = WINDOW - V7X_SUBLANES
        halo = pp_ref[top:WINDOW, dm.o_cg:dm.o_cg + cw] * pp_ref[top:WINDOW, dm.o_u:dm.o_u + cw]
        halo = jnp.where(n == 0, jnp.zeros_like(halo), halo)
        y = cw_ref[0:1, :] * _shift_down(z, halo, 2) + cw_ref[1:2, :] * _shift_down(z, halo, 1) + cw_ref[2:3, :] * z
        y_ref[...] = y
        conv = pc_ref[:, dm.o_bg:dm.o_bg + cw] * y
        mixed_ref[:, aw:d] = _rms_fwd(conv, gc_ref[...]).astype(mixed_ref.dtype)

    prev = lambda n: (jnp.maximum(n - 1, 0), 0)
    cur = lambda n: (n, 0)
    fixed = lambda n: (0, 0)
    blocks = 2 * WINDOW * inw * 4 + WINDOW * (d * 2 + aw * 4 + cw * 4 + nq * 4)
    ca = _CommArgs(list(comm), 8, 4)
    return pl.pallas_call(
        _carrying(body, 8, 4, nb, ca), name="mixer_fwd", grid=(nb,),
        in_specs=[pl.BlockSpec((WINDOW, inw), prev), pl.BlockSpec((WINDOW, inw), cur),
                  pl.BlockSpec((WINDOW, 2 * V7X_LANES), prev), pl.BlockSpec((WINDOW, 2 * V7X_LANES), cur),
                  pl.BlockSpec(memory_space=pltpu.SMEM),
                  pl.BlockSpec((1, aw), fixed), pl.BlockSpec((1, cw), fixed), pl.BlockSpec((V7X_SUBLANES, cw), fixed)]
        + [_ANY] * len(ca.operands),
        out_specs=[pl.BlockSpec((WINDOW, d), cur), pl.BlockSpec((WINDOW, aw), cur),
                   pl.BlockSpec((WINDOW, nq), cur), pl.BlockSpec((WINDOW, cw), cur)] + [_ANY] * len(ca.out_shape),
        out_shape=[jax.ShapeDtypeStruct((s, d), _CDT), jax.ShapeDtypeStruct((s, aw), _F32),
                   jax.ShapeDtypeStruct((s, nq), _F32), jax.ShapeDtypeStruct((s, cw), _F32)] + ca.out_shape,
        scratch_shapes=ca.sems, input_output_aliases=ca.aliases,
        compiler_params=pltpu.CompilerParams(dimension_semantics=("arbitrary",), vmem_limit_bytes=_vmem_limit(blocks)),
    )(proj, proj, rope, rope, sinks, g_attn, g_conv, conv_w8, *ca.operands)


def _patch_columns(name, a, part, offset):
    s, pw = part.shape
    assert offset % pw == 0 and pw % V7X_LANES == 0
    tr = _pick(s, (512, 256, 128))

    def body(a_ref, p_ref, o_ref):
        del a_ref
        o_ref[...] = p_ref[...]

    return pl.pallas_call(
        body, name=name, grid=(s // tr,),
        in_specs=[_ANY, pl.BlockSpec((tr, pw), lambda i: (i, 0))],
        out_specs=pl.BlockSpec((tr, pw), lambda i: (i, offset // pw)),
        out_shape=jax.ShapeDtypeStruct(a.shape, a.dtype), input_output_aliases={0: 0},
        compiler_params=pltpu.CompilerParams(dimension_semantics=("arbitrary",)),
    )(a, part)


def _mixer_bwd(dm, proj, rope, sinks, g_attn, g_conv, conv_w8, dmixed, attn, lse, y, comm=()):
    s, d, aw, cw, nq, inw, nb = dm.s, dm.d, dm.aw, dm.cw, dm.nq, dm.inw, dm.nb

    def body(pp_ref, pc_ref, pn_ref, ropep_ref, ropec_ref, dmc_ref, dmn_ref, ac_ref,
             lsec_ref, yc_ref, yn_ref, sinks_ref, ga_ref, gc_ref, cw_ref,
             dproj_ref, dkv_ref, dga_ref, dgc_ref, dsinks_ref, dcw_ref, dk_carry, dv_carry):
        n = pl.program_id(0)
        first = n == 0
        live = n < nb
        has_next = n < nb - 1
        cos_p, sgn_p = ropep_ref[:, 0:V7X_LANES], ropep_ref[:, V7X_LANES:2 * V7X_LANES]
        cos_c, sgn_c = ropec_ref[:, 0:V7X_LANES], ropec_ref[:, V7X_LANES:2 * V7X_LANES]

        @pl.when(first)
        def _():
            dk_carry[...] = jnp.zeros(dk_carry.shape, _F32)
            dv_carry[...] = jnp.zeros(dv_carry.shape, _F32)

        def write_kv(dk2, dv2, cos, sgn):
            lo = _lane((WINDOW, 128)) < HEAD_DIM
            for g in range(KV_WIDTH // 128):
                dk = jnp.where(lo, _fold_halves(dk2[2 * g]), _fold_halves(dk2[2 * g + 1]))
                dv = jnp.where(lo, _fold_halves(dv2[2 * g]), _fold_halves(dv2[2 * g + 1]))
                dkv_ref[:, 128 * g:128 * g + 128] = _rope(dk, cos, sgn, -1.0).astype(dkv_ref.dtype)
                dkv_ref[:, KV_WIDTH + 128 * g:KV_WIDTH + 128 * g + 128] = dv.astype(dkv_ref.dtype)

        @pl.when(jnp.logical_not(live))
        def _():
            write_kv([dk_carry[h] for h in range(N_KV_HEADS)], [dv_carry[h] for h in range(N_KV_HEADS)], cos_c, sgn_c)

        @pl.when(live)
        def _():
            block_step(pp_ref, pc_ref, pn_ref, dmc_ref, dmn_ref, ac_ref, lsec_ref, yc_ref, yn_ref, sinks_ref, ga_ref,
                       gc_ref, cw_ref, dproj_ref, dga_ref, dgc_ref, dsinks_ref, dcw_ref, dk_carry, dv_carry,
                       first, has_next, cos_p, sgn_p, cos_c, sgn_c, write_kv)

    def block_step(pp_ref, pc_ref, pn_ref, dmc_ref, dmn_ref, ac_ref, lsec_ref, yc_ref, yn_ref, sinks_ref, ga_ref,
                   gc_ref, cw_ref, dproj_ref, dga_ref, dgc_ref, dsinks_ref, dcw_ref, dk_carry, dv_carry,
                   first, has_next, cos_p, sgn_p, cos_c, sgn_c, write_kv):
        da_c, dga = _rms_bwd(dmc_ref[:, 0:aw], ac_ref[...], ga_ref[...])
        _accumulate(dga_ref, dga, first)
        kk = jnp.concatenate(
            [jnp.concatenate([_rope(ref[:, dm.o_k + 128 * g:dm.o_k + 128 * g + 128], c, sg, 1.0)
                              for g in range(KV_WIDTH // 128)], axis=1)
             for ref, c, sg in ((pp_ref, cos_p, sgn_p), (pc_ref, cos_c, sgn_c))], axis=0)
        vv = jnp.concatenate([pp_ref[:, dm.o_v:dm.o_v + KV_WIDTH], pc_ref[:, dm.o_v:dm.o_v + KV_WIDTH]], axis=0)
        group, pairs = dm.group, dm.group // 2
        valid_c = _band_mask(group, 2 * WINDOW, first)
        dk_prev, dv_prev = [], []
        dsinks = jnp.zeros((1, nq), _F32)
        head_lane = _lane((1, nq))

        def stacked(q_ref, cos, sgn, da, o_ref, lse_ref_, h):
            cols = [slice(128 * j, 128 * j + 128) for j in range(pairs * h, pairs * (h + 1))]
            q4 = _stack_heads([_rope(q_ref[:, c], cos, sgn, 1.0) for c in cols])
            do4 = _stack_heads([da[:, c] for c in cols])
            lo = _lane((WINDOW, 128)) < HEAD_DIM
            deltas = []
            for c in cols:
                prod = o_ref[:, c] * da[:, c]
                deltas += [jnp.sum(jnp.where(lo, prod, 0.0), axis=1, keepdims=True),
                           jnp.sum(jnp.where(lo, 0.0, prod), axis=1, keepdims=True)]
            lse4 = jnp.concatenate([lse_ref_[:, group * h + r:group * h + r + 1] for r in range(group)], axis=0)
            return q4, do4, lse4, jnp.concatenate(deltas, axis=0)

        def scores_bwd(q4, do4, lse4, delta4, keys, vals, valid):
            sc = _dot(q4, keys, "nt") * ATTN_SCALE
            p = jnp.exp(jnp.where(valid, sc - lse4, MASKED))
            return p.astype(_CDT), (p * (_dot(do4, vals, "nt") - delta4) * ATTN_SCALE).astype(_CDT)

        for h in range(N_KV_HEADS):
            k2, v2 = _dup_head(kk, h).astype(_CDT), _dup_head(vv, h).astype(_CDT)
            q4, do4, lse4, delta4 = stacked(pc_ref, cos_c, sgn_c, da_c, ac_ref, lsec_ref, h)
            p, ds = scores_bwd(q4, do4, lse4, delta4, k2, v2, valid_c)
            for i, dq in enumerate(_unstack_heads(_dot(ds, k2, "nn"), pairs)):
                j = pairs * h + i
                dproj_ref[:, 128 * j:128 * j + 128] = _rope(dq, cos_c, sgn_c, -1.0).astype(dproj_ref.dtype)
            dk = _dot(ds, q4, "tn")
            dv = _dot(p, do4, "tn")
            dk_prev.append(dk_carry[h] + dk[0:WINDOW])
            dv_prev.append(dv_carry[h] + dv[0:WINDOW])
            dk_carry[h] = dk[WINDOW:2 * WINDOW]
            dv_carry[h] = dv[WINDOW:2 * WINDOW]
            sink4 = _per_head([sinks_ref[0, group * h + r] for r in range(group)])
            loss_sink = jnp.exp(sink4 - lse4) * delta4
            for r in range(group):
                dsinks = dsinks + jnp.where(head_lane == group * h + r,
                                            -jnp.sum(loss_sink[WINDOW * r:WINDOW * (r + 1)]), 0.0)
        _accumulate(dsinks_ref, dsinks, first)
        write_kv(dk_prev, dv_prev, cos_p, sgn_p)

        bg = pc_ref[:, dm.o_bg:dm.o_bg + cw]
        yc = yc_ref[...]
        dconv, dgc = _rms_bwd(dmc_ref[:, aw:d], bg * yc, gc_ref[...])
        _accumulate(dgc_ref, dgc, first)
        dproj_ref[:, dm.o_bg:dm.o_bg + cw] = (dconv * yc).astype(dproj_ref.dtype)
        dy = dconv * bg
        bg_n = pn_ref[:, dm.o_bg:dm.o_bg + cw]
        dconv_n, _ = _rms_bwd(dmn_ref[:, aw:d], bg_n * yn_ref[...], gc_ref[...])
        halo = jnp.where(has_next, dconv_n * bg_n, 0.0)
        dy1 = _shift_up(dy, halo, 1)
        dy2 = _shift_up(dy, halo, 2)
        dz = cw_ref[2:3, :] * dy + cw_ref[1:2, :] * dy1 + cw_ref[0:1, :] * dy2
        cg = pc_ref[:, dm.o_cg:dm.o_cg + cw]
        u = pc_ref[:, dm.o_u:dm.o_u + cw]
        dproj_ref[:, dm.o_cg:dm.o_cg + cw] = (dz * u).astype(dproj_ref.dtype)
        dproj_ref[:, dm.o_u:dm.o_u + cw] = (dz * cg).astype(dproj_ref.dtype)
        z = cg * u
        dcw = jnp.concatenate(
            [jnp.sum(z * t, axis=0, keepdims=True) for t in (dy2, dy1, dy)]
            + [jnp.zeros((V7X_SUBLANES - 3, cw), _F32)], axis=0)
        _accumulate(dcw_ref, dcw, first)

    at = lambda n: jnp.minimum(n, nb - 1)
    prev = lambda n: (jnp.maximum(at(n) - 1, 0), 0)
    cur = lambda n: (at(n), 0)
    done = lambda n: (jnp.maximum(n - 1, 0), 0)
    nxt8 = lambda n: (jnp.minimum((at(n) + 1) * (WINDOW // V7X_SUBLANES), s // V7X_SUBLANES - 1), 0)
    fixed = lambda n: (0, 0)
    blocks = WINDOW * (2 * inw * 4 + d * 4 + aw * 4 + cw * 4 + inw * 2 + 2 * KV_WIDTH * 2)
    carry = [pltpu.VMEM((N_KV_HEADS, WINDOW, 128), _F32), pltpu.VMEM((N_KV_HEADS, WINDOW, 128), _F32)]
    n_in, n_out = 15, 6
    ca = _CommArgs(list(comm), n_in, n_out)
    return pl.pallas_call(
        _carrying(body, n_in, n_out, nb + 1, ca, n_scratch=len(carry)), name="mixer_bwd", grid=(nb + 1,),
        in_specs=[pl.BlockSpec((WINDOW, inw), prev), pl.BlockSpec((WINDOW, inw), cur), pl.BlockSpec((V7X_SUBLANES, inw), nxt8),
                  pl.BlockSpec((WINDOW, 2 * V7X_LANES), prev), pl.BlockSpec((WINDOW, 2 * V7X_LANES), cur),
                  pl.BlockSpec((WINDOW, d), cur), pl.BlockSpec((V7X_SUBLANES, d), nxt8),
                  pl.BlockSpec((WINDOW, aw), cur), pl.BlockSpec((WINDOW, nq), cur),
                  pl.BlockSpec((WINDOW, cw), cur), pl.BlockSpec((V7X_SUBLANES, cw), nxt8),
                  pl.BlockSpec(memory_space=pltpu.SMEM),
                  pl.BlockSpec((1, aw), fixed), pl.BlockSpec((1, cw), fixed), pl.BlockSpec((V7X_SUBLANES, cw), fixed)]
        + [_ANY] * len(ca.operands),
        out_specs=[pl.BlockSpec((WINDOW, inw), cur), pl.BlockSpec((WINDOW, 2 * KV_WIDTH), done),
                   pl.BlockSpec((1, aw), fixed), pl.BlockSpec((1, cw), fixed),
                   pl.BlockSpec((1, nq), fixed), pl.BlockSpec((V7X_SUBLANES, cw), fixed)] + [_ANY] * len(ca.out_shape),
        out_shape=[jax.ShapeDtypeStruct((s, inw), _CDT), jax.ShapeDtypeStruct((s, 2 * KV_WIDTH), _CDT),
                   jax.ShapeDtypeStruct((1, aw), _F32), jax.ShapeDtypeStruct((1, cw), _F32),
                   jax.ShapeDtypeStruct((1, nq), _F32), jax.ShapeDtypeStruct((V7X_SUBLANES, cw), _F32)] + ca.out_shape,
        scratch_shapes=carry + ca.sems, input_output_aliases=ca.aliases,
        compiler_params=pltpu.CompilerParams(dimension_semantics=("arbitrary",), vmem_limit_bytes=_vmem_limit(blocks)),
    )(proj, proj, proj, rope, rope, dmixed, dmixed, attn, lse, y, y, sinks, g_attn, g_conv, conv_w8, *ca.operands)


def _position():
    return lax.axis_index("x"), lax.axis_index("y"), lax.axis_index("c")


def _linear(px, py, pc):
    return 4 * px + 2 * py + pc


def _comm_kernel(name, comm):
    ca = _CommArgs(list(comm), 0, 0)
    n_cin, n_cout = len(ca.operands), len(ca.out_shape)

    def body(*refs):
        cin, cout, sems = refs[:n_cin], refs[n_cin:n_cin + n_cout], refs[n_cin + n_cout:]
        ca.start(cin, cout, sems)
        ca.middle(cin, cout, sems)
        ca.finish(cin, cout, sems)

    return pl.pallas_call(
        body, name=name, out_shape=ca.out_shape, in_specs=[_ANY] * n_cin, out_specs=[_ANY] * n_cout,
        scratch_shapes=ca.sems, input_output_aliases=ca.aliases,
    )(*ca.operands)


def _gather_op(units):
    n = len(units)
    inputs, outputs, aliases = [], [], {}
    for shard, _, _, _ in units:
        inputs.append(shard)
        outputs.append(jax.ShapeDtypeStruct((N_DEV * shard.shape[0], shard.shape[1]), shard.dtype))
    for u, (_, buf, _, _) in enumerate(units):
        if buf is not None:
            aliases[len(inputs)] = u
            inputs.append(buf)

    def plan(ins, outs, sems, north):
        send_sems, recv_sems, local_sems = sems
        x, y, c = _position()
        me, sibling = (x, y, c), (x, y, 1 - c)
        xn, yn, dg = (1 - x, y), (x, 1 - y), (1 - x, 1 - y)
        via, to, k_via, k_other = (yn, xn, 2, 1) if north else (xn, yn, 1, 2)

        def rows(u, px, py, pc):
            shard, _, r0, r1 = units[u]
            return outs[u].at[pl.ds(pl.multiple_of(_linear(px, py, pc) * shard.shape[0] + r0, 16), r1 - r0), :]

        def own(u):
            _, _, r0, r1 = units[u]
            return ins[u].at[pl.ds(r0, r1 - r0), :]

        def copy(u, k, block, to_, src=None):
            return pltpu.make_async_remote_copy(
                src_ref=rows(u, *block) if src is None else src, dst_ref=rows(u, *block),
                send_sem=send_sems.at[u, k], recv_sem=recv_sems.at[u, k], device_id=to_, device_id_type=_MESH)

        us = range(n)
        return dict(
            mine=[pltpu.make_async_copy(own(u), rows(u, *me), local_sems.at[u]) for u in us],
            first=[cp for u in us for cp in (copy(u, 0, me, sibling, src=own(u)), copy(u, 1, me, (*xn, c), src=own(u)),
                                             copy(u, 2, me, (*yn, c), src=own(u)))],
            relay=[copy(u, 3, (*via, c), (*to, c)) for u in us],
            arrived={1: [copy(u, 1, (*xn, c), me) for u in us], 2: [copy(u, 2, (*yn, c), me) for u in us],
                     3: [copy(u, 3, (*dg, c), me) for u in us]},
            passed={1: [copy(u, 4, (*xn, c), sibling) for u in us], 2: [copy(u, 5, (*yn, c), sibling) for u in us],
                    3: [copy(u, 6, (*dg, c), sibling) for u in us]},
            rest=[cp for u in us for cp in (copy(u, 0, sibling, me), copy(u, 4, (*xn, 1 - c), me),
                                            copy(u, 5, (*yn, 1 - c), me), copy(u, 6, (*dg, 1 - c), me))],
            k_via=k_via, k_other=k_other)

    def land(p, k):
        for arrived, onward in zip(p["arrived"][k], p["passed"][k]):
            arrived.wait_recv()
            onward.start()

    def by_core(fn):
        c = lax.axis_index("c")
        for north in (True, False):
            pl.when(c == (1 if north else 0))(functools.partial(fn, north))

    def start(ins, outs, sems):
        p = plan(ins, outs, sems, True)
        for cp in p["mine"] + p["first"]:
            cp.start()

    def middle(ins, outs, sems):
        def go(north):
            p = plan(ins, outs, sems, north)
            land(p, p["k_via"])
            for cp in p["relay"]:
                cp.start()
            land(p, p["k_other"])
        by_core(go)

    def finish(ins, outs, sems):
        def go(north):
            p = plan(ins, outs, sems, north)
            land(p, 3)
            for cp in p["rest"]:
                cp.wait_recv()
            for cp in p["first"] + p["relay"] + [cp for k in (1, 2, 3) for cp in p["passed"][k]]:
                cp.wait_send()
            for cp in p["mine"]:
                cp.wait()
        by_core(go)

    sems = [pltpu.SemaphoreType.DMA((n, 7)), pltpu.SemaphoreType.DMA((n, 7)), pltpu.SemaphoreType.DMA((n,))]
    return _Comm(inputs, outputs, aliases, sems, start, finish, middle)


def _peers(x, y, c):
    out = []
    for k in range(1, N_DEV):
        fx, fy, fc = (k >> 2) & 1, (k >> 1) & 1, k & 1
        out.append((1 - x if fx else x, 1 - y if fy else y, 1 - c if fc else c))
    return out


def _exchange_op(partials):
    n = len(partials)
    outputs = [jax.ShapeDtypeStruct((4, p.shape[0] // N_DEV, p.shape[1]), p.dtype) for p in partials]

    def plan(ins, outs, sems):
        send_sems, recv_sems = sems
        x, y, c = _position()
        out = []
        for a in range(n):
            r = outs[a].shape[1]
            for ch in range(4):
                out.append(pltpu.make_async_remote_copy(
                    src_ref=ins[a].at[pl.ds(pl.multiple_of((2 * ch + 1 - c) * r, 16), r), :], dst_ref=outs[a].at[ch],
                    send_sem=send_sems.at[a, ch], recv_sem=recv_sems.at[a, ch], device_id=(x, y, 1 - c),
                    device_id_type=_MESH))
        return out

    def start(ins, outs, sems):
        for cp in plan(ins, outs, sems):
            cp.start()

    def finish(ins, outs, sems):
        copies = plan(ins, outs, sems)
        for cp in copies:
            cp.wait_recv()
        for cp in copies:
            cp.wait_send()

    sems = [pltpu.SemaphoreType.DMA((n, 4)), pltpu.SemaphoreType.DMA((n, 4))]
    return _Comm(list(partials), outputs, {}, sems, start, finish)


def _chip_send_op(units):
    n = len(units)
    inputs, outputs, aliases = [], [], {}
    for q, _, _, _ in units:
        inputs.append(q)
        outputs.append(jax.ShapeDtypeStruct(q.shape, q.dtype))
    for u, (_, buf, _, _) in enumerate(units):
        if buf is not None:
            aliases[len(inputs)] = u
            inputs.append(buf)

    def plan(ins, outs, sems):
        send_sems, recv_sems, local_sems = sems
        x, y, c = _position()
        my_chip = 2 * x + y
        chips = [(1 - x, y), (x, 1 - y), (1 - x, 1 - y)]
        mine, sends, arrivals = [], [], []
        for u, (_, _, r0, r1) in enumerate(units):
            span = pl.ds(r0, r1 - r0)
            mine.append(pltpu.make_async_copy(ins[u].at[my_chip, span, :], outs[u].at[my_chip, span, :], local_sems.at[u]))
            for k, (px, py) in enumerate(chips):
                sends.append(pltpu.make_async_remote_copy(
                    src_ref=ins[u].at[2 * px + py, span, :], dst_ref=outs[u].at[my_chip, span, :],
                    send_sem=send_sems.at[u, k], recv_sem=recv_sems.at[u, k], device_id=(px, py, c), device_id_type=_MESH))
                arrivals.append(pltpu.make_async_remote_copy(
                    src_ref=ins[u].at[my_chip, span, :], dst_ref=outs[u].at[2 * px + py, span, :],
                    send_sem=send_sems.at[u, k], recv_sem=recv_sems.at[u, k], device_id=(px, py, c), device_id_type=_MESH))
        return mine, sends, arrivals

    def start(ins, outs, sems):
        mine, sends, _ = plan(ins, outs, sems)
        for cp in mine + sends:
            cp.start()

    def finish(ins, outs, sems):
        mine, sends, arrivals = plan(ins, outs, sems)
        for cp in arrivals:
            cp.wait_recv()
        for cp in sends:
            cp.wait_send()
        for cp in mine:
            cp.wait()

    sems = [pltpu.SemaphoreType.DMA((n, 3)), pltpu.SemaphoreType.DMA((n, 3)), pltpu.SemaphoreType.DMA((n,))]
    return _Comm(inputs, outputs, aliases, sems, start, finish)


def _pair_sum(name, partial, received):
    _, rows, cols = received.shape
    tr = _pick(rows, (352, 288, 256, 128, 64, 32, 16))
    p4 = partial.reshape(4, 2, rows, cols)
    kind = jnp.reshape(lax.axis_index("c"), (1,)).astype(jnp.int32)

    def body(kind_ref, p_ref, r_ref, o_ref):
        o_ref[0] = (p_ref[0, 0].astype(_F32) + r_ref[0].astype(_F32)).astype(o_ref.dtype)

    return pl.pallas_call(
        body, name=name,
        grid_spec=pltpu.PrefetchScalarGridSpec(
            num_scalar_prefetch=1, grid=(4, rows // tr),
            in_specs=[pl.BlockSpec((1, 1, tr, cols), lambda ch, i, kind_ref: (ch, kind_ref[0], i, 0)),
                      pl.BlockSpec((1, tr, cols), lambda ch, i, kind_ref: (ch, i, 0))],
            out_specs=pl.BlockSpec((1, tr, cols), lambda ch, i, kind_ref: (ch, i, 0))),
        out_shape=jax.ShapeDtypeStruct(received.shape, received.dtype),
        compiler_params=pltpu.CompilerParams(dimension_semantics=("arbitrary", "arbitrary")),
    )(kind, p4, received)


def _all_reduce_small(name, v):
    rows = v.shape[0]

    def body(v_ref, out_ref, land_ref, send_sems, recv_sems):
        x, y, c = _position()
        me = _linear(x, y, c)
        peers = _peers(x, y, c)
        land_ref[me] = v_ref[...]
        sends = [pltpu.make_async_remote_copy(
            src_ref=v_ref, dst_ref=land_ref.at[me], send_sem=send_sems.at[k], recv_sem=recv_sems.at[k],
            device_id=peer, device_id_type=_MESH) for k, peer in enumerate(peers)]
        for cp in sends:
            cp.start()
        for k, peer in enumerate(peers):
            pltpu.make_async_remote_copy(
                src_ref=v_ref, dst_ref=land_ref.at[_linear(*peer)], send_sem=send_sems.at[k], recv_sem=recv_sems.at[k],
                device_id=peer, device_id_type=_MESH).wait_recv()
        for cp in sends:
            cp.wait_send()
        total = land_ref[0]
        for s in range(1, N_DEV):
            total = total + land_ref[s]
        out_ref[...] = total

    return pl.pallas_call(
        body, name=name, out_shape=jax.ShapeDtypeStruct(v.shape, _F32),
        in_specs=[pl.BlockSpec(memory_space=pltpu.VMEM)], out_specs=pl.BlockSpec(memory_space=pltpu.VMEM),
        scratch_shapes=[pltpu.VMEM((N_DEV, rows, V7X_LANES), _F32), pltpu.SemaphoreType.DMA((7,)), pltpu.SemaphoreType.DMA((7,))],
    )(v)


def _adamw(name, w, slots, m, v):
    rows, cols = w.shape
    n_slots = slots.shape[0]
    tr = _pick(rows, (176, 144, 128, 64, 32, 16, 8))

    def body(w_ref, s_ref, m_ref, v_ref, g_ref, d_ref, nm_ref, nv_ref):
        g = s_ref[0].astype(_F32)
        for k in range(1, n_slots):
            g = g + s_ref[k].astype(_F32)
        nm = ADAM_B1 * m_ref[...] + (1.0 - ADAM_B1) * g
        nv = ADAM_B2 * v_ref[...] + (1.0 - ADAM_B2) * (g * g)
        m_hat = nm / (1.0 - ADAM_B1 ** ADAM_STEP)
        v_hat = nv / (1.0 - ADAM_B2 ** ADAM_STEP)
        g_ref[...] = g
        d_ref[...] = -ADAM_LR * (m_hat / (jnp.sqrt(v_hat) + ADAM_EPS) + ADAM_WD * w_ref[...])
        nm_ref[...] = nm
        nv_ref[...] = nv

    spec = pl.BlockSpec((tr, cols), lambda i: (i, 0))
    blocks = 7 * tr * cols * 4 + _nbytes((n_slots, tr, cols), slots.dtype)
    return pl.pallas_call(
        body, name=name, grid=(rows // tr,),
        in_specs=[spec, pl.BlockSpec((n_slots, tr, cols), lambda i: (0, i, 0)), spec, spec], out_specs=[spec] * 4,
        out_shape=[jax.ShapeDtypeStruct((rows, cols), _F32)] * 4,
        compiler_params=pltpu.CompilerParams(dimension_semantics=("arbitrary",), vmem_limit_bytes=_vmem_limit(blocks)),
    )(w, slots, m, v)


def _pad_rows(a, rows):
    return jnp.pad(a, ((0, rows - a.shape[0]), (0, 0)))


def _pack(parts):
    rows, spans, at = [], [], 0
    for p in parts:
        p = p.reshape(-1)
        r = -(-p.shape[0] // V7X_LANES)
        rows.append(jnp.pad(p, (0, r * V7X_LANES - p.shape[0])).reshape(r, V7X_LANES))
        spans.append((at, r, p.shape[0]))
        at += r
    packed = jnp.concatenate(rows, axis=0)
    return _pad_rows(packed, -(-at // V7X_SUBLANES) * V7X_SUBLANES), spans


def _unpack(packed, spans, shapes):
    return [packed[at:at + r].reshape(-1)[:size].reshape(shape) for (at, r, size), shape in zip(spans, shapes)]


def kernel(x, positions, w_in, conv_w, sinks, g_attn, g_conv, w_out, ln1_g, ln1_b, w_gate, w_up, w_down, ln2_g, ln2_b, loss_target, m_w_in, m_conv_w, m_sinks, m_g_attn, m_g_conv, m_w_out, m_ln1_g, m_ln1_b, m_w_gate, m_w_up, m_w_down, m_ln2_g, m_ln2_b, v_w_in, v_conv_w, v_sinks, v_g_attn, v_g_conv, v_w_out, v_ln1_g, v_ln1_b, v_w_gate, v_w_up, v_w_down, v_ln2_g, v_ln2_b):
    _, s, d = x.shape
    d_ff = N_DEV * w_gate.shape[2]
    dm = _Dims(s, d, d_ff)
    aw, cw, nq, inw = dm.aw, dm.cw, dm.nq, dm.inw
    x2 = x[0]
    pos = positions[0].reshape(s, 1)
    inv_freq = ROPE_THETA ** (-jnp.arange(0, ROT_DIM, 2, dtype=_F32) / ROT_DIM)
    invf = jnp.tile(inv_freq, V7X_LANES // (ROT_DIM // 2)).reshape(1, V7X_LANES)

    conv_cols = conv_w.shape[2]
    sh_in, sh_out = w_in[0].T.astype(_CDT), w_out[0].astype(_CDT)
    sh_gate, sh_up, sh_down = w_gate[0].T.astype(_CDT), w_up[0].T.astype(_CDT), w_down[0].astype(_CDT)
    r_in, r_out, r_ff = sh_in.shape[0], sh_out.shape[0], sh_gate.shape[0]
    q_ff = r_ff // 4
    assert q_ff % 16 == 0
    def prepare_body(x_ref, pos_ref, invf_ref, xc_ref, rope_ref):
        xc_ref[...] = x_ref[...].astype(_CDT)
        cos, sgn = _rope_tables(pos_ref[...], invf_ref[...])
        rope_ref[:, 0:V7X_LANES] = cos
        rope_ref[:, V7X_LANES:2 * V7X_LANES] = sgn

    x_c, rope, w_in_t, conv_all = _row_kernel(
        "prepare_gather_w_in", prepare_body, [x2, pos], [invf], [((s, d), _CDT), ((s, 2 * V7X_LANES), _F32)], [],
        comm=[_gather_op([(sh_in, None, 0, r_in), (_pad_rows(conv_w[0], 16), None, 0, 16)])])
    conv_full = conv_all.reshape(N_DEV, 16, conv_cols)[:, :3, :].transpose(1, 0, 2).reshape(3, cw)
    conv_w8 = _pad_rows(conv_full, V7X_SUBLANES)

    tm = _pick(s, (1024, 512, 256, 128))
    tm2 = _pick(s, (2048, 1024, 512, 256, 128))
    tn_in = _pick(inw, (512, 256, 128))
    tn_ff = _pick(d_ff, (512, 256, 128))

    proj, w_out_f, w_gate_t = _matmul(
        "proj", [[(x_c, w_in_t, "nt")]], s, inw, d, tm2, tn_in, d, [],
        [((s, inw), _F32, (tm2, tn_in), _tile_ij)], _store_epilogue,
        comm=[_gather_op([(sh_out, None, 0, r_out), (sh_gate, None, 0, 2 * q_ff)])])
    mixed, attn, lse, y_conv, w_gate_t, w_up_t = _mixer_fwd(
        dm, proj, rope, sinks, g_attn, g_conv, conv_w8,
        comm=[_gather_op([(sh_gate, w_gate_t, 2 * q_ff, r_ff), (sh_up, None, 0, 2 * q_ff)])])

    def residual_epilogue(accs, ex, out, first):
        out[0][...] = DEEPNORM_ALPHA * ex[0][...] + accs[0]

    tn_d = _pick(d, (512,))
    r1, w_up_t = _matmul(
        "out_proj", [[(mixed, w_out_f, "nn")]], s, d, d, tm, tn_d, d, [(x2, (tm, tn_d), _tile_ij)],
        [((s, d), _F32, (tm, tn_d), _tile_ij)], residual_epilogue,
        comm=[_gather_op([(sh_up, w_up_t, 2 * q_ff, 3 * q_ff)])])
    h1, h1_c, w_up_t = _ln1_fwd_rows(r1, ln1_g, ln1_b, comm=[_gather_op([(sh_up, w_up_t, 3 * q_ff, r_ff)])])

    def swiglu_epilogue(accs, ex, out, first):
        gate_v, up_v = accs
        out[0][...] = gate_v
        out[1][...] = up_v
        out[2][...] = (gate_v * jax.nn.sigmoid(gate_v) * up_v).astype(_CDT)

    gate, up, act, w_down_f = _matmul(
        "gate_up", [[(h1_c, w_gate_t, "nt")], [(h1_c, w_up_t, "nt")]], s, d_ff, d, tm, tn_ff, d, [],
        [((s, d_ff), _F32, (tm, tn_ff), _tile_ij), ((s, d_ff), _F32, (tm, tn_ff), _tile_ij),
         ((s, d_ff), _CDT, (tm, tn_ff), _tile_ij)], swiglu_epilogue,
        comm=[_gather_op([(sh_down, None, 0, r_ff)])])

    (r2,) = _matmul("down", [[(act, w_down_f, "nn")]], s, d, d_ff, tm, tn_d, d_ff, [(h1, (tm, tn_d), _tile_ij)],
                    [((s, d), _F32, (tm, tn_d), _tile_ij)], residual_epilogue)
    dr2, dr2_c, loss_acc, d_ln2_g, d_ln2_b = _ln2_loss_bwd(r2, loss_target[0], ln2_g, ln2_b)

    def swiglu_bwd_epilogue(accs, ex, out, first):
        gate_v, up_v = ex[0][...], ex[1][...]
        sig = jax.nn.sigmoid(gate_v)
        out[0][...] = (accs[0] * up_v * (sig * (1.0 + gate_v * (1.0 - sig)))).astype(_CDT)
        out[1][...] = (accs[0] * (gate_v * sig)).astype(_CDT)

    dgate, dup = _matmul(
        "dact", [[(dr2_c, w_down_f, "nt")]], s, d_ff, d, tm2, tn_ff, d,
        [(gate, (tm2, tn_ff), _tile_ij), (up, (tm2, tn_ff), _tile_ij)],
        [((s, d_ff), _CDT, (tm2, tn_ff), _tile_ij), ((s, d_ff), _CDT, (tm2, tn_ff), _tile_ij)], swiglu_bwd_epilogue)
    def weight_grad(name, a, b, comm=()):
        rows = a.shape[1]
        tw, tn_w = _pick(rows, (512, 256, 128)), _pick(d, (1024, 512))
        return _matmul(name, [[(a, b, "tn")]], rows, d, s, tw, tn_w, s, [],
                       [((rows, d), _CDT, (tw, tn_w), _tile_ij)], _store_epilogue, comm=comm, j_outer=True)

    (dw_down,) = weight_grad("dw_down", act, dr2_c)
    dw_gate_t, x_down = weight_grad("dw_gate", dgate, h1_c, comm=[_exchange_op([dw_down])])
    q_down = _pair_sum("chip_sum_w_down", dw_down, x_down)
    dw_up_t, l_down, x_gate = weight_grad(
        "dw_up", dup, h1_c, comm=[_chip_send_op([(q_down, None, 0, 2 * q_ff)]), _exchange_op([dw_gate_t])])
    q_gate = _pair_sum("chip_sum_w_gate", dw_gate_t, x_gate)

    def add_epilogue(accs, ex, out, first):
        out[0][...] = ex[0][...] + accs[0]

    dh1_gate, l_down, l_gate, x_up = _matmul(
        "dh1_gate", [[(dgate, w_gate_t, "nn")]], s, d, d_ff, tm, tn_d, d_ff,
        [(dr2, (tm, tn_d), _tile_ij)], [((s, d), _F32, (tm, tn_d), _tile_ij)], residual_epilogue,
        comm=[_chip_send_op([(q_down, l_down, 2 * q_ff, r_ff), (q_gate, None, 0, q_ff)]), _exchange_op([dw_up_t])])
    q_up = _pair_sum("chip_sum_w_up", dw_up_t, x_up)
    dh1, l_gate = _matmul(
        "dh1_up", [[(dup, w_up_t, "nn")]], s, d, d_ff, tm, tn_d, d_ff,
        [(dh1_gate, (tm, tn_d), _tile_ij)], [((s, d), _F32, (tm, tn_d), _tile_ij)], add_epilogue,
        comm=[_chip_send_op([(q_gate, l_gate, q_ff, r_ff)])])
    dr1, dr1_c, d_ln1_g, d_ln1_b = _ln1_bwd_rows(dh1, r1, ln1_g)
    (dmixed,) = _matmul("dmixed", [[(dr1_c, w_out_f, "nt")]], s, d, d, tm2, tn_d, d, [],
                        [((s, d), _F32, (tm2, tn_d), _tile_ij)], _store_epilogue)
    (dw_out,) = weight_grad("dw_out", mixed, dr1_c)
    dproj, dkv, d_g_attn, d_g_conv, d_sinks, d_conv8, l_up, x_out = _mixer_bwd(
        dm, proj, rope, sinks, g_attn, g_conv, conv_w8, dmixed, attn, lse, y_conv,
        comm=[_chip_send_op([(q_up, None, 0, r_ff)]), _exchange_op([dw_out])])
    dproj = _patch_columns("dproj_kv", dproj, dkv, dm.o_k)
    q_out = _pair_sum("chip_sum_w_out", dw_out, x_out)
    dw_in_t, l_out = weight_grad("dw_in", dproj, x_c, comm=[_chip_send_op([(q_out, None, 0, r_out)])])
    (x_in,) = _comm_kernel("exchange_w_in", [_exchange_op([dw_in_t])])
    q_in = _pair_sum("chip_sum_w_in", dw_in_t, x_in)

    grad_x, l_in = _matmul("dx", [[(dproj, w_in_t, "nn")]], s, d, inw, tm, tn_d, inw,
                           [(dr1, (tm, tn_d), _tile_ij)], [((s, d), _F32, (tm, tn_d), _tile_ij)], residual_epilogue,
                           comm=[_chip_send_op([(q_in, None, 0, r_in)])])

    small_parts = [d_conv8[:3], d_sinks, d_g_attn, d_g_conv, d_ln1_g, d_ln1_b, d_ln2_g, d_ln2_b, loss_acc[0:1, 0:1]]
    packed, spans = _pack(small_parts)
    reduced = _unpack(_all_reduce_small("reduce_small", packed), spans, [p.shape for p in small_parts])
    g_conv_full, g_sinks, g_g_attn, g_g_conv, g_ln1_g, g_ln1_b, g_ln2_g, g_ln2_b, loss_sum = reduced
    me = _linear(*_position())
    g_conv_w = lax.dynamic_slice(g_conv_full, (0, me * conv_cols), (3, conv_cols))
    loss = loss_sum[0, 0]

    big = {"w_in": (w_in[0].T, l_in, m_w_in[0].T, v_w_in[0].T), "w_out": (w_out[0], l_out, m_w_out[0], v_w_out[0]),
           "w_gate": (w_gate[0].T, l_gate, m_w_gate[0].T, v_w_gate[0].T),
           "w_up": (w_up[0].T, l_up, m_w_up[0].T, v_w_up[0].T), "w_down": (w_down[0], l_down, m_w_down[0], v_w_down[0])}
    res = {nm: tuple(_adamw(f"adamw_{nm}", w, slots, m, v)) for nm, (w, slots, m, v) in big.items()}
    for nm in ("w_in", "w_gate", "w_up"):
        res[nm] = tuple(a.T for a in res[nm])
    small_names = ["conv_w", "sinks", "g_attn", "g_conv", "ln1_g", "ln1_b", "ln2_g", "ln2_b"]
    small_w = [conv_w, sinks, g_attn, g_conv, ln1_g, ln1_b, ln2_g, ln2_b]
    small_g = [g_conv_w[None], g_sinks, g_g_attn, g_g_conv, g_ln1_g, g_ln1_b, g_ln2_g, g_ln2_b]
    small_m = [m_conv_w, m_sinks, m_g_attn, m_g_conv, m_ln1_g, m_ln1_b, m_ln2_g, m_ln2_b]
    small_v = [v_conv_w, v_sinks, v_g_attn, v_g_conv, v_ln1_g, v_ln1_b, v_ln2_g, v_ln2_b]
    pw, sp = _pack(small_w)
    pg, _ = _pack(small_g)
    pm, _ = _pack(small_m)
    pv, _ = _pack(small_v)
    shapes = [w.shape for w in small_w]
    _, sd, sm, sv = [_unpack(p, sp, shapes) for p in _adamw("adamw_small", pw, pg[None], pm, pv)]
    for i, nm in enumerate(small_names):
        res[nm] = (small_g[i].reshape(shapes[i]), sd[i], sm[i], sv[i])

    order = ["w_in", "conv_w", "sinks", "g_attn", "g_conv", "w_out", "ln1_g", "ln1_b", "w_gate", "w_up", "w_down", "ln2_g", "ln2_b"]

    def lead(a, nm):
        return a[None] if nm in big else a

    return (loss, grad_x[None],
            *[lead(res[nm][0], nm) for nm in order], *[lead(res[nm][1], nm) for nm in order],
            *[lead(res[nm][2], nm) for nm in order], *[lead(res[nm][3], nm) for nm in order])
```

```python
import functools

import jax
import jax.numpy as jnp
from jax import lax
from jax.experimental import pallas as pl
from jax.experimental.pallas import tpu as pltpu

_F32 = jnp.float32
_CDT = jnp.bfloat16

HEAD_DIM = 64
WINDOW = 128
N_KV_HEADS = 4
KV_WIDTH = N_KV_HEADS * HEAD_DIM
ROT_DIM = HEAD_DIM // 4
ROPE_THETA = 500000.0
ATTN_SCALE = HEAD_DIM ** -0.5
DEPTH = 1
DEEPNORM_ALPHA = (2 * DEPTH) ** 0.25
LN_EPS = 1e-5
RMS_EPS = 1e-6
ADAM_LR = 0.001
ADAM_B1 = 0.9
ADAM_B2 = 0.999
ADAM_EPS = 1e-08
ADAM_WD = 0.01
ADAM_STEP = 10
N_DEV = 8
MASKED = -1e30

MIB = 1024 * 1024
V7X_VMEM_BYTES = 64 * MIB
V7X_LANES = 128
V7X_SUBLANES = 8
BODY_TEMPORARIES_BYTES = 16 * MIB
VMEM_LIMIT_FLOOR_BYTES = 56 * MIB
VMEM_LIMIT_CEILING_BYTES = V7X_VMEM_BYTES - 8 * MIB
_MESH = pl.DeviceIdType.MESH
_ANY = pl.BlockSpec(memory_space=pl.ANY)


def _vmem_limit(block_bytes, scratch_bytes=0):
    want = 2 * block_bytes + scratch_bytes + BODY_TEMPORARIES_BYTES
    return int(min(max(want, VMEM_LIMIT_FLOOR_BYTES), VMEM_LIMIT_CEILING_BYTES))


def _nbytes(shape, dtype):
    n = 1
    for s in shape:
        n *= s
    return n * jnp.dtype(dtype).itemsize


def _pick(n, candidates):
    for c in candidates:
        if n % c == 0:
            return c
    raise ValueError(f"no tile of {candidates} divides {n}")


_DOT_DIMS = {"nn": ((1,), (0,)), "nt": ((1,), (1,)), "tn": ((0,), (0,))}


def _dot(a, b, mode):
    return lax.dot_general(a.astype(_CDT), b.astype(_CDT), (_DOT_DIMS[mode], ((), ())),
                           preferred_element_type=_F32)


def _accumulate(ref, val, first):
    @pl.when(first)
    def _():
        ref[...] = val

    @pl.when(jnp.logical_not(first))
    def _():
        ref[...] += val


class _Comm:
    def __init__(self, inputs, outputs, aliases, sems, start, finish, middle=None):
        self.inputs, self.outputs, self.aliases, self.sems = inputs, outputs, aliases, sems
        self.start, self.finish, self.middle = start, finish, middle


def _middle_step(n_steps):
    return (2 * n_steps) // 3


class _CommArgs:
    def __init__(self, comms, n_in_before, n_out_before):
        self.comms, self.operands, self.out_shape, self.aliases, self.sems, self.at = comms, [], [], {}, [], []
        for cm in comms:
            self.at.append((len(self.operands), len(self.out_shape), len(self.sems)))
            for i_in, i_out in cm.aliases.items():
                self.aliases[n_in_before + len(self.operands) + i_in] = n_out_before + len(self.out_shape) + i_out
            self.operands += cm.inputs
            self.out_shape += cm.outputs
            self.sems += cm.sems

    def _each(self, in_refs, out_refs, sem_refs):
        for cm, (i0, o0, s0) in zip(self.comms, self.at):
            yield cm, (in_refs[i0:i0 + len(cm.inputs)], out_refs[o0:o0 + len(cm.outputs)], sem_refs[s0:s0 + len(cm.sems)])

    def start(self, in_refs, out_refs, sem_refs):
        for cm, refs in self._each(in_refs, out_refs, sem_refs):
            cm.start(*refs)

    def finish(self, in_refs, out_refs, sem_refs):
        for cm, refs in self._each(in_refs, out_refs, sem_refs):
            cm.finish(*refs)

    @property
    def has_middle(self):
        return any(cm.middle is not None for cm in self.comms)

    def middle(self, in_refs, out_refs, sem_refs):
        for cm, refs in self._each(in_refs, out_refs, sem_refs):
            if cm.middle is not None:
                cm.middle(*refs)


def _matmul(name, groups, m, n, k, tm, tn, tk, extras, outs, epilogue, comm=(), j_outer=False):
    assert m % tm == 0 and n % tn == 0 and k % tk == 0, (name, m, n, k, tm, tn, tk)
    nk = k // tk
    terms = [t for g in groups for t in g]
    operands, in_specs, block_bytes = [], [], 0

    def spec(blk, imap):
        return pl.BlockSpec(blk, (lambda g0, g1, kk: imap(g1, g0, kk)) if j_outer else imap)

    for a, b, mode in terms:
        assert a.shape == ((k, m) if mode == "tn" else (m, k)), (name, a.shape, mode)
        assert b.shape == ((n, k) if mode == "nt" else (k, n)), (name, b.shape, mode)
        if mode == "tn":
            a_blk, a_map = (tk, tm), (lambda i, j, kk: (kk, i))
        else:
            a_blk, a_map = (tm, tk), (lambda i, j, kk: (i, kk))
        if mode == "nt":
            b_blk, b_map = (tn, tk), (lambda i, j, kk: (j, kk))
        else:
            b_blk, b_map = (tk, tn), (lambda i, j, kk: (kk, j))
        operands += [a, b]
        in_specs += [spec(a_blk, a_map), spec(b_blk, b_map)]
        block_bytes += _nbytes(a_blk, a.dtype) + _nbytes(b_blk, b.dtype)
    for arr, blk, imap in extras:
        operands.append(arr)
        in_specs.append(spec(blk, lambda i, j, kk, imap=imap: imap(i, j)))
        block_bytes += _nbytes(blk, arr.dtype)
    out_shape, out_specs = [], []
    for shape, dtype, blk, imap in outs:
        out_shape.append(jax.ShapeDtypeStruct(shape, dtype))
        out_specs.append(spec(blk, lambda i, j, kk, imap=imap: imap(i, j)))
        block_bytes += _nbytes(blk, dtype)
    n_terms, n_extra, n_out, n_groups = len(terms), len(extras), len(outs), len(groups)
    scratch = [pltpu.VMEM((tm, tn), _F32) for _ in range(n_groups)] if nk > 1 else []
    ca = _CommArgs(list(comm), len(operands), n_out)
    n_cin, n_cout, n_acc = len(ca.operands), len(ca.out_shape), len(scratch)
    tiles = (m // tm, n // tn)
    grid = (tiles[1], tiles[0], nk) if j_outer else (tiles[0], tiles[1], nk)

    def body(*refs):
        refs = list(refs)
        term_refs = [refs.pop(0) for _ in range(2 * n_terms)]
        extra_refs = [refs.pop(0) for _ in range(n_extra)]
        cin_refs = [refs.pop(0) for _ in range(n_cin)]
        out_refs = [refs.pop(0) for _ in range(n_out)]
        cout_refs = [refs.pop(0) for _ in range(n_cout)]
        acc_refs = [refs.pop(0) for _ in range(n_acc)]
        sem_refs = refs
        g0, g1, kk = pl.program_id(0), pl.program_id(1), pl.program_id(2)
        first = jnp.logical_and(g0 == 0, g1 == 0)
        if comm:
            @pl.when(jnp.logical_and(first, kk == 0))
            def _():
                ca.start(cin_refs, cout_refs, sem_refs)
        if ca.has_middle:
            step = (g0 * grid[1] + g1) * nk + kk

            @pl.when(step == _middle_step(grid[0] * grid[1] * nk))
            def _():
                ca.middle(cin_refs, cout_refs, sem_refs)
        partial, t = [], 0
        for g in groups:
            s = None
            for _, _, mode in g:
                d = _dot(term_refs[2 * t][...], term_refs[2 * t + 1][...], mode)
                s = d if s is None else s + d
                t += 1
            partial.append(s)
        if nk == 1:
            epilogue(partial, extra_refs, out_refs, first)
        else:
            for acc, p in zip(acc_refs, partial):
                _accumulate(acc, p, kk == 0)

            @pl.when(kk == nk - 1)
            def _():
                epilogue([acc[...] for acc in acc_refs], extra_refs, out_refs, first)
        if comm:
            @pl.when(jnp.logical_and(jnp.logical_and(g0 == grid[0] - 1, g1 == grid[1] - 1), kk == nk - 1))
            def _():
                ca.finish(cin_refs, cout_refs, sem_refs)

    res = pl.pallas_call(
        body, name=name, grid=grid,
        in_specs=in_specs + [_ANY] * n_cin, out_specs=out_specs + [_ANY] * n_cout,
        out_shape=out_shape + ca.out_shape, scratch_shapes=scratch + ca.sems, input_output_aliases=ca.aliases,
        compiler_params=pltpu.CompilerParams(
            dimension_semantics=("arbitrary", "arbitrary", "arbitrary"),
            vmem_limit_bytes=_vmem_limit(block_bytes, n_groups * tm * tn * 4 if nk > 1 else 0)),
    )(*operands, *ca.operands)
    return list(res[:n_out]) + list(res[n_out:])


def _store_epilogue(accs, extra_refs, out_refs, first):
    for acc, ref in zip(accs, out_refs):
        ref[...] = acc.astype(ref.dtype)


def _tile_ij(i, j):
    return (i, j)


def _row_i(i, j):
    return (i, 0)


def _whole(i, j):
    return (0, 0)


def _mean(v):
    return jnp.mean(v, axis=-1, keepdims=True)


def _ln_fwd(r, g, b):
    xc = r - _mean(r)
    rstd = lax.rsqrt(_mean(xc * xc) + LN_EPS)
    xhat = xc * rstd
    return xhat * g + b, xhat, rstd


def _ln_bwd(dy, xhat, rstd, g):
    dxh = dy * g
    dr = rstd * (dxh - _mean(dxh) - xhat * _mean(dxh * xhat))
    return dr, jnp.sum(dy * xhat, axis=0, keepdims=True), jnp.sum(dy, axis=0, keepdims=True)


def _rms_fwd(a, g):
    rstd = lax.rsqrt(_mean(a * a) + RMS_EPS)
    return a * rstd * g


def _rms_bwd(dm, a, g):
    rstd = lax.rsqrt(_mean(a * a) + RMS_EPS)
    nhat = a * rstd
    dn = dm * g
    da = rstd * (dn - nhat * _mean(dn * nhat))
    return da, jnp.sum(dm * nhat, axis=0, keepdims=True)


def _lane(shape):
    return lax.broadcasted_iota(jnp.int32, shape, 1)


def _row(shape):
    return lax.broadcasted_iota(jnp.int32, shape, 0)


def _rope_tables(pos, invf):
    ang = pos.astype(_F32) * invf
    lane = _lane(ang.shape)
    in_rot = (lane % HEAD_DIM) < ROT_DIM
    first = (lane % ROT_DIM) < ROT_DIM // 2
    cos = jnp.where(in_rot, jnp.cos(ang), 1.0)
    sin = jnp.sin(ang)
    sgn = jnp.where(in_rot, jnp.where(first, -sin, sin), 0.0)
    return cos, sgn


def _rope(t, cos, sgn, sign):
    half = ROT_DIM // 2
    first = (_lane(t.shape) % ROT_DIM) < half
    partner = jnp.where(first, pltpu.roll(t, V7X_LANES - half, 1), pltpu.roll(t, half, 1))
    return t * cos + partner * (sgn * sign)


def _dup_head(t, h):
    g = t[:, 128 * (h // 2):128 * (h // 2) + 128]
    r = pltpu.roll(g, HEAD_DIM, 1)
    lo = _lane(g.shape) < HEAD_DIM
    return jnp.where(lo, g, r) if h % 2 == 0 else jnp.where(lo, r, g)


def _fold_halves(t):
    return t + pltpu.roll(t, HEAD_DIM, 1)


def _halves(t):
    lo = _lane(t.shape) < HEAD_DIM
    zero = jnp.zeros_like(t)
    return jnp.where(lo, t, zero), jnp.where(lo, zero, t)


def _band_mask(n_heads, n_keys, first_block):
    shape = (n_heads * WINDOW, n_keys)
    i = jnp.bitwise_and(_row(shape), WINDOW - 1)
    j = _lane(shape)
    valid = jnp.logical_and(j >= i + 1, j <= i + WINDOW)
    if first_block is not None:
        valid = jnp.logical_and(valid, jnp.logical_or(j >= WINDOW, jnp.logical_not(first_block)))
    return valid


def _stack_heads(pairs):
    return jnp.concatenate([half for t in pairs for half in _halves(t)], axis=0).astype(_CDT)


def _unstack_heads(t, n_pairs):
    lo = _lane((WINDOW, 128)) < HEAD_DIM
    return [jnp.where(lo, t[2 * WINDOW * i:2 * WINDOW * i + WINDOW], t[2 * WINDOW * i + WINDOW:2 * WINDOW * (i + 1)])
            for i in range(n_pairs)]


def _per_head(values):
    n_rows = len(values) * WINDOW
    block = jnp.right_shift(_row((n_rows, 1)), WINDOW.bit_length() - 1)
    out = jnp.zeros((n_rows, 1), _F32)
    for k, v in enumerate(values):
        out = jnp.where(block == k, v, out)
    return out


def _shift_down(z, halo, k):
    out = pltpu.roll(z, k, 0)
    r = _row(z.shape)
    for t in range(k):
        out = jnp.where(r == t, halo[V7X_SUBLANES - k + t:V7X_SUBLANES - k + t + 1, :], out)
    return out


def _shift_up(z, halo, k):
    rows = z.shape[0]
    out = pltpu.roll(z, rows - k, 0)
    r = _row(z.shape)
    for t in range(k):
        out = jnp.where(r == rows - k + t, halo[t:t + 1, :], out)
    return out


class _Dims:
    def __init__(self, s, d, d_ff):
        self.s, self.d, self.d_ff = s, d, d_ff
        self.aw = d // 2
        self.cw = d - self.aw
        self.nq = self.aw // HEAD_DIM
        self.group = self.nq // N_KV_HEADS
        assert self.group % 2 == 0, "a 128-lane pair of query heads must share its kv head"
        self.inw = self.aw + 2 * KV_WIDTH + 3 * self.cw
        self.o_k = self.aw
        self.o_v = self.aw + KV_WIDTH
        self.o_cg = self.aw + 2 * KV_WIDTH
        self.o_bg = self.o_cg + self.cw
        self.o_u = self.o_bg + self.cw
        self.nb = s // WINDOW
        assert s % WINDOW == 0


def _carrying(body, n_in, n_out, n_steps, ca, n_scratch=0):
    n_cin, n_cout = len(ca.operands), len(ca.out_shape)

    def wrapped(*refs):
        refs = list(refs)
        in_refs = [refs.pop(0) for _ in range(n_in)]
        cin_refs = [refs.pop(0) for _ in range(n_cin)]
        out_refs = [refs.pop(0) for _ in range(n_out)]
        cout_refs = [refs.pop(0) for _ in range(n_cout)]
        scratch_refs = [refs.pop(0) for _ in range(n_scratch)]
        if ca.comms:
            @pl.when(pl.program_id(0) == 0)
            def _():
                ca.start(cin_refs, cout_refs, refs)
        if ca.has_middle:
            @pl.when(pl.program_id(0) == _middle_step(n_steps))
            def _():
                ca.middle(cin_refs, cout_refs, refs)
        body(*in_refs, *out_refs, *scratch_refs)
        if ca.comms:
            @pl.when(pl.program_id(0) == n_steps - 1)
            def _():
                ca.finish(cin_refs, cout_refs, refs)

    return wrapped


def _row_kernel(name, body, rows_in, vecs_in, rows_out, vecs_out, comm=()):
    s = rows_in[0].shape[0]
    tr = _pick(s, (256, 128))
    row = lambda a: pl.BlockSpec((tr, a[1] if isinstance(a, tuple) else a.shape[1]), lambda i: (i, 0))
    vec = lambda shape: pl.BlockSpec(tuple(shape), lambda i: (0, 0))
    n_in, n_out = len(rows_in) + len(vecs_in), len(rows_out) + len(vecs_out)
    ca = _CommArgs(list(comm), n_in, n_out)
    blocks = sum(_nbytes((tr, a.shape[1]), a.dtype) for a in rows_in) + sum(_nbytes((tr, sh[1]), dt) for sh, dt in rows_out)
    res = pl.pallas_call(
        _carrying(body, n_in, n_out, s // tr, ca), name=name, grid=(s // tr,),
        in_specs=[row(a) for a in rows_in] + [vec(v.shape) for v in vecs_in] + [_ANY] * len(ca.operands),
        out_specs=[row(sh) for sh, _ in rows_out] + [vec(sh) for sh, _ in vecs_out] + [_ANY] * len(ca.out_shape),
        out_shape=[jax.ShapeDtypeStruct(sh, dt) for sh, dt in list(rows_out) + list(vecs_out)] + ca.out_shape,
        scratch_shapes=ca.sems, input_output_aliases=ca.aliases,
        compiler_params=pltpu.CompilerParams(dimension_semantics=("arbitrary",), vmem_limit_bytes=_vmem_limit(blocks)),
    )(*rows_in, *vecs_in, *ca.operands)
    return list(res)


def _ln2_loss_bwd(r2, target, gain, bias, comm=()):
    s, d = r2.shape

    def body(r_ref, t_ref, g_ref, b_ref, dr_ref, drc_ref, loss_ref, dg_ref, db_ref):
        first = pl.program_id(0) == 0
        yv, xhat, rstd = _ln_fwd(r_ref[...], g_ref[...], b_ref[...])
        err = yv - t_ref[...]
        dr2, dg, db = _ln_bwd(err * (1.0 / d), xhat, rstd, g_ref[...])
        dr_ref[...] = dr2
        drc_ref[...] = dr2.astype(_CDT)
        _accumulate(loss_ref, jnp.zeros(loss_ref.shape, _F32) + 0.5 * jnp.sum(err * err) * (1.0 / d), first)
        _accumulate(dg_ref, dg, first)
        _accumulate(db_ref, db, first)

    return _row_kernel("ln2_loss_bwd", body, [r2, target], [gain, bias], [((s, d), _F32), ((s, d), _CDT)],
                       [((V7X_SUBLANES, V7X_LANES), _F32), ((1, d), _F32), ((1, d), _F32)], comm)


def _ln1_fwd_rows(r1, gain, bias, comm=()):
    s, d = r1.shape

    def body(r_ref, g_ref, b_ref, h_ref, hc_ref):
        h1, _, _ = _ln_fwd(r_ref[...], g_ref[...], b_ref[...])
        h_ref[...] = h1
        hc_ref[...] = h1.astype(_CDT)

    return _row_kernel("ln1", body, [r1], [gain, bias], [((s, d), _F32), ((s, d), _CDT)], [], comm)


def _ln1_bwd_rows(dh1, r1, gain, comm=()):
    s, d = dh1.shape

    def body(dh_ref, r_ref, g_ref, dr_ref, drc_ref, dg_ref, db_ref):
        first = pl.program_id(0) == 0
        _, xhat, rstd = _ln_fwd(r_ref[...], g_ref[...], 0.0)
        dr1, dg, db = _ln_bwd(dh_ref[...], xhat, rstd, g_ref[...])
        dr_ref[...] = dr1
        drc_ref[...] = dr1.astype(_CDT)
        _accumulate(dg_ref, dg, first)
        _accumulate(db_ref, db, first)

    return _row_kernel("ln1_bwd", body, [dh1, r1], [gain], [((s, d), _F32), ((s, d), _CDT)],
                       [((1, d), _F32), ((1, d), _F32)], comm)


def _mixer_fwd(dm, proj, rope, sinks, g_attn, g_conv, conv_w8, comm=()):
    s, d, aw, cw, nq, inw, nb = dm.s, dm.d, dm.aw, dm.cw, dm.nq, dm.inw, dm.nb

    def body(pp_ref, pc_ref, ropep_ref, ropec_ref, sinks_ref, ga_ref, gc_ref, cw_ref,
             mixed_ref, attn_ref, lse_ref, y_ref):
        n = pl.program_id(0)
        cos_c, sgn_c = ropec_ref[:, 0:V7X_LANES], ropec_ref[:, V7X_LANES:2 * V7X_LANES]
        cos_p, sgn_p = ropep_ref[:, 0:V7X_LANES], ropep_ref[:, V7X_LANES:2 * V7X_LANES]
        kk = jnp.concatenate(
            [jnp.concatenate([_rope(ref[:, dm.o_k + 128 * g:dm.o_k + 128 * g + 128], c, sg, 1.0)
                              for g in range(KV_WIDTH // 128)], axis=1)
             for ref, c, sg in ((pp_ref, cos_p, sgn_p), (pc_ref, cos_c, sgn_c))], axis=0)
        vv = jnp.concatenate([pp_ref[:, dm.o_v:dm.o_v + KV_WIDTH], pc_ref[:, dm.o_v:dm.o_v + KV_WIDTH]], axis=0)
        group, pairs = dm.group, dm.group // 2
        valid = _band_mask(group, 2 * WINDOW, n == 0)
        for h in range(N_KV_HEADS):
            k2, v2 = _dup_head(kk, h).astype(_CDT), _dup_head(vv, h).astype(_CDT)
            q4 = _stack_heads([_rope(pc_ref[:, 128 * j:128 * j + 128], cos_c, sgn_c, 1.0)
                               for j in range(pairs * h, pairs * (h + 1))])
            sc = jnp.where(valid, _dot(q4, k2, "nt") * ATTN_SCALE, MASKED)
            sink = _per_head([sinks_ref[0, group * h + r] for r in range(group)])
            mx = jnp.maximum(jnp.max(sc, axis=1, keepdims=True), sink)
            p = jnp.exp(sc - mx)
            den = jnp.sum(p, axis=1, keepdims=True) + jnp.exp(sink - mx)
            out = _unstack_heads(_dot(p / den, v2, "nn"), pairs)
            lse = mx + jnp.log(den)
            for r in range(group):
                lse_ref[:, group * h + r:group * h + r + 1] = lse[WINDOW * r:WINDOW * (r + 1)]
            for i in range(pairs):
                j = pairs * h + i
                attn_ref[:, 128 * j:128 * j + 128] = out[i]
        mixed_ref[:, 0:aw] = _rms_fwd(attn_ref[...], ga_ref[...]).astype(mixed_ref.dtype)

        z = pc_ref[:, dm.o_cg:dm.o_cg + cw] * pc_ref[:, dm.o_u:dm.o_u + cw]
        top = WINDOW - V7X_SUBLANES
        halo = pp_ref[top:WINDOW, dm.o_cg:dm.o_cg + cw] * pp_ref[top:WINDOW, dm.o_u:dm.o_u + cw]
        halo = jnp.where(n == 0, jnp.zeros_like(halo), halo)
        y = cw_ref[0:1, :] * _shift_down(z, halo, 2) + cw_ref[1:2, :] * _shift_down(z, halo, 1) + cw_ref[2:3, :] * z
        y_ref[...] = y
        conv = pc_ref[:, dm.o_bg:dm.o_bg + cw] * y
        mixed_ref[:, aw:d] = _rms_fwd(conv, gc_ref[...]).astype(mixed_ref.dtype)

    prev = lambda n: (jnp.maximum(n - 1, 0), 0)
    cur = lambda n: (n, 0)
    fixed = lambda n: (0, 0)
    blocks = 2 * WINDOW * inw * 4 + WINDOW * (d * 2 + aw * 4 + cw * 4 + nq * 4)
    ca = _CommArgs(list(comm), 8, 4)
    return pl.pallas_call(
        _carrying(body, 8, 4, nb, ca), name="mixer_fwd", grid=(nb,),
        in_specs=[pl.BlockSpec((WINDOW, inw), prev), pl.BlockSpec((WINDOW, inw), cur),
                  pl.BlockSpec((WINDOW, 2 * V7X_LANES), prev), pl.BlockSpec((WINDOW, 2 * V7X_LANES), cur),
                  pl.BlockSpec(memory_space=pltpu.SMEM),
                  pl.BlockSpec((1, aw), fixed), pl.BlockSpec((1, cw), fixed), pl.BlockSpec((V7X_SUBLANES, cw), fixed)]
        + [_ANY] * len(ca.operands),
        out_specs=[pl.BlockSpec((WINDOW, d), cur), pl.BlockSpec((WINDOW, aw), cur),
                   pl.BlockSpec((WINDOW, nq), cur), pl.BlockSpec((WINDOW, cw), cur)] + [_ANY] * len(ca.out_shape),
        out_shape=[jax.ShapeDtypeStruct((s, d), _CDT), jax.ShapeDtypeStruct((s, aw), _F32),
                   jax.ShapeDtypeStruct((s, nq), _F32), jax.ShapeDtypeStruct((s, cw), _F32)] + ca.out_shape,
        scratch_shapes=ca.sems, input_output_aliases=ca.aliases,
        compiler_params=pltpu.CompilerParams(dimension_semantics=("arbitrary",), vmem_limit_bytes=_vmem_limit(blocks)),
    )(proj, proj, rope, rope, sinks, g_attn, g_conv, conv_w8, *ca.operands)


def _patch_columns(name, a, part, offset):
    s, pw = part.shape
    assert offset % pw == 0 and pw % V7X_LANES == 0
    tr = _pick(s, (512, 256, 128))

    def body(a_ref, p_ref, o_ref):
        del a_ref
        o_ref[...] = p_ref[...]

    return pl.pallas_call(
        body, name=name, grid=(s // tr,),
        in_specs=[_ANY, pl.BlockSpec((tr, pw), lambda i: (i, 0))],
        out_specs=pl.BlockSpec((tr, pw), lambda i: (i, offset // pw)),
        out_shape=jax.ShapeDtypeStruct(a.shape, a.dtype), input_output_aliases={0: 0},
        compiler_params=pltpu.CompilerParams(dimension_semantics=("arbitrary",)),
    )(a, part)


def _mixer_bwd(dm, proj, rope, sinks, g_attn, g_conv, conv_w8, dmixed, attn, lse, y, comm=()):
    s, d, aw, cw, nq, inw, nb = dm.s, dm.d, dm.aw, dm.cw, dm.nq, dm.inw, dm.nb

    def body(pp_ref, pc_ref, pn_ref, ropep_ref, ropec_ref, dmc_ref, dmn_ref, ac_ref,
             lsec_ref, yc_ref, yn_ref, sinks_ref, ga_ref, gc_ref, cw_ref,
             dproj_ref, dkv_ref, dga_ref, dgc_ref, dsinks_ref, dcw_ref, dk_carry, dv_carry):
        n = pl.program_id(0)
        first = n == 0
        live = n < nb
        has_next = n < nb - 1
        cos_p, sgn_p = ropep_ref[:, 0:V7X_LANES], ropep_ref[:, V7X_LANES:2 * V7X_LANES]
        cos_c, sgn_c = ropec_ref[:, 0:V7X_LANES], ropec_ref[:, V7X_LANES:2 * V7X_LANES]

        @pl.when(first)
        def _():
            dk_carry[...] = jnp.zeros(dk_carry.shape, _F32)
            dv_carry[...] = jnp.zeros(dv_carry.shape, _F32)

        def write_kv(dk2, dv2, cos, sgn):
            lo = _lane((WINDOW, 128)) < HEAD_DIM
            for g in range(KV_WIDTH // 128):
                dk = jnp.where(lo, _fold_halves(dk2[2 * g]), _fold_halves(dk2[2 * g + 1]))
                dv = jnp.where(lo, _fold_halves(dv2[2 * g]), _fold_halves(dv2[2 * g + 1]))
                dkv_ref[:, 128 * g:128 * g + 128] = _rope(dk, cos, sgn, -1.0).astype(dkv_ref.dtype)
                dkv_ref[:, KV_WIDTH + 128 * g:KV_WIDTH + 128 * g + 128] = dv.astype(dkv_ref.dtype)

        @pl.when(jnp.logical_not(live))
        def _():
            write_kv([dk_carry[h] for h in range(N_KV_HEADS)], [dv_carry[h] for h in range(N_KV_HEADS)], cos_c, sgn_c)

        @pl.when(live)
        def _():
            block_step(pp_ref, pc_ref, pn_ref, dmc_ref, dmn_ref, ac_ref, lsec_ref, yc_ref, yn_ref, sinks_ref, ga_ref,
                       gc_ref, cw_ref, dproj_ref, dga_ref, dgc_ref, dsinks_ref, dcw_ref, dk_carry, dv_carry,
                       first, has_next, cos_p, sgn_p, cos_c, sgn_c, write_kv)

    def block_step(pp_ref, pc_ref, pn_ref, dmc_ref, dmn_ref, ac_ref, lsec_ref, yc_ref, yn_ref, sinks_ref, ga_ref,
                   gc_ref, cw_ref, dproj_ref, dga_ref, dgc_ref, dsinks_ref, dcw_ref, dk_carry, dv_carry,
                   first, has_next, cos_p, sgn_p, cos_c, sgn_c, write_kv):
        da_c, dga = _rms_bwd(dmc_ref[:, 0:aw], ac_ref[...], ga_ref[...])
        _accumulate(dga_ref, dga, first)
        kk = jnp.concatenate(
            [jnp.concatenate([_rope(ref[:, dm.o_k + 128 * g:dm.o_k + 128 * g + 128], c, sg, 1.0)
                              for g in range(KV_WIDTH // 128)], axis=1)
             for ref, c, sg in ((pp_ref, cos_p, sgn_p), (pc_ref, cos_c, sgn_c))], axis=0)
        vv = jnp.concatenate([pp_ref[:, dm.o_v:dm.o_v + KV_WIDTH], pc_ref[:, dm.o_v:dm.o_v + KV_WIDTH]], axis=0)
        group, pairs = dm.group, dm.group // 2
        valid_c = _band_mask(group, 2 * WINDOW, first)
        dk_prev, dv_prev = [], []
        dsinks = jnp.zeros((1, nq), _F32)
        head_lane = _lane((1, nq))

        def stacked(q_ref, cos, sgn, da, o_ref, lse_ref_, h):
            cols = [slice(128 * j, 128 * j + 128) for j in range(pairs * h, pairs * (h + 1))]
            q4 = _stack_heads([_rope(q_ref[:, c], cos, sgn, 1.0) for c in cols])
            do4 = _stack_heads([da[:, c] for c in cols])
            lo = _lane((WINDOW, 128)) < HEAD_DIM
            deltas = []
            for c in cols:
                prod = o_ref[:, c] * da[:, c]
                deltas += [jnp.sum(jnp.where(lo, prod, 0.0), axis=1, keepdims=True),
                           jnp.sum(jnp.where(lo, 0.0, prod), axis=1, keepdims=True)]
            lse4 = jnp.concatenate([lse_ref_[:, group * h + r:group * h + r + 1] for r in range(group)], axis=0)
            return q4, do4, lse4, jnp.concatenate(deltas, axis=0)

        def scores_bwd(q4, do4, lse4, delta4, keys, vals, valid):
            sc = _dot(q4, keys, "nt") * ATTN_SCALE
            p = jnp.exp(jnp.where(valid, sc - lse4, MASKED))
            return p.astype(_CDT), (p * (_dot(do4, vals, "nt") - delta4) * ATTN_SCALE).astype(_CDT)

        for h in range(N_KV_HEADS):
            k2, v2 = _dup_head(kk, h).astype(_CDT), _dup_head(vv, h).astype(_CDT)
            q4, do4, lse4, delta4 = stacked(pc_ref, cos_c, sgn_c, da_c, ac_ref, lsec_ref, h)
            p, ds = scores_bwd(q4, do4, lse4, delta4, k2, v2, valid_c)
            for i, dq in enumerate(_unstack_heads(_dot(ds, k2, "nn"), pairs)):
                j = pairs * h + i
                dproj_ref[:, 128 * j:128 * j + 128] = _rope(dq, cos_c, sgn_c, -1.0).astype(dproj_ref.dtype)
            dk = _dot(ds, q4, "tn")
            dv = _dot(p, do4, "tn")
            dk_prev.append(dk_carry[h] + dk[0:WINDOW])
            dv_prev.append(dv_carry[h] + dv[0:WINDOW])
            dk_carry[h] = dk[WINDOW:2 * WINDOW]
            dv_carry[h] = dv[WINDOW:2 * WINDOW]
            sink4 = _per_head([sinks_ref[0, group * h + r] for r in range(group)])
            loss_sink = jnp.exp(sink4 - lse4) * delta4
            for r in range(group):
                dsinks = dsinks + jnp.where(head_lane == group * h + r,
                                            -jnp.sum(loss_sink[WINDOW * r:WINDOW * (r + 1)]), 0.0)
        _accumulate(dsinks_ref, dsinks, first)
        write_kv(dk_prev, dv_prev, cos_p, sgn_p)

        bg = pc_ref[:, dm.o_bg:dm.o_bg + cw]
        yc = yc_ref[...]
        dconv, dgc = _rms_bwd(dmc_ref[:, aw:d], bg * yc, gc_ref[...])
        _accumulate(dgc_ref, dgc, first)
        dproj_ref[:, dm.o_bg:dm.o_bg + cw] = (dconv * yc).astype(dproj_ref.dtype)
        dy = dconv * bg
        bg_n = pn_ref[:, dm.o_bg:dm.o_bg + cw]
        dconv_n, _ = _rms_bwd(dmn_ref[:, aw:d], bg_n * yn_ref[...], gc_ref[...])
        halo = jnp.where(has_next, dconv_n * bg_n, 0.0)
        dy1 = _shift_up(dy, halo, 1)
        dy2 = _shift_up(dy, halo, 2)
        dz = cw_ref[2:3, :] * dy + cw_ref[1:2, :] * dy1 + cw_ref[0:1, :] * dy2
        cg = pc_ref[:, dm.o_cg:dm.o_cg + cw]
        u = pc_ref[:, dm.o_u:dm.o_u + cw]
        dproj_ref[:, dm.o_cg:dm.o_cg + cw] = (dz * u).astype(dproj_ref.dtype)
        dproj_ref[:, dm.o_u:dm.o_u + cw] = (dz * cg).astype(dproj_ref.dtype)
        z = cg * u
        dcw = jnp.concatenate(
            [jnp.sum(z * t, axis=0, keepdims=True) for t in (dy2, dy1, dy)]
            + [jnp.zeros((V7X_SUBLANES - 3, cw), _F32)], axis=0)
        _accumulate(dcw_ref, dcw, first)

    at = lambda n: jnp.minimum(n, nb - 1)
    prev = lambda n: (jnp.maximum(at(n) - 1, 0), 0)
    cur = lambda n: (at(n), 0)
    done = lambda n: (jnp.maximum(n - 1, 0), 0)
    nxt8 = lambda n: (jnp.minimum((at(n) + 1) * (WINDOW // V7X_SUBLANES), s // V7X_SUBLANES - 1), 0)
    fixed = lambda n: (0, 0)
    blocks = WINDOW * (2 * inw * 4 + d * 4 + aw * 4 + cw * 4 + inw * 2 + 2 * KV_WIDTH * 2)
    carry = [pltpu.VMEM((N_KV_HEADS, WINDOW, 128), _F32), pltpu.VMEM((N_KV_HEADS, WINDOW, 128), _F32)]
    n_in, n_out = 15, 6
    ca = _CommArgs(list(comm), n_in, n_out)
    return pl.pallas_call(
        _carrying(body, n_in, n_out, nb + 1, ca, n_scratch=len(carry)), name="mixer_bwd", grid=(nb + 1,),
        in_specs=[pl.BlockSpec((WINDOW, inw), prev), pl.BlockSpec((WINDOW, inw), cur), pl.BlockSpec((V7X_SUBLANES, inw), nxt8),
                  pl.BlockSpec((WINDOW, 2 * V7X_LANES), prev), pl.BlockSpec((WINDOW, 2 * V7X_LANES), cur),
                  pl.BlockSpec((WINDOW, d), cur), pl.BlockSpec((V7X_SUBLANES, d), nxt8),
                  pl.BlockSpec((WINDOW, aw), cur), pl.BlockSpec((WINDOW, nq), cur),
                  pl.BlockSpec((WINDOW, cw), cur), pl.BlockSpec((V7X_SUBLANES, cw), nxt8),
                  pl.BlockSpec(memory_space=pltpu.SMEM),
                  pl.BlockSpec((1, aw), fixed), pl.BlockSpec((1, cw), fixed), pl.BlockSpec((V7X_SUBLANES, cw), fixed)]
        + [_ANY] * len(ca.operands),
        out_specs=[pl.BlockSpec((WINDOW, inw), cur), pl.BlockSpec((WINDOW, 2 * KV_WIDTH), done),
                   pl.BlockSpec((1, aw), fixed), pl.BlockSpec((1, cw), fixed),
                   pl.BlockSpec((1, nq), fixed), pl.BlockSpec((V7X_SUBLANES, cw), fixed)] + [_ANY] * len(ca.out_shape),
        out_shape=[jax.ShapeDtypeStruct((s, inw), _CDT), jax.ShapeDtypeStruct((s, 2 * KV_WIDTH), _CDT),
                   jax.ShapeDtypeStruct((1, aw), _F32), jax.ShapeDtypeStruct((1, cw), _F32),
                   jax.ShapeDtypeStruct((1, nq), _F32), jax.ShapeDtypeStruct((V7X_SUBLANES, cw), _F32)] + ca.out_shape,
        scratch_shapes=carry + ca.sems, input_output_aliases=ca.aliases,
        compiler_params=pltpu.CompilerParams(dimension_semantics=("arbitrary",), vmem_limit_bytes=_vmem_limit(blocks)),
    )(proj, proj, proj, rope, rope, dmixed, dmixed, attn, lse, y, y, sinks, g_attn, g_conv, conv_w8, *ca.operands)


def _position():
    return lax.axis_index("x"), lax.axis_index("y"), lax.axis_index("c")


def _linear(px, py, pc):
    return 4 * px + 2 * py + pc


def _comm_kernel(name, comm):
    ca = _CommArgs(list(comm), 0, 0)
    n_cin, n_cout = len(ca.operands), len(ca.out_shape)

    def body(*refs):
        cin, cout, sems = refs[:n_cin], refs[n_cin:n_cin + n_cout], refs[n_cin + n_cout:]
        ca.start(cin, cout, sems)
        ca.middle(cin, cout, sems)
        ca.finish(cin, cout, sems)

    return pl.pallas_call(
        body, name=name, out_shape=ca.out_shape, in_specs=[_ANY] * n_cin, out_specs=[_ANY] * n_cout,
        scratch_shapes=ca.sems, input_output_aliases=ca.aliases,
    )(*ca.operands)


def _gather_op(units):
    n = len(units)
    inputs, outputs, aliases = [], [], {}
    for shard, _, _, _ in units:
        inputs.append(shard)
        outputs.append(jax.ShapeDtypeStruct((N_DEV * shard.shape[0], shard.shape[1]), shard.dtype))
    for u, (_, buf, _, _) in enumerate(units):
        if buf is not None:
            aliases[len(inputs)] = u
            inputs.append(buf)

    def plan(ins, outs, sems, north):
        send_sems, recv_sems, local_sems = sems
        x, y, c = _position()
        me, sibling = (x, y, c), (x, y, 1 - c)
        xn, yn, dg = (1 - x, y), (x, 1 - y), (1 - x, 1 - y)
        via, to, k_via, k_other = (yn, xn, 2, 1) if north else (xn, yn, 1, 2)

        def rows(u, px, py, pc):
            shard, _, r0, r1 = units[u]
            return outs[u].at[pl.ds(pl.multiple_of(_linear(px, py, pc) * shard.shape[0] + r0, 16), r1 - r0), :]

        def own(u):
            _, _, r0, r1 = units[u]
            return ins[u].at[pl.ds(r0, r1 - r0), :]

        def copy(u, k, block, to_, src=None):
            return pltpu.make_async_remote_copy(
                src_ref=rows(u, *block) if src is None else src, dst_ref=rows(u, *block),
                send_sem=send_sems.at[u, k], recv_sem=recv_sems.at[u, k], device_id=to_, device_id_type=_MESH)

        us = range(n)
        return dict(
            mine=[pltpu.make_async_copy(own(u), rows(u, *me), local_sems.at[u]) for u in us],
            first=[cp for u in us for cp in (copy(u, 0, me, sibling, src=own(u)), copy(u, 1, me, (*xn, c), src=own(u)),
                                             copy(u, 2, me, (*yn, c), src=own(u)))],
            relay=[copy(u, 3, (*via, c), (*to, c)) for u in us],
            arrived={1: [copy(u, 1, (*xn, c), me) for u in us], 2: [copy(u, 2, (*yn, c), me) for u in us],
                     3: [copy(u, 3, (*dg, c), me) for u in us]},
            passed={1: [copy(u, 4, (*xn, c), sibling) for u in us], 2: [copy(u, 5, (*yn, c), sibling) for u in us],
                    3: [copy(u, 6, (*dg, c), sibling) for u in us]},
            rest=[cp for u in us for cp in (copy(u, 0, sibling, me), copy(u, 4, (*xn, 1 - c), me),
                                            copy(u, 5, (*yn, 1 - c), me), copy(u, 6, (*dg, 1 - c), me))],
            k_via=k_via, k_other=k_other)

    def land(p, k):
        for arrived, onward in zip(p["arrived"][k], p["passed"][k]):
            arrived.wait_recv()
            onward.start()

    def by_core(fn):
        c = lax.axis_index("c")
        for north in (True, False):
            pl.when(c == (1 if north else 0))(functools.partial(fn, north))

    def start(ins, outs, sems):
        p = plan(ins, outs, sems, True)
        for cp in p["mine"] + p["first"]:
            cp.start()

    def middle(ins, outs, sems):
        def go(north):
            p = plan(ins, outs, sems, north)
            land(p, p["k_via"])
            for cp in p["relay"]:
                cp.start()
            land(p, p["k_other"])
        by_core(go)

    def finish(ins, outs, sems):
        def go(north):
            p = plan(ins, outs, sems, north)
            land(p, 3)
            for cp in p["rest"]:
                cp.wait_recv()
            for cp in p["first"] + p["relay"] + [cp for k in (1, 2, 3) for cp in p["passed"][k]]:
                cp.wait_send()
            for cp in p["mine"]:
                cp.wait()
        by_core(go)

    sems = [pltpu.SemaphoreType.DMA((n, 7)), pltpu.SemaphoreType.DMA((n, 7)), pltpu.SemaphoreType.DMA((n,))]
    return _Comm(inputs, outputs, aliases, sems, start, finish, middle)


def _peers(x, y, c):
    out = []
    for k in range(1, N_DEV):
        fx, fy, fc = (k >> 2) & 1, (k >> 1) & 1, k & 1
        out.append((1 - x if fx else x, 1 - y if fy else y, 1 - c if fc else c))
    return out


def _exchange_op(partials):
    n = len(partials)
    outputs = [jax.ShapeDtypeStruct((4, p.shape[0] // N_DEV, p.shape[1]), p.dtype) for p in partials]

    def plan(ins, outs, sems):
        send_sems, recv_sems = sems
        x, y, c = _position()
        out = []
        for a in range(n):
            r = outs[a].shape[1]
            for ch in range(4):
                out.append(pltpu.make_async_remote_copy(
                    src_ref=ins[a].at[pl.ds(pl.multiple_of((2 * ch + 1 - c) * r, 16), r), :], dst_ref=outs[a].at[ch],
                    send_sem=send_sems.at[a, ch], recv_sem=recv_sems.at[a, ch], device_id=(x, y, 1 - c),
                    device_id_type=_MESH))
        return out

    def start(ins, outs, sems):
        for cp in plan(ins, outs, sems):
            cp.start()

    def finish(ins, outs, sems):
        copies = plan(ins, outs, sems)
        for cp in copies:
            cp.wait_recv()
        for cp in copies:
            cp.wait_send()

    sems = [pltpu.SemaphoreType.DMA((n, 4)), pltpu.SemaphoreType.DMA((n, 4))]
    return _Comm(list(partials), outputs, {}, sems, start, finish)


def _chip_send_op(units):
    n = len(units)
    inputs, outputs, aliases = [], [], {}
    for q, _, _, _ in units:
        inputs.append(q)
        outputs.append(jax.ShapeDtypeStruct(q.shape, q.dtype))
    for u, (_, buf, _, _) in enumerate(units):
        if buf is not None:
            aliases[len(inputs)] = u
            inputs.append(buf)

    def plan(ins, outs, sems):
        send_sems, recv_sems, local_sems = sems
        x, y, c = _position()
        my_chip = 2 * x + y
        chips = [(1 - x, y), (x, 1 - y), (1 - x, 1 - y)]
        mine, sends, arrivals = [], [], []
        for u, (_, _, r0, r1) in enumerate(units):
            span = pl.ds(r0, r1 - r0)
            mine.append(pltpu.make_async_copy(ins[u].at[my_chip, span, :], outs[u].at[my_chip, span, :], local_sems.at[u]))
            for k, (px, py) in enumerate(chips):
                sends.append(pltpu.make_async_remote_copy(
                    src_ref=ins[u].at[2 * px + py, span, :], dst_ref=outs[u].at[my_chip, span, :],
                    send_sem=send_sems.at[u, k], recv_sem=recv_sems.at[u, k], device_id=(px, py, c), device_id_type=_MESH))
                arrivals.append(pltpu.make_async_remote_copy(
                    src_ref=ins[u].at[my_chip, span, :], dst_ref=outs[u].at[2 * px + py, span, :],
                    send_sem=send_sems.at[u, k], recv_sem=recv_sems.at[u, k], device_id=(px, py, c), device_id_type=_MESH))
        return mine, sends, arrivals

    def start(ins, outs, sems):
        mine, sends, _ = plan(ins, outs, sems)
        for cp in mine + sends:
            cp.start()

    def finish(ins, outs, sems):
        mine, sends, arrivals = plan(ins, outs, sems)
        for cp in arrivals:
            cp.wait_recv()
        for cp in sends:
            cp.wait_send()
        for cp in mine:
            cp.wait()

    sems = [pltpu.SemaphoreType.DMA((n, 3)), pltpu.SemaphoreType.DMA((n, 3)), pltpu.SemaphoreType.DMA((n,))]
    return _Comm(inputs, outputs, aliases, sems, start, finish)


def _pair_sum(name, partial, received):
    _, rows, cols = received.shape
    tr = _pick(rows, (352, 288, 256, 128, 64, 32, 16))
    p4 = partial.reshape(4, 2, rows, cols)
    kind = jnp.reshape(lax.axis_index("c"), (1,)).astype(jnp.int32)

    def body(kind_ref, p_ref, r_ref, o_ref):
        o_ref[0] = (p_ref[0, 0].astype(_F32) + r_ref[0].astype(_F32)).astype(o_ref.dtype)

    return pl.pallas_call(
        body, name=name,
        grid_spec=pltpu.PrefetchScalarGridSpec(
            num_scalar_prefetch=1, grid=(4, rows // tr),
            in_specs=[pl.BlockSpec((1, 1, tr, cols), lambda ch, i, kind_ref: (ch, kind_ref[0], i, 0)),
                      pl.BlockSpec((1, tr, cols), lambda ch, i, kind_ref: (ch, i, 0))],
            out_specs=pl.BlockSpec((1, tr, cols), lambda ch, i, kind_ref: (ch, i, 0))),
        out_shape=jax.ShapeDtypeStruct(received.shape, received.dtype),
        compiler_params=pltpu.CompilerParams(dimension_semantics=("arbitrary", "arbitrary")),
    )(kind, p4, received)


def _all_reduce_small(name, v):
    rows = v.shape[0]

    def body(v_ref, out_ref, land_ref, send_sems, recv_sems):
        x, y, c = _position()
        me = _linear(x, y, c)
        peers = _peers(x, y, c)
        land_ref[me] = v_ref[...]
        sends = [pltpu.make_async_remote_copy(
            src_ref=v_ref, dst_ref=land_ref.at[me], send_sem=send_sems.at[k], recv_sem=recv_sems.at[k],
            device_id=peer, device_id_type=_MESH) for k, peer in enumerate(peers)]
        for cp in sends:
            cp.start()
        for k, peer in enumerate(peers):
            pltpu.make_async_remote_copy(
                src_ref=v_ref, dst_ref=land_ref.at[_linear(*peer)], send_sem=send_sems.at[k], recv_sem=recv_sems.at[k],
                device_id=peer, device_id_type=_MESH).wait_recv()
        for cp in sends:
            cp.wait_send()
        total = land_ref[0]
        for s in range(1, N_DEV):
            total = total + land_ref[s]
        out_ref[...] = total

    return pl.pallas_call(
        body, name=name, out_shape=jax.ShapeDtypeStruct(v.shape, _F32),
        in_specs=[pl.BlockSpec(memory_space=pltpu.VMEM)], out_specs=pl.BlockSpec(memory_space=pltpu.VMEM),
        scratch_shapes=[pltpu.VMEM((N_DEV, rows, V7X_LANES), _F32), pltpu.SemaphoreType.DMA((7,)), pltpu.SemaphoreType.DMA((7,))],
    )(v)


def _adamw(name, w, slots, m, v):
    rows, cols = w.shape
    n_slots = slots.shape[0]
    tr = _pick(rows, (176, 144, 128, 64, 32, 16, 8))

    def body(w_ref, s_ref, m_ref, v_ref, g_ref, d_ref, nm_ref, nv_ref):
        g = s_ref[0].astype(_F32)
        for k in range(1, n_slots):
            g = g + s_ref[k].astype(_F32)
        nm = ADAM_B1 * m_ref[...] + (1.0 - ADAM_B1) * g
        nv = ADAM_B2 * v_ref[...] + (1.0 - ADAM_B2) * (g * g)
        m_hat = nm / (1.0 - ADAM_B1 ** ADAM_STEP)
        v_hat = nv / (1.0 - ADAM_B2 ** ADAM_STEP)
        g_ref[...] = g
        d_ref[...] = -ADAM_LR * (m_hat / (jnp.sqrt(v_hat) + ADAM_EPS) + ADAM_WD * w_ref[...])
        nm_ref[...] = nm
        nv_ref[...] = nv

    spec = pl.BlockSpec((tr, cols), lambda i: (i, 0))
    blocks = 7 * tr * cols * 4 + _nbytes((n_slots, tr, cols), slots.dtype)
    return pl.pallas_call(
        body, name=name, grid=(rows // tr,),
        in_specs=[spec, pl.BlockSpec((n_slots, tr, cols), lambda i: (0, i, 0)), spec, spec], out_specs=[spec] * 4,
        out_shape=[jax.ShapeDtypeStruct((rows, cols), _F32)] * 4,
        compiler_params=pltpu.CompilerParams(dimension_semantics=("arbitrary",), vmem_limit_bytes=_vmem_limit(blocks)),
    )(w, slots, m, v)


def _pad_rows(a, rows):
    return jnp.pad(a, ((0, rows - a.shape[0]), (0, 0)))


def _pack(parts):
    rows, spans, at = [], [], 0
    for p in parts:
        p = p.reshape(-1)
        r = -(-p.shape[0] // V7X_LANES)
        rows.append(jnp.pad(p, (0, r * V7X_LANES - p.shape[0])).reshape(r, V7X_LANES))
        spans.append((at, r, p.shape[0]))
        at += r
    packed = jnp.concatenate(rows, axis=0)
    return _pad_rows(packed, -(-at // V7X_SUBLANES) * V7X_SUBLANES), spans


def _unpack(packed, spans, shapes):
    return [packed[at:at + r].reshape(-1)[:size].reshape(shape) for (at, r, size), shape in zip(spans, shapes)]


def kernel(x, positions, w_in, conv_w, sinks, g_attn, g_conv, w_out, ln1_g, ln1_b, w_gate, w_up, w_down, ln2_g, ln2_b, loss_target, m_w_in, m_conv_w, m_sinks, m_g_attn, m_g_conv, m_w_out, m_ln1_g, m_ln1_b, m_w_gate, m_w_up, m_w_down, m_ln2_g, m_ln2_b, v_w_in, v_conv_w, v_sinks, v_g_attn, v_g_conv, v_w_out, v_ln1_g, v_ln1_b, v_w_gate, v_w_up, v_w_down, v_ln2_g, v_ln2_b):
    _, s, d = x.shape
    d_ff = N_DEV * w_gate.shape[2]
    dm = _Dims(s, d, d_ff)
    aw, cw, nq, inw = dm.aw, dm.cw, dm.nq, dm.inw
    x2 = x[0]
    pos = positions[0].reshape(s, 1)
    inv_freq = ROPE_THETA ** (-jnp.arange(0, ROT_DIM, 2, dtype=_F32) / ROT_DIM)
    invf = jnp.tile(inv_freq, V7X_LANES // (ROT_DIM // 2)).reshape(1, V7X_LANES)

    conv_cols = conv_w.shape[2]
    sh_in, sh_out = w_in[0].T.astype(_CDT), w_out[0].astype(_CDT)
    sh_gate, sh_up, sh_down = w_gate[0].T.astype(_CDT), w_up[0].T.astype(_CDT), w_down[0].astype(_CDT)
    r_in, r_out, r_ff = sh_in.shape[0], sh_out.shape[0], sh_gate.shape[0]
    q_ff = r_ff // 4
    assert q_ff % 16 == 0
    def prepare_body(x_ref, pos_ref, invf_ref, xc_ref, rope_ref):
        xc_ref[...] = x_ref[...].astype(_CDT)
        cos, sgn = _rope_tables(pos_ref[...], invf_ref[...])
        rope_ref[:, 0:V7X_LANES] = cos
        rope_ref[:, V7X_LANES:2 * V7X_LANES] = sgn

    x_c, rope, w_in_t, conv_all = _row_kernel(
        "prepare_gather_w_in", prepare_body, [x2, pos], [invf], [((s, d), _CDT), ((s, 2 * V7X_LANES), _F32)], [],
        comm=[_gather_op([(sh_in, None, 0, r_in), (_pad_rows(conv_w[0], 16), None, 0, 16)])])
    conv_full = conv_all.reshape(N_DEV, 16, conv_cols)[:, :3, :].transpose(1, 0, 2).reshape(3, cw)
    conv_w8 = _pad_rows(conv_full, V7X_SUBLANES)

    tm = _pick(s, (1024, 512, 256, 128))
    tm2 = _pick(s, (2048, 1024, 512, 256, 128))
    tr = _pick(s, (512, 256, 128))
    tn_in = _pick(inw, (512, 256, 128))
    tn_ff = _pick(d_ff, (512, 256, 128))

    proj, w_out_f, w_gate_t = _matmul(
        "proj", [[(x_c, w_in_t, "nt")]], s, inw, d, tm2, tn_in, d, [],
        [((s, inw), _F32, (tm2, tn_in), _tile_ij)], _store_epilogue,
        comm=[_gather_op([(sh_out, None, 0, r_out), (sh_gate, None, 0, 2 * q_ff)])])
    mixed, attn, lse, y_conv, w_gate_t, w_up_t = _mixer_fwd(
        dm, proj, rope, sinks, g_attn, g_conv, conv_w8,
        comm=[_gather_op([(sh_gate, w_gate_t, 2 * q_ff, r_ff), (sh_up, None, 0, 2 * q_ff)])])

    def residual_epilogue(accs, ex, out, first):
        out[0][...] = DEEPNORM_ALPHA * ex[0][...] + accs[0]

    tn_d = _pick(d, (512,))
    r1, w_up_t = _matmul(
        "out_proj", [[(mixed, w_out_f, "nn")]], s, d, d, tm, tn_d, d, [(x2, (tm, tn_d), _tile_ij)],
        [((s, d), _F32, (tm, tn_d), _tile_ij)], residual_epilogue,
        comm=[_gather_op([(sh_up, w_up_t, 2 * q_ff, 3 * q_ff)])])
    h1, h1_c, w_up_t = _ln1_fwd_rows(r1, ln1_g, ln1_b, comm=[_gather_op([(sh_up, w_up_t, 3 * q_ff, r_ff)])])

    def swiglu_epilogue(accs, ex, out, first):
        gate_v, up_v = accs
        out[0][...] = gate_v
        out[1][...] = up_v
        out[2][...] = (gate_v * jax.nn.sigmoid(gate_v) * up_v).astype(_CDT)

    gate, up, act, w_down_f = _matmul(
        "gate_up", [[(h1_c, w_gate_t, "nt")], [(h1_c, w_up_t, "nt")]], s, d_ff, d, tm, tn_ff, d, [],
        [((s, d_ff), _F32, (tm, tn_ff), _tile_ij), ((s, d_ff), _F32, (tm, tn_ff), _tile_ij),
         ((s, d_ff), _CDT, (tm, tn_ff), _tile_ij)], swiglu_epilogue,
        comm=[_gather_op([(sh_down, None, 0, r_ff)])])

    (r2,) = _matmul("down", [[(act, w_down_f, "nn")]], s, d, d_ff, tm, tn_d, d_ff, [(h1, (tm, tn_d), _tile_ij)],
                    [((s, d), _F32, (tm, tn_d), _tile_ij)], residual_epilogue)
    dr2, dr2_c, loss_acc, d_ln2_g, d_ln2_b = _ln2_loss_bwd(r2, loss_target[0], ln2_g, ln2_b)

    def swiglu_bwd_epilogue(accs, ex, out, first):
        gate_v, up_v = ex[0][...], ex[1][...]
        sig = jax.nn.sigmoid(gate_v)
        out[0][...] = (accs[0] * up_v * (sig * (1.0 + gate_v * (1.0 - sig)))).astype(_CDT)
        out[1][...] = (accs[0] * (gate_v * sig)).astype(_CDT)

    dgate, dup = _matmul(
        "dact", [[(dr2_c, w_down_f, "nt")]], s, d_ff, d, tm2, tn_ff, d,
        [(gate, (tm2, tn_ff), _tile_ij), (up, (tm2, tn_ff), _tile_ij)],
        [((s, d_ff), _CDT, (tm2, tn_ff), _tile_ij), ((s, d_ff), _CDT, (tm2, tn_ff), _tile_ij)], swiglu_bwd_epilogue)
    def weight_grad(name, a, b, comm=()):
        rows = a.shape[1]
        tw, tn_w = _pick(rows, (512, 256, 128)), _pick(d, (1024, 512))
        return _matmul(name, [[(a, b, "tn")]], rows, d, s, tw, tn_w, s, [],
                       [((rows, d), _CDT, (tw, tn_w), _tile_ij)], _store_epilogue, comm=comm, j_outer=True)

    (dw_down,) = weight_grad("dw_down", act, dr2_c)
    dw_gate_t, x_down = weight_grad("dw_gate", dgate, h1_c, comm=[_exchange_op([dw_down])])
    q_down = _pair_sum("chip_sum_w_down", dw_down, x_down)
    dw_up_t, l_down, x_gate = weight_grad(
        "dw_up", dup, h1_c, comm=[_chip_send_op([(q_down, None, 0, 2 * q_ff)]), _exchange_op([dw_gate_t])])
    q_gate = _pair_sum("chip_sum_w_gate", dw_gate_t, x_gate)

    tn_h = _pick(d, (512,))
    dh1, l_down, l_gate, x_up = _matmul(
        "dh1", [[(dgate, w_gate_t, "nn"), (dup, w_up_t, "nn")]], s, d, d_ff, tr, tn_h, d_ff,
        [(dr2, (tr, tn_h), _tile_ij)], [((s, d), _F32, (tr, tn_h), _tile_ij)], residual_epilogue,
        comm=[_chip_send_op([(q_down, l_down, 2 * q_ff, r_ff), (q_gate, None, 0, r_ff)]), _exchange_op([dw_up_t])])
    q_up = _pair_sum("chip_sum_w_up", dw_up_t, x_up)
    dr1, dr1_c, d_ln1_g, d_ln1_b = _ln1_bwd_rows(dh1, r1, ln1_g)
    (dmixed,) = _matmul("dmixed", [[(dr1_c, w_out_f, "nt")]], s, d, d, tm2, tn_d, d, [],
                        [((s, d), _F32, (tm2, tn_d), _tile_ij)], _store_epilogue)
    (dw_out,) = weight_grad("dw_out", mixed, dr1_c)
    dproj, dkv, d_g_attn, d_g_conv, d_sinks, d_conv8, l_up, x_out = _mixer_bwd(
        dm, proj, rope, sinks, g_attn, g_conv, conv_w8, dmixed, attn, lse, y_conv,
        comm=[_chip_send_op([(q_up, None, 0, r_ff)]), _exchange_op([dw_out])])
    dproj = _patch_columns("dproj_kv", dproj, dkv, dm.o_k)
    q_out = _pair_sum("chip_sum_w_out", dw_out, x_out)
    dw_in_t, l_out = weight_grad("dw_in", dproj, x_c, comm=[_chip_send_op([(q_out, None, 0, r_out)])])
    (x_in,) = _comm_kernel("exchange_w_in", [_exchange_op([dw_in_t])])
    q_in = _pair_sum("chip_sum_w_in", dw_in_t, x_in)

    grad_x, l_in = _matmul("dx", [[(dproj, w_in_t, "nn")]], s, d, inw, tm, tn_d, inw,
                           [(dr1, (tm, tn_d), _tile_ij)], [((s, d), _F32, (tm, tn_d), _tile_ij)], residual_epilogue,
                           comm=[_chip_send_op([(q_in, None, 0, r_in)])])

    small_parts = [d_conv8[:3], d_sinks, d_g_attn, d_g_conv, d_ln1_g, d_ln1_b, d_ln2_g, d_ln2_b, loss_acc[0:1, 0:1]]
    packed, spans = _pack(small_parts)
    reduced = _unpack(_all_reduce_small("reduce_small", packed), spans, [p.shape for p in small_parts])
    g_conv_full, g_sinks, g_g_attn, g_g_conv, g_ln1_g, g_ln1_b, g_ln2_g, g_ln2_b, loss_sum = reduced
    me = _linear(*_position())
    g_conv_w = lax.dynamic_slice(g_conv_full, (0, me * conv_cols), (3, conv_cols))
    loss = loss_sum[0, 0]

    big = {"w_in": (w_in[0].T, l_in, m_w_in[0].T, v_w_in[0].T), "w_out": (w_out[0], l_out, m_w_out[0], v_w_out[0]),
           "w_gate": (w_gate[0].T, l_gate, m_w_gate[0].T, v_w_gate[0].T),
           "w_up": (w_up[0].T, l_up, m_w_up[0].T, v_w_up[0].T), "w_down": (w_down[0], l_down, m_w_down[0], v_w_down[0])}
    res = {nm: tuple(_adamw(f"adamw_{nm}", w, slots, m, v)) for nm, (w, slots, m, v) in big.items()}
    for nm in ("w_in", "w_gate", "w_up"):
        res[nm] = tuple(a.T for a in res[nm])
    small_names = ["conv_w", "sinks", "g_attn", "g_conv", "ln1_g", "ln1_b", "ln2_g", "ln2_b"]
    small_w = [conv_w, sinks, g_attn, g_conv, ln1_g, ln1_b, ln2_g, ln2_b]
    small_g = [g_conv_w[None], g_sinks, g_g_attn, g_g_conv, g_ln1_g, g_ln1_b, g_ln2_g, g_ln2_b]
    small_m = [m_conv_w, m_sinks, m_g_attn, m_g_conv, m_ln1_g, m_ln1_b, m_ln2_g, m_ln2_b]
    small_v = [v_conv_w, v_sinks, v_g_attn, v_g_conv, v_ln1_g, v_ln1_b, v_ln2_g, v_ln2_b]
    pw, sp = _pack(small_w)
    pg, _ = _pack(small_g)
    pm, _ = _pack(small_m)
    pv, _ = _pack(small_v)
    shapes = [w.shape for w in small_w]
    _, sd, sm, sv = [_unpack(p, sp, shapes) for p in _adamw("adamw_small", pw, pg[None], pm, pv)]
    for i, nm in enumerate(small_names):
        res[nm] = (small_g[i].reshape(shapes[i]), sd[i], sm[i], sv[i])

    order = ["w_in", "conv_w", "sinks", "g_attn", "g_conv", "w_out", "ln1_g", "ln1_b", "w_gate", "w_up", "w_down", "ln2_g", "ln2_b"]

    def lead(a, nm):
        return a[None] if nm in big else a

    return (loss, grad_x[None],
            *[lead(res[nm][0], nm) for nm in order], *[lead(res[nm][1], nm) for nm in order],
            *[lead(res[nm][2], nm) for nm in order], *[lead(res[nm][3], nm) for nm in order])
```

```python
import functools

import jax
import jax.numpy as jnp
from jax import lax
from jax.experimental import pallas as pl
from jax.experimental.pallas import tpu as pltpu

_F32 = jnp.float32
_CDT = jnp.bfloat16

HEAD_DIM = 64
WINDOW = 128
N_KV_HEADS = 4
KV_WIDTH = N_KV_HEADS * HEAD_DIM
ROT_DIM = HEAD_DIM // 4
ROPE_THETA = 500000.0
ATTN_SCALE = HEAD_DIM ** -0.5
DEPTH = 1
DEEPNORM_ALPHA = (2 * DEPTH) ** 0.25
LN_EPS = 1e-5
RMS_EPS = 1e-6
ADAM_LR = 0.001
ADAM_B1 = 0.9
ADAM_B2 = 0.999
ADAM_EPS = 1e-08
ADAM_WD = 0.01
ADAM_STEP = 10
N_DEV = 8
MASKED = -1e30

MIB = 1024 * 1024
V7X_VMEM_BYTES = 64 * MIB
V7X_LANES = 128
V7X_SUBLANES = 8
BODY_TEMPORARIES_BYTES = 16 * MIB
VMEM_LIMIT_FLOOR_BYTES = 16 * MIB
VMEM_LIMIT_CEILING_BYTES = V7X_VMEM_BYTES - 8 * MIB
_MESH = pl.DeviceIdType.MESH
_ANY = pl.BlockSpec(memory_space=pl.ANY)


def _vmem_limit(block_bytes, scratch_bytes=0):
    want = 2 * block_bytes + scratch_bytes + BODY_TEMPORARIES_BYTES
    return int(min(max(want, VMEM_LIMIT_FLOOR_BYTES), VMEM_LIMIT_CEILING_BYTES))


def _nbytes(shape, dtype):
    n = 1
    for s in shape:
        n *= s
    return n * jnp.dtype(dtype).itemsize


def _pick(n, candidates):
    for c in candidates:
        if n % c == 0:
            return c
    raise ValueError(f"no tile of {candidates} divides {n}")


_DOT_DIMS = {"nn": ((1,), (0,)), "nt": ((1,), (1,)), "tn": ((0,), (0,))}


def _dot(a, b, mode):
    return lax.dot_general(a.astype(_CDT), b.astype(_CDT), (_DOT_DIMS[mode], ((), ())),
                           preferred_element_type=_F32)


def _accumulate(ref, val, first):
    @pl.when(first)
    def _():
        ref[...] = val

    @pl.when(jnp.logical_not(first))
    def _():
        ref[...] += val


class _Comm:
    def __init__(self, inputs, outputs, aliases, sems, start, finish, middle=None):
        self.inputs, self.outputs, self.aliases, self.sems = inputs, outputs, aliases, sems
        self.start, self.finish, self.middle = start, finish, middle


def _middle_step(n_steps):
    return (2 * n_steps) // 3


class _CommArgs:
    def __init__(self, comms, n_in_before, n_out_before):
        self.comms, self.operands, self.out_shape, self.aliases, self.sems, self.at = comms, [], [], {}, [], []
        for cm in comms:
            self.at.append((len(self.operands), len(self.out_shape), len(self.sems)))
            for i_in, i_out in cm.aliases.items():
                self.aliases[n_in_before + len(self.operands) + i_in] = n_out_before + len(self.out_shape) + i_out
            self.operands += cm.inputs
            self.out_shape += cm.outputs
            self.sems += cm.sems

    def _each(self, in_refs, out_refs, sem_refs):
        for cm, (i0, o0, s0) in zip(self.comms, self.at):
            yield cm, (in_refs[i0:i0 + len(cm.inputs)], out_refs[o0:o0 + len(cm.outputs)], sem_refs[s0:s0 + len(cm.sems)])

    def start(self, in_refs, out_refs, sem_refs):
        for cm, refs in self._each(in_refs, out_refs, sem_refs):
            cm.start(*refs)

    def finish(self, in_refs, out_refs, sem_refs):
        for cm, refs in self._each(in_refs, out_refs, sem_refs):
            cm.finish(*refs)

    @property
    def has_middle(self):
        return any(cm.middle is not None for cm in self.comms)

    def middle(self, in_refs, out_refs, sem_refs):
        for cm, refs in self._each(in_refs, out_refs, sem_refs):
            if cm.middle is not None:
                cm.middle(*refs)


def _matmul(name, groups, m, n, k, tm, tn, tk, extras, outs, epilogue, comm=(), j_outer=False):
    assert m % tm == 0 and n % tn == 0 and k % tk == 0, (name, m, n, k, tm, tn, tk)
    nk = k // tk
    terms = [t for g in groups for t in g]
    operands, in_specs, block_bytes = [], [], 0

    def spec(blk, imap):
        return pl.BlockSpec(blk, (lambda g0, g1, kk: imap(g1, g0, kk)) if j_outer else imap)

    for a, b, mode in terms:
        assert a.shape == ((k, m) if mode == "tn" else (m, k)), (name, a.shape, mode)
        assert b.shape == ((n, k) if mode == "nt" else (k, n)), (name, b.shape, mode)
        if mode == "tn":
            a_blk, a_map = (tk, tm), (lambda i, j, kk: (kk, i))
        else:
            a_blk, a_map = (tm, tk), (lambda i, j, kk: (i, kk))
        if mode == "nt":
            b_blk, b_map = (tn, tk), (lambda i, j, kk: (j, kk))
        else:
            b_blk, b_map = (tk, tn), (lambda i, j, kk: (kk, j))
        operands += [a, b]
        in_specs += [spec(a_blk, a_map), spec(b_blk, b_map)]
        block_bytes += _nbytes(a_blk, a.dtype) + _nbytes(b_blk, b.dtype)
    for arr, blk, imap in extras:
        operands.append(arr)
        in_specs.append(spec(blk, lambda i, j, kk, imap=imap: imap(i, j)))
        block_bytes += _nbytes(blk, arr.dtype)
    out_shape, out_specs = [], []
    for shape, dtype, blk, imap in outs:
        out_shape.append(jax.ShapeDtypeStruct(shape, dtype))
        out_specs.append(spec(blk, lambda i, j, kk, imap=imap: imap(i, j)))
        block_bytes += _nbytes(blk, dtype)
    n_terms, n_extra, n_out, n_groups = len(terms), len(extras), len(outs), len(groups)
    scratch = [pltpu.VMEM((tm, tn), _F32) for _ in range(n_groups)] if nk > 1 else []
    ca = _CommArgs(list(comm), len(operands), n_out)
    n_cin, n_cout, n_acc = len(ca.operands), len(ca.out_shape), len(scratch)
    tiles = (m // tm, n // tn)
    grid = (tiles[1], tiles[0], nk) if j_outer else (tiles[0], tiles[1], nk)

    def body(*refs):
        refs = list(refs)
        term_refs = [refs.pop(0) for _ in range(2 * n_terms)]
        extra_refs = [refs.pop(0) for _ in range(n_extra)]
        cin_refs = [refs.pop(0) for _ in range(n_cin)]
        out_refs = [refs.pop(0) for _ in range(n_out)]
        cout_refs = [refs.pop(0) for _ in range(n_cout)]
        acc_refs = [refs.pop(0) for _ in range(n_acc)]
        sem_refs = refs
        g0, g1, kk = pl.program_id(0), pl.program_id(1), pl.program_id(2)
        first = jnp.logical_and(g0 == 0, g1 == 0)
        if comm:
            @pl.when(jnp.logical_and(first, kk == 0))
            def _():
                ca.start(cin_refs, cout_refs, sem_refs)
        if ca.has_middle:
            step = (g0 * grid[1] + g1) * nk + kk

            @pl.when(step == _middle_step(grid[0] * grid[1] * nk))
            def _():
                ca.middle(cin_refs, cout_refs, sem_refs)
        partial, t = [], 0
        for g in groups:
            s = None
            for _, _, mode in g:
                d = _dot(term_refs[2 * t][...], term_refs[2 * t + 1][...], mode)
                s = d if s is None else s + d
                t += 1
            partial.append(s)
        if nk == 1:
            epilogue(partial, extra_refs, out_refs, first)
        else:
            for acc, p in zip(acc_refs, partial):
                _accumulate(acc, p, kk == 0)

            @pl.when(kk == nk - 1)
            def _():
                epilogue([acc[...] for acc in acc_refs], extra_refs, out_refs, first)
        if comm:
            @pl.when(jnp.logical_and(jnp.logical_and(g0 == grid[0] - 1, g1 == grid[1] - 1), kk == nk - 1))
            def _():
                ca.finish(cin_refs, cout_refs, sem_refs)

    res = pl.pallas_call(
        body, name=name, grid=grid,
        in_specs=in_specs + [_ANY] * n_cin, out_specs=out_specs + [_ANY] * n_cout,
        out_shape=out_shape + ca.out_shape, scratch_shapes=scratch + ca.sems, input_output_aliases=ca.aliases,
        compiler_params=pltpu.CompilerParams(
            dimension_semantics=("arbitrary", "arbitrary", "arbitrary"),
            vmem_limit_bytes=_vmem_limit(block_bytes, n_groups * tm * tn * 4 if nk > 1 else 0)),
    )(*operands, *ca.operands)
    return list(res[:n_out]) + list(res[n_out:])


def _store_epilogue(accs, extra_refs, out_refs, first):
    for acc, ref in zip(accs, out_refs):
        ref[...] = acc.astype(ref.dtype)


def _tile_ij(i, j):
    return (i, j)


def _row_i(i, j):
    return (i, 0)


def _whole(i, j):
    return (0, 0)


def _mean(v):
    return jnp.mean(v, axis=-1, keepdims=True)


def _ln_fwd(r, g, b):
    xc = r - _mean(r)
    rstd = lax.rsqrt(_mean(xc * xc) + LN_EPS)
    xhat = xc * rstd
    return xhat * g + b, xhat, rstd


def _ln_bwd(dy, xhat, rstd, g):
    dxh = dy * g
    dr = rstd * (dxh - _mean(dxh) - xhat * _mean(dxh * xhat))
    return dr, jnp.sum(dy * xhat, axis=0, keepdims=True), jnp.sum(dy, axis=0, keepdims=True)


def _rms_fwd(a, g):
    rstd = lax.rsqrt(_mean(a * a) + RMS_EPS)
    return a * rstd * g


def _rms_bwd(dm, a, g):
    rstd = lax.rsqrt(_mean(a * a) + RMS_EPS)
    nhat = a * rstd
    dn = dm * g
    da = rstd * (dn - nhat * _mean(dn * nhat))
    return da, jnp.sum(dm * nhat, axis=0, keepdims=True)


def _lane(shape):
    return lax.broadcasted_iota(jnp.int32, shape, 1)


def _row(shape):
    return lax.broadcasted_iota(jnp.int32, shape, 0)


def _rope_tables(pos, invf):
    ang = pos.astype(_F32) * invf
    lane = _lane(ang.shape)
    in_rot = (lane % HEAD_DIM) < ROT_DIM
    first = (lane % ROT_DIM) < ROT_DIM // 2
    cos = jnp.where(in_rot, jnp.cos(ang), 1.0)
    sin = jnp.sin(ang)
    sgn = jnp.where(in_rot, jnp.where(first, -sin, sin), 0.0)
    return cos, sgn


def _rope(t, cos, sgn, sign):
    half = ROT_DIM // 2
    first = (_lane(t.shape) % ROT_DIM) < half
    partner = jnp.where(first, pltpu.roll(t, V7X_LANES - half, 1), pltpu.roll(t, half, 1))
    return t * cos + partner * (sgn * sign)


def _dup_head(t, h):
    g = t[:, 128 * (h // 2):128 * (h // 2) + 128]
    r = pltpu.roll(g, HEAD_DIM, 1)
    lo = _lane(g.shape) < HEAD_DIM
    return jnp.where(lo, g, r) if h % 2 == 0 else jnp.where(lo, r, g)


def _fold_halves(t):
    return t + pltpu.roll(t, HEAD_DIM, 1)


def _halves(t):
    lo = _lane(t.shape) < HEAD_DIM
    zero = jnp.zeros_like(t)
    return jnp.where(lo, t, zero), jnp.where(lo, zero, t)


def _band_mask(n_heads, n_keys, first_block):
    shape = (n_heads * WINDOW, n_keys)
    i = jnp.bitwise_and(_row(shape), WINDOW - 1)
    j = _lane(shape)
    valid = jnp.logical_and(j >= i + 1, j <= i + WINDOW)
    if first_block is not None:
        valid = jnp.logical_and(valid, jnp.logical_or(j >= WINDOW, jnp.logical_not(first_block)))
    return valid


def _stack_heads(pairs):
    return jnp.concatenate([half for t in pairs for half in _halves(t)], axis=0).astype(_CDT)


def _unstack_heads(t, n_pairs):
    lo = _lane((WINDOW, 128)) < HEAD_DIM
    return [jnp.where(lo, t[2 * WINDOW * i:2 * WINDOW * i + WINDOW], t[2 * WINDOW * i + WINDOW:2 * WINDOW * (i + 1)])
            for i in range(n_pairs)]


def _per_head(values):
    n_rows = len(values) * WINDOW
    block = jnp.right_shift(_row((n_rows, 1)), WINDOW.bit_length() - 1)
    out = jnp.zeros((n_rows, 1), _F32)
    for k, v in enumerate(values):
        out = jnp.where(block == k, v, out)
    return out


def _shift_down(z, halo, k):
    out = pltpu.roll(z, k, 0)
    r = _row(z.shape)
    for t in range(k):
        out = jnp.where(r == t, halo[V7X_SUBLANES - k + t:V7X_SUBLANES - k + t + 1, :], out)
    return out


def _shift_up(z, halo, k):
    rows = z.shape[0]
    out = pltpu.roll(z, rows - k, 0)
    r = _row(z.shape)
    for t in range(k):
        out = jnp.where(r == rows - k + t, halo[t:t + 1, :], out)
    return out


class _Dims:
    def __init__(self, s, d, d_ff):
        self.s, self.d, self.d_ff = s, d, d_ff
        self.aw = d // 2
        self.cw = d - self.aw
        self.nq = self.aw // HEAD_DIM
        self.group = self.nq // N_KV_HEADS
        assert self.group % 2 == 0, "a 128-lane pair of query heads must share its kv head"
        self.inw = self.aw + 2 * KV_WIDTH + 3 * self.cw
        self.o_k = self.aw
        self.o_v = self.aw + KV_WIDTH
        self.o_cg = self.aw + 2 * KV_WIDTH
        self.o_bg = self.o_cg + self.cw
        self.o_u = self.o_bg + self.cw
        self.nb = s // WINDOW
        assert s % WINDOW == 0


def _carrying(body, n_in, n_out, n_steps, ca, n_scratch=0):
    n_cin, n_cout = len(ca.operands), len(ca.out_shape)

    def wrapped(*refs):
        refs = list(refs)
        in_refs = [refs.pop(0) for _ in range(n_in)]
        cin_refs = [refs.pop(0) for _ in range(n_cin)]
        out_refs = [refs.pop(0) for _ in range(n_out)]
        cout_refs = [refs.pop(0) for _ in range(n_cout)]
        scratch_refs = [refs.pop(0) for _ in range(n_scratch)]
        if ca.comms:
            @pl.when(pl.program_id(0) == 0)
            def _():
                ca.start(cin_refs, cout_refs, refs)
        if ca.has_middle:
            @pl.when(pl.program_id(0) == _middle_step(n_steps))
            def _():
                ca.middle(cin_refs, cout_refs, refs)
        body(*in_refs, *out_refs, *scratch_refs)
        if ca.comms:
            @pl.when(pl.program_id(0) == n_steps - 1)
            def _():
                ca.finish(cin_refs, cout_refs, refs)

    return wrapped


def _row_kernel(name, body, rows_in, vecs_in, rows_out, vecs_out, comm=()):
    s = rows_in[0].shape[0]
    tr = _pick(s, (256, 128))
    row = lambda a: pl.BlockSpec((tr, a[1] if isinstance(a, tuple) else a.shape[1]), lambda i: (i, 0))
    vec = lambda shape: pl.BlockSpec(tuple(shape), lambda i: (0, 0))
    n_in, n_out = len(rows_in) + len(vecs_in), len(rows_out) + len(vecs_out)
    ca = _CommArgs(list(comm), n_in, n_out)
    blocks = sum(_nbytes((tr, a.shape[1]), a.dtype) for a in rows_in) + sum(_nbytes((tr, sh[1]), dt) for sh, dt in rows_out)
    res = pl.pallas_call(
        _carrying(body, n_in, n_out, s // tr, ca), name=name, grid=(s // tr,),
        in_specs=[row(a) for a in rows_in] + [vec(v.shape) for v in vecs_in] + [_ANY] * len(ca.operands),
        out_specs=[row(sh) for sh, _ in rows_out] + [vec(sh) for sh, _ in vecs_out] + [_ANY] * len(ca.out_shape),
        out_shape=[jax.ShapeDtypeStruct(sh, dt) for sh, dt in list(rows_out) + list(vecs_out)] + ca.out_shape,
        scratch_shapes=ca.sems, input_output_aliases=ca.aliases,
        compiler_params=pltpu.CompilerParams(dimension_semantics=("arbitrary",), vmem_limit_bytes=_vmem_limit(blocks)),
    )(*rows_in, *vecs_in, *ca.operands)
    return list(res)


def _ln2_loss_bwd(r2, target, gain, bias, comm=()):
    s, d = r2.shape

    def body(r_ref, t_ref, g_ref, b_ref, dr_ref, drc_ref, loss_ref, dg_ref, db_ref):
        first = pl.program_id(0) == 0
        yv, xhat, rstd = _ln_fwd(r_ref[...], g_ref[...], b_ref[...])
        err = yv - t_ref[...]
        dr2, dg, db = _ln_bwd(err * (1.0 / d), xhat, rstd, g_ref[...])
        dr_ref[...] = dr2
        drc_ref[...] = dr2.astype(_CDT)
        _accumulate(loss_ref, jnp.zeros(loss_ref.shape, _F32) + 0.5 * jnp.sum(err * err) * (1.0 / d), first)
        _accumulate(dg_ref, dg, first)
        _accumulate(db_ref, db, first)

    return _row_kernel("ln2_loss_bwd", body, [r2, target], [gain, bias], [((s, d), _F32), ((s, d), _CDT)],
                       [((V7X_SUBLANES, V7X_LANES), _F32), ((1, d), _F32), ((1, d), _F32)], comm)


def _ln1_fwd_rows(r1, gain, bias, comm=()):
    s, d = r1.shape

    def body(r_ref, g_ref, b_ref, h_ref, hc_ref):
        h1, _, _ = _ln_fwd(r_ref[...], g_ref[...], b_ref[...])
        h_ref[...] = h1
        hc_ref[...] = h1.astype(_CDT)

    return _row_kernel("ln1", body, [r1], [gain, bias], [((s, d), _F32), ((s, d), _CDT)], [], comm)


def _ln1_bwd_rows(dh1, r1, gain, comm=()):
    s, d = dh1.shape

    def body(dh_ref, r_ref, g_ref, dr_ref, drc_ref, dg_ref, db_ref):
        first = pl.program_id(0) == 0
        _, xhat, rstd = _ln_fwd(r_ref[...], g_ref[...], 0.0)
        dr1, dg, db = _ln_bwd(dh_ref[...], xhat, rstd, g_ref[...])
        dr_ref[...] = dr1
        drc_ref[...] = dr1.astype(_CDT)
        _accumulate(dg_ref, dg, first)
        _accumulate(db_ref, db, first)

    return _row_kernel("ln1_bwd", body, [dh1, r1], [gain], [((s, d), _F32), ((s, d), _CDT)],
                       [((1, d), _F32), ((1, d), _F32)], comm)


def _mixer_fwd(dm, proj, rope, sinks, g_attn, g_conv, conv_w8, comm=()):
    s, d, aw, cw, nq, inw, nb = dm.s, dm.d, dm.aw, dm.cw, dm.nq, dm.inw, dm.nb

    def body(pp_ref, pc_ref, ropep_ref, ropec_ref, sinks_ref, ga_ref, gc_ref, cw_ref,
             mixed_ref, attn_ref, lse_ref, y_ref):
        n = pl.program_id(0)
        cos_c, sgn_c = ropec_ref[:, 0:V7X_LANES], ropec_ref[:, V7X_LANES:2 * V7X_LANES]
        cos_p, sgn_p = ropep_ref[:, 0:V7X_LANES], ropep_ref[:, V7X_LANES:2 * V7X_LANES]
        kk = jnp.concatenate(
            [jnp.concatenate([_rope(ref[:, dm.o_k + 128 * g:dm.o_k + 128 * g + 128], c, sg, 1.0)
                              for g in range(KV_WIDTH // 128)], axis=1)
             for ref, c, sg in ((pp_ref, cos_p, sgn_p), (pc_ref, cos_c, sgn_c))], axis=0)
        vv = jnp.concatenate([pp_ref[:, dm.o_v:dm.o_v + KV_WIDTH], pc_ref[:, dm.o_v:dm.o_v + KV_WIDTH]], axis=0)
        group, pairs = dm.group, dm.group // 2
        valid = _band_mask(group, 2 * WINDOW, n == 0)
        for h in range(N_KV_HEADS):
            k2, v2 = _dup_head(kk, h).astype(_CDT), _dup_head(vv, h).astype(_CDT)
            q4 = _stack_heads([_rope(pc_ref[:, 128 * j:128 * j + 128], cos_c, sgn_c, 1.0)
                               for j in range(pairs * h, pairs * (h + 1))])
            sc = jnp.where(valid, _dot(q4, k2, "nt") * ATTN_SCALE, MASKED)
            sink = _per_head([sinks_ref[0, group * h + r] for r in range(group)])
            mx = jnp.maximum(jnp.max(sc, axis=1, keepdims=True), sink)
            p = jnp.exp(sc - mx)
            den = jnp.sum(p, axis=1, keepdims=True) + jnp.exp(sink - mx)
            out = _unstack_heads(_dot(p / den, v2, "nn"), pairs)
            lse = mx + jnp.log(den)
            for r in range(group):
                lse_ref[:, group * h + r:group * h + r + 1] = lse[WINDOW * r:WINDOW * (r + 1)]
            for i in range(pairs):
                j = pairs * h + i
                attn_ref[:, 128 * j:128 * j + 128] = out[i]
        mixed_ref[:, 0:aw] = _rms_fwd(attn_ref[...], ga_ref[...]).astype(mixed_ref.dtype)

        z = pc_ref[:, dm.o_cg:dm.o_cg + cw] * pc_ref[:, dm.o_u:dm.o_u + cw]
        top = WINDOW - V7X_SUBLANES
        halo = pp_ref[top:WINDOW, dm.o_cg:dm.o_cg + cw] * pp_ref[top:WINDOW, dm.o_u:dm.o_u + cw]
        halo = jnp.where(n == 0, jnp.zeros_like(halo), halo)
        y = cw_ref[0:1, :] * _shift_down(z, halo, 2) + cw_ref[1:2, :] * _shift_down(z, halo, 1) + cw_ref[2:3, :] * z
        y_ref[...] = y
        conv = pc_ref[:, dm.o_bg:dm.o_bg + cw] * y
        mixed_ref[:, aw:d] = _rms_fwd(conv, gc_ref[...]).astype(mixed_ref.dtype)

    prev = lambda n: (jnp.maximum(n - 1, 0), 0)
    cur = lambda n: (n, 0)
    fixed = lambda n: (0, 0)
    blocks = 2 * WINDOW * inw * 4 + WINDOW * (d * 2 + aw * 4 + cw * 4 + nq * 4)
    ca = _CommArgs(list(comm), 8, 4)
    return pl.pallas_call(
        _carrying(body, 8, 4, nb, ca), name="mixer_fwd", grid=(nb,),
        in_specs=[pl.BlockSpec((WINDOW, inw), prev), pl.BlockSpec((WINDOW, inw), cur),
                  pl.BlockSpec((WINDOW, 2 * V7X_LANES), prev), pl.BlockSpec((WINDOW, 2 * V7X_LANES), cur),
                  pl.BlockSpec(memory_space=pltpu.SMEM),
                  pl.BlockSpec((1, aw), fixed), pl.BlockSpec((1, cw), fixed), pl.BlockSpec((V7X_SUBLANES, cw), fixed)]
        + [_ANY] * len(ca.operands),
        out_specs=[pl.BlockSpec((WINDOW, d), cur), pl.BlockSpec((WINDOW, aw), cur),
                   pl.BlockSpec((WINDOW, nq), cur), pl.BlockSpec((WINDOW, cw), cur)] + [_ANY] * len(ca.out_shape),
        out_shape=[jax.ShapeDtypeStruct((s, d), _CDT), jax.ShapeDtypeStruct((s, aw), _F32),
                   jax.ShapeDtypeStruct((s, nq), _F32), jax.ShapeDtypeStruct((s, cw), _F32)] + ca.out_shape,
        scratch_shapes=ca.sems, input_output_aliases=ca.aliases,
        compiler_params=pltpu.CompilerParams(dimension_semantics=("arbitrary",), vmem_limit_bytes=_vmem_limit(blocks)),
    )(proj, proj, rope, rope, sinks, g_attn, g_conv, conv_w8, *ca.operands)


def _patch_columns(name, a, part, offset):
    s, pw = part.shape
    assert offset % pw == 0 and pw % V7X_LANES == 0
    tr = _pick(s, (512, 256, 128))

    def body(a_ref, p_ref, o_ref):
        del a_ref
        o_ref[...] = p_ref[...]

    return pl.pallas_call(
        body, name=name, grid=(s // tr,),
        in_specs=[_ANY, pl.BlockSpec((tr, pw), lambda i: (i, 0))],
        out_specs=pl.BlockSpec((tr, pw), lambda i: (i, offset // pw)),
        out_shape=jax.ShapeDtypeStruct(a.shape, a.dtype), input_output_aliases={0: 0},
        compiler_params=pltpu.CompilerParams(dimension_semantics=("arbitrary",)),
    )(a, part)


def _mixer_bwd(dm, proj, rope, sinks, g_attn, g_conv, conv_w8, dmixed, attn, lse, y, comm=()):
    s, d, aw, cw, nq, inw, nb = dm.s, dm.d, dm.aw, dm.cw, dm.nq, dm.inw, dm.nb

    def body(pp_ref, pc_ref, pn_ref, ropep_ref, ropec_ref, dmc_ref, dmn_ref, ac_ref,
             lsec_ref, yc_ref, yn_ref, sinks_ref, ga_ref, gc_ref, cw_ref,
             dproj_ref, dkv_ref, dga_ref, dgc_ref, dsinks_ref, dcw_ref, dk_carry, dv_carry):
        n = pl.program_id(0)
        first = n == 0
        live = n < nb
        has_next = n < nb - 1
        cos_p, sgn_p = ropep_ref[:, 0:V7X_LANES], ropep_ref[:, V7X_LANES:2 * V7X_LANES]
        cos_c, sgn_c = ropec_ref[:, 0:V7X_LANES], ropec_ref[:, V7X_LANES:2 * V7X_LANES]

        @pl.when(first)
        def _():
            dk_carry[...] = jnp.zeros(dk_carry.shape, _F32)
            dv_carry[...] = jnp.zeros(dv_carry.shape, _F32)

        def write_kv(dk2, dv2, cos, sgn):
            lo = _lane((WINDOW, 128)) < HEAD_DIM
            for g in range(KV_WIDTH // 128):
                dk = jnp.where(lo, _fold_halves(dk2[2 * g]), _fold_halves(dk2[2 * g + 1]))
                dv = jnp.where(lo, _fold_halves(dv2[2 * g]), _fold_halves(dv2[2 * g + 1]))
                dkv_ref[:, 128 * g:128 * g + 128] = _rope(dk, cos, sgn, -1.0).astype(dkv_ref.dtype)
                dkv_ref[:, KV_WIDTH + 128 * g:KV_WIDTH + 128 * g + 128] = dv.astype(dkv_ref.dtype)

        @pl.when(jnp.logical_not(live))
        def _():
            write_kv([dk_carry[h] for h in range(N_KV_HEADS)], [dv_carry[h] for h in range(N_KV_HEADS)], cos_c, sgn_c)

        @pl.when(live)
        def _():
            block_step(pp_ref, pc_ref, pn_ref, dmc_ref, dmn_ref, ac_ref, lsec_ref, yc_ref, yn_ref, sinks_ref, ga_ref,
                       gc_ref, cw_ref, dproj_ref, dga_ref, dgc_ref, dsinks_ref, dcw_ref, dk_carry, dv_carry,
                       first, has_next, cos_p, sgn_p, cos_c, sgn_c, write_kv)

    def block_step(pp_ref, pc_ref, pn_ref, dmc_ref, dmn_ref, ac_ref, lsec_ref, yc_ref, yn_ref, sinks_ref, ga_ref,
                   gc_ref, cw_ref, dproj_ref, dga_ref, dgc_ref, dsinks_ref, dcw_ref, dk_carry, dv_carry,
                   first, has_next, cos_p, sgn_p, cos_c, sgn_c, write_kv):
        da_c, dga = _rms_bwd(dmc_ref[:, 0:aw], ac_ref[...], ga_ref[...])
        _accumulate(dga_ref, dga, first)
        kk = jnp.concatenate(
            [jnp.concatenate([_rope(ref[:, dm.o_k + 128 * g:dm.o_k + 128 * g + 128], c, sg, 1.0)
                              for g in range(KV_WIDTH // 128)], axis=1)
             for ref, c, sg in ((pp_ref, cos_p, sgn_p), (pc_ref, cos_c, sgn_c))], axis=0)
        vv = jnp.concatenate([pp_ref[:, dm.o_v:dm.o_v + KV_WIDTH], pc_ref[:, dm.o_v:dm.o_v + KV_WIDTH]], axis=0)
        group, pairs = dm.group, dm.group // 2
        valid_c = _band_mask(group, 2 * WINDOW, first)
        dk_prev, dv_prev = [], []
        dsinks = jnp.zeros((1, nq), _F32)
        head_lane = _lane((1, nq))

        def stacked(q_ref, cos, sgn, da, o_ref, lse_ref_, h):
            cols = [slice(128 * j, 128 * j + 128) for j in range(pairs * h, pairs * (h + 1))]
            q4 = _stack_heads([_rope(q_ref[:, c], cos, sgn, 1.0) for c in cols])
            do4 = _stack_heads([da[:, c] for c in cols])
            lo = _lane((WINDOW, 128)) < HEAD_DIM
            deltas = []
            for c in cols:
                prod = o_ref[:, c] * da[:, c]
                deltas += [jnp.sum(jnp.where(lo, prod, 0.0), axis=1, keepdims=True),
                           jnp.sum(jnp.where(lo, 0.0, prod), axis=1, keepdims=True)]
            lse4 = jnp.concatenate([lse_ref_[:, group * h + r:group * h + r + 1] for r in range(group)], axis=0)
            return q4, do4, lse4, jnp.concatenate(deltas, axis=0)

        def scores_bwd(q4, do4, lse4, delta4, keys, vals, valid):
            sc = _dot(q4, keys, "nt") * ATTN_SCALE
            p = jnp.exp(jnp.where(valid, sc - lse4, MASKED))
            return p.astype(_CDT), (p * (_dot(do4, vals, "nt") - delta4) * ATTN_SCALE).astype(_CDT)

        for h in range(N_KV_HEADS):
            k2, v2 = _dup_head(kk, h).astype(_CDT), _dup_head(vv, h).astype(_CDT)
            q4, do4, lse4, delta4 = stacked(pc_ref, cos_c, sgn_c, da_c, ac_ref, lsec_ref, h)
            p, ds = scores_bwd(q4, do4, lse4, delta4, k2, v2, valid_c)
            for i, dq in enumerate(_unstack_heads(_dot(ds, k2, "nn"), pairs)):
                j = pairs * h + i
                dproj_ref[:, 128 * j:128 * j + 128] = _rope(dq, cos_c, sgn_c, -1.0).astype(dproj_ref.dtype)
            dk = _dot(ds, q4, "tn")
            dv = _dot(p, do4, "tn")
            dk_prev.append(dk_carry[h] + dk[0:WINDOW])
            dv_prev.append(dv_carry[h] + dv[0:WINDOW])
            dk_carry[h] = dk[WINDOW:2 * WINDOW]
            dv_carry[h] = dv[WINDOW:2 * WINDOW]
            sink4 = _per_head([sinks_ref[0, group * h + r] for r in range(group)])
            loss_sink = jnp.exp(sink4 - lse4) * delta4
            for r in range(group):
                dsinks = dsinks + jnp.where(head_lane == group * h + r,
                                            -jnp.sum(loss_sink[WINDOW * r:WINDOW * (r + 1)]), 0.0)
        _accumulate(dsinks_ref, dsinks, first)
        write_kv(dk_prev, dv_prev, cos_p, sgn_p)

        bg = pc_ref[:, dm.o_bg:dm.o_bg + cw]
        yc = yc_ref[...]
        dconv, dgc = _rms_bwd(dmc_ref[:, aw:d], bg * yc, gc_ref[...])
        _accumulate(dgc_ref, dgc, first)
        dproj_ref[:, dm.o_bg:dm.o_bg + cw] = (dconv * yc).astype(dproj_ref.dtype)
        dy = dconv * bg
        bg_n = pn_ref[:, dm.o_bg:dm.o_bg + cw]
        dconv_n, _ = _rms_bwd(dmn_ref[:, aw:d], bg_n * yn_ref[...], gc_ref[...])
        halo = jnp.where(has_next, dconv_n * bg_n, 0.0)
        dy1 = _shift_up(dy, halo, 1)
        dy2 = _shift_up(dy, halo, 2)
        dz = cw_ref[2:3, :] * dy + cw_ref[1:2, :] * dy1 + cw_ref[0:1, :] * dy2
        cg = pc_ref[:, dm.o_cg:dm.o_cg + cw]
        u = pc_ref[:, dm.o_u:dm.o_u + cw]
        dproj_ref[:, dm.o_cg:dm.o_cg + cw] = (dz * u).astype(dproj_ref.dtype)
        dproj_ref[:, dm.o_u:dm.o_u + cw] = (dz * cg).astype(dproj_ref.dtype)
        z = cg * u
        dcw = jnp.concatenate(
            [jnp.sum(z * t, axis=0, keepdims=True) for t in (dy2, dy1, dy)]
            + [jnp.zeros((V7X_SUBLANES - 3, cw), _F32)], axis=0)
        _accumulate(dcw_ref, dcw, first)

    at = lambda n: jnp.minimum(n, nb - 1)
    prev = lambda n: (jnp.maximum(at(n) - 1, 0), 0)
    cur = lambda n: (at(n), 0)
    done = lambda n: (jnp.maximum(n - 1, 0), 0)
    nxt8 = lambda n: (jnp.minimum((at(n) + 1) * (WINDOW // V7X_SUBLANES), s // V7X_SUBLANES - 1), 0)
    fixed = lambda n: (0, 0)
    blocks = WINDOW * (2 * inw * 4 + d * 4 + aw * 4 + cw * 4 + inw * 2 + 2 * KV_WIDTH * 2)
    carry = [pltpu.VMEM((N_KV_HEADS, WINDOW, 128), _F32), pltpu.VMEM((N_KV_HEADS, WINDOW, 128), _F32)]
    n_in, n_out = 15, 6
    ca = _CommArgs(list(comm), n_in, n_out)
    return pl.pallas_call(
        _carrying(body, n_in, n_out, nb + 1, ca, n_scratch=len(carry)), name="mixer_bwd", grid=(nb + 1,),
        in_specs=[pl.BlockSpec((WINDOW, inw), prev), pl.BlockSpec((WINDOW, inw), cur), pl.BlockSpec((V7X_SUBLANES, inw), nxt8),
                  pl.BlockSpec((WINDOW, 2 * V7X_LANES), prev), pl.BlockSpec((WINDOW, 2 * V7X_LANES), cur),
                  pl.BlockSpec((WINDOW, d), cur), pl.BlockSpec((V7X_SUBLANES, d), nxt8),
                  pl.BlockSpec((WINDOW, aw), cur), pl.BlockSpec((WINDOW, nq), cur),
                  pl.BlockSpec((WINDOW, cw), cur), pl.BlockSpec((V7X_SUBLANES, cw), nxt8),
                  pl.BlockSpec(memory_space=pltpu.SMEM),
                  pl.BlockSpec((1, aw), fixed), pl.BlockSpec((1, cw), fixed), pl.BlockSpec((V7X_SUBLANES, cw), fixed)]
        + [_ANY] * len(ca.operands),
        out_specs=[pl.BlockSpec((WINDOW, inw), cur), pl.BlockSpec((WINDOW, 2 * KV_WIDTH), done),
                   pl.BlockSpec((1, aw), fixed), pl.BlockSpec((1, cw), fixed),
                   pl.BlockSpec((1, nq), fixed), pl.BlockSpec((V7X_SUBLANES, cw), fixed)] + [_ANY] * len(ca.out_shape),
        out_shape=[jax.ShapeDtypeStruct((s, inw), _CDT), jax.ShapeDtypeStruct((s, 2 * KV_WIDTH), _CDT),
                   jax.ShapeDtypeStruct((1, aw), _F32), jax.ShapeDtypeStruct((1, cw), _F32),
                   jax.ShapeDtypeStruct((1, nq), _F32), jax.ShapeDtypeStruct((V7X_SUBLANES, cw), _F32)] + ca.out_shape,
        scratch_shapes=carry + ca.sems, input_output_aliases=ca.aliases,
        compiler_params=pltpu.CompilerParams(dimension_semantics=("arbitrary",), vmem_limit_bytes=_vmem_limit(blocks)),
    )(proj, proj, proj, rope, rope, dmixed, dmixed, attn, lse, y, y, sinks, g_attn, g_conv, conv_w8, *ca.operands)


def _position():
    return lax.axis_index("x"), lax.axis_index("y"), lax.axis_index("c")


def _linear(px, py, pc):
    return 4 * px + 2 * py + pc


def _comm_kernel(name, comm):
    ca = _CommArgs(list(comm), 0, 0)
    n_cin, n_cout = len(ca.operands), len(ca.out_shape)

    def body(*refs):
        cin, cout, sems = refs[:n_cin], refs[n_cin:n_cin + n_cout], refs[n_cin + n_cout:]
        ca.start(cin, cout, sems)
        ca.middle(cin, cout, sems)
        ca.finish(cin, cout, sems)

    return pl.pallas_call(
        body, name=name, out_shape=ca.out_shape, in_specs=[_ANY] * n_cin, out_specs=[_ANY] * n_cout,
        scratch_shapes=ca.sems, input_output_aliases=ca.aliases,
    )(*ca.operands)


def _gather_op(units):
    n = len(units)
    inputs, outputs, aliases = [], [], {}
    for shard, _, _, _ in units:
        inputs.append(shard)
        outputs.append(jax.ShapeDtypeStruct((N_DEV * shard.shape[0], shard.shape[1]), shard.dtype))
    for u, (_, buf, _, _) in enumerate(units):
        if buf is not None:
            aliases[len(inputs)] = u
            inputs.append(buf)

    def plan(ins, outs, sems, north):
        send_sems, recv_sems, local_sems = sems
        x, y, c = _position()
        me, sibling = (x, y, c), (x, y, 1 - c)
        xn, yn, dg = (1 - x, y), (x, 1 - y), (1 - x, 1 - y)
        via, to, k_via, k_other = (yn, xn, 2, 1) if north else (xn, yn, 1, 2)

        def rows(u, px, py, pc):
            shard, _, r0, r1 = units[u]
            return outs[u].at[pl.ds(pl.multiple_of(_linear(px, py, pc) * shard.shape[0] + r0, 16), r1 - r0), :]

        def own(u):
            _, _, r0, r1 = units[u]
            return ins[u].at[pl.ds(r0, r1 - r0), :]

        def copy(u, k, block, to_, src=None):
            return pltpu.make_async_remote_copy(
                src_ref=rows(u, *block) if src is None else src, dst_ref=rows(u, *block),
                send_sem=send_sems.at[u, k], recv_sem=recv_sems.at[u, k], device_id=to_, device_id_type=_MESH)

        us = range(n)
        return dict(
            mine=[pltpu.make_async_copy(own(u), rows(u, *me), local_sems.at[u]) for u in us],
            first=[cp for u in us for cp in (copy(u, 0, me, sibling, src=own(u)), copy(u, 1, me, (*xn, c), src=own(u)),
                                             copy(u, 2, me, (*yn, c), src=own(u)))],
            relay=[copy(u, 3, (*via, c), (*to, c)) for u in us],
            arrived={1: [copy(u, 1, (*xn, c), me) for u in us], 2: [copy(u, 2, (*yn, c), me) for u in us],
                     3: [copy(u, 3, (*dg, c), me) for u in us]},
            passed={1: [copy(u, 4, (*xn, c), sibling) for u in us], 2: [copy(u, 5, (*yn, c), sibling) for u in us],
                    3: [copy(u, 6, (*dg, c), sibling) for u in us]},
            rest=[cp for u in us for cp in (copy(u, 0, sibling, me), copy(u, 4, (*xn, 1 - c), me),
                                            copy(u, 5, (*yn, 1 - c), me), copy(u, 6, (*dg, 1 - c), me))],
            k_via=k_via, k_other=k_other)

    def land(p, k):
        for arrived, onward in zip(p["arrived"][k], p["passed"][k]):
            arrived.wait_recv()
            onward.start()

    def by_core(fn):
        c = lax.axis_index("c")
        for north in (True, False):
            pl.when(c == (1 if north else 0))(functools.partial(fn, north))

    def start(ins, outs, sems):
        p = plan(ins, outs, sems, True)
        for cp in p["mine"] + p["first"]:
            cp.start()

    def middle(ins, outs, sems):
        def go(north):
            p = plan(ins, outs, sems, north)
            land(p, p["k_via"])
            for cp in p["relay"]:
                cp.start()
            land(p, p["k_other"])
        by_core(go)

    def finish(ins, outs, sems):
        def go(north):
            p = plan(ins, outs, sems, north)
            land(p, 3)
            for cp in p["rest"]:
                cp.wait_recv()
            for cp in p["first"] + p["relay"] + [cp for k in (1, 2, 3) for cp in p["passed"][k]]:
                cp.wait_send()
            for cp in p["mine"]:
                cp.wait()
        by_core(go)

    sems = [pltpu.SemaphoreType.DMA((n, 7)), pltpu.SemaphoreType.DMA((n, 7)), pltpu.SemaphoreType.DMA((n,))]
    return _Comm(inputs, outputs, aliases, sems, start, finish, middle)


def _peers(x, y, c):
    out = []
    for k in range(1, N_DEV):
        fx, fy, fc = (k >> 2) & 1, (k >> 1) & 1, k & 1
        out.append((1 - x if fx else x, 1 - y if fy else y, 1 - c if fc else c))
    return out


def _exchange_op(partials):
    n = len(partials)
    outputs = [jax.ShapeDtypeStruct((4, p.shape[0] // N_DEV, p.shape[1]), p.dtype) for p in partials]

    def plan(ins, outs, sems):
        send_sems, recv_sems = sems
        x, y, c = _position()
        out = []
        for a in range(n):
            r = outs[a].shape[1]
            for ch in range(4):
                out.append(pltpu.make_async_remote_copy(
                    src_ref=ins[a].at[pl.ds(pl.multiple_of((2 * ch + 1 - c) * r, 16), r), :], dst_ref=outs[a].at[ch],
                    send_sem=send_sems.at[a, ch], recv_sem=recv_sems.at[a, ch], device_id=(x, y, 1 - c),
                    device_id_type=_MESH))
        return out

    def start(ins, outs, sems):
        for cp in plan(ins, outs, sems):
            cp.start()

    def finish(ins, outs, sems):
        copies = plan(ins, outs, sems)
        for cp in copies:
            cp.wait_recv()
        for cp in copies:
            cp.wait_send()

    sems = [pltpu.SemaphoreType.DMA((n, 4)), pltpu.SemaphoreType.DMA((n, 4))]
    return _Comm(list(partials), outputs, {}, sems, start, finish)


def _chip_send_op(units):
    n = len(units)
    inputs, outputs, aliases = [], [], {}
    for q, _, _, _ in units:
        inputs.append(q)
        outputs.append(jax.ShapeDtypeStruct(q.shape, q.dtype))
    for u, (_, buf, _, _) in enumerate(units):
        if buf is not None:
            aliases[len(inputs)] = u
            inputs.append(buf)

    def plan(ins, outs, sems):
        send_sems, recv_sems, local_sems = sems
        x, y, c = _position()
        my_chip = 2 * x + y
        chips = [(1 - x, y), (x, 1 - y), (1 - x, 1 - y)]
        mine, sends, arrivals = [], [], []
        for u, (_, _, r0, r1) in enumerate(units):
            span = pl.ds(r0, r1 - r0)
            mine.append(pltpu.make_async_copy(ins[u].at[my_chip, span, :], outs[u].at[my_chip, span, :], local_sems.at[u]))
            for k, (px, py) in enumerate(chips):
                sends.append(pltpu.make_async_remote_copy(
                    src_ref=ins[u].at[2 * px + py, span, :], dst_ref=outs[u].at[my_chip, span, :],
                    send_sem=send_sems.at[u, k], recv_sem=recv_sems.at[u, k], device_id=(px, py, c), device_id_type=_MESH))
                arrivals.append(pltpu.make_async_remote_copy(
                    src_ref=ins[u].at[my_chip, span, :], dst_ref=outs[u].at[2 * px + py, span, :],
                    send_sem=send_sems.at[u, k], recv_sem=recv_sems.at[u, k], device_id=(px, py, c), device_id_type=_MESH))
        return mine, sends, arrivals

    def start(ins, outs, sems):
        mine, sends, _ = plan(ins, outs, sems)
        for cp in mine + sends:
            cp.start()

    def finish(ins, outs, sems):
        mine, sends, arrivals = plan(ins, outs, sems)
        for cp in arrivals:
            cp.wait_recv()
        for cp in sends:
            cp.wait_send()
        for cp in mine:
            cp.wait()

    sems = [pltpu.SemaphoreType.DMA((n, 3)), pltpu.SemaphoreType.DMA((n, 3)), pltpu.SemaphoreType.DMA((n,))]
    return _Comm(inputs, outputs, aliases, sems, start, finish)


def _pair_sum(name, partial, received):
    _, rows, cols = received.shape
    tr = _pick(rows, (352, 288, 256, 128, 64, 32, 16))
    p4 = partial.reshape(4, 2, rows, cols)
    kind = jnp.reshape(lax.axis_index("c"), (1,)).astype(jnp.int32)

    def body(kind_ref, p_ref, r_ref, o_ref):
        o_ref[0] = (p_ref[0, 0].astype(_F32) + r_ref[0].astype(_F32)).astype(o_ref.dtype)

    return pl.pallas_call(
        body, name=name,
        grid_spec=pltpu.PrefetchScalarGridSpec(
            num_scalar_prefetch=1, grid=(4, rows // tr),
            in_specs=[pl.BlockSpec((1, 1, tr, cols), lambda ch, i, kind_ref: (ch, kind_ref[0], i, 0)),
                      pl.BlockSpec((1, tr, cols), lambda ch, i, kind_ref: (ch, i, 0))],
            out_specs=pl.BlockSpec((1, tr, cols), lambda ch, i, kind_ref: (ch, i, 0))),
        out_shape=jax.ShapeDtypeStruct(received.shape, received.dtype),
        compiler_params=pltpu.CompilerParams(dimension_semantics=("arbitrary", "arbitrary")),
    )(kind, p4, received)


def _all_reduce_small(name, v):
    rows = v.shape[0]

    def body(v_ref, out_ref, land_ref, send_sems, recv_sems):
        x, y, c = _position()
        me = _linear(x, y, c)
        peers = _peers(x, y, c)
        land_ref[me] = v_ref[...]
        sends = [pltpu.make_async_remote_copy(
            src_ref=v_ref, dst_ref=land_ref.at[me], send_sem=send_sems.at[k], recv_sem=recv_sems.at[k],
            device_id=peer, device_id_type=_MESH) for k, peer in enumerate(peers)]
        for cp in sends:
            cp.start()
        for k, peer in enumerate(peers):
            pltpu.make_async_remote_copy(
                src_ref=v_ref, dst_ref=land_ref.at[_linear(*peer)], send_sem=send_sems.at[k], recv_sem=recv_sems.at[k],
                device_id=peer, device_id_type=_MESH).wait_recv()
        for cp in sends:
            cp.wait_send()
        total = land_ref[0]
        for s in range(1, N_DEV):
            total = total + land_ref[s]
        out_ref[...] = total

    return pl.pallas_call(
        body, name=name, out_shape=jax.ShapeDtypeStruct(v.shape, _F32),
        in_specs=[pl.BlockSpec(memory_space=pltpu.VMEM)], out_specs=pl.BlockSpec(memory_space=pltpu.VMEM),
        scratch_shapes=[pltpu.VMEM((N_DEV, rows, V7X_LANES), _F32), pltpu.SemaphoreType.DMA((7,)), pltpu.SemaphoreType.DMA((7,))],
    )(v)


def _adamw(name, w, slots, m, v):
    rows, cols = w.shape
    n_slots = slots.shape[0]
    tr = _pick(rows, (176, 144, 128, 64, 32, 16, 8))

    def body(w_ref, s_ref, m_ref, v_ref, g_ref, d_ref, nm_ref, nv_ref):
        g = s_ref[0].astype(_F32)
        for k in range(1, n_slots):
            g = g + s_ref[k].astype(_F32)
        nm = ADAM_B1 * m_ref[...] + (1.0 - ADAM_B1) * g
        nv = ADAM_B2 * v_ref[...] + (1.0 - ADAM_B2) * (g * g)
        m_hat = nm / (1.0 - ADAM_B1 ** ADAM_STEP)
        v_hat = nv / (1.0 - ADAM_B2 ** ADAM_STEP)
        g_ref[...] = g
        d_ref[...] = -ADAM_LR * (m_hat / (jnp.sqrt(v_hat) + ADAM_EPS) + ADAM_WD * w_ref[...])
        nm_ref[...] = nm
        nv_ref[...] = nv

    spec = pl.BlockSpec((tr, cols), lambda i: (i, 0))
    blocks = 7 * tr * cols * 4 + _nbytes((n_slots, tr, cols), slots.dtype)
    return pl.pallas_call(
        body, name=name, grid=(rows // tr,),
        in_specs=[spec, pl.BlockSpec((n_slots, tr, cols), lambda i: (0, i, 0)), spec, spec], out_specs=[spec] * 4,
        out_shape=[jax.ShapeDtypeStruct((rows, cols), _F32)] * 4,
        compiler_params=pltpu.CompilerParams(dimension_semantics=("arbitrary",), vmem_limit_bytes=_vmem_limit(blocks)),
    )(w, slots, m, v)


def _pad_rows(a, rows):
    return jnp.pad(a, ((0, rows - a.shape[0]), (0, 0)))


def _pack(parts):
    rows, spans, at = [], [], 0
    for p in parts:
        p = p.reshape(-1)
        r = -(-p.shape[0] // V7X_LANES)
        rows.append(jnp.pad(p, (0, r * V7X_LANES - p.shape[0])).reshape(r, V7X_LANES))
        spans.append((at, r, p.shape[0]))
        at += r
    packed = jnp.concatenate(rows, axis=0)
    return _pad_rows(packed, -(-at // V7X_SUBLANES) * V7X_SUBLANES), spans


def _unpack(packed, spans, shapes):
    return [packed[at:at + r].reshape(-1)[:size].reshape(shape) for (at, r, size), shape in zip(spans, shapes)]


def kernel(x, positions, w_in, conv_w, sinks, g_attn, g_conv, w_out, ln1_g, ln1_b, w_gate, w_up, w_down, ln2_g, ln2_b, loss_target, m_w_in, m_conv_w, m_sinks, m_g_attn, m_g_conv, m_w_out, m_ln1_g, m_ln1_b, m_w_gate, m_w_up, m_w_down, m_ln2_g, m_ln2_b, v_w_in, v_conv_w, v_sinks, v_g_attn, v_g_conv, v_w_out, v_ln1_g, v_ln1_b, v_w_gate, v_w_up, v_w_down, v_ln2_g, v_ln2_b):
    _, s, d = x.shape
    d_ff = N_DEV * w_gate.shape[2]
    dm = _Dims(s, d, d_ff)
    aw, cw, nq, inw = dm.aw, dm.cw, dm.nq, dm.inw
    x2 = x[0]
    pos = positions[0].reshape(s, 1)
    inv_freq = ROPE_THETA ** (-jnp.arange(0, ROT_DIM, 2, dtype=_F32) / ROT_DIM)
    invf = jnp.tile(inv_freq, V7X_LANES // (ROT_DIM // 2)).reshape(1, V7X_LANES)

    conv_cols = conv_w.shape[2]
    sh_in, sh_out = w_in[0].T.astype(_CDT), w_out[0].astype(_CDT)
    sh_gate, sh_up, sh_down = w_gate[0].T.astype(_CDT), w_up[0].T.astype(_CDT), w_down[0].astype(_CDT)
    r_in, r_out, r_ff = sh_in.shape[0], sh_out.shape[0], sh_gate.shape[0]
    q_ff = r_ff // 4
    assert q_ff % 16 == 0
    def prepare_body(x_ref, pos_ref, invf_ref, xc_ref, rope_ref):
        xc_ref[...] = x_ref[...].astype(_CDT)
        cos, sgn = _rope_tables(pos_ref[...], invf_ref[...])
        rope_ref[:, 0:V7X_LANES] = cos
        rope_ref[:, V7X_LANES:2 * V7X_LANES] = sgn

    x_c, rope, w_in_t, conv_all = _row_kernel(
        "prepare_gather_w_in", prepare_body, [x2, pos], [invf], [((s, d), _CDT), ((s, 2 * V7X_LANES), _F32)], [],
        comm=[_gather_op([(sh_in, None, 0, r_in), (_pad_rows(conv_w[0], 16), None, 0, 16)])])
    conv_full = conv_all.reshape(N_DEV, 16, conv_cols)[:, :3, :].transpose(1, 0, 2).reshape(3, cw)
    conv_w8 = _pad_rows(conv_full, V7X_SUBLANES)

    tm = _pick(s, (1024, 512, 256, 128))
    tm2 = _pick(s, (2048, 1024, 512, 256, 128))
    tr = _pick(s, (512, 256, 128))
    tn_in = _pick(inw, (512, 256, 128))
    tn_ff = _pick(d_ff, (512, 256, 128))

    proj, w_out_f, w_gate_t = _matmul(
        "proj", [[(x_c, w_in_t, "nt")]], s, inw, d, tm2, tn_in, d, [],
        [((s, inw), _F32, (tm2, tn_in), _tile_ij)], _store_epilogue,
        comm=[_gather_op([(sh_out, None, 0, r_out), (sh_gate, None, 0, 2 * q_ff)])])
    mixed, attn, lse, y_conv, w_gate_t, w_up_t = _mixer_fwd(
        dm, proj, rope, sinks, g_attn, g_conv, conv_w8,
        comm=[_gather_op([(sh_gate, w_gate_t, 2 * q_ff, r_ff), (sh_up, None, 0, 2 * q_ff)])])

    def residual_epilogue(accs, ex, out, first):
        out[0][...] = DEEPNORM_ALPHA * ex[0][...] + accs[0]

    tn_d = _pick(d, (512,))
    r1, w_up_t = _matmul(
        "out_proj", [[(mixed, w_out_f, "nn")]], s, d, d, tm, tn_d, d, [(x2, (tm, tn_d), _tile_ij)],
        [((s, d), _F32, (tm, tn_d), _tile_ij)], residual_epilogue,
        comm=[_gather_op([(sh_up, w_up_t, 2 * q_ff, 3 * q_ff)])])
    h1, h1_c, w_up_t = _ln1_fwd_rows(r1, ln1_g, ln1_b, comm=[_gather_op([(sh_up, w_up_t, 3 * q_ff, r_ff)])])

    def swiglu_epilogue(accs, ex, out, first):
        gate_v, up_v = accs
        out[0][...] = gate_v
        out[1][...] = up_v
        out[2][...] = (gate_v * jax.nn.sigmoid(gate_v) * up_v).astype(_CDT)

    gate, up, act, w_down_f = _matmul(
        "gate_up", [[(h1_c, w_gate_t, "nt")], [(h1_c, w_up_t, "nt")]], s, d_ff, d, tm, tn_ff, d, [],
        [((s, d_ff), _F32, (tm, tn_ff), _tile_ij), ((s, d_ff), _F32, (tm, tn_ff), _tile_ij),
         ((s, d_ff), _CDT, (tm, tn_ff), _tile_ij)], swiglu_epilogue,
        comm=[_gather_op([(sh_down, None, 0, r_ff)])])

    (r2,) = _matmul("down", [[(act, w_down_f, "nn")]], s, d, d_ff, tm, tn_d, d_ff, [(h1, (tm, tn_d), _tile_ij)],
                    [((s, d), _F32, (tm, tn_d), _tile_ij)], residual_epilogue)
    dr2, dr2_c, loss_acc, d_ln2_g, d_ln2_b = _ln2_loss_bwd(r2, loss_target[0], ln2_g, ln2_b)

    def swiglu_bwd_epilogue(accs, ex, out, first):
        gate_v, up_v = ex[0][...], ex[1][...]
        sig = jax.nn.sigmoid(gate_v)
        out[0][...] = (accs[0] * up_v * (sig * (1.0 + gate_v * (1.0 - sig)))).astype(_CDT)
        out[1][...] = (accs[0] * (gate_v * sig)).astype(_CDT)

    dgate, dup = _matmul(
        "dact", [[(dr2_c, w_down_f, "nt")]], s, d_ff, d, tm2, tn_ff, d,
        [(gate, (tm2, tn_ff), _tile_ij), (up, (tm2, tn_ff), _tile_ij)],
        [((s, d_ff), _CDT, (tm2, tn_ff), _tile_ij), ((s, d_ff), _CDT, (tm2, tn_ff), _tile_ij)], swiglu_bwd_epilogue)
    def weight_grad(name, a, b, comm=()):
        rows = a.shape[1]
        tw, tn_w = _pick(rows, (512, 256, 128)), _pick(d, (1024, 512))
        return _matmul(name, [[(a, b, "tn")]], rows, d, s, tw, tn_w, s, [],
                       [((rows, d), _CDT, (tw, tn_w), _tile_ij)], _store_epilogue, comm=comm, j_outer=True)

    (dw_down,) = weight_grad("dw_down", act, dr2_c)
    dw_gate_t, x_down = weight_grad("dw_gate", dgate, h1_c, comm=[_exchange_op([dw_down])])
    q_down = _pair_sum("chip_sum_w_down", dw_down, x_down)
    dw_up_t, l_down, x_gate = weight_grad(
        "dw_up", dup, h1_c, comm=[_chip_send_op([(q_down, None, 0, 2 * q_ff)]), _exchange_op([dw_gate_t])])
    q_gate = _pair_sum("chip_sum_w_gate", dw_gate_t, x_gate)

    tn_h = _pick(d, (512,))
    dh1, l_down, l_gate, x_up = _matmul(
        "dh1", [[(dgate, w_gate_t, "nn"), (dup, w_up_t, "nn")]], s, d, d_ff, tr, tn_h, d_ff,
        [(dr2, (tr, tn_h), _tile_ij)], [((s, d), _F32, (tr, tn_h), _tile_ij)], residual_epilogue,
        comm=[_chip_send_op([(q_down, l_down, 2 * q_ff, r_ff), (q_gate, None, 0, r_ff)]), _exchange_op([dw_up_t])])
    q_up = _pair_sum("chip_sum_w_up", dw_up_t, x_up)
    dr1, dr1_c, d_ln1_g, d_ln1_b = _ln1_bwd_rows(dh1, r1, ln1_g)
    (dmixed,) = _matmul("dmixed", [[(dr1_c, w_out_f, "nt")]], s, d, d, tm2, tn_d, d, [],
                        [((s, d), _F32, (tm2, tn_d), _tile_ij)], _store_epilogue)
    (dw_out,) = weight_grad("dw_out", mixed, dr1_c)
    dproj, dkv, d_g_attn, d_g_conv, d_sinks, d_conv8, l_up, x_out = _mixer_bwd(
        dm, proj, rope, sinks, g_attn, g_conv, conv_w8, dmixed, attn, lse, y_conv,
        comm=[_chip_send_op([(q_up, None, 0, r_ff)]), _exchange_op([dw_out])])
    dproj = _patch_columns("dproj_kv", dproj, dkv, dm.o_k)
    q_out = _pair_sum("chip_sum_w_out", dw_out, x_out)
    dw_in_t, l_out = weight_grad("dw_in", dproj, x_c, comm=[_chip_send_op([(q_out, None, 0, r_out)])])
    (x_in,) = _comm_kernel("exchange_w_in", [_exchange_op([dw_in_t])])
    q_in = _pair_sum("chip_sum_w_in", dw_in_t, x_in)

    grad_x, l_in = _matmul("dx", [[(dproj, w_in_t, "nn")]], s, d, inw, tm, tn_d, inw,
                           [(dr1, (tm, tn_d), _tile_ij)], [((s, d), _F32, (tm, tn_d), _tile_ij)], residual_epilogue,
                           comm=[_chip_send_op([(q_in, None, 0, r_in)])])

    small_parts = [d_conv8[:3], d_sinks, d_g_attn, d_g_conv, d_ln1_g, d_ln1_b, d_ln2_g, d_ln2_b, loss_acc[0:1, 0:1]]
    packed, spans = _pack(small_parts)
    reduced = _unpack(_all_reduce_small("reduce_small", packed), spans, [p.shape for p in small_parts])
    g_conv_full, g_sinks, g_g_attn, g_g_conv, g_ln1_g, g_ln1_b, g_ln2_g, g_ln2_b, loss_sum = reduced
    me = _linear(*_position())
    g_conv_w = lax.dynamic_slice(g_conv_full, (0, me * conv_cols), (3, conv_cols))
    loss = loss_sum[0, 0]

    big = {"w_in": (w_in[0].T, l_in, m_w_in[0].T, v_w_in[0].T), "w_out": (w_out[0], l_out, m_w_out[0], v_w_out[0]),
           "w_gate": (w_gate[0].T, l_gate, m_w_gate[0].T, v_w_gate[0].T),
           "w_up": (w_up[0].T, l_up, m_w_up[0].T, v_w_up[0].T), "w_down": (w_down[0], l_down, m_w_down[0], v_w_down[0])}
    res = {nm: tuple(_adamw(f"adamw_{nm}", w, slots, m, v)) for nm, (w, slots, m, v) in big.items()}
    for nm in ("w_in", "w_gate", "w_up"):
        res[nm] = tuple(a.T for a in res[nm])
    small_names = ["conv_w", "sinks", "g_attn", "g_conv", "ln1_g", "ln1_b", "ln2_g", "ln2_b"]
    small_w = [conv_w, sinks, g_attn, g_conv, ln1_g, ln1_b, ln2_g, ln2_b]
    small_g = [g_conv_w[None], g_sinks, g_g_attn, g_g_conv, g_ln1_g, g_ln1_b, g_ln2_g, g_ln2_b]
    small_m = [m_conv_w, m_sinks, m_g_attn, m_g_conv, m_ln1_g, m_ln1_b, m_ln2_g, m_ln2_b]
    small_v = [v_conv_w, v_sinks, v_g_attn, v_g_conv, v_ln1_g, v_ln1_b, v_ln2_g, v_ln2_b]
    pw, sp = _pack(small_w)
    pg, _ = _pack(small_g)
    pm, _ = _pack(small_m)
    pv, _ = _pack(small_v)
    shapes = [w.shape for w in small_w]
    _, sd, sm, sv = [_unpack(p, sp, shapes) for p in _adamw("adamw_small", pw, pg[None], pm, pv)]
    for i, nm in enumerate(small_names):
        res[nm] = (small_g[i].reshape(shapes[i]), sd[i], sm[i], sv[i])

    order = ["w_in", "conv_w", "sinks", "g_attn", "g_conv", "w_out", "ln1_g", "ln1_b", "w_gate", "w_up", "w_down", "ln2_g", "ln2_b"]

    def lead(a, nm):
        return a[None] if nm in big else a

    return (loss, grad_x[None],
            *[lead(res[nm][0], nm) for nm in order], *[lead(res[nm][1], nm) for nm in order],
            *[lead(res[nm][2], nm) for nm in order], *[lead(res[nm][3], nm) for nm in order])
```

```python
import functools

import jax
import jax.numpy as jnp
from jax import lax
from jax.experimental import pallas as pl
from jax.experimental.pallas import tpu as pltpu

_F32 = jnp.float32
_CDT = jnp.bfloat16

HEAD_DIM = 64
WINDOW = 128
N_KV_HEADS = 4
KV_WIDTH = N_KV_HEADS * HEAD_DIM
ROT_DIM = HEAD_DIM // 4
ROPE_THETA = 500000.0
ATTN_SCALE = HEAD_DIM ** -0.5
DEPTH = 1
DEEPNORM_ALPHA = (2 * DEPTH) ** 0.25
LN_EPS = 1e-5
RMS_EPS = 1e-6
ADAM_LR = 0.001
ADAM_B1 = 0.9
ADAM_B2 = 0.999
ADAM_EPS = 1e-08
ADAM_WD = 0.01
ADAM_STEP = 10
N_DEV = 8
MASKED = -1e30

MIB = 1024 * 1024
V7X_VMEM_BYTES = 64 * MIB
V7X_LANES = 128
V7X_SUBLANES = 8
BODY_TEMPORARIES_BYTES = 16 * MIB
VMEM_LIMIT_FLOOR_BYTES = 32 * MIB
VMEM_LIMIT_CEILING_BYTES = V7X_VMEM_BYTES - 8 * MIB
_MESH = pl.DeviceIdType.MESH
_ANY = pl.BlockSpec(memory_space=pl.ANY)


def _vmem_limit(block_bytes, scratch_bytes=0):
    want = 2 * block_bytes + scratch_bytes + BODY_TEMPORARIES_BYTES
    return int(min(max(want, VMEM_LIMIT_FLOOR_BYTES), VMEM_LIMIT_CEILING_BYTES))


def _nbytes(shape, dtype):
    n = 1
    for s in shape:
        n *= s
    return n * jnp.dtype(dtype).itemsize


def _pick(n, candidates):
    for c in candidates:
        if n % c == 0:
            return c
    raise ValueError(f"no tile of {candidates} divides {n}")


_DOT_DIMS = {"nn": ((1,), (0,)), "nt": ((1,), (1,)), "tn": ((0,), (0,))}


def _dot(a, b, mode):
    return lax.dot_general(a.astype(_CDT), b.astype(_CDT), (_DOT_DIMS[mode], ((), ())),
                           preferred_element_type=_F32)


def _accumulate(ref, val, first):
    @pl.when(first)
    def _():
        ref[...] = val

    @pl.when(jnp.logical_not(first))
    def _():
        ref[...] += val


class _Comm:
    def __init__(self, inputs, outputs, aliases, sems, start, finish, middle=None):
        self.inputs, self.outputs, self.aliases, self.sems = inputs, outputs, aliases, sems
        self.start, self.finish, self.middle = start, finish, middle


def _middle_step(n_steps):
    return (2 * n_steps) // 3


class _CommArgs:
    def __init__(self, comms, n_in_before, n_out_before):
        self.comms, self.operands, self.out_shape, self.aliases, self.sems, self.at = comms, [], [], {}, [], []
        for cm in comms:
            self.at.append((len(self.operands), len(self.out_shape), len(self.sems)))
            for i_in, i_out in cm.aliases.items():
                self.aliases[n_in_before + len(self.operands) + i_in] = n_out_before + len(self.out_shape) + i_out
            self.operands += cm.inputs
            self.out_shape += cm.outputs
            self.sems += cm.sems

    def _each(self, in_refs, out_refs, sem_refs):
        for cm, (i0, o0, s0) in zip(self.comms, self.at):
            yield cm, (in_refs[i0:i0 + len(cm.inputs)], out_refs[o0:o0 + len(cm.outputs)], sem_refs[s0:s0 + len(cm.sems)])

    def start(self, in_refs, out_refs, sem_refs):
        for cm, refs in self._each(in_refs, out_refs, sem_refs):
            cm.start(*refs)

    def finish(self, in_refs, out_refs, sem_refs):
        for cm, refs in self._each(in_refs, out_refs, sem_refs):
            cm.finish(*refs)

    @property
    def has_middle(self):
        return any(cm.middle is not None for cm in self.comms)

    def middle(self, in_refs, out_refs, sem_refs):
        for cm, refs in self._each(in_refs, out_refs, sem_refs):
            if cm.middle is not None:
                cm.middle(*refs)


def _matmul(name, groups, m, n, k, tm, tn, tk, extras, outs, epilogue, comm=(), j_outer=False):
    assert m % tm == 0 and n % tn == 0 and k % tk == 0, (name, m, n, k, tm, tn, tk)
    nk = k // tk
    terms = [t for g in groups for t in g]
    operands, in_specs, block_bytes = [], [], 0

    def spec(blk, imap):
        return pl.BlockSpec(blk, (lambda g0, g1, kk: imap(g1, g0, kk)) if j_outer else imap)

    for a, b, mode in terms:
        assert a.shape == ((k, m) if mode == "tn" else (m, k)), (name, a.shape, mode)
        assert b.shape == ((n, k) if mode == "nt" else (k, n)), (name, b.shape, mode)
        if mode == "tn":
            a_blk, a_map = (tk, tm), (lambda i, j, kk: (kk, i))
        else:
            a_blk, a_map = (tm, tk), (lambda i, j, kk: (i, kk))
        if mode == "nt":
            b_blk, b_map = (tn, tk), (lambda i, j, kk: (j, kk))
        else:
            b_blk, b_map = (tk, tn), (lambda i, j, kk: (kk, j))
        operands += [a, b]
        in_specs += [spec(a_blk, a_map), spec(b_blk, b_map)]
        block_bytes += _nbytes(a_blk, a.dtype) + _nbytes(b_blk, b.dtype)
    for arr, blk, imap in extras:
        operands.append(arr)
        in_specs.append(spec(blk, lambda i, j, kk, imap=imap: imap(i, j)))
        block_bytes += _nbytes(blk, arr.dtype)
    out_shape, out_specs = [], []
    for shape, dtype, blk, imap in outs:
        out_shape.append(jax.ShapeDtypeStruct(shape, dtype))
        out_specs.append(spec(blk, lambda i, j, kk, imap=imap: imap(i, j)))
        block_bytes += _nbytes(blk, dtype)
    n_terms, n_extra, n_out, n_groups = len(terms), len(extras), len(outs), len(groups)
    scratch = [pltpu.VMEM((tm, tn), _F32) for _ in range(n_groups)] if nk > 1 else []
    ca = _CommArgs(list(comm), len(operands), n_out)
    n_cin, n_cout, n_acc = len(ca.operands), len(ca.out_shape), len(scratch)
    tiles = (m // tm, n // tn)
    grid = (tiles[1], tiles[0], nk) if j_outer else (tiles[0], tiles[1], nk)

    def body(*refs):
        refs = list(refs)
        term_refs = [refs.pop(0) for _ in range(2 * n_terms)]
        extra_refs = [refs.pop(0) for _ in range(n_extra)]
        cin_refs = [refs.pop(0) for _ in range(n_cin)]
        out_refs = [refs.pop(0) for _ in range(n_out)]
        cout_refs = [refs.pop(0) for _ in range(n_cout)]
        acc_refs = [refs.pop(0) for _ in range(n_acc)]
        sem_refs = refs
        g0, g1, kk = pl.program_id(0), pl.program_id(1), pl.program_id(2)
        first = jnp.logical_and(g0 == 0, g1 == 0)
        if comm:
            @pl.when(jnp.logical_and(first, kk == 0))
            def _():
                ca.start(cin_refs, cout_refs, sem_refs)
        if ca.has_middle:
            step = (g0 * grid[1] + g1) * nk + kk

            @pl.when(step == _middle_step(grid[0] * grid[1] * nk))
            def _():
                ca.middle(cin_refs, cout_refs, sem_refs)
        partial, t = [], 0
        for g in groups:
            s = None
            for _, _, mode in g:
                d = _dot(term_refs[2 * t][...], term_refs[2 * t + 1][...], mode)
                s = d if s is None else s + d
                t += 1
            partial.append(s)
        if nk == 1:
            epilogue(partial, extra_refs, out_refs, first)
        else:
            for acc, p in zip(acc_refs, partial):
                _accumulate(acc, p, kk == 0)

            @pl.when(kk == nk - 1)
            def _():
                epilogue([acc[...] for acc in acc_refs], extra_refs, out_refs, first)
        if comm:
            @pl.when(jnp.logical_and(jnp.logical_and(g0 == grid[0] - 1, g1 == grid[1] - 1), kk == nk - 1))
            def _():
                ca.finish(cin_refs, cout_refs, sem_refs)

    res = pl.pallas_call(
        body, name=name, grid=grid,
        in_specs=in_specs + [_ANY] * n_cin, out_specs=out_specs + [_ANY] * n_cout,
        out_shape=out_shape + ca.out_shape, scratch_shapes=scratch + ca.sems, input_output_aliases=ca.aliases,
        compiler_params=pltpu.CompilerParams(
            dimension_semantics=("arbitrary", "arbitrary", "arbitrary"),
            vmem_limit_bytes=_vmem_limit(block_bytes, n_groups * tm * tn * 4 if nk > 1 else 0)),
    )(*operands, *ca.operands)
    return list(res[:n_out]) + list(res[n_out:])


def _store_epilogue(accs, extra_refs, out_refs, first):
    for acc, ref in zip(accs, out_refs):
        ref[...] = acc.astype(ref.dtype)


def _tile_ij(i, j):
    return (i, j)


def _row_i(i, j):
    return (i, 0)


def _whole(i, j):
    return (0, 0)


def _mean(v):
    return jnp.mean(v, axis=-1, keepdims=True)


def _ln_fwd(r, g, b):
    xc = r - _mean(r)
    rstd = lax.rsqrt(_mean(xc * xc) + LN_EPS)
    xhat = xc * rstd
    return xhat * g + b, xhat, rstd


def _ln_bwd(dy, xhat, rstd, g):
    dxh = dy * g
    dr = rstd * (dxh - _mean(dxh) - xhat * _mean(dxh * xhat))
    return dr, jnp.sum(dy * xhat, axis=0, keepdims=True), jnp.sum(dy, axis=0, keepdims=True)


def _rms_fwd(a, g):
    rstd = lax.rsqrt(_mean(a * a) + RMS_EPS)
    return a * rstd * g


def _rms_bwd(dm, a, g):
    rstd = lax.rsqrt(_mean(a * a) + RMS_EPS)
    nhat = a * rstd
    dn = dm * g
    da = rstd * (dn - nhat * _mean(dn * nhat))
    return da, jnp.sum(dm * nhat, axis=0, keepdims=True)


def _lane(shape):
    return lax.broadcasted_iota(jnp.int32, shape, 1)


def _row(shape):
    return lax.broadcasted_iota(jnp.int32, shape, 0)


def _rope_tables(pos, invf):
    ang = pos.astype(_F32) * invf
    lane = _lane(ang.shape)
    in_rot = (lane % HEAD_DIM) < ROT_DIM
    first = (lane % ROT_DIM) < ROT_DIM // 2
    cos = jnp.where(in_rot, jnp.cos(ang), 1.0)
    sin = jnp.sin(ang)
    sgn = jnp.where(in_rot, jnp.where(first, -sin, sin), 0.0)
    return cos, sgn


def _rope(t, cos, sgn, sign):
    half = ROT_DIM // 2
    first = (_lane(t.shape) % ROT_DIM) < half
    partner = jnp.where(first, pltpu.roll(t, V7X_LANES - half, 1), pltpu.roll(t, half, 1))
    return t * cos + partner * (sgn * sign)


def _dup_head(t, h):
    g = t[:, 128 * (h // 2):128 * (h // 2) + 128]
    r = pltpu.roll(g, HEAD_DIM, 1)
    lo = _lane(g.shape) < HEAD_DIM
    return jnp.where(lo, g, r) if h % 2 == 0 else jnp.where(lo, r, g)


def _fold_halves(t):
    return t + pltpu.roll(t, HEAD_DIM, 1)


def _halves(t):
    lo = _lane(t.shape) < HEAD_DIM
    zero = jnp.zeros_like(t)
    return jnp.where(lo, t, zero), jnp.where(lo, zero, t)


def _band_mask(n_heads, n_keys, first_block):
    shape = (n_heads * WINDOW, n_keys)
    i = jnp.bitwise_and(_row(shape), WINDOW - 1)
    j = _lane(shape)
    valid = jnp.logical_and(j >= i + 1, j <= i + WINDOW)
    if first_block is not None:
        valid = jnp.logical_and(valid, jnp.logical_or(j >= WINDOW, jnp.logical_not(first_block)))
    return valid


def _stack_heads(pairs):
    return jnp.concatenate([half for t in pairs for half in _halves(t)], axis=0).astype(_CDT)


def _unstack_heads(t, n_pairs):
    lo = _lane((WINDOW, 128)) < HEAD_DIM
    return [jnp.where(lo, t[2 * WINDOW * i:2 * WINDOW * i + WINDOW], t[2 * WINDOW * i + WINDOW:2 * WINDOW * (i + 1)])
            for i in range(n_pairs)]


def _per_head(values):
    n_rows = len(values) * WINDOW
    block = jnp.right_shift(_row((n_rows, 1)), WINDOW.bit_length() - 1)
    out = jnp.zeros((n_rows, 1), _F32)
    for k, v in enumerate(values):
        out = jnp.where(block == k, v, out)
    return out


def _shift_down(z, halo, k):
    out = pltpu.roll(z, k, 0)
    r = _row(z.shape)
    for t in range(k):
        out = jnp.where(r == t, halo[V7X_SUBLANES - k + t:V7X_SUBLANES - k + t + 1, :], out)
    return out


def _shift_up(z, halo, k):
    rows = z.shape[0]
    out = pltpu.roll(z, rows - k, 0)
    r = _row(z.shape)
    for t in range(k):
        out = jnp.where(r == rows - k + t, halo[t:t + 1, :], out)
    return out


class _Dims:
    def __init__(self, s, d, d_ff):
        self.s, self.d, self.d_ff = s, d, d_ff
        self.aw = d // 2
        self.cw = d - self.aw
        self.nq = self.aw // HEAD_DIM
        self.group = self.nq // N_KV_HEADS
        assert self.group % 2 == 0, "a 128-lane pair of query heads must share its kv head"
        self.inw = self.aw + 2 * KV_WIDTH + 3 * self.cw
        self.o_k = self.aw
        self.o_v = self.aw + KV_WIDTH
        self.o_cg = self.aw + 2 * KV_WIDTH
        self.o_bg = self.o_cg + self.cw
        self.o_u = self.o_bg + self.cw
        self.nb = s // WINDOW
        assert s % WINDOW == 0


def _carrying(body, n_in, n_out, n_steps, ca, n_scratch=0):
    n_cin, n_cout = len(ca.operands), len(ca.out_shape)

    def wrapped(*refs):
        refs = list(refs)
        in_refs = [refs.pop(0) for _ in range(n_in)]
        cin_refs = [refs.pop(0) for _ in range(n_cin)]
        out_refs = [refs.pop(0) for _ in range(n_out)]
        cout_refs = [refs.pop(0) for _ in range(n_cout)]
        scratch_refs = [refs.pop(0) for _ in range(n_scratch)]
        if ca.comms:
            @pl.when(pl.program_id(0) == 0)
            def _():
                ca.start(cin_refs, cout_refs, refs)
        if ca.has_middle:
            @pl.when(pl.program_id(0) == _middle_step(n_steps))
            def _():
                ca.middle(cin_refs, cout_refs, refs)
        body(*in_refs, *out_refs, *scratch_refs)
        if ca.comms:
            @pl.when(pl.program_id(0) == n_steps - 1)
            def _():
                ca.finish(cin_refs, cout_refs, refs)

    return wrapped


def _row_kernel(name, body, rows_in, vecs_in, rows_out, vecs_out, comm=()):
    s = rows_in[0].shape[0]
    tr = _pick(s, (256, 128))
    row = lambda a: pl.BlockSpec((tr, a[1] if isinstance(a, tuple) else a.shape[1]), lambda i: (i, 0))
    vec = lambda shape: pl.BlockSpec(tuple(shape), lambda i: (0, 0))
    n_in, n_out = len(rows_in) + len(vecs_in), len(rows_out) + len(vecs_out)
    ca = _CommArgs(list(comm), n_in, n_out)
    blocks = sum(_nbytes((tr, a.shape[1]), a.dtype) for a in rows_in) + sum(_nbytes((tr, sh[1]), dt) for sh, dt in rows_out)
    res = pl.pallas_call(
        _carrying(body, n_in, n_out, s // tr, ca), name=name, grid=(s // tr,),
        in_specs=[row(a) for a in rows_in] + [vec(v.shape) for v in vecs_in] + [_ANY] * len(ca.operands),
        out_specs=[row(sh) for sh, _ in rows_out] + [vec(sh) for sh, _ in vecs_out] + [_ANY] * len(ca.out_shape),
        out_shape=[jax.ShapeDtypeStruct(sh, dt) for sh, dt in list(rows_out) + list(vecs_out)] + ca.out_shape,
        scratch_shapes=ca.sems, input_output_aliases=ca.aliases,
        compiler_params=pltpu.CompilerParams(dimension_semantics=("arbitrary",), vmem_limit_bytes=_vmem_limit(blocks)),
    )(*rows_in, *vecs_in, *ca.operands)
    return list(res)


def _ln2_loss_bwd(r2, target, gain, bias, comm=()):
    s, d = r2.shape

    def body(r_ref, t_ref, g_ref, b_ref, dr_ref, drc_ref, loss_ref, dg_ref, db_ref):
        first = pl.program_id(0) == 0
        yv, xhat, rstd = _ln_fwd(r_ref[...], g_ref[...], b_ref[...])
        err = yv - t_ref[...]
        dr2, dg, db = _ln_bwd(err * (1.0 / d), xhat, rstd, g_ref[...])
        dr_ref[...] = dr2
        drc_ref[...] = dr2.astype(_CDT)
        _accumulate(loss_ref, jnp.zeros(loss_ref.shape, _F32) + 0.5 * jnp.sum(err * err) * (1.0 / d), first)
        _accumulate(dg_ref, dg, first)
        _accumulate(db_ref, db, first)

    return _row_kernel("ln2_loss_bwd", body, [r2, target], [gain, bias], [((s, d), _F32), ((s, d), _CDT)],
                       [((V7X_SUBLANES, V7X_LANES), _F32), ((1, d), _F32), ((1, d), _F32)], comm)


def _ln1_fwd_rows(r1, gain, bias, comm=()):
    s, d = r1.shape

    def body(r_ref, g_ref, b_ref, h_ref, hc_ref):
        h1, _, _ = _ln_fwd(r_ref[...], g_ref[...], b_ref[...])
        h_ref[...] = h1
        hc_ref[...] = h1.astype(_CDT)

    return _row_kernel("ln1", body, [r1], [gain, bias], [((s, d), _F32), ((s, d), _CDT)], [], comm)


def _ln1_bwd_rows(dh1, r1, gain, comm=()):
    s, d = dh1.shape

    def body(dh_ref, r_ref, g_ref, dr_ref, drc_ref, dg_ref, db_ref):
        first = pl.program_id(0) == 0
        _, xhat, rstd = _ln_fwd(r_ref[...], g_ref[...], 0.0)
        dr1, dg, db = _ln_bwd(dh_ref[...], xhat, rstd, g_ref[...])
        dr_ref[...] = dr1
        drc_ref[...] = dr1.astype(_CDT)
        _accumulate(dg_ref, dg, first)
        _accumulate(db_ref, db, first)

    return _row_kernel("ln1_bwd", body, [dh1, r1], [gain], [((s, d), _F32), ((s, d), _CDT)],
                       [((1, d), _F32), ((1, d), _F32)], comm)


def _mixer_fwd(dm, proj, rope, sinks, g_attn, g_conv, conv_w8, comm=()):
    s, d, aw, cw, nq, inw, nb = dm.s, dm.d, dm.aw, dm.cw, dm.nq, dm.inw, dm.nb

    def body(pp_ref, pc_ref, ropep_ref, ropec_ref, sinks_ref, ga_ref, gc_ref, cw_ref,
             mixed_ref, attn_ref, lse_ref, y_ref, qk_ref):
        n = pl.program_id(0)
        cos_c, sgn_c = ropec_ref[:, 0:V7X_LANES], ropec_ref[:, V7X_LANES:2 * V7X_LANES]
        cos_p, sgn_p = ropep_ref[:, 0:V7X_LANES], ropep_ref[:, V7X_LANES:2 * V7X_LANES]
        for g in range(KV_WIDTH // 128):
            qk_ref[:, aw + 128 * g:aw + 128 * g + 128] = _rope(
                pc_ref[:, dm.o_k + 128 * g:dm.o_k + 128 * g + 128], cos_c, sgn_c, 1.0).astype(qk_ref.dtype)
        for j in range(nq // 2):
            qk_ref[:, 128 * j:128 * j + 128] = _rope(pc_ref[:, 128 * j:128 * j + 128], cos_c, sgn_c, 1.0).astype(qk_ref.dtype)
        k_prev = jnp.concatenate([_rope(pp_ref[:, dm.o_k + 128 * g:dm.o_k + 128 * g + 128], cos_p, sgn_p, 1.0)
                                  for g in range(KV_WIDTH // 128)], axis=1)
        kk = jnp.concatenate([k_prev, qk_ref[:, aw:aw + KV_WIDTH].astype(_F32)], axis=0)
        vv = jnp.concatenate([pp_ref[:, dm.o_v:dm.o_v + KV_WIDTH], pc_ref[:, dm.o_v:dm.o_v + KV_WIDTH]], axis=0)
        group, pairs = dm.group, dm.group // 2
        valid = _band_mask(group, 2 * WINDOW, n == 0)
        for h in range(N_KV_HEADS):
            k2, v2 = _dup_head(kk, h).astype(_CDT), _dup_head(vv, h).astype(_CDT)
            q4 = _stack_heads([qk_ref[:, 128 * j:128 * j + 128] for j in range(pairs * h, pairs * (h + 1))])
            sc = jnp.where(valid, _dot(q4, k2, "nt") * ATTN_SCALE, MASKED)
            sink = _per_head([sinks_ref[0, group * h + r] for r in range(group)])
            mx = jnp.maximum(jnp.max(sc, axis=1, keepdims=True), sink)
            p = jnp.exp(sc - mx)
            den = jnp.sum(p, axis=1, keepdims=True) + jnp.exp(sink - mx)
            out = _unstack_heads(_dot(p / den, v2, "nn"), pairs)
            lse = mx + jnp.log(den)
            for r in range(group):
                lse_ref[:, group * h + r:group * h + r + 1] = lse[WINDOW * r:WINDOW * (r + 1)]
            for i in range(pairs):
                j = pairs * h + i
                attn_ref[:, 128 * j:128 * j + 128] = out[i]
        mixed_ref[:, 0:aw] = _rms_fwd(attn_ref[...], ga_ref[...]).astype(mixed_ref.dtype)

        z = pc_ref[:, dm.o_cg:dm.o_cg + cw] * pc_ref[:, dm.o_u:dm.o_u + cw]
        top = WINDOW - V7X_SUBLANES
        halo = pp_ref[top:WINDOW, dm.o_cg:dm.o_cg + cw] * pp_ref[top:WINDOW, dm.o_u:dm.o_u + cw]
        halo = jnp.where(n == 0, jnp.zeros_like(halo), halo)
        y = cw_ref[0:1, :] * _shift_down(z, halo, 2) + cw_ref[1:2, :] * _shift_down(z, halo, 1) + cw_ref[2:3, :] * z
        y_ref[...] = y
        conv = pc_ref[:, dm.o_bg:dm.o_bg + cw] * y
        mixed_ref[:, aw:d] = _rms_fwd(conv, gc_ref[...]).astype(mixed_ref.dtype)

    prev = lambda n: (jnp.maximum(n - 1, 0), 0)
    cur = lambda n: (n, 0)
    fixed = lambda n: (0, 0)
    blocks = 2 * WINDOW * inw * 4 + WINDOW * (d * 2 + aw * 4 + cw * 4 + nq * 4)
    ca = _CommArgs(list(comm), 8, 5)
    return pl.pallas_call(
        _carrying(body, 8, 5, nb, ca), name="mixer_fwd", grid=(nb,),
        in_specs=[pl.BlockSpec((WINDOW, inw), prev), pl.BlockSpec((WINDOW, inw), cur),
                  pl.BlockSpec((WINDOW, 2 * V7X_LANES), prev), pl.BlockSpec((WINDOW, 2 * V7X_LANES), cur),
                  pl.BlockSpec(memory_space=pltpu.SMEM),
                  pl.BlockSpec((1, aw), fixed), pl.BlockSpec((1, cw), fixed), pl.BlockSpec((V7X_SUBLANES, cw), fixed)]
        + [_ANY] * len(ca.operands),
        out_specs=[pl.BlockSpec((WINDOW, d), cur), pl.BlockSpec((WINDOW, aw), cur),
                   pl.BlockSpec((WINDOW, nq), cur), pl.BlockSpec((WINDOW, cw), cur),
                   pl.BlockSpec((WINDOW, aw + KV_WIDTH), cur)] + [_ANY] * len(ca.out_shape),
        out_shape=[jax.ShapeDtypeStruct((s, d), _CDT), jax.ShapeDtypeStruct((s, aw), _F32),
                   jax.ShapeDtypeStruct((s, nq), _F32), jax.ShapeDtypeStruct((s, cw), _F32),
                   jax.ShapeDtypeStruct((s, aw + KV_WIDTH), _CDT)] + ca.out_shape,
        scratch_shapes=ca.sems, input_output_aliases=ca.aliases,
        compiler_params=pltpu.CompilerParams(dimension_semantics=("arbitrary",), vmem_limit_bytes=_vmem_limit(blocks)),
    )(proj, proj, rope, rope, sinks, g_attn, g_conv, conv_w8, *ca.operands)


def _patch_columns(name, a, part, offset):
    s, pw = part.shape
    assert offset % pw == 0 and pw % V7X_LANES == 0
    tr = _pick(s, (512, 256, 128))

    def body(a_ref, p_ref, o_ref):
        del a_ref
        o_ref[...] = p_ref[...]

    return pl.pallas_call(
        body, name=name, grid=(s // tr,),
        in_specs=[_ANY, pl.BlockSpec((tr, pw), lambda i: (i, 0))],
        out_specs=pl.BlockSpec((tr, pw), lambda i: (i, offset // pw)),
        out_shape=jax.ShapeDtypeStruct(a.shape, a.dtype), input_output_aliases={0: 0},
        compiler_params=pltpu.CompilerParams(dimension_semantics=("arbitrary",)),
    )(a, part)


def _mixer_bwd(dm, proj, rope, sinks, g_attn, g_conv, conv_w8, dmixed, attn, lse, y, qk, comm=()):
    s, d, aw, cw, nq, inw, nb = dm.s, dm.d, dm.aw, dm.cw, dm.nq, dm.inw, dm.nb

    def body(pp_ref, pc_ref, pn_ref, ropep_ref, ropec_ref, dmc_ref, dmn_ref, ac_ref,
             lsec_ref, yc_ref, yn_ref, qkp_ref, qkc_ref, sinks_ref, ga_ref, gc_ref, cw_ref,
             dproj_ref, dkv_ref, dga_ref, dgc_ref, dsinks_ref, dcw_ref, dk_carry, dv_carry):
        n = pl.program_id(0)
        first = n == 0
        live = n < nb
        has_next = n < nb - 1
        cos_p, sgn_p = ropep_ref[:, 0:V7X_LANES], ropep_ref[:, V7X_LANES:2 * V7X_LANES]
        cos_c, sgn_c = ropec_ref[:, 0:V7X_LANES], ropec_ref[:, V7X_LANES:2 * V7X_LANES]

        @pl.when(first)
        def _():
            dk_carry[...] = jnp.zeros(dk_carry.shape, _F32)
            dv_carry[...] = jnp.zeros(dv_carry.shape, _F32)

        def write_kv(dk2, dv2, cos, sgn):
            lo = _lane((WINDOW, 128)) < HEAD_DIM
            for g in range(KV_WIDTH // 128):
                dk = jnp.where(lo, _fold_halves(dk2[2 * g]), _fold_halves(dk2[2 * g + 1]))
                dv = jnp.where(lo, _fold_halves(dv2[2 * g]), _fold_halves(dv2[2 * g + 1]))
                dkv_ref[:, 128 * g:128 * g + 128] = _rope(dk, cos, sgn, -1.0).astype(dkv_ref.dtype)
                dkv_ref[:, KV_WIDTH + 128 * g:KV_WIDTH + 128 * g + 128] = dv.astype(dkv_ref.dtype)

        @pl.when(jnp.logical_not(live))
        def _():
            write_kv([dk_carry[h] for h in range(N_KV_HEADS)], [dv_carry[h] for h in range(N_KV_HEADS)], cos_c, sgn_c)

        @pl.when(live)
        def _():
            block_step(pp_ref, pc_ref, pn_ref, dmc_ref, dmn_ref, ac_ref, lsec_ref, yc_ref, yn_ref, qkp_ref, qkc_ref,
                       sinks_ref, ga_ref, gc_ref, cw_ref, dproj_ref, dga_ref, dgc_ref, dsinks_ref, dcw_ref, dk_carry,
                       dv_carry, first, has_next, cos_p, sgn_p, cos_c, sgn_c, write_kv)

    def block_step(pp_ref, pc_ref, pn_ref, dmc_ref, dmn_ref, ac_ref, lsec_ref, yc_ref, yn_ref, qkp_ref, qkc_ref,
                   sinks_ref, ga_ref, gc_ref, cw_ref, dproj_ref, dga_ref, dgc_ref, dsinks_ref, dcw_ref, dk_carry,
                   dv_carry, first, has_next, cos_p, sgn_p, cos_c, sgn_c, write_kv):
        da_c, dga = _rms_bwd(dmc_ref[:, 0:aw], ac_ref[...], ga_ref[...])
        _accumulate(dga_ref, dga, first)
        kk = jnp.concatenate([qkp_ref[:, aw:aw + KV_WIDTH], qkc_ref[:, aw:aw + KV_WIDTH]], axis=0).astype(_F32)
        vv = jnp.concatenate([pp_ref[:, dm.o_v:dm.o_v + KV_WIDTH], pc_ref[:, dm.o_v:dm.o_v + KV_WIDTH]], axis=0)
        group, pairs = dm.group, dm.group // 2
        valid_c = _band_mask(group, 2 * WINDOW, first)
        dk_prev, dv_prev = [], []
        dsinks = jnp.zeros((1, nq), _F32)
        head_lane = _lane((1, nq))

        def stacked(q_ref, da, o_ref, lse_ref_, h):
            cols = [slice(128 * j, 128 * j + 128) for j in range(pairs * h, pairs * (h + 1))]
            q4 = _stack_heads([q_ref[:, c] for c in cols])
            do4 = _stack_heads([da[:, c] for c in cols])
            lo = _lane((WINDOW, 128)) < HEAD_DIM
            deltas = []
            for c in cols:
                prod = o_ref[:, c] * da[:, c]
                deltas += [jnp.sum(jnp.where(lo, prod, 0.0), axis=1, keepdims=True),
                           jnp.sum(jnp.where(lo, 0.0, prod), axis=1, keepdims=True)]
            lse4 = jnp.concatenate([lse_ref_[:, group * h + r:group * h + r + 1] for r in range(group)], axis=0)
            return q4, do4, lse4, jnp.concatenate(deltas, axis=0)

        def scores_bwd(q4, do4, lse4, delta4, keys, vals, valid):
            sc = _dot(q4, keys, "nt") * ATTN_SCALE
            p = jnp.exp(jnp.where(valid, sc - lse4, MASKED))
            return p.astype(_CDT), (p * (_dot(do4, vals, "nt") - delta4) * ATTN_SCALE).astype(_CDT)

        for h in range(N_KV_HEADS):
            k2, v2 = _dup_head(kk, h).astype(_CDT), _dup_head(vv, h).astype(_CDT)
            q4, do4, lse4, delta4 = stacked(qkc_ref, da_c, ac_ref, lsec_ref, h)
            p, ds = scores_bwd(q4, do4, lse4, delta4, k2, v2, valid_c)
            for i, dq in enumerate(_unstack_heads(_dot(ds, k2, "nn"), pairs)):
                j = pairs * h + i
                dproj_ref[:, 128 * j:128 * j + 128] = _rope(dq, cos_c, sgn_c, -1.0).astype(dproj_ref.dtype)
            dk = _dot(ds, q4, "tn")
            dv = _dot(p, do4, "tn")
            dk_prev.append(dk_carry[h] + dk[0:WINDOW])
            dv_prev.append(dv_carry[h] + dv[0:WINDOW])
            dk_carry[h] = dk[WINDOW:2 * WINDOW]
            dv_carry[h] = dv[WINDOW:2 * WINDOW]
            sink4 = _per_head([sinks_ref[0, group * h + r] for r in range(group)])
            loss_sink = jnp.exp(sink4 - lse4) * delta4
            for r in range(group):
                dsinks = dsinks + jnp.where(head_lane == group * h + r,
                                            -jnp.sum(loss_sink[WINDOW * r:WINDOW * (r + 1)]), 0.0)
        _accumulate(dsinks_ref, dsinks, first)
        write_kv(dk_prev, dv_prev, cos_p, sgn_p)

        bg = pc_ref[:, dm.o_bg:dm.o_bg + cw]
        yc = yc_ref[...]
        dconv, dgc = _rms_bwd(dmc_ref[:, aw:d], bg * yc, gc_ref[...])
        _accumulate(dgc_ref, dgc, first)
        dproj_ref[:, dm.o_bg:dm.o_bg + cw] = (dconv * yc).astype(dproj_ref.dtype)
        dy = dconv * bg
        bg_n = pn_ref[:, dm.o_bg:dm.o_bg + cw]
        dconv_n, _ = _rms_bwd(dmn_ref[:, aw:d], bg_n * yn_ref[...], gc_ref[...])
        halo = jnp.where(has_next, dconv_n * bg_n, 0.0)
        dy1 = _shift_up(dy, halo, 1)
        dy2 = _shift_up(dy, halo, 2)
        dz = cw_ref[2:3, :] * dy + cw_ref[1:2, :] * dy1 + cw_ref[0:1, :] * dy2
        cg = pc_ref[:, dm.o_cg:dm.o_cg + cw]
        u = pc_ref[:, dm.o_u:dm.o_u + cw]
        dproj_ref[:, dm.o_cg:dm.o_cg + cw] = (dz * u).astype(dproj_ref.dtype)
        dproj_ref[:, dm.o_u:dm.o_u + cw] = (dz * cg).astype(dproj_ref.dtype)
        z = cg * u
        dcw = jnp.concatenate(
            [jnp.sum(z * t, axis=0, keepdims=True) for t in (dy2, dy1, dy)]
            + [jnp.zeros((V7X_SUBLANES - 3, cw), _F32)], axis=0)
        _accumulate(dcw_ref, dcw, first)

    at = lambda n: jnp.minimum(n, nb - 1)
    prev = lambda n: (jnp.maximum(at(n) - 1, 0), 0)
    cur = lambda n: (at(n), 0)
    done = lambda n: (jnp.maximum(n - 1, 0), 0)
    nxt8 = lambda n: (jnp.minimum((at(n) + 1) * (WINDOW // V7X_SUBLANES), s // V7X_SUBLANES - 1), 0)
    fixed = lambda n: (0, 0)
    blocks = WINDOW * (2 * inw * 4 + d * 4 + aw * 4 + cw * 4 + inw * 2 + 2 * KV_WIDTH * 2)
    carry = [pltpu.VMEM((N_KV_HEADS, WINDOW, 128), _F32), pltpu.VMEM((N_KV_HEADS, WINDOW, 128), _F32)]
    n_in, n_out = 17, 6
    ca = _CommArgs(list(comm), n_in, n_out)
    return pl.pallas_call(
        _carrying(body, n_in, n_out, nb + 1, ca, n_scratch=len(carry)), name="mixer_bwd", grid=(nb + 1,),
        in_specs=[pl.BlockSpec((WINDOW, inw), prev), pl.BlockSpec((WINDOW, inw), cur), pl.BlockSpec((V7X_SUBLANES, inw), nxt8),
                  pl.BlockSpec((WINDOW, 2 * V7X_LANES), prev), pl.BlockSpec((WINDOW, 2 * V7X_LANES), cur),
                  pl.BlockSpec((WINDOW, d), cur), pl.BlockSpec((V7X_SUBLANES, d), nxt8),
                  pl.BlockSpec((WINDOW, aw), cur), pl.BlockSpec((WINDOW, nq), cur),
                  pl.BlockSpec((WINDOW, cw), cur), pl.BlockSpec((V7X_SUBLANES, cw), nxt8),
                  pl.BlockSpec((WINDOW, aw + KV_WIDTH), prev), pl.BlockSpec((WINDOW, aw + KV_WIDTH), cur),
                  pl.BlockSpec(memory_space=pltpu.SMEM),
                  pl.BlockSpec((1, aw), fixed), pl.BlockSpec((1, cw), fixed), pl.BlockSpec((V7X_SUBLANES, cw), fixed)]
        + [_ANY] * len(ca.operands),
        out_specs=[pl.BlockSpec((WINDOW, inw), cur), pl.BlockSpec((WINDOW, 2 * KV_WIDTH), done),
                   pl.BlockSpec((1, aw), fixed), pl.BlockSpec((1, cw), fixed),
                   pl.BlockSpec((1, nq), fixed), pl.BlockSpec((V7X_SUBLANES, cw), fixed)] + [_ANY] * len(ca.out_shape),
        out_shape=[jax.ShapeDtypeStruct((s, inw), _CDT), jax.ShapeDtypeStruct((s, 2 * KV_WIDTH), _CDT),
                   jax.ShapeDtypeStruct((1, aw), _F32), jax.ShapeDtypeStruct((1, cw), _F32),
                   jax.ShapeDtypeStruct((1, nq), _F32), jax.ShapeDtypeStruct((V7X_SUBLANES, cw), _F32)] + ca.out_shape,
        scratch_shapes=carry + ca.sems, input_output_aliases=ca.aliases,
        compiler_params=pltpu.CompilerParams(dimension_semantics=("arbitrary",), vmem_limit_bytes=_vmem_limit(blocks)),
    )(proj, proj, proj, rope, rope, dmixed, dmixed, attn, lse, y, y, qk, qk, sinks, g_attn, g_conv, conv_w8, *ca.operands)


def _position():
    return lax.axis_index("x"), lax.axis_index("y"), lax.axis_index("c")


def _linear(px, py, pc):
    return 4 * px + 2 * py + pc


def _comm_kernel(name, comm):
    ca = _CommArgs(list(comm), 0, 0)
    n_cin, n_cout = len(ca.operands), len(ca.out_shape)

    def body(*refs):
        cin, cout, sems = refs[:n_cin], refs[n_cin:n_cin + n_cout], refs[n_cin + n_cout:]
        ca.start(cin, cout, sems)
        ca.middle(cin, cout, sems)
        ca.finish(cin, cout, sems)

    return pl.pallas_call(
        body, name=name, out_shape=ca.out_shape, in_specs=[_ANY] * n_cin, out_specs=[_ANY] * n_cout,
        scratch_shapes=ca.sems, input_output_aliases=ca.aliases,
    )(*ca.operands)


def _gather_op(units):
    n = len(units)
    inputs, outputs, aliases = [], [], {}
    for shard, _, _, _ in units:
        inputs.append(shard)
        outputs.append(jax.ShapeDtypeStruct((N_DEV * shard.shape[0], shard.shape[1]), shard.dtype))
    for u, (_, buf, _, _) in enumerate(units):
        if buf is not None:
            aliases[len(inputs)] = u
            inputs.append(buf)

    def plan(ins, outs, sems, north):
        send_sems, recv_sems, local_sems = sems
        x, y, c = _position()
        me, sibling = (x, y, c), (x, y, 1 - c)
        xn, yn, dg = (1 - x, y), (x, 1 - y), (1 - x, 1 - y)
        via, to, k_via, k_other = (yn, xn, 2, 1) if north else (xn, yn, 1, 2)

        def rows(u, px, py, pc):
            shard, _, r0, r1 = units[u]
            return outs[u].at[pl.ds(pl.multiple_of(_linear(px, py, pc) * shard.shape[0] + r0, 16), r1 - r0), :]

        def own(u):
            _, _, r0, r1 = units[u]
            return ins[u].at[pl.ds(r0, r1 - r0), :]

        def copy(u, k, block, to_, src=None):
            return pltpu.make_async_remote_copy(
                src_ref=rows(u, *block) if src is None else src, dst_ref=rows(u, *block),
                send_sem=send_sems.at[u, k], recv_sem=recv_sems.at[u, k], device_id=to_, device_id_type=_MESH)

        us = range(n)
        return dict(
            mine=[pltpu.make_async_copy(own(u), rows(u, *me), local_sems.at[u]) for u in us],
            first=[cp for u in us for cp in (copy(u, 0, me, sibling, src=own(u)), copy(u, 1, me, (*xn, c), src=own(u)),
                                             copy(u, 2, me, (*yn, c), src=own(u)))],
            relay=[copy(u, 3, (*via, c), (*to, c)) for u in us],
            arrived={1: [copy(u, 1, (*xn, c), me) for u in us], 2: [copy(u, 2, (*yn, c), me) for u in us],
                     3: [copy(u, 3, (*dg, c), me) for u in us]},
            passed={1: [copy(u, 4, (*xn, c), sibling) for u in us], 2: [copy(u, 5, (*yn, c), sibling) for u in us],
                    3: [copy(u, 6, (*dg, c), sibling) for u in us]},
            rest=[cp for u in us for cp in (copy(u, 0, sibling, me), copy(u, 4, (*xn, 1 - c), me),
                                            copy(u, 5, (*yn, 1 - c), me), copy(u, 6, (*dg, 1 - c), me))],
            k_via=k_via, k_other=k_other)

    def land(p, k):
        for arrived, onward in zip(p["arrived"][k], p["passed"][k]):
            arrived.wait_recv()
            onward.start()

    def by_core(fn):
        c = lax.axis_index("c")
        for north in (True, False):
            pl.when(c == (1 if north else 0))(functools.partial(fn, north))

    def start(ins, outs, sems):
        p = plan(ins, outs, sems, True)
        for cp in p["mine"] + p["first"]:
            cp.start()

    def middle(ins, outs, sems):
        def go(north):
            p = plan(ins, outs, sems, north)
            land(p, p["k_via"])
            for cp in p["relay"]:
                cp.start()
            land(p, p["k_other"])
        by_core(go)

    def finish(ins, outs, sems):
        def go(north):
            p = plan(ins, outs, sems, north)
            land(p, 3)
            for cp in p["rest"]:
                cp.wait_recv()
            for cp in p["first"] + p["relay"] + [cp for k in (1, 2, 3) for cp in p["passed"][k]]:
                cp.wait_send()
            for cp in p["mine"]:
                cp.wait()
        by_core(go)

    sems = [pltpu.SemaphoreType.DMA((n, 7)), pltpu.SemaphoreType.DMA((n, 7)), pltpu.SemaphoreType.DMA((n,))]
    return _Comm(inputs, outputs, aliases, sems, start, finish, middle)


def _peers(x, y, c):
    out = []
    for k in range(1, N_DEV):
        fx, fy, fc = (k >> 2) & 1, (k >> 1) & 1, k & 1
        out.append((1 - x if fx else x, 1 - y if fy else y, 1 - c if fc else c))
    return out


def _exchange_op(partials):
    n = len(partials)
    outputs = [jax.ShapeDtypeStruct((4, p.shape[0] // N_DEV, p.shape[1]), p.dtype) for p in partials]

    def plan(ins, outs, sems):
        send_sems, recv_sems = sems
        x, y, c = _position()
        out = []
        for a in range(n):
            r = outs[a].shape[1]
            for ch in range(4):
                out.append(pltpu.make_async_remote_copy(
                    src_ref=ins[a].at[pl.ds(pl.multiple_of((2 * ch + 1 - c) * r, 16), r), :], dst_ref=outs[a].at[ch],
                    send_sem=send_sems.at[a, ch], recv_sem=recv_sems.at[a, ch], device_id=(x, y, 1 - c),
                    device_id_type=_MESH))
        return out

    def start(ins, outs, sems):
        for cp in plan(ins, outs, sems):
            cp.start()

    def finish(ins, outs, sems):
        copies = plan(ins, outs, sems)
        for cp in copies:
            cp.wait_recv()
        for cp in copies:
            cp.wait_send()

    sems = [pltpu.SemaphoreType.DMA((n, 4)), pltpu.SemaphoreType.DMA((n, 4))]
    return _Comm(list(partials), outputs, {}, sems, start, finish)


def _chip_send_op(units):
    n = len(units)
    inputs, outputs, aliases = [], [], {}
    for q, _, _, _ in units:
        inputs.append(q)
        outputs.append(jax.ShapeDtypeStruct(q.shape, q.dtype))
    for u, (_, buf, _, _) in enumerate(units):
        if buf is not None:
            aliases[len(inputs)] = u
            inputs.append(buf)

    def plan(ins, outs, sems):
        send_sems, recv_sems, local_sems = sems
        x, y, c = _position()
        my_chip = 2 * x + y
        chips = [(1 - x, y), (x, 1 - y), (1 - x, 1 - y)]
        mine, sends, arrivals = [], [], []
        for u, (_, _, r0, r1) in enumerate(units):
            span = pl.ds(r0, r1 - r0)
            mine.append(pltpu.make_async_copy(ins[u].at[my_chip, span, :], outs[u].at[my_chip, span, :], local_sems.at[u]))
            for k, (px, py) in enumerate(chips):
                sends.append(pltpu.make_async_remote_copy(
                    src_ref=ins[u].at[2 * px + py, span, :], dst_ref=outs[u].at[my_chip, span, :],
                    send_sem=send_sems.at[u, k], recv_sem=recv_sems.at[u, k], device_id=(px, py, c), device_id_type=_MESH))
                arrivals.append(pltpu.make_async_remote_copy(
                    src_ref=ins[u].at[my_chip, span, :], dst_ref=outs[u].at[2 * px + py, span, :],
                    send_sem=send_sems.at[u, k], recv_sem=recv_sems.at[u, k], device_id=(px, py, c), device_id_type=_MESH))
        return mine, sends, arrivals

    def start(ins, outs, sems):
        mine, sends, _ = plan(ins, outs, sems)
        for cp in mine + sends:
            cp.start()

    def finish(ins, outs, sems):
        mine, sends, arrivals = plan(ins, outs, sems)
        for cp in arrivals:
            cp.wait_recv()
        for cp in sends:
            cp.wait_send()
        for cp in mine:
            cp.wait()

    sems = [pltpu.SemaphoreType.DMA((n, 3)), pltpu.SemaphoreType.DMA((n, 3)), pltpu.SemaphoreType.DMA((n,))]
    return _Comm(inputs, outputs, aliases, sems, start, finish)


def _pair_sum(name, partial, received):
    _, rows, cols = received.shape
    tr = _pick(rows, (352, 288, 256, 128, 64, 32, 16))
    p4 = partial.reshape(4, 2, rows, cols)
    kind = jnp.reshape(lax.axis_index("c"), (1,)).astype(jnp.int32)

    def body(kind_ref, p_ref, r_ref, o_ref):
        o_ref[0] = (p_ref[0, 0].astype(_F32) + r_ref[0].astype(_F32)).astype(o_ref.dtype)

    return pl.pallas_call(
        body, name=name,
        grid_spec=pltpu.PrefetchScalarGridSpec(
            num_scalar_prefetch=1, grid=(4, rows // tr),
            in_specs=[pl.BlockSpec((1, 1, tr, cols), lambda ch, i, kind_ref: (ch, kind_ref[0], i, 0)),
                      pl.BlockSpec((1, tr, cols), lambda ch, i, kind_ref: (ch, i, 0))],
            out_specs=pl.BlockSpec((1, tr, cols), lambda ch, i, kind_ref: (ch, i, 0))),
        out_shape=jax.ShapeDtypeStruct(received.shape, received.dtype),
        compiler_params=pltpu.CompilerParams(dimension_semantics=("arbitrary", "arbitrary")),
    )(kind, p4, received)


def _all_reduce_small(name, v):
    rows = v.shape[0]

    def body(v_ref, out_ref, land_ref, send_sems, recv_sems):
        x, y, c = _position()
        me = _linear(x, y, c)
        peers = _peers(x, y, c)
        land_ref[me] = v_ref[...]
        sends = [pltpu.make_async_remote_copy(
            src_ref=v_ref, dst_ref=land_ref.at[me], send_sem=send_sems.at[k], recv_sem=recv_sems.at[k],
            device_id=peer, device_id_type=_MESH) for k, peer in enumerate(peers)]
        for cp in sends:
            cp.start()
        for k, peer in enumerate(peers):
            pltpu.make_async_remote_copy(
                src_ref=v_ref, dst_ref=land_ref.at[_linear(*peer)], send_sem=send_sems.at[k], recv_sem=recv_sems.at[k],
                device_id=peer, device_id_type=_MESH).wait_recv()
        for cp in sends:
            cp.wait_send()
        total = land_ref[0]
        for s in range(1, N_DEV):
            total = total + land_ref[s]
        out_ref[...] = total

    return pl.pallas_call(
        body, name=name, out_shape=jax.ShapeDtypeStruct(v.shape, _F32),
        in_specs=[pl.BlockSpec(memory_space=pltpu.VMEM)], out_specs=pl.BlockSpec(memory_space=pltpu.VMEM),
        scratch_shapes=[pltpu.VMEM((N_DEV, rows, V7X_LANES), _F32), pltpu.SemaphoreType.DMA((7,)), pltpu.SemaphoreType.DMA((7,))],
    )(v)


def _adamw(name, w, slots, m, v):
    rows, cols = w.shape
    n_slots = slots.shape[0]
    tr = _pick(rows, (176, 144, 128, 64, 32, 16, 8))

    def body(w_ref, s_ref, m_ref, v_ref, g_ref, d_ref, nm_ref, nv_ref):
        g = s_ref[0].astype(_F32)
        for k in range(1, n_slots):
            g = g + s_ref[k].astype(_F32)
        nm = ADAM_B1 * m_ref[...] + (1.0 - ADAM_B1) * g
        nv = ADAM_B2 * v_ref[...] + (1.0 - ADAM_B2) * (g * g)
        m_hat = nm / (1.0 - ADAM_B1 ** ADAM_STEP)
        v_hat = nv / (1.0 - ADAM_B2 ** ADAM_STEP)
        g_ref[...] = g
        d_ref[...] = -ADAM_LR * (m_hat / (jnp.sqrt(v_hat) + ADAM_EPS) + ADAM_WD * w_ref[...])
        nm_ref[...] = nm
        nv_ref[...] = nv

    spec = pl.BlockSpec((tr, cols), lambda i: (i, 0))
    blocks = 7 * tr * cols * 4 + _nbytes((n_slots, tr, cols), slots.dtype)
    return pl.pallas_call(
        body, name=name, grid=(rows // tr,),
        in_specs=[spec, pl.BlockSpec((n_slots, tr, cols), lambda i: (0, i, 0)), spec, spec], out_specs=[spec] * 4,
        out_shape=[jax.ShapeDtypeStruct((rows, cols), _F32)] * 4,
        compiler_params=pltpu.CompilerParams(dimension_semantics=("arbitrary",), vmem_limit_bytes=_vmem_limit(blocks)),
    )(w, slots, m, v)


def _pad_rows(a, rows):
    return jnp.pad(a, ((0, rows - a.shape[0]), (0, 0)))


def _pack(parts):
    rows, spans, at = [], [], 0
    for p in parts:
        p = p.reshape(-1)
        r = -(-p.shape[0] // V7X_LANES)
        rows.append(jnp.pad(p, (0, r * V7X_LANES - p.shape[0])).reshape(r, V7X_LANES))
        spans.append((at, r, p.shape[0]))
        at += r
    packed = jnp.concatenate(rows, axis=0)
    return _pad_rows(packed, -(-at // V7X_SUBLANES) * V7X_SUBLANES), spans


def _unpack(packed, spans, shapes):
    return [packed[at:at + r].reshape(-1)[:size].reshape(shape) for (at, r, size), shape in zip(spans, shapes)]


def kernel(x, positions, w_in, conv_w, sinks, g_attn, g_conv, w_out, ln1_g, ln1_b, w_gate, w_up, w_down, ln2_g, ln2_b, loss_target, m_w_in, m_conv_w, m_sinks, m_g_attn, m_g_conv, m_w_out, m_ln1_g, m_ln1_b, m_w_gate, m_w_up, m_w_down, m_ln2_g, m_ln2_b, v_w_in, v_conv_w, v_sinks, v_g_attn, v_g_conv, v_w_out, v_ln1_g, v_ln1_b, v_w_gate, v_w_up, v_w_down, v_ln2_g, v_ln2_b):
    _, s, d = x.shape
    d_ff = N_DEV * w_gate.shape[2]
    dm = _Dims(s, d, d_ff)
    aw, cw, nq, inw = dm.aw, dm.cw, dm.nq, dm.inw
    x2 = x[0]
    pos = positions[0].reshape(s, 1)
    inv_freq = ROPE_THETA ** (-jnp.arange(0, ROT_DIM, 2, dtype=_F32) / ROT_DIM)
    invf = jnp.tile(inv_freq, V7X_LANES // (ROT_DIM // 2)).reshape(1, V7X_LANES)

    conv_cols = conv_w.shape[2]
    sh_in, sh_out = w_in[0].T.astype(_CDT), w_out[0].astype(_CDT)
    sh_gate, sh_up, sh_down = w_gate[0].T.astype(_CDT), w_up[0].T.astype(_CDT), w_down[0].astype(_CDT)
    r_in, r_out, r_ff = sh_in.shape[0], sh_out.shape[0], sh_gate.shape[0]
    q_ff = r_ff // 4
    assert q_ff % 16 == 0
    def prepare_body(x_ref, pos_ref, invf_ref, xc_ref, rope_ref):
        xc_ref[...] = x_ref[...].astype(_CDT)
        cos, sgn = _rope_tables(pos_ref[...], invf_ref[...])
        rope_ref[:, 0:V7X_LANES] = cos
        rope_ref[:, V7X_LANES:2 * V7X_LANES] = sgn

    x_c, rope, w_in_t, conv_all = _row_kernel(
        "prepare_gather_w_in", prepare_body, [x2, pos], [invf], [((s, d), _CDT), ((s, 2 * V7X_LANES), _F32)], [],
        comm=[_gather_op([(sh_in, None, 0, r_in), (_pad_rows(conv_w[0], 16), None, 0, 16)])])
    conv_full = conv_all.reshape(N_DEV, 16, conv_cols)[:, :3, :].transpose(1, 0, 2).reshape(3, cw)
    conv_w8 = _pad_rows(conv_full, V7X_SUBLANES)

    tm = _pick(s, (1024, 512, 256, 128))
    tm2 = _pick(s, (2048, 1024, 512, 256, 128))
    tr = _pick(s, (512, 256, 128))
    tn_in = _pick(inw, (512, 256, 128))
    tn_ff = _pick(d_ff, (512, 256, 128))

    proj, w_out_f, w_gate_t = _matmul(
        "proj", [[(x_c, w_in_t, "nt")]], s, inw, d, tm2, tn_in, d, [],
        [((s, inw), _F32, (tm2, tn_in), _tile_ij)], _store_epilogue,
        comm=[_gather_op([(sh_out, None, 0, r_out), (sh_gate, None, 0, 2 * q_ff)])])
    mixed, attn, lse, y_conv, qk_rot, w_gate_t, w_up_t = _mixer_fwd(
        dm, proj, rope, sinks, g_attn, g_conv, conv_w8,
        comm=[_gather_op([(sh_gate, w_gate_t, 2 * q_ff, r_ff), (sh_up, None, 0, 2 * q_ff)])])

    def residual_epilogue(accs, ex, out, first):
        out[0][...] = DEEPNORM_ALPHA * ex[0][...] + accs[0]

    tn_d = _pick(d, (512,))
    r1, w_up_t = _matmul(
        "out_proj", [[(mixed, w_out_f, "nn")]], s, d, d, tm, tn_d, d, [(x2, (tm, tn_d), _tile_ij)],
        [((s, d), _F32, (tm, tn_d), _tile_ij)], residual_epilogue,
        comm=[_gather_op([(sh_up, w_up_t, 2 * q_ff, 3 * q_ff)])])
    h1, h1_c, w_up_t = _ln1_fwd_rows(r1, ln1_g, ln1_b, comm=[_gather_op([(sh_up, w_up_t, 3 * q_ff, r_ff)])])

    def swiglu_epilogue(accs, ex, out, first):
        gate_v, up_v = accs
        out[0][...] = gate_v
        out[1][...] = up_v
        out[2][...] = (gate_v * jax.nn.sigmoid(gate_v) * up_v).astype(_CDT)

    gate, up, act, w_down_f = _matmul(
        "gate_up", [[(h1_c, w_gate_t, "nt")], [(h1_c, w_up_t, "nt")]], s, d_ff, d, tm, tn_ff, d, [],
        [((s, d_ff), _F32, (tm, tn_ff), _tile_ij), ((s, d_ff), _F32, (tm, tn_ff), _tile_ij),
         ((s, d_ff), _CDT, (tm, tn_ff), _tile_ij)], swiglu_epilogue,
        comm=[_gather_op([(sh_down, None, 0, r_ff)])])

    (r2,) = _matmul("down", [[(act, w_down_f, "nn")]], s, d, d_ff, tm, tn_d, d_ff, [(h1, (tm, tn_d), _tile_ij)],
                    [((s, d), _F32, (tm, tn_d), _tile_ij)], residual_epilogue)
    dr2, dr2_c, loss_acc, d_ln2_g, d_ln2_b = _ln2_loss_bwd(r2, loss_target[0], ln2_g, ln2_b)

    def swiglu_bwd_epilogue(accs, ex, out, first):
        gate_v, up_v = ex[0][...], ex[1][...]
        sig = jax.nn.sigmoid(gate_v)
        out[0][...] = (accs[0] * up_v * (sig * (1.0 + gate_v * (1.0 - sig)))).astype(_CDT)
        out[1][...] = (accs[0] * (gate_v * sig)).astype(_CDT)

    dgate, dup = _matmul(
        "dact", [[(dr2_c, w_down_f, "nt")]], s, d_ff, d, tm2, tn_ff, d,
        [(gate, (tm2, tn_ff), _tile_ij), (up, (tm2, tn_ff), _tile_ij)],
        [((s, d_ff), _CDT, (tm2, tn_ff), _tile_ij), ((s, d_ff), _CDT, (tm2, tn_ff), _tile_ij)], swiglu_bwd_epilogue)
    def weight_grad(name, a, b, comm=()):
        rows = a.shape[1]
        tw, tn_w = _pick(rows, (512, 256, 128)), _pick(d, (1024, 512))
        return _matmul(name, [[(a, b, "tn")]], rows, d, s, tw, tn_w, s, [],
                       [((rows, d), _CDT, (tw, tn_w), _tile_ij)], _store_epilogue, comm=comm, j_outer=True)

    (dw_down,) = weight_grad("dw_down", act, dr2_c)
    dw_gate_t, x_down = weight_grad("dw_gate", dgate, h1_c, comm=[_exchange_op([dw_down])])
    q_down = _pair_sum("chip_sum_w_down", dw_down, x_down)
    dw_up_t, l_down, x_gate = weight_grad(
        "dw_up", dup, h1_c, comm=[_chip_send_op([(q_down, None, 0, 2 * q_ff)]), _exchange_op([dw_gate_t])])
    q_gate = _pair_sum("chip_sum_w_gate", dw_gate_t, x_gate)

    tn_h = _pick(d, (512,))
    dh1, l_down, l_gate, x_up = _matmul(
        "dh1", [[(dgate, w_gate_t, "nn"), (dup, w_up_t, "nn")]], s, d, d_ff, tr, tn_h, d_ff,
        [(dr2, (tr, tn_h), _tile_ij)], [((s, d), _F32, (tr, tn_h), _tile_ij)], residual_epilogue,
        comm=[_chip_send_op([(q_down, l_down, 2 * q_ff, r_ff), (q_gate, None, 0, r_ff)]), _exchange_op([dw_up_t])])
    q_up = _pair_sum("chip_sum_w_up", dw_up_t, x_up)
    dr1, dr1_c, d_ln1_g, d_ln1_b = _ln1_bwd_rows(dh1, r1, ln1_g)
    (dmixed,) = _matmul("dmixed", [[(dr1_c, w_out_f, "nt")]], s, d, d, tm2, tn_d, d, [],
                        [((s, d), _F32, (tm2, tn_d), _tile_ij)], _store_epilogue)
    (dw_out,) = weight_grad("dw_out", mixed, dr1_c)
    dproj, dkv, d_g_attn, d_g_conv, d_sinks, d_conv8, l_up, x_out = _mixer_bwd(
        dm, proj, rope, sinks, g_attn, g_conv, conv_w8, dmixed, attn, lse, y_conv, qk_rot,
        comm=[_chip_send_op([(q_up, None, 0, r_ff)]), _exchange_op([dw_out])])
    dproj = _patch_columns("dproj_kv", dproj, dkv, dm.o_k)
    q_out = _pair_sum("chip_sum_w_out", dw_out, x_out)
    dw_in_t, l_out = weight_grad("dw_in", dproj, x_c, comm=[_chip_send_op([(q_out, None, 0, r_out)])])
    (x_in,) = _comm_kernel("exchange_w_in", [_exchange_op([dw_in_t])])
    q_in = _pair_sum("chip_sum_w_in", dw_in_t, x_in)

    grad_x, l_in = _matmul("dx", [[(dproj, w_in_t, "nn")]], s, d, inw, tm, tn_d, inw,
                           [(dr1, (tm, tn_d), _tile_ij)], [((s, d), _F32, (tm, tn_d), _tile_ij)], residual_epilogue,
                           comm=[_chip_send_op([(q_in, None, 0, r_in)])])

    small_parts = [d_conv8[:3], d_sinks, d_g_attn, d_g_conv, d_ln1_g, d_ln1_b, d_ln2_g, d_ln2_b, loss_acc[0:1, 0:1]]
    packed, spans = _pack(small_parts)
    reduced = _unpack(_all_reduce_small("reduce_small", packed), spans, [p.shape for p in small_parts])
    g_conv_full, g_sinks, g_g_attn, g_g_conv, g_ln1_g, g_ln1_b, g_ln2_g, g_ln2_b, loss_sum = reduced
    me = _linear(*_position())
    g_conv_w = lax.dynamic_slice(g_conv_full, (0, me * conv_cols), (3, conv_cols))
    loss = loss_sum[0, 0]

    big = {"w_in": (w_in[0].T, l_in, m_w_in[0].T, v_w_in[0].T), "w_out": (w_out[0], l_out, m_w_out[0], v_w_out[0]),
           "w_gate": (w_gate[0].T, l_gate, m_w_gate[0].T, v_w_gate[0].T),
           "w_up": (w_up[0].T, l_up, m_w_up[0].T, v_w_up[0].T), "w_down": (w_down[0], l_down, m_w_down[0], v_w_down[0])}
    res = {nm: tuple(_adamw(f"adamw_{nm}", w, slots, m, v)) for nm, (w, slots, m, v) in big.items()}
    for nm in ("w_in", "w_gate", "w_up"):
        res[nm] = tuple(a.T for a in res[nm])
    small_names = ["conv_w", "sinks", "g_attn", "g_conv", "ln1_g", "ln1_b", "ln2_g", "ln2_b"]
    small_w = [conv_w, sinks, g_attn, g_conv, ln1_g, ln1_b, ln2_g, ln2_b]
    small_g = [g_conv_w[None], g_sinks, g_g_attn, g_g_conv, g_ln1_g, g_ln1_b, g_ln2_g, g_ln2_b]
    small_m = [m_conv_w, m_sinks, m_g_attn, m_g_conv, m_ln1_g, m_ln1_b, m_ln2_g, m_ln2_b]
    small_v = [v_conv_w, v_sinks, v_g_attn, v_g_conv, v_ln1_g, v_ln1_b, v_ln2_g, v_ln2_b]
    pw, sp = _pack(small_w)
    pg, _ = _pack(small_g)
    pm, _ = _pack(small_m)
    pv, _ = _pack(small_v)
    shapes = [w.shape for w in small_w]
    _, sd, sm, sv = [_unpack(p, sp, shapes) for p in _adamw("adamw_small", pw, pg[None], pm, pv)]
    for i, nm in enumerate(small_names):
        res[nm] = (small_g[i].reshape(shapes[i]), sd[i], sm[i], sv[i])

    order = ["w_in", "conv_w", "sinks", "g_attn", "g_conv", "w_out", "ln1_g", "ln1_b", "w_gate", "w_up", "w_down", "ln2_g", "ln2_b"]

    def lead(a, nm):
        return a[None] if nm in big else a

    return (loss, grad_x[None],
            *[lead(res[nm][0], nm) for nm in order], *[lead(res[nm][1], nm) for nm in order],
            *[lead(res[nm][2], nm) for nm in order], *[lead(res[nm][3], nm) for nm in order])
```

```python
import functools

import jax
import jax.numpy as jnp
from jax import lax
from jax.experimental import pallas as pl
from jax.experimental.pallas import tpu as pltpu

_F32 = jnp.float32
_CDT = jnp.bfloat16

HEAD_DIM = 64
WINDOW = 128
N_KV_HEADS = 4
KV_WIDTH = N_KV_HEADS * HEAD_DIM
ROT_DIM = HEAD_DIM // 4
ROPE_THETA = 500000.0
ATTN_SCALE = HEAD_DIM ** -0.5
DEPTH = 1
DEEPNORM_ALPHA = (2 * DEPTH) ** 0.25
LN_EPS = 1e-5
RMS_EPS = 1e-6
ADAM_LR = 0.001
ADAM_B1 = 0.9
ADAM_B2 = 0.999
ADAM_EPS = 1e-08
ADAM_WD = 0.01
ADAM_STEP = 10
N_DEV = 8
MASKED = -1e30

MIB = 1024 * 1024
V7X_VMEM_BYTES = 64 * MIB
V7X_LANES = 128
V7X_SUBLANES = 8
BODY_TEMPORARIES_BYTES = 16 * MIB
VMEM_LIMIT_FLOOR_BYTES = 32 * MIB
VMEM_LIMIT_CEILING_BYTES = V7X_VMEM_BYTES - 8 * MIB
_MESH = pl.DeviceIdType.MESH
_ANY = pl.BlockSpec(memory_space=pl.ANY)


def _vmem_limit(block_bytes, scratch_bytes=0):
    want = 2 * block_bytes + scratch_bytes + BODY_TEMPORARIES_BYTES
    return int(min(max(want, VMEM_LIMIT_FLOOR_BYTES), VMEM_LIMIT_CEILING_BYTES))


def _nbytes(shape, dtype):
    n = 1
    for s in shape:
        n *= s
    return n * jnp.dtype(dtype).itemsize


def _pick(n, candidates):
    for c in candidates:
        if n % c == 0:
            return c
    raise ValueError(f"no tile of {candidates} divides {n}")


_DOT_DIMS = {"nn": ((1,), (0,)), "nt": ((1,), (1,)), "tn": ((0,), (0,))}


def _dot(a, b, mode):
    return lax.dot_general(a.astype(_CDT), b.astype(_CDT), (_DOT_DIMS[mode], ((), ())),
                           preferred_element_type=_F32)


def _accumulate(ref, val, first):
    @pl.when(first)
    def _():
        ref[...] = val

    @pl.when(jnp.logical_not(first))
    def _():
        ref[...] += val


class _Comm:
    def __init__(self, inputs, outputs, aliases, sems, start, finish, middle=None):
        self.inputs, self.outputs, self.aliases, self.sems = inputs, outputs, aliases, sems
        self.start, self.finish, self.middle = start, finish, middle


def _middle_step(n_steps):
    return (2 * n_steps) // 3


class _CommArgs:
    def __init__(self, comms, n_in_before, n_out_before):
        self.comms, self.operands, self.out_shape, self.aliases, self.sems, self.at = comms, [], [], {}, [], []
        for cm in comms:
            self.at.append((len(self.operands), len(self.out_shape), len(self.sems)))
            for i_in, i_out in cm.aliases.items():
                self.aliases[n_in_before + len(self.operands) + i_in] = n_out_before + len(self.out_shape) + i_out
            self.operands += cm.inputs
            self.out_shape += cm.outputs
            self.sems += cm.sems

    def _each(self, in_refs, out_refs, sem_refs):
        for cm, (i0, o0, s0) in zip(self.comms, self.at):
            yield cm, (in_refs[i0:i0 + len(cm.inputs)], out_refs[o0:o0 + len(cm.outputs)], sem_refs[s0:s0 + len(cm.sems)])

    def start(self, in_refs, out_refs, sem_refs):
        for cm, refs in self._each(in_refs, out_refs, sem_refs):
            cm.start(*refs)

    def finish(self, in_refs, out_refs, sem_refs):
        for cm, refs in self._each(in_refs, out_refs, sem_refs):
            cm.finish(*refs)

    @property
    def has_middle(self):
        return any(cm.middle is not None for cm in self.comms)

    def middle(self, in_refs, out_refs, sem_refs):
        for cm, refs in self._each(in_refs, out_refs, sem_refs):
            if cm.middle is not None:
                cm.middle(*refs)


def _matmul(name, groups, m, n, k, tm, tn, tk, extras, outs, epilogue, comm=(), j_outer=False):
    assert m % tm == 0 and n % tn == 0 and k % tk == 0, (name, m, n, k, tm, tn, tk)
    nk = k // tk
    terms = [t for g in groups for t in g]
    operands, in_specs, block_bytes = [], [], 0

    def spec(blk, imap):
        return pl.BlockSpec(blk, (lambda g0, g1, kk: imap(g1, g0, kk)) if j_outer else imap)

    for a, b, mode in terms:
        assert a.shape == ((k, m) if mode == "tn" else (m, k)), (name, a.shape, mode)
        assert b.shape == ((n, k) if mode == "nt" else (k, n)), (name, b.shape, mode)
        if mode == "tn":
            a_blk, a_map = (tk, tm), (lambda i, j, kk: (kk, i))
        else:
            a_blk, a_map = (tm, tk), (lambda i, j, kk: (i, kk))
        if mode == "nt":
            b_blk, b_map = (tn, tk), (lambda i, j, kk: (j, kk))
        else:
            b_blk, b_map = (tk, tn), (lambda i, j, kk: (kk, j))
        operands += [a, b]
        in_specs += [spec(a_blk, a_map), spec(b_blk, b_map)]
        block_bytes += _nbytes(a_blk, a.dtype) + _nbytes(b_blk, b.dtype)
    for arr, blk, imap in extras:
        operands.append(arr)
        in_specs.append(spec(blk, lambda i, j, kk, imap=imap: imap(i, j)))
        block_bytes += _nbytes(blk, arr.dtype)
    out_shape, out_specs = [], []
    for shape, dtype, blk, imap in outs:
        out_shape.append(jax.ShapeDtypeStruct(shape, dtype))
        out_specs.append(spec(blk, lambda i, j, kk, imap=imap: imap(i, j)))
        block_bytes += _nbytes(blk, dtype)
    n_terms, n_extra, n_out, n_groups = len(terms), len(extras), len(outs), len(groups)
    scratch = [pltpu.VMEM((tm, tn), _F32) for _ in range(n_groups)] if nk > 1 else []
    ca = _CommArgs(list(comm), len(operands), n_out)
    n_cin, n_cout, n_acc = len(ca.operands), len(ca.out_shape), len(scratch)
    tiles = (m // tm, n // tn)
    grid = (tiles[1], tiles[0], nk) if j_outer else (tiles[0], tiles[1], nk)

    def body(*refs):
        refs = list(refs)
        term_refs = [refs.pop(0) for _ in range(2 * n_terms)]
        extra_refs = [refs.pop(0) for _ in range(n_extra)]
        cin_refs = [refs.pop(0) for _ in range(n_cin)]
        out_refs = [refs.pop(0) for _ in range(n_out)]
        cout_refs = [refs.pop(0) for _ in range(n_cout)]
        acc_refs = [refs.pop(0) for _ in range(n_acc)]
        sem_refs = refs
        g0, g1, kk = pl.program_id(0), pl.program_id(1), pl.program_id(2)
        first = jnp.logical_and(g0 == 0, g1 == 0)
        if comm:
            @pl.when(jnp.logical_and(first, kk == 0))
            def _():
                ca.start(cin_refs, cout_refs, sem_refs)
        if ca.has_middle:
            step = (g0 * grid[1] + g1) * nk + kk

            @pl.when(step == _middle_step(grid[0] * grid[1] * nk))
            def _():
                ca.middle(cin_refs, cout_refs, sem_refs)
        partial, t = [], 0
        for g in groups:
            s = None
            for _, _, mode in g:
                d = _dot(term_refs[2 * t][...], term_refs[2 * t + 1][...], mode)
                s = d if s is None else s + d
                t += 1
            partial.append(s)
        if nk == 1:
            epilogue(partial, extra_refs, out_refs, first)
        else:
            for acc, p in zip(acc_refs, partial):
                _accumulate(acc, p, kk == 0)

            @pl.when(kk == nk - 1)
            def _():
                epilogue([acc[...] for acc in acc_refs], extra_refs, out_refs, first)
        if comm:
            @pl.when(jnp.logical_and(jnp.logical_and(g0 == grid[0] - 1, g1 == grid[1] - 1), kk == nk - 1))
            def _():
                ca.finish(cin_refs, cout_refs, sem_refs)

    res = pl.pallas_call(
        body, name=name, grid=grid,
        in_specs=in_specs + [_ANY] * n_cin, out_specs=out_specs + [_ANY] * n_cout,
        out_shape=out_shape + ca.out_shape, scratch_shapes=scratch + ca.sems, input_output_aliases=ca.aliases,
        compiler_params=pltpu.CompilerParams(
            dimension_semantics=("arbitrary", "arbitrary", "arbitrary"),
            vmem_limit_bytes=_vmem_limit(block_bytes, n_groups * tm * tn * 4 if nk > 1 else 0)),
    )(*operands, *ca.operands)
    return list(res[:n_out]) + list(res[n_out:])


def _store_epilogue(accs, extra_refs, out_refs, first):
    for acc, ref in zip(accs, out_refs):
        ref[...] = acc.astype(ref.dtype)


def _tile_ij(i, j):
    return (i, j)


def _row_i(i, j):
    return (i, 0)


def _whole(i, j):
    return (0, 0)


def _mean(v):
    return jnp.mean(v, axis=-1, keepdims=True)


def _ln_fwd(r, g, b):
    xc = r - _mean(r)
    rstd = lax.rsqrt(_mean(xc * xc) + LN_EPS)
    xhat = xc * rstd
    return xhat * g + b, xhat, rstd


def _ln_bwd(dy, xhat, rstd, g):
    dxh = dy * g
    dr = rstd * (dxh - _mean(dxh) - xhat * _mean(dxh * xhat))
    return dr, jnp.sum(dy * xhat, axis=0, keepdims=True), jnp.sum(dy, axis=0, keepdims=True)


def _rms_fwd(a, g):
    rstd = lax.rsqrt(_mean(a * a) + RMS_EPS)
    return a * rstd * g


def _rms_bwd(dm, a, g):
    rstd = lax.rsqrt(_mean(a * a) + RMS_EPS)
    nhat = a * rstd
    dn = dm * g
    da = rstd * (dn - nhat * _mean(dn * nhat))
    return da, jnp.sum(dm * nhat, axis=0, keepdims=True)


def _lane(shape):
    return lax.broadcasted_iota(jnp.int32, shape, 1)


def _row(shape):
    return lax.broadcasted_iota(jnp.int32, shape, 0)


def _rope_tables(pos, invf):
    ang = pos.astype(_F32) * invf
    lane = _lane(ang.shape)
    in_rot = (lane % HEAD_DIM) < ROT_DIM
    first = (lane % ROT_DIM) < ROT_DIM // 2
    cos = jnp.where(in_rot, jnp.cos(ang), 1.0)
    sin = jnp.sin(ang)
    sgn = jnp.where(in_rot, jnp.where(first, -sin, sin), 0.0)
    return cos, sgn


def _rope(t, cos, sgn, sign):
    half = ROT_DIM // 2
    first = (_lane(t.shape) % ROT_DIM) < half
    partner = jnp.where(first, pltpu.roll(t, V7X_LANES - half, 1), pltpu.roll(t, half, 1))
    return t * cos + partner * (sgn * sign)


def _dup_head(t, h):
    g = t[:, 128 * (h // 2):128 * (h // 2) + 128]
    r = pltpu.roll(g, HEAD_DIM, 1)
    lo = _lane(g.shape) < HEAD_DIM
    return jnp.where(lo, g, r) if h % 2 == 0 else jnp.where(lo, r, g)


def _fold_halves(t):
    return t + pltpu.roll(t, HEAD_DIM, 1)


def _halves(t):
    lo = _lane(t.shape) < HEAD_DIM
    zero = jnp.zeros_like(t)
    return jnp.where(lo, t, zero), jnp.where(lo, zero, t)


def _band_mask(n_heads, n_keys, first_block):
    shape = (n_heads * WINDOW, n_keys)
    i = jnp.bitwise_and(_row(shape), WINDOW - 1)
    j = _lane(shape)
    valid = jnp.logical_and(j >= i + 1, j <= i + WINDOW)
    if first_block is not None:
        valid = jnp.logical_and(valid, jnp.logical_or(j >= WINDOW, jnp.logical_not(first_block)))
    return valid


def _stack_heads(pairs):
    return jnp.concatenate([half for t in pairs for half in _halves(t.astype(_CDT))], axis=0)


def _unstack_heads(t, n_pairs):
    lo = _lane((WINDOW, 128)) < HEAD_DIM
    return [jnp.where(lo, t[2 * WINDOW * i:2 * WINDOW * i + WINDOW], t[2 * WINDOW * i + WINDOW:2 * WINDOW * (i + 1)])
            for i in range(n_pairs)]


def _per_head(values):
    n_rows = len(values) * WINDOW
    block = jnp.right_shift(_row((n_rows, 1)), WINDOW.bit_length() - 1)
    out = jnp.zeros((n_rows, 1), _F32)
    for k, v in enumerate(values):
        out = jnp.where(block == k, v, out)
    return out


def _shift_down(z, halo, k):
    out = pltpu.roll(z, k, 0)
    r = _row(z.shape)
    for t in range(k):
        out = jnp.where(r == t, halo[V7X_SUBLANES - k + t:V7X_SUBLANES - k + t + 1, :], out)
    return out


def _shift_up(z, halo, k):
    rows = z.shape[0]
    out = pltpu.roll(z, rows - k, 0)
    r = _row(z.shape)
    for t in range(k):
        out = jnp.where(r == rows - k + t, halo[t:t + 1, :], out)
    return out


class _Dims:
    def __init__(self, s, d, d_ff):
        self.s, self.d, self.d_ff = s, d, d_ff
        self.aw = d // 2
        self.cw = d - self.aw
        self.nq = self.aw // HEAD_DIM
        self.group = self.nq // N_KV_HEADS
        assert self.group % 2 == 0, "a 128-lane pair of query heads must share its kv head"
        self.inw = self.aw + 2 * KV_WIDTH + 3 * self.cw
        self.o_k = self.aw
        self.o_v = self.aw + KV_WIDTH
        self.o_cg = self.aw + 2 * KV_WIDTH
        self.o_bg = self.o_cg + self.cw
        self.o_u = self.o_bg + self.cw
        self.nb = s // WINDOW
        assert s % WINDOW == 0


def _carrying(body, n_in, n_out, n_steps, ca, n_scratch=0):
    n_cin, n_cout = len(ca.operands), len(ca.out_shape)

    def wrapped(*refs):
        refs = list(refs)
        in_refs = [refs.pop(0) for _ in range(n_in)]
        cin_refs = [refs.pop(0) for _ in range(n_cin)]
        out_refs = [refs.pop(0) for _ in range(n_out)]
        cout_refs = [refs.pop(0) for _ in range(n_cout)]
        scratch_refs = [refs.pop(0) for _ in range(n_scratch)]
        if ca.comms:
            @pl.when(pl.program_id(0) == 0)
            def _():
                ca.start(cin_refs, cout_refs, refs)
        if ca.has_middle:
            @pl.when(pl.program_id(0) == _middle_step(n_steps))
            def _():
                ca.middle(cin_refs, cout_refs, refs)
        body(*in_refs, *out_refs, *scratch_refs)
        if ca.comms:
            @pl.when(pl.program_id(0) == n_steps - 1)
            def _():
                ca.finish(cin_refs, cout_refs, refs)

    return wrapped


def _row_kernel(name, body, rows_in, vecs_in, rows_out, vecs_out, comm=()):
    s = rows_in[0].shape[0]
    tr = _pick(s, (256, 128))
    row = lambda a: pl.BlockSpec((tr, a[1] if isinstance(a, tuple) else a.shape[1]), lambda i: (i, 0))
    vec = lambda shape: pl.BlockSpec(tuple(shape), lambda i: (0, 0))
    n_in, n_out = len(rows_in) + len(vecs_in), len(rows_out) + len(vecs_out)
    ca = _CommArgs(list(comm), n_in, n_out)
    blocks = sum(_nbytes((tr, a.shape[1]), a.dtype) for a in rows_in) + sum(_nbytes((tr, sh[1]), dt) for sh, dt in rows_out)
    res = pl.pallas_call(
        _carrying(body, n_in, n_out, s // tr, ca), name=name, grid=(s // tr,),
        in_specs=[row(a) for a in rows_in] + [vec(v.shape) for v in vecs_in] + [_ANY] * len(ca.operands),
        out_specs=[row(sh) for sh, _ in rows_out] + [vec(sh) for sh, _ in vecs_out] + [_ANY] * len(ca.out_shape),
        out_shape=[jax.ShapeDtypeStruct(sh, dt) for sh, dt in list(rows_out) + list(vecs_out)] + ca.out_shape,
        scratch_shapes=ca.sems, input_output_aliases=ca.aliases,
        compiler_params=pltpu.CompilerParams(dimension_semantics=("arbitrary",), vmem_limit_bytes=_vmem_limit(blocks)),
    )(*rows_in, *vecs_in, *ca.operands)
    return list(res)


def _ln2_loss_bwd(r2, target, gain, bias, comm=()):
    s, d = r2.shape

    def body(r_ref, t_ref, g_ref, b_ref, dr_ref, drc_ref, loss_ref, dg_ref, db_ref):
        first = pl.program_id(0) == 0
        yv, xhat, rstd = _ln_fwd(r_ref[...], g_ref[...], b_ref[...])
        err = yv - t_ref[...]
        dr2, dg, db = _ln_bwd(err * (1.0 / d), xhat, rstd, g_ref[...])
        dr_ref[...] = dr2
        drc_ref[...] = dr2.astype(_CDT)
        _accumulate(loss_ref, jnp.zeros(loss_ref.shape, _F32) + 0.5 * jnp.sum(err * err) * (1.0 / d), first)
        _accumulate(dg_ref, dg, first)
        _accumulate(db_ref, db, first)

    return _row_kernel("ln2_loss_bwd", body, [r2, target], [gain, bias], [((s, d), _F32), ((s, d), _CDT)],
                       [((V7X_SUBLANES, V7X_LANES), _F32), ((1, d), _F32), ((1, d), _F32)], comm)


def _ln1_fwd_rows(r1, gain, bias, comm=()):
    s, d = r1.shape

    def body(r_ref, g_ref, b_ref, h_ref, hc_ref):
        h1, _, _ = _ln_fwd(r_ref[...], g_ref[...], b_ref[...])
        h_ref[...] = h1
        hc_ref[...] = h1.astype(_CDT)

    return _row_kernel("ln1", body, [r1], [gain, bias], [((s, d), _F32), ((s, d), _CDT)], [], comm)


def _ln1_bwd_rows(dh1, r1, gain, comm=()):
    s, d = dh1.shape

    def body(dh_ref, r_ref, g_ref, dr_ref, drc_ref, dg_ref, db_ref):
        first = pl.program_id(0) == 0
        _, xhat, rstd = _ln_fwd(r_ref[...], g_ref[...], 0.0)
        dr1, dg, db = _ln_bwd(dh_ref[...], xhat, rstd, g_ref[...])
        dr_ref[...] = dr1
        drc_ref[...] = dr1.astype(_CDT)
        _accumulate(dg_ref, dg, first)
        _accumulate(db_ref, db, first)

    return _row_kernel("ln1_bwd", body, [dh1, r1], [gain], [((s, d), _F32), ((s, d), _CDT)],
                       [((1, d), _F32), ((1, d), _F32)], comm)


def _mixer_fwd(dm, proj, rope, sinks, g_attn, g_conv, conv_w8, comm=()):
    s, d, aw, cw, nq, inw, nb = dm.s, dm.d, dm.aw, dm.cw, dm.nq, dm.inw, dm.nb

    def body(pp_ref, pc_ref, ropep_ref, ropec_ref, sinks_ref, ga_ref, gc_ref, cw_ref,
             mixed_ref, attn_ref, lse_ref, y_ref, qk_ref):
        n = pl.program_id(0)
        cos_c, sgn_c = ropec_ref[:, 0:V7X_LANES], ropec_ref[:, V7X_LANES:2 * V7X_LANES]
        cos_p, sgn_p = ropep_ref[:, 0:V7X_LANES], ropep_ref[:, V7X_LANES:2 * V7X_LANES]
        for g in range(KV_WIDTH // 128):
            qk_ref[:, aw + 128 * g:aw + 128 * g + 128] = _rope(
                pc_ref[:, dm.o_k + 128 * g:dm.o_k + 128 * g + 128], cos_c, sgn_c, 1.0).astype(qk_ref.dtype)
        for j in range(nq // 2):
            qk_ref[:, 128 * j:128 * j + 128] = _rope(pc_ref[:, 128 * j:128 * j + 128], cos_c, sgn_c, 1.0).astype(qk_ref.dtype)
        k_prev = jnp.concatenate([_rope(pp_ref[:, dm.o_k + 128 * g:dm.o_k + 128 * g + 128], cos_p, sgn_p, 1.0)
                                  for g in range(KV_WIDTH // 128)], axis=1)
        kk = jnp.concatenate([k_prev, qk_ref[:, aw:aw + KV_WIDTH].astype(_F32)], axis=0)
        vv = jnp.concatenate([pp_ref[:, dm.o_v:dm.o_v + KV_WIDTH], pc_ref[:, dm.o_v:dm.o_v + KV_WIDTH]], axis=0)
        group, pairs = dm.group, dm.group // 2
        valid = _band_mask(group, 2 * WINDOW, n == 0)
        for h in range(N_KV_HEADS):
            k2, v2 = _dup_head(kk, h).astype(_CDT), _dup_head(vv, h).astype(_CDT)
            q4 = _stack_heads([qk_ref[:, 128 * j:128 * j + 128] for j in range(pairs * h, pairs * (h + 1))])
            sc = jnp.where(valid, _dot(q4, k2, "nt") * ATTN_SCALE, MASKED)
            sink = _per_head([sinks_ref[0, group * h + r] for r in range(group)])
            mx = jnp.maximum(jnp.max(sc, axis=1, keepdims=True), sink)
            p = jnp.exp(sc - mx)
            den = jnp.sum(p, axis=1, keepdims=True) + jnp.exp(sink - mx)
            out = _unstack_heads(_dot(p / den, v2, "nn"), pairs)
            lse = mx + jnp.log(den)
            for r in range(group):
                lse_ref[:, group * h + r:group * h + r + 1] = lse[WINDOW * r:WINDOW * (r + 1)]
            for i in range(pairs):
                j = pairs * h + i
                attn_ref[:, 128 * j:128 * j + 128] = out[i]
        mixed_ref[:, 0:aw] = _rms_fwd(attn_ref[...], ga_ref[...]).astype(mixed_ref.dtype)

        z = pc_ref[:, dm.o_cg:dm.o_cg + cw] * pc_ref[:, dm.o_u:dm.o_u + cw]
        top = WINDOW - V7X_SUBLANES
        halo = pp_ref[top:WINDOW, dm.o_cg:dm.o_cg + cw] * pp_ref[top:WINDOW, dm.o_u:dm.o_u + cw]
        halo = jnp.where(n == 0, jnp.zeros_like(halo), halo)
        y = cw_ref[0:1, :] * _shift_down(z, halo, 2) + cw_ref[1:2, :] * _shift_down(z, halo, 1) + cw_ref[2:3, :] * z
        y_ref[...] = y
        conv = pc_ref[:, dm.o_bg:dm.o_bg + cw] * y
        mixed_ref[:, aw:d] = _rms_fwd(conv, gc_ref[...]).astype(mixed_ref.dtype)

    prev = lambda n: (jnp.maximum(n - 1, 0), 0)
    cur = lambda n: (n, 0)
    fixed = lambda n: (0, 0)
    blocks = 2 * WINDOW * inw * 4 + WINDOW * (d * 2 + aw * 4 + cw * 4 + nq * 4)
    ca = _CommArgs(list(comm), 8, 5)
    return pl.pallas_call(
        _carrying(body, 8, 5, nb, ca), name="mixer_fwd", grid=(nb,),
        in_specs=[pl.BlockSpec((WINDOW, inw), prev), pl.BlockSpec((WINDOW, inw), cur),
                  pl.BlockSpec((WINDOW, 2 * V7X_LANES), prev), pl.BlockSpec((WINDOW, 2 * V7X_LANES), cur),
                  pl.BlockSpec(memory_space=pltpu.SMEM),
                  pl.BlockSpec((1, aw), fixed), pl.BlockSpec((1, cw), fixed), pl.BlockSpec((V7X_SUBLANES, cw), fixed)]
        + [_ANY] * len(ca.operands),
        out_specs=[pl.BlockSpec((WINDOW, d), cur), pl.BlockSpec((WINDOW, aw), cur),
                   pl.BlockSpec((WINDOW, nq), cur), pl.BlockSpec((WINDOW, cw), cur),
                   pl.BlockSpec((WINDOW, aw + KV_WIDTH), cur)] + [_ANY] * len(ca.out_shape),
        out_shape=[jax.ShapeDtypeStruct((s, d), _CDT), jax.ShapeDtypeStruct((s, aw), _F32),
                   jax.ShapeDtypeStruct((s, nq), _F32), jax.ShapeDtypeStruct((s, cw), _F32),
                   jax.ShapeDtypeStruct((s, aw + KV_WIDTH), _CDT)] + ca.out_shape,
        scratch_shapes=ca.sems, input_output_aliases=ca.aliases,
        compiler_params=pltpu.CompilerParams(dimension_semantics=("arbitrary",), vmem_limit_bytes=_vmem_limit(blocks)),
    )(proj, proj, rope, rope, sinks, g_attn, g_conv, conv_w8, *ca.operands)


def _patch_columns(name, a, part, offset):
    s, pw = part.shape
    assert offset % pw == 0 and pw % V7X_LANES == 0
    tr = _pick(s, (512, 256, 128))

    def body(a_ref, p_ref, o_ref):
        del a_ref
        o_ref[...] = p_ref[...]

    return pl.pallas_call(
        body, name=name, grid=(s // tr,),
        in_specs=[_ANY, pl.BlockSpec((tr, pw), lambda i: (i, 0))],
        out_specs=pl.BlockSpec((tr, pw), lambda i: (i, offset // pw)),
        out_shape=jax.ShapeDtypeStruct(a.shape, a.dtype), input_output_aliases={0: 0},
        compiler_params=pltpu.CompilerParams(dimension_semantics=("arbitrary",)),
    )(a, part)


def _mixer_bwd(dm, proj, rope, sinks, g_attn, g_conv, conv_w8, dmixed, attn, lse, y, qk, comm=()):
    s, d, aw, cw, nq, inw, nb = dm.s, dm.d, dm.aw, dm.cw, dm.nq, dm.inw, dm.nb

    def body(pp_ref, pc_ref, pn_ref, ropep_ref, ropec_ref, dmc_ref, dmn_ref, ac_ref,
             lsec_ref, yc_ref, yn_ref, qkp_ref, qkc_ref, sinks_ref, ga_ref, gc_ref, cw_ref,
             dproj_ref, dkv_ref, dga_ref, dgc_ref, dsinks_ref, dcw_ref, dk_carry, dv_carry):
        n = pl.program_id(0)
        first = n == 0
        live = n < nb
        has_next = n < nb - 1
        cos_p, sgn_p = ropep_ref[:, 0:V7X_LANES], ropep_ref[:, V7X_LANES:2 * V7X_LANES]
        cos_c, sgn_c = ropec_ref[:, 0:V7X_LANES], ropec_ref[:, V7X_LANES:2 * V7X_LANES]

        @pl.when(first)
        def _():
            dk_carry[...] = jnp.zeros(dk_carry.shape, _F32)
            dv_carry[...] = jnp.zeros(dv_carry.shape, _F32)

        def write_kv(dk2, dv2, cos, sgn):
            lo = _lane((WINDOW, 128)) < HEAD_DIM
            for g in range(KV_WIDTH // 128):
                dk = jnp.where(lo, _fold_halves(dk2[2 * g]), _fold_halves(dk2[2 * g + 1]))
                dv = jnp.where(lo, _fold_halves(dv2[2 * g]), _fold_halves(dv2[2 * g + 1]))
                dkv_ref[:, 128 * g:128 * g + 128] = _rope(dk, cos, sgn, -1.0).astype(dkv_ref.dtype)
                dkv_ref[:, KV_WIDTH + 128 * g:KV_WIDTH + 128 * g + 128] = dv.astype(dkv_ref.dtype)

        @pl.when(jnp.logical_not(live))
        def _():
            write_kv([dk_carry[h] for h in range(N_KV_HEADS)], [dv_carry[h] for h in range(N_KV_HEADS)], cos_c, sgn_c)

        @pl.when(live)
        def _():
            block_step(pp_ref, pc_ref, pn_ref, dmc_ref, dmn_ref, ac_ref, lsec_ref, yc_ref, yn_ref, qkp_ref, qkc_ref,
                       sinks_ref, ga_ref, gc_ref, cw_ref, dproj_ref, dga_ref, dgc_ref, dsinks_ref, dcw_ref, dk_carry,
                       dv_carry, first, has_next, cos_p, sgn_p, cos_c, sgn_c, write_kv)

    def block_step(pp_ref, pc_ref, pn_ref, dmc_ref, dmn_ref, ac_ref, lsec_ref, yc_ref, yn_ref, qkp_ref, qkc_ref,
                   sinks_ref, ga_ref, gc_ref, cw_ref, dproj_ref, dga_ref, dgc_ref, dsinks_ref, dcw_ref, dk_carry,
                   dv_carry, first, has_next, cos_p, sgn_p, cos_c, sgn_c, write_kv):
        da_c, dga = _rms_bwd(dmc_ref[:, 0:aw], ac_ref[...], ga_ref[...])
        _accumulate(dga_ref, dga, first)
        kk = jnp.concatenate([qkp_ref[:, aw:aw + KV_WIDTH], qkc_ref[:, aw:aw + KV_WIDTH]], axis=0).astype(_F32)
        vv = jnp.concatenate([pp_ref[:, dm.o_v:dm.o_v + KV_WIDTH], pc_ref[:, dm.o_v:dm.o_v + KV_WIDTH]], axis=0)
        group, pairs = dm.group, dm.group // 2
        valid_c = _band_mask(group, 2 * WINDOW, first)
        dk_prev, dv_prev = [], []
        dsinks = jnp.zeros((1, nq), _F32)
        head_lane = _lane((1, nq))

        def stacked(q_ref, da, o_ref, lse_ref_, h):
            cols = [slice(128 * j, 128 * j + 128) for j in range(pairs * h, pairs * (h + 1))]
            q4 = _stack_heads([q_ref[:, c] for c in cols])
            do4 = _stack_heads([da[:, c] for c in cols])
            lo = _lane((WINDOW, 128)) < HEAD_DIM
            deltas = []
            for c in cols:
                prod = o_ref[:, c] * da[:, c]
                deltas += [jnp.sum(jnp.where(lo, prod, 0.0), axis=1, keepdims=True),
                           jnp.sum(jnp.where(lo, 0.0, prod), axis=1, keepdims=True)]
            lse4 = jnp.concatenate([lse_ref_[:, group * h + r:group * h + r + 1] for r in range(group)], axis=0)
            return q4, do4, lse4, jnp.concatenate(deltas, axis=0)

        def scores_bwd(q4, do4, lse4, delta4, keys, vals, valid):
            sc = _dot(q4, keys, "nt") * ATTN_SCALE
            p = jnp.exp(jnp.where(valid, sc - lse4, MASKED))
            return p.astype(_CDT), (p * (_dot(do4, vals, "nt") - delta4) * ATTN_SCALE).astype(_CDT)

        for h in range(N_KV_HEADS):
            k2, v2 = _dup_head(kk, h).astype(_CDT), _dup_head(vv, h).astype(_CDT)
            q4, do4, lse4, delta4 = stacked(qkc_ref, da_c, ac_ref, lsec_ref, h)
            p, ds = scores_bwd(q4, do4, lse4, delta4, k2, v2, valid_c)
            for i, dq in enumerate(_unstack_heads(_dot(ds, k2, "nn"), pairs)):
                j = pairs * h + i
                dproj_ref[:, 128 * j:128 * j + 128] = _rope(dq, cos_c, sgn_c, -1.0).astype(dproj_ref.dtype)
            dk = _dot(ds, q4, "tn")
            dv = _dot(p, do4, "tn")
            dk_prev.append(dk_carry[h] + dk[0:WINDOW])
            dv_prev.append(dv_carry[h] + dv[0:WINDOW])
            dk_carry[h] = dk[WINDOW:2 * WINDOW]
            dv_carry[h] = dv[WINDOW:2 * WINDOW]
            sink4 = _per_head([sinks_ref[0, group * h + r] for r in range(group)])
            loss_sink = jnp.exp(sink4 - lse4) * delta4
            for r in range(group):
                dsinks = dsinks + jnp.where(head_lane == group * h + r,
                                            -jnp.sum(loss_sink[WINDOW * r:WINDOW * (r + 1)]), 0.0)
        _accumulate(dsinks_ref, dsinks, first)
        write_kv(dk_prev, dv_prev, cos_p, sgn_p)

        bg = pc_ref[:, dm.o_bg:dm.o_bg + cw]
        yc = yc_ref[...]
        dconv, dgc = _rms_bwd(dmc_ref[:, aw:d], bg * yc, gc_ref[...])
        _accumulate(dgc_ref, dgc, first)
        dproj_ref[:, dm.o_bg:dm.o_bg + cw] = (dconv * yc).astype(dproj_ref.dtype)
        dy = dconv * bg
        bg_n = pn_ref[:, dm.o_bg:dm.o_bg + cw]
        dconv_n, _ = _rms_bwd(dmn_ref[:, aw:d], bg_n * yn_ref[...], gc_ref[...])
        halo = jnp.where(has_next, dconv_n * bg_n, 0.0)
        dy1 = _shift_up(dy, halo, 1)
        dy2 = _shift_up(dy, halo, 2)
        dz = cw_ref[2:3, :] * dy + cw_ref[1:2, :] * dy1 + cw_ref[0:1, :] * dy2
        cg = pc_ref[:, dm.o_cg:dm.o_cg + cw]
        u = pc_ref[:, dm.o_u:dm.o_u + cw]
        dproj_ref[:, dm.o_cg:dm.o_cg + cw] = (dz * u).astype(dproj_ref.dtype)
        dproj_ref[:, dm.o_u:dm.o_u + cw] = (dz * cg).astype(dproj_ref.dtype)
        z = cg * u
        dcw = jnp.concatenate(
            [jnp.sum(z * t, axis=0, keepdims=True) for t in (dy2, dy1, dy)]
            + [jnp.zeros((V7X_SUBLANES - 3, cw), _F32)], axis=0)
        _accumulate(dcw_ref, dcw, first)

    at = lambda n: jnp.minimum(n, nb - 1)
    prev = lambda n: (jnp.maximum(at(n) - 1, 0), 0)
    cur = lambda n: (at(n), 0)
    done = lambda n: (jnp.maximum(n - 1, 0), 0)
    nxt8 = lambda n: (jnp.minimum((at(n) + 1) * (WINDOW // V7X_SUBLANES), s // V7X_SUBLANES - 1), 0)
    fixed = lambda n: (0, 0)
    blocks = WINDOW * (2 * inw * 4 + d * 4 + aw * 4 + cw * 4 + inw * 2 + 2 * KV_WIDTH * 2)
    carry = [pltpu.VMEM((N_KV_HEADS, WINDOW, 128), _F32), pltpu.VMEM((N_KV_HEADS, WINDOW, 128), _F32)]
    n_in, n_out = 17, 6
    ca = _CommArgs(list(comm), n_in, n_out)
    return pl.pallas_call(
        _carrying(body, n_in, n_out, nb + 1, ca, n_scratch=len(carry)), name="mixer_bwd", grid=(nb + 1,),
        in_specs=[pl.BlockSpec((WINDOW, inw), prev), pl.BlockSpec((WINDOW, inw), cur), pl.BlockSpec((V7X_SUBLANES, inw), nxt8),
                  pl.BlockSpec((WINDOW, 2 * V7X_LANES), prev), pl.BlockSpec((WINDOW, 2 * V7X_LANES), cur),
                  pl.BlockSpec((WINDOW, d), cur), pl.BlockSpec((V7X_SUBLANES, d), nxt8),
                  pl.BlockSpec((WINDOW, aw), cur), pl.BlockSpec((WINDOW, nq), cur),
                  pl.BlockSpec((WINDOW, cw), cur), pl.BlockSpec((V7X_SUBLANES, cw), nxt8),
                  pl.BlockSpec((WINDOW, aw + KV_WIDTH), prev), pl.BlockSpec((WINDOW, aw + KV_WIDTH), cur),
                  pl.BlockSpec(memory_space=pltpu.SMEM),
                  pl.BlockSpec((1, aw), fixed), pl.BlockSpec((1, cw), fixed), pl.BlockSpec((V7X_SUBLANES, cw), fixed)]
        + [_ANY] * len(ca.operands),
        out_specs=[pl.BlockSpec((WINDOW, inw), cur), pl.BlockSpec((WINDOW, 2 * KV_WIDTH), done),
                   pl.BlockSpec((1, aw), fixed), pl.BlockSpec((1, cw), fixed),
                   pl.BlockSpec((1, nq), fixed), pl.BlockSpec((V7X_SUBLANES, cw), fixed)] + [_ANY] * len(ca.out_shape),
        out_shape=[jax.ShapeDtypeStruct((s, inw), _CDT), jax.ShapeDtypeStruct((s, 2 * KV_WIDTH), _CDT),
                   jax.ShapeDtypeStruct((1, aw), _F32), jax.ShapeDtypeStruct((1, cw), _F32),
                   jax.ShapeDtypeStruct((1, nq), _F32), jax.ShapeDtypeStruct((V7X_SUBLANES, cw), _F32)] + ca.out_shape,
        scratch_shapes=carry + ca.sems, input_output_aliases=ca.aliases,
        compiler_params=pltpu.CompilerParams(dimension_semantics=("arbitrary",), vmem_limit_bytes=_vmem_limit(blocks)),
    )(proj, proj, proj, rope, rope, dmixed, dmixed, attn, lse, y, y, qk, qk, sinks, g_attn, g_conv, conv_w8, *ca.operands)


def _position():
    return lax.axis_index("x"), lax.axis_index("y"), lax.axis_index("c")


def _linear(px, py, pc):
    return 4 * px + 2 * py + pc


def _comm_kernel(name, comm):
    ca = _CommArgs(list(comm), 0, 0)
    n_cin, n_cout = len(ca.operands), len(ca.out_shape)

    def body(*refs):
        cin, cout, sems = refs[:n_cin], refs[n_cin:n_cin + n_cout], refs[n_cin + n_cout:]
        ca.start(cin, cout, sems)
        ca.middle(cin, cout, sems)
        ca.finish(cin, cout, sems)

    return pl.pallas_call(
        body, name=name, out_shape=ca.out_shape, in_specs=[_ANY] * n_cin, out_specs=[_ANY] * n_cout,
        scratch_shapes=ca.sems, input_output_aliases=ca.aliases,
    )(*ca.operands)


def _gather_op(units):
    n = len(units)
    inputs, outputs, aliases = [], [], {}
    for shard, _, _, _ in units:
        inputs.append(shard)
        outputs.append(jax.ShapeDtypeStruct((N_DEV * shard.shape[0], shard.shape[1]), shard.dtype))
    for u, (_, buf, _, _) in enumerate(units):
        if buf is not None:
            aliases[len(inputs)] = u
            inputs.append(buf)

    def plan(ins, outs, sems, north):
        send_sems, recv_sems, local_sems = sems
        x, y, c = _position()
        me, sibling = (x, y, c), (x, y, 1 - c)
        xn, yn, dg = (1 - x, y), (x, 1 - y), (1 - x, 1 - y)
        via, to, k_via, k_other = (yn, xn, 2, 1) if north else (xn, yn, 1, 2)

        def rows(u, px, py, pc):
            shard, _, r0, r1 = units[u]
            return outs[u].at[pl.ds(pl.multiple_of(_linear(px, py, pc) * shard.shape[0] + r0, 16), r1 - r0), :]

        def own(u):
            _, _, r0, r1 = units[u]
            return ins[u].at[pl.ds(r0, r1 - r0), :]

        def copy(u, k, block, to_, src=None):
            return pltpu.make_async_remote_copy(
                src_ref=rows(u, *block) if src is None else src, dst_ref=rows(u, *block),
                send_sem=send_sems.at[u, k], recv_sem=recv_sems.at[u, k], device_id=to_, device_id_type=_MESH)

        us = range(n)
        return dict(
            mine=[pltpu.make_async_copy(own(u), rows(u, *me), local_sems.at[u]) for u in us],
            first=[cp for u in us for cp in (copy(u, 0, me, sibling, src=own(u)), copy(u, 1, me, (*xn, c), src=own(u)),
                                             copy(u, 2, me, (*yn, c), src=own(u)))],
            relay=[copy(u, 3, (*via, c), (*to, c)) for u in us],
            arrived={1: [copy(u, 1, (*xn, c), me) for u in us], 2: [copy(u, 2, (*yn, c), me) for u in us],
                     3: [copy(u, 3, (*dg, c), me) for u in us]},
            passed={1: [copy(u, 4, (*xn, c), sibling) for u in us], 2: [copy(u, 5, (*yn, c), sibling) for u in us],
                    3: [copy(u, 6, (*dg, c), sibling) for u in us]},
            rest=[cp for u in us for cp in (copy(u, 0, sibling, me), copy(u, 4, (*xn, 1 - c), me),
                                            copy(u, 5, (*yn, 1 - c), me), copy(u, 6, (*dg, 1 - c), me))],
            k_via=k_via, k_other=k_other)

    def land(p, k):
        for arrived, onward in zip(p["arrived"][k], p["passed"][k]):
            arrived.wait_recv()
            onward.start()

    def by_core(fn):
        c = lax.axis_index("c")
        for north in (True, False):
            pl.when(c == (1 if north else 0))(functools.partial(fn, north))

    def start(ins, outs, sems):
        p = plan(ins, outs, sems, True)
        for cp in p["mine"] + p["first"]:
            cp.start()

    def middle(ins, outs, sems):
        def go(north):
            p = plan(ins, outs, sems, north)
            land(p, p["k_via"])
            for cp in p["relay"]:
                cp.start()
            land(p, p["k_other"])
        by_core(go)

    def finish(ins, outs, sems):
        def go(north):
            p = plan(ins, outs, sems, north)
            land(p, 3)
            for cp in p["rest"]:
                cp.wait_recv()
            for cp in p["first"] + p["relay"] + [cp for k in (1, 2, 3) for cp in p["passed"][k]]:
                cp.wait_send()
            for cp in p["mine"]:
                cp.wait()
        by_core(go)

    sems = [pltpu.SemaphoreType.DMA((n, 7)), pltpu.SemaphoreType.DMA((n, 7)), pltpu.SemaphoreType.DMA((n,))]
    return _Comm(inputs, outputs, aliases, sems, start, finish, middle)


def _peers(x, y, c):
    out = []
    for k in range(1, N_DEV):
        fx, fy, fc = (k >> 2) & 1, (k >> 1) & 1, k & 1
        out.append((1 - x if fx else x, 1 - y if fy else y, 1 - c if fc else c))
    return out


def _exchange_op(partials):
    n = len(partials)
    outputs = [jax.ShapeDtypeStruct((4, p.shape[0] // N_DEV, p.shape[1]), p.dtype) for p in partials]

    def plan(ins, outs, sems):
        send_sems, recv_sems = sems
        x, y, c = _position()
        out = []
        for a in range(n):
            r = outs[a].shape[1]
            for ch in range(4):
                out.append(pltpu.make_async_remote_copy(
                    src_ref=ins[a].at[pl.ds(pl.multiple_of((2 * ch + 1 - c) * r, 16), r), :], dst_ref=outs[a].at[ch],
                    send_sem=send_sems.at[a, ch], recv_sem=recv_sems.at[a, ch], device_id=(x, y, 1 - c),
                    device_id_type=_MESH))
        return out

    def start(ins, outs, sems):
        for cp in plan(ins, outs, sems):
            cp.start()

    def finish(ins, outs, sems):
        copies = plan(ins, outs, sems)
        for cp in copies:
            cp.wait_recv()
        for cp in copies:
            cp.wait_send()

    sems = [pltpu.SemaphoreType.DMA((n, 4)), pltpu.SemaphoreType.DMA((n, 4))]
    return _Comm(list(partials), outputs, {}, sems, start, finish)


def _chip_send_op(units):
    n = len(units)
    inputs, outputs, aliases = [], [], {}
    for q, _, _, _ in units:
        inputs.append(q)
        outputs.append(jax.ShapeDtypeStruct(q.shape, q.dtype))
    for u, (_, buf, _, _) in enumerate(units):
        if buf is not None:
            aliases[len(inputs)] = u
            inputs.append(buf)

    def plan(ins, outs, sems):
        send_sems, recv_sems, local_sems = sems
        x, y, c = _position()
        my_chip = 2 * x + y
        chips = [(1 - x, y), (x, 1 - y), (1 - x, 1 - y)]
        mine, sends, arrivals = [], [], []
        for u, (_, _, r0, r1) in enumerate(units):
            span = pl.ds(r0, r1 - r0)
            mine.append(pltpu.make_async_copy(ins[u].at[my_chip, span, :], outs[u].at[my_chip, span, :], local_sems.at[u]))
            for k, (px, py) in enumerate(chips):
                sends.append(pltpu.make_async_remote_copy(
                    src_ref=ins[u].at[2 * px + py, span, :], dst_ref=outs[u].at[my_chip, span, :],
                    send_sem=send_sems.at[u, k], recv_sem=recv_sems.at[u, k], device_id=(px, py, c), device_id_type=_MESH))
                arrivals.append(pltpu.make_async_remote_copy(
                    src_ref=ins[u].at[my_chip, span, :], dst_ref=outs[u].at[2 * px + py, span, :],
                    send_sem=send_sems.at[u, k], recv_sem=recv_sems.at[u, k], device_id=(px, py, c), device_id_type=_MESH))
        return mine, sends, arrivals

    def start(ins, outs, sems):
        mine, sends, _ = plan(ins, outs, sems)
        for cp in mine + sends:
            cp.start()

    def finish(ins, outs, sems):
        mine, sends, arrivals = plan(ins, outs, sems)
        for cp in arrivals:
            cp.wait_recv()
        for cp in sends:
            cp.wait_send()
        for cp in mine:
            cp.wait()

    sems = [pltpu.SemaphoreType.DMA((n, 3)), pltpu.SemaphoreType.DMA((n, 3)), pltpu.SemaphoreType.DMA((n,))]
    return _Comm(inputs, outputs, aliases, sems, start, finish)


def _pair_sum(name, partial, received):
    _, rows, cols = received.shape
    tr = _pick(rows, (352, 288, 256, 128, 64, 32, 16))
    p4 = partial.reshape(4, 2, rows, cols)
    kind = jnp.reshape(lax.axis_index("c"), (1,)).astype(jnp.int32)

    def body(kind_ref, p_ref, r_ref, o_ref):
        o_ref[0] = (p_ref[0, 0].astype(_F32) + r_ref[0].astype(_F32)).astype(o_ref.dtype)

    return pl.pallas_call(
        body, name=name,
        grid_spec=pltpu.PrefetchScalarGridSpec(
            num_scalar_prefetch=1, grid=(4, rows // tr),
            in_specs=[pl.BlockSpec((1, 1, tr, cols), lambda ch, i, kind_ref: (ch, kind_ref[0], i, 0)),
                      pl.BlockSpec((1, tr, cols), lambda ch, i, kind_ref: (ch, i, 0))],
            out_specs=pl.BlockSpec((1, tr, cols), lambda ch, i, kind_ref: (ch, i, 0))),
        out_shape=jax.ShapeDtypeStruct(received.shape, received.dtype),
        compiler_params=pltpu.CompilerParams(dimension_semantics=("arbitrary", "arbitrary")),
    )(kind, p4, received)


def _all_reduce_small(name, v):
    rows = v.shape[0]

    def body(v_ref, out_ref, land_ref, send_sems, recv_sems):
        x, y, c = _position()
        me = _linear(x, y, c)
        peers = _peers(x, y, c)
        land_ref[me] = v_ref[...]
        sends = [pltpu.make_async_remote_copy(
            src_ref=v_ref, dst_ref=land_ref.at[me], send_sem=send_sems.at[k], recv_sem=recv_sems.at[k],
            device_id=peer, device_id_type=_MESH) for k, peer in enumerate(peers)]
        for cp in sends:
            cp.start()
        for k, peer in enumerate(peers):
            pltpu.make_async_remote_copy(
                src_ref=v_ref, dst_ref=land_ref.at[_linear(*peer)], send_sem=send_sems.at[k], recv_sem=recv_sems.at[k],
                device_id=peer, device_id_type=_MESH).wait_recv()
        for cp in sends:
            cp.wait_send()
        total = land_ref[0]
        for s in range(1, N_DEV):
            total = total + land_ref[s]
        out_ref[...] = total

    return pl.pallas_call(
        body, name=name, out_shape=jax.ShapeDtypeStruct(v.shape, _F32),
        in_specs=[pl.BlockSpec(memory_space=pltpu.VMEM)], out_specs=pl.BlockSpec(memory_space=pltpu.VMEM),
        scratch_shapes=[pltpu.VMEM((N_DEV, rows, V7X_LANES), _F32), pltpu.SemaphoreType.DMA((7,)), pltpu.SemaphoreType.DMA((7,))],
    )(v)


def _adamw(name, w, slots, m, v):
    rows, cols = w.shape
    n_slots = slots.shape[0]
    tr = _pick(rows, (176, 144, 128, 64, 32, 16, 8))

    def body(w_ref, s_ref, m_ref, v_ref, g_ref, d_ref, nm_ref, nv_ref):
        g = s_ref[0].astype(_F32)
        for k in range(1, n_slots):
            g = g + s_ref[k].astype(_F32)
        nm = ADAM_B1 * m_ref[...] + (1.0 - ADAM_B1) * g
        nv = ADAM_B2 * v_ref[...] + (1.0 - ADAM_B2) * (g * g)
        m_hat = nm / (1.0 - ADAM_B1 ** ADAM_STEP)
        v_hat = nv / (1.0 - ADAM_B2 ** ADAM_STEP)
        g_ref[...] = g
        d_ref[...] = -ADAM_LR * (m_hat / (jnp.sqrt(v_hat) + ADAM_EPS) + ADAM_WD * w_ref[...])
        nm_ref[...] = nm
        nv_ref[...] = nv

    spec = pl.BlockSpec((tr, cols), lambda i: (i, 0))
    blocks = 7 * tr * cols * 4 + _nbytes((n_slots, tr, cols), slots.dtype)
    return pl.pallas_call(
        body, name=name, grid=(rows // tr,),
        in_specs=[spec, pl.BlockSpec((n_slots, tr, cols), lambda i: (0, i, 0)), spec, spec], out_specs=[spec] * 4,
        out_shape=[jax.ShapeDtypeStruct((rows, cols), _F32)] * 4,
        compiler_params=pltpu.CompilerParams(dimension_semantics=("arbitrary",), vmem_limit_bytes=_vmem_limit(blocks)),
    )(w, slots, m, v)


def _pad_rows(a, rows):
    return jnp.pad(a, ((0, rows - a.shape[0]), (0, 0)))


def _pack(parts):
    rows, spans, at = [], [], 0
    for p in parts:
        p = p.reshape(-1)
        r = -(-p.shape[0] // V7X_LANES)
        rows.append(jnp.pad(p, (0, r * V7X_LANES - p.shape[0])).reshape(r, V7X_LANES))
        spans.append((at, r, p.shape[0]))
        at += r
    packed = jnp.concatenate(rows, axis=0)
    return _pad_rows(packed, -(-at // V7X_SUBLANES) * V7X_SUBLANES), spans


def _unpack(packed, spans, shapes):
    return [packed[at:at + r].reshape(-1)[:size].reshape(shape) for (at, r, size), shape in zip(spans, shapes)]


def kernel(x, positions, w_in, conv_w, sinks, g_attn, g_conv, w_out, ln1_g, ln1_b, w_gate, w_up, w_down, ln2_g, ln2_b, loss_target, m_w_in, m_conv_w, m_sinks, m_g_attn, m_g_conv, m_w_out, m_ln1_g, m_ln1_b, m_w_gate, m_w_up, m_w_down, m_ln2_g, m_ln2_b, v_w_in, v_conv_w, v_sinks, v_g_attn, v_g_conv, v_w_out, v_ln1_g, v_ln1_b, v_w_gate, v_w_up, v_w_down, v_ln2_g, v_ln2_b):
    _, s, d = x.shape
    d_ff = N_DEV * w_gate.shape[2]
    dm = _Dims(s, d, d_ff)
    aw, cw, nq, inw = dm.aw, dm.cw, dm.nq, dm.inw
    x2 = x[0]
    pos = positions[0].reshape(s, 1)
    inv_freq = ROPE_THETA ** (-jnp.arange(0, ROT_DIM, 2, dtype=_F32) / ROT_DIM)
    invf = jnp.tile(inv_freq, V7X_LANES // (ROT_DIM // 2)).reshape(1, V7X_LANES)

    conv_cols = conv_w.shape[2]
    sh_in, sh_out = w_in[0].T.astype(_CDT), w_out[0].astype(_CDT)
    sh_gate, sh_up, sh_down = w_gate[0].T.astype(_CDT), w_up[0].T.astype(_CDT), w_down[0].astype(_CDT)
    r_in, r_out, r_ff = sh_in.shape[0], sh_out.shape[0], sh_gate.shape[0]
    q_ff = r_ff // 4
    assert q_ff % 16 == 0
    def prepare_body(x_ref, pos_ref, invf_ref, xc_ref, rope_ref):
        xc_ref[...] = x_ref[...].astype(_CDT)
        cos, sgn = _rope_tables(pos_ref[...], invf_ref[...])
        rope_ref[:, 0:V7X_LANES] = cos
        rope_ref[:, V7X_LANES:2 * V7X_LANES] = sgn

    x_c, rope, w_in_t, conv_all = _row_kernel(
        "prepare_gather_w_in", prepare_body, [x2, pos], [invf], [((s, d), _CDT), ((s, 2 * V7X_LANES), _F32)], [],
        comm=[_gather_op([(sh_in, None, 0, r_in), (_pad_rows(conv_w[0], 16), None, 0, 16)])])
    conv_full = conv_all.reshape(N_DEV, 16, conv_cols)[:, :3, :].transpose(1, 0, 2).reshape(3, cw)
    conv_w8 = _pad_rows(conv_full, V7X_SUBLANES)

    tm = _pick(s, (1024, 512, 256, 128))
    tm2 = _pick(s, (2048, 1024, 512, 256, 128))
    tr = _pick(s, (512, 256, 128))
    tn_in = _pick(inw, (512, 256, 128))
    tn_ff = _pick(d_ff, (512, 256, 128))

    proj, w_out_f, w_gate_t = _matmul(
        "proj", [[(x_c, w_in_t, "nt")]], s, inw, d, tm2, tn_in, d, [],
        [((s, inw), _F32, (tm2, tn_in), _tile_ij)], _store_epilogue,
        comm=[_gather_op([(sh_out, None, 0, r_out), (sh_gate, None, 0, 2 * q_ff)])])
    mixed, attn, lse, y_conv, qk_rot, w_gate_t, w_up_t = _mixer_fwd(
        dm, proj, rope, sinks, g_attn, g_conv, conv_w8,
        comm=[_gather_op([(sh_gate, w_gate_t, 2 * q_ff, r_ff), (sh_up, None, 0, 2 * q_ff)])])

    def residual_epilogue(accs, ex, out, first):
        out[0][...] = DEEPNORM_ALPHA * ex[0][...] + accs[0]

    tn_d = _pick(d, (512,))
    r1, w_up_t = _matmul(
        "out_proj", [[(mixed, w_out_f, "nn")]], s, d, d, tm, tn_d, d, [(x2, (tm, tn_d), _tile_ij)],
        [((s, d), _F32, (tm, tn_d), _tile_ij)], residual_epilogue,
        comm=[_gather_op([(sh_up, w_up_t, 2 * q_ff, 3 * q_ff)])])
    h1, h1_c, w_up_t = _ln1_fwd_rows(r1, ln1_g, ln1_b, comm=[_gather_op([(sh_up, w_up_t, 3 * q_ff, r_ff)])])

    def swiglu_epilogue(accs, ex, out, first):
        gate_v, up_v = accs
        out[0][...] = gate_v
        out[1][...] = up_v
        out[2][...] = (gate_v * jax.nn.sigmoid(gate_v) * up_v).astype(_CDT)

    gate, up, act, w_down_f = _matmul(
        "gate_up", [[(h1_c, w_gate_t, "nt")], [(h1_c, w_up_t, "nt")]], s, d_ff, d, tm, tn_ff, d, [],
        [((s, d_ff), _F32, (tm, tn_ff), _tile_ij), ((s, d_ff), _F32, (tm, tn_ff), _tile_ij),
         ((s, d_ff), _CDT, (tm, tn_ff), _tile_ij)], swiglu_epilogue,
        comm=[_gather_op([(sh_down, None, 0, r_ff)])])

    (r2,) = _matmul("down", [[(act, w_down_f, "nn")]], s, d, d_ff, tm, tn_d, d_ff, [(h1, (tm, tn_d), _tile_ij)],
                    [((s, d), _F32, (tm, tn_d), _tile_ij)], residual_epilogue)
    dr2, dr2_c, loss_acc, d_ln2_g, d_ln2_b = _ln2_loss_bwd(r2, loss_target[0], ln2_g, ln2_b)

    def swiglu_bwd_epilogue(accs, ex, out, first):
        gate_v, up_v = ex[0][...], ex[1][...]
        sig = jax.nn.sigmoid(gate_v)
        out[0][...] = (accs[0] * up_v * (sig * (1.0 + gate_v * (1.0 - sig)))).astype(_CDT)
        out[1][...] = (accs[0] * (gate_v * sig)).astype(_CDT)

    dgate, dup = _matmul(
        "dact", [[(dr2_c, w_down_f, "nt")]], s, d_ff, d, tm2, tn_ff, d,
        [(gate, (tm2, tn_ff), _tile_ij), (up, (tm2, tn_ff), _tile_ij)],
        [((s, d_ff), _CDT, (tm2, tn_ff), _tile_ij), ((s, d_ff), _CDT, (tm2, tn_ff), _tile_ij)], swiglu_bwd_epilogue)
    def weight_grad(name, a, b, comm=()):
        rows = a.shape[1]
        tw, tn_w = _pick(rows, (512, 256, 128)), _pick(d, (1024, 512))
        return _matmul(name, [[(a, b, "tn")]], rows, d, s, tw, tn_w, s, [],
                       [((rows, d), _CDT, (tw, tn_w), _tile_ij)], _store_epilogue, comm=comm, j_outer=True)

    (dw_down,) = weight_grad("dw_down", act, dr2_c)
    dw_gate_t, x_down = weight_grad("dw_gate", dgate, h1_c, comm=[_exchange_op([dw_down])])
    q_down = _pair_sum("chip_sum_w_down", dw_down, x_down)
    dw_up_t, l_down, x_gate = weight_grad(
        "dw_up", dup, h1_c, comm=[_chip_send_op([(q_down, None, 0, 2 * q_ff)]), _exchange_op([dw_gate_t])])
    q_gate = _pair_sum("chip_sum_w_gate", dw_gate_t, x_gate)

    tn_h = _pick(d, (512,))
    dh1, l_down, l_gate, x_up = _matmul(
        "dh1", [[(dgate, w_gate_t, "nn"), (dup, w_up_t, "nn")]], s, d, d_ff, tr, tn_h, d_ff,
        [(dr2, (tr, tn_h), _tile_ij)], [((s, d), _F32, (tr, tn_h), _tile_ij)], residual_epilogue,
        comm=[_chip_send_op([(q_down, l_down, 2 * q_ff, r_ff), (q_gate, None, 0, r_ff)]), _exchange_op([dw_up_t])])
    q_up = _pair_sum("chip_sum_w_up", dw_up_t, x_up)
    dr1, dr1_c, d_ln1_g, d_ln1_b = _ln1_bwd_rows(dh1, r1, ln1_g)
    (dmixed,) = _matmul("dmixed", [[(dr1_c, w_out_f, "nt")]], s, d, d, tm2, tn_d, d, [],
                        [((s, d), _F32, (tm2, tn_d), _tile_ij)], _store_epilogue)
    (dw_out,) = weight_grad("dw_out", mixed, dr1_c)
    dproj, dkv, d_g_attn, d_g_conv, d_sinks, d_conv8, l_up, x_out = _mixer_bwd(
        dm, proj, rope, sinks, g_attn, g_conv, conv_w8, dmixed, attn, lse, y_conv, qk_rot,
        comm=[_chip_send_op([(q_up, None, 0, r_ff)]), _exchange_op([dw_out])])
    dproj = _patch_columns("dproj_kv", dproj, dkv, dm.o_k)
    q_out = _pair_sum("chip_sum_w_out", dw_out, x_out)
    dw_in_t, l_out = weight_grad("dw_in", dproj, x_c, comm=[_chip_send_op([(q_out, None, 0, r_out)])])
    (x_in,) = _comm_kernel("exchange_w_in", [_exchange_op([dw_in_t])])
    q_in = _pair_sum("chip_sum_w_in", dw_in_t, x_in)

    grad_x, l_in = _matmul("dx", [[(dproj, w_in_t, "nn")]], s, d, inw, tm, tn_d, inw,
                           [(dr1, (tm, tn_d), _tile_ij)], [((s, d), _F32, (tm, tn_d), _tile_ij)], residual_epilogue,
                           comm=[_chip_send_op([(q_in, None, 0, r_in)])])

    small_parts = [d_conv8[:3], d_sinks, d_g_attn, d_g_conv, d_ln1_g, d_ln1_b, d_ln2_g, d_ln2_b, loss_acc[0:1, 0:1]]
    packed, spans = _pack(small_parts)
    reduced = _unpack(_all_reduce_small("reduce_small", packed), spans, [p.shape for p in small_parts])
    g_conv_full, g_sinks, g_g_attn, g_g_conv, g_ln1_g, g_ln1_b, g_ln2_g, g_ln2_b, loss_sum = reduced
    me = _linear(*_position())
    g_conv_w = lax.dynamic_slice(g_conv_full, (0, me * conv_cols), (3, conv_cols))
    loss = loss_sum[0, 0]

    big = {"w_in": (w_in[0].T, l_in, m_w_in[0].T, v_w_in[0].T), "w_out": (w_out[0], l_out, m_w_out[0], v_w_out[0]),
           "w_gate": (w_gate[0].T, l_gate, m_w_gate[0].T, v_w_gate[0].T),
           "w_up": (w_up[0].T, l_up, m_w_up[0].T, v_w_up[0].T), "w_down": (w_down[0], l_down, m_w_down[0], v_w_down[0])}
    res = {nm: tuple(_adamw(f"adamw_{nm}", w, slots, m, v)) for nm, (w, slots, m, v) in big.items()}
    for nm in ("w_in", "w_gate", "w_up"):
        res[nm] = tuple(a.T for a in res[nm])
    small_names = ["conv_w", "sinks", "g_attn", "g_conv", "ln1_g", "ln1_b", "ln2_g", "ln2_b"]
    small_w = [conv_w, sinks, g_attn, g_conv, ln1_g, ln1_b, ln2_g, ln2_b]
    small_g = [g_conv_w[None], g_sinks, g_g_attn, g_g_conv, g_ln1_g, g_ln1_b, g_ln2_g, g_ln2_b]
    small_m = [m_conv_w, m_sinks, m_g_attn, m_g_conv, m_ln1_g, m_ln1_b, m_ln2_g, m_ln2_b]
    small_v = [v_conv_w, v_sinks, v_g_attn, v_g_conv, v_ln1_g, v_ln1_b, v_ln2_g, v_ln2_b]
    pw, sp = _pack(small_w)
    pg, _ = _pack(small_g)
    pm, _ = _pack(small_m)
    pv, _ = _pack(small_v)
    shapes = [w.shape for w in small_w]
    _, sd, sm, sv = [_unpack(p, sp, shapes) for p in _adamw("adamw_small", pw, pg[None], pm, pv)]
    for i, nm in enumerate(small_names):
        res[nm] = (small_g[i].reshape(shapes[i]), sd[i], sm[i], sv[i])

    order = ["w_in", "conv_w", "sinks", "g_attn", "g_conv", "w_out", "ln1_g", "ln1_b", "w_gate", "w_up", "w_down", "ln2_g", "ln2_b"]

    def lead(a, nm):
        return a[None] if nm in big else a

    return (loss, grad_x[None],
            *[lead(res[nm][0], nm) for nm in order], *[lead(res[nm][1], nm) for nm in order],
            *[lead(res[nm][2], nm) for nm in order], *[lead(res[nm][3], nm) for nm in order])
```

```python
import functools

import jax
import jax.numpy as jnp
from jax import lax
from jax.experimental import pallas as pl
from jax.experimental.pallas import tpu as pltpu

_F32 = jnp.float32
_CDT = jnp.bfloat16

HEAD_DIM = 64
WINDOW = 128
N_KV_HEADS = 4
KV_WIDTH = N_KV_HEADS * HEAD_DIM
ROT_DIM = HEAD_DIM // 4
ROPE_THETA = 500000.0
ATTN_SCALE = HEAD_DIM ** -0.5
DEPTH = 1
DEEPNORM_ALPHA = (2 * DEPTH) ** 0.25
LN_EPS = 1e-5
RMS_EPS = 1e-6
ADAM_LR = 0.001
ADAM_B1 = 0.9
ADAM_B2 = 0.999
ADAM_EPS = 1e-08
ADAM_WD = 0.01
ADAM_STEP = 10
N_DEV = 8
MASKED = -1e30

MIB = 1024 * 1024
V7X_VMEM_BYTES = 64 * MIB
V7X_LANES = 128
V7X_SUBLANES = 8
BODY_TEMPORARIES_BYTES = 16 * MIB
VMEM_LIMIT_FLOOR_BYTES = 32 * MIB
VMEM_LIMIT_CEILING_BYTES = V7X_VMEM_BYTES - 8 * MIB
_MESH = pl.DeviceIdType.MESH
_ANY = pl.BlockSpec(memory_space=pl.ANY)


def _vmem_limit(block_bytes, scratch_bytes=0):
    want = 2 * block_bytes + scratch_bytes + BODY_TEMPORARIES_BYTES
    return int(min(max(want, VMEM_LIMIT_FLOOR_BYTES), VMEM_LIMIT_CEILING_BYTES))


def _nbytes(shape, dtype):
    n = 1
    for s in shape:
        n *= s
    return n * jnp.dtype(dtype).itemsize


def _pick(n, candidates):
    for c in candidates:
        if n % c == 0:
            return c
    raise ValueError(f"no tile of {candidates} divides {n}")


_DOT_DIMS = {"nn": ((1,), (0,)), "nt": ((1,), (1,)), "tn": ((0,), (0,))}


def _dot(a, b, mode):
    return lax.dot_general(a.astype(_CDT), b.astype(_CDT), (_DOT_DIMS[mode], ((), ())),
                           preferred_element_type=_F32)


def _accumulate(ref, val, first):
    @pl.when(first)
    def _():
        ref[...] = val

    @pl.when(jnp.logical_not(first))
    def _():
        ref[...] += val


class _Comm:
    def __init__(self, inputs, outputs, aliases, sems, start, finish, middle=None):
        self.inputs, self.outputs, self.aliases, self.sems = inputs, outputs, aliases, sems
        self.start, self.finish, self.middle = start, finish, middle


def _middle_step(n_steps):
    return (2 * n_steps) // 3


class _CommArgs:
    def __init__(self, comms, n_in_before, n_out_before):
        self.comms, self.operands, self.out_shape, self.aliases, self.sems, self.at = comms, [], [], {}, [], []
        for cm in comms:
            self.at.append((len(self.operands), len(self.out_shape), len(self.sems)))
            for i_in, i_out in cm.aliases.items():
                self.aliases[n_in_before + len(self.operands) + i_in] = n_out_before + len(self.out_shape) + i_out
            self.operands += cm.inputs
            self.out_shape += cm.outputs
            self.sems += cm.sems

    def _each(self, in_refs, out_refs, sem_refs):
        for cm, (i0, o0, s0) in zip(self.comms, self.at):
            yield cm, (in_refs[i0:i0 + len(cm.inputs)], out_refs[o0:o0 + len(cm.outputs)], sem_refs[s0:s0 + len(cm.sems)])

    def start(self, in_refs, out_refs, sem_refs):
        for cm, refs in self._each(in_refs, out_refs, sem_refs):
            cm.start(*refs)

    def finish(self, in_refs, out_refs, sem_refs):
        for cm, refs in self._each(in_refs, out_refs, sem_refs):
            cm.finish(*refs)

    @property
    def has_middle(self):
        return any(cm.middle is not None for cm in self.comms)

    def middle(self, in_refs, out_refs, sem_refs):
        for cm, refs in self._each(in_refs, out_refs, sem_refs):
            if cm.middle is not None:
                cm.middle(*refs)


def _matmul(name, groups, m, n, k, tm, tn, tk, extras, outs, epilogue, comm=(), j_outer=False):
    assert m % tm == 0 and n % tn == 0 and k % tk == 0, (name, m, n, k, tm, tn, tk)
    nk = k // tk
    terms = [t for g in groups for t in g]
    operands, in_specs, block_bytes = [], [], 0

    def spec(blk, imap):
        return pl.BlockSpec(blk, (lambda g0, g1, kk: imap(g1, g0, kk)) if j_outer else imap)

    for a, b, mode in terms:
        assert a.shape == ((k, m) if mode == "tn" else (m, k)), (name, a.shape, mode)
        assert b.shape == ((n, k) if mode == "nt" else (k, n)), (name, b.shape, mode)
        if mode == "tn":
            a_blk, a_map = (tk, tm), (lambda i, j, kk: (kk, i))
        else:
            a_blk, a_map = (tm, tk), (lambda i, j, kk: (i, kk))
        if mode == "nt":
            b_blk, b_map = (tn, tk), (lambda i, j, kk: (j, kk))
        else:
            b_blk, b_map = (tk, tn), (lambda i, j, kk: (kk, j))
        operands += [a, b]
        in_specs += [spec(a_blk, a_map), spec(b_blk, b_map)]
        block_bytes += _nbytes(a_blk, a.dtype) + _nbytes(b_blk, b.dtype)
    for arr, blk, imap in extras:
        operands.append(arr)
        in_specs.append(spec(blk, lambda i, j, kk, imap=imap: imap(i, j)))
        block_bytes += _nbytes(blk, arr.dtype)
    out_shape, out_specs = [], []
    for shape, dtype, blk, imap in outs:
        out_shape.append(jax.ShapeDtypeStruct(shape, dtype))
        out_specs.append(spec(blk, lambda i, j, kk, imap=imap: imap(i, j)))
        block_bytes += _nbytes(blk, dtype)
    n_terms, n_extra, n_out, n_groups = len(terms), len(extras), len(outs), len(groups)
    scratch = [pltpu.VMEM((tm, tn), _F32) for _ in range(n_groups)] if nk > 1 else []
    ca = _CommArgs(list(comm), len(operands), n_out)
    n_cin, n_cout, n_acc = len(ca.operands), len(ca.out_shape), len(scratch)
    tiles = (m // tm, n // tn)
    grid = (tiles[1], tiles[0], nk) if j_outer else (tiles[0], tiles[1], nk)

    def body(*refs):
        refs = list(refs)
        term_refs = [refs.pop(0) for _ in range(2 * n_terms)]
        extra_refs = [refs.pop(0) for _ in range(n_extra)]
        cin_refs = [refs.pop(0) for _ in range(n_cin)]
        out_refs = [refs.pop(0) for _ in range(n_out)]
        cout_refs = [refs.pop(0) for _ in range(n_cout)]
        acc_refs = [refs.pop(0) for _ in range(n_acc)]
        sem_refs = refs
        g0, g1, kk = pl.program_id(0), pl.program_id(1), pl.program_id(2)
        first = jnp.logical_and(g0 == 0, g1 == 0)
        if comm:
            @pl.when(jnp.logical_and(first, kk == 0))
            def _():
                ca.start(cin_refs, cout_refs, sem_refs)
        if ca.has_middle:
            step = (g0 * grid[1] + g1) * nk + kk

            @pl.when(step == _middle_step(grid[0] * grid[1] * nk))
            def _():
                ca.middle(cin_refs, cout_refs, sem_refs)
        partial, t = [], 0
        for g in groups:
            s = None
            for _, _, mode in g:
                d = _dot(term_refs[2 * t][...], term_refs[2 * t + 1][...], mode)
                s = d if s is None else s + d
                t += 1
            partial.append(s)
        if nk == 1:
            epilogue(partial, extra_refs, out_refs, first)
        else:
            for acc, p in zip(acc_refs, partial):
                _accumulate(acc, p, kk == 0)

            @pl.when(kk == nk - 1)
            def _():
                epilogue([acc[...] for acc in acc_refs], extra_refs, out_refs, first)
        if comm:
            @pl.when(jnp.logical_and(jnp.logical_and(g0 == grid[0] - 1, g1 == grid[1] - 1), kk == nk - 1))
            def _():
                ca.finish(cin_refs, cout_refs, sem_refs)

    res = pl.pallas_call(
        body, name=name, grid=grid,
        in_specs=in_specs + [_ANY] * n_cin, out_specs=out_specs + [_ANY] * n_cout,
        out_shape=out_shape + ca.out_shape, scratch_shapes=scratch + ca.sems, input_output_aliases=ca.aliases,
        compiler_params=pltpu.CompilerParams(
            dimension_semantics=("arbitrary", "arbitrary", "arbitrary"),
            vmem_limit_bytes=_vmem_limit(block_bytes, n_groups * tm * tn * 4 if nk > 1 else 0)),
    )(*operands, *ca.operands)
    return list(res[:n_out]) + list(res[n_out:])


def _store_epilogue(accs, extra_refs, out_refs, first):
    for acc, ref in zip(accs, out_refs):
        ref[...] = acc.astype(ref.dtype)


def _tile_ij(i, j):
    return (i, j)


def _row_i(i, j):
    return (i, 0)


def _whole(i, j):
    return (0, 0)


def _mean(v):
    return jnp.mean(v, axis=-1, keepdims=True)


def _ln_fwd(r, g, b):
    xc = r - _mean(r)
    rstd = lax.rsqrt(_mean(xc * xc) + LN_EPS)
    xhat = xc * rstd
    return xhat * g + b, xhat, rstd


def _ln_bwd(dy, xhat, rstd, g):
    dxh = dy * g
    dr = rstd * (dxh - _mean(dxh) - xhat * _mean(dxh * xhat))
    return dr, jnp.sum(dy * xhat, axis=0, keepdims=True), jnp.sum(dy, axis=0, keepdims=True)


def _rms_fwd(a, g):
    rstd = lax.rsqrt(_mean(a * a) + RMS_EPS)
    return a * rstd * g


def _rms_bwd(dm, a, g):
    rstd = lax.rsqrt(_mean(a * a) + RMS_EPS)
    nhat = a * rstd
    dn = dm * g
    da = rstd * (dn - nhat * _mean(dn * nhat))
    return da, jnp.sum(dm * nhat, axis=0, keepdims=True)


def _lane(shape):
    return lax.broadcasted_iota(jnp.int32, shape, 1)


def _row(shape):
    return lax.broadcasted_iota(jnp.int32, shape, 0)


def _rope_tables(pos, invf):
    ang = pos.astype(_F32) * invf
    lane = _lane(ang.shape)
    in_rot = (lane % HEAD_DIM) < ROT_DIM
    first = (lane % ROT_DIM) < ROT_DIM // 2
    cos = jnp.where(in_rot, jnp.cos(ang), 1.0)
    sin = jnp.sin(ang)
    sgn = jnp.where(in_rot, jnp.where(first, -sin, sin), 0.0)
    return cos, sgn


def _rope(t, cos, sgn, sign):
    half = ROT_DIM // 2
    first = (_lane(t.shape) % ROT_DIM) < half
    partner = jnp.where(first, pltpu.roll(t, V7X_LANES - half, 1), pltpu.roll(t, half, 1))
    return t * cos + partner * (sgn * sign)


def _dup_head(t, h):
    g = t[:, 128 * (h // 2):128 * (h // 2) + 128]
    r = pltpu.roll(g, HEAD_DIM, 1)
    lo = _lane(g.shape) < HEAD_DIM
    return jnp.where(lo, g, r) if h % 2 == 0 else jnp.where(lo, r, g)


def _fold_halves(t):
    return t + pltpu.roll(t, HEAD_DIM, 1)


def _halves(t):
    lo = _lane(t.shape) < HEAD_DIM
    zero = jnp.zeros_like(t)
    return jnp.where(lo, t, zero), jnp.where(lo, zero, t)


def _band_mask(n_heads, n_keys, first_block):
    shape = (n_heads * WINDOW, n_keys)
    i = jnp.bitwise_and(_row(shape), WINDOW - 1)
    j = _lane(shape)
    valid = jnp.logical_and(j >= i + 1, j <= i + WINDOW)
    if first_block is not None:
        valid = jnp.logical_and(valid, jnp.logical_or(j >= WINDOW, jnp.logical_not(first_block)))
    return valid


def _stack_heads(pairs):
    return jnp.concatenate([half for t in pairs for half in _halves(t.astype(_CDT))], axis=0)


def _unstack_heads(t, n_pairs):
    lo = _lane((WINDOW, 128)) < HEAD_DIM
    return [jnp.where(lo, t[2 * WINDOW * i:2 * WINDOW * i + WINDOW], t[2 * WINDOW * i + WINDOW:2 * WINDOW * (i + 1)])
            for i in range(n_pairs)]


def _per_head(values):
    n_rows = len(values) * WINDOW
    block = jnp.right_shift(_row((n_rows, 1)), WINDOW.bit_length() - 1)
    out = jnp.zeros((n_rows, 1), _F32)
    for k, v in enumerate(values):
        out = jnp.where(block == k, v, out)
    return out


def _shift_down(z, halo, k):
    out = pltpu.roll(z, k, 0)
    r = _row(z.shape)
    for t in range(k):
        out = jnp.where(r == t, halo[V7X_SUBLANES - k + t:V7X_SUBLANES - k + t + 1, :], out)
    return out


def _shift_up(z, halo, k):
    rows = z.shape[0]
    out = pltpu.roll(z, rows - k, 0)
    r = _row(z.shape)
    for t in range(k):
        out = jnp.where(r == rows - k + t, halo[t:t + 1, :], out)
    return out


class _Dims:
    def __init__(self, s, d, d_ff):
        self.s, self.d, self.d_ff = s, d, d_ff
        self.aw = d // 2
        self.cw = d - self.aw
        self.nq = self.aw // HEAD_DIM
        self.group = self.nq // N_KV_HEADS
        assert self.group % 2 == 0, "a 128-lane pair of query heads must share its kv head"
        self.inw = self.aw + 2 * KV_WIDTH + 3 * self.cw
        self.o_k = self.aw
        self.o_v = self.aw + KV_WIDTH
        self.o_cg = self.aw + 2 * KV_WIDTH
        self.o_bg = self.o_cg + self.cw
        self.o_u = self.o_bg + self.cw
        self.nb = s // WINDOW
        assert s % WINDOW == 0


def _carrying(body, n_in, n_out, n_steps, ca, n_scratch=0):
    n_cin, n_cout = len(ca.operands), len(ca.out_shape)

    def wrapped(*refs):
        refs = list(refs)
        in_refs = [refs.pop(0) for _ in range(n_in)]
        cin_refs = [refs.pop(0) for _ in range(n_cin)]
        out_refs = [refs.pop(0) for _ in range(n_out)]
        cout_refs = [refs.pop(0) for _ in range(n_cout)]
        scratch_refs = [refs.pop(0) for _ in range(n_scratch)]
        if ca.comms:
            @pl.when(pl.program_id(0) == 0)
            def _():
                ca.start(cin_refs, cout_refs, refs)
        if ca.has_middle:
            @pl.when(pl.program_id(0) == _middle_step(n_steps))
            def _():
                ca.middle(cin_refs, cout_refs, refs)
        body(*in_refs, *out_refs, *scratch_refs)
        if ca.comms:
            @pl.when(pl.program_id(0) == n_steps - 1)
            def _():
                ca.finish(cin_refs, cout_refs, refs)

    return wrapped


def _row_kernel(name, body, rows_in, vecs_in, rows_out, vecs_out, comm=()):
    s = rows_in[0].shape[0]
    tr = _pick(s, (256, 128))
    row = lambda a: pl.BlockSpec((tr, a[1] if isinstance(a, tuple) else a.shape[1]), lambda i: (i, 0))
    vec = lambda shape: pl.BlockSpec(tuple(shape), lambda i: (0, 0))
    n_in, n_out = len(rows_in) + len(vecs_in), len(rows_out) + len(vecs_out)
    ca = _CommArgs(list(comm), n_in, n_out)
    blocks = sum(_nbytes((tr, a.shape[1]), a.dtype) for a in rows_in) + sum(_nbytes((tr, sh[1]), dt) for sh, dt in rows_out)
    res = pl.pallas_call(
        _carrying(body, n_in, n_out, s // tr, ca), name=name, grid=(s // tr,),
        in_specs=[row(a) for a in rows_in] + [vec(v.shape) for v in vecs_in] + [_ANY] * len(ca.operands),
        out_specs=[row(sh) for sh, _ in rows_out] + [vec(sh) for sh, _ in vecs_out] + [_ANY] * len(ca.out_shape),
        out_shape=[jax.ShapeDtypeStruct(sh, dt) for sh, dt in list(rows_out) + list(vecs_out)] + ca.out_shape,
        scratch_shapes=ca.sems, input_output_aliases=ca.aliases,
        compiler_params=pltpu.CompilerParams(dimension_semantics=("arbitrary",), vmem_limit_bytes=_vmem_limit(blocks)),
    )(*rows_in, *vecs_in, *ca.operands)
    return list(res)


def _ln2_loss_bwd(r2, target, gain, bias, comm=()):
    s, d = r2.shape

    def body(r_ref, t_ref, g_ref, b_ref, dr_ref, drc_ref, loss_ref, dg_ref, db_ref):
        first = pl.program_id(0) == 0
        yv, xhat, rstd = _ln_fwd(r_ref[...], g_ref[...], b_ref[...])
        err = yv - t_ref[...]
        dr2, dg, db = _ln_bwd(err * (1.0 / d), xhat, rstd, g_ref[...])
        dr_ref[...] = dr2
        drc_ref[...] = dr2.astype(_CDT)
        _accumulate(loss_ref, jnp.zeros(loss_ref.shape, _F32) + 0.5 * jnp.sum(err * err) * (1.0 / d), first)
        _accumulate(dg_ref, dg, first)
        _accumulate(db_ref, db, first)

    return _row_kernel("ln2_loss_bwd", body, [r2, target], [gain, bias], [((s, d), _F32), ((s, d), _CDT)],
                       [((V7X_SUBLANES, V7X_LANES), _F32), ((1, d), _F32), ((1, d), _F32)], comm)


def _ln1_fwd_rows(r1, gain, bias, comm=()):
    s, d = r1.shape

    def body(r_ref, g_ref, b_ref, h_ref, hc_ref):
        h1, _, _ = _ln_fwd(r_ref[...], g_ref[...], b_ref[...])
        h_ref[...] = h1
        hc_ref[...] = h1.astype(_CDT)

    return _row_kernel("ln1", body, [r1], [gain, bias], [((s, d), _F32), ((s, d), _CDT)], [], comm)


def _ln1_bwd_rows(dh1, r1, gain, comm=()):
    s, d = dh1.shape

    def body(dh_ref, r_ref, g_ref, dr_ref, drc_ref, dg_ref, db_ref):
        first = pl.program_id(0) == 0
        _, xhat, rstd = _ln_fwd(r_ref[...], g_ref[...], 0.0)
        dr1, dg, db = _ln_bwd(dh_ref[...], xhat, rstd, g_ref[...])
        dr_ref[...] = dr1
        drc_ref[...] = dr1.astype(_CDT)
        _accumulate(dg_ref, dg, first)
        _accumulate(db_ref, db, first)

    return _row_kernel("ln1_bwd", body, [dh1, r1], [gain], [((s, d), _F32), ((s, d), _CDT)],
                       [((1, d), _F32), ((1, d), _F32)], comm)


def _mixer_fwd(dm, proj, rope, sinks, g_attn, g_conv, conv_w8, comm=()):
    s, d, aw, cw, nq, inw, nb = dm.s, dm.d, dm.aw, dm.cw, dm.nq, dm.inw, dm.nb

    def body(pp_ref, pc_ref, ropep_ref, ropec_ref, sinks_ref, ga_ref, gc_ref, cw_ref,
             mixed_ref, attn_ref, lse_ref, y_ref, qk_ref):
        n = pl.program_id(0)
        cos_c, sgn_c = ropec_ref[:, 0:V7X_LANES], ropec_ref[:, V7X_LANES:2 * V7X_LANES]
        cos_p, sgn_p = ropep_ref[:, 0:V7X_LANES], ropep_ref[:, V7X_LANES:2 * V7X_LANES]
        for g in range(KV_WIDTH // 128):
            qk_ref[:, aw + 128 * g:aw + 128 * g + 128] = _rope(
                pc_ref[:, dm.o_k + 128 * g:dm.o_k + 128 * g + 128], cos_c, sgn_c, 1.0).astype(qk_ref.dtype)
        for j in range(nq // 2):
            qk_ref[:, 128 * j:128 * j + 128] = _rope(pc_ref[:, 128 * j:128 * j + 128], cos_c, sgn_c, 1.0).astype(qk_ref.dtype)
        k_prev = jnp.concatenate([_rope(pp_ref[:, dm.o_k + 128 * g:dm.o_k + 128 * g + 128], cos_p, sgn_p, 1.0)
                                  for g in range(KV_WIDTH // 128)], axis=1)
        kk = jnp.concatenate([k_prev, qk_ref[:, aw:aw + KV_WIDTH].astype(_F32)], axis=0)
        vv = jnp.concatenate([pp_ref[:, dm.o_v:dm.o_v + KV_WIDTH], pc_ref[:, dm.o_v:dm.o_v + KV_WIDTH]], axis=0)
        group, pairs = dm.group, dm.group // 2
        valid = _band_mask(group, 2 * WINDOW, n == 0)
        for h in range(N_KV_HEADS):
            k2, v2 = _dup_head(kk, h).astype(_CDT), _dup_head(vv, h).astype(_CDT)
            q4 = _stack_heads([qk_ref[:, 128 * j:128 * j + 128] for j in range(pairs * h, pairs * (h + 1))])
            sc = jnp.where(valid, _dot(q4, k2, "nt") * ATTN_SCALE, MASKED)
            sink = _per_head([sinks_ref[0, group * h + r] for r in range(group)])
            mx = jnp.maximum(jnp.max(sc, axis=1, keepdims=True), sink)
            p = jnp.exp(sc - mx)
            den = jnp.sum(p, axis=1, keepdims=True) + jnp.exp(sink - mx)
            out = _unstack_heads(_dot(p / den, v2, "nn"), pairs)
            lse = mx + jnp.log(den)
            for r in range(group):
                lse_ref[:, group * h + r:group * h + r + 1] = lse[WINDOW * r:WINDOW * (r + 1)]
            for i in range(pairs):
                j = pairs * h + i
                attn_ref[:, 128 * j:128 * j + 128] = out[i]
        mixed_ref[:, 0:aw] = _rms_fwd(attn_ref[...], ga_ref[...]).astype(mixed_ref.dtype)

        z = pc_ref[:, dm.o_cg:dm.o_cg + cw] * pc_ref[:, dm.o_u:dm.o_u + cw]
        top = WINDOW - V7X_SUBLANES
        halo = pp_ref[top:WINDOW, dm.o_cg:dm.o_cg + cw] * pp_ref[top:WINDOW, dm.o_u:dm.o_u + cw]
        halo = jnp.where(n == 0, jnp.zeros_like(halo), halo)
        y = cw_ref[0:1, :] * _shift_down(z, halo, 2) + cw_ref[1:2, :] * _shift_down(z, halo, 1) + cw_ref[2:3, :] * z
        y_ref[...] = y
        conv = pc_ref[:, dm.o_bg:dm.o_bg + cw] * y
        mixed_ref[:, aw:d] = _rms_fwd(conv, gc_ref[...]).astype(mixed_ref.dtype)

    prev = lambda n: (jnp.maximum(n - 1, 0), 0)
    cur = lambda n: (n, 0)
    fixed = lambda n: (0, 0)
    blocks = 2 * WINDOW * inw * 4 + WINDOW * (d * 2 + aw * 4 + cw * 4 + nq * 4)
    ca = _CommArgs(list(comm), 8, 5)
    return pl.pallas_call(
        _carrying(body, 8, 5, nb, ca), name="mixer_fwd", grid=(nb,),
        in_specs=[pl.BlockSpec((WINDOW, inw), prev), pl.BlockSpec((WINDOW, inw), cur),
                  pl.BlockSpec((WINDOW, 2 * V7X_LANES), prev), pl.BlockSpec((WINDOW, 2 * V7X_LANES), cur),
                  pl.BlockSpec(memory_space=pltpu.SMEM),
                  pl.BlockSpec((1, aw), fixed), pl.BlockSpec((1, cw), fixed), pl.BlockSpec((V7X_SUBLANES, cw), fixed)]
        + [_ANY] * len(ca.operands),
        out_specs=[pl.BlockSpec((WINDOW, d), cur), pl.BlockSpec((WINDOW, aw), cur),
                   pl.BlockSpec((WINDOW, nq), cur), pl.BlockSpec((WINDOW, cw), cur),
                   pl.BlockSpec((WINDOW, aw + KV_WIDTH), cur)] + [_ANY] * len(ca.out_shape),
        out_shape=[jax.ShapeDtypeStruct((s, d), _CDT), jax.ShapeDtypeStruct((s, aw), _F32),
                   jax.ShapeDtypeStruct((s, nq), _F32), jax.ShapeDtypeStruct((s, cw), _F32),
                   jax.ShapeDtypeStruct((s, aw + KV_WIDTH), _CDT)] + ca.out_shape,
        scratch_shapes=ca.sems, input_output_aliases=ca.aliases,
        compiler_params=pltpu.CompilerParams(dimension_semantics=("arbitrary",), vmem_limit_bytes=_vmem_limit(blocks)),
    )(proj, proj, rope, rope, sinks, g_attn, g_conv, conv_w8, *ca.operands)


def _patch_columns(name, a, part, offset):
    s, pw = part.shape
    assert offset % pw == 0 and pw % V7X_LANES == 0
    tr = _pick(s, (512, 256, 128))

    def body(a_ref, p_ref, o_ref):
        del a_ref
        o_ref[...] = p_ref[...]

    return pl.pallas_call(
        body, name=name, grid=(s // tr,),
        in_specs=[_ANY, pl.BlockSpec((tr, pw), lambda i: (i, 0))],
        out_specs=pl.BlockSpec((tr, pw), lambda i: (i, offset // pw)),
        out_shape=jax.ShapeDtypeStruct(a.shape, a.dtype), input_output_aliases={0: 0},
        compiler_params=pltpu.CompilerParams(dimension_semantics=("arbitrary",)),
    )(a, part)


def _mixer_bwd(dm, proj, rope, sinks, g_attn, g_conv, conv_w8, dmixed, attn, lse, y, qk, comm=()):
    s, d, aw, cw, nq, inw, nb = dm.s, dm.d, dm.aw, dm.cw, dm.nq, dm.inw, dm.nb

    def body(pp_ref, pc_ref, pn_ref, ropep_ref, ropec_ref, dmc_ref, dmn_ref, ac_ref,
             lsec_ref, yc_ref, yn_ref, qkp_ref, qkc_ref, sinks_ref, ga_ref, gc_ref, cw_ref,
             dproj_ref, dkv_ref, dga_ref, dgc_ref, dsinks_ref, dcw_ref, dk_carry, dv_carry):
        n = pl.program_id(0)
        first = n == 0
        live = n < nb
        has_next = n < nb - 1
        cos_p, sgn_p = ropep_ref[:, 0:V7X_LANES], ropep_ref[:, V7X_LANES:2 * V7X_LANES]
        cos_c, sgn_c = ropec_ref[:, 0:V7X_LANES], ropec_ref[:, V7X_LANES:2 * V7X_LANES]

        @pl.when(first)
        def _():
            dk_carry[...] = jnp.zeros(dk_carry.shape, _F32)
            dv_carry[...] = jnp.zeros(dv_carry.shape, _F32)

        def write_kv(dk2, dv2, cos, sgn):
            lo = _lane((WINDOW, 128)) < HEAD_DIM
            for g in range(KV_WIDTH // 128):
                dk = jnp.where(lo, _fold_halves(dk2[2 * g]), _fold_halves(dk2[2 * g + 1]))
                dv = jnp.where(lo, _fold_halves(dv2[2 * g]), _fold_halves(dv2[2 * g + 1]))
                dkv_ref[:, 128 * g:128 * g + 128] = _rope(dk, cos, sgn, -1.0).astype(dkv_ref.dtype)
                dkv_ref[:, KV_WIDTH + 128 * g:KV_WIDTH + 128 * g + 128] = dv.astype(dkv_ref.dtype)

        @pl.when(jnp.logical_not(live))
        def _():
            write_kv([dk_carry[h] for h in range(N_KV_HEADS)], [dv_carry[h] for h in range(N_KV_HEADS)], cos_c, sgn_c)

        @pl.when(live)
        def _():
            block_step(pp_ref, pc_ref, pn_ref, dmc_ref, dmn_ref, ac_ref, lsec_ref, yc_ref, yn_ref, qkp_ref, qkc_ref,
                       sinks_ref, ga_ref, gc_ref, cw_ref, dproj_ref, dga_ref, dgc_ref, dsinks_ref, dcw_ref, dk_carry,
                       dv_carry, first, has_next, cos_p, sgn_p, cos_c, sgn_c, write_kv)

    def block_step(pp_ref, pc_ref, pn_ref, dmc_ref, dmn_ref, ac_ref, lsec_ref, yc_ref, yn_ref, qkp_ref, qkc_ref,
                   sinks_ref, ga_ref, gc_ref, cw_ref, dproj_ref, dga_ref, dgc_ref, dsinks_ref, dcw_ref, dk_carry,
                   dv_carry, first, has_next, cos_p, sgn_p, cos_c, sgn_c, write_kv):
        da_c, dga = _rms_bwd(dmc_ref[:, 0:aw], ac_ref[...], ga_ref[...])
        _accumulate(dga_ref, dga, first)
        kk = jnp.concatenate([qkp_ref[:, aw:aw + KV_WIDTH], qkc_ref[:, aw:aw + KV_WIDTH]], axis=0).astype(_F32)
        vv = jnp.concatenate([pp_ref[:, dm.o_v:dm.o_v + KV_WIDTH], pc_ref[:, dm.o_v:dm.o_v + KV_WIDTH]], axis=0)
        group, pairs = dm.group, dm.group // 2
        valid_c = _band_mask(group, 2 * WINDOW, first)
        dk_prev, dv_prev = [], []

        def stacked(q_ref, da, o_ref, lse_ref_, h):
            cols = [slice(128 * j, 128 * j + 128) for j in range(pairs * h, pairs * (h + 1))]
            q4 = _stack_heads([q_ref[:, c] for c in cols])
            do4 = _stack_heads([da[:, c] for c in cols])
            lo = _lane((WINDOW, 128)) < HEAD_DIM
            deltas = []
            for c in cols:
                prod = o_ref[:, c] * da[:, c]
                deltas += [jnp.sum(jnp.where(lo, prod, 0.0), axis=1, keepdims=True),
                           jnp.sum(jnp.where(lo, 0.0, prod), axis=1, keepdims=True)]
            lse4 = jnp.concatenate([lse_ref_[:, group * h + r:group * h + r + 1] for r in range(group)], axis=0)
            return q4, do4, lse4, deltas

        def scores_bwd(q4, do4, lse4, delta4, keys, vals, valid):
            sc = _dot(q4, keys, "nt") * ATTN_SCALE
            p = jnp.exp(jnp.where(valid, sc - lse4, MASKED))
            return p.astype(_CDT), (p * (_dot(do4, vals, "nt") - delta4) * ATTN_SCALE).astype(_CDT)

        for h in range(N_KV_HEADS):
            k2, v2 = _dup_head(kk, h).astype(_CDT), _dup_head(vv, h).astype(_CDT)
            q4, do4, lse4, deltas = stacked(qkc_ref, da_c, ac_ref, lsec_ref, h)
            delta4 = jnp.concatenate(deltas, axis=0)
            p, ds = scores_bwd(q4, do4, lse4, delta4, k2, v2, valid_c)
            for i, dq in enumerate(_unstack_heads(_dot(ds, k2, "nn"), pairs)):
                j = pairs * h + i
                dproj_ref[:, 128 * j:128 * j + 128] = _rope(dq, cos_c, sgn_c, -1.0).astype(dproj_ref.dtype)
            dk = _dot(ds, q4, "tn")
            dv = _dot(p, do4, "tn")
            dk_prev.append(dk_carry[h] + dk[0:WINDOW])
            dv_prev.append(dv_carry[h] + dv[0:WINDOW])
            dk_carry[h] = dk[WINDOW:2 * WINDOW]
            dv_carry[h] = dv[WINDOW:2 * WINDOW]
            heads = slice(group * h, group * (h + 1))
            sink_row, delta_heads = jnp.zeros((1, group), _F32), jnp.zeros((WINDOW, group), _F32)
            for r in range(group):
                sink_row = jnp.where(_lane((1, group)) == r, sinks_ref[0, group * h + r], sink_row)
                delta_heads = jnp.where(_lane((WINDOW, group)) == r, deltas[r], delta_heads)
            loss_sink = jnp.exp(sink_row - lsec_ref[:, heads]) * delta_heads
            _accumulate(dsinks_ref.at[:, heads], -jnp.sum(loss_sink, axis=0, keepdims=True), first)
        write_kv(dk_prev, dv_prev, cos_p, sgn_p)

        bg = pc_ref[:, dm.o_bg:dm.o_bg + cw]
        yc = yc_ref[...]
        dconv, dgc = _rms_bwd(dmc_ref[:, aw:d], bg * yc, gc_ref[...])
        _accumulate(dgc_ref, dgc, first)
        dproj_ref[:, dm.o_bg:dm.o_bg + cw] = (dconv * yc).astype(dproj_ref.dtype)
        dy = dconv * bg
        bg_n = pn_ref[:, dm.o_bg:dm.o_bg + cw]
        dconv_n, _ = _rms_bwd(dmn_ref[:, aw:d], bg_n * yn_ref[...], gc_ref[...])
        halo = jnp.where(has_next, dconv_n * bg_n, 0.0)
        dy1 = _shift_up(dy, halo, 1)
        dy2 = _shift_up(dy, halo, 2)
        dz = cw_ref[2:3, :] * dy + cw_ref[1:2, :] * dy1 + cw_ref[0:1, :] * dy2
        cg = pc_ref[:, dm.o_cg:dm.o_cg + cw]
        u = pc_ref[:, dm.o_u:dm.o_u + cw]
        dproj_ref[:, dm.o_cg:dm.o_cg + cw] = (dz * u).astype(dproj_ref.dtype)
        dproj_ref[:, dm.o_u:dm.o_u + cw] = (dz * cg).astype(dproj_ref.dtype)
        z = cg * u
        dcw = jnp.concatenate(
            [jnp.sum(z * t, axis=0, keepdims=True) for t in (dy2, dy1, dy)]
            + [jnp.zeros((V7X_SUBLANES - 3, cw), _F32)], axis=0)
        _accumulate(dcw_ref, dcw, first)

    at = lambda n: jnp.minimum(n, nb - 1)
    prev = lambda n: (jnp.maximum(at(n) - 1, 0), 0)
    cur = lambda n: (at(n), 0)
    done = lambda n: (jnp.maximum(n - 1, 0), 0)
    nxt8 = lambda n: (jnp.minimum((at(n) + 1) * (WINDOW // V7X_SUBLANES), s // V7X_SUBLANES - 1), 0)
    fixed = lambda n: (0, 0)
    blocks = WINDOW * (2 * inw * 4 + d * 4 + aw * 4 + cw * 4 + inw * 2 + 2 * KV_WIDTH * 2)
    carry = [pltpu.VMEM((N_KV_HEADS, WINDOW, 128), _F32), pltpu.VMEM((N_KV_HEADS, WINDOW, 128), _F32)]
    n_in, n_out = 17, 6
    ca = _CommArgs(list(comm), n_in, n_out)
    return pl.pallas_call(
        _carrying(body, n_in, n_out, nb + 1, ca, n_scratch=len(carry)), name="mixer_bwd", grid=(nb + 1,),
        in_specs=[pl.BlockSpec((WINDOW, inw), prev), pl.BlockSpec((WINDOW, inw), cur), pl.BlockSpec((V7X_SUBLANES, inw), nxt8),
                  pl.BlockSpec((WINDOW, 2 * V7X_LANES), prev), pl.BlockSpec((WINDOW, 2 * V7X_LANES), cur),
                  pl.BlockSpec((WINDOW, d), cur), pl.BlockSpec((V7X_SUBLANES, d), nxt8),
                  pl.BlockSpec((WINDOW, aw), cur), pl.BlockSpec((WINDOW, nq), cur),
                  pl.BlockSpec((WINDOW, cw), cur), pl.BlockSpec((V7X_SUBLANES, cw), nxt8),
                  pl.BlockSpec((WINDOW, aw + KV_WIDTH), prev), pl.BlockSpec((WINDOW, aw + KV_WIDTH), cur),
                  pl.BlockSpec(memory_space=pltpu.SMEM),
                  pl.BlockSpec((1, aw), fixed), pl.BlockSpec((1, cw), fixed), pl.BlockSpec((V7X_SUBLANES, cw), fixed)]
        + [_ANY] * len(ca.operands),
        out_specs=[pl.BlockSpec((WINDOW, inw), cur), pl.BlockSpec((WINDOW, 2 * KV_WIDTH), done),
                   pl.BlockSpec((1, aw), fixed), pl.BlockSpec((1, cw), fixed),
                   pl.BlockSpec((1, nq), fixed), pl.BlockSpec((V7X_SUBLANES, cw), fixed)] + [_ANY] * len(ca.out_shape),
        out_shape=[jax.ShapeDtypeStruct((s, inw), _CDT), jax.ShapeDtypeStruct((s, 2 * KV_WIDTH), _CDT),
                   jax.ShapeDtypeStruct((1, aw), _F32), jax.ShapeDtypeStruct((1, cw), _F32),
                   jax.ShapeDtypeStruct((1, nq), _F32), jax.ShapeDtypeStruct((V7X_SUBLANES, cw), _F32)] + ca.out_shape,
        scratch_shapes=carry + ca.sems, input_output_aliases=ca.aliases,
        compiler_params=pltpu.CompilerParams(dimension_semantics=("arbitrary",), vmem_limit_bytes=_vmem_limit(blocks)),
    )(proj, proj, proj, rope, rope, dmixed, dmixed, attn, lse, y, y, qk, qk, sinks, g_attn, g_conv, conv_w8, *ca.operands)


def _position():
    return lax.axis_index("x"), lax.axis_index("y"), lax.axis_index("c")


def _linear(px, py, pc):
    return 4 * px + 2 * py + pc


def _comm_kernel(name, comm):
    ca = _CommArgs(list(comm), 0, 0)
    n_cin, n_cout = len(ca.operands), len(ca.out_shape)

    def body(*refs):
        cin, cout, sems = refs[:n_cin], refs[n_cin:n_cin + n_cout], refs[n_cin + n_cout:]
        ca.start(cin, cout, sems)
        ca.middle(cin, cout, sems)
        ca.finish(cin, cout, sems)

    return pl.pallas_call(
        body, name=name, out_shape=ca.out_shape, in_specs=[_ANY] * n_cin, out_specs=[_ANY] * n_cout,
        scratch_shapes=ca.sems, input_output_aliases=ca.aliases,
    )(*ca.operands)


def _gather_op(units):
    n = len(units)
    inputs, outputs, aliases = [], [], {}
    for shard, _, _, _ in units:
        inputs.append(shard)
        outputs.append(jax.ShapeDtypeStruct((N_DEV * shard.shape[0], shard.shape[1]), shard.dtype))
    for u, (_, buf, _, _) in enumerate(units):
        if buf is not None:
            aliases[len(inputs)] = u
            inputs.append(buf)

    def plan(ins, outs, sems, north):
        send_sems, recv_sems, local_sems = sems
        x, y, c = _position()
        me, sibling = (x, y, c), (x, y, 1 - c)
        xn, yn, dg = (1 - x, y), (x, 1 - y), (1 - x, 1 - y)
        via, to, k_via, k_other = (yn, xn, 2, 1) if north else (xn, yn, 1, 2)

        def rows(u, px, py, pc):
            shard, _, r0, r1 = units[u]
            return outs[u].at[pl.ds(pl.multiple_of(_linear(px, py, pc) * shard.shape[0] + r0, 16), r1 - r0), :]

        def own(u):
            _, _, r0, r1 = units[u]
            return ins[u].at[pl.ds(r0, r1 - r0), :]

        def copy(u, k, block, to_, src=None):
            return pltpu.make_async_remote_copy(
                src_ref=rows(u, *block) if src is None else src, dst_ref=rows(u, *block),
                send_sem=send_sems.at[u, k], recv_sem=recv_sems.at[u, k], device_id=to_, device_id_type=_MESH)

        us = range(n)
        return dict(
            mine=[pltpu.make_async_copy(own(u), rows(u, *me), local_sems.at[u]) for u in us],
            first=[cp for u in us for cp in (copy(u, 0, me, sibling, src=own(u)), copy(u, 1, me, (*xn, c), src=own(u)),
                                             copy(u, 2, me, (*yn, c), src=own(u)))],
            relay=[copy(u, 3, (*via, c), (*to, c)) for u in us],
            arrived={1: [copy(u, 1, (*xn, c), me) for u in us], 2: [copy(u, 2, (*yn, c), me) for u in us],
                     3: [copy(u, 3, (*dg, c), me) for u in us]},
            passed={1: [copy(u, 4, (*xn, c), sibling) for u in us], 2: [copy(u, 5, (*yn, c), sibling) for u in us],
                    3: [copy(u, 6, (*dg, c), sibling) for u in us]},
            rest=[cp for u in us for cp in (copy(u, 0, sibling, me), copy(u, 4, (*xn, 1 - c), me),
                                            copy(u, 5, (*yn, 1 - c), me), copy(u, 6, (*dg, 1 - c), me))],
            k_via=k_via, k_other=k_other)

    def land(p, k):
        for arrived, onward in zip(p["arrived"][k], p["passed"][k]):
            arrived.wait_recv()
            onward.start()

    def by_core(fn):
        c = lax.axis_index("c")
        for north in (True, False):
            pl.when(c == (1 if north else 0))(functools.partial(fn, north))

    def start(ins, outs, sems):
        p = plan(ins, outs, sems, True)
        for cp in p["mine"] + p["first"]:
            cp.start()

    def middle(ins, outs, sems):
        def go(north):
            p = plan(ins, outs, sems, north)
            land(p, p["k_via"])
            for cp in p["relay"]:
                cp.start()
            land(p, p["k_other"])
        by_core(go)

    def finish(ins, outs, sems):
        def go(north):
            p = plan(ins, outs, sems, north)
            land(p, 3)
            for cp in p["rest"]:
                cp.wait_recv()
            for cp in p["first"] + p["relay"] + [cp for k in (1, 2, 3) for cp in p["passed"][k]]:
                cp.wait_send()
            for cp in p["mine"]:
                cp.wait()
        by_core(go)

    sems = [pltpu.SemaphoreType.DMA((n, 7)), pltpu.SemaphoreType.DMA((n, 7)), pltpu.SemaphoreType.DMA((n,))]
    return _Comm(inputs, outputs, aliases, sems, start, finish, middle)


def _peers(x, y, c):
    out = []
    for k in range(1, N_DEV):
        fx, fy, fc = (k >> 2) & 1, (k >> 1) & 1, k & 1
        out.append((1 - x if fx else x, 1 - y if fy else y, 1 - c if fc else c))
    return out


def _exchange_op(partials):
    n = len(partials)
    outputs = [jax.ShapeDtypeStruct((4, p.shape[0] // N_DEV, p.shape[1]), p.dtype) for p in partials]

    def plan(ins, outs, sems):
        send_sems, recv_sems = sems
        x, y, c = _position()
        out = []
        for a in range(n):
            r = outs[a].shape[1]
            for ch in range(4):
                out.append(pltpu.make_async_remote_copy(
                    src_ref=ins[a].at[pl.ds(pl.multiple_of((2 * ch + 1 - c) * r, 16), r), :], dst_ref=outs[a].at[ch],
                    send_sem=send_sems.at[a, ch], recv_sem=recv_sems.at[a, ch], device_id=(x, y, 1 - c),
                    device_id_type=_MESH))
        return out

    def start(ins, outs, sems):
        for cp in plan(ins, outs, sems):
            cp.start()

    def finish(ins, outs, sems):
        copies = plan(ins, outs, sems)
        for cp in copies:
            cp.wait_recv()
        for cp in copies:
            cp.wait_send()

    sems = [pltpu.SemaphoreType.DMA((n, 4)), pltpu.SemaphoreType.DMA((n, 4))]
    return _Comm(list(partials), outputs, {}, sems, start, finish)


def _chip_send_op(units):
    n = len(units)
    inputs, outputs, aliases = [], [], {}
    for q, _, _, _ in units:
        inputs.append(q)
        outputs.append(jax.ShapeDtypeStruct(q.shape, q.dtype))
    for u, (_, buf, _, _) in enumerate(units):
        if buf is not None:
            aliases[len(inputs)] = u
            inputs.append(buf)

    def plan(ins, outs, sems):
        send_sems, recv_sems, local_sems = sems
        x, y, c = _position()
        my_chip = 2 * x + y
        chips = [(1 - x, y), (x, 1 - y), (1 - x, 1 - y)]
        mine, sends, arrivals = [], [], []
        for u, (_, _, r0, r1) in enumerate(units):
            span = pl.ds(r0, r1 - r0)
            mine.append(pltpu.make_async_copy(ins[u].at[my_chip, span, :], outs[u].at[my_chip, span, :], local_sems.at[u]))
            for k, (px, py) in enumerate(chips):
                sends.append(pltpu.make_async_remote_copy(
                    src_ref=ins[u].at[2 * px + py, span, :], dst_ref=outs[u].at[my_chip, span, :],
                    send_sem=send_sems.at[u, k], recv_sem=recv_sems.at[u, k], device_id=(px, py, c), device_id_type=_MESH))
                arrivals.append(pltpu.make_async_remote_copy(
                    src_ref=ins[u].at[my_chip, span, :], dst_ref=outs[u].at[2 * px + py, span, :],
                    send_sem=send_sems.at[u, k], recv_sem=recv_sems.at[u, k], device_id=(px, py, c), device_id_type=_MESH))
        return mine, sends, arrivals

    def start(ins, outs, sems):
        mine, sends, _ = plan(ins, outs, sems)
        for cp in mine + sends:
            cp.start()

    def finish(ins, outs, sems):
        mine, sends, arrivals = plan(ins, outs, sems)
        for cp in arrivals:
            cp.wait_recv()
        for cp in sends:
            cp.wait_send()
        for cp in mine:
            cp.wait()

    sems = [pltpu.SemaphoreType.DMA((n, 3)), pltpu.SemaphoreType.DMA((n, 3)), pltpu.SemaphoreType.DMA((n,))]
    return _Comm(inputs, outputs, aliases, sems, start, finish)


def _pair_sum(name, partial, received):
    _, rows, cols = received.shape
    tr = _pick(rows, (352, 288, 256, 128, 64, 32, 16))
    p4 = partial.reshape(4, 2, rows, cols)
    kind = jnp.reshape(lax.axis_index("c"), (1,)).astype(jnp.int32)

    def body(kind_ref, p_ref, r_ref, o_ref):
        o_ref[0] = (p_ref[0, 0].astype(_F32) + r_ref[0].astype(_F32)).astype(o_ref.dtype)

    return pl.pallas_call(
        body, name=name,
        grid_spec=pltpu.PrefetchScalarGridSpec(
            num_scalar_prefetch=1, grid=(4, rows // tr),
            in_specs=[pl.BlockSpec((1, 1, tr, cols), lambda ch, i, kind_ref: (ch, kind_ref[0], i, 0)),
                      pl.BlockSpec((1, tr, cols), lambda ch, i, kind_ref: (ch, i, 0))],
            out_specs=pl.BlockSpec((1, tr, cols), lambda ch, i, kind_ref: (ch, i, 0))),
        out_shape=jax.ShapeDtypeStruct(received.shape, received.dtype),
        compiler_params=pltpu.CompilerParams(dimension_semantics=("arbitrary", "arbitrary")),
    )(kind, p4, received)


def _all_reduce_small(name, v):
    rows = v.shape[0]

    def body(v_ref, out_ref, land_ref, send_sems, recv_sems):
        x, y, c = _position()
        me = _linear(x, y, c)
        peers = _peers(x, y, c)
        land_ref[me] = v_ref[...]
        sends = [pltpu.make_async_remote_copy(
            src_ref=v_ref, dst_ref=land_ref.at[me], send_sem=send_sems.at[k], recv_sem=recv_sems.at[k],
            device_id=peer, device_id_type=_MESH) for k, peer in enumerate(peers)]
        for cp in sends:
            cp.start()
        for k, peer in enumerate(peers):
            pltpu.make_async_remote_copy(
                src_ref=v_ref, dst_ref=land_ref.at[_linear(*peer)], send_sem=send_sems.at[k], recv_sem=recv_sems.at[k],
                device_id=peer, device_id_type=_MESH).wait_recv()
        for cp in sends:
            cp.wait_send()
        total = land_ref[0]
        for s in range(1, N_DEV):
            total = total + land_ref[s]
        out_ref[...] = total

    return pl.pallas_call(
        body, name=name, out_shape=jax.ShapeDtypeStruct(v.shape, _F32),
        in_specs=[pl.BlockSpec(memory_space=pltpu.VMEM)], out_specs=pl.BlockSpec(memory_space=pltpu.VMEM),
        scratch_shapes=[pltpu.VMEM((N_DEV, rows, V7X_LANES), _F32), pltpu.SemaphoreType.DMA((7,)), pltpu.SemaphoreType.DMA((7,))],
    )(v)


def _adamw(name, w, slots, m, v):
    rows, cols = w.shape
    n_slots = slots.shape[0]
    tr = _pick(rows, (176, 144, 128, 64, 32, 16, 8))

    def body(w_ref, s_ref, m_ref, v_ref, g_ref, d_ref, nm_ref, nv_ref):
        g = s_ref[0].astype(_F32)
        for k in range(1, n_slots):
            g = g + s_ref[k].astype(_F32)
        nm = ADAM_B1 * m_ref[...] + (1.0 - ADAM_B1) * g
        nv = ADAM_B2 * v_ref[...] + (1.0 - ADAM_B2) * (g * g)
        m_hat = nm / (1.0 - ADAM_B1 ** ADAM_STEP)
        v_hat = nv / (1.0 - ADAM_B2 ** ADAM_STEP)
        g_ref[...] = g
        d_ref[...] = -ADAM_LR * (m_hat / (jnp.sqrt(v_hat) + ADAM_EPS) + ADAM_WD * w_ref[...])
        nm_ref[...] = nm
        nv_ref[...] = nv

    spec = pl.BlockSpec((tr, cols), lambda i: (i, 0))
    blocks = 7 * tr * cols * 4 + _nbytes((n_slots, tr, cols), slots.dtype)
    return pl.pallas_call(
        body, name=name, grid=(rows // tr,),
        in_specs=[spec, pl.BlockSpec((n_slots, tr, cols), lambda i: (0, i, 0)), spec, spec], out_specs=[spec] * 4,
        out_shape=[jax.ShapeDtypeStruct((rows, cols), _F32)] * 4,
        compiler_params=pltpu.CompilerParams(dimension_semantics=("arbitrary",), vmem_limit_bytes=_vmem_limit(blocks)),
    )(w, slots, m, v)


def _pad_rows(a, rows):
    return jnp.pad(a, ((0, rows - a.shape[0]), (0, 0)))


def _pack(parts):
    rows, spans, at = [], [], 0
    for p in parts:
        p = p.reshape(-1)
        r = -(-p.shape[0] // V7X_LANES)
        rows.append(jnp.pad(p, (0, r * V7X_LANES - p.shape[0])).reshape(r, V7X_LANES))
        spans.append((at, r, p.shape[0]))
        at += r
    packed = jnp.concatenate(rows, axis=0)
    return _pad_rows(packed, -(-at // V7X_SUBLANES) * V7X_SUBLANES), spans


def _unpack(packed, spans, shapes):
    return [packed[at:at + r].reshape(-1)[:size].reshape(shape) for (at, r, size), shape in zip(spans, shapes)]


def kernel(x, positions, w_in, conv_w, sinks, g_attn, g_conv, w_out, ln1_g, ln1_b, w_gate, w_up, w_down, ln2_g, ln2_b, loss_target, m_w_in, m_conv_w, m_sinks, m_g_attn, m_g_conv, m_w_out, m_ln1_g, m_ln1_b, m_w_gate, m_w_up, m_w_down, m_ln2_g, m_ln2_b, v_w_in, v_conv_w, v_sinks, v_g_attn, v_g_conv, v_w_out, v_ln1_g, v_ln1_b, v_w_gate, v_w_up, v_w_down, v_ln2_g, v_ln2_b):
    _, s, d = x.shape
    d_ff = N_DEV * w_gate.shape[2]
    dm = _Dims(s, d, d_ff)
    aw, cw, nq, inw = dm.aw, dm.cw, dm.nq, dm.inw
    x2 = x[0]
    pos = positions[0].reshape(s, 1)
    inv_freq = ROPE_THETA ** (-jnp.arange(0, ROT_DIM, 2, dtype=_F32) / ROT_DIM)
    invf = jnp.tile(inv_freq, V7X_LANES // (ROT_DIM // 2)).reshape(1, V7X_LANES)

    conv_cols = conv_w.shape[2]
    sh_in, sh_out = w_in[0].T.astype(_CDT), w_out[0].astype(_CDT)
    sh_gate, sh_up, sh_down = w_gate[0].T.astype(_CDT), w_up[0].T.astype(_CDT), w_down[0].astype(_CDT)
    r_in, r_out, r_ff = sh_in.shape[0], sh_out.shape[0], sh_gate.shape[0]
    q_ff = r_ff // 4
    assert q_ff % 16 == 0
    def prepare_body(x_ref, pos_ref, invf_ref, xc_ref, rope_ref):
        xc_ref[...] = x_ref[...].astype(_CDT)
        cos, sgn = _rope_tables(pos_ref[...], invf_ref[...])
        rope_ref[:, 0:V7X_LANES] = cos
        rope_ref[:, V7X_LANES:2 * V7X_LANES] = sgn

    x_c, rope, w_in_t, conv_all = _row_kernel(
        "prepare_gather_w_in", prepare_body, [x2, pos], [invf], [((s, d), _CDT), ((s, 2 * V7X_LANES), _F32)], [],
        comm=[_gather_op([(sh_in, None, 0, r_in), (_pad_rows(conv_w[0], 16), None, 0, 16)])])
    conv_full = conv_all.reshape(N_DEV, 16, conv_cols)[:, :3, :].transpose(1, 0, 2).reshape(3, cw)
    conv_w8 = _pad_rows(conv_full, V7X_SUBLANES)

    tm = _pick(s, (1024, 512, 256, 128))
    tm2 = _pick(s, (2048, 1024, 512, 256, 128))
    tr = _pick(s, (512, 256, 128))
    tn_in = _pick(inw, (512, 256, 128))
    tn_ff = _pick(d_ff, (512, 256, 128))

    proj, w_out_f, w_gate_t = _matmul(
        "proj", [[(x_c, w_in_t, "nt")]], s, inw, d, tm2, tn_in, d, [],
        [((s, inw), _F32, (tm2, tn_in), _tile_ij)], _store_epilogue,
        comm=[_gather_op([(sh_out, None, 0, r_out), (sh_gate, None, 0, 2 * q_ff)])])
    mixed, attn, lse, y_conv, qk_rot, w_gate_t, w_up_t = _mixer_fwd(
        dm, proj, rope, sinks, g_attn, g_conv, conv_w8,
        comm=[_gather_op([(sh_gate, w_gate_t, 2 * q_ff, r_ff), (sh_up, None, 0, 2 * q_ff)])])

    def residual_epilogue(accs, ex, out, first):
        out[0][...] = DEEPNORM_ALPHA * ex[0][...] + accs[0]

    tn_d = _pick(d, (512,))
    r1, w_up_t = _matmul(
        "out_proj", [[(mixed, w_out_f, "nn")]], s, d, d, tm, tn_d, d, [(x2, (tm, tn_d), _tile_ij)],
        [((s, d), _F32, (tm, tn_d), _tile_ij)], residual_epilogue,
        comm=[_gather_op([(sh_up, w_up_t, 2 * q_ff, 3 * q_ff)])])
    h1, h1_c, w_up_t = _ln1_fwd_rows(r1, ln1_g, ln1_b, comm=[_gather_op([(sh_up, w_up_t, 3 * q_ff, r_ff)])])

    def swiglu_epilogue(accs, ex, out, first):
        gate_v, up_v = accs
        out[0][...] = gate_v
        out[1][...] = up_v
        out[2][...] = (gate_v * jax.nn.sigmoid(gate_v) * up_v).astype(_CDT)

    gate, up, act, w_down_f = _matmul(
        "gate_up", [[(h1_c, w_gate_t, "nt")], [(h1_c, w_up_t, "nt")]], s, d_ff, d, tm, tn_ff, d, [],
        [((s, d_ff), _F32, (tm, tn_ff), _tile_ij), ((s, d_ff), _F32, (tm, tn_ff), _tile_ij),
         ((s, d_ff), _CDT, (tm, tn_ff), _tile_ij)], swiglu_epilogue,
        comm=[_gather_op([(sh_down, None, 0, r_ff)])])

    (r2,) = _matmul("down", [[(act, w_down_f, "nn")]], s, d, d_ff, tm, tn_d, d_ff, [(h1, (tm, tn_d), _tile_ij)],
                    [((s, d), _F32, (tm, tn_d), _tile_ij)], residual_epilogue)
    dr2, dr2_c, loss_acc, d_ln2_g, d_ln2_b = _ln2_loss_bwd(r2, loss_target[0], ln2_g, ln2_b)

    def swiglu_bwd_epilogue(accs, ex, out, first):
        gate_v, up_v = ex[0][...], ex[1][...]
        sig = jax.nn.sigmoid(gate_v)
        out[0][...] = (accs[0] * up_v * (sig * (1.0 + gate_v * (1.0 - sig)))).astype(_CDT)
        out[1][...] = (accs[0] * (gate_v * sig)).astype(_CDT)

    dgate, dup = _matmul(
        "dact", [[(dr2_c, w_down_f, "nt")]], s, d_ff, d, tm2, tn_ff, d,
        [(gate, (tm2, tn_ff), _tile_ij), (up, (tm2, tn_ff), _tile_ij)],
        [((s, d_ff), _CDT, (tm2, tn_ff), _tile_ij), ((s, d_ff), _CDT, (tm2, tn_ff), _tile_ij)], swiglu_bwd_epilogue)
    def weight_grad(name, a, b, comm=()):
        rows = a.shape[1]
        tw, tn_w = _pick(rows, (512, 256, 128)), _pick(d, (1024, 512))
        return _matmul(name, [[(a, b, "tn")]], rows, d, s, tw, tn_w, s, [],
                       [((rows, d), _CDT, (tw, tn_w), _tile_ij)], _store_epilogue, comm=comm, j_outer=True)

    (dw_down,) = weight_grad("dw_down", act, dr2_c)
    dw_gate_t, x_down = weight_grad("dw_gate", dgate, h1_c, comm=[_exchange_op([dw_down])])
    q_down = _pair_sum("chip_sum_w_down", dw_down, x_down)
    dw_up_t, l_down, x_gate = weight_grad(
        "dw_up", dup, h1_c, comm=[_chip_send_op([(q_down, None, 0, 2 * q_ff)]), _exchange_op([dw_gate_t])])
    q_gate = _pair_sum("chip_sum_w_gate", dw_gate_t, x_gate)

    tn_h = _pick(d, (512,))
    dh1, l_down, l_gate, x_up = _matmul(
        "dh1", [[(dgate, w_gate_t, "nn"), (dup, w_up_t, "nn")]], s, d, d_ff, tr, tn_h, d_ff,
        [(dr2, (tr, tn_h), _tile_ij)], [((s, d), _F32, (tr, tn_h), _tile_ij)], residual_epilogue,
        comm=[_chip_send_op([(q_down, l_down, 2 * q_ff, r_ff), (q_gate, None, 0, r_ff)]), _exchange_op([dw_up_t])])
    q_up = _pair_sum("chip_sum_w_up", dw_up_t, x_up)
    dr1, dr1_c, d_ln1_g, d_ln1_b = _ln1_bwd_rows(dh1, r1, ln1_g)
    (dmixed,) = _matmul("dmixed", [[(dr1_c, w_out_f, "nt")]], s, d, d, tm2, tn_d, d, [],
                        [((s, d), _F32, (tm2, tn_d), _tile_ij)], _store_epilogue)
    (dw_out,) = weight_grad("dw_out", mixed, dr1_c)
    dproj, dkv, d_g_attn, d_g_conv, d_sinks, d_conv8, l_up, x_out = _mixer_bwd(
        dm, proj, rope, sinks, g_attn, g_conv, conv_w8, dmixed, attn, lse, y_conv, qk_rot,
        comm=[_chip_send_op([(q_up, None, 0, r_ff)]), _exchange_op([dw_out])])
    dproj = _patch_columns("dproj_kv", dproj, dkv, dm.o_k)
    q_out = _pair_sum("chip_sum_w_out", dw_out, x_out)
    dw_in_t, l_out = weight_grad("dw_in", dproj, x_c, comm=[_chip_send_op([(q_out, None, 0, r_out)])])
    (x_in,) = _comm_kernel("exchange_w_in", [_exchange_op([dw_in_t])])
    q_in = _pair_sum("chip_sum_w_in", dw_in_t, x_in)

    grad_x, l_in = _matmul("dx", [[(dproj, w_in_t, "nn")]], s, d, inw, tm, tn_d, inw,
                           [(dr1, (tm, tn_d), _tile_ij)], [((s, d), _F32, (tm, tn_d), _tile_ij)], residual_epilogue,
                           comm=[_chip_send_op([(q_in, None, 0, r_in)])])

    small_parts = [d_conv8[:3], d_sinks, d_g_attn, d_g_conv, d_ln1_g, d_ln1_b, d_ln2_g, d_ln2_b, loss_acc[0:1, 0:1]]
    packed, spans = _pack(small_parts)
    reduced = _unpack(_all_reduce_small("reduce_small", packed), spans, [p.shape for p in small_parts])
    g_conv_full, g_sinks, g_g_attn, g_g_conv, g_ln1_g, g_ln1_b, g_ln2_g, g_ln2_b, loss_sum = reduced
    me = _linear(*_position())
    g_conv_w = lax.dynamic_slice(g_conv_full, (0, me * conv_cols), (3, conv_cols))
    loss = loss_sum[0, 0]

    big = {"w_in": (w_in[0].T, l_in, m_w_in[0].T, v_w_in[0].T), "w_out": (w_out[0], l_out, m_w_out[0], v_w_out[0]),
           "w_gate": (w_gate[0].T, l_gate, m_w_gate[0].T, v_w_gate[0].T),
           "w_up": (w_up[0].T, l_up, m_w_up[0].T, v_w_up[0].T), "w_down": (w_down[0], l_down, m_w_down[0], v_w_down[0])}
    res = {nm: tuple(_adamw(f"adamw_{nm}", w, slots, m, v)) for nm, (w, slots, m, v) in big.items()}
    for nm in ("w_in", "w_gate", "w_up"):
        res[nm] = tuple(a.T for a in res[nm])
    small_names = ["conv_w", "sinks", "g_attn", "g_conv", "ln1_g", "ln1_b", "ln2_g", "ln2_b"]
    small_w = [conv_w, sinks, g_attn, g_conv, ln1_g, ln1_b, ln2_g, ln2_b]
    small_g = [g_conv_w[None], g_sinks, g_g_attn, g_g_conv, g_ln1_g, g_ln1_b, g_ln2_g, g_ln2_b]
    small_m = [m_conv_w, m_sinks, m_g_attn, m_g_conv, m_ln1_g, m_ln1_b, m_ln2_g, m_ln2_b]
    small_v = [v_conv_w, v_sinks, v_g_attn, v_g_conv, v_ln1_g, v_ln1_b, v_ln2_g, v_ln2_b]
    pw, sp = _pack(small_w)
    pg, _ = _pack(small_g)
    pm, _ = _pack(small_m)
    pv, _ = _pack(small_v)
    shapes = [w.shape for w in small_w]
    _, sd, sm, sv = [_unpack(p, sp, shapes) for p in _adamw("adamw_small", pw, pg[None], pm, pv)]
    for i, nm in enumerate(small_names):
        res[nm] = (small_g[i].reshape(shapes[i]), sd[i], sm[i], sv[i])

    order = ["w_in", "conv_w", "sinks", "g_attn", "g_conv", "w_out", "ln1_g", "ln1_b", "w_gate", "w_up", "w_down", "ln2_g", "ln2_b"]

    def lead(a, nm):
        return a[None] if nm in big else a

    return (loss, grad_x[None],
            *[lead(res[nm][0], nm) for nm in order], *[lead(res[nm][1], nm) for nm in order],
            *[lead(res[nm][2], nm) for nm in order], *[lead(res[nm][3], nm) for nm in order])
```

```python
import functools

import jax
import jax.numpy as jnp
from jax import lax
from jax.experimental import pallas as pl
from jax.experimental.pallas import tpu as pltpu

_F32 = jnp.float32
_CDT = jnp.bfloat16

HEAD_DIM = 64
WINDOW = 128
N_KV_HEADS = 4
KV_WIDTH = N_KV_HEADS * HEAD_DIM
ROT_DIM = HEAD_DIM // 4
ROPE_THETA = 500000.0
ATTN_SCALE = HEAD_DIM ** -0.5
DEPTH = 1
DEEPNORM_ALPHA = (2 * DEPTH) ** 0.25
LN_EPS = 1e-5
RMS_EPS = 1e-6
ADAM_LR = 0.001
ADAM_B1 = 0.9
ADAM_B2 = 0.999
ADAM_EPS = 1e-08
ADAM_WD = 0.01
ADAM_STEP = 10
N_DEV = 8
MASKED = -1e30

MIB = 1024 * 1024
V7X_VMEM_BYTES = 64 * MIB
V7X_LANES = 128
V7X_SUBLANES = 8
BODY_TEMPORARIES_BYTES = 16 * MIB
VMEM_LIMIT_FLOOR_BYTES = 32 * MIB
VMEM_LIMIT_CEILING_BYTES = V7X_VMEM_BYTES - 8 * MIB
_MESH = pl.DeviceIdType.MESH
_ANY = pl.BlockSpec(memory_space=pl.ANY)


def _vmem_limit(block_bytes, scratch_bytes=0):
    want = 2 * block_bytes + scratch_bytes + BODY_TEMPORARIES_BYTES
    return int(min(max(want, VMEM_LIMIT_FLOOR_BYTES), VMEM_LIMIT_CEILING_BYTES))


def _nbytes(shape, dtype):
    n = 1
    for s in shape:
        n *= s
    return n * jnp.dtype(dtype).itemsize


def _pick(n, candidates):
    for c in candidates:
        if n % c == 0:
            return c
    raise ValueError(f"no tile of {candidates} divides {n}")


_DOT_DIMS = {"nn": ((1,), (0,)), "nt": ((1,), (1,)), "tn": ((0,), (0,))}


def _dot(a, b, mode):
    return lax.dot_general(a.astype(_CDT), b.astype(_CDT), (_DOT_DIMS[mode], ((), ())),
                           preferred_element_type=_F32)


def _accumulate(ref, val, first):
    @pl.when(first)
    def _():
        ref[...] = val

    @pl.when(jnp.logical_not(first))
    def _():
        ref[...] += val


class _Comm:
    def __init__(self, inputs, outputs, aliases, sems, start, finish, middle=None):
        self.inputs, self.outputs, self.aliases, self.sems = inputs, outputs, aliases, sems
        self.start, self.finish, self.middle = start, finish, middle


def _middle_step(n_steps):
    return (2 * n_steps) // 3


class _CommArgs:
    def __init__(self, comms, n_in_before, n_out_before):
        self.comms, self.operands, self.out_shape, self.aliases, self.sems, self.at = comms, [], [], {}, [], []
        for cm in comms:
            self.at.append((len(self.operands), len(self.out_shape), len(self.sems)))
            for i_in, i_out in cm.aliases.items():
                self.aliases[n_in_before + len(self.operands) + i_in] = n_out_before + len(self.out_shape) + i_out
            self.operands += cm.inputs
            self.out_shape += cm.outputs
            self.sems += cm.sems

    def _each(self, in_refs, out_refs, sem_refs):
        for cm, (i0, o0, s0) in zip(self.comms, self.at):
            yield cm, (in_refs[i0:i0 + len(cm.inputs)], out_refs[o0:o0 + len(cm.outputs)], sem_refs[s0:s0 + len(cm.sems)])

    def start(self, in_refs, out_refs, sem_refs):
        for cm, refs in self._each(in_refs, out_refs, sem_refs):
            cm.start(*refs)

    def finish(self, in_refs, out_refs, sem_refs):
        for cm, refs in self._each(in_refs, out_refs, sem_refs):
            cm.finish(*refs)

    @property
    def has_middle(self):
        return any(cm.middle is not None for cm in self.comms)

    def middle(self, in_refs, out_refs, sem_refs):
        for cm, refs in self._each(in_refs, out_refs, sem_refs):
            if cm.middle is not None:
                cm.middle(*refs)


def _matmul(name, groups, m, n, k, tm, tn, tk, extras, outs, epilogue, comm=(), j_outer=False):
    assert m % tm == 0 and n % tn == 0 and k % tk == 0, (name, m, n, k, tm, tn, tk)
    nk = k // tk
    terms = [t for g in groups for t in g]
    operands, in_specs, block_bytes = [], [], 0

    def spec(blk, imap):
        return pl.BlockSpec(blk, (lambda g0, g1, kk: imap(g1, g0, kk)) if j_outer else imap)

    for a, b, mode in terms:
        assert a.shape == ((k, m) if mode == "tn" else (m, k)), (name, a.shape, mode)
        assert b.shape == ((n, k) if mode == "nt" else (k, n)), (name, b.shape, mode)
        if mode == "tn":
            a_blk, a_map = (tk, tm), (lambda i, j, kk: (kk, i))
        else:
            a_blk, a_map = (tm, tk), (lambda i, j, kk: (i, kk))
        if mode == "nt":
            b_blk, b_map = (tn, tk), (lambda i, j, kk: (j, kk))
        else:
            b_blk, b_map = (tk, tn), (lambda i, j, kk: (kk, j))
        operands += [a, b]
        in_specs += [spec(a_blk, a_map), spec(b_blk, b_map)]
        block_bytes += _nbytes(a_blk, a.dtype) + _nbytes(b_blk, b.dtype)
    for arr, blk, imap in extras:
        operands.append(arr)
        in_specs.append(spec(blk, lambda i, j, kk, imap=imap: imap(i, j)))
        block_bytes += _nbytes(blk, arr.dtype)
    out_shape, out_specs = [], []
    for shape, dtype, blk, imap in outs:
        out_shape.append(jax.ShapeDtypeStruct(shape, dtype))
        out_specs.append(spec(blk, lambda i, j, kk, imap=imap: imap(i, j)))
        block_bytes += _nbytes(blk, dtype)
    n_terms, n_extra, n_out, n_groups = len(terms), len(extras), len(outs), len(groups)
    scratch = [pltpu.VMEM((tm, tn), _F32) for _ in range(n_groups)] if nk > 1 else []
    ca = _CommArgs(list(comm), len(operands), n_out)
    n_cin, n_cout, n_acc = len(ca.operands), len(ca.out_shape), len(scratch)
    tiles = (m // tm, n // tn)
    grid = (tiles[1], tiles[0], nk) if j_outer else (tiles[0], tiles[1], nk)

    def body(*refs):
        refs = list(refs)
        term_refs = [refs.pop(0) for _ in range(2 * n_terms)]
        extra_refs = [refs.pop(0) for _ in range(n_extra)]
        cin_refs = [refs.pop(0) for _ in range(n_cin)]
        out_refs = [refs.pop(0) for _ in range(n_out)]
        cout_refs = [refs.pop(0) for _ in range(n_cout)]
        acc_refs = [refs.pop(0) for _ in range(n_acc)]
        sem_refs = refs
        g0, g1, kk = pl.program_id(0), pl.program_id(1), pl.program_id(2)
        first = jnp.logical_and(g0 == 0, g1 == 0)
        if comm:
            @pl.when(jnp.logical_and(first, kk == 0))
            def _():
                ca.start(cin_refs, cout_refs, sem_refs)
        if ca.has_middle:
            step = (g0 * grid[1] + g1) * nk + kk

            @pl.when(step == _middle_step(grid[0] * grid[1] * nk))
            def _():
                ca.middle(cin_refs, cout_refs, sem_refs)
        partial, t = [], 0
        for g in groups:
            s = None
            for _, _, mode in g:
                d = _dot(term_refs[2 * t][...], term_refs[2 * t + 1][...], mode)
                s = d if s is None else s + d
                t += 1
            partial.append(s)
        if nk == 1:
            epilogue(partial, extra_refs, out_refs, first)
        else:
            for acc, p in zip(acc_refs, partial):
                _accumulate(acc, p, kk == 0)

            @pl.when(kk == nk - 1)
            def _():
                epilogue([acc[...] for acc in acc_refs], extra_refs, out_refs, first)
        if comm:
            @pl.when(jnp.logical_and(jnp.logical_and(g0 == grid[0] - 1, g1 == grid[1] - 1), kk == nk - 1))
            def _():
                ca.finish(cin_refs, cout_refs, sem_refs)

    res = pl.pallas_call(
        body, name=name, grid=grid,
        in_specs=in_specs + [_ANY] * n_cin, out_specs=out_specs + [_ANY] * n_cout,
        out_shape=out_shape + ca.out_shape, scratch_shapes=scratch + ca.sems, input_output_aliases=ca.aliases,
        compiler_params=pltpu.CompilerParams(
            dimension_semantics=("arbitrary", "arbitrary", "arbitrary"),
            vmem_limit_bytes=_vmem_limit(block_bytes, n_groups * tm * tn * 4 if nk > 1 else 0)),
    )(*operands, *ca.operands)
    return list(res[:n_out]) + list(res[n_out:])


def _store_epilogue(accs, extra_refs, out_refs, first):
    for acc, ref in zip(accs, out_refs):
        ref[...] = acc.astype(ref.dtype)


def _tile_ij(i, j):
    return (i, j)


def _row_i(i, j):
    return (i, 0)


def _whole(i, j):
    return (0, 0)


def _mean(v):
    return jnp.mean(v, axis=-1, keepdims=True)


def _ln_fwd(r, g, b):
    xc = r - _mean(r)
    rstd = lax.rsqrt(_mean(xc * xc) + LN_EPS)
    xhat = xc * rstd
    return xhat * g + b, xhat, rstd


def _ln_bwd(dy, xhat, rstd, g):
    dxh = dy * g
    dr = rstd * (dxh - _mean(dxh) - xhat * _mean(dxh * xhat))
    return dr, jnp.sum(dy * xhat, axis=0, keepdims=True), jnp.sum(dy, axis=0, keepdims=True)


def _rms_fwd(a, g):
    rstd = lax.rsqrt(_mean(a * a) + RMS_EPS)
    return a * rstd * g


def _rms_bwd(dm, a, g):
    rstd = lax.rsqrt(_mean(a * a) + RMS_EPS)
    nhat = a * rstd
    dn = dm * g
    da = rstd * (dn - nhat * _mean(dn * nhat))
    return da, jnp.sum(dm * nhat, axis=0, keepdims=True)


def _lane(shape):
    return lax.broadcasted_iota(jnp.int32, shape, 1)


def _row(shape):
    return lax.broadcasted_iota(jnp.int32, shape, 0)


def _rope_tables(pos, invf):
    ang = pos.astype(_F32) * invf
    lane = _lane(ang.shape)
    in_rot = (lane % HEAD_DIM) < ROT_DIM
    first = (lane % ROT_DIM) < ROT_DIM // 2
    cos = jnp.where(in_rot, jnp.cos(ang), 1.0)
    sin = jnp.sin(ang)
    sgn = jnp.where(in_rot, jnp.where(first, -sin, sin), 0.0)
    return cos, sgn


def _rope(t, cos, sgn, sign):
    half = ROT_DIM // 2
    first = (_lane(t.shape) % ROT_DIM) < half
    partner = jnp.where(first, pltpu.roll(t, V7X_LANES - half, 1), pltpu.roll(t, half, 1))
    return t * cos + partner * (sgn * sign)


def _dup_head(t, h):
    g = t[:, 128 * (h // 2):128 * (h // 2) + 128]
    r = pltpu.roll(g, HEAD_DIM, 1)
    lo = _lane(g.shape) < HEAD_DIM
    return jnp.where(lo, g, r) if h % 2 == 0 else jnp.where(lo, r, g)


def _fold_halves(t):
    return t + pltpu.roll(t, HEAD_DIM, 1)


def _halves(t):
    lo = _lane(t.shape) < HEAD_DIM
    zero = jnp.zeros_like(t)
    return jnp.where(lo, t, zero), jnp.where(lo, zero, t)


def _band_mask(n_heads, n_keys, first_block):
    shape = (n_heads * WINDOW, n_keys)
    i = jnp.bitwise_and(_row(shape), WINDOW - 1)
    j = _lane(shape)
    valid = jnp.logical_and(j >= i + 1, j <= i + WINDOW)
    if first_block is not None:
        valid = jnp.logical_and(valid, jnp.logical_or(j >= WINDOW, jnp.logical_not(first_block)))
    return valid


def _stack_heads(pairs):
    return jnp.concatenate([half for t in pairs for half in _halves(t.astype(_CDT))], axis=0)


def _unstack_heads(t, n_pairs):
    lo = _lane((WINDOW, 128)) < HEAD_DIM
    return [jnp.where(lo, t[2 * WINDOW * i:2 * WINDOW * i + WINDOW], t[2 * WINDOW * i + WINDOW:2 * WINDOW * (i + 1)])
            for i in range(n_pairs)]


def _shift_down(z, halo, k):
    out = pltpu.roll(z, k, 0)
    r = _row(z.shape)
    for t in range(k):
        out = jnp.where(r == t, halo[V7X_SUBLANES - k + t:V7X_SUBLANES - k + t + 1, :], out)
    return out


def _shift_up(z, halo, k):
    rows = z.shape[0]
    out = pltpu.roll(z, rows - k, 0)
    r = _row(z.shape)
    for t in range(k):
        out = jnp.where(r == rows - k + t, halo[t:t + 1, :], out)
    return out


class _Dims:
    def __init__(self, s, d, d_ff):
        self.s, self.d, self.d_ff = s, d, d_ff
        self.aw = d // 2
        self.cw = d - self.aw
        self.nq = self.aw // HEAD_DIM
        self.group = self.nq // N_KV_HEADS
        assert self.group % 2 == 0, "a 128-lane pair of query heads must share its kv head"
        self.inw = self.aw + 2 * KV_WIDTH + 3 * self.cw
        self.o_k = self.aw
        self.o_v = self.aw + KV_WIDTH
        self.o_cg = self.aw + 2 * KV_WIDTH
        self.o_bg = self.o_cg + self.cw
        self.o_u = self.o_bg + self.cw
        self.nb = s // WINDOW
        assert s % WINDOW == 0


def _carrying(body, n_in, n_out, n_steps, ca, n_scratch=0):
    n_cin, n_cout = len(ca.operands), len(ca.out_shape)

    def wrapped(*refs):
        refs = list(refs)
        in_refs = [refs.pop(0) for _ in range(n_in)]
        cin_refs = [refs.pop(0) for _ in range(n_cin)]
        out_refs = [refs.pop(0) for _ in range(n_out)]
        cout_refs = [refs.pop(0) for _ in range(n_cout)]
        scratch_refs = [refs.pop(0) for _ in range(n_scratch)]
        if ca.comms:
            @pl.when(pl.program_id(0) == 0)
            def _():
                ca.start(cin_refs, cout_refs, refs)
        if ca.has_middle:
            @pl.when(pl.program_id(0) == _middle_step(n_steps))
            def _():
                ca.middle(cin_refs, cout_refs, refs)
        body(*in_refs, *out_refs, *scratch_refs)
        if ca.comms:
            @pl.when(pl.program_id(0) == n_steps - 1)
            def _():
                ca.finish(cin_refs, cout_refs, refs)

    return wrapped


def _row_kernel(name, body, rows_in, vecs_in, rows_out, vecs_out, comm=()):
    s = rows_in[0].shape[0]
    tr = _pick(s, (256, 128))
    row = lambda a: pl.BlockSpec((tr, a[1] if isinstance(a, tuple) else a.shape[1]), lambda i: (i, 0))
    vec = lambda shape: pl.BlockSpec(tuple(shape), lambda i: (0, 0))
    n_in, n_out = len(rows_in) + len(vecs_in), len(rows_out) + len(vecs_out)
    ca = _CommArgs(list(comm), n_in, n_out)
    blocks = sum(_nbytes((tr, a.shape[1]), a.dtype) for a in rows_in) + sum(_nbytes((tr, sh[1]), dt) for sh, dt in rows_out)
    res = pl.pallas_call(
        _carrying(body, n_in, n_out, s // tr, ca), name=name, grid=(s // tr,),
        in_specs=[row(a) for a in rows_in] + [vec(v.shape) for v in vecs_in] + [_ANY] * len(ca.operands),
        out_specs=[row(sh) for sh, _ in rows_out] + [vec(sh) for sh, _ in vecs_out] + [_ANY] * len(ca.out_shape),
        out_shape=[jax.ShapeDtypeStruct(sh, dt) for sh, dt in list(rows_out) + list(vecs_out)] + ca.out_shape,
        scratch_shapes=ca.sems, input_output_aliases=ca.aliases,
        compiler_params=pltpu.CompilerParams(dimension_semantics=("arbitrary",), vmem_limit_bytes=_vmem_limit(blocks)),
    )(*rows_in, *vecs_in, *ca.operands)
    return list(res)


def _ln2_loss_bwd(r2, target, gain, bias, comm=()):
    s, d = r2.shape

    def body(r_ref, t_ref, g_ref, b_ref, dr_ref, drc_ref, loss_ref, dg_ref, db_ref):
        first = pl.program_id(0) == 0
        yv, xhat, rstd = _ln_fwd(r_ref[...], g_ref[...], b_ref[...])
        err = yv - t_ref[...]
        dr2, dg, db = _ln_bwd(err * (1.0 / d), xhat, rstd, g_ref[...])
        dr_ref[...] = dr2
        drc_ref[...] = dr2.astype(_CDT)
        _accumulate(loss_ref, jnp.zeros(loss_ref.shape, _F32) + 0.5 * jnp.sum(err * err) * (1.0 / d), first)
        _accumulate(dg_ref, dg, first)
        _accumulate(db_ref, db, first)

    return _row_kernel("ln2_loss_bwd", body, [r2, target], [gain, bias], [((s, d), _F32), ((s, d), _CDT)],
                       [((V7X_SUBLANES, V7X_LANES), _F32), ((1, d), _F32), ((1, d), _F32)], comm)


def _ln1_fwd_rows(r1, gain, bias, comm=()):
    s, d = r1.shape

    def body(r_ref, g_ref, b_ref, h_ref, hc_ref):
        h1, _, _ = _ln_fwd(r_ref[...], g_ref[...], b_ref[...])
        h_ref[...] = h1
        hc_ref[...] = h1.astype(_CDT)

    return _row_kernel("ln1", body, [r1], [gain, bias], [((s, d), _F32), ((s, d), _CDT)], [], comm)


def _ln1_bwd_rows(dh1, r1, gain, comm=()):
    s, d = dh1.shape

    def body(dh_ref, r_ref, g_ref, dr_ref, drc_ref, dg_ref, db_ref):
        first = pl.program_id(0) == 0
        _, xhat, rstd = _ln_fwd(r_ref[...], g_ref[...], 0.0)
        dr1, dg, db = _ln_bwd(dh_ref[...], xhat, rstd, g_ref[...])
        dr_ref[...] = dr1
        drc_ref[...] = dr1.astype(_CDT)
        _accumulate(dg_ref, dg, first)
        _accumulate(db_ref, db, first)

    return _row_kernel("ln1_bwd", body, [dh1, r1], [gain], [((s, d), _F32), ((s, d), _CDT)],
                       [((1, d), _F32), ((1, d), _F32)], comm)


def _mixer_fwd(dm, proj, rope, sinks, g_attn, g_conv, conv_w8, comm=()):
    s, d, aw, cw, nq, inw, nb = dm.s, dm.d, dm.aw, dm.cw, dm.nq, dm.inw, dm.nb

    def body(pp_ref, pc_ref, ropep_ref, ropec_ref, sinks_ref, ga_ref, gc_ref, cw_ref,
             mixed_ref, attn_ref, lse_ref, y_ref, qk_ref):
        n = pl.program_id(0)
        cos_c, sgn_c = ropec_ref[:, 0:V7X_LANES], ropec_ref[:, V7X_LANES:2 * V7X_LANES]
        cos_p, sgn_p = ropep_ref[:, 0:V7X_LANES], ropep_ref[:, V7X_LANES:2 * V7X_LANES]
        for g in range(KV_WIDTH // 128):
            qk_ref[:, aw + 128 * g:aw + 128 * g + 128] = _rope(
                pc_ref[:, dm.o_k + 128 * g:dm.o_k + 128 * g + 128], cos_c, sgn_c, 1.0).astype(qk_ref.dtype)
        for j in range(nq // 2):
            qk_ref[:, 128 * j:128 * j + 128] = _rope(pc_ref[:, 128 * j:128 * j + 128], cos_c, sgn_c, 1.0).astype(qk_ref.dtype)
        k_prev = jnp.concatenate([_rope(pp_ref[:, dm.o_k + 128 * g:dm.o_k + 128 * g + 128], cos_p, sgn_p, 1.0)
                                  for g in range(KV_WIDTH // 128)], axis=1)
        kk = jnp.concatenate([k_prev, qk_ref[:, aw:aw + KV_WIDTH].astype(_F32)], axis=0)
        vv = jnp.concatenate([pp_ref[:, dm.o_v:dm.o_v + KV_WIDTH], pc_ref[:, dm.o_v:dm.o_v + KV_WIDTH]], axis=0)
        group, pairs = dm.group, dm.group // 2
        valid = _band_mask(group, 2 * WINDOW, n == 0)
        for h in range(N_KV_HEADS):
            k2, v2 = _dup_head(kk, h).astype(_CDT), _dup_head(vv, h).astype(_CDT)
            q4 = _stack_heads([qk_ref[:, 128 * j:128 * j + 128] for j in range(pairs * h, pairs * (h + 1))])
            sc = jnp.where(valid, _dot(q4, k2, "nt") * ATTN_SCALE, MASKED)
            row_max = jnp.max(sc, axis=1, keepdims=True)
            rows = [slice(WINDOW * r, WINDOW * (r + 1)) for r in range(group)]
            sink = [sinks_ref[0, group * h + r] for r in range(group)]
            mx = jnp.concatenate([jnp.maximum(row_max[rows[r]], sink[r]) for r in range(group)], axis=0)
            p = jnp.exp(sc - mx)
            den = jnp.sum(p, axis=1, keepdims=True) + jnp.concatenate(
                [jnp.exp(sink[r] - mx[rows[r]]) for r in range(group)], axis=0)
            out = _unstack_heads(_dot(p / den, v2, "nn"), pairs)
            lse = mx + jnp.log(den)
            for r in range(group):
                lse_ref[:, group * h + r:group * h + r + 1] = lse[WINDOW * r:WINDOW * (r + 1)]
            for i in range(pairs):
                j = pairs * h + i
                attn_ref[:, 128 * j:128 * j + 128] = out[i]
        mixed_ref[:, 0:aw] = _rms_fwd(attn_ref[...], ga_ref[...]).astype(mixed_ref.dtype)

        z = pc_ref[:, dm.o_cg:dm.o_cg + cw] * pc_ref[:, dm.o_u:dm.o_u + cw]
        top = WINDOW - V7X_SUBLANES
        halo = pp_ref[top:WINDOW, dm.o_cg:dm.o_cg + cw] * pp_ref[top:WINDOW, dm.o_u:dm.o_u + cw]
        halo = jnp.where(n == 0, jnp.zeros_like(halo), halo)
        y = cw_ref[0:1, :] * _shift_down(z, halo, 2) + cw_ref[1:2, :] * _shift_down(z, halo, 1) + cw_ref[2:3, :] * z
        y_ref[...] = y
        conv = pc_ref[:, dm.o_bg:dm.o_bg + cw] * y
        mixed_ref[:, aw:d] = _rms_fwd(conv, gc_ref[...]).astype(mixed_ref.dtype)

    prev = lambda n: (jnp.maximum(n - 1, 0), 0)
    cur = lambda n: (n, 0)
    fixed = lambda n: (0, 0)
    blocks = 2 * WINDOW * inw * 4 + WINDOW * (d * 2 + aw * 4 + cw * 4 + nq * 4)
    ca = _CommArgs(list(comm), 8, 5)
    return pl.pallas_call(
        _carrying(body, 8, 5, nb, ca), name="mixer_fwd", grid=(nb,),
        in_specs=[pl.BlockSpec((WINDOW, inw), prev), pl.BlockSpec((WINDOW, inw), cur),
                  pl.BlockSpec((WINDOW, 2 * V7X_LANES), prev), pl.BlockSpec((WINDOW, 2 * V7X_LANES), cur),
                  pl.BlockSpec(memory_space=pltpu.SMEM),
                  pl.BlockSpec((1, aw), fixed), pl.BlockSpec((1, cw), fixed), pl.BlockSpec((V7X_SUBLANES, cw), fixed)]
        + [_ANY] * len(ca.operands),
        out_specs=[pl.BlockSpec((WINDOW, d), cur), pl.BlockSpec((WINDOW, aw), cur),
                   pl.BlockSpec((WINDOW, nq), cur), pl.BlockSpec((WINDOW, cw), cur),
                   pl.BlockSpec((WINDOW, aw + KV_WIDTH), cur)] + [_ANY] * len(ca.out_shape),
        out_shape=[jax.ShapeDtypeStruct((s, d), _CDT), jax.ShapeDtypeStruct((s, aw), _F32),
                   jax.ShapeDtypeStruct((s, nq), _F32), jax.ShapeDtypeStruct((s, cw), _F32),
                   jax.ShapeDtypeStruct((s, aw + KV_WIDTH), _CDT)] + ca.out_shape,
        scratch_shapes=ca.sems, input_output_aliases=ca.aliases,
        compiler_params=pltpu.CompilerParams(dimension_semantics=("arbitrary",), vmem_limit_bytes=_vmem_limit(blocks)),
    )(proj, proj, rope, rope, sinks, g_attn, g_conv, conv_w8, *ca.operands)


def _patch_columns(name, a, part, offset):
    s, pw = part.shape
    assert offset % pw == 0 and pw % V7X_LANES == 0
    tr = _pick(s, (512, 256, 128))

    def body(a_ref, p_ref, o_ref):
        del a_ref
        o_ref[...] = p_ref[...]

    return pl.pallas_call(
        body, name=name, grid=(s // tr,),
        in_specs=[_ANY, pl.BlockSpec((tr, pw), lambda i: (i, 0))],
        out_specs=pl.BlockSpec((tr, pw), lambda i: (i, offset // pw)),
        out_shape=jax.ShapeDtypeStruct(a.shape, a.dtype), input_output_aliases={0: 0},
        compiler_params=pltpu.CompilerParams(dimension_semantics=("arbitrary",)),
    )(a, part)


def _mixer_bwd(dm, proj, rope, sinks, g_attn, g_conv, conv_w8, dmixed, attn, lse, y, qk, comm=()):
    s, d, aw, cw, nq, inw, nb = dm.s, dm.d, dm.aw, dm.cw, dm.nq, dm.inw, dm.nb

    def body(pp_ref, pc_ref, pn_ref, ropep_ref, ropec_ref, dmc_ref, dmn_ref, ac_ref,
             lsec_ref, yc_ref, yn_ref, qkp_ref, qkc_ref, sinks_ref, ga_ref, gc_ref, cw_ref,
             dproj_ref, dkv_ref, dga_ref, dgc_ref, dsinks_ref, dcw_ref, dk_carry, dv_carry):
        n = pl.program_id(0)
        first = n == 0
        live = n < nb
        has_next = n < nb - 1
        cos_p, sgn_p = ropep_ref[:, 0:V7X_LANES], ropep_ref[:, V7X_LANES:2 * V7X_LANES]
        cos_c, sgn_c = ropec_ref[:, 0:V7X_LANES], ropec_ref[:, V7X_LANES:2 * V7X_LANES]

        @pl.when(first)
        def _():
            dk_carry[...] = jnp.zeros(dk_carry.shape, _F32)
            dv_carry[...] = jnp.zeros(dv_carry.shape, _F32)

        def write_kv(dk2, dv2, cos, sgn):
            lo = _lane((WINDOW, 128)) < HEAD_DIM
            for g in range(KV_WIDTH // 128):
                dk = jnp.where(lo, _fold_halves(dk2[2 * g]), _fold_halves(dk2[2 * g + 1]))
                dv = jnp.where(lo, _fold_halves(dv2[2 * g]), _fold_halves(dv2[2 * g + 1]))
                dkv_ref[:, 128 * g:128 * g + 128] = _rope(dk, cos, sgn, -1.0).astype(dkv_ref.dtype)
                dkv_ref[:, KV_WIDTH + 128 * g:KV_WIDTH + 128 * g + 128] = dv.astype(dkv_ref.dtype)

        @pl.when(jnp.logical_not(live))
        def _():
            write_kv([dk_carry[h] for h in range(N_KV_HEADS)], [dv_carry[h] for h in range(N_KV_HEADS)], cos_c, sgn_c)

        @pl.when(live)
        def _():
            block_step(pp_ref, pc_ref, pn_ref, dmc_ref, dmn_ref, ac_ref, lsec_ref, yc_ref, yn_ref, qkp_ref, qkc_ref,
                       sinks_ref, ga_ref, gc_ref, cw_ref, dproj_ref, dga_ref, dgc_ref, dsinks_ref, dcw_ref, dk_carry,
                       dv_carry, first, has_next, cos_p, sgn_p, cos_c, sgn_c, write_kv)

    def block_step(pp_ref, pc_ref, pn_ref, dmc_ref, dmn_ref, ac_ref, lsec_ref, yc_ref, yn_ref, qkp_ref, qkc_ref,
                   sinks_ref, ga_ref, gc_ref, cw_ref, dproj_ref, dga_ref, dgc_ref, dsinks_ref, dcw_ref, dk_carry,
                   dv_carry, first, has_next, cos_p, sgn_p, cos_c, sgn_c, write_kv):
        da_c, dga = _rms_bwd(dmc_ref[:, 0:aw], ac_ref[...], ga_ref[...])
        _accumulate(dga_ref, dga, first)
        kk = jnp.concatenate([qkp_ref[:, aw:aw + KV_WIDTH], qkc_ref[:, aw:aw + KV_WIDTH]], axis=0).astype(_F32)
        vv = jnp.concatenate([pp_ref[:, dm.o_v:dm.o_v + KV_WIDTH], pc_ref[:, dm.o_v:dm.o_v + KV_WIDTH]], axis=0)
        group, pairs = dm.group, dm.group // 2
        valid_c = _band_mask(group, 2 * WINDOW, first)
        dk_prev, dv_prev = [], []

        def stacked(q_ref, da, o_ref, lse_ref_, h):
            cols = [slice(128 * j, 128 * j + 128) for j in range(pairs * h, pairs * (h + 1))]
            q4 = _stack_heads([q_ref[:, c] for c in cols])
            do4 = _stack_heads([da[:, c] for c in cols])
            lo = _lane((WINDOW, 128)) < HEAD_DIM
            deltas = []
            for c in cols:
                prod = o_ref[:, c] * da[:, c]
                deltas += [jnp.sum(jnp.where(lo, prod, 0.0), axis=1, keepdims=True),
                           jnp.sum(jnp.where(lo, 0.0, prod), axis=1, keepdims=True)]
            lse4 = jnp.concatenate([lse_ref_[:, group * h + r:group * h + r + 1] for r in range(group)], axis=0)
            return q4, do4, lse4, deltas

        def scores_bwd(q4, do4, lse4, delta4, keys, vals, valid):
            sc = _dot(q4, keys, "nt") * ATTN_SCALE
            p = jnp.exp(jnp.where(valid, sc - lse4, MASKED))
            return p.astype(_CDT), (p * (_dot(do4, vals, "nt") - delta4) * ATTN_SCALE).astype(_CDT)

        for h in range(N_KV_HEADS):
            k2, v2 = _dup_head(kk, h).astype(_CDT), _dup_head(vv, h).astype(_CDT)
            q4, do4, lse4, deltas = stacked(qkc_ref, da_c, ac_ref, lsec_ref, h)
            delta4 = jnp.concatenate(deltas, axis=0)
            p, ds = scores_bwd(q4, do4, lse4, delta4, k2, v2, valid_c)
            for i, dq in enumerate(_unstack_heads(_dot(ds, k2, "nn"), pairs)):
                j = pairs * h + i
                dproj_ref[:, 128 * j:128 * j + 128] = _rope(dq, cos_c, sgn_c, -1.0).astype(dproj_ref.dtype)
            dk = _dot(ds, q4, "tn")
            dv = _dot(p, do4, "tn")
            dk_prev.append(dk_carry[h] + dk[0:WINDOW])
            dv_prev.append(dv_carry[h] + dv[0:WINDOW])
            dk_carry[h] = dk[WINDOW:2 * WINDOW]
            dv_carry[h] = dv[WINDOW:2 * WINDOW]
            heads = slice(group * h, group * (h + 1))
            sink_row, delta_heads = jnp.zeros((1, group), _F32), jnp.zeros((WINDOW, group), _F32)
            for r in range(group):
                sink_row = jnp.where(_lane((1, group)) == r, sinks_ref[0, group * h + r], sink_row)
                delta_heads = jnp.where(_lane((WINDOW, group)) == r, deltas[r], delta_heads)
            loss_sink = jnp.exp(sink_row - lsec_ref[:, heads]) * delta_heads
            _accumulate(dsinks_ref.at[:, heads], -jnp.sum(loss_sink, axis=0, keepdims=True), first)
        write_kv(dk_prev, dv_prev, cos_p, sgn_p)

        bg = pc_ref[:, dm.o_bg:dm.o_bg + cw]
        yc = yc_ref[...]
        dconv, dgc = _rms_bwd(dmc_ref[:, aw:d], bg * yc, gc_ref[...])
        _accumulate(dgc_ref, dgc, first)
        dproj_ref[:, dm.o_bg:dm.o_bg + cw] = (dconv * yc).astype(dproj_ref.dtype)
        dy = dconv * bg
        bg_n = pn_ref[:, dm.o_bg:dm.o_bg + cw]
        dconv_n, _ = _rms_bwd(dmn_ref[:, aw:d], bg_n * yn_ref[...], gc_ref[...])
        halo = jnp.where(has_next, dconv_n * bg_n, 0.0)
        dy1 = _shift_up(dy, halo, 1)
        dy2 = _shift_up(dy, halo, 2)
        dz = cw_ref[2:3, :] * dy + cw_ref[1:2, :] * dy1 + cw_ref[0:1, :] * dy2
        cg = pc_ref[:, dm.o_cg:dm.o_cg + cw]
        u = pc_ref[:, dm.o_u:dm.o_u + cw]
        dproj_ref[:, dm.o_cg:dm.o_cg + cw] = (dz * u).astype(dproj_ref.dtype)
        dproj_ref[:, dm.o_u:dm.o_u + cw] = (dz * cg).astype(dproj_ref.dtype)
        z = cg * u
        dcw = jnp.concatenate(
            [jnp.sum(z * t, axis=0, keepdims=True) for t in (dy2, dy1, dy)]
            + [jnp.zeros((V7X_SUBLANES - 3, cw), _F32)], axis=0)
        _accumulate(dcw_ref, dcw, first)

    at = lambda n: jnp.minimum(n, nb - 1)
    prev = lambda n: (jnp.maximum(at(n) - 1, 0), 0)
    cur = lambda n: (at(n), 0)
    done = lambda n: (jnp.maximum(n - 1, 0), 0)
    nxt8 = lambda n: (jnp.minimum((at(n) + 1) * (WINDOW // V7X_SUBLANES), s // V7X_SUBLANES - 1), 0)
    fixed = lambda n: (0, 0)
    blocks = WINDOW * (2 * inw * 4 + d * 4 + aw * 4 + cw * 4 + inw * 2 + 2 * KV_WIDTH * 2)
    carry = [pltpu.VMEM((N_KV_HEADS, WINDOW, 128), _F32), pltpu.VMEM((N_KV_HEADS, WINDOW, 128), _F32)]
    n_in, n_out = 17, 6
    ca = _CommArgs(list(comm), n_in, n_out)
    return pl.pallas_call(
        _carrying(body, n_in, n_out, nb + 1, ca, n_scratch=len(carry)), name="mixer_bwd", grid=(nb + 1,),
        in_specs=[pl.BlockSpec((WINDOW, inw), prev), pl.BlockSpec((WINDOW, inw), cur), pl.BlockSpec((V7X_SUBLANES, inw), nxt8),
                  pl.BlockSpec((WINDOW, 2 * V7X_LANES), prev), pl.BlockSpec((WINDOW, 2 * V7X_LANES), cur),
                  pl.BlockSpec((WINDOW, d), cur), pl.BlockSpec((V7X_SUBLANES, d), nxt8),
                  pl.BlockSpec((WINDOW, aw), cur), pl.BlockSpec((WINDOW, nq), cur),
                  pl.BlockSpec((WINDOW, cw), cur), pl.BlockSpec((V7X_SUBLANES, cw), nxt8),
                  pl.BlockSpec((WINDOW, aw + KV_WIDTH), prev), pl.BlockSpec((WINDOW, aw + KV_WIDTH), cur),
                  pl.BlockSpec(memory_space=pltpu.SMEM),
                  pl.BlockSpec((1, aw), fixed), pl.BlockSpec((1, cw), fixed), pl.BlockSpec((V7X_SUBLANES, cw), fixed)]
        + [_ANY] * len(ca.operands),
        out_specs=[pl.BlockSpec((WINDOW, inw), cur), pl.BlockSpec((WINDOW, 2 * KV_WIDTH), done),
                   pl.BlockSpec((1, aw), fixed), pl.BlockSpec((1, cw), fixed),
                   pl.BlockSpec((1, nq), fixed), pl.BlockSpec((V7X_SUBLANES, cw), fixed)] + [_ANY] * len(ca.out_shape),
        out_shape=[jax.ShapeDtypeStruct((s, inw), _CDT), jax.ShapeDtypeStruct((s, 2 * KV_WIDTH), _CDT),
                   jax.ShapeDtypeStruct((1, aw), _F32), jax.ShapeDtypeStruct((1, cw), _F32),
                   jax.ShapeDtypeStruct((1, nq), _F32), jax.ShapeDtypeStruct((V7X_SUBLANES, cw), _F32)] + ca.out_shape,
        scratch_shapes=carry + ca.sems, input_output_aliases=ca.aliases,
        compiler_params=pltpu.CompilerParams(dimension_semantics=("arbitrary",), vmem_limit_bytes=_vmem_limit(blocks)),
    )(proj, proj, proj, rope, rope, dmixed, dmixed, attn, lse, y, y, qk, qk, sinks, g_attn, g_conv, conv_w8, *ca.operands)


def _position():
    return lax.axis_index("x"), lax.axis_index("y"), lax.axis_index("c")


def _linear(px, py, pc):
    return 4 * px + 2 * py + pc


def _comm_kernel(name, comm):
    ca = _CommArgs(list(comm), 0, 0)
    n_cin, n_cout = len(ca.operands), len(ca.out_shape)

    def body(*refs):
        cin, cout, sems = refs[:n_cin], refs[n_cin:n_cin + n_cout], refs[n_cin + n_cout:]
        ca.start(cin, cout, sems)
        ca.middle(cin, cout, sems)
        ca.finish(cin, cout, sems)

    return pl.pallas_call(
        body, name=name, out_shape=ca.out_shape, in_specs=[_ANY] * n_cin, out_specs=[_ANY] * n_cout,
        scratch_shapes=ca.sems, input_output_aliases=ca.aliases,
    )(*ca.operands)


def _gather_op(units):
    n = len(units)
    inputs, outputs, aliases = [], [], {}
    for shard, _, _, _ in units:
        inputs.append(shard)
        outputs.append(jax.ShapeDtypeStruct((N_DEV * shard.shape[0], shard.shape[1]), shard.dtype))
    for u, (_, buf, _, _) in enumerate(units):
        if buf is not None:
            aliases[len(inputs)] = u
            inputs.append(buf)

    def plan(ins, outs, sems, north):
        send_sems, recv_sems, local_sems = sems
        x, y, c = _position()
        me, sibling = (x, y, c), (x, y, 1 - c)
        xn, yn, dg = (1 - x, y), (x, 1 - y), (1 - x, 1 - y)
        via, to, k_via, k_other = (yn, xn, 2, 1) if north else (xn, yn, 1, 2)

        def rows(u, px, py, pc):
            shard, _, r0, r1 = units[u]
            return outs[u].at[pl.ds(pl.multiple_of(_linear(px, py, pc) * shard.shape[0] + r0, 16), r1 - r0), :]

        def own(u):
            _, _, r0, r1 = units[u]
            return ins[u].at[pl.ds(r0, r1 - r0), :]

        def copy(u, k, block, to_, src=None):
            return pltpu.make_async_remote_copy(
                src_ref=rows(u, *block) if src is None else src, dst_ref=rows(u, *block),
                send_sem=send_sems.at[u, k], recv_sem=recv_sems.at[u, k], device_id=to_, device_id_type=_MESH)

        us = range(n)
        return dict(
            mine=[pltpu.make_async_copy(own(u), rows(u, *me), local_sems.at[u]) for u in us],
            first=[cp for u in us for cp in (copy(u, 0, me, sibling, src=own(u)), copy(u, 1, me, (*xn, c), src=own(u)),
                                             copy(u, 2, me, (*yn, c), src=own(u)))],
            relay=[copy(u, 3, (*via, c), (*to, c)) for u in us],
            arrived={1: [copy(u, 1, (*xn, c), me) for u in us], 2: [copy(u, 2, (*yn, c), me) for u in us],
                     3: [copy(u, 3, (*dg, c), me) for u in us]},
            passed={1: [copy(u, 4, (*xn, c), sibling) for u in us], 2: [copy(u, 5, (*yn, c), sibling) for u in us],
                    3: [copy(u, 6, (*dg, c), sibling) for u in us]},
            rest=[cp for u in us for cp in (copy(u, 0, sibling, me), copy(u, 4, (*xn, 1 - c), me),
                                            copy(u, 5, (*yn, 1 - c), me), copy(u, 6, (*dg, 1 - c), me))],
            k_via=k_via, k_other=k_other)

    def land(p, k):
        for arrived, onward in zip(p["arrived"][k], p["passed"][k]):
            arrived.wait_recv()
            onward.start()

    def by_core(fn):
        c = lax.axis_index("c")
        for north in (True, False):
            pl.when(c == (1 if north else 0))(functools.partial(fn, north))

    def start(ins, outs, sems):
        p = plan(ins, outs, sems, True)
        for cp in p["mine"] + p["first"]:
            cp.start()

    def middle(ins, outs, sems):
        def go(north):
            p = plan(ins, outs, sems, north)
            land(p, p["k_via"])
            for cp in p["relay"]:
                cp.start()
            land(p, p["k_other"])
        by_core(go)

    def finish(ins, outs, sems):
        def go(north):
            p = plan(ins, outs, sems, north)
            land(p, 3)
            for cp in p["rest"]:
                cp.wait_recv()
            for cp in p["first"] + p["relay"] + [cp for k in (1, 2, 3) for cp in p["passed"][k]]:
                cp.wait_send()
            for cp in p["mine"]:
                cp.wait()
        by_core(go)

    sems = [pltpu.SemaphoreType.DMA((n, 7)), pltpu.SemaphoreType.DMA((n, 7)), pltpu.SemaphoreType.DMA((n,))]
    return _Comm(inputs, outputs, aliases, sems, start, finish, middle)


def _peers(x, y, c):
    out = []
    for k in range(1, N_DEV):
        fx, fy, fc = (k >> 2) & 1, (k >> 1) & 1, k & 1
        out.append((1 - x if fx else x, 1 - y if fy else y, 1 - c if fc else c))
    return out


def _exchange_op(partials):
    n = len(partials)
    outputs = [jax.ShapeDtypeStruct((4, p.shape[0] // N_DEV, p.shape[1]), p.dtype) for p in partials]

    def plan(ins, outs, sems):
        send_sems, recv_sems = sems
        x, y, c = _position()
        out = []
        for a in range(n):
            r = outs[a].shape[1]
            for ch in range(4):
                out.append(pltpu.make_async_remote_copy(
                    src_ref=ins[a].at[pl.ds(pl.multiple_of((2 * ch + 1 - c) * r, 16), r), :], dst_ref=outs[a].at[ch],
                    send_sem=send_sems.at[a, ch], recv_sem=recv_sems.at[a, ch], device_id=(x, y, 1 - c),
                    device_id_type=_MESH))
        return out

    def start(ins, outs, sems):
        for cp in plan(ins, outs, sems):
            cp.start()

    def finish(ins, outs, sems):
        copies = plan(ins, outs, sems)
        for cp in copies:
            cp.wait_recv()
        for cp in copies:
            cp.wait_send()

    sems = [pltpu.SemaphoreType.DMA((n, 4)), pltpu.SemaphoreType.DMA((n, 4))]
    return _Comm(list(partials), outputs, {}, sems, start, finish)


def _chip_send_op(units):
    n = len(units)
    inputs, outputs, aliases = [], [], {}
    for q, _, _, _ in units:
        inputs.append(q)
        outputs.append(jax.ShapeDtypeStruct(q.shape, q.dtype))
    for u, (_, buf, _, _) in enumerate(units):
        if buf is not None:
            aliases[len(inputs)] = u
            inputs.append(buf)

    def plan(ins, outs, sems):
        send_sems, recv_sems, local_sems = sems
        x, y, c = _position()
        my_chip = 2 * x + y
        chips = [(1 - x, y), (x, 1 - y), (1 - x, 1 - y)]
        mine, sends, arrivals = [], [], []
        for u, (_, _, r0, r1) in enumerate(units):
            span = pl.ds(r0, r1 - r0)
            mine.append(pltpu.make_async_copy(ins[u].at[my_chip, span, :], outs[u].at[my_chip, span, :], local_sems.at[u]))
            for k, (px, py) in enumerate(chips):
                sends.append(pltpu.make_async_remote_copy(
                    src_ref=ins[u].at[2 * px + py, span, :], dst_ref=outs[u].at[my_chip, span, :],
                    send_sem=send_sems.at[u, k], recv_sem=recv_sems.at[u, k], device_id=(px, py, c), device_id_type=_MESH))
                arrivals.append(pltpu.make_async_remote_copy(
                    src_ref=ins[u].at[my_chip, span, :], dst_ref=outs[u].at[2 * px + py, span, :],
                    send_sem=send_sems.at[u, k], recv_sem=recv_sems.at[u, k], device_id=(px, py, c), device_id_type=_MESH))
        return mine, sends, arrivals

    def start(ins, outs, sems):
        mine, sends, _ = plan(ins, outs, sems)
        for cp in mine + sends:
            cp.start()

    def finish(ins, outs, sems):
        mine, sends, arrivals = plan(ins, outs, sems)
        for cp in arrivals:
            cp.wait_recv()
        for cp in sends:
            cp.wait_send()
        for cp in mine:
            cp.wait()

    sems = [pltpu.SemaphoreType.DMA((n, 3)), pltpu.SemaphoreType.DMA((n, 3)), pltpu.SemaphoreType.DMA((n,))]
    return _Comm(inputs, outputs, aliases, sems, start, finish)


def _pair_sum(name, partial, received):
    _, rows, cols = received.shape
    tr = _pick(rows, (352, 288, 256, 128, 64, 32, 16))
    p4 = partial.reshape(4, 2, rows, cols)
    kind = jnp.reshape(lax.axis_index("c"), (1,)).astype(jnp.int32)

    def body(kind_ref, p_ref, r_ref, o_ref):
        o_ref[0] = (p_ref[0, 0].astype(_F32) + r_ref[0].astype(_F32)).astype(o_ref.dtype)

    return pl.pallas_call(
        body, name=name,
        grid_spec=pltpu.PrefetchScalarGridSpec(
            num_scalar_prefetch=1, grid=(4, rows // tr),
            in_specs=[pl.BlockSpec((1, 1, tr, cols), lambda ch, i, kind_ref: (ch, kind_ref[0], i, 0)),
                      pl.BlockSpec((1, tr, cols), lambda ch, i, kind_ref: (ch, i, 0))],
            out_specs=pl.BlockSpec((1, tr, cols), lambda ch, i, kind_ref: (ch, i, 0))),
        out_shape=jax.ShapeDtypeStruct(received.shape, received.dtype),
        compiler_params=pltpu.CompilerParams(dimension_semantics=("arbitrary", "arbitrary")),
    )(kind, p4, received)


def _all_reduce_small(name, v):
    rows = v.shape[0]

    def body(v_ref, out_ref, land_ref, send_sems, recv_sems):
        x, y, c = _position()
        me = _linear(x, y, c)
        peers = _peers(x, y, c)
        land_ref[me] = v_ref[...]
        sends = [pltpu.make_async_remote_copy(
            src_ref=v_ref, dst_ref=land_ref.at[me], send_sem=send_sems.at[k], recv_sem=recv_sems.at[k],
            device_id=peer, device_id_type=_MESH) for k, peer in enumerate(peers)]
        for cp in sends:
            cp.start()
        for k, peer in enumerate(peers):
            pltpu.make_async_remote_copy(
                src_ref=v_ref, dst_ref=land_ref.at[_linear(*peer)], send_sem=send_sems.at[k], recv_sem=recv_sems.at[k],
                device_id=peer, device_id_type=_MESH).wait_recv()
        for cp in sends:
            cp.wait_send()
        total = land_ref[0]
        for s in range(1, N_DEV):
            total = total + land_ref[s]
        out_ref[...] = total

    return pl.pallas_call(
        body, name=name, out_shape=jax.ShapeDtypeStruct(v.shape, _F32),
        in_specs=[pl.BlockSpec(memory_space=pltpu.VMEM)], out_specs=pl.BlockSpec(memory_space=pltpu.VMEM),
        scratch_shapes=[pltpu.VMEM((N_DEV, rows, V7X_LANES), _F32), pltpu.SemaphoreType.DMA((7,)), pltpu.SemaphoreType.DMA((7,))],
    )(v)


def _adamw(name, w, slots, m, v):
    rows, cols = w.shape
    n_slots = slots.shape[0]
    tr = _pick(rows, (176, 144, 128, 64, 32, 16, 8))

    def body(w_ref, s_ref, m_ref, v_ref, g_ref, d_ref, nm_ref, nv_ref):
        g = s_ref[0].astype(_F32)
        for k in range(1, n_slots):
            g = g + s_ref[k].astype(_F32)
        nm = ADAM_B1 * m_ref[...] + (1.0 - ADAM_B1) * g
        nv = ADAM_B2 * v_ref[...] + (1.0 - ADAM_B2) * (g * g)
        m_hat = nm / (1.0 - ADAM_B1 ** ADAM_STEP)
        v_hat = nv / (1.0 - ADAM_B2 ** ADAM_STEP)
        g_ref[...] = g
        d_ref[...] = -ADAM_LR * (m_hat / (jnp.sqrt(v_hat) + ADAM_EPS) + ADAM_WD * w_ref[...])
        nm_ref[...] = nm
        nv_ref[...] = nv

    spec = pl.BlockSpec((tr, cols), lambda i: (i, 0))
    blocks = 7 * tr * cols * 4 + _nbytes((n_slots, tr, cols), slots.dtype)
    return pl.pallas_call(
        body, name=name, grid=(rows // tr,),
        in_specs=[spec, pl.BlockSpec((n_slots, tr, cols), lambda i: (0, i, 0)), spec, spec], out_specs=[spec] * 4,
        out_shape=[jax.ShapeDtypeStruct((rows, cols), _F32)] * 4,
        compiler_params=pltpu.CompilerParams(dimension_semantics=("arbitrary",), vmem_limit_bytes=_vmem_limit(blocks)),
    )(w, slots, m, v)


def _pad_rows(a, rows):
    return jnp.pad(a, ((0, rows - a.shape[0]), (0, 0)))


def _pack(parts):
    rows, spans, at = [], [], 0
    for p in parts:
        p = p.reshape(-1)
        r = -(-p.shape[0] // V7X_LANES)
        rows.append(jnp.pad(p, (0, r * V7X_LANES - p.shape[0])).reshape(r, V7X_LANES))
        spans.append((at, r, p.shape[0]))
        at += r
    packed = jnp.concatenate(rows, axis=0)
    return _pad_rows(packed, -(-at // V7X_SUBLANES) * V7X_SUBLANES), spans


def _unpack(packed, spans, shapes):
    return [packed[at:at + r].reshape(-1)[:size].reshape(shape) for (at, r, size), shape in zip(spans, shapes)]


def kernel(x, positions, w_in, conv_w, sinks, g_attn, g_conv, w_out, ln1_g, ln1_b, w_gate, w_up, w_down, ln2_g, ln2_b, loss_target, m_w_in, m_conv_w, m_sinks, m_g_attn, m_g_conv, m_w_out, m_ln1_g, m_ln1_b, m_w_gate, m_w_up, m_w_down, m_ln2_g, m_ln2_b, v_w_in, v_conv_w, v_sinks, v_g_attn, v_g_conv, v_w_out, v_ln1_g, v_ln1_b, v_w_gate, v_w_up, v_w_down, v_ln2_g, v_ln2_b):
    _, s, d = x.shape
    d_ff = N_DEV * w_gate.shape[2]
    dm = _Dims(s, d, d_ff)
    aw, cw, nq, inw = dm.aw, dm.cw, dm.nq, dm.inw
    x2 = x[0]
    pos = positions[0].reshape(s, 1)
    inv_freq = ROPE_THETA ** (-jnp.arange(0, ROT_DIM, 2, dtype=_F32) / ROT_DIM)
    invf = jnp.tile(inv_freq, V7X_LANES // (ROT_DIM // 2)).reshape(1, V7X_LANES)

    conv_cols = conv_w.shape[2]
    sh_in, sh_out = w_in[0].T.astype(_CDT), w_out[0].astype(_CDT)
    sh_gate, sh_up, sh_down = w_gate[0].T.astype(_CDT), w_up[0].T.astype(_CDT), w_down[0].astype(_CDT)
    r_in, r_out, r_ff = sh_in.shape[0], sh_out.shape[0], sh_gate.shape[0]
    q_ff = r_ff // 4
    assert q_ff % 16 == 0
    def prepare_body(x_ref, pos_ref, invf_ref, xc_ref, rope_ref):
        xc_ref[...] = x_ref[...].astype(_CDT)
        cos, sgn = _rope_tables(pos_ref[...], invf_ref[...])
        rope_ref[:, 0:V7X_LANES] = cos
        rope_ref[:, V7X_LANES:2 * V7X_LANES] = sgn

    x_c, rope, w_in_t, conv_all = _row_kernel(
        "prepare_gather_w_in", prepare_body, [x2, pos], [invf], [((s, d), _CDT), ((s, 2 * V7X_LANES), _F32)], [],
        comm=[_gather_op([(sh_in, None, 0, r_in), (_pad_rows(conv_w[0], 16), None, 0, 16)])])
    conv_full = conv_all.reshape(N_DEV, 16, conv_cols)[:, :3, :].transpose(1, 0, 2).reshape(3, cw)
    conv_w8 = _pad_rows(conv_full, V7X_SUBLANES)

    tm = _pick(s, (1024, 512, 256, 128))
    tm2 = _pick(s, (2048, 1024, 512, 256, 128))
    tr = _pick(s, (512, 256, 128))
    tn_in = _pick(inw, (512, 256, 128))
    tn_ff = _pick(d_ff, (512, 256, 128))

    proj, w_out_f, w_gate_t = _matmul(
        "proj", [[(x_c, w_in_t, "nt")]], s, inw, d, tm2, tn_in, d, [],
        [((s, inw), _F32, (tm2, tn_in), _tile_ij)], _store_epilogue,
        comm=[_gather_op([(sh_out, None, 0, r_out), (sh_gate, None, 0, 2 * q_ff)])])
    mixed, attn, lse, y_conv, qk_rot, w_gate_t, w_up_t = _mixer_fwd(
        dm, proj, rope, sinks, g_attn, g_conv, conv_w8,
        comm=[_gather_op([(sh_gate, w_gate_t, 2 * q_ff, r_ff), (sh_up, None, 0, 2 * q_ff)])])

    def residual_epilogue(accs, ex, out, first):
        out[0][...] = DEEPNORM_ALPHA * ex[0][...] + accs[0]

    tn_d = _pick(d, (512,))
    r1, w_up_t = _matmul(
        "out_proj", [[(mixed, w_out_f, "nn")]], s, d, d, tm, tn_d, d, [(x2, (tm, tn_d), _tile_ij)],
        [((s, d), _F32, (tm, tn_d), _tile_ij)], residual_epilogue,
        comm=[_gather_op([(sh_up, w_up_t, 2 * q_ff, r_ff)])])
    h1, h1_c = _ln1_fwd_rows(r1, ln1_g, ln1_b)

    def swiglu_epilogue(accs, ex, out, first):
        gate_v, up_v = accs
        out[0][...] = gate_v
        out[1][...] = up_v
        out[2][...] = (gate_v * jax.nn.sigmoid(gate_v) * up_v).astype(_CDT)

    gate, up, act, w_down_f = _matmul(
        "gate_up", [[(h1_c, w_gate_t, "nt")], [(h1_c, w_up_t, "nt")]], s, d_ff, d, tm, tn_ff, d, [],
        [((s, d_ff), _F32, (tm, tn_ff), _tile_ij), ((s, d_ff), _F32, (tm, tn_ff), _tile_ij),
         ((s, d_ff), _CDT, (tm, tn_ff), _tile_ij)], swiglu_epilogue,
        comm=[_gather_op([(sh_down, None, 0, r_ff)])])

    (r2,) = _matmul("down", [[(act, w_down_f, "nn")]], s, d, d_ff, tm, tn_d, d_ff, [(h1, (tm, tn_d), _tile_ij)],
                    [((s, d), _F32, (tm, tn_d), _tile_ij)], residual_epilogue)
    dr2, dr2_c, loss_acc, d_ln2_g, d_ln2_b = _ln2_loss_bwd(r2, loss_target[0], ln2_g, ln2_b)

    def swiglu_bwd_epilogue(accs, ex, out, first):
        gate_v, up_v = ex[0][...], ex[1][...]
        sig = jax.nn.sigmoid(gate_v)
        out[0][...] = (accs[0] * up_v * (sig * (1.0 + gate_v * (1.0 - sig)))).astype(_CDT)
        out[1][...] = (accs[0] * (gate_v * sig)).astype(_CDT)

    dgate, dup = _matmul(
        "dact", [[(dr2_c, w_down_f, "nt")]], s, d_ff, d, tm2, tn_ff, d,
        [(gate, (tm2, tn_ff), _tile_ij), (up, (tm2, tn_ff), _tile_ij)],
        [((s, d_ff), _CDT, (tm2, tn_ff), _tile_ij), ((s, d_ff), _CDT, (tm2, tn_ff), _tile_ij)], swiglu_bwd_epilogue)
    def weight_grad(name, a, b, comm=()):
        rows = a.shape[1]
        tw, tn_w = _pick(rows, (512, 256, 128)), _pick(d, (1024, 512))
        return _matmul(name, [[(a, b, "tn")]], rows, d, s, tw, tn_w, s, [],
                       [((rows, d), _CDT, (tw, tn_w), _tile_ij)], _store_epilogue, comm=comm, j_outer=True)

    (dw_down,) = weight_grad("dw_down", act, dr2_c)
    dw_gate_t, x_down = weight_grad("dw_gate", dgate, h1_c, comm=[_exchange_op([dw_down])])
    q_down = _pair_sum("chip_sum_w_down", dw_down, x_down)
    dw_up_t, l_down, x_gate = weight_grad(
        "dw_up", dup, h1_c, comm=[_chip_send_op([(q_down, None, 0, 2 * q_ff)]), _exchange_op([dw_gate_t])])
    q_gate = _pair_sum("chip_sum_w_gate", dw_gate_t, x_gate)

    tn_h = _pick(d, (512,))
    dh1, l_down, l_gate, x_up = _matmul(
        "dh1", [[(dgate, w_gate_t, "nn"), (dup, w_up_t, "nn")]], s, d, d_ff, tr, tn_h, d_ff,
        [(dr2, (tr, tn_h), _tile_ij)], [((s, d), _F32, (tr, tn_h), _tile_ij)], residual_epilogue,
        comm=[_chip_send_op([(q_down, l_down, 2 * q_ff, r_ff), (q_gate, None, 0, r_ff)]), _exchange_op([dw_up_t])])
    q_up = _pair_sum("chip_sum_w_up", dw_up_t, x_up)
    dr1, dr1_c, d_ln1_g, d_ln1_b = _ln1_bwd_rows(dh1, r1, ln1_g)
    (dmixed,) = _matmul("dmixed", [[(dr1_c, w_out_f, "nt")]], s, d, d, tm2, tn_d, d, [],
                        [((s, d), _F32, (tm2, tn_d), _tile_ij)], _store_epilogue)
    (dw_out,) = weight_grad("dw_out", mixed, dr1_c)
    dproj, dkv, d_g_attn, d_g_conv, d_sinks, d_conv8, l_up, x_out = _mixer_bwd(
        dm, proj, rope, sinks, g_attn, g_conv, conv_w8, dmixed, attn, lse, y_conv, qk_rot,
        comm=[_chip_send_op([(q_up, None, 0, r_ff)]), _exchange_op([dw_out])])
    dproj = _patch_columns("dproj_kv", dproj, dkv, dm.o_k)
    q_out = _pair_sum("chip_sum_w_out", dw_out, x_out)
    dw_in_t, l_out = weight_grad("dw_in", dproj, x_c, comm=[_chip_send_op([(q_out, None, 0, r_out)])])
    (x_in,) = _comm_kernel("exchange_w_in", [_exchange_op([dw_in_t])])
    q_in = _pair_sum("chip_sum_w_in", dw_in_t, x_in)

    grad_x, l_in = _matmul("dx", [[(dproj, w_in_t, "nn")]], s, d, inw, tm, tn_d, inw,
                           [(dr1, (tm, tn_d), _tile_ij)], [((s, d), _F32, (tm, tn_d), _tile_ij)], residual_epilogue,
                           comm=[_chip_send_op([(q_in, None, 0, r_in)])])

    small_parts = [d_conv8[:3], d_sinks, d_g_attn, d_g_conv, d_ln1_g, d_ln1_b, d_ln2_g, d_ln2_b, loss_acc[0:1, 0:1]]
    packed, spans = _pack(small_parts)
    reduced = _unpack(_all_reduce_small("reduce_small", packed), spans, [p.shape for p in small_parts])
    g_conv_full, g_sinks, g_g_attn, g_g_conv, g_ln1_g, g_ln1_b, g_ln2_g, g_ln2_b, loss_sum = reduced
    me = _linear(*_position())
    g_conv_w = lax.dynamic_slice(g_conv_full, (0, me * conv_cols), (3, conv_cols))
    loss = loss_sum[0, 0]

    big = {"w_in": (w_in[0].T, l_in, m_w_in[0].T, v_w_in[0].T), "w_out": (w_out[0], l_out, m_w_out[0], v_w_out[0]),
           "w_gate": (w_gate[0].T, l_gate, m_w_gate[0].T, v_w_gate[0].T),
           "w_up": (w_up[0].T, l_up, m_w_up[0].T, v_w_up[0].T), "w_down": (w_down[0], l_down, m_w_down[0], v_w_down[0])}
    res = {nm: tuple(_adamw(f"adamw_{nm}", w, slots, m, v)) for nm, (w, slots, m, v) in big.items()}
    for nm in ("w_in", "w_gate", "w_up"):
        res[nm] = tuple(a.T for a in res[nm])
    small_names = ["conv_w", "sinks", "g_attn", "g_conv", "ln1_g", "ln1_b", "ln2_g", "ln2_b"]
    small_w = [conv_w, sinks, g_attn, g_conv, ln1_g, ln1_b, ln2_g, ln2_b]
    small_g = [g_conv_w[None], g_sinks, g_g_attn, g_g_conv, g_ln1_g, g_ln1_b, g_ln2_g, g_ln2_b]
    small_m = [m_conv_w, m_sinks, m_g_attn, m_g_conv, m_ln1_g, m_ln1_b, m_ln2_g, m_ln2_b]
    small_v = [v_conv_w, v_sinks, v_g_attn, v_g_conv, v_ln1_g, v_ln1_b, v_ln2_g, v_ln2_b]
    pw, sp = _pack(small_w)
    pg, _ = _pack(small_g)
    pm, _ = _pack(small_m)
    pv, _ = _pack(small_v)
    shapes = [w.shape for w in small_w]
    _, sd, sm, sv = [_unpack(p, sp, shapes) for p in _adamw("adamw_small", pw, pg[None], pm, pv)]
    for i, nm in enumerate(small_names):
        res[nm] = (small_g[i].reshape(shapes[i]), sd[i], sm[i], sv[i])

    order = ["w_in", "conv_w", "sinks", "g_attn", "g_conv", "w_out", "ln1_g", "ln1_b", "w_gate", "w_up", "w_down", "ln2_g", "ln2_b"]

    def lead(a, nm):
        return a[None] if nm in big else a

    return (loss, grad_x[None],
            *[lead(res[nm][0], nm) for nm in order], *[lead(res[nm][1], nm) for nm in order],
            *[lead(res[nm][2], nm) for nm in order], *[lead(res[nm][3], nm) for nm in order])
```

```python
import functools

import jax
import jax.numpy as jnp
from jax import lax
from jax.experimental import pallas as pl
from jax.experimental.pallas import tpu as pltpu

_F32 = jnp.float32
_CDT = jnp.bfloat16

HEAD_DIM = 64
WINDOW = 128
N_KV_HEADS = 4
KV_WIDTH = N_KV_HEADS * HEAD_DIM
ROT_DIM = HEAD_DIM // 4
ROPE_THETA = 500000.0
ATTN_SCALE = HEAD_DIM ** -0.5
DEPTH = 1
DEEPNORM_ALPHA = (2 * DEPTH) ** 0.25
LN_EPS = 1e-5
RMS_EPS = 1e-6
ADAM_LR = 0.001
ADAM_B1 = 0.9
ADAM_B2 = 0.999
ADAM_EPS = 1e-08
ADAM_WD = 0.01
ADAM_STEP = 10
N_DEV = 8
MASKED = -1e30

MIB = 1024 * 1024
V7X_VMEM_BYTES = 64 * MIB
V7X_LANES = 128
V7X_SUBLANES = 8
BODY_TEMPORARIES_BYTES = 16 * MIB
VMEM_LIMIT_FLOOR_BYTES = 32 * MIB
VMEM_LIMIT_CEILING_BYTES = V7X_VMEM_BYTES - 8 * MIB
_MESH = pl.DeviceIdType.MESH
_ANY = pl.BlockSpec(memory_space=pl.ANY)


def _vmem_limit(block_bytes, scratch_bytes=0):
    want = 2 * block_bytes + scratch_bytes + BODY_TEMPORARIES_BYTES
    return int(min(max(want, VMEM_LIMIT_FLOOR_BYTES), VMEM_LIMIT_CEILING_BYTES))


def _nbytes(shape, dtype):
    n = 1
    for s in shape:
        n *= s
    return n * jnp.dtype(dtype).itemsize


def _pick(n, candidates):
    for c in candidates:
        if n % c == 0:
            return c
    raise ValueError(f"no tile of {candidates} divides {n}")


_DOT_DIMS = {"nn": ((1,), (0,)), "nt": ((1,), (1,)), "tn": ((0,), (0,))}


def _dot(a, b, mode):
    return lax.dot_general(a.astype(_CDT), b.astype(_CDT), (_DOT_DIMS[mode], ((), ())),
                           preferred_element_type=_F32)


def _accumulate(ref, val, first):
    @pl.when(first)
    def _():
        ref[...] = val

    @pl.when(jnp.logical_not(first))
    def _():
        ref[...] += val


class _Comm:
    def __init__(self, inputs, outputs, aliases, sems, start, finish, middle=None):
        self.inputs, self.outputs, self.aliases, self.sems = inputs, outputs, aliases, sems
        self.start, self.finish, self.middle = start, finish, middle


def _middle_step(n_steps):
    return (2 * n_steps) // 3


class _CommArgs:
    def __init__(self, comms, n_in_before, n_out_before):
        self.comms, self.operands, self.out_shape, self.aliases, self.sems, self.at = comms, [], [], {}, [], []
        for cm in comms:
            self.at.append((len(self.operands), len(self.out_shape), len(self.sems)))
            for i_in, i_out in cm.aliases.items():
                self.aliases[n_in_before + len(self.operands) + i_in] = n_out_before + len(self.out_shape) + i_out
            self.operands += cm.inputs
            self.out_shape += cm.outputs
            self.sems += cm.sems

    def _each(self, in_refs, out_refs, sem_refs):
        for cm, (i0, o0, s0) in zip(self.comms, self.at):
            yield cm, (in_refs[i0:i0 + len(cm.inputs)], out_refs[o0:o0 + len(cm.outputs)], sem_refs[s0:s0 + len(cm.sems)])

    def start(self, in_refs, out_refs, sem_refs):
        for cm, refs in self._each(in_refs, out_refs, sem_refs):
            cm.start(*refs)

    def finish(self, in_refs, out_refs, sem_refs):
        for cm, refs in self._each(in_refs, out_refs, sem_refs):
            cm.finish(*refs)

    @property
    def has_middle(self):
        return any(cm.middle is not None for cm in self.comms)

    def middle(self, in_refs, out_refs, sem_refs):
        for cm, refs in self._each(in_refs, out_refs, sem_refs):
            if cm.middle is not None:
                cm.middle(*refs)


def _matmul(name, groups, m, n, k, tm, tn, tk, extras, outs, epilogue, comm=(), j_outer=False):
    assert m % tm == 0 and n % tn == 0 and k % tk == 0, (name, m, n, k, tm, tn, tk)
    nk = k // tk
    terms = [t for g in groups for t in g]
    operands, in_specs, block_bytes = [], [], 0

    def spec(blk, imap):
        return pl.BlockSpec(blk, (lambda g0, g1, kk: imap(g1, g0, kk)) if j_outer else imap)

    for a, b, mode in terms:
        assert a.shape == ((k, m) if mode == "tn" else (m, k)), (name, a.shape, mode)
        assert b.shape == ((n, k) if mode == "nt" else (k, n)), (name, b.shape, mode)
        if mode == "tn":
            a_blk, a_map = (tk, tm), (lambda i, j, kk: (kk, i))
        else:
            a_blk, a_map = (tm, tk), (lambda i, j, kk: (i, kk))
        if mode == "nt":
            b_blk, b_map = (tn, tk), (lambda i, j, kk: (j, kk))
        else:
            b_blk, b_map = (tk, tn), (lambda i, j, kk: (kk, j))
        operands += [a, b]
        in_specs += [spec(a_blk, a_map), spec(b_blk, b_map)]
        block_bytes += _nbytes(a_blk, a.dtype) + _nbytes(b_blk, b.dtype)
    for arr, blk, imap in extras:
        operands.append(arr)
        in_specs.append(spec(blk, lambda i, j, kk, imap=imap: imap(i, j)))
        block_bytes += _nbytes(blk, arr.dtype)
    out_shape, out_specs = [], []
    for shape, dtype, blk, imap in outs:
        out_shape.append(jax.ShapeDtypeStruct(shape, dtype))
        out_specs.append(spec(blk, lambda i, j, kk, imap=imap: imap(i, j)))
        block_bytes += _nbytes(blk, dtype)
    n_terms, n_extra, n_out, n_groups = len(terms), len(extras), len(outs), len(groups)
    scratch = [pltpu.VMEM((tm, tn), _F32) for _ in range(n_groups)] if nk > 1 else []
    ca = _CommArgs(list(comm), len(operands), n_out)
    n_cin, n_cout, n_acc = len(ca.operands), len(ca.out_shape), len(scratch)
    tiles = (m // tm, n // tn)
    grid = (tiles[1], tiles[0], nk) if j_outer else (tiles[0], tiles[1], nk)

    def body(*refs):
        refs = list(refs)
        term_refs = [refs.pop(0) for _ in range(2 * n_terms)]
        extra_refs = [refs.pop(0) for _ in range(n_extra)]
        cin_refs = [refs.pop(0) for _ in range(n_cin)]
        out_refs = [refs.pop(0) for _ in range(n_out)]
        cout_refs = [refs.pop(0) for _ in range(n_cout)]
        acc_refs = [refs.pop(0) for _ in range(n_acc)]
        sem_refs = refs
        g0, g1, kk = pl.program_id(0), pl.program_id(1), pl.program_id(2)
        first = jnp.logical_and(g0 == 0, g1 == 0)
        if comm:
            @pl.when(jnp.logical_and(first, kk == 0))
            def _():
                ca.start(cin_refs, cout_refs, sem_refs)
        if ca.has_middle:
            step = (g0 * grid[1] + g1) * nk + kk

            @pl.when(step == _middle_step(grid[0] * grid[1] * nk))
            def _():
                ca.middle(cin_refs, cout_refs, sem_refs)
        partial, t = [], 0
        for g in groups:
            s = None
            for _, _, mode in g:
                d = _dot(term_refs[2 * t][...], term_refs[2 * t + 1][...], mode)
                s = d if s is None else s + d
                t += 1
            partial.append(s)
        if nk == 1:
            epilogue(partial, extra_refs, out_refs, first)
        else:
            for acc, p in zip(acc_refs, partial):
                _accumulate(acc, p, kk == 0)

            @pl.when(kk == nk - 1)
            def _():
                epilogue([acc[...] for acc in acc_refs], extra_refs, out_refs, first)
        if comm:
            @pl.when(jnp.logical_and(jnp.logical_and(g0 == grid[0] - 1, g1 == grid[1] - 1), kk == nk - 1))
            def _():
                ca.finish(cin_refs, cout_refs, sem_refs)

    res = pl.pallas_call(
        body, name=name, grid=grid,
        in_specs=in_specs + [_ANY] * n_cin, out_specs=out_specs + [_ANY] * n_cout,
        out_shape=out_shape + ca.out_shape, scratch_shapes=scratch + ca.sems, input_output_aliases=ca.aliases,
        compiler_params=pltpu.CompilerParams(
            dimension_semantics=("arbitrary", "arbitrary", "arbitrary"),
            vmem_limit_bytes=_vmem_limit(block_bytes, n_groups * tm * tn * 4 if nk > 1 else 0)),
    )(*operands, *ca.operands)
    return list(res[:n_out]) + list(res[n_out:])


def _store_epilogue(accs, extra_refs, out_refs, first):
    for acc, ref in zip(accs, out_refs):
        ref[...] = acc.astype(ref.dtype)


def _tile_ij(i, j):
    return (i, j)


def _row_i(i, j):
    return (i, 0)


def _whole(i, j):
    return (0, 0)


def _mean(v):
    return jnp.mean(v, axis=-1, keepdims=True)


def _ln_fwd(r, g, b):
    xc = r - _mean(r)
    rstd = lax.rsqrt(_mean(xc * xc) + LN_EPS)
    xhat = xc * rstd
    return xhat * g + b, xhat, rstd


def _ln_bwd(dy, xhat, rstd, g):
    dxh = dy * g
    dr = rstd * (dxh - _mean(dxh) - xhat * _mean(dxh * xhat))
    return dr, jnp.sum(dy * xhat, axis=0, keepdims=True), jnp.sum(dy, axis=0, keepdims=True)


def _rms_fwd(a, g):
    rstd = lax.rsqrt(_mean(a * a) + RMS_EPS)
    return a * rstd * g


def _rms_bwd(dm, a, g):
    rstd = lax.rsqrt(_mean(a * a) + RMS_EPS)
    nhat = a * rstd
    dn = dm * g
    da = rstd * (dn - nhat * _mean(dn * nhat))
    return da, jnp.sum(dm * nhat, axis=0, keepdims=True)


def _lane(shape):
    return lax.broadcasted_iota(jnp.int32, shape, 1)


def _row(shape):
    return lax.broadcasted_iota(jnp.int32, shape, 0)


def _rope_tables(pos, invf):
    ang = pos.astype(_F32) * invf
    lane = _lane(ang.shape)
    in_rot = (lane % HEAD_DIM) < ROT_DIM
    first = (lane % ROT_DIM) < ROT_DIM // 2
    cos = jnp.where(in_rot, jnp.cos(ang), 1.0)
    sin = jnp.sin(ang)
    sgn = jnp.where(in_rot, jnp.where(first, -sin, sin), 0.0)
    return cos, sgn


def _rope(t, cos, sgn, sign):
    half = ROT_DIM // 2
    first = (_lane(t.shape) % ROT_DIM) < half
    partner = jnp.where(first, pltpu.roll(t, V7X_LANES - half, 1), pltpu.roll(t, half, 1))
    return t * cos + partner * (sgn * sign)


def _dup_head(t, h):
    g = t[:, 128 * (h // 2):128 * (h // 2) + 128]
    r = pltpu.roll(g, HEAD_DIM, 1)
    lo = _lane(g.shape) < HEAD_DIM
    return jnp.where(lo, g, r) if h % 2 == 0 else jnp.where(lo, r, g)


def _fold_halves(t):
    return t + pltpu.roll(t, HEAD_DIM, 1)


def _halves(t):
    lo = _lane(t.shape) < HEAD_DIM
    zero = jnp.zeros_like(t)
    return jnp.where(lo, t, zero), jnp.where(lo, zero, t)


def _band_mask(n_heads, n_keys, first_block):
    shape = (n_heads * WINDOW, n_keys)
    i = jnp.bitwise_and(_row(shape), WINDOW - 1)
    j = _lane(shape)
    valid = jnp.logical_and(j >= i + 1, j <= i + WINDOW)
    if first_block is not None:
        valid = jnp.logical_and(valid, jnp.logical_or(j >= WINDOW, jnp.logical_not(first_block)))
    return valid


def _stack_heads(pairs):
    return jnp.concatenate([half for t in pairs for half in _halves(t.astype(_CDT))], axis=0)


def _unstack_heads(t, n_pairs):
    lo = _lane((WINDOW, 128)) < HEAD_DIM
    return [jnp.where(lo, t[2 * WINDOW * i:2 * WINDOW * i + WINDOW], t[2 * WINDOW * i + WINDOW:2 * WINDOW * (i + 1)])
            for i in range(n_pairs)]


def _per_head(values):
    n_rows = len(values) * WINDOW
    block = jnp.right_shift(_row((n_rows, 1)), WINDOW.bit_length() - 1)
    out = jnp.zeros((n_rows, 1), _F32)
    for k, v in enumerate(values):
        out = jnp.where(block == k, v, out)
    return out


def _shift_down(z, halo, k):
    out = pltpu.roll(z, k, 0)
    r = _row(z.shape)
    for t in range(k):
        out = jnp.where(r == t, halo[V7X_SUBLANES - k + t:V7X_SUBLANES - k + t + 1, :], out)
    return out


def _shift_up(z, halo, k):
    rows = z.shape[0]
    out = pltpu.roll(z, rows - k, 0)
    r = _row(z.shape)
    for t in range(k):
        out = jnp.where(r == rows - k + t, halo[t:t + 1, :], out)
    return out


class _Dims:
    def __init__(self, s, d, d_ff):
        self.s, self.d, self.d_ff = s, d, d_ff
        self.aw = d // 2
        self.cw = d - self.aw
        self.nq = self.aw // HEAD_DIM
        self.group = self.nq // N_KV_HEADS
        assert self.group % 2 == 0, "a 128-lane pair of query heads must share its kv head"
        self.inw = self.aw + 2 * KV_WIDTH + 3 * self.cw
        self.o_k = self.aw
        self.o_v = self.aw + KV_WIDTH
        self.o_cg = self.aw + 2 * KV_WIDTH
        self.o_bg = self.o_cg + self.cw
        self.o_u = self.o_bg + self.cw
        self.nb = s // WINDOW
        assert s % WINDOW == 0


def _carrying(body, n_in, n_out, n_steps, ca, n_scratch=0):
    n_cin, n_cout = len(ca.operands), len(ca.out_shape)

    def wrapped(*refs):
        refs = list(refs)
        in_refs = [refs.pop(0) for _ in range(n_in)]
        cin_refs = [refs.pop(0) for _ in range(n_cin)]
        out_refs = [refs.pop(0) for _ in range(n_out)]
        cout_refs = [refs.pop(0) for _ in range(n_cout)]
        scratch_refs = [refs.pop(0) for _ in range(n_scratch)]
        if ca.comms:
            @pl.when(pl.program_id(0) == 0)
            def _():
                ca.start(cin_refs, cout_refs, refs)
        if ca.has_middle:
            @pl.when(pl.program_id(0) == _middle_step(n_steps))
            def _():
                ca.middle(cin_refs, cout_refs, refs)
        body(*in_refs, *out_refs, *scratch_refs)
        if ca.comms:
            @pl.when(pl.program_id(0) == n_steps - 1)
            def _():
                ca.finish(cin_refs, cout_refs, refs)

    return wrapped


def _row_kernel(name, body, rows_in, vecs_in, rows_out, vecs_out, comm=()):
    s = rows_in[0].shape[0]
    tr = _pick(s, (256, 128))
    row = lambda a: pl.BlockSpec((tr, a[1] if isinstance(a, tuple) else a.shape[1]), lambda i: (i, 0))
    vec = lambda shape: pl.BlockSpec(tuple(shape), lambda i: (0, 0))
    n_in, n_out = len(rows_in) + len(vecs_in), len(rows_out) + len(vecs_out)
    ca = _CommArgs(list(comm), n_in, n_out)
    blocks = sum(_nbytes((tr, a.shape[1]), a.dtype) for a in rows_in) + sum(_nbytes((tr, sh[1]), dt) for sh, dt in rows_out)
    res = pl.pallas_call(
        _carrying(body, n_in, n_out, s // tr, ca), name=name, grid=(s // tr,),
        in_specs=[row(a) for a in rows_in] + [vec(v.shape) for v in vecs_in] + [_ANY] * len(ca.operands),
        out_specs=[row(sh) for sh, _ in rows_out] + [vec(sh) for sh, _ in vecs_out] + [_ANY] * len(ca.out_shape),
        out_shape=[jax.ShapeDtypeStruct(sh, dt) for sh, dt in list(rows_out) + list(vecs_out)] + ca.out_shape,
        scratch_shapes=ca.sems, input_output_aliases=ca.aliases,
        compiler_params=pltpu.CompilerParams(dimension_semantics=("arbitrary",), vmem_limit_bytes=_vmem_limit(blocks)),
    )(*rows_in, *vecs_in, *ca.operands)
    return list(res)


def _ln2_loss_bwd(r2, target, gain, bias, comm=()):
    s, d = r2.shape

    def body(r_ref, t_ref, g_ref, b_ref, dr_ref, drc_ref, loss_ref, dg_ref, db_ref):
        first = pl.program_id(0) == 0
        yv, xhat, rstd = _ln_fwd(r_ref[...], g_ref[...], b_ref[...])
        err = yv - t_ref[...]
        dr2, dg, db = _ln_bwd(err * (1.0 / d), xhat, rstd, g_ref[...])
        dr_ref[...] = dr2
        drc_ref[...] = dr2.astype(_CDT)
        _accumulate(loss_ref, jnp.zeros(loss_ref.shape, _F32) + 0.5 * jnp.sum(err * err) * (1.0 / d), first)
        _accumulate(dg_ref, dg, first)
        _accumulate(db_ref, db, first)

    return _row_kernel("ln2_loss_bwd", body, [r2, target], [gain, bias], [((s, d), _F32), ((s, d), _CDT)],
                       [((V7X_SUBLANES, V7X_LANES), _F32), ((1, d), _F32), ((1, d), _F32)], comm)


def _ln1_fwd_rows(r1, gain, bias, comm=()):
    s, d = r1.shape

    def body(r_ref, g_ref, b_ref, h_ref, hc_ref):
        h1, _, _ = _ln_fwd(r_ref[...], g_ref[...], b_ref[...])
        h_ref[...] = h1
        hc_ref[...] = h1.astype(_CDT)

    return _row_kernel("ln1", body, [r1], [gain, bias], [((s, d), _F32), ((s, d), _CDT)], [], comm)


def _ln1_bwd_rows(dh1, r1, gain, comm=()):
    s, d = dh1.shape

    def body(dh_ref, r_ref, g_ref, dr_ref, drc_ref, dg_ref, db_ref):
        first = pl.program_id(0) == 0
        _, xhat, rstd = _ln_fwd(r_ref[...], g_ref[...], 0.0)
        dr1, dg, db = _ln_bwd(dh_ref[...], xhat, rstd, g_ref[...])
        dr_ref[...] = dr1
        drc_ref[...] = dr1.astype(_CDT)
        _accumulate(dg_ref, dg, first)
        _accumulate(db_ref, db, first)

    return _row_kernel("ln1_bwd", body, [dh1, r1], [gain], [((s, d), _F32), ((s, d), _CDT)],
                       [((1, d), _F32), ((1, d), _F32)], comm)


def _mixer_fwd(dm, proj, rope, sinks, g_attn, g_conv, conv_w8, comm=()):
    s, d, aw, cw, nq, inw, nb = dm.s, dm.d, dm.aw, dm.cw, dm.nq, dm.inw, dm.nb

    def body(pp_ref, pc_ref, ropep_ref, ropec_ref, sinks_ref, ga_ref, gc_ref, cw_ref,
             mixed_ref, attn_ref, lse_ref, y_ref, qk_ref):
        n = pl.program_id(0)
        cos_c, sgn_c = ropec_ref[:, 0:V7X_LANES], ropec_ref[:, V7X_LANES:2 * V7X_LANES]
        cos_p, sgn_p = ropep_ref[:, 0:V7X_LANES], ropep_ref[:, V7X_LANES:2 * V7X_LANES]
        for g in range(KV_WIDTH // 128):
            qk_ref[:, aw + 128 * g:aw + 128 * g + 128] = _rope(
                pc_ref[:, dm.o_k + 128 * g:dm.o_k + 128 * g + 128], cos_c, sgn_c, 1.0).astype(qk_ref.dtype)
        for j in range(nq // 2):
            qk_ref[:, 128 * j:128 * j + 128] = _rope(pc_ref[:, 128 * j:128 * j + 128], cos_c, sgn_c, 1.0).astype(qk_ref.dtype)
        k_prev = jnp.concatenate([_rope(pp_ref[:, dm.o_k + 128 * g:dm.o_k + 128 * g + 128], cos_p, sgn_p, 1.0)
                                  for g in range(KV_WIDTH // 128)], axis=1)
        kk = jnp.concatenate([k_prev, qk_ref[:, aw:aw + KV_WIDTH].astype(_F32)], axis=0)
        vv = jnp.concatenate([pp_ref[:, dm.o_v:dm.o_v + KV_WIDTH], pc_ref[:, dm.o_v:dm.o_v + KV_WIDTH]], axis=0)
        group, pairs = dm.group, dm.group // 2
        valid = _band_mask(group, 2 * WINDOW, n == 0)
        for h in range(N_KV_HEADS):
            k2, v2 = _dup_head(kk, h).astype(_CDT), _dup_head(vv, h).astype(_CDT)
            q4 = _stack_heads([qk_ref[:, 128 * j:128 * j + 128] for j in range(pairs * h, pairs * (h + 1))])
            sc = jnp.where(valid, _dot(q4, k2, "nt") * ATTN_SCALE, MASKED)
            sink = _per_head([sinks_ref[0, group * h + r] for r in range(group)])
            mx = jnp.maximum(jnp.max(sc, axis=1, keepdims=True), sink)
            p = jnp.exp(sc - mx)
            den = jnp.sum(p, axis=1, keepdims=True) + jnp.exp(sink - mx)
            out = _unstack_heads(_dot(p / den, v2, "nn"), pairs)
            lse = mx + jnp.log(den)
            for r in range(group):
                lse_ref[:, group * h + r:group * h + r + 1] = lse[WINDOW * r:WINDOW * (r + 1)]
            for i in range(pairs):
                j = pairs * h + i
                attn_ref[:, 128 * j:128 * j + 128] = out[i]
        mixed_ref[:, 0:aw] = _rms_fwd(attn_ref[...], ga_ref[...]).astype(mixed_ref.dtype)

        z = pc_ref[:, dm.o_cg:dm.o_cg + cw] * pc_ref[:, dm.o_u:dm.o_u + cw]
        top = WINDOW - V7X_SUBLANES
        halo = pp_ref[top:WINDOW, dm.o_cg:dm.o_cg + cw] * pp_ref[top:WINDOW, dm.o_u:dm.o_u + cw]
        halo = jnp.where(n == 0, jnp.zeros_like(halo), halo)
        y = cw_ref[0:1, :] * _shift_down(z, halo, 2) + cw_ref[1:2, :] * _shift_down(z, halo, 1) + cw_ref[2:3, :] * z
        y_ref[...] = y
        conv = pc_ref[:, dm.o_bg:dm.o_bg + cw] * y
        mixed_ref[:, aw:d] = _rms_fwd(conv, gc_ref[...]).astype(mixed_ref.dtype)

    prev = lambda n: (jnp.maximum(n - 1, 0), 0)
    cur = lambda n: (n, 0)
    fixed = lambda n: (0, 0)
    blocks = 2 * WINDOW * inw * 4 + WINDOW * (d * 2 + aw * 4 + cw * 4 + nq * 4)
    ca = _CommArgs(list(comm), 8, 5)
    return pl.pallas_call(
        _carrying(body, 8, 5, nb, ca), name="mixer_fwd", grid=(nb,),
        in_specs=[pl.BlockSpec((WINDOW, inw), prev), pl.BlockSpec((WINDOW, inw), cur),
                  pl.BlockSpec((WINDOW, 2 * V7X_LANES), prev), pl.BlockSpec((WINDOW, 2 * V7X_LANES), cur),
                  pl.BlockSpec(memory_space=pltpu.SMEM),
                  pl.BlockSpec((1, aw), fixed), pl.BlockSpec((1, cw), fixed), pl.BlockSpec((V7X_SUBLANES, cw), fixed)]
        + [_ANY] * len(ca.operands),
        out_specs=[pl.BlockSpec((WINDOW, d), cur), pl.BlockSpec((WINDOW, aw), cur),
                   pl.BlockSpec((WINDOW, nq), cur), pl.BlockSpec((WINDOW, cw), cur),
                   pl.BlockSpec((WINDOW, aw + KV_WIDTH), cur)] + [_ANY] * len(ca.out_shape),
        out_shape=[jax.ShapeDtypeStruct((s, d), _CDT), jax.ShapeDtypeStruct((s, aw), _F32),
                   jax.ShapeDtypeStruct((s, nq), _F32), jax.ShapeDtypeStruct((s, cw), _F32),
                   jax.ShapeDtypeStruct((s, aw + KV_WIDTH), _CDT)] + ca.out_shape,
        scratch_shapes=ca.sems, input_output_aliases=ca.aliases,
        compiler_params=pltpu.CompilerParams(dimension_semantics=("arbitrary",), vmem_limit_bytes=_vmem_limit(blocks)),
    )(proj, proj, rope, rope, sinks, g_attn, g_conv, conv_w8, *ca.operands)


def _patch_columns(name, a, part, offset):
    s, pw = part.shape
    assert offset % pw == 0 and pw % V7X_LANES == 0
    tr = _pick(s, (512, 256, 128))

    def body(a_ref, p_ref, o_ref):
        del a_ref
        o_ref[...] = p_ref[...]

    return pl.pallas_call(
        body, name=name, grid=(s // tr,),
        in_specs=[_ANY, pl.BlockSpec((tr, pw), lambda i: (i, 0))],
        out_specs=pl.BlockSpec((tr, pw), lambda i: (i, offset // pw)),
        out_shape=jax.ShapeDtypeStruct(a.shape, a.dtype), input_output_aliases={0: 0},
        compiler_params=pltpu.CompilerParams(dimension_semantics=("arbitrary",)),
    )(a, part)


def _mixer_bwd(dm, proj, rope, sinks, g_attn, g_conv, conv_w8, dmixed, attn, lse, y, qk, comm=()):
    s, d, aw, cw, nq, inw, nb = dm.s, dm.d, dm.aw, dm.cw, dm.nq, dm.inw, dm.nb

    def body(pp_ref, pc_ref, pn_ref, ropep_ref, ropec_ref, dmc_ref, dmn_ref, ac_ref,
             lsec_ref, yc_ref, yn_ref, qkp_ref, qkc_ref, sinks_ref, ga_ref, gc_ref, cw_ref,
             dproj_ref, dkv_ref, dga_ref, dgc_ref, dsinks_ref, dcw_ref, dk_carry, dv_carry):
        n = pl.program_id(0)
        first = n == 0
        live = n < nb
        has_next = n < nb - 1
        cos_p, sgn_p = ropep_ref[:, 0:V7X_LANES], ropep_ref[:, V7X_LANES:2 * V7X_LANES]
        cos_c, sgn_c = ropec_ref[:, 0:V7X_LANES], ropec_ref[:, V7X_LANES:2 * V7X_LANES]

        @pl.when(first)
        def _():
            dk_carry[...] = jnp.zeros(dk_carry.shape, _F32)
            dv_carry[...] = jnp.zeros(dv_carry.shape, _F32)

        def write_kv(dk2, dv2, cos, sgn):
            lo = _lane((WINDOW, 128)) < HEAD_DIM
            for g in range(KV_WIDTH // 128):
                dk = jnp.where(lo, _fold_halves(dk2[2 * g]), _fold_halves(dk2[2 * g + 1]))
                dv = jnp.where(lo, _fold_halves(dv2[2 * g]), _fold_halves(dv2[2 * g + 1]))
                dkv_ref[:, 128 * g:128 * g + 128] = _rope(dk, cos, sgn, -1.0).astype(dkv_ref.dtype)
                dkv_ref[:, KV_WIDTH + 128 * g:KV_WIDTH + 128 * g + 128] = dv.astype(dkv_ref.dtype)

        @pl.when(jnp.logical_not(live))
        def _():
            write_kv([dk_carry[h] for h in range(N_KV_HEADS)], [dv_carry[h] for h in range(N_KV_HEADS)], cos_c, sgn_c)

        @pl.when(live)
        def _():
            block_step(pp_ref, pc_ref, pn_ref, dmc_ref, dmn_ref, ac_ref, lsec_ref, yc_ref, yn_ref, qkp_ref, qkc_ref,
                       sinks_ref, ga_ref, gc_ref, cw_ref, dproj_ref, dga_ref, dgc_ref, dsinks_ref, dcw_ref, dk_carry,
                       dv_carry, first, has_next, cos_p, sgn_p, cos_c, sgn_c, write_kv)

    def block_step(pp_ref, pc_ref, pn_ref, dmc_ref, dmn_ref, ac_ref, lsec_ref, yc_ref, yn_ref, qkp_ref, qkc_ref,
                   sinks_ref, ga_ref, gc_ref, cw_ref, dproj_ref, dga_ref, dgc_ref, dsinks_ref, dcw_ref, dk_carry,
                   dv_carry, first, has_next, cos_p, sgn_p, cos_c, sgn_c, write_kv):
        da_c, dga = _rms_bwd(dmc_ref[:, 0:aw], ac_ref[...], ga_ref[...])
        _accumulate(dga_ref, dga, first)
        kk = jnp.concatenate([qkp_ref[:, aw:aw + KV_WIDTH], qkc_ref[:, aw:aw + KV_WIDTH]], axis=0).astype(_F32)
        vv = jnp.concatenate([pp_ref[:, dm.o_v:dm.o_v + KV_WIDTH], pc_ref[:, dm.o_v:dm.o_v + KV_WIDTH]], axis=0)
        group, pairs = dm.group, dm.group // 2
        valid_c = _band_mask(group, 2 * WINDOW, first)
        dk_prev, dv_prev = [], []

        def stacked(q_ref, da, o_ref, lse_ref_, h):
            cols = [slice(128 * j, 128 * j + 128) for j in range(pairs * h, pairs * (h + 1))]
            q4 = _stack_heads([q_ref[:, c] for c in cols])
            do4 = _stack_heads([da[:, c] for c in cols])
            lo = _lane((WINDOW, 128)) < HEAD_DIM
            deltas = []
            for c in cols:
                prod = o_ref[:, c] * da[:, c]
                deltas += [jnp.sum(jnp.where(lo, prod, 0.0), axis=1, keepdims=True),
                           jnp.sum(jnp.where(lo, 0.0, prod), axis=1, keepdims=True)]
            lse4 = jnp.concatenate([lse_ref_[:, group * h + r:group * h + r + 1] for r in range(group)], axis=0)
            return q4, do4, lse4, deltas

        def scores_bwd(q4, do4, lse4, delta4, keys, vals, valid):
            sc = _dot(q4, keys, "nt") * ATTN_SCALE
            p = jnp.exp(jnp.where(valid, sc - lse4, MASKED))
            return p.astype(_CDT), (p * (_dot(do4, vals, "nt") - delta4) * ATTN_SCALE).astype(_CDT)

        for h in range(N_KV_HEADS):
            k2, v2 = _dup_head(kk, h).astype(_CDT), _dup_head(vv, h).astype(_CDT)
            q4, do4, lse4, deltas = stacked(qkc_ref, da_c, ac_ref, lsec_ref, h)
            delta4 = jnp.concatenate(deltas, axis=0)
            p, ds = scores_bwd(q4, do4, lse4, delta4, k2, v2, valid_c)
            for i, dq in enumerate(_unstack_heads(_dot(ds, k2, "nn"), pairs)):
                j = pairs * h + i
                dproj_ref[:, 128 * j:128 * j + 128] = _rope(dq, cos_c, sgn_c, -1.0).astype(dproj_ref.dtype)
            dk = _dot(ds, q4, "tn")
            dv = _dot(p, do4, "tn")
            dk_prev.append(dk_carry[h] + dk[0:WINDOW])
            dv_prev.append(dv_carry[h] + dv[0:WINDOW])
            dk_carry[h] = dk[WINDOW:2 * WINDOW]
            dv_carry[h] = dv[WINDOW:2 * WINDOW]
            heads = slice(group * h, group * (h + 1))
            sink_row, delta_heads = jnp.zeros((1, group), _F32), jnp.zeros((WINDOW, group), _F32)
            for r in range(group):
                sink_row = jnp.where(_lane((1, group)) == r, sinks_ref[0, group * h + r], sink_row)
                delta_heads = jnp.where(_lane((WINDOW, group)) == r, deltas[r], delta_heads)
            loss_sink = jnp.exp(sink_row - lsec_ref[:, heads]) * delta_heads
            _accumulate(dsinks_ref.at[:, heads], -jnp.sum(loss_sink, axis=0, keepdims=True), first)
        write_kv(dk_prev, dv_prev, cos_p, sgn_p)

        bg = pc_ref[:, dm.o_bg:dm.o_bg + cw]
        yc = yc_ref[...]
        dconv, dgc = _rms_bwd(dmc_ref[:, aw:d], bg * yc, gc_ref[...])
        _accumulate(dgc_ref, dgc, first)
        dproj_ref[:, dm.o_bg:dm.o_bg + cw] = (dconv * yc).astype(dproj_ref.dtype)
        dy = dconv * bg
        bg_n = pn_ref[:, dm.o_bg:dm.o_bg + cw]
        dconv_n, _ = _rms_bwd(dmn_ref[:, aw:d], bg_n * yn_ref[...], gc_ref[...])
        halo = jnp.where(has_next, dconv_n * bg_n, 0.0)
        dy1 = _shift_up(dy, halo, 1)
        dy2 = _shift_up(dy, halo, 2)
        dz = cw_ref[2:3, :] * dy + cw_ref[1:2, :] * dy1 + cw_ref[0:1, :] * dy2
        cg = pc_ref[:, dm.o_cg:dm.o_cg + cw]
        u = pc_ref[:, dm.o_u:dm.o_u + cw]
        dproj_ref[:, dm.o_cg:dm.o_cg + cw] = (dz * u).astype(dproj_ref.dtype)
        dproj_ref[:, dm.o_u:dm.o_u + cw] = (dz * cg).astype(dproj_ref.dtype)
        z = cg * u
        dcw = jnp.concatenate(
            [jnp.sum(z * t, axis=0, keepdims=True) for t in (dy2, dy1, dy)]
            + [jnp.zeros((V7X_SUBLANES - 3, cw), _F32)], axis=0)
        _accumulate(dcw_ref, dcw, first)

    at = lambda n: jnp.minimum(n, nb - 1)
    prev = lambda n: (jnp.maximum(at(n) - 1, 0), 0)
    cur = lambda n: (at(n), 0)
    done = lambda n: (jnp.maximum(n - 1, 0), 0)
    nxt8 = lambda n: (jnp.minimum((at(n) + 1) * (WINDOW // V7X_SUBLANES), s // V7X_SUBLANES - 1), 0)
    fixed = lambda n: (0, 0)
    blocks = WINDOW * (2 * inw * 4 + d * 4 + aw * 4 + cw * 4 + inw * 2 + 2 * KV_WIDTH * 2)
    carry = [pltpu.VMEM((N_KV_HEADS, WINDOW, 128), _F32), pltpu.VMEM((N_KV_HEADS, WINDOW, 128), _F32)]
    n_in, n_out = 17, 6
    ca = _CommArgs(list(comm), n_in, n_out)
    return pl.pallas_call(
        _carrying(body, n_in, n_out, nb + 1, ca, n_scratch=len(carry)), name="mixer_bwd", grid=(nb + 1,),
        in_specs=[pl.BlockSpec((WINDOW, inw), prev), pl.BlockSpec((WINDOW, inw), cur), pl.BlockSpec((V7X_SUBLANES, inw), nxt8),
                  pl.BlockSpec((WINDOW, 2 * V7X_LANES), prev), pl.BlockSpec((WINDOW, 2 * V7X_LANES), cur),
                  pl.BlockSpec((WINDOW, d), cur), pl.BlockSpec((V7X_SUBLANES, d), nxt8),
                  pl.BlockSpec((WINDOW, aw), cur), pl.BlockSpec((WINDOW, nq), cur),
                  pl.BlockSpec((WINDOW, cw), cur), pl.BlockSpec((V7X_SUBLANES, cw), nxt8),
                  pl.BlockSpec((WINDOW, aw + KV_WIDTH), prev), pl.BlockSpec((WINDOW, aw + KV_WIDTH), cur),
                  pl.BlockSpec(memory_space=pltpu.SMEM),
                  pl.BlockSpec((1, aw), fixed), pl.BlockSpec((1, cw), fixed), pl.BlockSpec((V7X_SUBLANES, cw), fixed)]
        + [_ANY] * len(ca.operands),
        out_specs=[pl.BlockSpec((WINDOW, inw), cur), pl.BlockSpec((WINDOW, 2 * KV_WIDTH), done),
                   pl.BlockSpec((1, aw), fixed), pl.BlockSpec((1, cw), fixed),
                   pl.BlockSpec((1, nq), fixed), pl.BlockSpec((V7X_SUBLANES, cw), fixed)] + [_ANY] * len(ca.out_shape),
        out_shape=[jax.ShapeDtypeStruct((s, inw), _CDT), jax.ShapeDtypeStruct((s, 2 * KV_WIDTH), _CDT),
                   jax.ShapeDtypeStruct((1, aw), _F32), jax.ShapeDtypeStruct((1, cw), _F32),
                   jax.ShapeDtypeStruct((1, nq), _F32), jax.ShapeDtypeStruct((V7X_SUBLANES, cw), _F32)] + ca.out_shape,
        scratch_shapes=carry + ca.sems, input_output_aliases=ca.aliases,
        compiler_params=pltpu.CompilerParams(dimension_semantics=("arbitrary",), vmem_limit_bytes=_vmem_limit(blocks)),
    )(proj, proj, proj, rope, rope, dmixed, dmixed, attn, lse, y, y, qk, qk, sinks, g_attn, g_conv, conv_w8, *ca.operands)


def _position():
    return lax.axis_index("x"), lax.axis_index("y"), lax.axis_index("c")


def _linear(px, py, pc):
    return 4 * px + 2 * py + pc


def _comm_kernel(name, comm):
    ca = _CommArgs(list(comm), 0, 0)
    n_cin, n_cout = len(ca.operands), len(ca.out_shape)

    def body(*refs):
        cin, cout, sems = refs[:n_cin], refs[n_cin:n_cin + n_cout], refs[n_cin + n_cout:]
        ca.start(cin, cout, sems)
        ca.middle(cin, cout, sems)
        ca.finish(cin, cout, sems)

    return pl.pallas_call(
        body, name=name, out_shape=ca.out_shape, in_specs=[_ANY] * n_cin, out_specs=[_ANY] * n_cout,
        scratch_shapes=ca.sems, input_output_aliases=ca.aliases,
    )(*ca.operands)


def _gather_op(units):
    n = len(units)
    inputs, outputs, aliases = [], [], {}
    for shard, _, _, _ in units:
        inputs.append(shard)
        outputs.append(jax.ShapeDtypeStruct((N_DEV * shard.shape[0], shard.shape[1]), shard.dtype))
    for u, (_, buf, _, _) in enumerate(units):
        if buf is not None:
            aliases[len(inputs)] = u
            inputs.append(buf)

    def plan(ins, outs, sems, north):
        send_sems, recv_sems, local_sems = sems
        x, y, c = _position()
        me, sibling = (x, y, c), (x, y, 1 - c)
        xn, yn, dg = (1 - x, y), (x, 1 - y), (1 - x, 1 - y)
        via, to, k_via, k_other = (yn, xn, 2, 1) if north else (xn, yn, 1, 2)

        def rows(u, px, py, pc):
            shard, _, r0, r1 = units[u]
            return outs[u].at[pl.ds(pl.multiple_of(_linear(px, py, pc) * shard.shape[0] + r0, 16), r1 - r0), :]

        def own(u):
            _, _, r0, r1 = units[u]
            return ins[u].at[pl.ds(r0, r1 - r0), :]

        def copy(u, k, block, to_, src=None):
            return pltpu.make_async_remote_copy(
                src_ref=rows(u, *block) if src is None else src, dst_ref=rows(u, *block),
                send_sem=send_sems.at[u, k], recv_sem=recv_sems.at[u, k], device_id=to_, device_id_type=_MESH)

        us = range(n)
        return dict(
            mine=[pltpu.make_async_copy(own(u), rows(u, *me), local_sems.at[u]) for u in us],
            first=[cp for u in us for cp in (copy(u, 0, me, sibling, src=own(u)), copy(u, 1, me, (*xn, c), src=own(u)),
                                             copy(u, 2, me, (*yn, c), src=own(u)))],
            relay=[copy(u, 3, (*via, c), (*to, c)) for u in us],
            arrived={1: [copy(u, 1, (*xn, c), me) for u in us], 2: [copy(u, 2, (*yn, c), me) for u in us],
                     3: [copy(u, 3, (*dg, c), me) for u in us]},
            passed={1: [copy(u, 4, (*xn, c), sibling) for u in us], 2: [copy(u, 5, (*yn, c), sibling) for u in us],
                    3: [copy(u, 6, (*dg, c), sibling) for u in us]},
            rest=[cp for u in us for cp in (copy(u, 0, sibling, me), copy(u, 4, (*xn, 1 - c), me),
                                            copy(u, 5, (*yn, 1 - c), me), copy(u, 6, (*dg, 1 - c), me))],
            k_via=k_via, k_other=k_other)

    def land(p, k):
        for arrived, onward in zip(p["arrived"][k], p["passed"][k]):
            arrived.wait_recv()
            onward.start()

    def by_core(fn):
        c = lax.axis_index("c")
        for north in (True, False):
            pl.when(c == (1 if north else 0))(functools.partial(fn, north))

    def start(ins, outs, sems):
        p = plan(ins, outs, sems, True)
        for cp in p["mine"] + p["first"]:
            cp.start()

    def middle(ins, outs, sems):
        def go(north):
            p = plan(ins, outs, sems, north)
            land(p, p["k_via"])
            for cp in p["relay"]:
                cp.start()
            land(p, p["k_other"])
        by_core(go)

    def finish(ins, outs, sems):
        def go(north):
            p = plan(ins, outs, sems, north)
            land(p, 3)
            for cp in p["rest"]:
                cp.wait_recv()
            for cp in p["first"] + p["relay"] + [cp for k in (1, 2, 3) for cp in p["passed"][k]]:
                cp.wait_send()
            for cp in p["mine"]:
                cp.wait()
        by_core(go)

    sems = [pltpu.SemaphoreType.DMA((n, 7)), pltpu.SemaphoreType.DMA((n, 7)), pltpu.SemaphoreType.DMA((n,))]
    return _Comm(inputs, outputs, aliases, sems, start, finish, middle)


def _peers(x, y, c):
    out = []
    for k in range(1, N_DEV):
        fx, fy, fc = (k >> 2) & 1, (k >> 1) & 1, k & 1
        out.append((1 - x if fx else x, 1 - y if fy else y, 1 - c if fc else c))
    return out


def _exchange_op(partials):
    n = len(partials)
    outputs = [jax.ShapeDtypeStruct((4, p.shape[0] // N_DEV, p.shape[1]), p.dtype) for p in partials]

    def plan(ins, outs, sems):
        send_sems, recv_sems = sems
        x, y, c = _position()
        out = []
        for a in range(n):
            r = outs[a].shape[1]
            for ch in range(4):
                out.append(pltpu.make_async_remote_copy(
                    src_ref=ins[a].at[pl.ds(pl.multiple_of((2 * ch + 1 - c) * r, 16), r), :], dst_ref=outs[a].at[ch],
                    send_sem=send_sems.at[a, ch], recv_sem=recv_sems.at[a, ch], device_id=(x, y, 1 - c),
                    device_id_type=_MESH))
        return out

    def start(ins, outs, sems):
        for cp in plan(ins, outs, sems):
            cp.start()

    def finish(ins, outs, sems):
        copies = plan(ins, outs, sems)
        for cp in copies:
            cp.wait_recv()
        for cp in copies:
            cp.wait_send()

    sems = [pltpu.SemaphoreType.DMA((n, 4)), pltpu.SemaphoreType.DMA((n, 4))]
    return _Comm(list(partials), outputs, {}, sems, start, finish)


def _chip_send_op(units):
    n = len(units)
    inputs, outputs, aliases = [], [], {}
    for q, _, _, _ in units:
        inputs.append(q)
        outputs.append(jax.ShapeDtypeStruct(q.shape, q.dtype))
    for u, (_, buf, _, _) in enumerate(units):
        if buf is not None:
            aliases[len(inputs)] = u
            inputs.append(buf)

    def plan(ins, outs, sems):
        send_sems, recv_sems, local_sems = sems
        x, y, c = _position()
        my_chip = 2 * x + y
        chips = [(1 - x, y), (x, 1 - y), (1 - x, 1 - y)]
        mine, sends, arrivals = [], [], []
        for u, (_, _, r0, r1) in enumerate(units):
            span = pl.ds(r0, r1 - r0)
            mine.append(pltpu.make_async_copy(ins[u].at[my_chip, span, :], outs[u].at[my_chip, span, :], local_sems.at[u]))
            for k, (px, py) in enumerate(chips):
                sends.append(pltpu.make_async_remote_copy(
                    src_ref=ins[u].at[2 * px + py, span, :], dst_ref=outs[u].at[my_chip, span, :],
                    send_sem=send_sems.at[u, k], recv_sem=recv_sems.at[u, k], device_id=(px, py, c), device_id_type=_MESH))
                arrivals.append(pltpu.make_async_remote_copy(
                    src_ref=ins[u].at[my_chip, span, :], dst_ref=outs[u].at[2 * px + py, span, :],
                    send_sem=send_sems.at[u, k], recv_sem=recv_sems.at[u, k], device_id=(px, py, c), device_id_type=_MESH))
        return mine, sends, arrivals

    def start(ins, outs, sems):
        mine, sends, _ = plan(ins, outs, sems)
        for cp in mine + sends:
            cp.start()

    def finish(ins, outs, sems):
        mine, sends, arrivals = plan(ins, outs, sems)
        for cp in arrivals:
            cp.wait_recv()
        for cp in sends:
            cp.wait_send()
        for cp in mine:
            cp.wait()

    sems = [pltpu.SemaphoreType.DMA((n, 3)), pltpu.SemaphoreType.DMA((n, 3)), pltpu.SemaphoreType.DMA((n,))]
    return _Comm(inputs, outputs, aliases, sems, start, finish)


def _pair_sum(name, partial, received):
    _, rows, cols = received.shape
    tr = _pick(rows, (352, 288, 256, 128, 64, 32, 16))
    p4 = partial.reshape(4, 2, rows, cols)
    kind = jnp.reshape(lax.axis_index("c"), (1,)).astype(jnp.int32)

    def body(kind_ref, p_ref, r_ref, o_ref):
        o_ref[0] = (p_ref[0, 0].astype(_F32) + r_ref[0].astype(_F32)).astype(o_ref.dtype)

    return pl.pallas_call(
        body, name=name,
        grid_spec=pltpu.PrefetchScalarGridSpec(
            num_scalar_prefetch=1, grid=(4, rows // tr),
            in_specs=[pl.BlockSpec((1, 1, tr, cols), lambda ch, i, kind_ref: (ch, kind_ref[0], i, 0)),
                      pl.BlockSpec((1, tr, cols), lambda ch, i, kind_ref: (ch, i, 0))],
            out_specs=pl.BlockSpec((1, tr, cols), lambda ch, i, kind_ref: (ch, i, 0))),
        out_shape=jax.ShapeDtypeStruct(received.shape, received.dtype),
        compiler_params=pltpu.CompilerParams(dimension_semantics=("arbitrary", "arbitrary")),
    )(kind, p4, received)


def _all_reduce_small(name, v):
    rows = v.shape[0]

    def body(v_ref, out_ref, land_ref, send_sems, recv_sems):
        x, y, c = _position()
        me = _linear(x, y, c)
        peers = _peers(x, y, c)
        land_ref[me] = v_ref[...]
        sends = [pltpu.make_async_remote_copy(
            src_ref=v_ref, dst_ref=land_ref.at[me], send_sem=send_sems.at[k], recv_sem=recv_sems.at[k],
            device_id=peer, device_id_type=_MESH) for k, peer in enumerate(peers)]
        for cp in sends:
            cp.start()
        for k, peer in enumerate(peers):
            pltpu.make_async_remote_copy(
                src_ref=v_ref, dst_ref=land_ref.at[_linear(*peer)], send_sem=send_sems.at[k], recv_sem=recv_sems.at[k],
                device_id=peer, device_id_type=_MESH).wait_recv()
        for cp in sends:
            cp.wait_send()
        total = land_ref[0]
        for s in range(1, N_DEV):
            total = total + land_ref[s]
        out_ref[...] = total

    return pl.pallas_call(
        body, name=name, out_shape=jax.ShapeDtypeStruct(v.shape, _F32),
        in_specs=[pl.BlockSpec(memory_space=pltpu.VMEM)], out_specs=pl.BlockSpec(memory_space=pltpu.VMEM),
        scratch_shapes=[pltpu.VMEM((N_DEV, rows, V7X_LANES), _F32), pltpu.SemaphoreType.DMA((7,)), pltpu.SemaphoreType.DMA((7,))],
    )(v)


def _adamw(name, w, slots, m, v):
    rows, cols = w.shape
    n_slots = slots.shape[0]
    tr = _pick(rows, (176, 144, 128, 64, 32, 16, 8))

    def body(w_ref, s_ref, m_ref, v_ref, g_ref, d_ref, nm_ref, nv_ref):
        g = s_ref[0].astype(_F32)
        for k in range(1, n_slots):
            g = g + s_ref[k].astype(_F32)
        nm = ADAM_B1 * m_ref[...] + (1.0 - ADAM_B1) * g
        nv = ADAM_B2 * v_ref[...] + (1.0 - ADAM_B2) * (g * g)
        m_hat = nm / (1.0 - ADAM_B1 ** ADAM_STEP)
        v_hat = nv / (1.0 - ADAM_B2 ** ADAM_STEP)
        g_ref[...] = g
        d_ref[...] = -ADAM_LR * (m_hat / (jnp.sqrt(v_hat) + ADAM_EPS) + ADAM_WD * w_ref[...])
        nm_ref[...] = nm
        nv_ref[...] = nv

    spec = pl.BlockSpec((tr, cols), lambda i: (i, 0))
    blocks = 7 * tr * cols * 4 + _nbytes((n_slots, tr, cols), slots.dtype)
    return pl.pallas_call(
        body, name=name, grid=(rows // tr,),
        in_specs=[spec, pl.BlockSpec((n_slots, tr, cols), lambda i: (0, i, 0)), spec, spec], out_specs=[spec] * 4,
        out_shape=[jax.ShapeDtypeStruct((rows, cols), _F32)] * 4,
        compiler_params=pltpu.CompilerParams(dimension_semantics=("arbitrary",), vmem_limit_bytes=_vmem_limit(blocks)),
    )(w, slots, m, v)


def _pad_rows(a, rows):
    return jnp.pad(a, ((0, rows - a.shape[0]), (0, 0)))


def _pack(parts):
    rows, spans, at = [], [], 0
    for p in parts:
        p = p.reshape(-1)
        r = -(-p.shape[0] // V7X_LANES)
        rows.append(jnp.pad(p, (0, r * V7X_LANES - p.shape[0])).reshape(r, V7X_LANES))
        spans.append((at, r, p.shape[0]))
        at += r
    packed = jnp.concatenate(rows, axis=0)
    return _pad_rows(packed, -(-at // V7X_SUBLANES) * V7X_SUBLANES), spans


def _unpack(packed, spans, shapes):
    return [packed[at:at + r].reshape(-1)[:size].reshape(shape) for (at, r, size), shape in zip(spans, shapes)]


def kernel(x, positions, w_in, conv_w, sinks, g_attn, g_conv, w_out, ln1_g, ln1_b, w_gate, w_up, w_down, ln2_g, ln2_b, loss_target, m_w_in, m_conv_w, m_sinks, m_g_attn, m_g_conv, m_w_out, m_ln1_g, m_ln1_b, m_w_gate, m_w_up, m_w_down, m_ln2_g, m_ln2_b, v_w_in, v_conv_w, v_sinks, v_g_attn, v_g_conv, v_w_out, v_ln1_g, v_ln1_b, v_w_gate, v_w_up, v_w_down, v_ln2_g, v_ln2_b):
    _, s, d = x.shape
    d_ff = N_DEV * w_gate.shape[2]
    dm = _Dims(s, d, d_ff)
    aw, cw, nq, inw = dm.aw, dm.cw, dm.nq, dm.inw
    x2 = x[0]
    pos = positions[0].reshape(s, 1)
    inv_freq = ROPE_THETA ** (-jnp.arange(0, ROT_DIM, 2, dtype=_F32) / ROT_DIM)
    invf = jnp.tile(inv_freq, V7X_LANES // (ROT_DIM // 2)).reshape(1, V7X_LANES)

    conv_cols = conv_w.shape[2]
    sh_in, sh_out = w_in[0].T.astype(_CDT), w_out[0].astype(_CDT)
    sh_gate, sh_up, sh_down = w_gate[0].T.astype(_CDT), w_up[0].T.astype(_CDT), w_down[0].astype(_CDT)
    r_in, r_out, r_ff = sh_in.shape[0], sh_out.shape[0], sh_gate.shape[0]
    q_ff = r_ff // 4
    assert q_ff % 16 == 0
    def prepare_body(x_ref, pos_ref, invf_ref, xc_ref, rope_ref):
        xc_ref[...] = x_ref[...].astype(_CDT)
        cos, sgn = _rope_tables(pos_ref[...], invf_ref[...])
        rope_ref[:, 0:V7X_LANES] = cos
        rope_ref[:, V7X_LANES:2 * V7X_LANES] = sgn

    x_c, rope, w_in_t, conv_all = _row_kernel(
        "prepare_gather_w_in", prepare_body, [x2, pos], [invf], [((s, d), _CDT), ((s, 2 * V7X_LANES), _F32)], [],
        comm=[_gather_op([(sh_in, None, 0, r_in), (_pad_rows(conv_w[0], 16), None, 0, 16)])])
    conv_full = conv_all.reshape(N_DEV, 16, conv_cols)[:, :3, :].transpose(1, 0, 2).reshape(3, cw)
    conv_w8 = _pad_rows(conv_full, V7X_SUBLANES)

    tm = _pick(s, (1024, 512, 256, 128))
    tm2 = _pick(s, (2048, 1024, 512, 256, 128))
    tr = _pick(s, (512, 256, 128))
    tn_in = _pick(inw, (512, 256, 128))
    tn_ff = _pick(d_ff, (512, 256, 128))

    proj, w_out_f, w_gate_t = _matmul(
        "proj", [[(x_c, w_in_t, "nt")]], s, inw, d, tm2, tn_in, d, [],
        [((s, inw), _F32, (tm2, tn_in), _tile_ij)], _store_epilogue,
        comm=[_gather_op([(sh_out, None, 0, r_out), (sh_gate, None, 0, 2 * q_ff)])])
    mixed, attn, lse, y_conv, qk_rot, w_gate_t, w_up_t = _mixer_fwd(
        dm, proj, rope, sinks, g_attn, g_conv, conv_w8,
        comm=[_gather_op([(sh_gate, w_gate_t, 2 * q_ff, r_ff), (sh_up, None, 0, 2 * q_ff)])])

    def residual_epilogue(accs, ex, out, first):
        out[0][...] = DEEPNORM_ALPHA * ex[0][...] + accs[0]

    tn_d = _pick(d, (512,))
    r1, w_up_t = _matmul(
        "out_proj", [[(mixed, w_out_f, "nn")]], s, d, d, tm, tn_d, d, [(x2, (tm, tn_d), _tile_ij)],
        [((s, d), _F32, (tm, tn_d), _tile_ij)], residual_epilogue,
        comm=[_gather_op([(sh_up, w_up_t, 2 * q_ff, 3 * q_ff)])])
    h1, h1_c, w_up_t = _ln1_fwd_rows(r1, ln1_g, ln1_b, comm=[_gather_op([(sh_up, w_up_t, 3 * q_ff, r_ff)])])

    def swiglu_epilogue(accs, ex, out, first):
        gate_v, up_v = accs
        out[0][...] = gate_v
        out[1][...] = up_v
        out[2][...] = (gate_v * jax.nn.sigmoid(gate_v) * up_v).astype(_CDT)

    gate, up, act, w_down_f = _matmul(
        "gate_up", [[(h1_c, w_gate_t, "nt")], [(h1_c, w_up_t, "nt")]], s, d_ff, d, tm, tn_ff, d, [],
        [((s, d_ff), _F32, (tm, tn_ff), _tile_ij), ((s, d_ff), _F32, (tm, tn_ff), _tile_ij),
         ((s, d_ff), _CDT, (tm, tn_ff), _tile_ij)], swiglu_epilogue,
        comm=[_gather_op([(sh_down, None, 0, r_ff)])])

    (r2,) = _matmul("down", [[(act, w_down_f, "nn")]], s, d, d_ff, tm, tn_d, d_ff, [(h1, (tm, tn_d), _tile_ij)],
                    [((s, d), _F32, (tm, tn_d), _tile_ij)], residual_epilogue)
    dr2, dr2_c, loss_acc, d_ln2_g, d_ln2_b = _ln2_loss_bwd(r2, loss_target[0], ln2_g, ln2_b)

    def swiglu_bwd_epilogue(accs, ex, out, first):
        gate_v, up_v = ex[0][...], ex[1][...]
        sig = jax.nn.sigmoid(gate_v)
        out[0][...] = (accs[0] * up_v * (sig * (1.0 + gate_v * (1.0 - sig)))).astype(_CDT)
        out[1][...] = (accs[0] * (gate_v * sig)).astype(_CDT)

    dgate, dup = _matmul(
        "dact", [[(dr2_c, w_down_f, "nt")]], s, d_ff, d, tm2, tn_ff, d,
        [(gate, (tm2, tn_ff), _tile_ij), (up, (tm2, tn_ff), _tile_ij)],
        [((s, d_ff), _CDT, (tm2, tn_ff), _tile_ij), ((s, d_ff), _CDT, (tm2, tn_ff), _tile_ij)], swiglu_bwd_epilogue)
    def weight_grad(name, a, b, comm=()):
        rows = a.shape[1]
        tw, tn_w = _pick(rows, (512, 256, 128)), d
        return _matmul(name, [[(a, b, "tn")]], rows, d, s, tw, tn_w, s, [],
                       [((rows, d), _CDT, (tw, tn_w), _tile_ij)], _store_epilogue, comm=comm, j_outer=True)

    (dw_down,) = weight_grad("dw_down", act, dr2_c)
    dw_gate_t, x_down = weight_grad("dw_gate", dgate, h1_c, comm=[_exchange_op([dw_down])])
    q_down = _pair_sum("chip_sum_w_down", dw_down, x_down)
    dw_up_t, l_down, x_gate = weight_grad(
        "dw_up", dup, h1_c, comm=[_chip_send_op([(q_down, None, 0, 2 * q_ff)]), _exchange_op([dw_gate_t])])
    q_gate = _pair_sum("chip_sum_w_gate", dw_gate_t, x_gate)

    tn_h = _pick(d, (512,))
    dh1, l_down, l_gate, x_up = _matmul(
        "dh1", [[(dgate, w_gate_t, "nn"), (dup, w_up_t, "nn")]], s, d, d_ff, tr, tn_h, d_ff,
        [(dr2, (tr, tn_h), _tile_ij)], [((s, d), _F32, (tr, tn_h), _tile_ij)], residual_epilogue,
        comm=[_chip_send_op([(q_down, l_down, 2 * q_ff, r_ff), (q_gate, None, 0, r_ff)]), _exchange_op([dw_up_t])])
    q_up = _pair_sum("chip_sum_w_up", dw_up_t, x_up)
    dr1, dr1_c, d_ln1_g, d_ln1_b = _ln1_bwd_rows(dh1, r1, ln1_g)
    (dmixed,) = _matmul("dmixed", [[(dr1_c, w_out_f, "nt")]], s, d, d, tm2, tn_d, d, [],
                        [((s, d), _F32, (tm2, tn_d), _tile_ij)], _store_epilogue)
    (dw_out,) = weight_grad("dw_out", mixed, dr1_c)
    dproj, dkv, d_g_attn, d_g_conv, d_sinks, d_conv8, l_up, x_out = _mixer_bwd(
        dm, proj, rope, sinks, g_attn, g_conv, conv_w8, dmixed, attn, lse, y_conv, qk_rot,
        comm=[_chip_send_op([(q_up, None, 0, r_ff)]), _exchange_op([dw_out])])
    dproj = _patch_columns("dproj_kv", dproj, dkv, dm.o_k)
    q_out = _pair_sum("chip_sum_w_out", dw_out, x_out)
    dw_in_t, l_out = weight_grad("dw_in", dproj, x_c, comm=[_chip_send_op([(q_out, None, 0, r_out)])])
    (x_in,) = _comm_kernel("exchange_w_in", [_exchange_op([dw_in_t])])
    q_in = _pair_sum("chip_sum_w_in", dw_in_t, x_in)

    grad_x, l_in = _matmul("dx", [[(dproj, w_in_t, "nn")]], s, d, inw, tm, tn_d, inw,
                           [(dr1, (tm, tn_d), _tile_ij)], [((s, d), _F32, (tm, tn_d), _tile_ij)], residual_epilogue,
                           comm=[_chip_send_op([(q_in, None, 0, r_in)])])

    small_parts = [d_conv8[:3], d_sinks, d_g_attn, d_g_conv, d_ln1_g, d_ln1_b, d_ln2_g, d_ln2_b, loss_acc[0:1, 0:1]]
    packed, spans = _pack(small_parts)
    reduced = _unpack(_all_reduce_small("reduce_small", packed), spans, [p.shape for p in small_parts])
    g_conv_full, g_sinks, g_g_attn, g_g_conv, g_ln1_g, g_ln1_b, g_ln2_g, g_ln2_b, loss_sum = reduced
    me = _linear(*_position())
    g_conv_w = lax.dynamic_slice(g_conv_full, (0, me * conv_cols), (3, conv_cols))
    loss = loss_sum[0, 0]

    big = {"w_in": (w_in[0].T, l_in, m_w_in[0].T, v_w_in[0].T), "w_out": (w_out[0], l_out, m_w_out[0], v_w_out[0]),
           "w_gate": (w_gate[0].T, l_gate, m_w_gate[0].T, v_w_gate[0].T),
           "w_up": (w_up[0].T, l_up, m_w_up[0].T, v_w_up[0].T), "w_down": (w_down[0], l_down, m_w_down[0], v_w_down[0])}
    res = {nm: tuple(_adamw(f"adamw_{nm}", w, slots, m, v)) for nm, (w, slots, m, v) in big.items()}
    for nm in ("w_in", "w_gate", "w_up"):
        res[nm] = tuple(a.T for a in res[nm])
    small_names = ["conv_w", "sinks", "g_attn", "g_conv", "ln1_g", "ln1_b", "ln2_g", "ln2_b"]
    small_w = [conv_w, sinks, g_attn, g_conv, ln1_g, ln1_b, ln2_g, ln2_b]
    small_g = [g_conv_w[None], g_sinks, g_g_attn, g_g_conv, g_ln1_g, g_ln1_b, g_ln2_g, g_ln2_b]
    small_m = [m_conv_w, m_sinks, m_g_attn, m_g_conv, m_ln1_g, m_ln1_b, m_ln2_g, m_ln2_b]
    small_v = [v_conv_w, v_sinks, v_g_attn, v_g_conv, v_ln1_g, v_ln1_b, v_ln2_g, v_ln2_b]
    pw, sp = _pack(small_w)
    pg, _ = _pack(small_g)
    pm, _ = _pack(small_m)
    pv, _ = _pack(small_v)
    shapes = [w.shape for w in small_w]
    _, sd, sm, sv = [_unpack(p, sp, shapes) for p in _adamw("adamw_small", pw, pg[None], pm, pv)]
    for i, nm in enumerate(small_names):
        res[nm] = (small_g[i].reshape(shapes[i]), sd[i], sm[i], sv[i])

    order = ["w_in", "conv_w", "sinks", "g_attn", "g_conv", "w_out", "ln1_g", "ln1_b", "w_gate", "w_up", "w_down", "ln2_g", "ln2_b"]

    def lead(a, nm):
        return a[None] if nm in big else a

    return (loss, grad_x[None],
            *[lead(res[nm][0], nm) for nm in order], *[lead(res[nm][1], nm) for nm in order],
            *[lead(res[nm][2], nm) for nm in order], *[lead(res[nm][3], nm) for nm in order])
```

```python
import functools

import jax
import jax.numpy as jnp
from jax import lax
from jax.experimental import pallas as pl
from jax.experimental.pallas import tpu as pltpu

_F32 = jnp.float32
_CDT = jnp.bfloat16

HEAD_DIM = 64
WINDOW = 128
N_KV_HEADS = 4
KV_WIDTH = N_KV_HEADS * HEAD_DIM
ROT_DIM = HEAD_DIM // 4
ROPE_THETA = 500000.0
ATTN_SCALE = HEAD_DIM ** -0.5
DEPTH = 1
DEEPNORM_ALPHA = (2 * DEPTH) ** 0.25
LN_EPS = 1e-5
RMS_EPS = 1e-6
ADAM_LR = 0.001
ADAM_B1 = 0.9
ADAM_B2 = 0.999
ADAM_EPS = 1e-08
ADAM_WD = 0.01
ADAM_STEP = 10
N_DEV = 8
MASKED = -1e30

MIB = 1024 * 1024
V7X_VMEM_BYTES = 64 * MIB
V7X_LANES = 128
V7X_SUBLANES = 8
BODY_TEMPORARIES_BYTES = 16 * MIB
VMEM_LIMIT_FLOOR_BYTES = 32 * MIB
VMEM_LIMIT_CEILING_BYTES = V7X_VMEM_BYTES - 8 * MIB
_MESH = pl.DeviceIdType.MESH
_ANY = pl.BlockSpec(memory_space=pl.ANY)


def _vmem_limit(block_bytes, scratch_bytes=0):
    want = 2 * block_bytes + scratch_bytes + BODY_TEMPORARIES_BYTES
    return int(min(max(want, VMEM_LIMIT_FLOOR_BYTES), VMEM_LIMIT_CEILING_BYTES))


def _nbytes(shape, dtype):
    n = 1
    for s in shape:
        n *= s
    return n * jnp.dtype(dtype).itemsize


def _pick(n, candidates):
    for c in candidates:
        if n % c == 0:
            return c
    raise ValueError(f"no tile of {candidates} divides {n}")


_DOT_DIMS = {"nn": ((1,), (0,)), "nt": ((1,), (1,)), "tn": ((0,), (0,))}


def _dot(a, b, mode):
    return lax.dot_general(a.astype(_CDT), b.astype(_CDT), (_DOT_DIMS[mode], ((), ())),
                           preferred_element_type=_F32)


def _accumulate(ref, val, first):
    @pl.when(first)
    def _():
        ref[...] = val

    @pl.when(jnp.logical_not(first))
    def _():
        ref[...] += val


class _Comm:
    def __init__(self, inputs, outputs, aliases, sems, start, finish, middle=None):
        self.inputs, self.outputs, self.aliases, self.sems = inputs, outputs, aliases, sems
        self.start, self.finish, self.middle = start, finish, middle


def _middle_step(n_steps):
    return (2 * n_steps) // 3


class _CommArgs:
    def __init__(self, comms, n_in_before, n_out_before):
        self.comms, self.operands, self.out_shape, self.aliases, self.sems, self.at = comms, [], [], {}, [], []
        for cm in comms:
            self.at.append((len(self.operands), len(self.out_shape), len(self.sems)))
            for i_in, i_out in cm.aliases.items():
                self.aliases[n_in_before + len(self.operands) + i_in] = n_out_before + len(self.out_shape) + i_out
            self.operands += cm.inputs
            self.out_shape += cm.outputs
            self.sems += cm.sems

    def _each(self, in_refs, out_refs, sem_refs):
        for cm, (i0, o0, s0) in zip(self.comms, self.at):
            yield cm, (in_refs[i0:i0 + len(cm.inputs)], out_refs[o0:o0 + len(cm.outputs)], sem_refs[s0:s0 + len(cm.sems)])

    def start(self, in_refs, out_refs, sem_refs):
        for cm, refs in self._each(in_refs, out_refs, sem_refs):
            cm.start(*refs)

    def finish(self, in_refs, out_refs, sem_refs):
        for cm, refs in self._each(in_refs, out_refs, sem_refs):
            cm.finish(*refs)

    @property
    def has_middle(self):
        return any(cm.middle is not None for cm in self.comms)

    def middle(self, in_refs, out_refs, sem_refs):
        for cm, refs in self._each(in_refs, out_refs, sem_refs):
            if cm.middle is not None:
                cm.middle(*refs)


def _matmul(name, groups, m, n, k, tm, tn, tk, extras, outs, epilogue, comm=(), j_outer=False):
    assert m % tm == 0 and n % tn == 0 and k % tk == 0, (name, m, n, k, tm, tn, tk)
    nk = k // tk
    terms = [t for g in groups for t in g]
    operands, in_specs, block_bytes = [], [], 0

    def spec(blk, imap):
        return pl.BlockSpec(blk, (lambda g0, g1, kk: imap(g1, g0, kk)) if j_outer else imap)

    for a, b, mode in terms:
        assert a.shape == ((k, m) if mode == "tn" else (m, k)), (name, a.shape, mode)
        assert b.shape == ((n, k) if mode == "nt" else (k, n)), (name, b.shape, mode)
        if mode == "tn":
            a_blk, a_map = (tk, tm), (lambda i, j, kk: (kk, i))
        else:
            a_blk, a_map = (tm, tk), (lambda i, j, kk: (i, kk))
        if mode == "nt":
            b_blk, b_map = (tn, tk), (lambda i, j, kk: (j, kk))
        else:
            b_blk, b_map = (tk, tn), (lambda i, j, kk: (kk, j))
        operands += [a, b]
        in_specs += [spec(a_blk, a_map), spec(b_blk, b_map)]
        block_bytes += _nbytes(a_blk, a.dtype) + _nbytes(b_blk, b.dtype)
    for arr, blk, imap in extras:
        operands.append(arr)
        in_specs.append(spec(blk, lambda i, j, kk, imap=imap: imap(i, j)))
        block_bytes += _nbytes(blk, arr.dtype)
    out_shape, out_specs = [], []
    for shape, dtype, blk, imap in outs:
        out_shape.append(jax.ShapeDtypeStruct(shape, dtype))
        out_specs.append(spec(blk, lambda i, j, kk, imap=imap: imap(i, j)))
        block_bytes += _nbytes(blk, dtype)
    n_terms, n_extra, n_out, n_groups = len(terms), len(extras), len(outs), len(groups)
    scratch = [pltpu.VMEM((tm, tn), _F32) for _ in range(n_groups)] if nk > 1 else []
    ca = _CommArgs(list(comm), len(operands), n_out)
    n_cin, n_cout, n_acc = len(ca.operands), len(ca.out_shape), len(scratch)
    tiles = (m // tm, n // tn)
    grid = (tiles[1], tiles[0], nk) if j_outer else (tiles[0], tiles[1], nk)

    def body(*refs):
        refs = list(refs)
        term_refs = [refs.pop(0) for _ in range(2 * n_terms)]
        extra_refs = [refs.pop(0) for _ in range(n_extra)]
        cin_refs = [refs.pop(0) for _ in range(n_cin)]
        out_refs = [refs.pop(0) for _ in range(n_out)]
        cout_refs = [refs.pop(0) for _ in range(n_cout)]
        acc_refs = [refs.pop(0) for _ in range(n_acc)]
        sem_refs = refs
        g0, g1, kk = pl.program_id(0), pl.program_id(1), pl.program_id(2)
        first = jnp.logical_and(g0 == 0, g1 == 0)
        if comm:
            @pl.when(jnp.logical_and(first, kk == 0))
            def _():
                ca.start(cin_refs, cout_refs, sem_refs)
        if ca.has_middle:
            step = (g0 * grid[1] + g1) * nk + kk

            @pl.when(step == _middle_step(grid[0] * grid[1] * nk))
            def _():
                ca.middle(cin_refs, cout_refs, sem_refs)
        partial, t = [], 0
        for g in groups:
            s = None
            for _, _, mode in g:
                d = _dot(term_refs[2 * t][...], term_refs[2 * t + 1][...], mode)
                s = d if s is None else s + d
                t += 1
            partial.append(s)
        if nk == 1:
            epilogue(partial, extra_refs, out_refs, first)
        else:
            for acc, p in zip(acc_refs, partial):
                _accumulate(acc, p, kk == 0)

            @pl.when(kk == nk - 1)
            def _():
                epilogue([acc[...] for acc in acc_refs], extra_refs, out_refs, first)
        if comm:
            @pl.when(jnp.logical_and(jnp.logical_and(g0 == grid[0] - 1, g1 == grid[1] - 1), kk == nk - 1))
            def _():
                ca.finish(cin_refs, cout_refs, sem_refs)

    res = pl.pallas_call(
        body, name=name, grid=grid,
        in_specs=in_specs + [_ANY] * n_cin, out_specs=out_specs + [_ANY] * n_cout,
        out_shape=out_shape + ca.out_shape, scratch_shapes=scratch + ca.sems, input_output_aliases=ca.aliases,
        compiler_params=pltpu.CompilerParams(
            dimension_semantics=("arbitrary", "arbitrary", "arbitrary"),
            vmem_limit_bytes=_vmem_limit(block_bytes, n_groups * tm * tn * 4 if nk > 1 else 0)),
    )(*operands, *ca.operands)
    return list(res[:n_out]) + list(res[n_out:])


def _store_epilogue(accs, extra_refs, out_refs, first):
    for acc, ref in zip(accs, out_refs):
        ref[...] = acc.astype(ref.dtype)


def _tile_ij(i, j):
    return (i, j)


def _row_i(i, j):
    return (i, 0)


def _whole(i, j):
    return (0, 0)


def _mean(v):
    return jnp.mean(v, axis=-1, keepdims=True)


def _ln_fwd(r, g, b):
    xc = r - _mean(r)
    rstd = lax.rsqrt(_mean(xc * xc) + LN_EPS)
    xhat = xc * rstd
    return xhat * g + b, xhat, rstd


def _ln_bwd(dy, xhat, rstd, g):
    dxh = dy * g
    dr = rstd * (dxh - _mean(dxh) - xhat * _mean(dxh * xhat))
    return dr, jnp.sum(dy * xhat, axis=0, keepdims=True), jnp.sum(dy, axis=0, keepdims=True)


def _rms_fwd(a, g):
    rstd = lax.rsqrt(_mean(a * a) + RMS_EPS)
    return a * rstd * g


def _rms_bwd(dm, a, g):
    rstd = lax.rsqrt(_mean(a * a) + RMS_EPS)
    nhat = a * rstd
    dn = dm * g
    da = rstd * (dn - nhat * _mean(dn * nhat))
    return da, jnp.sum(dm * nhat, axis=0, keepdims=True)


def _lane(shape):
    return lax.broadcasted_iota(jnp.int32, shape, 1)


def _row(shape):
    return lax.broadcasted_iota(jnp.int32, shape, 0)


def _rope_tables(pos, invf):
    ang = pos.astype(_F32) * invf
    lane = _lane(ang.shape)
    in_rot = (lane % HEAD_DIM) < ROT_DIM
    first = (lane % ROT_DIM) < ROT_DIM // 2
    cos = jnp.where(in_rot, jnp.cos(ang), 1.0)
    sin = jnp.sin(ang)
    sgn = jnp.where(in_rot, jnp.where(first, -sin, sin), 0.0)
    return cos, sgn


def _rope(t, cos, sgn, sign):
    half = ROT_DIM // 2
    first = (_lane(t.shape) % ROT_DIM) < half
    partner = jnp.where(first, pltpu.roll(t, V7X_LANES - half, 1), pltpu.roll(t, half, 1))
    return t * cos + partner * (sgn * sign)


def _dup_head(t, h):
    g = t[:, 128 * (h // 2):128 * (h // 2) + 128]
    r = pltpu.roll(g, HEAD_DIM, 1)
    lo = _lane(g.shape) < HEAD_DIM
    return jnp.where(lo, g, r) if h % 2 == 0 else jnp.where(lo, r, g)


def _fold_halves(t):
    return t + pltpu.roll(t, HEAD_DIM, 1)


def _halves(t):
    lo = _lane(t.shape) < HEAD_DIM
    zero = jnp.zeros_like(t)
    return jnp.where(lo, t, zero), jnp.where(lo, zero, t)


def _band_mask(n_heads, n_keys, first_block):
    shape = (n_heads * WINDOW, n_keys)
    i = jnp.bitwise_and(_row(shape), WINDOW - 1)
    j = _lane(shape)
    valid = jnp.logical_and(j >= i + 1, j <= i + WINDOW)
    if first_block is not None:
        valid = jnp.logical_and(valid, jnp.logical_or(j >= WINDOW, jnp.logical_not(first_block)))
    return valid


def _stack_heads(pairs):
    return jnp.concatenate([half for t in pairs for half in _halves(t.astype(_CDT))], axis=0)


def _unstack_heads(t, n_pairs):
    lo = _lane((WINDOW, 128)) < HEAD_DIM
    return [jnp.where(lo, t[2 * WINDOW * i:2 * WINDOW * i + WINDOW], t[2 * WINDOW * i + WINDOW:2 * WINDOW * (i + 1)])
            for i in range(n_pairs)]


def _per_head(values):
    n_rows = len(values) * WINDOW
    block = jnp.right_shift(_row((n_rows, 1)), WINDOW.bit_length() - 1)
    out = jnp.zeros((n_rows, 1), _F32)
    for k, v in enumerate(values):
        out = jnp.where(block == k, v, out)
    return out


def _shift_down(z, halo, k):
    out = pltpu.roll(z, k, 0)
    r = _row(z.shape)
    for t in range(k):
        out = jnp.where(r == t, halo[V7X_SUBLANES - k + t:V7X_SUBLANES - k + t + 1, :], out)
    return out


def _shift_up(z, halo, k):
    rows = z.shape[0]
    out = pltpu.roll(z, rows - k, 0)
    r = _row(z.shape)
    for t in range(k):
        out = jnp.where(r == rows - k + t, halo[t:t + 1, :], out)
    return out


class _Dims:
    def __init__(self, s, d, d_ff):
        self.s, self.d, self.d_ff = s, d, d_ff
        self.aw = d // 2
        self.cw = d - self.aw
        self.nq = self.aw // HEAD_DIM
        self.group = self.nq // N_KV_HEADS
        assert self.group % 2 == 0, "a 128-lane pair of query heads must share its kv head"
        self.inw = self.aw + 2 * KV_WIDTH + 3 * self.cw
        self.o_k = self.aw
        self.o_v = self.aw + KV_WIDTH
        self.o_cg = self.aw + 2 * KV_WIDTH
        self.o_bg = self.o_cg + self.cw
        self.o_u = self.o_bg + self.cw
        self.nb = s // WINDOW
        assert s % WINDOW == 0


def _carrying(body, n_in, n_out, n_steps, ca, n_scratch=0):
    n_cin, n_cout = len(ca.operands), len(ca.out_shape)

    def wrapped(*refs):
        refs = list(refs)
        in_refs = [refs.pop(0) for _ in range(n_in)]
        cin_refs = [refs.pop(0) for _ in range(n_cin)]
        out_refs = [refs.pop(0) for _ in range(n_out)]
        cout_refs = [refs.pop(0) for _ in range(n_cout)]
        scratch_refs = [refs.pop(0) for _ in range(n_scratch)]
        if ca.comms:
            @pl.when(pl.program_id(0) == 0)
            def _():
                ca.start(cin_refs, cout_refs, refs)
        if ca.has_middle:
            @pl.when(pl.program_id(0) == _middle_step(n_steps))
            def _():
                ca.middle(cin_refs, cout_refs, refs)
        body(*in_refs, *out_refs, *scratch_refs)
        if ca.comms:
            @pl.when(pl.program_id(0) == n_steps - 1)
            def _():
                ca.finish(cin_refs, cout_refs, refs)

    return wrapped


def _row_kernel(name, body, rows_in, vecs_in, rows_out, vecs_out, comm=()):
    s = rows_in[0].shape[0]
    tr = _pick(s, (256, 128))
    row = lambda a: pl.BlockSpec((tr, a[1] if isinstance(a, tuple) else a.shape[1]), lambda i: (i, 0))
    vec = lambda shape: pl.BlockSpec(tuple(shape), lambda i: (0, 0))
    n_in, n_out = len(rows_in) + len(vecs_in), len(rows_out) + len(vecs_out)
    ca = _CommArgs(list(comm), n_in, n_out)
    blocks = sum(_nbytes((tr, a.shape[1]), a.dtype) for a in rows_in) + sum(_nbytes((tr, sh[1]), dt) for sh, dt in rows_out)
    res = pl.pallas_call(
        _carrying(body, n_in, n_out, s // tr, ca), name=name, grid=(s // tr,),
        in_specs=[row(a) for a in rows_in] + [vec(v.shape) for v in vecs_in] + [_ANY] * len(ca.operands),
        out_specs=[row(sh) for sh, _ in rows_out] + [vec(sh) for sh, _ in vecs_out] + [_ANY] * len(ca.out_shape),
        out_shape=[jax.ShapeDtypeStruct(sh, dt) for sh, dt in list(rows_out) + list(vecs_out)] + ca.out_shape,
        scratch_shapes=ca.sems, input_output_aliases=ca.aliases,
        compiler_params=pltpu.CompilerParams(dimension_semantics=("arbitrary",), vmem_limit_bytes=_vmem_limit(blocks)),
    )(*rows_in, *vecs_in, *ca.operands)
    return list(res)


def _ln2_loss_bwd(r2, target, gain, bias, comm=()):
    s, d = r2.shape

    def body(r_ref, t_ref, g_ref, b_ref, dr_ref, drc_ref, loss_ref, dg_ref, db_ref):
        first = pl.program_id(0) == 0
        yv, xhat, rstd = _ln_fwd(r_ref[...], g_ref[...], b_ref[...])
        err = yv - t_ref[...]
        dr2, dg, db = _ln_bwd(err * (1.0 / d), xhat, rstd, g_ref[...])
        dr_ref[...] = dr2
        drc_ref[...] = dr2.astype(_CDT)
        _accumulate(loss_ref, jnp.zeros(loss_ref.shape, _F32) + 0.5 * jnp.sum(err * err) * (1.0 / d), first)
        _accumulate(dg_ref, dg, first)
        _accumulate(db_ref, db, first)

    return _row_kernel("ln2_loss_bwd", body, [r2, target], [gain, bias], [((s, d), _F32), ((s, d), _CDT)],
                       [((V7X_SUBLANES, V7X_LANES), _F32), ((1, d), _F32), ((1, d), _F32)], comm)


def _ln1_fwd_rows(r1, gain, bias, comm=()):
    s, d = r1.shape

    def body(r_ref, g_ref, b_ref, h_ref, hc_ref):
        h1, _, _ = _ln_fwd(r_ref[...], g_ref[...], b_ref[...])
        h_ref[...] = h1
        hc_ref[...] = h1.astype(_CDT)

    return _row_kernel("ln1", body, [r1], [gain, bias], [((s, d), _F32), ((s, d), _CDT)], [], comm)


def _ln1_bwd_rows(dh1, r1, gain, comm=()):
    s, d = dh1.shape

    def body(dh_ref, r_ref, g_ref, dr_ref, drc_ref, dg_ref, db_ref):
        first = pl.program_id(0) == 0
        _, xhat, rstd = _ln_fwd(r_ref[...], g_ref[...], 0.0)
        dr1, dg, db = _ln_bwd(dh_ref[...], xhat, rstd, g_ref[...])
        dr_ref[...] = dr1
        drc_ref[...] = dr1.astype(_CDT)
        _accumulate(dg_ref, dg, first)
        _accumulate(db_ref, db, first)

    return _row_kernel("ln1_bwd", body, [dh1, r1], [gain], [((s, d), _F32), ((s, d), _CDT)],
                       [((1, d), _F32), ((1, d), _F32)], comm)


def _mixer_fwd(dm, proj, rope, sinks, g_attn, g_conv, conv_w8, comm=()):
    s, d, aw, cw, nq, inw, nb = dm.s, dm.d, dm.aw, dm.cw, dm.nq, dm.inw, dm.nb

    def body(pp_ref, pc_ref, ropep_ref, ropec_ref, sinks_ref, ga_ref, gc_ref, cw_ref,
             mixed_ref, attn_ref, lse_ref, y_ref, qk_ref):
        n = pl.program_id(0)
        cos_c, sgn_c = ropec_ref[:, 0:V7X_LANES], ropec_ref[:, V7X_LANES:2 * V7X_LANES]
        cos_p, sgn_p = ropep_ref[:, 0:V7X_LANES], ropep_ref[:, V7X_LANES:2 * V7X_LANES]
        for g in range(KV_WIDTH // 128):
            qk_ref[:, aw + 128 * g:aw + 128 * g + 128] = _rope(
                pc_ref[:, dm.o_k + 128 * g:dm.o_k + 128 * g + 128], cos_c, sgn_c, 1.0).astype(qk_ref.dtype)
        for j in range(nq // 2):
            qk_ref[:, 128 * j:128 * j + 128] = _rope(pc_ref[:, 128 * j:128 * j + 128], cos_c, sgn_c, 1.0).astype(qk_ref.dtype)
        k_prev = jnp.concatenate([_rope(pp_ref[:, dm.o_k + 128 * g:dm.o_k + 128 * g + 128], cos_p, sgn_p, 1.0)
                                  for g in range(KV_WIDTH // 128)], axis=1)
        kk = jnp.concatenate([k_prev, qk_ref[:, aw:aw + KV_WIDTH].astype(_F32)], axis=0)
        vv = jnp.concatenate([pp_ref[:, dm.o_v:dm.o_v + KV_WIDTH], pc_ref[:, dm.o_v:dm.o_v + KV_WIDTH]], axis=0)
        group, pairs = dm.group, dm.group // 2
        valid = _band_mask(group, 2 * WINDOW, n == 0)
        for h in range(N_KV_HEADS):
            k2, v2 = _dup_head(kk, h).astype(_CDT), _dup_head(vv, h).astype(_CDT)
            q4 = _stack_heads([qk_ref[:, 128 * j:128 * j + 128] for j in range(pairs * h, pairs * (h + 1))])
            sc = jnp.where(valid, _dot(q4, k2, "nt") * ATTN_SCALE, MASKED)
            sink = _per_head([sinks_ref[0, group * h + r] for r in range(group)])
            mx = jnp.maximum(jnp.max(sc, axis=1, keepdims=True), sink)
            p = jnp.exp(sc - mx)
            den = jnp.sum(p, axis=1, keepdims=True) + jnp.exp(sink - mx)
            out = _unstack_heads(_dot(p / den, v2, "nn"), pairs)
            lse = mx + jnp.log(den)
            for r in range(group):
                lse_ref[:, group * h + r:group * h + r + 1] = lse[WINDOW * r:WINDOW * (r + 1)]
            for i in range(pairs):
                j = pairs * h + i
                attn_ref[:, 128 * j:128 * j + 128] = out[i]
        mixed_ref[:, 0:aw] = _rms_fwd(attn_ref[...], ga_ref[...]).astype(mixed_ref.dtype)

        z = pc_ref[:, dm.o_cg:dm.o_cg + cw] * pc_ref[:, dm.o_u:dm.o_u + cw]
        top = WINDOW - V7X_SUBLANES
        halo = pp_ref[top:WINDOW, dm.o_cg:dm.o_cg + cw] * pp_ref[top:WINDOW, dm.o_u:dm.o_u + cw]
        halo = jnp.where(n == 0, jnp.zeros_like(halo), halo)
        y = cw_ref[0:1, :] * _shift_down(z, halo, 2) + cw_ref[1:2, :] * _shift_down(z, halo, 1) + cw_ref[2:3, :] * z
        y_ref[...] = y
        conv = pc_ref[:, dm.o_bg:dm.o_bg + cw] * y
        mixed_ref[:, aw:d] = _rms_fwd(conv, gc_ref[...]).astype(mixed_ref.dtype)

    prev = lambda n: (jnp.maximum(n - 1, 0), 0)
    cur = lambda n: (n, 0)
    fixed = lambda n: (0, 0)
    blocks = 2 * WINDOW * inw * 4 + WINDOW * (d * 2 + aw * 4 + cw * 4 + nq * 4)
    ca = _CommArgs(list(comm), 8, 5)
    return pl.pallas_call(
        _carrying(body, 8, 5, nb, ca), name="mixer_fwd", grid=(nb,),
        in_specs=[pl.BlockSpec((WINDOW, inw), prev), pl.BlockSpec((WINDOW, inw), cur),
                  pl.BlockSpec((WINDOW, 2 * V7X_LANES), prev), pl.BlockSpec((WINDOW, 2 * V7X_LANES), cur),
                  pl.BlockSpec(memory_space=pltpu.SMEM),
                  pl.BlockSpec((1, aw), fixed), pl.BlockSpec((1, cw), fixed), pl.BlockSpec((V7X_SUBLANES, cw), fixed)]
        + [_ANY] * len(ca.operands),
        out_specs=[pl.BlockSpec((WINDOW, d), cur), pl.BlockSpec((WINDOW, aw), cur),
                   pl.BlockSpec((WINDOW, nq), cur), pl.BlockSpec((WINDOW, cw), cur),
                   pl.BlockSpec((WINDOW, aw + KV_WIDTH), cur)] + [_ANY] * len(ca.out_shape),
        out_shape=[jax.ShapeDtypeStruct((s, d), _CDT), jax.ShapeDtypeStruct((s, aw), _F32),
                   jax.ShapeDtypeStruct((s, nq), _F32), jax.ShapeDtypeStruct((s, cw), _F32),
                   jax.ShapeDtypeStruct((s, aw + KV_WIDTH), _CDT)] + ca.out_shape,
        scratch_shapes=ca.sems, input_output_aliases=ca.aliases,
        compiler_params=pltpu.CompilerParams(dimension_semantics=("arbitrary",), vmem_limit_bytes=_vmem_limit(blocks)),
    )(proj, proj, rope, rope, sinks, g_attn, g_conv, conv_w8, *ca.operands)


def _patch_columns(name, a, part, offset):
    s, pw = part.shape
    assert offset % pw == 0 and pw % V7X_LANES == 0
    tr = _pick(s, (512, 256, 128))

    def body(a_ref, p_ref, o_ref):
        del a_ref
        o_ref[...] = p_ref[...]

    return pl.pallas_call(
        body, name=name, grid=(s // tr,),
        in_specs=[_ANY, pl.BlockSpec((tr, pw), lambda i: (i, 0))],
        out_specs=pl.BlockSpec((tr, pw), lambda i: (i, offset // pw)),
        out_shape=jax.ShapeDtypeStruct(a.shape, a.dtype), input_output_aliases={0: 0},
        compiler_params=pltpu.CompilerParams(dimension_semantics=("arbitrary",)),
    )(a, part)


def _mixer_bwd(dm, proj, rope, sinks, g_attn, g_conv, conv_w8, dmixed, attn, lse, y, qk, comm=()):
    s, d, aw, cw, nq, inw, nb = dm.s, dm.d, dm.aw, dm.cw, dm.nq, dm.inw, dm.nb

    def body(pp_ref, pc_ref, pn_ref, ropep_ref, ropec_ref, dmc_ref, dmn_ref, ac_ref,
             lsec_ref, yc_ref, yn_ref, qkp_ref, qkc_ref, sinks_ref, ga_ref, gc_ref, cw_ref,
             dproj_ref, dkv_ref, dga_ref, dgc_ref, dsinks_ref, dcw_ref, dk_carry, dv_carry):
        n = pl.program_id(0)
        first = n == 0
        live = n < nb
        has_next = n < nb - 1
        cos_p, sgn_p = ropep_ref[:, 0:V7X_LANES], ropep_ref[:, V7X_LANES:2 * V7X_LANES]
        cos_c, sgn_c = ropec_ref[:, 0:V7X_LANES], ropec_ref[:, V7X_LANES:2 * V7X_LANES]

        @pl.when(first)
        def _():
            dk_carry[...] = jnp.zeros(dk_carry.shape, _F32)
            dv_carry[...] = jnp.zeros(dv_carry.shape, _F32)

        def write_kv(dk2, dv2, cos, sgn):
            lo = _lane((WINDOW, 128)) < HEAD_DIM
            for g in range(KV_WIDTH // 128):
                dk = jnp.where(lo, _fold_halves(dk2[2 * g]), _fold_halves(dk2[2 * g + 1]))
                dv = jnp.where(lo, _fold_halves(dv2[2 * g]), _fold_halves(dv2[2 * g + 1]))
                dkv_ref[:, 128 * g:128 * g + 128] = _rope(dk, cos, sgn, -1.0).astype(dkv_ref.dtype)
                dkv_ref[:, KV_WIDTH + 128 * g:KV_WIDTH + 128 * g + 128] = dv.astype(dkv_ref.dtype)

        @pl.when(jnp.logical_not(live))
        def _():
            write_kv([dk_carry[h] for h in range(N_KV_HEADS)], [dv_carry[h] for h in range(N_KV_HEADS)], cos_c, sgn_c)

        @pl.when(live)
        def _():
            block_step(pp_ref, pc_ref, pn_ref, dmc_ref, dmn_ref, ac_ref, lsec_ref, yc_ref, yn_ref, qkp_ref, qkc_ref,
                       sinks_ref, ga_ref, gc_ref, cw_ref, dproj_ref, dga_ref, dgc_ref, dsinks_ref, dcw_ref, dk_carry,
                       dv_carry, first, has_next, cos_p, sgn_p, cos_c, sgn_c, write_kv)

    def block_step(pp_ref, pc_ref, pn_ref, dmc_ref, dmn_ref, ac_ref, lsec_ref, yc_ref, yn_ref, qkp_ref, qkc_ref,
                   sinks_ref, ga_ref, gc_ref, cw_ref, dproj_ref, dga_ref, dgc_ref, dsinks_ref, dcw_ref, dk_carry,
                   dv_carry, first, has_next, cos_p, sgn_p, cos_c, sgn_c, write_kv):
        da_c, dga = _rms_bwd(dmc_ref[:, 0:aw], ac_ref[...], ga_ref[...])
        _accumulate(dga_ref, dga, first)
        kk = jnp.concatenate([qkp_ref[:, aw:aw + KV_WIDTH], qkc_ref[:, aw:aw + KV_WIDTH]], axis=0).astype(_F32)
        vv = jnp.concatenate([pp_ref[:, dm.o_v:dm.o_v + KV_WIDTH], pc_ref[:, dm.o_v:dm.o_v + KV_WIDTH]], axis=0)
        group, pairs = dm.group, dm.group // 2
        valid_c = _band_mask(group, 2 * WINDOW, first)
        dk_prev, dv_prev = [], []

        def stacked(q_ref, da, o_ref, lse_ref_, h):
            cols = [slice(128 * j, 128 * j + 128) for j in range(pairs * h, pairs * (h + 1))]
            q4 = _stack_heads([q_ref[:, c] for c in cols])
            do4 = _stack_heads([da[:, c] for c in cols])
            lo = _lane((WINDOW, 128)) < HEAD_DIM
            deltas = []
            for c in cols:
                prod = o_ref[:, c] * da[:, c]
                deltas += [jnp.sum(jnp.where(lo, prod, 0.0), axis=1, keepdims=True),
                           jnp.sum(jnp.where(lo, 0.0, prod), axis=1, keepdims=True)]
            lse4 = jnp.concatenate([lse_ref_[:, group * h + r:group * h + r + 1] for r in range(group)], axis=0)
            return q4, do4, lse4, deltas

        def scores_bwd(q4, do4, lse4, delta4, keys, vals, valid):
            sc = _dot(q4, keys, "nt") * ATTN_SCALE
            p = jnp.exp(jnp.where(valid, sc - lse4, MASKED))
            return p.astype(_CDT), (p * (_dot(do4, vals, "nt") - delta4) * ATTN_SCALE).astype(_CDT)

        for h in range(N_KV_HEADS):
            k2, v2 = _dup_head(kk, h).astype(_CDT), _dup_head(vv, h).astype(_CDT)
            q4, do4, lse4, deltas = stacked(qkc_ref, da_c, ac_ref, lsec_ref, h)
            delta4 = jnp.concatenate(deltas, axis=0)
            p, ds = scores_bwd(q4, do4, lse4, delta4, k2, v2, valid_c)
            for i, dq in enumerate(_unstack_heads(_dot(ds, k2, "nn"), pairs)):
                j = pairs * h + i
                dproj_ref[:, 128 * j:128 * j + 128] = _rope(dq, cos_c, sgn_c, -1.0).astype(dproj_ref.dtype)
            dk = _dot(ds, q4, "tn")
            dv = _dot(p, do4, "tn")
            dk_prev.append(dk_carry[h] + dk[0:WINDOW])
            dv_prev.append(dv_carry[h] + dv[0:WINDOW])
            dk_carry[h] = dk[WINDOW:2 * WINDOW]
            dv_carry[h] = dv[WINDOW:2 * WINDOW]
            heads = slice(group * h, group * (h + 1))
            sink_row, delta_heads = jnp.zeros((1, group), _F32), jnp.zeros((WINDOW, group), _F32)
            for r in range(group):
                sink_row = jnp.where(_lane((1, group)) == r, sinks_ref[0, group * h + r], sink_row)
                delta_heads = jnp.where(_lane((WINDOW, group)) == r, deltas[r], delta_heads)
            loss_sink = jnp.exp(sink_row - lsec_ref[:, heads]) * delta_heads
            _accumulate(dsinks_ref.at[:, heads], -jnp.sum(loss_sink, axis=0, keepdims=True), first)
        write_kv(dk_prev, dv_prev, cos_p, sgn_p)

        bg = pc_ref[:, dm.o_bg:dm.o_bg + cw]
        yc = yc_ref[...]
        dconv, dgc = _rms_bwd(dmc_ref[:, aw:d], bg * yc, gc_ref[...])
        _accumulate(dgc_ref, dgc, first)
        dproj_ref[:, dm.o_bg:dm.o_bg + cw] = (dconv * yc).astype(dproj_ref.dtype)
        dy = dconv * bg
        bg_n = pn_ref[:, dm.o_bg:dm.o_bg + cw]
        dconv_n, _ = _rms_bwd(dmn_ref[:, aw:d], bg_n * yn_ref[...], gc_ref[...])
        halo = jnp.where(has_next, dconv_n * bg_n, 0.0)
        dy1 = _shift_up(dy, halo, 1)
        dy2 = _shift_up(dy, halo, 2)
        dz = cw_ref[2:3, :] * dy + cw_ref[1:2, :] * dy1 + cw_ref[0:1, :] * dy2
        cg = pc_ref[:, dm.o_cg:dm.o_cg + cw]
        u = pc_ref[:, dm.o_u:dm.o_u + cw]
        dproj_ref[:, dm.o_cg:dm.o_cg + cw] = (dz * u).astype(dproj_ref.dtype)
        dproj_ref[:, dm.o_u:dm.o_u + cw] = (dz * cg).astype(dproj_ref.dtype)
        z = cg * u
        dcw = jnp.concatenate(
            [jnp.sum(z * t, axis=0, keepdims=True) for t in (dy2, dy1, dy)]
            + [jnp.zeros((V7X_SUBLANES - 3, cw), _F32)], axis=0)
        _accumulate(dcw_ref, dcw, first)

    at = lambda n: jnp.minimum(n, nb - 1)
    prev = lambda n: (jnp.maximum(at(n) - 1, 0), 0)
    cur = lambda n: (at(n), 0)
    done = lambda n: (jnp.maximum(n - 1, 0), 0)
    nxt8 = lambda n: (jnp.minimum((at(n) + 1) * (WINDOW // V7X_SUBLANES), s // V7X_SUBLANES - 1), 0)
    fixed = lambda n: (0, 0)
    blocks = WINDOW * (2 * inw * 4 + d * 4 + aw * 4 + cw * 4 + inw * 2 + 2 * KV_WIDTH * 2)
    carry = [pltpu.VMEM((N_KV_HEADS, WINDOW, 128), _F32), pltpu.VMEM((N_KV_HEADS, WINDOW, 128), _F32)]
    n_in, n_out = 17, 6
    ca = _CommArgs(list(comm), n_in, n_out)
    return pl.pallas_call(
        _carrying(body, n_in, n_out, nb + 1, ca, n_scratch=len(carry)), name="mixer_bwd", grid=(nb + 1,),
        in_specs=[pl.BlockSpec((WINDOW, inw), prev), pl.BlockSpec((WINDOW, inw), cur), pl.BlockSpec((V7X_SUBLANES, inw), nxt8),
                  pl.BlockSpec((WINDOW, 2 * V7X_LANES), prev), pl.BlockSpec((WINDOW, 2 * V7X_LANES), cur),
                  pl.BlockSpec((WINDOW, d), cur), pl.BlockSpec((V7X_SUBLANES, d), nxt8),
                  pl.BlockSpec((WINDOW, aw), cur), pl.BlockSpec((WINDOW, nq), cur),
                  pl.BlockSpec((WINDOW, cw), cur), pl.BlockSpec((V7X_SUBLANES, cw), nxt8),
                  pl.BlockSpec((WINDOW, aw + KV_WIDTH), prev), pl.BlockSpec((WINDOW, aw + KV_WIDTH), cur),
                  pl.BlockSpec(memory_space=pltpu.SMEM),
                  pl.BlockSpec((1, aw), fixed), pl.BlockSpec((1, cw), fixed), pl.BlockSpec((V7X_SUBLANES, cw), fixed)]
        + [_ANY] * len(ca.operands),
        out_specs=[pl.BlockSpec((WINDOW, inw), cur), pl.BlockSpec((WINDOW, 2 * KV_WIDTH), done),
                   pl.BlockSpec((1, aw), fixed), pl.BlockSpec((1, cw), fixed),
                   pl.BlockSpec((1, nq), fixed), pl.BlockSpec((V7X_SUBLANES, cw), fixed)] + [_ANY] * len(ca.out_shape),
        out_shape=[jax.ShapeDtypeStruct((s, inw), _CDT), jax.ShapeDtypeStruct((s, 2 * KV_WIDTH), _CDT),
                   jax.ShapeDtypeStruct((1, aw), _F32), jax.ShapeDtypeStruct((1, cw), _F32),
                   jax.ShapeDtypeStruct((1, nq), _F32), jax.ShapeDtypeStruct((V7X_SUBLANES, cw), _F32)] + ca.out_shape,
        scratch_shapes=carry + ca.sems, input_output_aliases=ca.aliases,
        compiler_params=pltpu.CompilerParams(dimension_semantics=("arbitrary",), vmem_limit_bytes=_vmem_limit(blocks)),
    )(proj, proj, proj, rope, rope, dmixed, dmixed, attn, lse, y, y, qk, qk, sinks, g_attn, g_conv, conv_w8, *ca.operands)


def _position():
    return lax.axis_index("x"), lax.axis_index("y"), lax.axis_index("c")


def _linear(px, py, pc):
    return 4 * px + 2 * py + pc


def _comm_kernel(name, comm):
    ca = _CommArgs(list(comm), 0, 0)
    n_cin, n_cout = len(ca.operands), len(ca.out_shape)

    def body(*refs):
        cin, cout, sems = refs[:n_cin], refs[n_cin:n_cin + n_cout], refs[n_cin + n_cout:]
        ca.start(cin, cout, sems)
        ca.middle(cin, cout, sems)
        ca.finish(cin, cout, sems)

    return pl.pallas_call(
        body, name=name, out_shape=ca.out_shape, in_specs=[_ANY] * n_cin, out_specs=[_ANY] * n_cout,
        scratch_shapes=ca.sems, input_output_aliases=ca.aliases,
    )(*ca.operands)


def _gather_op(units):
    n = len(units)
    inputs, outputs, aliases = [], [], {}
    for shard, _, _, _ in units:
        inputs.append(shard)
        outputs.append(jax.ShapeDtypeStruct((N_DEV * shard.shape[0], shard.shape[1]), shard.dtype))
    for u, (_, buf, _, _) in enumerate(units):
        if buf is not None:
            aliases[len(inputs)] = u
            inputs.append(buf)

    def plan(ins, outs, sems, north):
        send_sems, recv_sems, local_sems = sems
        x, y, c = _position()
        me, sibling = (x, y, c), (x, y, 1 - c)
        xn, yn, dg = (1 - x, y), (x, 1 - y), (1 - x, 1 - y)
        via, to, k_via, k_other = (yn, xn, 2, 1) if north else (xn, yn, 1, 2)

        def rows(u, px, py, pc):
            shard, _, r0, r1 = units[u]
            return outs[u].at[pl.ds(pl.multiple_of(_linear(px, py, pc) * shard.shape[0] + r0, 16), r1 - r0), :]

        def own(u):
            _, _, r0, r1 = units[u]
            return ins[u].at[pl.ds(r0, r1 - r0), :]

        def copy(u, k, block, to_, src=None):
            return pltpu.make_async_remote_copy(
                src_ref=rows(u, *block) if src is None else src, dst_ref=rows(u, *block),
                send_sem=send_sems.at[u, k], recv_sem=recv_sems.at[u, k], device_id=to_, device_id_type=_MESH)

        us = range(n)
        return dict(
            mine=[pltpu.make_async_copy(own(u), rows(u, *me), local_sems.at[u]) for u in us],
            first=[cp for u in us for cp in (copy(u, 0, me, sibling, src=own(u)), copy(u, 1, me, (*xn, c), src=own(u)),
                                             copy(u, 2, me, (*yn, c), src=own(u)))],
            relay=[copy(u, 3, (*via, c), (*to, c)) for u in us],
            arrived={1: [copy(u, 1, (*xn, c), me) for u in us], 2: [copy(u, 2, (*yn, c), me) for u in us],
                     3: [copy(u, 3, (*dg, c), me) for u in us]},
            passed={1: [copy(u, 4, (*xn, c), sibling) for u in us], 2: [copy(u, 5, (*yn, c), sibling) for u in us],
                    3: [copy(u, 6, (*dg, c), sibling) for u in us]},
            rest=[cp for u in us for cp in (copy(u, 0, sibling, me), copy(u, 4, (*xn, 1 - c), me),
                                            copy(u, 5, (*yn, 1 - c), me), copy(u, 6, (*dg, 1 - c), me))],
            k_via=k_via, k_other=k_other)

    def land(p, k):
        for arrived, onward in zip(p["arrived"][k], p["passed"][k]):
            arrived.wait_recv()
            onward.start()

    def by_core(fn):
        c = lax.axis_index("c")
        for north in (True, False):
            pl.when(c == (1 if north else 0))(functools.partial(fn, north))

    def start(ins, outs, sems):
        p = plan(ins, outs, sems, True)
        for cp in p["mine"] + p["first"]:
            cp.start()

    def middle(ins, outs, sems):
        def go(north):
            p = plan(ins, outs, sems, north)
            land(p, p["k_via"])
            for cp in p["relay"]:
                cp.start()
            land(p, p["k_other"])
        by_core(go)

    def finish(ins, outs, sems):
        def go(north):
            p = plan(ins, outs, sems, north)
            land(p, 3)
            for cp in p["rest"]:
                cp.wait_recv()
            for cp in p["first"] + p["relay"] + [cp for k in (1, 2, 3) for cp in p["passed"][k]]:
                cp.wait_send()
            for cp in p["mine"]:
                cp.wait()
        by_core(go)

    sems = [pltpu.SemaphoreType.DMA((n, 7)), pltpu.SemaphoreType.DMA((n, 7)), pltpu.SemaphoreType.DMA((n,))]
    return _Comm(inputs, outputs, aliases, sems, start, finish, middle)


def _peers(x, y, c):
    out = []
    for k in range(1, N_DEV):
        fx, fy, fc = (k >> 2) & 1, (k >> 1) & 1, k & 1
        out.append((1 - x if fx else x, 1 - y if fy else y, 1 - c if fc else c))
    return out


def _exchange_op(partials):
    n = len(partials)
    outputs = [jax.ShapeDtypeStruct((4, p.shape[0] // N_DEV, p.shape[1]), p.dtype) for p in partials]

    def plan(ins, outs, sems):
        send_sems, recv_sems = sems
        x, y, c = _position()
        out = []
        for a in range(n):
            r = outs[a].shape[1]
            for ch in range(4):
                out.append(pltpu.make_async_remote_copy(
                    src_ref=ins[a].at[pl.ds(pl.multiple_of((2 * ch + 1 - c) * r, 16), r), :], dst_ref=outs[a].at[ch],
                    send_sem=send_sems.at[a, ch], recv_sem=recv_sems.at[a, ch], device_id=(x, y, 1 - c),
                    device_id_type=_MESH))
        return out

    def start(ins, outs, sems):
        for cp in plan(ins, outs, sems):
            cp.start()

    def finish(ins, outs, sems):
        copies = plan(ins, outs, sems)
        for cp in copies:
            cp.wait_recv()
        for cp in copies:
            cp.wait_send()

    sems = [pltpu.SemaphoreType.DMA((n, 4)), pltpu.SemaphoreType.DMA((n, 4))]
    return _Comm(list(partials), outputs, {}, sems, start, finish)


def _chip_send_op(units):
    n = len(units)
    inputs, outputs, aliases = [], [], {}
    for q, _, _, _ in units:
        inputs.append(q)
        outputs.append(jax.ShapeDtypeStruct(q.shape, q.dtype))
    for u, (_, buf, _, _) in enumerate(units):
        if buf is not None:
            aliases[len(inputs)] = u
            inputs.append(buf)

    def plan(ins, outs, sems):
        send_sems, recv_sems, local_sems = sems
        x, y, c = _position()
        my_chip = 2 * x + y
        chips = [(1 - x, y), (x, 1 - y), (1 - x, 1 - y)]
        mine, sends, arrivals = [], [], []
        for u, (_, _, r0, r1) in enumerate(units):
            span = pl.ds(r0, r1 - r0)
            mine.append(pltpu.make_async_copy(ins[u].at[my_chip, span, :], outs[u].at[my_chip, span, :], local_sems.at[u]))
            for k, (px, py) in enumerate(chips):
                sends.append(pltpu.make_async_remote_copy(
                    src_ref=ins[u].at[2 * px + py, span, :], dst_ref=outs[u].at[my_chip, span, :],
                    send_sem=send_sems.at[u, k], recv_sem=recv_sems.at[u, k], device_id=(px, py, c), device_id_type=_MESH))
                arrivals.append(pltpu.make_async_remote_copy(
                    src_ref=ins[u].at[my_chip, span, :], dst_ref=outs[u].at[2 * px + py, span, :],
                    send_sem=send_sems.at[u, k], recv_sem=recv_sems.at[u, k], device_id=(px, py, c), device_id_type=_MESH))
        return mine, sends, arrivals

    def start(ins, outs, sems):
        mine, sends, _ = plan(ins, outs, sems)
        for cp in mine + sends:
            cp.start()

    def finish(ins, outs, sems):
        mine, sends, arrivals = plan(ins, outs, sems)
        for cp in arrivals:
            cp.wait_recv()
        for cp in sends:
            cp.wait_send()
        for cp in mine:
            cp.wait()

    sems = [pltpu.SemaphoreType.DMA((n, 3)), pltpu.SemaphoreType.DMA((n, 3)), pltpu.SemaphoreType.DMA((n,))]
    return _Comm(inputs, outputs, aliases, sems, start, finish)


def _pair_sum(name, partial, received):
    _, rows, cols = received.shape
    tr = _pick(rows, (704, 576, 352, 288, 256, 128, 64, 32, 16))
    p4 = partial.reshape(4, 2, rows, cols)
    kind = jnp.reshape(lax.axis_index("c"), (1,)).astype(jnp.int32)

    def body(kind_ref, p_ref, r_ref, o_ref):
        o_ref[0] = (p_ref[0, 0].astype(_F32) + r_ref[0].astype(_F32)).astype(o_ref.dtype)

    return pl.pallas_call(
        body, name=name,
        grid_spec=pltpu.PrefetchScalarGridSpec(
            num_scalar_prefetch=1, grid=(4, rows // tr),
            in_specs=[pl.BlockSpec((1, 1, tr, cols), lambda ch, i, kind_ref: (ch, kind_ref[0], i, 0)),
                      pl.BlockSpec((1, tr, cols), lambda ch, i, kind_ref: (ch, i, 0))],
            out_specs=pl.BlockSpec((1, tr, cols), lambda ch, i, kind_ref: (ch, i, 0))),
        out_shape=jax.ShapeDtypeStruct(received.shape, received.dtype),
        compiler_params=pltpu.CompilerParams(
            dimension_semantics=("arbitrary", "arbitrary"),
            vmem_limit_bytes=_vmem_limit(3 * _nbytes((tr, cols), received.dtype))),
    )(kind, p4, received)


def _all_reduce_small(name, v):
    rows = v.shape[0]

    def body(v_ref, out_ref, land_ref, send_sems, recv_sems):
        x, y, c = _position()
        me = _linear(x, y, c)
        peers = _peers(x, y, c)
        land_ref[me] = v_ref[...]
        sends = [pltpu.make_async_remote_copy(
            src_ref=v_ref, dst_ref=land_ref.at[me], send_sem=send_sems.at[k], recv_sem=recv_sems.at[k],
            device_id=peer, device_id_type=_MESH) for k, peer in enumerate(peers)]
        for cp in sends:
            cp.start()
        for k, peer in enumerate(peers):
            pltpu.make_async_remote_copy(
                src_ref=v_ref, dst_ref=land_ref.at[_linear(*peer)], send_sem=send_sems.at[k], recv_sem=recv_sems.at[k],
                device_id=peer, device_id_type=_MESH).wait_recv()
        for cp in sends:
            cp.wait_send()
        total = land_ref[0]
        for s in range(1, N_DEV):
            total = total + land_ref[s]
        out_ref[...] = total

    return pl.pallas_call(
        body, name=name, out_shape=jax.ShapeDtypeStruct(v.shape, _F32),
        in_specs=[pl.BlockSpec(memory_space=pltpu.VMEM)], out_specs=pl.BlockSpec(memory_space=pltpu.VMEM),
        scratch_shapes=[pltpu.VMEM((N_DEV, rows, V7X_LANES), _F32), pltpu.SemaphoreType.DMA((7,)), pltpu.SemaphoreType.DMA((7,))],
    )(v)


def _adamw(name, w, slots, m, v):
    rows, cols = w.shape
    n_slots = slots.shape[0]
    tr = _pick(rows, (176, 144, 128, 64, 32, 16, 8))

    def body(w_ref, s_ref, m_ref, v_ref, g_ref, d_ref, nm_ref, nv_ref):
        g = s_ref[0].astype(_F32)
        for k in range(1, n_slots):
            g = g + s_ref[k].astype(_F32)
        nm = ADAM_B1 * m_ref[...] + (1.0 - ADAM_B1) * g
        nv = ADAM_B2 * v_ref[...] + (1.0 - ADAM_B2) * (g * g)
        m_hat = nm / (1.0 - ADAM_B1 ** ADAM_STEP)
        v_hat = nv / (1.0 - ADAM_B2 ** ADAM_STEP)
        g_ref[...] = g
        d_ref[...] = -ADAM_LR * (m_hat / (jnp.sqrt(v_hat) + ADAM_EPS) + ADAM_WD * w_ref[...])
        nm_ref[...] = nm
        nv_ref[...] = nv

    spec = pl.BlockSpec((tr, cols), lambda i: (i, 0))
    blocks = 7 * tr * cols * 4 + _nbytes((n_slots, tr, cols), slots.dtype)
    return pl.pallas_call(
        body, name=name, grid=(rows // tr,),
        in_specs=[spec, pl.BlockSpec((n_slots, tr, cols), lambda i: (0, i, 0)), spec, spec], out_specs=[spec] * 4,
        out_shape=[jax.ShapeDtypeStruct((rows, cols), _F32)] * 4,
        compiler_params=pltpu.CompilerParams(dimension_semantics=("arbitrary",), vmem_limit_bytes=_vmem_limit(blocks)),
    )(w, slots, m, v)


def _pad_rows(a, rows):
    return jnp.pad(a, ((0, rows - a.shape[0]), (0, 0)))


def _pack(parts):
    rows, spans, at = [], [], 0
    for p in parts:
        p = p.reshape(-1)
        r = -(-p.shape[0] // V7X_LANES)
        rows.append(jnp.pad(p, (0, r * V7X_LANES - p.shape[0])).reshape(r, V7X_LANES))
        spans.append((at, r, p.shape[0]))
        at += r
    packed = jnp.concatenate(rows, axis=0)
    return _pad_rows(packed, -(-at // V7X_SUBLANES) * V7X_SUBLANES), spans


def _unpack(packed, spans, shapes):
    return [packed[at:at + r].reshape(-1)[:size].reshape(shape) for (at, r, size), shape in zip(spans, shapes)]


def kernel(x, positions, w_in, conv_w, sinks, g_attn, g_conv, w_out, ln1_g, ln1_b, w_gate, w_up, w_down, ln2_g, ln2_b, loss_target, m_w_in, m_conv_w, m_sinks, m_g_attn, m_g_conv, m_w_out, m_ln1_g, m_ln1_b, m_w_gate, m_w_up, m_w_down, m_ln2_g, m_ln2_b, v_w_in, v_conv_w, v_sinks, v_g_attn, v_g_conv, v_w_out, v_ln1_g, v_ln1_b, v_w_gate, v_w_up, v_w_down, v_ln2_g, v_ln2_b):
    _, s, d = x.shape
    d_ff = N_DEV * w_gate.shape[2]
    dm = _Dims(s, d, d_ff)
    aw, cw, nq, inw = dm.aw, dm.cw, dm.nq, dm.inw
    x2 = x[0]
    pos = positions[0].reshape(s, 1)
    inv_freq = ROPE_THETA ** (-jnp.arange(0, ROT_DIM, 2, dtype=_F32) / ROT_DIM)
    invf = jnp.tile(inv_freq, V7X_LANES // (ROT_DIM // 2)).reshape(1, V7X_LANES)

    conv_cols = conv_w.shape[2]
    sh_in, sh_out = w_in[0].T.astype(_CDT), w_out[0].astype(_CDT)
    sh_gate, sh_up, sh_down = w_gate[0].T.astype(_CDT), w_up[0].T.astype(_CDT), w_down[0].astype(_CDT)
    r_in, r_out, r_ff = sh_in.shape[0], sh_out.shape[0], sh_gate.shape[0]
    q_ff = r_ff // 4
    assert q_ff % 16 == 0
    def prepare_body(x_ref, pos_ref, invf_ref, xc_ref, rope_ref):
        xc_ref[...] = x_ref[...].astype(_CDT)
        cos, sgn = _rope_tables(pos_ref[...], invf_ref[...])
        rope_ref[:, 0:V7X_LANES] = cos
        rope_ref[:, V7X_LANES:2 * V7X_LANES] = sgn

    x_c, rope, w_in_t, conv_all = _row_kernel(
        "prepare_gather_w_in", prepare_body, [x2, pos], [invf], [((s, d), _CDT), ((s, 2 * V7X_LANES), _F32)], [],
        comm=[_gather_op([(sh_in, None, 0, r_in), (_pad_rows(conv_w[0], 16), None, 0, 16)])])
    conv_full = conv_all.reshape(N_DEV, 16, conv_cols)[:, :3, :].transpose(1, 0, 2).reshape(3, cw)
    conv_w8 = _pad_rows(conv_full, V7X_SUBLANES)

    tm = _pick(s, (1024, 512, 256, 128))
    tm2 = _pick(s, (2048, 1024, 512, 256, 128))
    tr = _pick(s, (512, 256, 128))
    tn_in = _pick(inw, (512, 256, 128))
    tn_ff = _pick(d_ff, (512, 256, 128))

    proj, w_out_f, w_gate_t = _matmul(
        "proj", [[(x_c, w_in_t, "nt")]], s, inw, d, tm2, tn_in, d, [],
        [((s, inw), _F32, (tm2, tn_in), _tile_ij)], _store_epilogue,
        comm=[_gather_op([(sh_out, None, 0, r_out), (sh_gate, None, 0, 2 * q_ff)])])
    mixed, attn, lse, y_conv, qk_rot, w_gate_t, w_up_t = _mixer_fwd(
        dm, proj, rope, sinks, g_attn, g_conv, conv_w8,
        comm=[_gather_op([(sh_gate, w_gate_t, 2 * q_ff, r_ff), (sh_up, None, 0, 2 * q_ff)])])

    def residual_epilogue(accs, ex, out, first):
        out[0][...] = DEEPNORM_ALPHA * ex[0][...] + accs[0]

    tn_d = _pick(d, (512,))
    r1, w_up_t = _matmul(
        "out_proj", [[(mixed, w_out_f, "nn")]], s, d, d, tm, tn_d, d, [(x2, (tm, tn_d), _tile_ij)],
        [((s, d), _F32, (tm, tn_d), _tile_ij)], residual_epilogue,
        comm=[_gather_op([(sh_up, w_up_t, 2 * q_ff, 3 * q_ff)])])
    h1, h1_c, w_up_t = _ln1_fwd_rows(r1, ln1_g, ln1_b, comm=[_gather_op([(sh_up, w_up_t, 3 * q_ff, r_ff)])])

    def swiglu_epilogue(accs, ex, out, first):
        gate_v, up_v = accs
        out[0][...] = gate_v
        out[1][...] = up_v
        out[2][...] = (gate_v * jax.nn.sigmoid(gate_v) * up_v).astype(_CDT)

    gate, up, act, w_down_f = _matmul(
        "gate_up", [[(h1_c, w_gate_t, "nt")], [(h1_c, w_up_t, "nt")]], s, d_ff, d, tm, tn_ff, d, [],
        [((s, d_ff), _F32, (tm, tn_ff), _tile_ij), ((s, d_ff), _F32, (tm, tn_ff), _tile_ij),
         ((s, d_ff), _CDT, (tm, tn_ff), _tile_ij)], swiglu_epilogue,
        comm=[_gather_op([(sh_down, None, 0, r_ff)])])

    (r2,) = _matmul("down", [[(act, w_down_f, "nn")]], s, d, d_ff, tm, tn_d, d_ff, [(h1, (tm, tn_d), _tile_ij)],
                    [((s, d), _F32, (tm, tn_d), _tile_ij)], residual_epilogue)
    dr2, dr2_c, loss_acc, d_ln2_g, d_ln2_b = _ln2_loss_bwd(r2, loss_target[0], ln2_g, ln2_b)

    def swiglu_bwd_epilogue(accs, ex, out, first):
        gate_v, up_v = ex[0][...], ex[1][...]
        sig = jax.nn.sigmoid(gate_v)
        out[0][...] = (accs[0] * up_v * (sig * (1.0 + gate_v * (1.0 - sig)))).astype(_CDT)
        out[1][...] = (accs[0] * (gate_v * sig)).astype(_CDT)

    dgate, dup = _matmul(
        "dact", [[(dr2_c, w_down_f, "nt")]], s, d_ff, d, tm2, tn_ff, d,
        [(gate, (tm2, tn_ff), _tile_ij), (up, (tm2, tn_ff), _tile_ij)],
        [((s, d_ff), _CDT, (tm2, tn_ff), _tile_ij), ((s, d_ff), _CDT, (tm2, tn_ff), _tile_ij)], swiglu_bwd_epilogue)
    def weight_grad(name, a, b, comm=()):
        rows = a.shape[1]
        tw, tn_w = _pick(rows, (512, 256, 128)), d
        return _matmul(name, [[(a, b, "tn")]], rows, d, s, tw, tn_w, s, [],
                       [((rows, d), _CDT, (tw, tn_w), _tile_ij)], _store_epilogue, comm=comm, j_outer=True)

    (dw_down,) = weight_grad("dw_down", act, dr2_c)
    dw_gate_t, x_down = weight_grad("dw_gate", dgate, h1_c, comm=[_exchange_op([dw_down])])
    q_down = _pair_sum("chip_sum_w_down", dw_down, x_down)
    dw_up_t, l_down, x_gate = weight_grad(
        "dw_up", dup, h1_c, comm=[_chip_send_op([(q_down, None, 0, 2 * q_ff)]), _exchange_op([dw_gate_t])])
    q_gate = _pair_sum("chip_sum_w_gate", dw_gate_t, x_gate)

    tn_h = _pick(d, (512,))
    dh1, l_down, l_gate, x_up = _matmul(
        "dh1", [[(dgate, w_gate_t, "nn"), (dup, w_up_t, "nn")]], s, d, d_ff, tr, tn_h, d_ff,
        [(dr2, (tr, tn_h), _tile_ij)], [((s, d), _F32, (tr, tn_h), _tile_ij)], residual_epilogue,
        comm=[_chip_send_op([(q_down, l_down, 2 * q_ff, r_ff), (q_gate, None, 0, r_ff)]), _exchange_op([dw_up_t])])
    q_up = _pair_sum("chip_sum_w_up", dw_up_t, x_up)
    dr1, dr1_c, d_ln1_g, d_ln1_b = _ln1_bwd_rows(dh1, r1, ln1_g)
    (dmixed,) = _matmul("dmixed", [[(dr1_c, w_out_f, "nt")]], s, d, d, tm2, tn_d, d, [],
                        [((s, d), _F32, (tm2, tn_d), _tile_ij)], _store_epilogue)
    (dw_out,) = weight_grad("dw_out", mixed, dr1_c)
    dproj, dkv, d_g_attn, d_g_conv, d_sinks, d_conv8, l_up, x_out = _mixer_bwd(
        dm, proj, rope, sinks, g_attn, g_conv, conv_w8, dmixed, attn, lse, y_conv, qk_rot,
        comm=[_chip_send_op([(q_up, None, 0, r_ff)]), _exchange_op([dw_out])])
    dproj = _patch_columns("dproj_kv", dproj, dkv, dm.o_k)
    q_out = _pair_sum("chip_sum_w_out", dw_out, x_out)
    dw_in_t, l_out = weight_grad("dw_in", dproj, x_c, comm=[_chip_send_op([(q_out, None, 0, r_out)])])
    (x_in,) = _comm_kernel("exchange_w_in", [_exchange_op([dw_in_t])])
    q_in = _pair_sum("chip_sum_w_in", dw_in_t, x_in)

    grad_x, l_in = _matmul("dx", [[(dproj, w_in_t, "nn")]], s, d, inw, tm, tn_d, inw,
                           [(dr1, (tm, tn_d), _tile_ij)], [((s, d), _F32, (tm, tn_d), _tile_ij)], residual_epilogue,
                           comm=[_chip_send_op([(q_in, None, 0, r_in)])])

    small_parts = [d_conv8[:3], d_sinks, d_g_attn, d_g_conv, d_ln1_g, d_ln1_b, d_ln2_g, d_ln2_b, loss_acc[0:1, 0:1]]
    packed, spans = _pack(small_parts)
    reduced = _unpack(_all_reduce_small("reduce_small", packed), spans, [p.shape for p in small_parts])
    g_conv_full, g_sinks, g_g_attn, g_g_conv, g_ln1_g, g_ln1_b, g_ln2_g, g_ln2_b, loss_sum = reduced
    me = _linear(*_position())
    g_conv_w = lax.dynamic_slice(g_conv_full, (0, me * conv_cols), (3, conv_cols))
    loss = loss_sum[0, 0]

    big = {"w_in": (w_in[0].T, l_in, m_w_in[0].T, v_w_in[0].T), "w_out": (w_out[0], l_out, m_w_out[0], v_w_out[0]),
           "w_gate": (w_gate[0].T, l_gate, m_w_gate[0].T, v_w_gate[0].T),
           "w_up": (w_up[0].T, l_up, m_w_up[0].T, v_w_up[0].T), "w_down": (w_down[0], l_down, m_w_down[0], v_w_down[0])}
    res = {nm: tuple(_adamw(f"adamw_{nm}", w, slots, m, v)) for nm, (w, slots, m, v) in big.items()}
    for nm in ("w_in", "w_gate", "w_up"):
        res[nm] = tuple(a.T for a in res[nm])
    small_names = ["conv_w", "sinks", "g_attn", "g_conv", "ln1_g", "ln1_b", "ln2_g", "ln2_b"]
    small_w = [conv_w, sinks, g_attn, g_conv, ln1_g, ln1_b, ln2_g, ln2_b]
    small_g = [g_conv_w[None], g_sinks, g_g_attn, g_g_conv, g_ln1_g, g_ln1_b, g_ln2_g, g_ln2_b]
    small_m = [m_conv_w, m_sinks, m_g_attn, m_g_conv, m_ln1_g, m_ln1_b, m_ln2_g, m_ln2_b]
    small_v = [v_conv_w, v_sinks, v_g_attn, v_g_conv, v_ln1_g, v_ln1_b, v_ln2_g, v_ln2_b]
    pw, sp = _pack(small_w)
    pg, _ = _pack(small_g)
    pm, _ = _pack(small_m)
    pv, _ = _pack(small_v)
    shapes = [w.shape for w in small_w]
    _, sd, sm, sv = [_unpack(p, sp, shapes) for p in _adamw("adamw_small", pw, pg[None], pm, pv)]
    for i, nm in enumerate(small_names):
        res[nm] = (small_g[i].reshape(shapes[i]), sd[i], sm[i], sv[i])

    order = ["w_in", "conv_w", "sinks", "g_attn", "g_conv", "w_out", "ln1_g", "ln1_b", "w_gate", "w_up", "w_down", "ln2_g", "ln2_b"]

    def lead(a, nm):
        return a[None] if nm in big else a

    return (loss, grad_x[None],
            *[lead(res[nm][0], nm) for nm in order], *[lead(res[nm][1], nm) for nm in order],
            *[lead(res[nm][2], nm) for nm in order], *[lead(res[nm][3], nm) for nm in order])
```

```python
import functools

import jax
import jax.numpy as jnp
from jax import lax
from jax.experimental import pallas as pl
from jax.experimental.pallas import tpu as pltpu

_F32 = jnp.float32
_CDT = jnp.bfloat16

HEAD_DIM = 64
WINDOW = 128
N_KV_HEADS = 4
KV_WIDTH = N_KV_HEADS * HEAD_DIM
ROT_DIM = HEAD_DIM // 4
ROPE_THETA = 500000.0
ATTN_SCALE = HEAD_DIM ** -0.5
DEPTH = 1
DEEPNORM_ALPHA = (2 * DEPTH) ** 0.25
LN_EPS = 1e-5
RMS_EPS = 1e-6
ADAM_LR = 0.001
ADAM_B1 = 0.9
ADAM_B2 = 0.999
ADAM_EPS = 1e-08
ADAM_WD = 0.01
ADAM_STEP = 10
N_DEV = 8
MASKED = -1e30

MIB = 1024 * 1024
V7X_VMEM_BYTES = 64 * MIB
V7X_LANES = 128
V7X_SUBLANES = 8
BODY_TEMPORARIES_BYTES = 16 * MIB
VMEM_LIMIT_FLOOR_BYTES = 32 * MIB
VMEM_LIMIT_CEILING_BYTES = V7X_VMEM_BYTES - 8 * MIB
_MESH = pl.DeviceIdType.MESH
_ANY = pl.BlockSpec(memory_space=pl.ANY)


def _vmem_limit(block_bytes, scratch_bytes=0):
    want = 2 * block_bytes + scratch_bytes + BODY_TEMPORARIES_BYTES
    return int(min(max(want, VMEM_LIMIT_FLOOR_BYTES), VMEM_LIMIT_CEILING_BYTES))


def _nbytes(shape, dtype):
    n = 1
    for s in shape:
        n *= s
    return n * jnp.dtype(dtype).itemsize


def _pick(n, candidates):
    for c in candidates:
        if n % c == 0:
            return c
    raise ValueError(f"no tile of {candidates} divides {n}")


_DOT_DIMS = {"nn": ((1,), (0,)), "nt": ((1,), (1,)), "tn": ((0,), (0,))}


def _dot(a, b, mode):
    return lax.dot_general(a.astype(_CDT), b.astype(_CDT), (_DOT_DIMS[mode], ((), ())),
                           preferred_element_type=_F32)


def _accumulate(ref, val, first):
    @pl.when(first)
    def _():
        ref[...] = val

    @pl.when(jnp.logical_not(first))
    def _():
        ref[...] += val


class _Comm:
    def __init__(self, inputs, outputs, aliases, sems, start, finish, middle=None):
        self.inputs, self.outputs, self.aliases, self.sems = inputs, outputs, aliases, sems
        self.start, self.finish, self.middle = start, finish, middle


def _middle_step(n_steps):
    return (2 * n_steps) // 3


class _CommArgs:
    def __init__(self, comms, n_in_before, n_out_before):
        self.comms, self.operands, self.out_shape, self.aliases, self.sems, self.at = comms, [], [], {}, [], []
        for cm in comms:
            self.at.append((len(self.operands), len(self.out_shape), len(self.sems)))
            for i_in, i_out in cm.aliases.items():
                self.aliases[n_in_before + len(self.operands) + i_in] = n_out_before + len(self.out_shape) + i_out
            self.operands += cm.inputs
            self.out_shape += cm.outputs
            self.sems += cm.sems

    def _each(self, in_refs, out_refs, sem_refs):
        for cm, (i0, o0, s0) in zip(self.comms, self.at):
            yield cm, (in_refs[i0:i0 + len(cm.inputs)], out_refs[o0:o0 + len(cm.outputs)], sem_refs[s0:s0 + len(cm.sems)])

    def start(self, in_refs, out_refs, sem_refs):
        for cm, refs in self._each(in_refs, out_refs, sem_refs):
            cm.start(*refs)

    def finish(self, in_refs, out_refs, sem_refs):
        for cm, refs in self._each(in_refs, out_refs, sem_refs):
            cm.finish(*refs)

    @property
    def has_middle(self):
        return any(cm.middle is not None for cm in self.comms)

    def middle(self, in_refs, out_refs, sem_refs):
        for cm, refs in self._each(in_refs, out_refs, sem_refs):
            if cm.middle is not None:
                cm.middle(*refs)


def _matmul(name, groups, m, n, k, tm, tn, tk, extras, outs, epilogue, comm=(), j_outer=False):
    assert m % tm == 0 and n % tn == 0 and k % tk == 0, (name, m, n, k, tm, tn, tk)
    nk = k // tk
    terms = [t for g in groups for t in g]
    operands, in_specs, block_bytes = [], [], 0

    def spec(blk, imap):
        return pl.BlockSpec(blk, (lambda g0, g1, kk: imap(g1, g0, kk)) if j_outer else imap)

    for a, b, mode in terms:
        assert a.shape == ((k, m) if mode == "tn" else (m, k)), (name, a.shape, mode)
        assert b.shape == ((n, k) if mode == "nt" else (k, n)), (name, b.shape, mode)
        if mode == "tn":
            a_blk, a_map = (tk, tm), (lambda i, j, kk: (kk, i))
        else:
            a_blk, a_map = (tm, tk), (lambda i, j, kk: (i, kk))
        if mode == "nt":
            b_blk, b_map = (tn, tk), (lambda i, j, kk: (j, kk))
        else:
            b_blk, b_map = (tk, tn), (lambda i, j, kk: (kk, j))
        operands += [a, b]
        in_specs += [spec(a_blk, a_map), spec(b_blk, b_map)]
        block_bytes += _nbytes(a_blk, a.dtype) + _nbytes(b_blk, b.dtype)
    for arr, blk, imap in extras:
        operands.append(arr)
        in_specs.append(spec(blk, lambda i, j, kk, imap=imap: imap(i, j)))
        block_bytes += _nbytes(blk, arr.dtype)
    out_shape, out_specs = [], []
    for shape, dtype, blk, imap in outs:
        out_shape.append(jax.ShapeDtypeStruct(shape, dtype))
        out_specs.append(spec(blk, lambda i, j, kk, imap=imap: imap(i, j)))
        block_bytes += _nbytes(blk, dtype)
    n_terms, n_extra, n_out, n_groups = len(terms), len(extras), len(outs), len(groups)
    scratch = [pltpu.VMEM((tm, tn), _F32) for _ in range(n_groups)] if nk > 1 else []
    ca = _CommArgs(list(comm), len(operands), n_out)
    n_cin, n_cout, n_acc = len(ca.operands), len(ca.out_shape), len(scratch)
    tiles = (m // tm, n // tn)
    grid = (tiles[1], tiles[0], nk) if j_outer else (tiles[0], tiles[1], nk)

    def body(*refs):
        refs = list(refs)
        term_refs = [refs.pop(0) for _ in range(2 * n_terms)]
        extra_refs = [refs.pop(0) for _ in range(n_extra)]
        cin_refs = [refs.pop(0) for _ in range(n_cin)]
        out_refs = [refs.pop(0) for _ in range(n_out)]
        cout_refs = [refs.pop(0) for _ in range(n_cout)]
        acc_refs = [refs.pop(0) for _ in range(n_acc)]
        sem_refs = refs
        g0, g1, kk = pl.program_id(0), pl.program_id(1), pl.program_id(2)
        first = jnp.logical_and(g0 == 0, g1 == 0)
        if comm:
            @pl.when(jnp.logical_and(first, kk == 0))
            def _():
                ca.start(cin_refs, cout_refs, sem_refs)
        if ca.has_middle:
            step = (g0 * grid[1] + g1) * nk + kk

            @pl.when(step == _middle_step(grid[0] * grid[1] * nk))
            def _():
                ca.middle(cin_refs, cout_refs, sem_refs)
        partial, t = [], 0
        for g in groups:
            s = None
            for _, _, mode in g:
                d = _dot(term_refs[2 * t][...], term_refs[2 * t + 1][...], mode)
                s = d if s is None else s + d
                t += 1
            partial.append(s)
        if nk == 1:
            epilogue(partial, extra_refs, out_refs, first)
        else:
            for acc, p in zip(acc_refs, partial):
                _accumulate(acc, p, kk == 0)

            @pl.when(kk == nk - 1)
            def _():
                epilogue([acc[...] for acc in acc_refs], extra_refs, out_refs, first)
        if comm:
            @pl.when(jnp.logical_and(jnp.logical_and(g0 == grid[0] - 1, g1 == grid[1] - 1), kk == nk - 1))
            def _():
                ca.finish(cin_refs, cout_refs, sem_refs)

    res = pl.pallas_call(
        body, name=name, grid=grid,
        in_specs=in_specs + [_ANY] * n_cin, out_specs=out_specs + [_ANY] * n_cout,
        out_shape=out_shape + ca.out_shape, scratch_shapes=scratch + ca.sems, input_output_aliases=ca.aliases,
        compiler_params=pltpu.CompilerParams(
            dimension_semantics=("arbitrary", "arbitrary", "arbitrary"),
            vmem_limit_bytes=_vmem_limit(block_bytes, n_groups * tm * tn * 4 if nk > 1 else 0)),
    )(*operands, *ca.operands)
    return list(res[:n_out]) + list(res[n_out:])


def _store_epilogue(accs, extra_refs, out_refs, first):
    for acc, ref in zip(accs, out_refs):
        ref[...] = acc.astype(ref.dtype)


def _tile_ij(i, j):
    return (i, j)


def _row_i(i, j):
    return (i, 0)


def _whole(i, j):
    return (0, 0)


def _mean(v):
    return jnp.mean(v, axis=-1, keepdims=True)


def _ln_fwd(r, g, b):
    xc = r - _mean(r)
    rstd = lax.rsqrt(_mean(xc * xc) + LN_EPS)
    xhat = xc * rstd
    return xhat * g + b, xhat, rstd


def _ln_bwd(dy, xhat, rstd, g):
    dxh = dy * g
    dr = rstd * (dxh - _mean(dxh) - xhat * _mean(dxh * xhat))
    return dr, jnp.sum(dy * xhat, axis=0, keepdims=True), jnp.sum(dy, axis=0, keepdims=True)


def _rms_fwd(a, g):
    rstd = lax.rsqrt(_mean(a * a) + RMS_EPS)
    return a * rstd * g


def _rms_bwd(dm, a, g):
    rstd = lax.rsqrt(_mean(a * a) + RMS_EPS)
    nhat = a * rstd
    dn = dm * g
    da = rstd * (dn - nhat * _mean(dn * nhat))
    return da, jnp.sum(dm * nhat, axis=0, keepdims=True)


def _lane(shape):
    return lax.broadcasted_iota(jnp.int32, shape, 1)


def _row(shape):
    return lax.broadcasted_iota(jnp.int32, shape, 0)


def _rope_tables(pos, invf):
    ang = pos.astype(_F32) * invf
    lane = _lane(ang.shape)
    in_rot = (lane % HEAD_DIM) < ROT_DIM
    first = (lane % ROT_DIM) < ROT_DIM // 2
    cos = jnp.where(in_rot, jnp.cos(ang), 1.0)
    sin = jnp.sin(ang)
    sgn = jnp.where(in_rot, jnp.where(first, -sin, sin), 0.0)
    return cos, sgn


def _rope(t, cos, sgn, sign):
    half = ROT_DIM // 2
    first = (_lane(t.shape) % ROT_DIM) < half
    partner = jnp.where(first, pltpu.roll(t, V7X_LANES - half, 1), pltpu.roll(t, half, 1))
    return t * cos + partner * (sgn * sign)


def _dup_head(t, h):
    g = t[:, 128 * (h // 2):128 * (h // 2) + 128]
    r = pltpu.roll(g, HEAD_DIM, 1)
    lo = _lane(g.shape) < HEAD_DIM
    return jnp.where(lo, g, r) if h % 2 == 0 else jnp.where(lo, r, g)


def _fold_halves(t):
    return t + pltpu.roll(t, HEAD_DIM, 1)


def _halves(t):
    lo = _lane(t.shape) < HEAD_DIM
    zero = jnp.zeros_like(t)
    return jnp.where(lo, t, zero), jnp.where(lo, zero, t)


def _band_mask(n_heads, n_keys, first_block):
    shape = (n_heads * WINDOW, n_keys)
    i = jnp.bitwise_and(_row(shape), WINDOW - 1)
    j = _lane(shape)
    valid = jnp.logical_and(j >= i + 1, j <= i + WINDOW)
    if first_block is not None:
        valid = jnp.logical_and(valid, jnp.logical_or(j >= WINDOW, jnp.logical_not(first_block)))
    return valid


def _stack_heads(pairs):
    return jnp.concatenate([half for t in pairs for half in _halves(t.astype(_CDT))], axis=0)


def _unstack_heads(t, n_pairs):
    lo = _lane((WINDOW, 128)) < HEAD_DIM
    return [jnp.where(lo, t[2 * WINDOW * i:2 * WINDOW * i + WINDOW], t[2 * WINDOW * i + WINDOW:2 * WINDOW * (i + 1)])
            for i in range(n_pairs)]


def _per_head(values):
    n_rows = len(values) * WINDOW
    block = jnp.right_shift(_row((n_rows, 1)), WINDOW.bit_length() - 1)
    out = jnp.zeros((n_rows, 1), _F32)
    for k, v in enumerate(values):
        out = jnp.where(block == k, v, out)
    return out


def _shift_down(z, halo, k):
    out = pltpu.roll(z, k, 0)
    r = _row(z.shape)
    for t in range(k):
        out = jnp.where(r == t, halo[V7X_SUBLANES - k + t:V7X_SUBLANES - k + t + 1, :], out)
    return out


def _shift_up(z, halo, k):
    rows = z.shape[0]
    out = pltpu.roll(z, rows - k, 0)
    r = _row(z.shape)
    for t in range(k):
        out = jnp.where(r == rows - k + t, halo[t:t + 1, :], out)
    return out


class _Dims:
    def __init__(self, s, d, d_ff):
        self.s, self.d, self.d_ff = s, d, d_ff
        self.aw = d // 2
        self.cw = d - self.aw
        self.nq = self.aw // HEAD_DIM
        self.group = self.nq // N_KV_HEADS
        assert self.group % 2 == 0, "a 128-lane pair of query heads must share its kv head"
        self.inw = self.aw + 2 * KV_WIDTH + 3 * self.cw
        self.o_k = self.aw
        self.o_v = self.aw + KV_WIDTH
        self.o_cg = self.aw + 2 * KV_WIDTH
        self.o_bg = self.o_cg + self.cw
        self.o_u = self.o_bg + self.cw
        self.nb = s // WINDOW
        assert s % WINDOW == 0


def _carrying(body, n_in, n_out, n_steps, ca, n_scratch=0):
    n_cin, n_cout = len(ca.operands), len(ca.out_shape)

    def wrapped(*refs):
        refs = list(refs)
        in_refs = [refs.pop(0) for _ in range(n_in)]
        cin_refs = [refs.pop(0) for _ in range(n_cin)]
        out_refs = [refs.pop(0) for _ in range(n_out)]
        cout_refs = [refs.pop(0) for _ in range(n_cout)]
        scratch_refs = [refs.pop(0) for _ in range(n_scratch)]
        if ca.comms:
            @pl.when(pl.program_id(0) == 0)
            def _():
                ca.start(cin_refs, cout_refs, refs)
        if ca.has_middle:
            @pl.when(pl.program_id(0) == _middle_step(n_steps))
            def _():
                ca.middle(cin_refs, cout_refs, refs)
        body(*in_refs, *out_refs, *scratch_refs)
        if ca.comms:
            @pl.when(pl.program_id(0) == n_steps - 1)
            def _():
                ca.finish(cin_refs, cout_refs, refs)

    return wrapped


def _row_kernel(name, body, rows_in, vecs_in, rows_out, vecs_out, comm=()):
    s = rows_in[0].shape[0]
    tr = _pick(s, (512, 256, 128))
    row = lambda a: pl.BlockSpec((tr, a[1] if isinstance(a, tuple) else a.shape[1]), lambda i: (i, 0))
    vec = lambda shape: pl.BlockSpec(tuple(shape), lambda i: (0, 0))
    n_in, n_out = len(rows_in) + len(vecs_in), len(rows_out) + len(vecs_out)
    ca = _CommArgs(list(comm), n_in, n_out)
    blocks = sum(_nbytes((tr, a.shape[1]), a.dtype) for a in rows_in) + sum(_nbytes((tr, sh[1]), dt) for sh, dt in rows_out)
    res = pl.pallas_call(
        _carrying(body, n_in, n_out, s // tr, ca), name=name, grid=(s // tr,),
        in_specs=[row(a) for a in rows_in] + [vec(v.shape) for v in vecs_in] + [_ANY] * len(ca.operands),
        out_specs=[row(sh) for sh, _ in rows_out] + [vec(sh) for sh, _ in vecs_out] + [_ANY] * len(ca.out_shape),
        out_shape=[jax.ShapeDtypeStruct(sh, dt) for sh, dt in list(rows_out) + list(vecs_out)] + ca.out_shape,
        scratch_shapes=ca.sems, input_output_aliases=ca.aliases,
        compiler_params=pltpu.CompilerParams(dimension_semantics=("arbitrary",), vmem_limit_bytes=_vmem_limit(blocks)),
    )(*rows_in, *vecs_in, *ca.operands)
    return list(res)


def _ln2_loss_bwd(r2, target, gain, bias, comm=()):
    s, d = r2.shape

    def body(r_ref, t_ref, g_ref, b_ref, dr_ref, drc_ref, loss_ref, dg_ref, db_ref):
        first = pl.program_id(0) == 0
        yv, xhat, rstd = _ln_fwd(r_ref[...], g_ref[...], b_ref[...])
        err = yv - t_ref[...]
        dr2, dg, db = _ln_bwd(err * (1.0 / d), xhat, rstd, g_ref[...])
        dr_ref[...] = dr2
        drc_ref[...] = dr2.astype(_CDT)
        _accumulate(loss_ref, jnp.zeros(loss_ref.shape, _F32) + 0.5 * jnp.sum(err * err) * (1.0 / d), first)
        _accumulate(dg_ref, dg, first)
        _accumulate(db_ref, db, first)

    return _row_kernel("ln2_loss_bwd", body, [r2, target], [gain, bias], [((s, d), _F32), ((s, d), _CDT)],
                       [((V7X_SUBLANES, V7X_LANES), _F32), ((1, d), _F32), ((1, d), _F32)], comm)


def _ln1_fwd_rows(r1, gain, bias, comm=()):
    s, d = r1.shape

    def body(r_ref, g_ref, b_ref, h_ref, hc_ref):
        h1, _, _ = _ln_fwd(r_ref[...], g_ref[...], b_ref[...])
        h_ref[...] = h1
        hc_ref[...] = h1.astype(_CDT)

    return _row_kernel("ln1", body, [r1], [gain, bias], [((s, d), _F32), ((s, d), _CDT)], [], comm)


def _ln1_bwd_rows(dh1, r1, gain, comm=()):
    s, d = dh1.shape

    def body(dh_ref, r_ref, g_ref, dr_ref, drc_ref, dg_ref, db_ref):
        first = pl.program_id(0) == 0
        _, xhat, rstd = _ln_fwd(r_ref[...], g_ref[...], 0.0)
        dr1, dg, db = _ln_bwd(dh_ref[...], xhat, rstd, g_ref[...])
        dr_ref[...] = dr1
        drc_ref[...] = dr1.astype(_CDT)
        _accumulate(dg_ref, dg, first)
        _accumulate(db_ref, db, first)

    return _row_kernel("ln1_bwd", body, [dh1, r1], [gain], [((s, d), _F32), ((s, d), _CDT)],
                       [((1, d), _F32), ((1, d), _F32)], comm)


def _mixer_fwd(dm, proj, rope, sinks, g_attn, g_conv, conv_w8, comm=()):
    s, d, aw, cw, nq, inw, nb = dm.s, dm.d, dm.aw, dm.cw, dm.nq, dm.inw, dm.nb

    def body(pp_ref, pc_ref, ropep_ref, ropec_ref, sinks_ref, ga_ref, gc_ref, cw_ref,
             mixed_ref, attn_ref, lse_ref, y_ref, qk_ref):
        n = pl.program_id(0)
        cos_c, sgn_c = ropec_ref[:, 0:V7X_LANES], ropec_ref[:, V7X_LANES:2 * V7X_LANES]
        cos_p, sgn_p = ropep_ref[:, 0:V7X_LANES], ropep_ref[:, V7X_LANES:2 * V7X_LANES]
        for g in range(KV_WIDTH // 128):
            qk_ref[:, aw + 128 * g:aw + 128 * g + 128] = _rope(
                pc_ref[:, dm.o_k + 128 * g:dm.o_k + 128 * g + 128], cos_c, sgn_c, 1.0).astype(qk_ref.dtype)
        for j in range(nq // 2):
            qk_ref[:, 128 * j:128 * j + 128] = _rope(pc_ref[:, 128 * j:128 * j + 128], cos_c, sgn_c, 1.0).astype(qk_ref.dtype)
        k_prev = jnp.concatenate([_rope(pp_ref[:, dm.o_k + 128 * g:dm.o_k + 128 * g + 128], cos_p, sgn_p, 1.0)
                                  for g in range(KV_WIDTH // 128)], axis=1)
        kk = jnp.concatenate([k_prev, qk_ref[:, aw:aw + KV_WIDTH].astype(_F32)], axis=0)
        vv = jnp.concatenate([pp_ref[:, dm.o_v:dm.o_v + KV_WIDTH], pc_ref[:, dm.o_v:dm.o_v + KV_WIDTH]], axis=0)
        group, pairs = dm.group, dm.group // 2
        valid = _band_mask(group, 2 * WINDOW, n == 0)
        for h in range(N_KV_HEADS):
            k2, v2 = _dup_head(kk, h).astype(_CDT), _dup_head(vv, h).astype(_CDT)
            q4 = _stack_heads([qk_ref[:, 128 * j:128 * j + 128] for j in range(pairs * h, pairs * (h + 1))])
            sc = jnp.where(valid, _dot(q4, k2, "nt") * ATTN_SCALE, MASKED)
            sink = _per_head([sinks_ref[0, group * h + r] for r in range(group)])
            mx = jnp.maximum(jnp.max(sc, axis=1, keepdims=True), sink)
            p = jnp.exp(sc - mx)
            den = jnp.sum(p, axis=1, keepdims=True) + jnp.exp(sink - mx)
            out = _unstack_heads(_dot(p / den, v2, "nn"), pairs)
            lse = mx + jnp.log(den)
            for r in range(group):
                lse_ref[:, group * h + r:group * h + r + 1] = lse[WINDOW * r:WINDOW * (r + 1)]
            for i in range(pairs):
                j = pairs * h + i
                attn_ref[:, 128 * j:128 * j + 128] = out[i]
        mixed_ref[:, 0:aw] = _rms_fwd(attn_ref[...], ga_ref[...]).astype(mixed_ref.dtype)

        z = pc_ref[:, dm.o_cg:dm.o_cg + cw] * pc_ref[:, dm.o_u:dm.o_u + cw]
        top = WINDOW - V7X_SUBLANES
        halo = pp_ref[top:WINDOW, dm.o_cg:dm.o_cg + cw] * pp_ref[top:WINDOW, dm.o_u:dm.o_u + cw]
        halo = jnp.where(n == 0, jnp.zeros_like(halo), halo)
        y = cw_ref[0:1, :] * _shift_down(z, halo, 2) + cw_ref[1:2, :] * _shift_down(z, halo, 1) + cw_ref[2:3, :] * z
        y_ref[...] = y
        conv = pc_ref[:, dm.o_bg:dm.o_bg + cw] * y
        mixed_ref[:, aw:d] = _rms_fwd(conv, gc_ref[...]).astype(mixed_ref.dtype)

    prev = lambda n: (jnp.maximum(n - 1, 0), 0)
    cur = lambda n: (n, 0)
    fixed = lambda n: (0, 0)
    blocks = 2 * WINDOW * inw * 4 + WINDOW * (d * 2 + aw * 4 + cw * 4 + nq * 4)
    ca = _CommArgs(list(comm), 8, 5)
    return pl.pallas_call(
        _carrying(body, 8, 5, nb, ca), name="mixer_fwd", grid=(nb,),
        in_specs=[pl.BlockSpec((WINDOW, inw), prev), pl.BlockSpec((WINDOW, inw), cur),
                  pl.BlockSpec((WINDOW, 2 * V7X_LANES), prev), pl.BlockSpec((WINDOW, 2 * V7X_LANES), cur),
                  pl.BlockSpec(memory_space=pltpu.SMEM),
                  pl.BlockSpec((1, aw), fixed), pl.BlockSpec((1, cw), fixed), pl.BlockSpec((V7X_SUBLANES, cw), fixed)]
        + [_ANY] * len(ca.operands),
        out_specs=[pl.BlockSpec((WINDOW, d), cur), pl.BlockSpec((WINDOW, aw), cur),
                   pl.BlockSpec((WINDOW, nq), cur), pl.BlockSpec((WINDOW, cw), cur),
                   pl.BlockSpec((WINDOW, aw + KV_WIDTH), cur)] + [_ANY] * len(ca.out_shape),
        out_shape=[jax.ShapeDtypeStruct((s, d), _CDT), jax.ShapeDtypeStruct((s, aw), _F32),
                   jax.ShapeDtypeStruct((s, nq), _F32), jax.ShapeDtypeStruct((s, cw), _F32),
                   jax.ShapeDtypeStruct((s, aw + KV_WIDTH), _CDT)] + ca.out_shape,
        scratch_shapes=ca.sems, input_output_aliases=ca.aliases,
        compiler_params=pltpu.CompilerParams(dimension_semantics=("arbitrary",), vmem_limit_bytes=_vmem_limit(blocks)),
    )(proj, proj, rope, rope, sinks, g_attn, g_conv, conv_w8, *ca.operands)


def _patch_columns(name, a, part, offset):
    s, pw = part.shape
    assert offset % pw == 0 and pw % V7X_LANES == 0
    tr = _pick(s, (512, 256, 128))

    def body(a_ref, p_ref, o_ref):
        del a_ref
        o_ref[...] = p_ref[...]

    return pl.pallas_call(
        body, name=name, grid=(s // tr,),
        in_specs=[_ANY, pl.BlockSpec((tr, pw), lambda i: (i, 0))],
        out_specs=pl.BlockSpec((tr, pw), lambda i: (i, offset // pw)),
        out_shape=jax.ShapeDtypeStruct(a.shape, a.dtype), input_output_aliases={0: 0},
        compiler_params=pltpu.CompilerParams(dimension_semantics=("arbitrary",)),
    )(a, part)


def _mixer_bwd(dm, proj, rope, sinks, g_attn, g_conv, conv_w8, dmixed, attn, lse, y, qk, comm=()):
    s, d, aw, cw, nq, inw, nb = dm.s, dm.d, dm.aw, dm.cw, dm.nq, dm.inw, dm.nb

    def body(pp_ref, pc_ref, pn_ref, ropep_ref, ropec_ref, dmc_ref, dmn_ref, ac_ref,
             lsec_ref, yc_ref, yn_ref, qkp_ref, qkc_ref, sinks_ref, ga_ref, gc_ref, cw_ref,
             dproj_ref, dkv_ref, dga_ref, dgc_ref, dsinks_ref, dcw_ref, dk_carry, dv_carry):
        n = pl.program_id(0)
        first = n == 0
        live = n < nb
        has_next = n < nb - 1
        cos_p, sgn_p = ropep_ref[:, 0:V7X_LANES], ropep_ref[:, V7X_LANES:2 * V7X_LANES]
        cos_c, sgn_c = ropec_ref[:, 0:V7X_LANES], ropec_ref[:, V7X_LANES:2 * V7X_LANES]

        @pl.when(first)
        def _():
            dk_carry[...] = jnp.zeros(dk_carry.shape, _F32)
            dv_carry[...] = jnp.zeros(dv_carry.shape, _F32)

        def write_kv(dk2, dv2, cos, sgn):
            lo = _lane((WINDOW, 128)) < HEAD_DIM
            for g in range(KV_WIDTH // 128):
                dk = jnp.where(lo, _fold_halves(dk2[2 * g]), _fold_halves(dk2[2 * g + 1]))
                dv = jnp.where(lo, _fold_halves(dv2[2 * g]), _fold_halves(dv2[2 * g + 1]))
                dkv_ref[:, 128 * g:128 * g + 128] = _rope(dk, cos, sgn, -1.0).astype(dkv_ref.dtype)
                dkv_ref[:, KV_WIDTH + 128 * g:KV_WIDTH + 128 * g + 128] = dv.astype(dkv_ref.dtype)

        @pl.when(jnp.logical_not(live))
        def _():
            write_kv([dk_carry[h] for h in range(N_KV_HEADS)], [dv_carry[h] for h in range(N_KV_HEADS)], cos_c, sgn_c)

        @pl.when(live)
        def _():
            block_step(pp_ref, pc_ref, pn_ref, dmc_ref, dmn_ref, ac_ref, lsec_ref, yc_ref, yn_ref, qkp_ref, qkc_ref,
                       sinks_ref, ga_ref, gc_ref, cw_ref, dproj_ref, dga_ref, dgc_ref, dsinks_ref, dcw_ref, dk_carry,
                       dv_carry, first, has_next, cos_p, sgn_p, cos_c, sgn_c, write_kv)

    def block_step(pp_ref, pc_ref, pn_ref, dmc_ref, dmn_ref, ac_ref, lsec_ref, yc_ref, yn_ref, qkp_ref, qkc_ref,
                   sinks_ref, ga_ref, gc_ref, cw_ref, dproj_ref, dga_ref, dgc_ref, dsinks_ref, dcw_ref, dk_carry,
                   dv_carry, first, has_next, cos_p, sgn_p, cos_c, sgn_c, write_kv):
        da_c, dga = _rms_bwd(dmc_ref[:, 0:aw], ac_ref[...], ga_ref[...])
        _accumulate(dga_ref, dga, first)
        kk = jnp.concatenate([qkp_ref[:, aw:aw + KV_WIDTH], qkc_ref[:, aw:aw + KV_WIDTH]], axis=0).astype(_F32)
        vv = jnp.concatenate([pp_ref[:, dm.o_v:dm.o_v + KV_WIDTH], pc_ref[:, dm.o_v:dm.o_v + KV_WIDTH]], axis=0)
        group, pairs = dm.group, dm.group // 2
        valid_c = _band_mask(group, 2 * WINDOW, first)
        dk_prev, dv_prev = [], []

        def stacked(q_ref, da, o_ref, lse_ref_, h):
            cols = [slice(128 * j, 128 * j + 128) for j in range(pairs * h, pairs * (h + 1))]
            q4 = _stack_heads([q_ref[:, c] for c in cols])
            do4 = _stack_heads([da[:, c] for c in cols])
            lo = _lane((WINDOW, 128)) < HEAD_DIM
            deltas = []
            for c in cols:
                prod = o_ref[:, c] * da[:, c]
                deltas += [jnp.sum(jnp.where(lo, prod, 0.0), axis=1, keepdims=True),
                           jnp.sum(jnp.where(lo, 0.0, prod), axis=1, keepdims=True)]
            lse4 = jnp.concatenate([lse_ref_[:, group * h + r:group * h + r + 1] for r in range(group)], axis=0)
            return q4, do4, lse4, deltas

        def scores_bwd(q4, do4, lse4, delta4, keys, vals, valid):
            sc = _dot(q4, keys, "nt") * ATTN_SCALE
            p = jnp.exp(jnp.where(valid, sc - lse4, MASKED))
            return p.astype(_CDT), (p * (_dot(do4, vals, "nt") - delta4) * ATTN_SCALE).astype(_CDT)

        for h in range(N_KV_HEADS):
            k2, v2 = _dup_head(kk, h).astype(_CDT), _dup_head(vv, h).astype(_CDT)
            q4, do4, lse4, deltas = stacked(qkc_ref, da_c, ac_ref, lsec_ref, h)
            delta4 = jnp.concatenate(deltas, axis=0)
            p, ds = scores_bwd(q4, do4, lse4, delta4, k2, v2, valid_c)
            for i, dq in enumerate(_unstack_heads(_dot(ds, k2, "nn"), pairs)):
                j = pairs * h + i
                dproj_ref[:, 128 * j:128 * j + 128] = _rope(dq, cos_c, sgn_c, -1.0).astype(dproj_ref.dtype)
            dk = _dot(ds, q4, "tn")
            dv = _dot(p, do4, "tn")
            dk_prev.append(dk_carry[h] + dk[0:WINDOW])
            dv_prev.append(dv_carry[h] + dv[0:WINDOW])
            dk_carry[h] = dk[WINDOW:2 * WINDOW]
            dv_carry[h] = dv[WINDOW:2 * WINDOW]
            heads = slice(group * h, group * (h + 1))
            sink_row, delta_heads = jnp.zeros((1, group), _F32), jnp.zeros((WINDOW, group), _F32)
            for r in range(group):
                sink_row = jnp.where(_lane((1, group)) == r, sinks_ref[0, group * h + r], sink_row)
                delta_heads = jnp.where(_lane((WINDOW, group)) == r, deltas[r], delta_heads)
            loss_sink = jnp.exp(sink_row - lsec_ref[:, heads]) * delta_heads
            _accumulate(dsinks_ref.at[:, heads], -jnp.sum(loss_sink, axis=0, keepdims=True), first)
        write_kv(dk_prev, dv_prev, cos_p, sgn_p)

        bg = pc_ref[:, dm.o_bg:dm.o_bg + cw]
        yc = yc_ref[...]
        dconv, dgc = _rms_bwd(dmc_ref[:, aw:d], bg * yc, gc_ref[...])
        _accumulate(dgc_ref, dgc, first)
        dproj_ref[:, dm.o_bg:dm.o_bg + cw] = (dconv * yc).astype(dproj_ref.dtype)
        dy = dconv * bg
        bg_n = pn_ref[:, dm.o_bg:dm.o_bg + cw]
        dconv_n, _ = _rms_bwd(dmn_ref[:, aw:d], bg_n * yn_ref[...], gc_ref[...])
        halo = jnp.where(has_next, dconv_n * bg_n, 0.0)
        dy1 = _shift_up(dy, halo, 1)
        dy2 = _shift_up(dy, halo, 2)
        dz = cw_ref[2:3, :] * dy + cw_ref[1:2, :] * dy1 + cw_ref[0:1, :] * dy2
        cg = pc_ref[:, dm.o_cg:dm.o_cg + cw]
        u = pc_ref[:, dm.o_u:dm.o_u + cw]
        dproj_ref[:, dm.o_cg:dm.o_cg + cw] = (dz * u).astype(dproj_ref.dtype)
        dproj_ref[:, dm.o_u:dm.o_u + cw] = (dz * cg).astype(dproj_ref.dtype)
        z = cg * u
        dcw = jnp.concatenate(
            [jnp.sum(z * t, axis=0, keepdims=True) for t in (dy2, dy1, dy)]
            + [jnp.zeros((V7X_SUBLANES - 3, cw), _F32)], axis=0)
        _accumulate(dcw_ref, dcw, first)

    at = lambda n: jnp.minimum(n, nb - 1)
    prev = lambda n: (jnp.maximum(at(n) - 1, 0), 0)
    cur = lambda n: (at(n), 0)
    done = lambda n: (jnp.maximum(n - 1, 0), 0)
    nxt8 = lambda n: (jnp.minimum((at(n) + 1) * (WINDOW // V7X_SUBLANES), s // V7X_SUBLANES - 1), 0)
    fixed = lambda n: (0, 0)
    blocks = WINDOW * (2 * inw * 4 + d * 4 + aw * 4 + cw * 4 + inw * 2 + 2 * KV_WIDTH * 2)
    carry = [pltpu.VMEM((N_KV_HEADS, WINDOW, 128), _F32), pltpu.VMEM((N_KV_HEADS, WINDOW, 128), _F32)]
    n_in, n_out = 17, 6
    ca = _CommArgs(list(comm), n_in, n_out)
    return pl.pallas_call(
        _carrying(body, n_in, n_out, nb + 1, ca, n_scratch=len(carry)), name="mixer_bwd", grid=(nb + 1,),
        in_specs=[pl.BlockSpec((WINDOW, inw), prev), pl.BlockSpec((WINDOW, inw), cur), pl.BlockSpec((V7X_SUBLANES, inw), nxt8),
                  pl.BlockSpec((WINDOW, 2 * V7X_LANES), prev), pl.BlockSpec((WINDOW, 2 * V7X_LANES), cur),
                  pl.BlockSpec((WINDOW, d), cur), pl.BlockSpec((V7X_SUBLANES, d), nxt8),
                  pl.BlockSpec((WINDOW, aw), cur), pl.BlockSpec((WINDOW, nq), cur),
                  pl.BlockSpec((WINDOW, cw), cur), pl.BlockSpec((V7X_SUBLANES, cw), nxt8),
                  pl.BlockSpec((WINDOW, aw + KV_WIDTH), prev), pl.BlockSpec((WINDOW, aw + KV_WIDTH), cur),
                  pl.BlockSpec(memory_space=pltpu.SMEM),
                  pl.BlockSpec((1, aw), fixed), pl.BlockSpec((1, cw), fixed), pl.BlockSpec((V7X_SUBLANES, cw), fixed)]
        + [_ANY] * len(ca.operands),
        out_specs=[pl.BlockSpec((WINDOW, inw), cur), pl.BlockSpec((WINDOW, 2 * KV_WIDTH), done),
                   pl.BlockSpec((1, aw), fixed), pl.BlockSpec((1, cw), fixed),
                   pl.BlockSpec((1, nq), fixed), pl.BlockSpec((V7X_SUBLANES, cw), fixed)] + [_ANY] * len(ca.out_shape),
        out_shape=[jax.ShapeDtypeStruct((s, inw), _CDT), jax.ShapeDtypeStruct((s, 2 * KV_WIDTH), _CDT),
                   jax.ShapeDtypeStruct((1, aw), _F32), jax.ShapeDtypeStruct((1, cw), _F32),
                   jax.ShapeDtypeStruct((1, nq), _F32), jax.ShapeDtypeStruct((V7X_SUBLANES, cw), _F32)] + ca.out_shape,
        scratch_shapes=carry + ca.sems, input_output_aliases=ca.aliases,
        compiler_params=pltpu.CompilerParams(dimension_semantics=("arbitrary",), vmem_limit_bytes=_vmem_limit(blocks)),
    )(proj, proj, proj, rope, rope, dmixed, dmixed, attn, lse, y, y, qk, qk, sinks, g_attn, g_conv, conv_w8, *ca.operands)


def _position():
    return lax.axis_index("x"), lax.axis_index("y"), lax.axis_index("c")


def _linear(px, py, pc):
    return 4 * px + 2 * py + pc


def _comm_kernel(name, comm):
    ca = _CommArgs(list(comm), 0, 0)
    n_cin, n_cout = len(ca.operands), len(ca.out_shape)

    def body(*refs):
        cin, cout, sems = refs[:n_cin], refs[n_cin:n_cin + n_cout], refs[n_cin + n_cout:]
        ca.start(cin, cout, sems)
        ca.middle(cin, cout, sems)
        ca.finish(cin, cout, sems)

    return pl.pallas_call(
        body, name=name, out_shape=ca.out_shape, in_specs=[_ANY] * n_cin, out_specs=[_ANY] * n_cout,
        scratch_shapes=ca.sems, input_output_aliases=ca.aliases,
    )(*ca.operands)


def _gather_op(units):
    n = len(units)
    inputs, outputs, aliases = [], [], {}
    for shard, _, _, _ in units:
        inputs.append(shard)
        outputs.append(jax.ShapeDtypeStruct((N_DEV * shard.shape[0], shard.shape[1]), shard.dtype))
    for u, (_, buf, _, _) in enumerate(units):
        if buf is not None:
            aliases[len(inputs)] = u
            inputs.append(buf)

    def plan(ins, outs, sems, north):
        send_sems, recv_sems, local_sems = sems
        x, y, c = _position()
        me, sibling = (x, y, c), (x, y, 1 - c)
        xn, yn, dg = (1 - x, y), (x, 1 - y), (1 - x, 1 - y)
        via, to, k_via, k_other = (yn, xn, 2, 1) if north else (xn, yn, 1, 2)

        def rows(u, px, py, pc):
            shard, _, r0, r1 = units[u]
            return outs[u].at[pl.ds(pl.multiple_of(_linear(px, py, pc) * shard.shape[0] + r0, 16), r1 - r0), :]

        def own(u):
            _, _, r0, r1 = units[u]
            return ins[u].at[pl.ds(r0, r1 - r0), :]

        def copy(u, k, block, to_, src=None):
            return pltpu.make_async_remote_copy(
                src_ref=rows(u, *block) if src is None else src, dst_ref=rows(u, *block),
                send_sem=send_sems.at[u, k], recv_sem=recv_sems.at[u, k], device_id=to_, device_id_type=_MESH)

        us = range(n)
        return dict(
            mine=[pltpu.make_async_copy(own(u), rows(u, *me), local_sems.at[u]) for u in us],
            first=[cp for u in us for cp in (copy(u, 0, me, sibling, src=own(u)), copy(u, 1, me, (*xn, c), src=own(u)),
                                             copy(u, 2, me, (*yn, c), src=own(u)))],
            relay=[copy(u, 3, (*via, c), (*to, c)) for u in us],
            arrived={1: [copy(u, 1, (*xn, c), me) for u in us], 2: [copy(u, 2, (*yn, c), me) for u in us],
                     3: [copy(u, 3, (*dg, c), me) for u in us]},
            passed={1: [copy(u, 4, (*xn, c), sibling) for u in us], 2: [copy(u, 5, (*yn, c), sibling) for u in us],
                    3: [copy(u, 6, (*dg, c), sibling) for u in us]},
            rest=[cp for u in us for cp in (copy(u, 0, sibling, me), copy(u, 4, (*xn, 1 - c), me),
                                            copy(u, 5, (*yn, 1 - c), me), copy(u, 6, (*dg, 1 - c), me))],
            k_via=k_via, k_other=k_other)

    def land(p, k):
        for arrived, onward in zip(p["arrived"][k], p["passed"][k]):
            arrived.wait_recv()
            onward.start()

    def by_core(fn):
        c = lax.axis_index("c")
        for north in (True, False):
            pl.when(c == (1 if north else 0))(functools.partial(fn, north))

    def start(ins, outs, sems):
        p = plan(ins, outs, sems, True)
        for cp in p["mine"] + p["first"]:
            cp.start()

    def middle(ins, outs, sems):
        def go(north):
            p = plan(ins, outs, sems, north)
            land(p, p["k_via"])
            for cp in p["relay"]:
                cp.start()
            land(p, p["k_other"])
        by_core(go)

    def finish(ins, outs, sems):
        def go(north):
            p = plan(ins, outs, sems, north)
            land(p, 3)
            for cp in p["rest"]:
                cp.wait_recv()
            for cp in p["first"] + p["relay"] + [cp for k in (1, 2, 3) for cp in p["passed"][k]]:
                cp.wait_send()
            for cp in p["mine"]:
                cp.wait()
        by_core(go)

    sems = [pltpu.SemaphoreType.DMA((n, 7)), pltpu.SemaphoreType.DMA((n, 7)), pltpu.SemaphoreType.DMA((n,))]
    return _Comm(inputs, outputs, aliases, sems, start, finish, middle)


def _peers(x, y, c):
    out = []
    for k in range(1, N_DEV):
        fx, fy, fc = (k >> 2) & 1, (k >> 1) & 1, k & 1
        out.append((1 - x if fx else x, 1 - y if fy else y, 1 - c if fc else c))
    return out


def _exchange_op(partials):
    n = len(partials)
    outputs = [jax.ShapeDtypeStruct((4, p.shape[0] // N_DEV, p.shape[1]), p.dtype) for p in partials]

    def plan(ins, outs, sems):
        send_sems, recv_sems = sems
        x, y, c = _position()
        out = []
        for a in range(n):
            r = outs[a].shape[1]
            for ch in range(4):
                out.append(pltpu.make_async_remote_copy(
                    src_ref=ins[a].at[pl.ds(pl.multiple_of((2 * ch + 1 - c) * r, 16), r), :], dst_ref=outs[a].at[ch],
                    send_sem=send_sems.at[a, ch], recv_sem=recv_sems.at[a, ch], device_id=(x, y, 1 - c),
                    device_id_type=_MESH))
        return out

    def start(ins, outs, sems):
        for cp in plan(ins, outs, sems):
            cp.start()

    def finish(ins, outs, sems):
        copies = plan(ins, outs, sems)
        for cp in copies:
            cp.wait_recv()
        for cp in copies:
            cp.wait_send()

    sems = [pltpu.SemaphoreType.DMA((n, 4)), pltpu.SemaphoreType.DMA((n, 4))]
    return _Comm(list(partials), outputs, {}, sems, start, finish)


def _chip_send_op(units):
    n = len(units)
    inputs, outputs, aliases = [], [], {}
    for q, _, _, _ in units:
        inputs.append(q)
        outputs.append(jax.ShapeDtypeStruct(q.shape, q.dtype))
    for u, (_, buf, _, _) in enumerate(units):
        if buf is not None:
            aliases[len(inputs)] = u
            inputs.append(buf)

    def plan(ins, outs, sems):
        send_sems, recv_sems, local_sems = sems
        x, y, c = _position()
        my_chip = 2 * x + y
        chips = [(1 - x, y), (x, 1 - y), (1 - x, 1 - y)]
        mine, sends, arrivals = [], [], []
        for u, (_, _, r0, r1) in enumerate(units):
            span = pl.ds(r0, r1 - r0)
            mine.append(pltpu.make_async_copy(ins[u].at[my_chip, span, :], outs[u].at[my_chip, span, :], local_sems.at[u]))
            for k, (px, py) in enumerate(chips):
                sends.append(pltpu.make_async_remote_copy(
                    src_ref=ins[u].at[2 * px + py, span, :], dst_ref=outs[u].at[my_chip, span, :],
                    send_sem=send_sems.at[u, k], recv_sem=recv_sems.at[u, k], device_id=(px, py, c), device_id_type=_MESH))
                arrivals.append(pltpu.make_async_remote_copy(
                    src_ref=ins[u].at[my_chip, span, :], dst_ref=outs[u].at[2 * px + py, span, :],
                    send_sem=send_sems.at[u, k], recv_sem=recv_sems.at[u, k], device_id=(px, py, c), device_id_type=_MESH))
        return mine, sends, arrivals

    def start(ins, outs, sems):
        mine, sends, _ = plan(ins, outs, sems)
        for cp in mine + sends:
            cp.start()

    def finish(ins, outs, sems):
        mine, sends, arrivals = plan(ins, outs, sems)
        for cp in arrivals:
            cp.wait_recv()
        for cp in sends:
            cp.wait_send()
        for cp in mine:
            cp.wait()

    sems = [pltpu.SemaphoreType.DMA((n, 3)), pltpu.SemaphoreType.DMA((n, 3)), pltpu.SemaphoreType.DMA((n,))]
    return _Comm(inputs, outputs, aliases, sems, start, finish)


def _pair_sum(name, partial, received):
    _, rows, cols = received.shape
    tr = _pick(rows, (352, 288, 256, 128, 64, 32, 16))
    p4 = partial.reshape(4, 2, rows, cols)
    kind = jnp.reshape(lax.axis_index("c"), (1,)).astype(jnp.int32)

    def body(kind_ref, p_ref, r_ref, o_ref):
        o_ref[0] = (p_ref[0, 0].astype(_F32) + r_ref[0].astype(_F32)).astype(o_ref.dtype)

    return pl.pallas_call(
        body, name=name,
        grid_spec=pltpu.PrefetchScalarGridSpec(
            num_scalar_prefetch=1, grid=(4, rows // tr),
            in_specs=[pl.BlockSpec((1, 1, tr, cols), lambda ch, i, kind_ref: (ch, kind_ref[0], i, 0)),
                      pl.BlockSpec((1, tr, cols), lambda ch, i, kind_ref: (ch, i, 0))],
            out_specs=pl.BlockSpec((1, tr, cols), lambda ch, i, kind_ref: (ch, i, 0))),
        out_shape=jax.ShapeDtypeStruct(received.shape, received.dtype),
        compiler_params=pltpu.CompilerParams(dimension_semantics=("arbitrary", "arbitrary")),
    )(kind, p4, received)


def _all_reduce_small(name, v):
    rows = v.shape[0]

    def body(v_ref, out_ref, land_ref, send_sems, recv_sems):
        x, y, c = _position()
        me = _linear(x, y, c)
        peers = _peers(x, y, c)
        land_ref[me] = v_ref[...]
        sends = [pltpu.make_async_remote_copy(
            src_ref=v_ref, dst_ref=land_ref.at[me], send_sem=send_sems.at[k], recv_sem=recv_sems.at[k],
            device_id=peer, device_id_type=_MESH) for k, peer in enumerate(peers)]
        for cp in sends:
            cp.start()
        for k, peer in enumerate(peers):
            pltpu.make_async_remote_copy(
                src_ref=v_ref, dst_ref=land_ref.at[_linear(*peer)], send_sem=send_sems.at[k], recv_sem=recv_sems.at[k],
                device_id=peer, device_id_type=_MESH).wait_recv()
        for cp in sends:
            cp.wait_send()
        total = land_ref[0]
        for s in range(1, N_DEV):
            total = total + land_ref[s]
        out_ref[...] = total

    return pl.pallas_call(
        body, name=name, out_shape=jax.ShapeDtypeStruct(v.shape, _F32),
        in_specs=[pl.BlockSpec(memory_space=pltpu.VMEM)], out_specs=pl.BlockSpec(memory_space=pltpu.VMEM),
        scratch_shapes=[pltpu.VMEM((N_DEV, rows, V7X_LANES), _F32), pltpu.SemaphoreType.DMA((7,)), pltpu.SemaphoreType.DMA((7,))],
    )(v)


def _adamw(name, w, slots, m, v):
    rows, cols = w.shape
    n_slots = slots.shape[0]
    tr = _pick(rows, (176, 144, 128, 64, 32, 16, 8))

    def body(w_ref, s_ref, m_ref, v_ref, g_ref, d_ref, nm_ref, nv_ref):
        g = s_ref[0].astype(_F32)
        for k in range(1, n_slots):
            g = g + s_ref[k].astype(_F32)
        nm = ADAM_B1 * m_ref[...] + (1.0 - ADAM_B1) * g
        nv = ADAM_B2 * v_ref[...] + (1.0 - ADAM_B2) * (g * g)
        m_hat = nm / (1.0 - ADAM_B1 ** ADAM_STEP)
        v_hat = nv / (1.0 - ADAM_B2 ** ADAM_STEP)
        g_ref[...] = g
        d_ref[...] = -ADAM_LR * (m_hat / (jnp.sqrt(v_hat) + ADAM_EPS) + ADAM_WD * w_ref[...])
        nm_ref[...] = nm
        nv_ref[...] = nv

    spec = pl.BlockSpec((tr, cols), lambda i: (i, 0))
    blocks = 7 * tr * cols * 4 + _nbytes((n_slots, tr, cols), slots.dtype)
    return pl.pallas_call(
        body, name=name, grid=(rows // tr,),
        in_specs=[spec, pl.BlockSpec((n_slots, tr, cols), lambda i: (0, i, 0)), spec, spec], out_specs=[spec] * 4,
        out_shape=[jax.ShapeDtypeStruct((rows, cols), _F32)] * 4,
        compiler_params=pltpu.CompilerParams(dimension_semantics=("arbitrary",), vmem_limit_bytes=_vmem_limit(blocks)),
    )(w, slots, m, v)


def _pad_rows(a, rows):
    return jnp.pad(a, ((0, rows - a.shape[0]), (0, 0)))


def _pack(parts):
    rows, spans, at = [], [], 0
    for p in parts:
        p = p.reshape(-1)
        r = -(-p.shape[0] // V7X_LANES)
        rows.append(jnp.pad(p, (0, r * V7X_LANES - p.shape[0])).reshape(r, V7X_LANES))
        spans.append((at, r, p.shape[0]))
        at += r
    packed = jnp.concatenate(rows, axis=0)
    return _pad_rows(packed, -(-at // V7X_SUBLANES) * V7X_SUBLANES), spans


def _unpack(packed, spans, shapes):
    return [packed[at:at + r].reshape(-1)[:size].reshape(shape) for (at, r, size), shape in zip(spans, shapes)]


def kernel(x, positions, w_in, conv_w, sinks, g_attn, g_conv, w_out, ln1_g, ln1_b, w_gate, w_up, w_down, ln2_g, ln2_b, loss_target, m_w_in, m_conv_w, m_sinks, m_g_attn, m_g_conv, m_w_out, m_ln1_g, m_ln1_b, m_w_gate, m_w_up, m_w_down, m_ln2_g, m_ln2_b, v_w_in, v_conv_w, v_sinks, v_g_attn, v_g_conv, v_w_out, v_ln1_g, v_ln1_b, v_w_gate, v_w_up, v_w_down, v_ln2_g, v_ln2_b):
    _, s, d = x.shape
    d_ff = N_DEV * w_gate.shape[2]
    dm = _Dims(s, d, d_ff)
    aw, cw, nq, inw = dm.aw, dm.cw, dm.nq, dm.inw
    x2 = x[0]
    pos = positions[0].reshape(s, 1)
    inv_freq = ROPE_THETA ** (-jnp.arange(0, ROT_DIM, 2, dtype=_F32) / ROT_DIM)
    invf = jnp.tile(inv_freq, V7X_LANES // (ROT_DIM // 2)).reshape(1, V7X_LANES)

    conv_cols = conv_w.shape[2]
    sh_in, sh_out = w_in[0].T.astype(_CDT), w_out[0].astype(_CDT)
    sh_gate, sh_up, sh_down = w_gate[0].T.astype(_CDT), w_up[0].T.astype(_CDT), w_down[0].astype(_CDT)
    r_in, r_out, r_ff = sh_in.shape[0], sh_out.shape[0], sh_gate.shape[0]
    q_ff = r_ff // 4
    assert q_ff % 16 == 0
    def prepare_body(x_ref, pos_ref, invf_ref, xc_ref, rope_ref):
        xc_ref[...] = x_ref[...].astype(_CDT)
        cos, sgn = _rope_tables(pos_ref[...], invf_ref[...])
        rope_ref[:, 0:V7X_LANES] = cos
        rope_ref[:, V7X_LANES:2 * V7X_LANES] = sgn

    x_c, rope, w_in_t, conv_all = _row_kernel(
        "prepare_gather_w_in", prepare_body, [x2, pos], [invf], [((s, d), _CDT), ((s, 2 * V7X_LANES), _F32)], [],
        comm=[_gather_op([(sh_in, None, 0, r_in), (_pad_rows(conv_w[0], 16), None, 0, 16)])])
    conv_full = conv_all.reshape(N_DEV, 16, conv_cols)[:, :3, :].transpose(1, 0, 2).reshape(3, cw)
    conv_w8 = _pad_rows(conv_full, V7X_SUBLANES)

    tm = _pick(s, (1024, 512, 256, 128))
    tm2 = _pick(s, (2048, 1024, 512, 256, 128))
    tr = _pick(s, (512, 256, 128))
    tn_in = _pick(inw, (512, 256, 128))
    tn_ff = _pick(d_ff, (512, 256, 128))

    proj, w_out_f, w_gate_t = _matmul(
        "proj", [[(x_c, w_in_t, "nt")]], s, inw, d, tm2, tn_in, d, [],
        [((s, inw), _F32, (tm2, tn_in), _tile_ij)], _store_epilogue,
        comm=[_gather_op([(sh_out, None, 0, r_out), (sh_gate, None, 0, 2 * q_ff)])])
    mixed, attn, lse, y_conv, qk_rot, w_gate_t, w_up_t = _mixer_fwd(
        dm, proj, rope, sinks, g_attn, g_conv, conv_w8,
        comm=[_gather_op([(sh_gate, w_gate_t, 2 * q_ff, r_ff), (sh_up, None, 0, 2 * q_ff)])])

    def residual_epilogue(accs, ex, out, first):
        out[0][...] = DEEPNORM_ALPHA * ex[0][...] + accs[0]

    tn_d = _pick(d, (512,))
    r1, w_up_t = _matmul(
        "out_proj", [[(mixed, w_out_f, "nn")]], s, d, d, tm, tn_d, d, [(x2, (tm, tn_d), _tile_ij)],
        [((s, d), _F32, (tm, tn_d), _tile_ij)], residual_epilogue,
        comm=[_gather_op([(sh_up, w_up_t, 2 * q_ff, 3 * q_ff)])])
    h1, h1_c, w_up_t = _ln1_fwd_rows(r1, ln1_g, ln1_b, comm=[_gather_op([(sh_up, w_up_t, 3 * q_ff, r_ff)])])

    def swiglu_epilogue(accs, ex, out, first):
        gate_v, up_v = accs
        out[0][...] = gate_v
        out[1][...] = up_v
        out[2][...] = (gate_v * jax.nn.sigmoid(gate_v) * up_v).astype(_CDT)

    gate, up, act, w_down_f = _matmul(
        "gate_up", [[(h1_c, w_gate_t, "nt")], [(h1_c, w_up_t, "nt")]], s, d_ff, d, tm, tn_ff, d, [],
        [((s, d_ff), _F32, (tm, tn_ff), _tile_ij), ((s, d_ff), _F32, (tm, tn_ff), _tile_ij),
         ((s, d_ff), _CDT, (tm, tn_ff), _tile_ij)], swiglu_epilogue,
        comm=[_gather_op([(sh_down, None, 0, r_ff)])])

    (r2,) = _matmul("down", [[(act, w_down_f, "nn")]], s, d, d_ff, tm, tn_d, d_ff, [(h1, (tm, tn_d), _tile_ij)],
                    [((s, d), _F32, (tm, tn_d), _tile_ij)], residual_epilogue)
    dr2, dr2_c, loss_acc, d_ln2_g, d_ln2_b = _ln2_loss_bwd(r2, loss_target[0], ln2_g, ln2_b)

    def swiglu_bwd_epilogue(accs, ex, out, first):
        gate_v, up_v = ex[0][...], ex[1][...]
        sig = jax.nn.sigmoid(gate_v)
        out[0][...] = (accs[0] * up_v * (sig * (1.0 + gate_v * (1.0 - sig)))).astype(_CDT)
        out[1][...] = (accs[0] * (gate_v * sig)).astype(_CDT)

    dgate, dup = _matmul(
        "dact", [[(dr2_c, w_down_f, "nt")]], s, d_ff, d, tm2, tn_ff, d,
        [(gate, (tm2, tn_ff), _tile_ij), (up, (tm2, tn_ff), _tile_ij)],
        [((s, d_ff), _CDT, (tm2, tn_ff), _tile_ij), ((s, d_ff), _CDT, (tm2, tn_ff), _tile_ij)], swiglu_bwd_epilogue)
    def weight_grad(name, a, b, comm=()):
        rows = a.shape[1]
        tw, tn_w = _pick(rows, (512, 256, 128)), d
        return _matmul(name, [[(a, b, "tn")]], rows, d, s, tw, tn_w, s, [],
                       [((rows, d), _CDT, (tw, tn_w), _tile_ij)], _store_epilogue, comm=comm, j_outer=True)

    (dw_down,) = weight_grad("dw_down", act, dr2_c)
    dw_gate_t, x_down = weight_grad("dw_gate", dgate, h1_c, comm=[_exchange_op([dw_down])])
    q_down = _pair_sum("chip_sum_w_down", dw_down, x_down)
    dw_up_t, l_down, x_gate = weight_grad(
        "dw_up", dup, h1_c, comm=[_chip_send_op([(q_down, None, 0, 2 * q_ff)]), _exchange_op([dw_gate_t])])
    q_gate = _pair_sum("chip_sum_w_gate", dw_gate_t, x_gate)

    tn_h = _pick(d, (512,))
    dh1, l_down, l_gate, x_up = _matmul(
        "dh1", [[(dgate, w_gate_t, "nn"), (dup, w_up_t, "nn")]], s, d, d_ff, tr, tn_h, d_ff,
        [(dr2, (tr, tn_h), _tile_ij)], [((s, d), _F32, (tr, tn_h), _tile_ij)], residual_epilogue,
        comm=[_chip_send_op([(q_down, l_down, 2 * q_ff, r_ff), (q_gate, None, 0, r_ff)]), _exchange_op([dw_up_t])])
    q_up = _pair_sum("chip_sum_w_up", dw_up_t, x_up)
    dr1, dr1_c, d_ln1_g, d_ln1_b = _ln1_bwd_rows(dh1, r1, ln1_g)
    (dmixed,) = _matmul("dmixed", [[(dr1_c, w_out_f, "nt")]], s, d, d, tm2, tn_d, d, [],
                        [((s, d), _F32, (tm2, tn_d), _tile_ij)], _store_epilogue)
    (dw_out,) = weight_grad("dw_out", mixed, dr1_c)
    dproj, dkv, d_g_attn, d_g_conv, d_sinks, d_conv8, l_up, x_out = _mixer_bwd(
        dm, proj, rope, sinks, g_attn, g_conv, conv_w8, dmixed, attn, lse, y_conv, qk_rot,
        comm=[_chip_send_op([(q_up, None, 0, r_ff)]), _exchange_op([dw_out])])
    dproj = _patch_columns("dproj_kv", dproj, dkv, dm.o_k)
    q_out = _pair_sum("chip_sum_w_out", dw_out, x_out)
    dw_in_t, l_out = weight_grad("dw_in", dproj, x_c, comm=[_chip_send_op([(q_out, None, 0, r_out)])])
    (x_in,) = _comm_kernel("exchange_w_in", [_exchange_op([dw_in_t])])
    q_in = _pair_sum("chip_sum_w_in", dw_in_t, x_in)

    grad_x, l_in = _matmul("dx", [[(dproj, w_in_t, "nn")]], s, d, inw, tm, tn_d, inw,
                           [(dr1, (tm, tn_d), _tile_ij)], [((s, d), _F32, (tm, tn_d), _tile_ij)], residual_epilogue,
                           comm=[_chip_send_op([(q_in, None, 0, r_in)])])

    small_parts = [d_conv8[:3], d_sinks, d_g_attn, d_g_conv, d_ln1_g, d_ln1_b, d_ln2_g, d_ln2_b, loss_acc[0:1, 0:1]]
    packed, spans = _pack(small_parts)
    reduced = _unpack(_all_reduce_small("reduce_small", packed), spans, [p.shape for p in small_parts])
    g_conv_full, g_sinks, g_g_attn, g_g_conv, g_ln1_g, g_ln1_b, g_ln2_g, g_ln2_b, loss_sum = reduced
    me = _linear(*_position())
    g_conv_w = lax.dynamic_slice(g_conv_full, (0, me * conv_cols), (3, conv_cols))
    loss = loss_sum[0, 0]

    big = {"w_in": (w_in[0].T, l_in, m_w_in[0].T, v_w_in[0].T), "w_out": (w_out[0], l_out, m_w_out[0], v_w_out[0]),
           "w_gate": (w_gate[0].T, l_gate, m_w_gate[0].T, v_w_gate[0].T),
           "w_up": (w_up[0].T, l_up, m_w_up[0].T, v_w_up[0].T), "w_down": (w_down[0], l_down, m_w_down[0], v_w_down[0])}
    res = {nm: tuple(_adamw(f"adamw_{nm}", w, slots, m, v)) for nm, (w, slots, m, v) in big.items()}
    for nm in ("w_in", "w_gate", "w_up"):
        res[nm] = tuple(a.T for a in res[nm])
    small_names = ["conv_w", "sinks", "g_attn", "g_conv", "ln1_g", "ln1_b", "ln2_g", "ln2_b"]
    small_w = [conv_w, sinks, g_attn, g_conv, ln1_g, ln1_b, ln2_g, ln2_b]
    small_g = [g_conv_w[None], g_sinks, g_g_attn, g_g_conv, g_ln1_g, g_ln1_b, g_ln2_g, g_ln2_b]
    small_m = [m_conv_w, m_sinks, m_g_attn, m_g_conv, m_ln1_g, m_ln1_b, m_ln2_g, m_ln2_b]
    small_v = [v_conv_w, v_sinks, v_g_attn, v_g_conv, v_ln1_g, v_ln1_b, v_ln2_g, v_ln2_b]
    pw, sp = _pack(small_w)
    pg, _ = _pack(small_g)
    pm, _ = _pack(small_m)
    pv, _ = _pack(small_v)
    shapes = [w.shape for w in small_w]
    _, sd, sm, sv = [_unpack(p, sp, shapes) for p in _adamw("adamw_small", pw, pg[None], pm, pv)]
    for i, nm in enumerate(small_names):
        res[nm] = (small_g[i].reshape(shapes[i]), sd[i], sm[i], sv[i])

    order = ["w_in", "conv_w", "sinks", "g_attn", "g_conv", "w_out", "ln1_g", "ln1_b", "w_gate", "w_up", "w_down", "ln2_g", "ln2_b"]

    def lead(a, nm):
        return a[None] if nm in big else a

    return (loss, grad_x[None],
            *[lead(res[nm][0], nm) for nm in order], *[lead(res[nm][1], nm) for nm in order],
            *[lead(res[nm][2], nm) for nm in order], *[lead(res[nm][3], nm) for nm in order])
```

```python
import functools

import jax
import jax.numpy as jnp
from jax import lax
from jax.experimental import pallas as pl
from jax.experimental.pallas import tpu as pltpu

_F32 = jnp.float32
_CDT = jnp.bfloat16

HEAD_DIM = 64
WINDOW = 128
N_KV_HEADS = 4
KV_WIDTH = N_KV_HEADS * HEAD_DIM
ROT_DIM = HEAD_DIM // 4
ROPE_THETA = 500000.0
ATTN_SCALE = HEAD_DIM ** -0.5
DEPTH = 1
DEEPNORM_ALPHA = (2 * DEPTH) ** 0.25
LN_EPS = 1e-5
RMS_EPS = 1e-6
ADAM_LR = 0.001
ADAM_B1 = 0.9
ADAM_B2 = 0.999
ADAM_EPS = 1e-08
ADAM_WD = 0.01
ADAM_STEP = 10
N_DEV = 8
MASKED = -1e30

MIB = 1024 * 1024
V7X_VMEM_BYTES = 64 * MIB
V7X_LANES = 128
V7X_SUBLANES = 8
BODY_TEMPORARIES_BYTES = 16 * MIB
VMEM_LIMIT_FLOOR_BYTES = 32 * MIB
VMEM_LIMIT_CEILING_BYTES = V7X_VMEM_BYTES - 8 * MIB
_MESH = pl.DeviceIdType.MESH
_ANY = pl.BlockSpec(memory_space=pl.ANY)


def _vmem_limit(block_bytes, scratch_bytes=0):
    want = 2 * block_bytes + scratch_bytes + BODY_TEMPORARIES_BYTES
    return int(min(max(want, VMEM_LIMIT_FLOOR_BYTES), VMEM_LIMIT_CEILING_BYTES))


def _nbytes(shape, dtype):
    n = 1
    for s in shape:
        n *= s
    return n * jnp.dtype(dtype).itemsize


def _pick(n, candidates):
    for c in candidates:
        if n % c == 0:
            return c
    raise ValueError(f"no tile of {candidates} divides {n}")


_DOT_DIMS = {"nn": ((1,), (0,)), "nt": ((1,), (1,)), "tn": ((0,), (0,))}


def _dot(a, b, mode):
    return lax.dot_general(a.astype(_CDT), b.astype(_CDT), (_DOT_DIMS[mode], ((), ())),
                           preferred_element_type=_F32)


def _accumulate(ref, val, first):
    @pl.when(first)
    def _():
        ref[...] = val

    @pl.when(jnp.logical_not(first))
    def _():
        ref[...] += val


class _Comm:
    def __init__(self, inputs, outputs, aliases, sems, start, finish, middle=None):
        self.inputs, self.outputs, self.aliases, self.sems = inputs, outputs, aliases, sems
        self.start, self.finish, self.middle = start, finish, middle


def _middle_step(n_steps):
    return (2 * n_steps) // 3


class _CommArgs:
    def __init__(self, comms, n_in_before, n_out_before):
        self.comms, self.operands, self.out_shape, self.aliases, self.sems, self.at = comms, [], [], {}, [], []
        for cm in comms:
            self.at.append((len(self.operands), len(self.out_shape), len(self.sems)))
            for i_in, i_out in cm.aliases.items():
                self.aliases[n_in_before + len(self.operands) + i_in] = n_out_before + len(self.out_shape) + i_out
            self.operands += cm.inputs
            self.out_shape += cm.outputs
            self.sems += cm.sems

    def _each(self, in_refs, out_refs, sem_refs):
        for cm, (i0, o0, s0) in zip(self.comms, self.at):
            yield cm, (in_refs[i0:i0 + len(cm.inputs)], out_refs[o0:o0 + len(cm.outputs)], sem_refs[s0:s0 + len(cm.sems)])

    def start(self, in_refs, out_refs, sem_refs):
        for cm, refs in self._each(in_refs, out_refs, sem_refs):
            cm.start(*refs)

    def finish(self, in_refs, out_refs, sem_refs):
        for cm, refs in self._each(in_refs, out_refs, sem_refs):
            cm.finish(*refs)

    @property
    def has_middle(self):
        return any(cm.middle is not None for cm in self.comms)

    def middle(self, in_refs, out_refs, sem_refs):
        for cm, refs in self._each(in_refs, out_refs, sem_refs):
            if cm.middle is not None:
                cm.middle(*refs)


def _matmul(name, groups, m, n, k, tm, tn, tk, extras, outs, epilogue, comm=(), j_outer=False):
    assert m % tm == 0 and n % tn == 0 and k % tk == 0, (name, m, n, k, tm, tn, tk)
    nk = k // tk
    terms = [t for g in groups for t in g]
    operands, in_specs, block_bytes = [], [], 0

    def spec(blk, imap):
        return pl.BlockSpec(blk, (lambda g0, g1, kk: imap(g1, g0, kk)) if j_outer else imap)

    for a, b, mode in terms:
        assert a.shape == ((k, m) if mode == "tn" else (m, k)), (name, a.shape, mode)
        assert b.shape == ((n, k) if mode == "nt" else (k, n)), (name, b.shape, mode)
        if mode == "tn":
            a_blk, a_map = (tk, tm), (lambda i, j, kk: (kk, i))
        else:
            a_blk, a_map = (tm, tk), (lambda i, j, kk: (i, kk))
        if mode == "nt":
            b_blk, b_map = (tn, tk), (lambda i, j, kk: (j, kk))
        else:
            b_blk, b_map = (tk, tn), (lambda i, j, kk: (kk, j))
        operands += [a, b]
        in_specs += [spec(a_blk, a_map), spec(b_blk, b_map)]
        block_bytes += _nbytes(a_blk, a.dtype) + _nbytes(b_blk, b.dtype)
    for arr, blk, imap in extras:
        operands.append(arr)
        in_specs.append(spec(blk, lambda i, j, kk, imap=imap: imap(i, j)))
        block_bytes += _nbytes(blk, arr.dtype)
    out_shape, out_specs = [], []
    for shape, dtype, blk, imap in outs:
        out_shape.append(jax.ShapeDtypeStruct(shape, dtype))
        out_specs.append(spec(blk, lambda i, j, kk, imap=imap: imap(i, j)))
        block_bytes += _nbytes(blk, dtype)
    n_terms, n_extra, n_out, n_groups = len(terms), len(extras), len(outs), len(groups)
    scratch = [pltpu.VMEM((tm, tn), _F32) for _ in range(n_groups)] if nk > 1 else []
    ca = _CommArgs(list(comm), len(operands), n_out)
    n_cin, n_cout, n_acc = len(ca.operands), len(ca.out_shape), len(scratch)
    tiles = (m // tm, n // tn)
    grid = (tiles[1], tiles[0], nk) if j_outer else (tiles[0], tiles[1], nk)

    def body(*refs):
        refs = list(refs)
        term_refs = [refs.pop(0) for _ in range(2 * n_terms)]
        extra_refs = [refs.pop(0) for _ in range(n_extra)]
        cin_refs = [refs.pop(0) for _ in range(n_cin)]
        out_refs = [refs.pop(0) for _ in range(n_out)]
        cout_refs = [refs.pop(0) for _ in range(n_cout)]
        acc_refs = [refs.pop(0) for _ in range(n_acc)]
        sem_refs = refs
        g0, g1, kk = pl.program_id(0), pl.program_id(1), pl.program_id(2)
        first = jnp.logical_and(g0 == 0, g1 == 0)
        if comm:
            @pl.when(jnp.logical_and(first, kk == 0))
            def _():
                ca.start(cin_refs, cout_refs, sem_refs)
        if ca.has_middle:
            step = (g0 * grid[1] + g1) * nk + kk

            @pl.when(step == _middle_step(grid[0] * grid[1] * nk))
            def _():
                ca.middle(cin_refs, cout_refs, sem_refs)
        partial, t = [], 0
        for g in groups:
            s = None
            for _, _, mode in g:
                d = _dot(term_refs[2 * t][...], term_refs[2 * t + 1][...], mode)
                s = d if s is None else s + d
                t += 1
            partial.append(s)
        if nk == 1:
            epilogue(partial, extra_refs, out_refs, first)
        else:
            for acc, p in zip(acc_refs, partial):
                _accumulate(acc, p, kk == 0)

            @pl.when(kk == nk - 1)
            def _():
                epilogue([acc[...] for acc in acc_refs], extra_refs, out_refs, first)
        if comm:
            @pl.when(jnp.logical_and(jnp.logical_and(g0 == grid[0] - 1, g1 == grid[1] - 1), kk == nk - 1))
            def _():
                ca.finish(cin_refs, cout_refs, sem_refs)

    res = pl.pallas_call(
        body, name=name, grid=grid,
        in_specs=in_specs + [_ANY] * n_cin, out_specs=out_specs + [_ANY] * n_cout,
        out_shape=out_shape + ca.out_shape, scratch_shapes=scratch + ca.sems, input_output_aliases=ca.aliases,
        compiler_params=pltpu.CompilerParams(
            dimension_semantics=("arbitrary", "arbitrary", "arbitrary"),
            vmem_limit_bytes=_vmem_limit(block_bytes, n_groups * tm * tn * 4 if nk > 1 else 0)),
    )(*operands, *ca.operands)
    return list(res[:n_out]) + list(res[n_out:])


def _store_epilogue(accs, extra_refs, out_refs, first):
    for acc, ref in zip(accs, out_refs):
        ref[...] = acc.astype(ref.dtype)


def _tile_ij(i, j):
    return (i, j)


def _row_i(i, j):
    return (i, 0)


def _whole(i, j):
    return (0, 0)


def _mean(v):
    return jnp.mean(v, axis=-1, keepdims=True)


def _ln_fwd(r, g, b):
    xc = r - _mean(r)
    rstd = lax.rsqrt(_mean(xc * xc) + LN_EPS)
    xhat = xc * rstd
    return xhat * g + b, xhat, rstd


def _ln_bwd(dy, xhat, rstd, g):
    dxh = dy * g
    dr = rstd * (dxh - _mean(dxh) - xhat * _mean(dxh * xhat))
    return dr, jnp.sum(dy * xhat, axis=0, keepdims=True), jnp.sum(dy, axis=0, keepdims=True)


def _rms_fwd(a, g):
    rstd = lax.rsqrt(_mean(a * a) + RMS_EPS)
    return a * rstd * g


def _rms_bwd(dm, a, g):
    rstd = lax.rsqrt(_mean(a * a) + RMS_EPS)
    nhat = a * rstd
    dn = dm * g
    da = rstd * (dn - nhat * _mean(dn * nhat))
    return da, jnp.sum(dm * nhat, axis=0, keepdims=True)


def _lane(shape):
    return lax.broadcasted_iota(jnp.int32, shape, 1)


def _row(shape):
    return lax.broadcasted_iota(jnp.int32, shape, 0)


def _rope_tables(pos, invf):
    ang = pos.astype(_F32) * invf
    lane = _lane(ang.shape)
    in_rot = (lane % HEAD_DIM) < ROT_DIM
    first = (lane % ROT_DIM) < ROT_DIM // 2
    cos = jnp.where(in_rot, jnp.cos(ang), 1.0)
    sin = jnp.sin(ang)
    sgn = jnp.where(in_rot, jnp.where(first, -sin, sin), 0.0)
    return cos, sgn


def _rope(t, cos, sgn, sign):
    half = ROT_DIM // 2
    first = (_lane(t.shape) % ROT_DIM) < half
    partner = jnp.where(first, pltpu.roll(t, V7X_LANES - half, 1), pltpu.roll(t, half, 1))
    return t * cos + partner * (sgn * sign)


def _dup_head(t, h):
    g = t[:, 128 * (h // 2):128 * (h // 2) + 128]
    r = pltpu.roll(g, HEAD_DIM, 1)
    lo = _lane(g.shape) < HEAD_DIM
    return jnp.where(lo, g, r) if h % 2 == 0 else jnp.where(lo, r, g)


def _fold_halves(t):
    return t + pltpu.roll(t, HEAD_DIM, 1)


def _halves(t):
    lo = _lane(t.shape) < HEAD_DIM
    zero = jnp.zeros_like(t)
    return jnp.where(lo, t, zero), jnp.where(lo, zero, t)


def _band_mask(n_heads, n_keys, first_block):
    shape = (n_heads * WINDOW, n_keys)
    i = jnp.bitwise_and(_row(shape), WINDOW - 1)
    j = _lane(shape)
    valid = jnp.logical_and(j >= i + 1, j <= i + WINDOW)
    if first_block is not None:
        valid = jnp.logical_and(valid, jnp.logical_or(j >= WINDOW, jnp.logical_not(first_block)))
    return valid


def _stack_heads(pairs):
    return jnp.concatenate([half for t in pairs for half in _halves(t.astype(_CDT))], axis=0)


def _unstack_heads(t, n_pairs):
    lo = _lane((WINDOW, 128)) < HEAD_DIM
    return [jnp.where(lo, t[2 * WINDOW * i:2 * WINDOW * i + WINDOW], t[2 * WINDOW * i + WINDOW:2 * WINDOW * (i + 1)])
            for i in range(n_pairs)]


def _per_head(values):
    n_rows = len(values) * WINDOW
    block = jnp.right_shift(_row((n_rows, 1)), WINDOW.bit_length() - 1)
    out = jnp.zeros((n_rows, 1), _F32)
    for k, v in enumerate(values):
        out = jnp.where(block == k, v, out)
    return out


def _shift_down(z, halo, k):
    out = pltpu.roll(z, k, 0)
    r = _row(z.shape)
    for t in range(k):
        out = jnp.where(r == t, halo[V7X_SUBLANES - k + t:V7X_SUBLANES - k + t + 1, :], out)
    return out


def _shift_up(z, halo, k):
    rows = z.shape[0]
    out = pltpu.roll(z, rows - k, 0)
    r = _row(z.shape)
    for t in range(k):
        out = jnp.where(r == rows - k + t, halo[t:t + 1, :], out)
    return out


class _Dims:
    def __init__(self, s, d, d_ff):
        self.s, self.d, self.d_ff = s, d, d_ff
        self.aw = d // 2
        self.cw = d - self.aw
        self.nq = self.aw // HEAD_DIM
        self.group = self.nq // N_KV_HEADS
        assert self.group % 2 == 0, "a 128-lane pair of query heads must share its kv head"
        self.inw = self.aw + 2 * KV_WIDTH + 3 * self.cw
        self.o_k = self.aw
        self.o_v = self.aw + KV_WIDTH
        self.o_cg = self.aw + 2 * KV_WIDTH
        self.o_bg = self.o_cg + self.cw
        self.o_u = self.o_bg + self.cw
        self.nb = s // WINDOW
        assert s % WINDOW == 0


def _carrying(body, n_in, n_out, n_steps, ca, n_scratch=0):
    n_cin, n_cout = len(ca.operands), len(ca.out_shape)

    def wrapped(*refs):
        refs = list(refs)
        in_refs = [refs.pop(0) for _ in range(n_in)]
        cin_refs = [refs.pop(0) for _ in range(n_cin)]
        out_refs = [refs.pop(0) for _ in range(n_out)]
        cout_refs = [refs.pop(0) for _ in range(n_cout)]
        scratch_refs = [refs.pop(0) for _ in range(n_scratch)]
        if ca.comms:
            @pl.when(pl.program_id(0) == 0)
            def _():
                ca.start(cin_refs, cout_refs, refs)
        if ca.has_middle:
            @pl.when(pl.program_id(0) == _middle_step(n_steps))
            def _():
                ca.middle(cin_refs, cout_refs, refs)
        body(*in_refs, *out_refs, *scratch_refs)
        if ca.comms:
            @pl.when(pl.program_id(0) == n_steps - 1)
            def _():
                ca.finish(cin_refs, cout_refs, refs)

    return wrapped


def _row_kernel(name, body, rows_in, vecs_in, rows_out, vecs_out, comm=()):
    s = rows_in[0].shape[0]
    tr = _pick(s, (512, 256, 128))
    row = lambda a: pl.BlockSpec((tr, a[1] if isinstance(a, tuple) else a.shape[1]), lambda i: (i, 0))
    vec = lambda shape: pl.BlockSpec(tuple(shape), lambda i: (0, 0))
    n_in, n_out = len(rows_in) + len(vecs_in), len(rows_out) + len(vecs_out)
    ca = _CommArgs(list(comm), n_in, n_out)
    blocks = sum(_nbytes((tr, a.shape[1]), a.dtype) for a in rows_in) + sum(_nbytes((tr, sh[1]), dt) for sh, dt in rows_out)
    res = pl.pallas_call(
        _carrying(body, n_in, n_out, s // tr, ca), name=name, grid=(s // tr,),
        in_specs=[row(a) for a in rows_in] + [vec(v.shape) for v in vecs_in] + [_ANY] * len(ca.operands),
        out_specs=[row(sh) for sh, _ in rows_out] + [vec(sh) for sh, _ in vecs_out] + [_ANY] * len(ca.out_shape),
        out_shape=[jax.ShapeDtypeStruct(sh, dt) for sh, dt in list(rows_out) + list(vecs_out)] + ca.out_shape,
        scratch_shapes=ca.sems, input_output_aliases=ca.aliases,
        compiler_params=pltpu.CompilerParams(dimension_semantics=("arbitrary",), vmem_limit_bytes=_vmem_limit(blocks)),
    )(*rows_in, *vecs_in, *ca.operands)
    return list(res)


def _ln2_loss_bwd(r2, target, gain, bias, comm=()):
    s, d = r2.shape

    def body(r_ref, t_ref, g_ref, b_ref, dr_ref, drc_ref, loss_ref, dg_ref, db_ref):
        first = pl.program_id(0) == 0
        yv, xhat, rstd = _ln_fwd(r_ref[...], g_ref[...], b_ref[...])
        err = yv - t_ref[...]
        dr2, dg, db = _ln_bwd(err * (1.0 / d), xhat, rstd, g_ref[...])
        dr_ref[...] = dr2
        drc_ref[...] = dr2.astype(_CDT)
        _accumulate(loss_ref, jnp.zeros(loss_ref.shape, _F32) + 0.5 * jnp.sum(err * err) * (1.0 / d), first)
        _accumulate(dg_ref, dg, first)
        _accumulate(db_ref, db, first)

    return _row_kernel("ln2_loss_bwd", body, [r2, target], [gain, bias], [((s, d), _F32), ((s, d), _CDT)],
                       [((V7X_SUBLANES, V7X_LANES), _F32), ((1, d), _F32), ((1, d), _F32)], comm)


def _ln1_fwd_rows(r1, gain, bias, comm=()):
    s, d = r1.shape

    def body(r_ref, g_ref, b_ref, h_ref, hc_ref):
        h1, _, _ = _ln_fwd(r_ref[...], g_ref[...], b_ref[...])
        h_ref[...] = h1
        hc_ref[...] = h1.astype(_CDT)

    return _row_kernel("ln1", body, [r1], [gain, bias], [((s, d), _F32), ((s, d), _CDT)], [], comm)


def _ln1_bwd_rows(dh1, r1, gain, comm=()):
    s, d = dh1.shape

    def body(dh_ref, r_ref, g_ref, dr_ref, drc_ref, dg_ref, db_ref):
        first = pl.program_id(0) == 0
        _, xhat, rstd = _ln_fwd(r_ref[...], g_ref[...], 0.0)
        dr1, dg, db = _ln_bwd(dh_ref[...], xhat, rstd, g_ref[...])
        dr_ref[...] = dr1
        drc_ref[...] = dr1.astype(_CDT)
        _accumulate(dg_ref, dg, first)
        _accumulate(db_ref, db, first)

    return _row_kernel("ln1_bwd", body, [dh1, r1], [gain], [((s, d), _F32), ((s, d), _CDT)],
                       [((1, d), _F32), ((1, d), _F32)], comm)


def _mixer_fwd(dm, proj, rope, sinks, g_attn, g_conv, conv_w8, comm=()):
    s, d, aw, cw, nq, inw, nb = dm.s, dm.d, dm.aw, dm.cw, dm.nq, dm.inw, dm.nb

    def body(pp_ref, pc_ref, ropep_ref, ropec_ref, sinks_ref, ga_ref, gc_ref, cw_ref,
             mixed_ref, attn_ref, lse_ref, y_ref, qk_ref):
        n = pl.program_id(0)
        cos_c, sgn_c = ropec_ref[:, 0:V7X_LANES], ropec_ref[:, V7X_LANES:2 * V7X_LANES]
        cos_p, sgn_p = ropep_ref[:, 0:V7X_LANES], ropep_ref[:, V7X_LANES:2 * V7X_LANES]
        for g in range(KV_WIDTH // 128):
            qk_ref[:, aw + 128 * g:aw + 128 * g + 128] = _rope(
                pc_ref[:, dm.o_k + 128 * g:dm.o_k + 128 * g + 128], cos_c, sgn_c, 1.0).astype(qk_ref.dtype)
        for j in range(nq // 2):
            qk_ref[:, 128 * j:128 * j + 128] = _rope(pc_ref[:, 128 * j:128 * j + 128], cos_c, sgn_c, 1.0).astype(qk_ref.dtype)
        k_prev = jnp.concatenate([_rope(pp_ref[:, dm.o_k + 128 * g:dm.o_k + 128 * g + 128], cos_p, sgn_p, 1.0)
                                  for g in range(KV_WIDTH // 128)], axis=1)
        kk = jnp.concatenate([k_prev, qk_ref[:, aw:aw + KV_WIDTH].astype(_F32)], axis=0)
        vv = jnp.concatenate([pp_ref[:, dm.o_v:dm.o_v + KV_WIDTH], pc_ref[:, dm.o_v:dm.o_v + KV_WIDTH]], axis=0)
        group, pairs = dm.group, dm.group // 2
        valid = _band_mask(group, 2 * WINDOW, n == 0)
        for h in range(N_KV_HEADS):
            k2, v2 = _dup_head(kk, h).astype(_CDT), _dup_head(vv, h).astype(_CDT)
            q4 = _stack_heads([qk_ref[:, 128 * j:128 * j + 128] for j in range(pairs * h, pairs * (h + 1))])
            sc = jnp.where(valid, _dot(q4, k2, "nt") * ATTN_SCALE, MASKED)
            sink = _per_head([sinks_ref[0, group * h + r] for r in range(group)])
            mx = jnp.maximum(jnp.max(sc, axis=1, keepdims=True), sink)
            p = jnp.exp(sc - mx)
            den = jnp.sum(p, axis=1, keepdims=True) + jnp.exp(sink - mx)
            out = _unstack_heads(_dot(p / den, v2, "nn"), pairs)
            lse = mx + jnp.log(den)
            for r in range(group):
                lse_ref[:, group * h + r:group * h + r + 1] = lse[WINDOW * r:WINDOW * (r + 1)]
            for i in range(pairs):
                j = pairs * h + i
                attn_ref[:, 128 * j:128 * j + 128] = out[i]
        mixed_ref[:, 0:aw] = _rms_fwd(attn_ref[...], ga_ref[...]).astype(mixed_ref.dtype)

        z = pc_ref[:, dm.o_cg:dm.o_cg + cw] * pc_ref[:, dm.o_u:dm.o_u + cw]
        top = WINDOW - V7X_SUBLANES
        halo = pp_ref[top:WINDOW, dm.o_cg:dm.o_cg + cw] * pp_ref[top:WINDOW, dm.o_u:dm.o_u + cw]
        halo = jnp.where(n == 0, jnp.zeros_like(halo), halo)
        y = cw_ref[0:1, :] * _shift_down(z, halo, 2) + cw_ref[1:2, :] * _shift_down(z, halo, 1) + cw_ref[2:3, :] * z
        y_ref[...] = y
        conv = pc_ref[:, dm.o_bg:dm.o_bg + cw] * y
        mixed_ref[:, aw:d] = _rms_fwd(conv, gc_ref[...]).astype(mixed_ref.dtype)

    prev = lambda n: (jnp.maximum(n - 1, 0), 0)
    cur = lambda n: (n, 0)
    fixed = lambda n: (0, 0)
    blocks = 2 * WINDOW * inw * 4 + WINDOW * (d * 2 + aw * 4 + cw * 4 + nq * 4)
    ca = _CommArgs(list(comm), 8, 5)
    return pl.pallas_call(
        _carrying(body, 8, 5, nb, ca), name="mixer_fwd", grid=(nb,),
        in_specs=[pl.BlockSpec((WINDOW, inw), prev), pl.BlockSpec((WINDOW, inw), cur),
                  pl.BlockSpec((WINDOW, 2 * V7X_LANES), prev), pl.BlockSpec((WINDOW, 2 * V7X_LANES), cur),
                  pl.BlockSpec(memory_space=pltpu.SMEM),
                  pl.BlockSpec((1, aw), fixed), pl.BlockSpec((1, cw), fixed), pl.BlockSpec((V7X_SUBLANES, cw), fixed)]
        + [_ANY] * len(ca.operands),
        out_specs=[pl.BlockSpec((WINDOW, d), cur), pl.BlockSpec((WINDOW, aw), cur),
                   pl.BlockSpec((WINDOW, nq), cur), pl.BlockSpec((WINDOW, cw), cur),
                   pl.BlockSpec((WINDOW, aw + KV_WIDTH), cur)] + [_ANY] * len(ca.out_shape),
        out_shape=[jax.ShapeDtypeStruct((s, d), _CDT), jax.ShapeDtypeStruct((s, aw), _F32),
                   jax.ShapeDtypeStruct((s, nq), _F32), jax.ShapeDtypeStruct((s, cw), _F32),
                   jax.ShapeDtypeStruct((s, aw + KV_WIDTH), _CDT)] + ca.out_shape,
        scratch_shapes=ca.sems, input_output_aliases=ca.aliases,
        compiler_params=pltpu.CompilerParams(dimension_semantics=("arbitrary",), vmem_limit_bytes=_vmem_limit(blocks)),
    )(proj, proj, rope, rope, sinks, g_attn, g_conv, conv_w8, *ca.operands)


def _patch_columns(name, a, part, offset):
    s, pw = part.shape
    assert offset % pw == 0 and pw % V7X_LANES == 0
    tr = _pick(s, (512, 256, 128))

    def body(a_ref, p_ref, o_ref):
        del a_ref
        o_ref[...] = p_ref[...]

    return pl.pallas_call(
        body, name=name, grid=(s // tr,),
        in_specs=[_ANY, pl.BlockSpec((tr, pw), lambda i: (i, 0))],
        out_specs=pl.BlockSpec((tr, pw), lambda i: (i, offset // pw)),
        out_shape=jax.ShapeDtypeStruct(a.shape, a.dtype), input_output_aliases={0: 0},
        compiler_params=pltpu.CompilerParams(dimension_semantics=("arbitrary",)),
    )(a, part)


def _mixer_bwd(dm, proj, rope, sinks, g_attn, g_conv, conv_w8, dmixed, attn, lse, y, qk, comm=()):
    s, d, aw, cw, nq, inw, nb = dm.s, dm.d, dm.aw, dm.cw, dm.nq, dm.inw, dm.nb

    def body(pp_ref, pc_ref, pn_ref, ropep_ref, ropec_ref, dmc_ref, dmn_ref, ac_ref,
             lsec_ref, yc_ref, yn_ref, qkp_ref, qkc_ref, sinks_ref, ga_ref, gc_ref, cw_ref,
             dproj_ref, dkv_ref, dga_ref, dgc_ref, dsinks_ref, dcw_ref, dk_carry, dv_carry):
        n = pl.program_id(0)
        first = n == 0
        live = n < nb
        has_next = n < nb - 1
        cos_p, sgn_p = ropep_ref[:, 0:V7X_LANES], ropep_ref[:, V7X_LANES:2 * V7X_LANES]
        cos_c, sgn_c = ropec_ref[:, 0:V7X_LANES], ropec_ref[:, V7X_LANES:2 * V7X_LANES]

        @pl.when(first)
        def _():
            dk_carry[...] = jnp.zeros(dk_carry.shape, _F32)
            dv_carry[...] = jnp.zeros(dv_carry.shape, _F32)

        def write_kv(dk2, dv2, cos, sgn):
            lo = _lane((WINDOW, 128)) < HEAD_DIM
            for g in range(KV_WIDTH // 128):
                dk = jnp.where(lo, _fold_halves(dk2[2 * g]), _fold_halves(dk2[2 * g + 1]))
                dv = jnp.where(lo, _fold_halves(dv2[2 * g]), _fold_halves(dv2[2 * g + 1]))
                dkv_ref[:, 128 * g:128 * g + 128] = _rope(dk, cos, sgn, -1.0).astype(dkv_ref.dtype)
                dkv_ref[:, KV_WIDTH + 128 * g:KV_WIDTH + 128 * g + 128] = dv.astype(dkv_ref.dtype)

        @pl.when(jnp.logical_not(live))
        def _():
            write_kv([dk_carry[h] for h in range(N_KV_HEADS)], [dv_carry[h] for h in range(N_KV_HEADS)], cos_c, sgn_c)

        @pl.when(live)
        def _():
            block_step(pp_ref, pc_ref, pn_ref, dmc_ref, dmn_ref, ac_ref, lsec_ref, yc_ref, yn_ref, qkp_ref, qkc_ref,
                       sinks_ref, ga_ref, gc_ref, cw_ref, dproj_ref, dga_ref, dgc_ref, dsinks_ref, dcw_ref, dk_carry,
                       dv_carry, first, has_next, cos_p, sgn_p, cos_c, sgn_c, write_kv)

    def block_step(pp_ref, pc_ref, pn_ref, dmc_ref, dmn_ref, ac_ref, lsec_ref, yc_ref, yn_ref, qkp_ref, qkc_ref,
                   sinks_ref, ga_ref, gc_ref, cw_ref, dproj_ref, dga_ref, dgc_ref, dsinks_ref, dcw_ref, dk_carry,
                   dv_carry, first, has_next, cos_p, sgn_p, cos_c, sgn_c, write_kv):
        da_c, dga = _rms_bwd(dmc_ref[:, 0:aw], ac_ref[...], ga_ref[...])
        _accumulate(dga_ref, dga, first)
        kk = jnp.concatenate([qkp_ref[:, aw:aw + KV_WIDTH], qkc_ref[:, aw:aw + KV_WIDTH]], axis=0).astype(_F32)
        vv = jnp.concatenate([pp_ref[:, dm.o_v:dm.o_v + KV_WIDTH], pc_ref[:, dm.o_v:dm.o_v + KV_WIDTH]], axis=0)
        group, pairs = dm.group, dm.group // 2
        valid_c = _band_mask(group, 2 * WINDOW, first)
        dk_prev, dv_prev = [], []

        def stacked(q_ref, da, o_ref, lse_ref_, h):
            cols = [slice(128 * j, 128 * j + 128) for j in range(pairs * h, pairs * (h + 1))]
            q4 = _stack_heads([q_ref[:, c] for c in cols])
            do4 = _stack_heads([da[:, c] for c in cols])
            lo = _lane((WINDOW, 128)) < HEAD_DIM
            deltas = []
            for c in cols:
                prod = o_ref[:, c] * da[:, c]
                deltas += [jnp.sum(jnp.where(lo, prod, 0.0), axis=1, keepdims=True),
                           jnp.sum(jnp.where(lo, 0.0, prod), axis=1, keepdims=True)]
            lse4 = jnp.concatenate([lse_ref_[:, group * h + r:group * h + r + 1] for r in range(group)], axis=0)
            return q4, do4, lse4, deltas

        def scores_bwd(q4, do4, lse4, delta4, keys, vals, valid):
            sc = _dot(q4, keys, "nt") * ATTN_SCALE
            p = jnp.exp(jnp.where(valid, sc - lse4, MASKED))
            return p.astype(_CDT), (p * (_dot(do4, vals, "nt") - delta4) * ATTN_SCALE).astype(_CDT)

        for h in range(N_KV_HEADS):
            k2, v2 = _dup_head(kk, h).astype(_CDT), _dup_head(vv, h).astype(_CDT)
            q4, do4, lse4, deltas = stacked(qkc_ref, da_c, ac_ref, lsec_ref, h)
            delta4 = jnp.concatenate(deltas, axis=0)
            p, ds = scores_bwd(q4, do4, lse4, delta4, k2, v2, valid_c)
            for i, dq in enumerate(_unstack_heads(_dot(ds, k2, "nn"), pairs)):
                j = pairs * h + i
                dproj_ref[:, 128 * j:128 * j + 128] = _rope(dq, cos_c, sgn_c, -1.0).astype(dproj_ref.dtype)
            dk = _dot(ds, q4, "tn")
            dv = _dot(p, do4, "tn")
            dk_prev.append(dk_carry[h] + dk[0:WINDOW])
            dv_prev.append(dv_carry[h] + dv[0:WINDOW])
            dk_carry[h] = dk[WINDOW:2 * WINDOW]
            dv_carry[h] = dv[WINDOW:2 * WINDOW]
            heads = slice(group * h, group * (h + 1))
            sink_row, delta_heads = jnp.zeros((1, group), _F32), jnp.zeros((WINDOW, group), _F32)
            for r in range(group):
                sink_row = jnp.where(_lane((1, group)) == r, sinks_ref[0, group * h + r], sink_row)
                delta_heads = jnp.where(_lane((WINDOW, group)) == r, deltas[r], delta_heads)
            loss_sink = jnp.exp(sink_row - lsec_ref[:, heads]) * delta_heads
            _accumulate(dsinks_ref.at[:, heads], -jnp.sum(loss_sink, axis=0, keepdims=True), first)
        write_kv(dk_prev, dv_prev, cos_p, sgn_p)

        bg = pc_ref[:, dm.o_bg:dm.o_bg + cw]
        yc = yc_ref[...]
        dconv, dgc = _rms_bwd(dmc_ref[:, aw:d], bg * yc, gc_ref[...])
        _accumulate(dgc_ref, dgc, first)
        dproj_ref[:, dm.o_bg:dm.o_bg + cw] = (dconv * yc).astype(dproj_ref.dtype)
        dy = dconv * bg
        bg_n = pn_ref[:, dm.o_bg:dm.o_bg + cw]
        dconv_n, _ = _rms_bwd(dmn_ref[:, aw:d], bg_n * yn_ref[...], gc_ref[...])
        halo = jnp.where(has_next, dconv_n * bg_n, 0.0)
        dy1 = _shift_up(dy, halo, 1)
        dy2 = _shift_up(dy, halo, 2)
        dz = cw_ref[2:3, :] * dy + cw_ref[1:2, :] * dy1 + cw_ref[0:1, :] * dy2
        cg = pc_ref[:, dm.o_cg:dm.o_cg + cw]
        u = pc_ref[:, dm.o_u:dm.o_u + cw]
        dproj_ref[:, dm.o_cg:dm.o_cg + cw] = (dz * u).astype(dproj_ref.dtype)
        dproj_ref[:, dm.o_u:dm.o_u + cw] = (dz * cg).astype(dproj_ref.dtype)
        z = cg * u
        dcw = jnp.concatenate(
            [jnp.sum(z * t, axis=0, keepdims=True) for t in (dy2, dy1, dy)]
            + [jnp.zeros((V7X_SUBLANES - 3, cw), _F32)], axis=0)
        _accumulate(dcw_ref, dcw, first)

    at = lambda n: jnp.minimum(n, nb - 1)
    prev = lambda n: (jnp.maximum(at(n) - 1, 0), 0)
    cur = lambda n: (at(n), 0)
    done = lambda n: (jnp.maximum(n - 1, 0), 0)
    nxt8 = lambda n: (jnp.minimum((at(n) + 1) * (WINDOW // V7X_SUBLANES), s // V7X_SUBLANES - 1), 0)
    fixed = lambda n: (0, 0)
    blocks = WINDOW * (2 * inw * 4 + d * 4 + aw * 4 + cw * 4 + inw * 2 + 2 * KV_WIDTH * 2)
    carry = [pltpu.VMEM((N_KV_HEADS, WINDOW, 128), _F32), pltpu.VMEM((N_KV_HEADS, WINDOW, 128), _F32)]
    n_in, n_out = 17, 6
    ca = _CommArgs(list(comm), n_in, n_out)
    return pl.pallas_call(
        _carrying(body, n_in, n_out, nb + 1, ca, n_scratch=len(carry)), name="mixer_bwd", grid=(nb + 1,),
        in_specs=[pl.BlockSpec((WINDOW, inw), prev), pl.BlockSpec((WINDOW, inw), cur), pl.BlockSpec((V7X_SUBLANES, inw), nxt8),
                  pl.BlockSpec((WINDOW, 2 * V7X_LANES), prev), pl.BlockSpec((WINDOW, 2 * V7X_LANES), cur),
                  pl.BlockSpec((WINDOW, d), cur), pl.BlockSpec((V7X_SUBLANES, d), nxt8),
                  pl.BlockSpec((WINDOW, aw), cur), pl.BlockSpec((WINDOW, nq), cur),
                  pl.BlockSpec((WINDOW, cw), cur), pl.BlockSpec((V7X_SUBLANES, cw), nxt8),
                  pl.BlockSpec((WINDOW, aw + KV_WIDTH), prev), pl.BlockSpec((WINDOW, aw + KV_WIDTH), cur),
                  pl.BlockSpec(memory_space=pltpu.SMEM),
                  pl.BlockSpec((1, aw), fixed), pl.BlockSpec((1, cw), fixed), pl.BlockSpec((V7X_SUBLANES, cw), fixed)]
        + [_ANY] * len(ca.operands),
        out_specs=[pl.BlockSpec((WINDOW, inw), cur), pl.BlockSpec((WINDOW, 2 * KV_WIDTH), done),
                   pl.BlockSpec((1, aw), fixed), pl.BlockSpec((1, cw), fixed),
                   pl.BlockSpec((1, nq), fixed), pl.BlockSpec((V7X_SUBLANES, cw), fixed)] + [_ANY] * len(ca.out_shape),
        out_shape=[jax.ShapeDtypeStruct((s, inw), _CDT), jax.ShapeDtypeStruct((s, 2 * KV_WIDTH), _CDT),
                   jax.ShapeDtypeStruct((1, aw), _F32), jax.ShapeDtypeStruct((1, cw), _F32),
                   jax.ShapeDtypeStruct((1, nq), _F32), jax.ShapeDtypeStruct((V7X_SUBLANES, cw), _F32)] + ca.out_shape,
        scratch_shapes=carry + ca.sems, input_output_aliases=ca.aliases,
        compiler_params=pltpu.CompilerParams(dimension_semantics=("arbitrary",), vmem_limit_bytes=_vmem_limit(blocks)),
    )(proj, proj, proj, rope, rope, dmixed, dmixed, attn, lse, y, y, qk, qk, sinks, g_attn, g_conv, conv_w8, *ca.operands)


def _position():
    return lax.axis_index("x"), lax.axis_index("y"), lax.axis_index("c")


def _linear(px, py, pc):
    return 4 * px + 2 * py + pc


def _comm_kernel(name, comm):
    ca = _CommArgs(list(comm), 0, 0)
    n_cin, n_cout = len(ca.operands), len(ca.out_shape)

    def body(*refs):
        cin, cout, sems = refs[:n_cin], refs[n_cin:n_cin + n_cout], refs[n_cin + n_cout:]
        ca.start(cin, cout, sems)
        ca.middle(cin, cout, sems)
        ca.finish(cin, cout, sems)

    return pl.pallas_call(
        body, name=name, out_shape=ca.out_shape, in_specs=[_ANY] * n_cin, out_specs=[_ANY] * n_cout,
        scratch_shapes=ca.sems, input_output_aliases=ca.aliases,
    )(*ca.operands)


def _gather_op(units):
    n = len(units)
    inputs, outputs, aliases = [], [], {}
    for shard, _, _, _ in units:
        inputs.append(shard)
        outputs.append(jax.ShapeDtypeStruct((N_DEV * shard.shape[0], shard.shape[1]), shard.dtype))
    for u, (_, buf, _, _) in enumerate(units):
        if buf is not None:
            aliases[len(inputs)] = u
            inputs.append(buf)

    def plan(ins, outs, sems, north):
        send_sems, recv_sems, local_sems = sems
        x, y, c = _position()
        me, sibling = (x, y, c), (x, y, 1 - c)
        xn, yn, dg = (1 - x, y), (x, 1 - y), (1 - x, 1 - y)
        via, to, k_via, k_other = (yn, xn, 2, 1) if north else (xn, yn, 1, 2)

        def rows(u, px, py, pc):
            shard, _, r0, r1 = units[u]
            return outs[u].at[pl.ds(pl.multiple_of(_linear(px, py, pc) * shard.shape[0] + r0, 16), r1 - r0), :]

        def own(u):
            _, _, r0, r1 = units[u]
            return ins[u].at[pl.ds(r0, r1 - r0), :]

        def copy(u, k, block, to_, src=None):
            return pltpu.make_async_remote_copy(
                src_ref=rows(u, *block) if src is None else src, dst_ref=rows(u, *block),
                send_sem=send_sems.at[u, k], recv_sem=recv_sems.at[u, k], device_id=to_, device_id_type=_MESH)

        us = range(n)
        return dict(
            mine=[pltpu.make_async_copy(own(u), rows(u, *me), local_sems.at[u]) for u in us],
            first=[cp for u in us for cp in (copy(u, 0, me, sibling, src=own(u)), copy(u, 1, me, (*xn, c), src=own(u)),
                                             copy(u, 2, me, (*yn, c), src=own(u)))],
            relay=[copy(u, 3, (*via, c), (*to, c)) for u in us],
            arrived={1: [copy(u, 1, (*xn, c), me) for u in us], 2: [copy(u, 2, (*yn, c), me) for u in us],
                     3: [copy(u, 3, (*dg, c), me) for u in us]},
            passed={1: [copy(u, 4, (*xn, c), sibling) for u in us], 2: [copy(u, 5, (*yn, c), sibling) for u in us],
                    3: [copy(u, 6, (*dg, c), sibling) for u in us]},
            rest=[cp for u in us for cp in (copy(u, 0, sibling, me), copy(u, 4, (*xn, 1 - c), me),
                                            copy(u, 5, (*yn, 1 - c), me), copy(u, 6, (*dg, 1 - c), me))],
            k_via=k_via, k_other=k_other)

    def land(p, k):
        for arrived, onward in zip(p["arrived"][k], p["passed"][k]):
            arrived.wait_recv()
            onward.start()

    def by_core(fn):
        c = lax.axis_index("c")
        for north in (True, False):
            pl.when(c == (1 if north else 0))(functools.partial(fn, north))

    def start(ins, outs, sems):
        p = plan(ins, outs, sems, True)
        for cp in p["mine"] + p["first"]:
            cp.start()

    def middle(ins, outs, sems):
        def go(north):
            p = plan(ins, outs, sems, north)
            land(p, p["k_via"])
            for cp in p["relay"]:
                cp.start()
            land(p, p["k_other"])
        by_core(go)

    def finish(ins, outs, sems):
        def go(north):
            p = plan(ins, outs, sems, north)
            land(p, 3)
            for cp in p["rest"]:
                cp.wait_recv()
            for cp in p["first"] + p["relay"] + [cp for k in (1, 2, 3) for cp in p["passed"][k]]:
                cp.wait_send()
            for cp in p["mine"]:
                cp.wait()
        by_core(go)

    sems = [pltpu.SemaphoreType.DMA((n, 7)), pltpu.SemaphoreType.DMA((n, 7)), pltpu.SemaphoreType.DMA((n,))]
    return _Comm(inputs, outputs, aliases, sems, start, finish, middle)


def _peers(x, y, c):
    out = []
    for k in range(1, N_DEV):
        fx, fy, fc = (k >> 2) & 1, (k >> 1) & 1, k & 1
        out.append((1 - x if fx else x, 1 - y if fy else y, 1 - c if fc else c))
    return out


def _exchange_op(partials):
    n = len(partials)
    outputs = [jax.ShapeDtypeStruct((4, p.shape[0] // N_DEV, p.shape[1]), p.dtype) for p in partials]

    def plan(ins, outs, sems):
        send_sems, recv_sems = sems
        x, y, c = _position()
        out = []
        for a in range(n):
            r = outs[a].shape[1]
            for ch in range(4):
                out.append(pltpu.make_async_remote_copy(
                    src_ref=ins[a].at[pl.ds(pl.multiple_of((2 * ch + 1 - c) * r, 16), r), :], dst_ref=outs[a].at[ch],
                    send_sem=send_sems.at[a, ch], recv_sem=recv_sems.at[a, ch], device_id=(x, y, 1 - c),
                    device_id_type=_MESH))
        return out

    def start(ins, outs, sems):
        for cp in plan(ins, outs, sems):
            cp.start()

    def finish(ins, outs, sems):
        copies = plan(ins, outs, sems)
        for cp in copies:
            cp.wait_recv()
        for cp in copies:
            cp.wait_send()

    sems = [pltpu.SemaphoreType.DMA((n, 4)), pltpu.SemaphoreType.DMA((n, 4))]
    return _Comm(list(partials), outputs, {}, sems, start, finish)


def _chip_send_op(units):
    n = len(units)
    inputs, outputs, aliases = [], [], {}
    for q, _, _, _ in units:
        inputs.append(q)
        outputs.append(jax.ShapeDtypeStruct(q.shape, q.dtype))
    for u, (_, buf, _, _) in enumerate(units):
        if buf is not None:
            aliases[len(inputs)] = u
            inputs.append(buf)

    def plan(ins, outs, sems):
        send_sems, recv_sems, local_sems = sems
        x, y, c = _position()
        my_chip = 2 * x + y
        chips = [(1 - x, y), (x, 1 - y), (1 - x, 1 - y)]
        mine, sends, arrivals = [], [], []
        for u, (_, _, r0, r1) in enumerate(units):
            span = pl.ds(r0, r1 - r0)
            mine.append(pltpu.make_async_copy(ins[u].at[my_chip, span, :], outs[u].at[my_chip, span, :], local_sems.at[u]))
            for k, (px, py) in enumerate(chips):
                sends.append(pltpu.make_async_remote_copy(
                    src_ref=ins[u].at[2 * px + py, span, :], dst_ref=outs[u].at[my_chip, span, :],
                    send_sem=send_sems.at[u, k], recv_sem=recv_sems.at[u, k], device_id=(px, py, c), device_id_type=_MESH))
                arrivals.append(pltpu.make_async_remote_copy(
                    src_ref=ins[u].at[my_chip, span, :], dst_ref=outs[u].at[2 * px + py, span, :],
                    send_sem=send_sems.at[u, k], recv_sem=recv_sems.at[u, k], device_id=(px, py, c), device_id_type=_MESH))
        return mine, sends, arrivals

    def start(ins, outs, sems):
        mine, sends, _ = plan(ins, outs, sems)
        for cp in mine + sends:
            cp.start()

    def finish(ins, outs, sems):
        mine, sends, arrivals = plan(ins, outs, sems)
        for cp in arrivals:
            cp.wait_recv()
        for cp in sends:
            cp.wait_send()
        for cp in mine:
            cp.wait()

    sems = [pltpu.SemaphoreType.DMA((n, 3)), pltpu.SemaphoreType.DMA((n, 3)), pltpu.SemaphoreType.DMA((n,))]
    return _Comm(inputs, outputs, aliases, sems, start, finish)


def _pair_sum(name, partial, received):
    _, rows, cols = received.shape
    tr = _pick(rows, (352, 288, 256, 128, 64, 32, 16))
    p4 = partial.reshape(4, 2, rows, cols)
    kind = jnp.reshape(lax.axis_index("c"), (1,)).astype(jnp.int32)

    def body(kind_ref, p_ref, r_ref, o_ref):
        o_ref[0] = (p_ref[0, 0].astype(_F32) + r_ref[0].astype(_F32)).astype(o_ref.dtype)

    return pl.pallas_call(
        body, name=name,
        grid_spec=pltpu.PrefetchScalarGridSpec(
            num_scalar_prefetch=1, grid=(4, rows // tr),
            in_specs=[pl.BlockSpec((1, 1, tr, cols), lambda ch, i, kind_ref: (ch, kind_ref[0], i, 0)),
                      pl.BlockSpec((1, tr, cols), lambda ch, i, kind_ref: (ch, i, 0))],
            out_specs=pl.BlockSpec((1, tr, cols), lambda ch, i, kind_ref: (ch, i, 0))),
        out_shape=jax.ShapeDtypeStruct(received.shape, received.dtype),
        compiler_params=pltpu.CompilerParams(dimension_semantics=("arbitrary", "arbitrary")),
    )(kind, p4, received)


def _all_reduce_small(name, v):
    rows = v.shape[0]

    def body(v_ref, out_ref, land_ref, send_sems, recv_sems):
        x, y, c = _position()
        me = _linear(x, y, c)
        peers = _peers(x, y, c)
        land_ref[me] = v_ref[...]
        sends = [pltpu.make_async_remote_copy(
            src_ref=v_ref, dst_ref=land_ref.at[me], send_sem=send_sems.at[k], recv_sem=recv_sems.at[k],
            device_id=peer, device_id_type=_MESH) for k, peer in enumerate(peers)]
        for cp in sends:
            cp.start()
        for k, peer in enumerate(peers):
            pltpu.make_async_remote_copy(
                src_ref=v_ref, dst_ref=land_ref.at[_linear(*peer)], send_sem=send_sems.at[k], recv_sem=recv_sems.at[k],
                device_id=peer, device_id_type=_MESH).wait_recv()
        for cp in sends:
            cp.wait_send()
        total = land_ref[0]
        for s in range(1, N_DEV):
            total = total + land_ref[s]
        out_ref[...] = total

    return pl.pallas_call(
        body, name=name, out_shape=jax.ShapeDtypeStruct(v.shape, _F32),
        in_specs=[pl.BlockSpec(memory_space=pltpu.VMEM)], out_specs=pl.BlockSpec(memory_space=pltpu.VMEM),
        scratch_shapes=[pltpu.VMEM((N_DEV, rows, V7X_LANES), _F32), pltpu.SemaphoreType.DMA((7,)), pltpu.SemaphoreType.DMA((7,))],
    )(v)


def _adamw(name, w, slots, m, v):
    rows, cols = w.shape
    n_slots = slots.shape[0]
    tr = _pick(rows, (176, 144, 128, 64, 32, 16, 8))

    def body(w_ref, s_ref, m_ref, v_ref, g_ref, d_ref, nm_ref, nv_ref):
        g = s_ref[0].astype(_F32)
        for k in range(1, n_slots):
            g = g + s_ref[k].astype(_F32)
        nm = ADAM_B1 * m_ref[...] + (1.0 - ADAM_B1) * g
        nv = ADAM_B2 * v_ref[...] + (1.0 - ADAM_B2) * (g * g)
        m_hat = nm / (1.0 - ADAM_B1 ** ADAM_STEP)
        v_hat = nv / (1.0 - ADAM_B2 ** ADAM_STEP)
        g_ref[...] = g
        d_ref[...] = -ADAM_LR * (m_hat / (jnp.sqrt(v_hat) + ADAM_EPS) + ADAM_WD * w_ref[...])
        nm_ref[...] = nm
        nv_ref[...] = nv

    spec = pl.BlockSpec((tr, cols), lambda i: (i, 0))
    blocks = 7 * tr * cols * 4 + _nbytes((n_slots, tr, cols), slots.dtype)
    return pl.pallas_call(
        body, name=name, grid=(rows // tr,),
        in_specs=[spec, pl.BlockSpec((n_slots, tr, cols), lambda i: (0, i, 0)), spec, spec], out_specs=[spec] * 4,
        out_shape=[jax.ShapeDtypeStruct((rows, cols), _F32)] * 4,
        compiler_params=pltpu.CompilerParams(dimension_semantics=("arbitrary",), vmem_limit_bytes=_vmem_limit(blocks)),
    )(w, slots, m, v)


def _pad_rows(a, rows):
    return jnp.pad(a, ((0, rows - a.shape[0]), (0, 0)))


def _pack(parts):
    rows, spans, at = [], [], 0
    for p in parts:
        p = p.reshape(-1)
        r = -(-p.shape[0] // V7X_LANES)
        rows.append(jnp.pad(p, (0, r * V7X_LANES - p.shape[0])).reshape(r, V7X_LANES))
        spans.append((at, r, p.shape[0]))
        at += r
    packed = jnp.concatenate(rows, axis=0)
    return _pad_rows(packed, -(-at // V7X_SUBLANES) * V7X_SUBLANES), spans


def _unpack(packed, spans, shapes):
    return [packed[at:at + r].reshape(-1)[:size].reshape(shape) for (at, r, size), shape in zip(spans, shapes)]


def kernel(x, positions, w_in, conv_w, sinks, g_attn, g_conv, w_out, ln1_g, ln1_b, w_gate, w_up, w_down, ln2_g, ln2_b, loss_target, m_w_in, m_conv_w, m_sinks, m_g_attn, m_g_conv, m_w_out, m_ln1_g, m_ln1_b, m_w_gate, m_w_up, m_w_down, m_ln2_g, m_ln2_b, v_w_in, v_conv_w, v_sinks, v_g_attn, v_g_conv, v_w_out, v_ln1_g, v_ln1_b, v_w_gate, v_w_up, v_w_down, v_ln2_g, v_ln2_b):
    _, s, d = x.shape
    d_ff = N_DEV * w_gate.shape[2]
    dm = _Dims(s, d, d_ff)
    aw, cw, nq, inw = dm.aw, dm.cw, dm.nq, dm.inw
    x2 = x[0]
    pos = positions[0].reshape(s, 1)
    inv_freq = ROPE_THETA ** (-jnp.arange(0, ROT_DIM, 2, dtype=_F32) / ROT_DIM)
    invf = jnp.tile(inv_freq, V7X_LANES // (ROT_DIM // 2)).reshape(1, V7X_LANES)

    conv_cols = conv_w.shape[2]
    sh_in, sh_out = w_in[0].T.astype(_CDT), w_out[0].astype(_CDT)
    sh_gate, sh_up, sh_down = w_gate[0].T.astype(_CDT), w_up[0].T.astype(_CDT), w_down[0].astype(_CDT)
    r_in, r_out, r_ff = sh_in.shape[0], sh_out.shape[0], sh_gate.shape[0]
    q_ff = r_ff // 4
    assert q_ff % 16 == 0
    def prepare_body(x_ref, pos_ref, invf_ref, xc_ref, rope_ref):
        xc_ref[...] = x_ref[...].astype(_CDT)
        cos, sgn = _rope_tables(pos_ref[...], invf_ref[...])
        rope_ref[:, 0:V7X_LANES] = cos
        rope_ref[:, V7X_LANES:2 * V7X_LANES] = sgn

    x_c, rope, w_in_t, conv_all = _row_kernel(
        "prepare_gather_w_in", prepare_body, [x2, pos], [invf], [((s, d), _CDT), ((s, 2 * V7X_LANES), _F32)], [],
        comm=[_gather_op([(sh_in, None, 0, r_in), (_pad_rows(conv_w[0], 16), None, 0, 16)])])
    conv_full = conv_all.reshape(N_DEV, 16, conv_cols)[:, :3, :].transpose(1, 0, 2).reshape(3, cw)
    conv_w8 = _pad_rows(conv_full, V7X_SUBLANES)

    tm = _pick(s, (1024, 512, 256, 128))
    tm2 = _pick(s, (2048, 1024, 512, 256, 128))
    tr = _pick(s, (512, 256, 128))
    tn_in = _pick(inw, (512, 256, 128))
    tn_ff = _pick(d_ff, (512, 256, 128))

    proj, w_out_f, w_gate_t = _matmul(
        "proj", [[(x_c, w_in_t, "nt")]], s, inw, d, tm2, tn_in, d, [],
        [((s, inw), _F32, (tm2, tn_in), _tile_ij)], _store_epilogue,
        comm=[_gather_op([(sh_out, None, 0, r_out), (sh_gate, None, 0, 2 * q_ff)])])
    mixed, attn, lse, y_conv, qk_rot, w_gate_t, w_up_t = _mixer_fwd(
        dm, proj, rope, sinks, g_attn, g_conv, conv_w8,
        comm=[_gather_op([(sh_gate, w_gate_t, 2 * q_ff, r_ff), (sh_up, None, 0, 2 * q_ff)])])

    def residual_epilogue(accs, ex, out, first):
        out[0][...] = DEEPNORM_ALPHA * ex[0][...] + accs[0]

    tn_d = _pick(d, (512,))
    r1, w_up_t = _matmul(
        "out_proj", [[(mixed, w_out_f, "nn")]], s, d, d, tm2, tn_d, d, [(x2, (tm2, tn_d), _tile_ij)],
        [((s, d), _F32, (tm2, tn_d), _tile_ij)], residual_epilogue,
        comm=[_gather_op([(sh_up, w_up_t, 2 * q_ff, 3 * q_ff)])])
    h1, h1_c, w_up_t = _ln1_fwd_rows(r1, ln1_g, ln1_b, comm=[_gather_op([(sh_up, w_up_t, 3 * q_ff, r_ff)])])

    def swiglu_epilogue(accs, ex, out, first):
        gate_v, up_v = accs
        out[0][...] = gate_v
        out[1][...] = up_v
        out[2][...] = (gate_v * jax.nn.sigmoid(gate_v) * up_v).astype(_CDT)

    gate, up, act, w_down_f = _matmul(
        "gate_up", [[(h1_c, w_gate_t, "nt")], [(h1_c, w_up_t, "nt")]], s, d_ff, d, tm, tn_ff, d, [],
        [((s, d_ff), _F32, (tm, tn_ff), _tile_ij), ((s, d_ff), _F32, (tm, tn_ff), _tile_ij),
         ((s, d_ff), _CDT, (tm, tn_ff), _tile_ij)], swiglu_epilogue,
        comm=[_gather_op([(sh_down, None, 0, r_ff)])])

    (r2,) = _matmul("down", [[(act, w_down_f, "nn")]], s, d, d_ff, tm, tn_d, d_ff, [(h1, (tm, tn_d), _tile_ij)],
                    [((s, d), _F32, (tm, tn_d), _tile_ij)], residual_epilogue)
    dr2, dr2_c, loss_acc, d_ln2_g, d_ln2_b = _ln2_loss_bwd(r2, loss_target[0], ln2_g, ln2_b)

    def swiglu_bwd_epilogue(accs, ex, out, first):
        gate_v, up_v = ex[0][...], ex[1][...]
        sig = jax.nn.sigmoid(gate_v)
        out[0][...] = (accs[0] * up_v * (sig * (1.0 + gate_v * (1.0 - sig)))).astype(_CDT)
        out[1][...] = (accs[0] * (gate_v * sig)).astype(_CDT)

    dgate, dup = _matmul(
        "dact", [[(dr2_c, w_down_f, "nt")]], s, d_ff, d, tm2, tn_ff, d,
        [(gate, (tm2, tn_ff), _tile_ij), (up, (tm2, tn_ff), _tile_ij)],
        [((s, d_ff), _CDT, (tm2, tn_ff), _tile_ij), ((s, d_ff), _CDT, (tm2, tn_ff), _tile_ij)], swiglu_bwd_epilogue)
    def weight_grad(name, a, b, comm=()):
        rows = a.shape[1]
        tw, tn_w = _pick(rows, (512, 256, 128)), d
        return _matmul(name, [[(a, b, "tn")]], rows, d, s, tw, tn_w, s, [],
                       [((rows, d), _CDT, (tw, tn_w), _tile_ij)], _store_epilogue, comm=comm, j_outer=True)

    (dw_down,) = weight_grad("dw_down", act, dr2_c)
    dw_gate_t, x_down = weight_grad("dw_gate", dgate, h1_c, comm=[_exchange_op([dw_down])])
    q_down = _pair_sum("chip_sum_w_down", dw_down, x_down)
    dw_up_t, l_down, x_gate = weight_grad(
        "dw_up", dup, h1_c, comm=[_chip_send_op([(q_down, None, 0, 2 * q_ff)]), _exchange_op([dw_gate_t])])
    q_gate = _pair_sum("chip_sum_w_gate", dw_gate_t, x_gate)

    tn_h = _pick(d, (512,))
    dh1, l_down, l_gate, x_up = _matmul(
        "dh1", [[(dgate, w_gate_t, "nn"), (dup, w_up_t, "nn")]], s, d, d_ff, tr, tn_h, d_ff,
        [(dr2, (tr, tn_h), _tile_ij)], [((s, d), _F32, (tr, tn_h), _tile_ij)], residual_epilogue,
        comm=[_chip_send_op([(q_down, l_down, 2 * q_ff, r_ff), (q_gate, None, 0, r_ff)]), _exchange_op([dw_up_t])])
    q_up = _pair_sum("chip_sum_w_up", dw_up_t, x_up)
    dr1, dr1_c, d_ln1_g, d_ln1_b = _ln1_bwd_rows(dh1, r1, ln1_g)
    (dmixed,) = _matmul("dmixed", [[(dr1_c, w_out_f, "nt")]], s, d, d, tm2, tn_d, d, [],
                        [((s, d), _F32, (tm2, tn_d), _tile_ij)], _store_epilogue)
    (dw_out,) = weight_grad("dw_out", mixed, dr1_c)
    dproj, dkv, d_g_attn, d_g_conv, d_sinks, d_conv8, l_up, x_out = _mixer_bwd(
        dm, proj, rope, sinks, g_attn, g_conv, conv_w8, dmixed, attn, lse, y_conv, qk_rot,
        comm=[_chip_send_op([(q_up, None, 0, r_ff)]), _exchange_op([dw_out])])
    dproj = _patch_columns("dproj_kv", dproj, dkv, dm.o_k)
    q_out = _pair_sum("chip_sum_w_out", dw_out, x_out)
    dw_in_t, l_out = weight_grad("dw_in", dproj, x_c, comm=[_chip_send_op([(q_out, None, 0, r_out)])])
    (x_in,) = _comm_kernel("exchange_w_in", [_exchange_op([dw_in_t])])
    q_in = _pair_sum("chip_sum_w_in", dw_in_t, x_in)

    grad_x, l_in = _matmul("dx", [[(dproj, w_in_t, "nn")]], s, d, inw, tm, tn_d, inw,
                           [(dr1, (tm, tn_d), _tile_ij)], [((s, d), _F32, (tm, tn_d), _tile_ij)], residual_epilogue,
                           comm=[_chip_send_op([(q_in, None, 0, r_in)])])

    small_parts = [d_conv8[:3], d_sinks, d_g_attn, d_g_conv, d_ln1_g, d_ln1_b, d_ln2_g, d_ln2_b, loss_acc[0:1, 0:1]]
    packed, spans = _pack(small_parts)
    reduced = _unpack(_all_reduce_small("reduce_small", packed), spans, [p.shape for p in small_parts])
    g_conv_full, g_sinks, g_g_attn, g_g_conv, g_ln1_g, g_ln1_b, g_ln2_g, g_ln2_b, loss_sum = reduced
    me = _linear(*_position())
    g_conv_w = lax.dynamic_slice(g_conv_full, (0, me * conv_cols), (3, conv_cols))
    loss = loss_sum[0, 0]

    big = {"w_in": (w_in[0].T, l_in, m_w_in[0].T, v_w_in[0].T), "w_out": (w_out[0], l_out, m_w_out[0], v_w_out[0]),
           "w_gate": (w_gate[0].T, l_gate, m_w_gate[0].T, v_w_gate[0].T),
           "w_up": (w_up[0].T, l_up, m_w_up[0].T, v_w_up[0].T), "w_down": (w_down[0], l_down, m_w_down[0], v_w_down[0])}
    res = {nm: tuple(_adamw(f"adamw_{nm}", w, slots, m, v)) for nm, (w, slots, m, v) in big.items()}
    for nm in ("w_in", "w_gate", "w_up"):
        res[nm] = tuple(a.T for a in res[nm])
    small_names = ["conv_w", "sinks", "g_attn", "g_conv", "ln1_g", "ln1_b", "ln2_g", "ln2_b"]
    small_w = [conv_w, sinks, g_attn, g_conv, ln1_g, ln1_b, ln2_g, ln2_b]
    small_g = [g_conv_w[None], g_sinks, g_g_attn, g_g_conv, g_ln1_g, g_ln1_b, g_ln2_g, g_ln2_b]
    small_m = [m_conv_w, m_sinks, m_g_attn, m_g_conv, m_ln1_g, m_ln1_b, m_ln2_g, m_ln2_b]
    small_v = [v_conv_w, v_sinks, v_g_attn, v_g_conv, v_ln1_g, v_ln1_b, v_ln2_g, v_ln2_b]
    pw, sp = _pack(small_w)
    pg, _ = _pack(small_g)
    pm, _ = _pack(small_m)
    pv, _ = _pack(small_v)
    shapes = [w.shape for w in small_w]
    _, sd, sm, sv = [_unpack(p, sp, shapes) for p in _adamw("adamw_small", pw, pg[None], pm, pv)]
    for i, nm in enumerate(small_names):
        res[nm] = (small_g[i].reshape(shapes[i]), sd[i], sm[i], sv[i])

    order = ["w_in", "conv_w", "sinks", "g_attn", "g_conv", "w_out", "ln1_g", "ln1_b", "w_gate", "w_up", "w_down", "ln2_g", "ln2_b"]

    def lead(a, nm):
        return a[None] if nm in big else a

    return (loss, grad_x[None],
            *[lead(res[nm][0], nm) for nm in order], *[lead(res[nm][1], nm) for nm in order],
            *[lead(res[nm][2], nm) for nm in order], *[lead(res[nm][3], nm) for nm in order])
```

```python
import functools

import jax
import jax.numpy as jnp
from jax import lax
from jax.experimental import pallas as pl
from jax.experimental.pallas import tpu as pltpu

_F32 = jnp.float32
_CDT = jnp.bfloat16

HEAD_DIM = 64
WINDOW = 128
N_KV_HEADS = 4
KV_WIDTH = N_KV_HEADS * HEAD_DIM
ROT_DIM = HEAD_DIM // 4
ROPE_THETA = 500000.0
ATTN_SCALE = HEAD_DIM ** -0.5
DEPTH = 1
DEEPNORM_ALPHA = (2 * DEPTH) ** 0.25
LN_EPS = 1e-5
RMS_EPS = 1e-6
ADAM_LR = 0.001
ADAM_B1 = 0.9
ADAM_B2 = 0.999
ADAM_EPS = 1e-08
ADAM_WD = 0.01
ADAM_STEP = 10
N_DEV = 8
MASKED = -1e30

MIB = 1024 * 1024
V7X_VMEM_BYTES = 64 * MIB
V7X_LANES = 128
V7X_SUBLANES = 8
BODY_TEMPORARIES_BYTES = 16 * MIB
VMEM_LIMIT_FLOOR_BYTES = 32 * MIB
VMEM_LIMIT_CEILING_BYTES = V7X_VMEM_BYTES - 8 * MIB
_MESH = pl.DeviceIdType.MESH
_ANY = pl.BlockSpec(memory_space=pl.ANY)


def _vmem_limit(block_bytes, scratch_bytes=0):
    want = 2 * block_bytes + scratch_bytes + BODY_TEMPORARIES_BYTES
    return int(min(max(want, VMEM_LIMIT_FLOOR_BYTES), VMEM_LIMIT_CEILING_BYTES))


def _nbytes(shape, dtype):
    n = 1
    for s in shape:
        n *= s
    return n * jnp.dtype(dtype).itemsize


def _pick(n, candidates):
    for c in candidates:
        if n % c == 0:
            return c
    raise ValueError(f"no tile of {candidates} divides {n}")


_DOT_DIMS = {"nn": ((1,), (0,)), "nt": ((1,), (1,)), "tn": ((0,), (0,))}


def _dot(a, b, mode):
    return lax.dot_general(a.astype(_CDT), b.astype(_CDT), (_DOT_DIMS[mode], ((), ())),
                           preferred_element_type=_F32)


def _accumulate(ref, val, first):
    @pl.when(first)
    def _():
        ref[...] = val

    @pl.when(jnp.logical_not(first))
    def _():
        ref[...] += val


class _Comm:
    def __init__(self, inputs, outputs, aliases, sems, start, finish, middle=None):
        self.inputs, self.outputs, self.aliases, self.sems = inputs, outputs, aliases, sems
        self.start, self.finish, self.middle = start, finish, middle


def _middle_step(n_steps):
    return (2 * n_steps) // 3


class _CommArgs:
    def __init__(self, comms, n_in_before, n_out_before):
        self.comms, self.operands, self.out_shape, self.aliases, self.sems, self.at = comms, [], [], {}, [], []
        for cm in comms:
            self.at.append((len(self.operands), len(self.out_shape), len(self.sems)))
            for i_in, i_out in cm.aliases.items():
                self.aliases[n_in_before + len(self.operands) + i_in] = n_out_before + len(self.out_shape) + i_out
            self.operands += cm.inputs
            self.out_shape += cm.outputs
            self.sems += cm.sems

    def _each(self, in_refs, out_refs, sem_refs):
        for cm, (i0, o0, s0) in zip(self.comms, self.at):
            yield cm, (in_refs[i0:i0 + len(cm.inputs)], out_refs[o0:o0 + len(cm.outputs)], sem_refs[s0:s0 + len(cm.sems)])

    def start(self, in_refs, out_refs, sem_refs):
        for cm, refs in self._each(in_refs, out_refs, sem_refs):
            cm.start(*refs)

    def finish(self, in_refs, out_refs, sem_refs):
        for cm, refs in self._each(in_refs, out_refs, sem_refs):
            cm.finish(*refs)

    @property
    def has_middle(self):
        return any(cm.middle is not None for cm in self.comms)

    def middle(self, in_refs, out_refs, sem_refs):
        for cm, refs in self._each(in_refs, out_refs, sem_refs):
            if cm.middle is not None:
                cm.middle(*refs)


def _matmul(name, groups, m, n, k, tm, tn, tk, extras, outs, epilogue, comm=(), j_outer=False, b_buffers=None):
    assert m % tm == 0 and n % tn == 0 and k % tk == 0, (name, m, n, k, tm, tn, tk)
    nk = k // tk
    terms = [t for g in groups for t in g]
    operands, in_specs, block_bytes = [], [], 0

    def spec(blk, imap, buffers=None):
        index_map = (lambda g0, g1, kk: imap(g1, g0, kk)) if j_outer else imap
        if buffers is None:
            return pl.BlockSpec(blk, index_map)
        return pl.BlockSpec(blk, index_map, pipeline_mode=pl.Buffered(buffers))

    for a, b, mode in terms:
        assert a.shape == ((k, m) if mode == "tn" else (m, k)), (name, a.shape, mode)
        assert b.shape == ((n, k) if mode == "nt" else (k, n)), (name, b.shape, mode)
        if mode == "tn":
            a_blk, a_map = (tk, tm), (lambda i, j, kk: (kk, i))
        else:
            a_blk, a_map = (tm, tk), (lambda i, j, kk: (i, kk))
        if mode == "nt":
            b_blk, b_map = (tn, tk), (lambda i, j, kk: (j, kk))
        else:
            b_blk, b_map = (tk, tn), (lambda i, j, kk: (kk, j))
        operands += [a, b]
        in_specs += [spec(a_blk, a_map), spec(b_blk, b_map, b_buffers)]
        block_bytes += _nbytes(a_blk, a.dtype) + _nbytes(b_blk, b.dtype) * (b_buffers or 2) // 2
    for arr, blk, imap in extras:
        operands.append(arr)
        in_specs.append(spec(blk, lambda i, j, kk, imap=imap: imap(i, j)))
        block_bytes += _nbytes(blk, arr.dtype)
    out_shape, out_specs = [], []
    for shape, dtype, blk, imap in outs:
        out_shape.append(jax.ShapeDtypeStruct(shape, dtype))
        out_specs.append(spec(blk, lambda i, j, kk, imap=imap: imap(i, j)))
        block_bytes += _nbytes(blk, dtype)
    n_terms, n_extra, n_out, n_groups = len(terms), len(extras), len(outs), len(groups)
    scratch = [pltpu.VMEM((tm, tn), _F32) for _ in range(n_groups)] if nk > 1 else []
    ca = _CommArgs(list(comm), len(operands), n_out)
    n_cin, n_cout, n_acc = len(ca.operands), len(ca.out_shape), len(scratch)
    tiles = (m // tm, n // tn)
    grid = (tiles[1], tiles[0], nk) if j_outer else (tiles[0], tiles[1], nk)

    def body(*refs):
        refs = list(refs)
        term_refs = [refs.pop(0) for _ in range(2 * n_terms)]
        extra_refs = [refs.pop(0) for _ in range(n_extra)]
        cin_refs = [refs.pop(0) for _ in range(n_cin)]
        out_refs = [refs.pop(0) for _ in range(n_out)]
        cout_refs = [refs.pop(0) for _ in range(n_cout)]
        acc_refs = [refs.pop(0) for _ in range(n_acc)]
        sem_refs = refs
        g0, g1, kk = pl.program_id(0), pl.program_id(1), pl.program_id(2)
        first = jnp.logical_and(g0 == 0, g1 == 0)
        if comm:
            @pl.when(jnp.logical_and(first, kk == 0))
            def _():
                ca.start(cin_refs, cout_refs, sem_refs)
        if ca.has_middle:
            step = (g0 * grid[1] + g1) * nk + kk

            @pl.when(step == _middle_step(grid[0] * grid[1] * nk))
            def _():
                ca.middle(cin_refs, cout_refs, sem_refs)
        partial, t = [], 0
        for g in groups:
            s = None
            for _, _, mode in g:
                d = _dot(term_refs[2 * t][...], term_refs[2 * t + 1][...], mode)
                s = d if s is None else s + d
                t += 1
            partial.append(s)
        if nk == 1:
            epilogue(partial, extra_refs, out_refs, first)
        else:
            for acc, p in zip(acc_refs, partial):
                _accumulate(acc, p, kk == 0)

            @pl.when(kk == nk - 1)
            def _():
                epilogue([acc[...] for acc in acc_refs], extra_refs, out_refs, first)
        if comm:
            @pl.when(jnp.logical_and(jnp.logical_and(g0 == grid[0] - 1, g1 == grid[1] - 1), kk == nk - 1))
            def _():
                ca.finish(cin_refs, cout_refs, sem_refs)

    res = pl.pallas_call(
        body, name=name, grid=grid,
        in_specs=in_specs + [_ANY] * n_cin, out_specs=out_specs + [_ANY] * n_cout,
        out_shape=out_shape + ca.out_shape, scratch_shapes=scratch + ca.sems, input_output_aliases=ca.aliases,
        compiler_params=pltpu.CompilerParams(
            dimension_semantics=("arbitrary", "arbitrary", "arbitrary"),
            vmem_limit_bytes=_vmem_limit(block_bytes, n_groups * tm * tn * 4 if nk > 1 else 0)),
    )(*operands, *ca.operands)
    return list(res[:n_out]) + list(res[n_out:])


def _store_epilogue(accs, extra_refs, out_refs, first):
    for acc, ref in zip(accs, out_refs):
        ref[...] = acc.astype(ref.dtype)


def _tile_ij(i, j):
    return (i, j)


def _row_i(i, j):
    return (i, 0)


def _whole(i, j):
    return (0, 0)


def _mean(v):
    return jnp.mean(v, axis=-1, keepdims=True)


def _ln_fwd(r, g, b):
    xc = r - _mean(r)
    rstd = lax.rsqrt(_mean(xc * xc) + LN_EPS)
    xhat = xc * rstd
    return xhat * g + b, xhat, rstd


def _ln_bwd(dy, xhat, rstd, g):
    dxh = dy * g
    dr = rstd * (dxh - _mean(dxh) - xhat * _mean(dxh * xhat))
    return dr, jnp.sum(dy * xhat, axis=0, keepdims=True), jnp.sum(dy, axis=0, keepdims=True)


def _rms_fwd(a, g):
    rstd = lax.rsqrt(_mean(a * a) + RMS_EPS)
    return a * rstd * g


def _rms_bwd(dm, a, g):
    rstd = lax.rsqrt(_mean(a * a) + RMS_EPS)
    nhat = a * rstd
    dn = dm * g
    da = rstd * (dn - nhat * _mean(dn * nhat))
    return da, jnp.sum(dm * nhat, axis=0, keepdims=True)


def _lane(shape):
    return lax.broadcasted_iota(jnp.int32, shape, 1)


def _row(shape):
    return lax.broadcasted_iota(jnp.int32, shape, 0)


def _rope_tables(pos, invf):
    ang = pos.astype(_F32) * invf
    lane = _lane(ang.shape)
    in_rot = (lane % HEAD_DIM) < ROT_DIM
    first = (lane % ROT_DIM) < ROT_DIM // 2
    cos = jnp.where(in_rot, jnp.cos(ang), 1.0)
    sin = jnp.sin(ang)
    sgn = jnp.where(in_rot, jnp.where(first, -sin, sin), 0.0)
    return cos, sgn


def _rope(t, cos, sgn, sign):
    half = ROT_DIM // 2
    first = (_lane(t.shape) % ROT_DIM) < half
    partner = jnp.where(first, pltpu.roll(t, V7X_LANES - half, 1), pltpu.roll(t, half, 1))
    return t * cos + partner * (sgn * sign)


def _dup_head(t, h):
    g = t[:, 128 * (h // 2):128 * (h // 2) + 128]
    r = pltpu.roll(g, HEAD_DIM, 1)
    lo = _lane(g.shape) < HEAD_DIM
    return jnp.where(lo, g, r) if h % 2 == 0 else jnp.where(lo, r, g)


def _fold_halves(t):
    return t + pltpu.roll(t, HEAD_DIM, 1)


def _halves(t):
    lo = _lane(t.shape) < HEAD_DIM
    zero = jnp.zeros_like(t)
    return jnp.where(lo, t, zero), jnp.where(lo, zero, t)


def _band_mask(n_heads, n_keys, first_block):
    shape = (n_heads * WINDOW, n_keys)
    i = jnp.bitwise_and(_row(shape), WINDOW - 1)
    j = _lane(shape)
    valid = jnp.logical_and(j >= i + 1, j <= i + WINDOW)
    if first_block is not None:
        valid = jnp.logical_and(valid, jnp.logical_or(j >= WINDOW, jnp.logical_not(first_block)))
    return valid


def _stack_heads(pairs):
    return jnp.concatenate([half for t in pairs for half in _halves(t.astype(_CDT))], axis=0)


def _unstack_heads(t, n_pairs):
    lo = _lane((WINDOW, 128)) < HEAD_DIM
    return [jnp.where(lo, t[2 * WINDOW * i:2 * WINDOW * i + WINDOW], t[2 * WINDOW * i + WINDOW:2 * WINDOW * (i + 1)])
            for i in range(n_pairs)]


def _per_head(values):
    n_rows = len(values) * WINDOW
    block = jnp.right_shift(_row((n_rows, 1)), WINDOW.bit_length() - 1)
    out = jnp.zeros((n_rows, 1), _F32)
    for k, v in enumerate(values):
        out = jnp.where(block == k, v, out)
    return out


def _shift_down(z, halo, k):
    out = pltpu.roll(z, k, 0)
    r = _row(z.shape)
    for t in range(k):
        out = jnp.where(r == t, halo[V7X_SUBLANES - k + t:V7X_SUBLANES - k + t + 1, :], out)
    return out


def _shift_up(z, halo, k):
    rows = z.shape[0]
    out = pltpu.roll(z, rows - k, 0)
    r = _row(z.shape)
    for t in range(k):
        out = jnp.where(r == rows - k + t, halo[t:t + 1, :], out)
    return out


class _Dims:
    def __init__(self, s, d, d_ff):
        self.s, self.d, self.d_ff = s, d, d_ff
        self.aw = d // 2
        self.cw = d - self.aw
        self.nq = self.aw // HEAD_DIM
        self.group = self.nq // N_KV_HEADS
        assert self.group % 2 == 0, "a 128-lane pair of query heads must share its kv head"
        self.inw = self.aw + 2 * KV_WIDTH + 3 * self.cw
        self.o_k = self.aw
        self.o_v = self.aw + KV_WIDTH
        self.o_cg = self.aw + 2 * KV_WIDTH
        self.o_bg = self.o_cg + self.cw
        self.o_u = self.o_bg + self.cw
        self.nb = s // WINDOW
        assert s % WINDOW == 0


def _carrying(body, n_in, n_out, n_steps, ca, n_scratch=0):
    n_cin, n_cout = len(ca.operands), len(ca.out_shape)

    def wrapped(*refs):
        refs = list(refs)
        in_refs = [refs.pop(0) for _ in range(n_in)]
        cin_refs = [refs.pop(0) for _ in range(n_cin)]
        out_refs = [refs.pop(0) for _ in range(n_out)]
        cout_refs = [refs.pop(0) for _ in range(n_cout)]
        scratch_refs = [refs.pop(0) for _ in range(n_scratch)]
        if ca.comms:
            @pl.when(pl.program_id(0) == 0)
            def _():
                ca.start(cin_refs, cout_refs, refs)
        if ca.has_middle:
            @pl.when(pl.program_id(0) == _middle_step(n_steps))
            def _():
                ca.middle(cin_refs, cout_refs, refs)
        body(*in_refs, *out_refs, *scratch_refs)
        if ca.comms:
            @pl.when(pl.program_id(0) == n_steps - 1)
            def _():
                ca.finish(cin_refs, cout_refs, refs)

    return wrapped


def _row_kernel(name, body, rows_in, vecs_in, rows_out, vecs_out, comm=()):
    s = rows_in[0].shape[0]
    tr = _pick(s, (512, 256, 128))
    row = lambda a: pl.BlockSpec((tr, a[1] if isinstance(a, tuple) else a.shape[1]), lambda i: (i, 0))
    vec = lambda shape: pl.BlockSpec(tuple(shape), lambda i: (0, 0))
    n_in, n_out = len(rows_in) + len(vecs_in), len(rows_out) + len(vecs_out)
    ca = _CommArgs(list(comm), n_in, n_out)
    blocks = sum(_nbytes((tr, a.shape[1]), a.dtype) for a in rows_in) + sum(_nbytes((tr, sh[1]), dt) for sh, dt in rows_out)
    res = pl.pallas_call(
        _carrying(body, n_in, n_out, s // tr, ca), name=name, grid=(s // tr,),
        in_specs=[row(a) for a in rows_in] + [vec(v.shape) for v in vecs_in] + [_ANY] * len(ca.operands),
        out_specs=[row(sh) for sh, _ in rows_out] + [vec(sh) for sh, _ in vecs_out] + [_ANY] * len(ca.out_shape),
        out_shape=[jax.ShapeDtypeStruct(sh, dt) for sh, dt in list(rows_out) + list(vecs_out)] + ca.out_shape,
        scratch_shapes=ca.sems, input_output_aliases=ca.aliases,
        compiler_params=pltpu.CompilerParams(dimension_semantics=("arbitrary",), vmem_limit_bytes=_vmem_limit(blocks)),
    )(*rows_in, *vecs_in, *ca.operands)
    return list(res)


def _ln2_loss_bwd(r2, target, gain, bias, comm=()):
    s, d = r2.shape

    def body(r_ref, t_ref, g_ref, b_ref, dr_ref, drc_ref, loss_ref, dg_ref, db_ref):
        first = pl.program_id(0) == 0
        yv, xhat, rstd = _ln_fwd(r_ref[...], g_ref[...], b_ref[...])
        err = yv - t_ref[...]
        dr2, dg, db = _ln_bwd(err * (1.0 / d), xhat, rstd, g_ref[...])
        dr_ref[...] = dr2
        drc_ref[...] = dr2.astype(_CDT)
        _accumulate(loss_ref, jnp.zeros(loss_ref.shape, _F32) + 0.5 * jnp.sum(err * err) * (1.0 / d), first)
        _accumulate(dg_ref, dg, first)
        _accumulate(db_ref, db, first)

    return _row_kernel("ln2_loss_bwd", body, [r2, target], [gain, bias], [((s, d), _F32), ((s, d), _CDT)],
                       [((V7X_SUBLANES, V7X_LANES), _F32), ((1, d), _F32), ((1, d), _F32)], comm)


def _ln1_fwd_rows(r1, gain, bias, comm=()):
    s, d = r1.shape

    def body(r_ref, g_ref, b_ref, h_ref, hc_ref):
        h1, _, _ = _ln_fwd(r_ref[...], g_ref[...], b_ref[...])
        h_ref[...] = h1
        hc_ref[...] = h1.astype(_CDT)

    return _row_kernel("ln1", body, [r1], [gain, bias], [((s, d), _F32), ((s, d), _CDT)], [], comm)


def _ln1_bwd_rows(dh1, r1, gain, comm=()):
    s, d = dh1.shape

    def body(dh_ref, r_ref, g_ref, dr_ref, drc_ref, dg_ref, db_ref):
        first = pl.program_id(0) == 0
        _, xhat, rstd = _ln_fwd(r_ref[...], g_ref[...], 0.0)
        dr1, dg, db = _ln_bwd(dh_ref[...], xhat, rstd, g_ref[...])
        dr_ref[...] = dr1
        drc_ref[...] = dr1.astype(_CDT)
        _accumulate(dg_ref, dg, first)
        _accumulate(db_ref, db, first)

    return _row_kernel("ln1_bwd", body, [dh1, r1], [gain], [((s, d), _F32), ((s, d), _CDT)],
                       [((1, d), _F32), ((1, d), _F32)], comm)


def _mixer_fwd(dm, proj, rope, sinks, g_attn, g_conv, conv_w8, comm=()):
    s, d, aw, cw, nq, inw, nb = dm.s, dm.d, dm.aw, dm.cw, dm.nq, dm.inw, dm.nb

    def body(pp_ref, pc_ref, ropep_ref, ropec_ref, sinks_ref, ga_ref, gc_ref, cw_ref,
             mixed_ref, attn_ref, lse_ref, y_ref, qk_ref):
        n = pl.program_id(0)
        cos_c, sgn_c = ropec_ref[:, 0:V7X_LANES], ropec_ref[:, V7X_LANES:2 * V7X_LANES]
        cos_p, sgn_p = ropep_ref[:, 0:V7X_LANES], ropep_ref[:, V7X_LANES:2 * V7X_LANES]
        for g in range(KV_WIDTH // 128):
            qk_ref[:, aw + 128 * g:aw + 128 * g + 128] = _rope(
                pc_ref[:, dm.o_k + 128 * g:dm.o_k + 128 * g + 128], cos_c, sgn_c, 1.0).astype(qk_ref.dtype)
        for j in range(nq // 2):
            qk_ref[:, 128 * j:128 * j + 128] = _rope(pc_ref[:, 128 * j:128 * j + 128], cos_c, sgn_c, 1.0).astype(qk_ref.dtype)
        k_prev = jnp.concatenate([_rope(pp_ref[:, dm.o_k + 128 * g:dm.o_k + 128 * g + 128], cos_p, sgn_p, 1.0)
                                  for g in range(KV_WIDTH // 128)], axis=1)
        kk = jnp.concatenate([k_prev, qk_ref[:, aw:aw + KV_WIDTH].astype(_F32)], axis=0)
        vv = jnp.concatenate([pp_ref[:, dm.o_v:dm.o_v + KV_WIDTH], pc_ref[:, dm.o_v:dm.o_v + KV_WIDTH]], axis=0)
        group, pairs = dm.group, dm.group // 2
        valid = _band_mask(group, 2 * WINDOW, n == 0)
        for h in range(N_KV_HEADS):
            k2, v2 = _dup_head(kk, h).astype(_CDT), _dup_head(vv, h).astype(_CDT)
            q4 = _stack_heads([qk_ref[:, 128 * j:128 * j + 128] for j in range(pairs * h, pairs * (h + 1))])
            sc = jnp.where(valid, _dot(q4, k2, "nt") * ATTN_SCALE, MASKED)
            sink = _per_head([sinks_ref[0, group * h + r] for r in range(group)])
            mx = jnp.maximum(jnp.max(sc, axis=1, keepdims=True), sink)
            p = jnp.exp(sc - mx)
            den = jnp.sum(p, axis=1, keepdims=True) + jnp.exp(sink - mx)
            out = _unstack_heads(_dot(p / den, v2, "nn"), pairs)
            lse = mx + jnp.log(den)
            for r in range(group):
                lse_ref[:, group * h + r:group * h + r + 1] = lse[WINDOW * r:WINDOW * (r + 1)]
            for i in range(pairs):
                j = pairs * h + i
                attn_ref[:, 128 * j:128 * j + 128] = out[i]
        mixed_ref[:, 0:aw] = _rms_fwd(attn_ref[...], ga_ref[...]).astype(mixed_ref.dtype)

        z = pc_ref[:, dm.o_cg:dm.o_cg + cw] * pc_ref[:, dm.o_u:dm.o_u + cw]
        top = WINDOW - V7X_SUBLANES
        halo = pp_ref[top:WINDOW, dm.o_cg:dm.o_cg + cw] * pp_ref[top:WINDOW, dm.o_u:dm.o_u + cw]
        halo = jnp.where(n == 0, jnp.zeros_like(halo), halo)
        y = cw_ref[0:1, :] * _shift_down(z, halo, 2) + cw_ref[1:2, :] * _shift_down(z, halo, 1) + cw_ref[2:3, :] * z
        y_ref[...] = y
        conv = pc_ref[:, dm.o_bg:dm.o_bg + cw] * y
        mixed_ref[:, aw:d] = _rms_fwd(conv, gc_ref[...]).astype(mixed_ref.dtype)

    prev = lambda n: (jnp.maximum(n - 1, 0), 0)
    cur = lambda n: (n, 0)
    fixed = lambda n: (0, 0)
    blocks = 2 * WINDOW * inw * 4 + WINDOW * (d * 2 + aw * 4 + cw * 4 + nq * 4)
    ca = _CommArgs(list(comm), 8, 5)
    return pl.pallas_call(
        _carrying(body, 8, 5, nb, ca), name="mixer_fwd", grid=(nb,),
        in_specs=[pl.BlockSpec((WINDOW, inw), prev), pl.BlockSpec((WINDOW, inw), cur),
                  pl.BlockSpec((WINDOW, 2 * V7X_LANES), prev), pl.BlockSpec((WINDOW, 2 * V7X_LANES), cur),
                  pl.BlockSpec(memory_space=pltpu.SMEM),
                  pl.BlockSpec((1, aw), fixed), pl.BlockSpec((1, cw), fixed), pl.BlockSpec((V7X_SUBLANES, cw), fixed)]
        + [_ANY] * len(ca.operands),
        out_specs=[pl.BlockSpec((WINDOW, d), cur), pl.BlockSpec((WINDOW, aw), cur),
                   pl.BlockSpec((WINDOW, nq), cur), pl.BlockSpec((WINDOW, cw), cur),
                   pl.BlockSpec((WINDOW, aw + KV_WIDTH), cur)] + [_ANY] * len(ca.out_shape),
        out_shape=[jax.ShapeDtypeStruct((s, d), _CDT), jax.ShapeDtypeStruct((s, aw), _F32),
                   jax.ShapeDtypeStruct((s, nq), _F32), jax.ShapeDtypeStruct((s, cw), _F32),
                   jax.ShapeDtypeStruct((s, aw + KV_WIDTH), _CDT)] + ca.out_shape,
        scratch_shapes=ca.sems, input_output_aliases=ca.aliases,
        compiler_params=pltpu.CompilerParams(dimension_semantics=("arbitrary",), vmem_limit_bytes=_vmem_limit(blocks)),
    )(proj, proj, rope, rope, sinks, g_attn, g_conv, conv_w8, *ca.operands)


def _patch_columns(name, a, part, offset):
    s, pw = part.shape
    assert offset % pw == 0 and pw % V7X_LANES == 0
    tr = _pick(s, (512, 256, 128))

    def body(a_ref, p_ref, o_ref):
        del a_ref
        o_ref[...] = p_ref[...]

    return pl.pallas_call(
        body, name=name, grid=(s // tr,),
        in_specs=[_ANY, pl.BlockSpec((tr, pw), lambda i: (i, 0))],
        out_specs=pl.BlockSpec((tr, pw), lambda i: (i, offset // pw)),
        out_shape=jax.ShapeDtypeStruct(a.shape, a.dtype), input_output_aliases={0: 0},
        compiler_params=pltpu.CompilerParams(dimension_semantics=("arbitrary",)),
    )(a, part)


def _mixer_bwd(dm, proj, rope, sinks, g_attn, g_conv, conv_w8, dmixed, attn, lse, y, qk, comm=()):
    s, d, aw, cw, nq, inw, nb = dm.s, dm.d, dm.aw, dm.cw, dm.nq, dm.inw, dm.nb

    def body(pp_ref, pc_ref, pn_ref, ropep_ref, ropec_ref, dmc_ref, dmn_ref, ac_ref,
             lsec_ref, yc_ref, yn_ref, qkp_ref, qkc_ref, sinks_ref, ga_ref, gc_ref, cw_ref,
             dproj_ref, dkv_ref, dga_ref, dgc_ref, dsinks_ref, dcw_ref, dk_carry, dv_carry):
        n = pl.program_id(0)
        first = n == 0
        live = n < nb
        has_next = n < nb - 1
        cos_p, sgn_p = ropep_ref[:, 0:V7X_LANES], ropep_ref[:, V7X_LANES:2 * V7X_LANES]
        cos_c, sgn_c = ropec_ref[:, 0:V7X_LANES], ropec_ref[:, V7X_LANES:2 * V7X_LANES]

        @pl.when(first)
        def _():
            dk_carry[...] = jnp.zeros(dk_carry.shape, _F32)
            dv_carry[...] = jnp.zeros(dv_carry.shape, _F32)

        def write_kv(dk2, dv2, cos, sgn):
            lo = _lane((WINDOW, 128)) < HEAD_DIM
            for g in range(KV_WIDTH // 128):
                dk = jnp.where(lo, _fold_halves(dk2[2 * g]), _fold_halves(dk2[2 * g + 1]))
                dv = jnp.where(lo, _fold_halves(dv2[2 * g]), _fold_halves(dv2[2 * g + 1]))
                dkv_ref[:, 128 * g:128 * g + 128] = _rope(dk, cos, sgn, -1.0).astype(dkv_ref.dtype)
                dkv_ref[:, KV_WIDTH + 128 * g:KV_WIDTH + 128 * g + 128] = dv.astype(dkv_ref.dtype)

        @pl.when(jnp.logical_not(live))
        def _():
            write_kv([dk_carry[h] for h in range(N_KV_HEADS)], [dv_carry[h] for h in range(N_KV_HEADS)], cos_c, sgn_c)

        @pl.when(live)
        def _():
            block_step(pp_ref, pc_ref, pn_ref, dmc_ref, dmn_ref, ac_ref, lsec_ref, yc_ref, yn_ref, qkp_ref, qkc_ref,
                       sinks_ref, ga_ref, gc_ref, cw_ref, dproj_ref, dga_ref, dgc_ref, dsinks_ref, dcw_ref, dk_carry,
                       dv_carry, first, has_next, cos_p, sgn_p, cos_c, sgn_c, write_kv)

    def block_step(pp_ref, pc_ref, pn_ref, dmc_ref, dmn_ref, ac_ref, lsec_ref, yc_ref, yn_ref, qkp_ref, qkc_ref,
                   sinks_ref, ga_ref, gc_ref, cw_ref, dproj_ref, dga_ref, dgc_ref, dsinks_ref, dcw_ref, dk_carry,
                   dv_carry, first, has_next, cos_p, sgn_p, cos_c, sgn_c, write_kv):
        da_c, dga = _rms_bwd(dmc_ref[:, 0:aw], ac_ref[...], ga_ref[...])
        _accumulate(dga_ref, dga, first)
        kk = jnp.concatenate([qkp_ref[:, aw:aw + KV_WIDTH], qkc_ref[:, aw:aw + KV_WIDTH]], axis=0).astype(_F32)
        vv = jnp.concatenate([pp_ref[:, dm.o_v:dm.o_v + KV_WIDTH], pc_ref[:, dm.o_v:dm.o_v + KV_WIDTH]], axis=0)
        group, pairs = dm.group, dm.group // 2
        valid_c = _band_mask(group, 2 * WINDOW, first)
        dk_prev, dv_prev = [], []

        def stacked(q_ref, da, o_ref, lse_ref_, h):
            cols = [slice(128 * j, 128 * j + 128) for j in range(pairs * h, pairs * (h + 1))]
            q4 = _stack_heads([q_ref[:, c] for c in cols])
            do4 = _stack_heads([da[:, c] for c in cols])
            lo = _lane((WINDOW, 128)) < HEAD_DIM
            deltas = []
            for c in cols:
                prod = o_ref[:, c] * da[:, c]
                deltas += [jnp.sum(jnp.where(lo, prod, 0.0), axis=1, keepdims=True),
                           jnp.sum(jnp.where(lo, 0.0, prod), axis=1, keepdims=True)]
            lse4 = jnp.concatenate([lse_ref_[:, group * h + r:group * h + r + 1] for r in range(group)], axis=0)
            return q4, do4, lse4, deltas

        def scores_bwd(q4, do4, lse4, delta4, keys, vals, valid):
            sc = _dot(q4, keys, "nt") * ATTN_SCALE
            p = jnp.exp(jnp.where(valid, sc - lse4, MASKED))
            return p.astype(_CDT), (p * (_dot(do4, vals, "nt") - delta4) * ATTN_SCALE).astype(_CDT)

        for h in range(N_KV_HEADS):
            k2, v2 = _dup_head(kk, h).astype(_CDT), _dup_head(vv, h).astype(_CDT)
            q4, do4, lse4, deltas = stacked(qkc_ref, da_c, ac_ref, lsec_ref, h)
            delta4 = jnp.concatenate(deltas, axis=0)
            p, ds = scores_bwd(q4, do4, lse4, delta4, k2, v2, valid_c)
            for i, dq in enumerate(_unstack_heads(_dot(ds, k2, "nn"), pairs)):
                j = pairs * h + i
                dproj_ref[:, 128 * j:128 * j + 128] = _rope(dq, cos_c, sgn_c, -1.0).astype(dproj_ref.dtype)
            dk = _dot(ds, q4, "tn")
            dv = _dot(p, do4, "tn")
            dk_prev.append(dk_carry[h] + dk[0:WINDOW])
            dv_prev.append(dv_carry[h] + dv[0:WINDOW])
            dk_carry[h] = dk[WINDOW:2 * WINDOW]
            dv_carry[h] = dv[WINDOW:2 * WINDOW]
            heads = slice(group * h, group * (h + 1))
            sink_row, delta_heads = jnp.zeros((1, group), _F32), jnp.zeros((WINDOW, group), _F32)
            for r in range(group):
                sink_row = jnp.where(_lane((1, group)) == r, sinks_ref[0, group * h + r], sink_row)
                delta_heads = jnp.where(_lane((WINDOW, group)) == r, deltas[r], delta_heads)
            loss_sink = jnp.exp(sink_row - lsec_ref[:, heads]) * delta_heads
            _accumulate(dsinks_ref.at[:, heads], -jnp.sum(loss_sink, axis=0, keepdims=True), first)
        write_kv(dk_prev, dv_prev, cos_p, sgn_p)

        bg = pc_ref[:, dm.o_bg:dm.o_bg + cw]
        yc = yc_ref[...]
        dconv, dgc = _rms_bwd(dmc_ref[:, aw:d], bg * yc, gc_ref[...])
        _accumulate(dgc_ref, dgc, first)
        dproj_ref[:, dm.o_bg:dm.o_bg + cw] = (dconv * yc).astype(dproj_ref.dtype)
        dy = dconv * bg
        bg_n = pn_ref[:, dm.o_bg:dm.o_bg + cw]
        dconv_n, _ = _rms_bwd(dmn_ref[:, aw:d], bg_n * yn_ref[...], gc_ref[...])
        halo = jnp.where(has_next, dconv_n * bg_n, 0.0)
        dy1 = _shift_up(dy, halo, 1)
        dy2 = _shift_up(dy, halo, 2)
        dz = cw_ref[2:3, :] * dy + cw_ref[1:2, :] * dy1 + cw_ref[0:1, :] * dy2
        cg = pc_ref[:, dm.o_cg:dm.o_cg + cw]
        u = pc_ref[:, dm.o_u:dm.o_u + cw]
        dproj_ref[:, dm.o_cg:dm.o_cg + cw] = (dz * u).astype(dproj_ref.dtype)
        dproj_ref[:, dm.o_u:dm.o_u + cw] = (dz * cg).astype(dproj_ref.dtype)
        z = cg * u
        dcw = jnp.concatenate(
            [jnp.sum(z * t, axis=0, keepdims=True) for t in (dy2, dy1, dy)]
            + [jnp.zeros((V7X_SUBLANES - 3, cw), _F32)], axis=0)
        _accumulate(dcw_ref, dcw, first)

    at = lambda n: jnp.minimum(n, nb - 1)
    prev = lambda n: (jnp.maximum(at(n) - 1, 0), 0)
    cur = lambda n: (at(n), 0)
    done = lambda n: (jnp.maximum(n - 1, 0), 0)
    nxt8 = lambda n: (jnp.minimum((at(n) + 1) * (WINDOW // V7X_SUBLANES), s // V7X_SUBLANES - 1), 0)
    fixed = lambda n: (0, 0)
    blocks = WINDOW * (2 * inw * 4 + d * 4 + aw * 4 + cw * 4 + inw * 2 + 2 * KV_WIDTH * 2)
    carry = [pltpu.VMEM((N_KV_HEADS, WINDOW, 128), _F32), pltpu.VMEM((N_KV_HEADS, WINDOW, 128), _F32)]
    n_in, n_out = 17, 6
    ca = _CommArgs(list(comm), n_in, n_out)
    return pl.pallas_call(
        _carrying(body, n_in, n_out, nb + 1, ca, n_scratch=len(carry)), name="mixer_bwd", grid=(nb + 1,),
        in_specs=[pl.BlockSpec((WINDOW, inw), prev), pl.BlockSpec((WINDOW, inw), cur), pl.BlockSpec((V7X_SUBLANES, inw), nxt8),
                  pl.BlockSpec((WINDOW, 2 * V7X_LANES), prev), pl.BlockSpec((WINDOW, 2 * V7X_LANES), cur),
                  pl.BlockSpec((WINDOW, d), cur), pl.BlockSpec((V7X_SUBLANES, d), nxt8),
                  pl.BlockSpec((WINDOW, aw), cur), pl.BlockSpec((WINDOW, nq), cur),
                  pl.BlockSpec((WINDOW, cw), cur), pl.BlockSpec((V7X_SUBLANES, cw), nxt8),
                  pl.BlockSpec((WINDOW, aw + KV_WIDTH), prev), pl.BlockSpec((WINDOW, aw + KV_WIDTH), cur),
                  pl.BlockSpec(memory_space=pltpu.SMEM),
                  pl.BlockSpec((1, aw), fixed), pl.BlockSpec((1, cw), fixed), pl.BlockSpec((V7X_SUBLANES, cw), fixed)]
        + [_ANY] * len(ca.operands),
        out_specs=[pl.BlockSpec((WINDOW, inw), cur), pl.BlockSpec((WINDOW, 2 * KV_WIDTH), done),
                   pl.BlockSpec((1, aw), fixed), pl.BlockSpec((1, cw), fixed),
                   pl.BlockSpec((1, nq), fixed), pl.BlockSpec((V7X_SUBLANES, cw), fixed)] + [_ANY] * len(ca.out_shape),
        out_shape=[jax.ShapeDtypeStruct((s, inw), _CDT), jax.ShapeDtypeStruct((s, 2 * KV_WIDTH), _CDT),
                   jax.ShapeDtypeStruct((1, aw), _F32), jax.ShapeDtypeStruct((1, cw), _F32),
                   jax.ShapeDtypeStruct((1, nq), _F32), jax.ShapeDtypeStruct((V7X_SUBLANES, cw), _F32)] + ca.out_shape,
        scratch_shapes=carry + ca.sems, input_output_aliases=ca.aliases,
        compiler_params=pltpu.CompilerParams(dimension_semantics=("arbitrary",), vmem_limit_bytes=_vmem_limit(blocks)),
    )(proj, proj, proj, rope, rope, dmixed, dmixed, attn, lse, y, y, qk, qk, sinks, g_attn, g_conv, conv_w8, *ca.operands)


def _position():
    return lax.axis_index("x"), lax.axis_index("y"), lax.axis_index("c")


def _linear(px, py, pc):
    return 4 * px + 2 * py + pc


def _comm_kernel(name, comm):
    ca = _CommArgs(list(comm), 0, 0)
    n_cin, n_cout = len(ca.operands), len(ca.out_shape)

    def body(*refs):
        cin, cout, sems = refs[:n_cin], refs[n_cin:n_cin + n_cout], refs[n_cin + n_cout:]
        ca.start(cin, cout, sems)
        ca.middle(cin, cout, sems)
        ca.finish(cin, cout, sems)

    return pl.pallas_call(
        body, name=name, out_shape=ca.out_shape, in_specs=[_ANY] * n_cin, out_specs=[_ANY] * n_cout,
        scratch_shapes=ca.sems, input_output_aliases=ca.aliases,
    )(*ca.operands)


def _gather_op(units):
    n = len(units)
    inputs, outputs, aliases = [], [], {}
    for shard, _, _, _ in units:
        inputs.append(shard)
        outputs.append(jax.ShapeDtypeStruct((N_DEV * shard.shape[0], shard.shape[1]), shard.dtype))
    for u, (_, buf, _, _) in enumerate(units):
        if buf is not None:
            aliases[len(inputs)] = u
            inputs.append(buf)

    def plan(ins, outs, sems, north):
        send_sems, recv_sems, local_sems = sems
        x, y, c = _position()
        me, sibling = (x, y, c), (x, y, 1 - c)
        xn, yn, dg = (1 - x, y), (x, 1 - y), (1 - x, 1 - y)
        via, to, k_via, k_other = (yn, xn, 2, 1) if north else (xn, yn, 1, 2)

        def rows(u, px, py, pc):
            shard, _, r0, r1 = units[u]
            return outs[u].at[pl.ds(pl.multiple_of(_linear(px, py, pc) * shard.shape[0] + r0, 16), r1 - r0), :]

        def own(u):
            _, _, r0, r1 = units[u]
            return ins[u].at[pl.ds(r0, r1 - r0), :]

        def copy(u, k, block, to_, src=None):
            return pltpu.make_async_remote_copy(
                src_ref=rows(u, *block) if src is None else src, dst_ref=rows(u, *block),
                send_sem=send_sems.at[u, k], recv_sem=recv_sems.at[u, k], device_id=to_, device_id_type=_MESH)

        us = range(n)
        return dict(
            mine=[pltpu.make_async_copy(own(u), rows(u, *me), local_sems.at[u]) for u in us],
            first=[cp for u in us for cp in (copy(u, 0, me, sibling, src=own(u)), copy(u, 1, me, (*xn, c), src=own(u)),
                                             copy(u, 2, me, (*yn, c), src=own(u)))],
            relay=[copy(u, 3, (*via, c), (*to, c)) for u in us],
            arrived={1: [copy(u, 1, (*xn, c), me) for u in us], 2: [copy(u, 2, (*yn, c), me) for u in us],
                     3: [copy(u, 3, (*dg, c), me) for u in us]},
            passed={1: [copy(u, 4, (*xn, c), sibling) for u in us], 2: [copy(u, 5, (*yn, c), sibling) for u in us],
                    3: [copy(u, 6, (*dg, c), sibling) for u in us]},
            rest=[cp for u in us for cp in (copy(u, 0, sibling, me), copy(u, 4, (*xn, 1 - c), me),
                                            copy(u, 5, (*yn, 1 - c), me), copy(u, 6, (*dg, 1 - c), me))],
            k_via=k_via, k_other=k_other)

    def land(p, k):
        for arrived, onward in zip(p["arrived"][k], p["passed"][k]):
            arrived.wait_recv()
            onward.start()

    def by_core(fn):
        c = lax.axis_index("c")
        for north in (True, False):
            pl.when(c == (1 if north else 0))(functools.partial(fn, north))

    def start(ins, outs, sems):
        p = plan(ins, outs, sems, True)
        for cp in p["mine"] + p["first"]:
            cp.start()

    def middle(ins, outs, sems):
        def go(north):
            p = plan(ins, outs, sems, north)
            land(p, p["k_via"])
            for cp in p["relay"]:
                cp.start()
            land(p, p["k_other"])
        by_core(go)

    def finish(ins, outs, sems):
        def go(north):
            p = plan(ins, outs, sems, north)
            land(p, 3)
            for cp in p["rest"]:
                cp.wait_recv()
            for cp in p["first"] + p["relay"] + [cp for k in (1, 2, 3) for cp in p["passed"][k]]:
                cp.wait_send()
            for cp in p["mine"]:
                cp.wait()
        by_core(go)

    sems = [pltpu.SemaphoreType.DMA((n, 7)), pltpu.SemaphoreType.DMA((n, 7)), pltpu.SemaphoreType.DMA((n,))]
    return _Comm(inputs, outputs, aliases, sems, start, finish, middle)


def _peers(x, y, c):
    out = []
    for k in range(1, N_DEV):
        fx, fy, fc = (k >> 2) & 1, (k >> 1) & 1, k & 1
        out.append((1 - x if fx else x, 1 - y if fy else y, 1 - c if fc else c))
    return out


def _exchange_op(partials):
    n = len(partials)
    outputs = [jax.ShapeDtypeStruct((4, p.shape[0] // N_DEV, p.shape[1]), p.dtype) for p in partials]

    def plan(ins, outs, sems):
        send_sems, recv_sems = sems
        x, y, c = _position()
        out = []
        for a in range(n):
            r = outs[a].shape[1]
            for ch in range(4):
                out.append(pltpu.make_async_remote_copy(
                    src_ref=ins[a].at[pl.ds(pl.multiple_of((2 * ch + 1 - c) * r, 16), r), :], dst_ref=outs[a].at[ch],
                    send_sem=send_sems.at[a, ch], recv_sem=recv_sems.at[a, ch], device_id=(x, y, 1 - c),
                    device_id_type=_MESH))
        return out

    def start(ins, outs, sems):
        for cp in plan(ins, outs, sems):
            cp.start()

    def finish(ins, outs, sems):
        copies = plan(ins, outs, sems)
        for cp in copies:
            cp.wait_recv()
        for cp in copies:
            cp.wait_send()

    sems = [pltpu.SemaphoreType.DMA((n, 4)), pltpu.SemaphoreType.DMA((n, 4))]
    return _Comm(list(partials), outputs, {}, sems, start, finish)


def _chip_send_op(units):
    n = len(units)
    inputs, outputs, aliases = [], [], {}
    for q, _, _, _ in units:
        inputs.append(q)
        outputs.append(jax.ShapeDtypeStruct(q.shape, q.dtype))
    for u, (_, buf, _, _) in enumerate(units):
        if buf is not None:
            aliases[len(inputs)] = u
            inputs.append(buf)

    def plan(ins, outs, sems):
        send_sems, recv_sems, local_sems = sems
        x, y, c = _position()
        my_chip = 2 * x + y
        chips = [(1 - x, y), (x, 1 - y), (1 - x, 1 - y)]
        mine, sends, arrivals = [], [], []
        for u, (_, _, r0, r1) in enumerate(units):
            span = pl.ds(r0, r1 - r0)
            mine.append(pltpu.make_async_copy(ins[u].at[my_chip, span, :], outs[u].at[my_chip, span, :], local_sems.at[u]))
            for k, (px, py) in enumerate(chips):
                sends.append(pltpu.make_async_remote_copy(
                    src_ref=ins[u].at[2 * px + py, span, :], dst_ref=outs[u].at[my_chip, span, :],
                    send_sem=send_sems.at[u, k], recv_sem=recv_sems.at[u, k], device_id=(px, py, c), device_id_type=_MESH))
                arrivals.append(pltpu.make_async_remote_copy(
                    src_ref=ins[u].at[my_chip, span, :], dst_ref=outs[u].at[2 * px + py, span, :],
                    send_sem=send_sems.at[u, k], recv_sem=recv_sems.at[u, k], device_id=(px, py, c), device_id_type=_MESH))
        return mine, sends, arrivals

    def start(ins, outs, sems):
        mine, sends, _ = plan(ins, outs, sems)
        for cp in mine + sends:
            cp.start()

    def finish(ins, outs, sems):
        mine, sends, arrivals = plan(ins, outs, sems)
        for cp in arrivals:
            cp.wait_recv()
        for cp in sends:
            cp.wait_send()
        for cp in mine:
            cp.wait()

    sems = [pltpu.SemaphoreType.DMA((n, 3)), pltpu.SemaphoreType.DMA((n, 3)), pltpu.SemaphoreType.DMA((n,))]
    return _Comm(inputs, outputs, aliases, sems, start, finish)


def _pair_sum(name, partial, received):
    _, rows, cols = received.shape
    tr = _pick(rows, (352, 288, 256, 128, 64, 32, 16))
    p4 = partial.reshape(4, 2, rows, cols)
    kind = jnp.reshape(lax.axis_index("c"), (1,)).astype(jnp.int32)

    def body(kind_ref, p_ref, r_ref, o_ref):
        o_ref[0] = (p_ref[0, 0].astype(_F32) + r_ref[0].astype(_F32)).astype(o_ref.dtype)

    return pl.pallas_call(
        body, name=name,
        grid_spec=pltpu.PrefetchScalarGridSpec(
            num_scalar_prefetch=1, grid=(4, rows // tr),
            in_specs=[pl.BlockSpec((1, 1, tr, cols), lambda ch, i, kind_ref: (ch, kind_ref[0], i, 0)),
                      pl.BlockSpec((1, tr, cols), lambda ch, i, kind_ref: (ch, i, 0))],
            out_specs=pl.BlockSpec((1, tr, cols), lambda ch, i, kind_ref: (ch, i, 0))),
        out_shape=jax.ShapeDtypeStruct(received.shape, received.dtype),
        compiler_params=pltpu.CompilerParams(dimension_semantics=("arbitrary", "arbitrary")),
    )(kind, p4, received)


def _all_reduce_small(name, v):
    rows = v.shape[0]

    def body(v_ref, out_ref, land_ref, send_sems, recv_sems):
        x, y, c = _position()
        me = _linear(x, y, c)
        peers = _peers(x, y, c)
        land_ref[me] = v_ref[...]
        sends = [pltpu.make_async_remote_copy(
            src_ref=v_ref, dst_ref=land_ref.at[me], send_sem=send_sems.at[k], recv_sem=recv_sems.at[k],
            device_id=peer, device_id_type=_MESH) for k, peer in enumerate(peers)]
        for cp in sends:
            cp.start()
        for k, peer in enumerate(peers):
            pltpu.make_async_remote_copy(
                src_ref=v_ref, dst_ref=land_ref.at[_linear(*peer)], send_sem=send_sems.at[k], recv_sem=recv_sems.at[k],
                device_id=peer, device_id_type=_MESH).wait_recv()
        for cp in sends:
            cp.wait_send()
        total = land_ref[0]
        for s in range(1, N_DEV):
            total = total + land_ref[s]
        out_ref[...] = total

    return pl.pallas_call(
        body, name=name, out_shape=jax.ShapeDtypeStruct(v.shape, _F32),
        in_specs=[pl.BlockSpec(memory_space=pltpu.VMEM)], out_specs=pl.BlockSpec(memory_space=pltpu.VMEM),
        scratch_shapes=[pltpu.VMEM((N_DEV, rows, V7X_LANES), _F32), pltpu.SemaphoreType.DMA((7,)), pltpu.SemaphoreType.DMA((7,))],
    )(v)


def _adamw(name, w, slots, m, v):
    rows, cols = w.shape
    n_slots = slots.shape[0]
    tr = _pick(rows, (176, 144, 128, 64, 32, 16, 8))

    def body(w_ref, s_ref, m_ref, v_ref, g_ref, d_ref, nm_ref, nv_ref):
        g = s_ref[0].astype(_F32)
        for k in range(1, n_slots):
            g = g + s_ref[k].astype(_F32)
        nm = ADAM_B1 * m_ref[...] + (1.0 - ADAM_B1) * g
        nv = ADAM_B2 * v_ref[...] + (1.0 - ADAM_B2) * (g * g)
        m_hat = nm / (1.0 - ADAM_B1 ** ADAM_STEP)
        v_hat = nv / (1.0 - ADAM_B2 ** ADAM_STEP)
        g_ref[...] = g
        d_ref[...] = -ADAM_LR * (m_hat / (jnp.sqrt(v_hat) + ADAM_EPS) + ADAM_WD * w_ref[...])
        nm_ref[...] = nm
        nv_ref[...] = nv

    spec = pl.BlockSpec((tr, cols), lambda i: (i, 0))
    blocks = 7 * tr * cols * 4 + _nbytes((n_slots, tr, cols), slots.dtype)
    return pl.pallas_call(
        body, name=name, grid=(rows // tr,),
        in_specs=[spec, pl.BlockSpec((n_slots, tr, cols), lambda i: (0, i, 0)), spec, spec], out_specs=[spec] * 4,
        out_shape=[jax.ShapeDtypeStruct((rows, cols), _F32)] * 4,
        compiler_params=pltpu.CompilerParams(dimension_semantics=("arbitrary",), vmem_limit_bytes=_vmem_limit(blocks)),
    )(w, slots, m, v)


def _pad_rows(a, rows):
    return jnp.pad(a, ((0, rows - a.shape[0]), (0, 0)))


def _pack(parts):
    rows, spans, at = [], [], 0
    for p in parts:
        p = p.reshape(-1)
        r = -(-p.shape[0] // V7X_LANES)
        rows.append(jnp.pad(p, (0, r * V7X_LANES - p.shape[0])).reshape(r, V7X_LANES))
        spans.append((at, r, p.shape[0]))
        at += r
    packed = jnp.concatenate(rows, axis=0)
    return _pad_rows(packed, -(-at // V7X_SUBLANES) * V7X_SUBLANES), spans


def _unpack(packed, spans, shapes):
    return [packed[at:at + r].reshape(-1)[:size].reshape(shape) for (at, r, size), shape in zip(spans, shapes)]


def kernel(x, positions, w_in, conv_w, sinks, g_attn, g_conv, w_out, ln1_g, ln1_b, w_gate, w_up, w_down, ln2_g, ln2_b, loss_target, m_w_in, m_conv_w, m_sinks, m_g_attn, m_g_conv, m_w_out, m_ln1_g, m_ln1_b, m_w_gate, m_w_up, m_w_down, m_ln2_g, m_ln2_b, v_w_in, v_conv_w, v_sinks, v_g_attn, v_g_conv, v_w_out, v_ln1_g, v_ln1_b, v_w_gate, v_w_up, v_w_down, v_ln2_g, v_ln2_b):
    _, s, d = x.shape
    d_ff = N_DEV * w_gate.shape[2]
    dm = _Dims(s, d, d_ff)
    aw, cw, nq, inw = dm.aw, dm.cw, dm.nq, dm.inw
    x2 = x[0]
    pos = positions[0].reshape(s, 1)
    inv_freq = ROPE_THETA ** (-jnp.arange(0, ROT_DIM, 2, dtype=_F32) / ROT_DIM)
    invf = jnp.tile(inv_freq, V7X_LANES // (ROT_DIM // 2)).reshape(1, V7X_LANES)

    conv_cols = conv_w.shape[2]
    sh_in, sh_out = w_in[0].T.astype(_CDT), w_out[0].astype(_CDT)
    sh_gate, sh_up, sh_down = w_gate[0].T.astype(_CDT), w_up[0].T.astype(_CDT), w_down[0].astype(_CDT)
    r_in, r_out, r_ff = sh_in.shape[0], sh_out.shape[0], sh_gate.shape[0]
    q_ff = r_ff // 4
    assert q_ff % 16 == 0
    def prepare_body(x_ref, pos_ref, invf_ref, xc_ref, rope_ref):
        xc_ref[...] = x_ref[...].astype(_CDT)
        cos, sgn = _rope_tables(pos_ref[...], invf_ref[...])
        rope_ref[:, 0:V7X_LANES] = cos
        rope_ref[:, V7X_LANES:2 * V7X_LANES] = sgn

    x_c, rope, w_in_t, conv_all = _row_kernel(
        "prepare_gather_w_in", prepare_body, [x2, pos], [invf], [((s, d), _CDT), ((s, 2 * V7X_LANES), _F32)], [],
        comm=[_gather_op([(sh_in, None, 0, r_in), (_pad_rows(conv_w[0], 16), None, 0, 16)])])
    conv_full = conv_all.reshape(N_DEV, 16, conv_cols)[:, :3, :].transpose(1, 0, 2).reshape(3, cw)
    conv_w8 = _pad_rows(conv_full, V7X_SUBLANES)

    tm = _pick(s, (1024, 512, 256, 128))
    tm2 = _pick(s, (2048, 1024, 512, 256, 128))
    tr = _pick(s, (512, 256, 128))
    tn_in = _pick(inw, (512, 256, 128))
    tn_ff = _pick(d_ff, (512, 256, 128))

    proj, w_out_f, w_gate_t = _matmul(
        "proj", [[(x_c, w_in_t, "nt")]], s, inw, d, tm2, tn_in, d, [],
        [((s, inw), _F32, (tm2, tn_in), _tile_ij)], _store_epilogue,
        comm=[_gather_op([(sh_out, None, 0, r_out), (sh_gate, None, 0, 2 * q_ff)])])
    mixed, attn, lse, y_conv, qk_rot, w_gate_t, w_up_t = _mixer_fwd(
        dm, proj, rope, sinks, g_attn, g_conv, conv_w8,
        comm=[_gather_op([(sh_gate, w_gate_t, 2 * q_ff, r_ff), (sh_up, None, 0, 2 * q_ff)])])

    def residual_epilogue(accs, ex, out, first):
        out[0][...] = DEEPNORM_ALPHA * ex[0][...] + accs[0]

    tn_d = _pick(d, (512,))
    r1, w_up_t = _matmul(
        "out_proj", [[(mixed, w_out_f, "nn")]], s, d, d, tm, tn_d, d, [(x2, (tm, tn_d), _tile_ij)],
        [((s, d), _F32, (tm, tn_d), _tile_ij)], residual_epilogue,
        comm=[_gather_op([(sh_up, w_up_t, 2 * q_ff, 3 * q_ff)])])
    h1, h1_c, w_up_t = _ln1_fwd_rows(r1, ln1_g, ln1_b, comm=[_gather_op([(sh_up, w_up_t, 3 * q_ff, r_ff)])])

    def swiglu_epilogue(accs, ex, out, first):
        gate_v, up_v = accs
        out[0][...] = gate_v
        out[1][...] = up_v
        out[2][...] = (gate_v * jax.nn.sigmoid(gate_v) * up_v).astype(_CDT)

    gate, up, act, w_down_f = _matmul(
        "gate_up", [[(h1_c, w_gate_t, "nt")], [(h1_c, w_up_t, "nt")]], s, d_ff, d, tm, tn_ff, d, [],
        [((s, d_ff), _F32, (tm, tn_ff), _tile_ij), ((s, d_ff), _F32, (tm, tn_ff), _tile_ij),
         ((s, d_ff), _CDT, (tm, tn_ff), _tile_ij)], swiglu_epilogue,
        comm=[_gather_op([(sh_down, None, 0, r_ff)])])

    (r2,) = _matmul("down", [[(act, w_down_f, "nn")]], s, d, d_ff, tm, tn_d, d_ff, [(h1, (tm, tn_d), _tile_ij)],
                    [((s, d), _F32, (tm, tn_d), _tile_ij)], residual_epilogue)
    dr2, dr2_c, loss_acc, d_ln2_g, d_ln2_b = _ln2_loss_bwd(r2, loss_target[0], ln2_g, ln2_b)

    def swiglu_bwd_epilogue(accs, ex, out, first):
        gate_v, up_v = ex[0][...], ex[1][...]
        sig = jax.nn.sigmoid(gate_v)
        out[0][...] = (accs[0] * up_v * (sig * (1.0 + gate_v * (1.0 - sig)))).astype(_CDT)
        out[1][...] = (accs[0] * (gate_v * sig)).astype(_CDT)

    dgate, dup = _matmul(
        "dact", [[(dr2_c, w_down_f, "nt")]], s, d_ff, d, tm2, tn_ff, d,
        [(gate, (tm2, tn_ff), _tile_ij), (up, (tm2, tn_ff), _tile_ij)],
        [((s, d_ff), _CDT, (tm2, tn_ff), _tile_ij), ((s, d_ff), _CDT, (tm2, tn_ff), _tile_ij)], swiglu_bwd_epilogue)
    def weight_grad(name, a, b, comm=()):
        rows = a.shape[1]
        tw, tn_w = _pick(rows, (512, 256, 128)), d
        return _matmul(name, [[(a, b, "tn")]], rows, d, s, tw, tn_w, s, [],
                       [((rows, d), _CDT, (tw, tn_w), _tile_ij)], _store_epilogue, comm=comm, j_outer=True,
                       b_buffers=1)

    (dw_down,) = weight_grad("dw_down", act, dr2_c)
    dw_gate_t, x_down = weight_grad("dw_gate", dgate, h1_c, comm=[_exchange_op([dw_down])])
    q_down = _pair_sum("chip_sum_w_down", dw_down, x_down)
    dw_up_t, l_down, x_gate = weight_grad(
        "dw_up", dup, h1_c, comm=[_chip_send_op([(q_down, None, 0, 2 * q_ff)]), _exchange_op([dw_gate_t])])
    q_gate = _pair_sum("chip_sum_w_gate", dw_gate_t, x_gate)

    tn_h = _pick(d, (512,))
    dh1, l_down, l_gate, x_up = _matmul(
        "dh1", [[(dgate, w_gate_t, "nn"), (dup, w_up_t, "nn")]], s, d, d_ff, tr, tn_h, d_ff,
        [(dr2, (tr, tn_h), _tile_ij)], [((s, d), _F32, (tr, tn_h), _tile_ij)], residual_epilogue,
        comm=[_chip_send_op([(q_down, l_down, 2 * q_ff, r_ff), (q_gate, None, 0, r_ff)]), _exchange_op([dw_up_t])])
    q_up = _pair_sum("chip_sum_w_up", dw_up_t, x_up)
    dr1, dr1_c, d_ln1_g, d_ln1_b = _ln1_bwd_rows(dh1, r1, ln1_g)
    (dmixed,) = _matmul("dmixed", [[(dr1_c, w_out_f, "nt")]], s, d, d, tm2, tn_d, d, [],
                        [((s, d), _F32, (tm2, tn_d), _tile_ij)], _store_epilogue)
    (dw_out,) = weight_grad("dw_out", mixed, dr1_c)
    dproj, dkv, d_g_attn, d_g_conv, d_sinks, d_conv8, l_up, x_out = _mixer_bwd(
        dm, proj, rope, sinks, g_attn, g_conv, conv_w8, dmixed, attn, lse, y_conv, qk_rot,
        comm=[_chip_send_op([(q_up, None, 0, r_ff)]), _exchange_op([dw_out])])
    dproj = _patch_columns("dproj_kv", dproj, dkv, dm.o_k)
    q_out = _pair_sum("chip_sum_w_out", dw_out, x_out)
    dw_in_t, l_out = weight_grad("dw_in", dproj, x_c, comm=[_chip_send_op([(q_out, None, 0, r_out)])])
    (x_in,) = _comm_kernel("exchange_w_in", [_exchange_op([dw_in_t])])
    q_in = _pair_sum("chip_sum_w_in", dw_in_t, x_in)

    grad_x, l_in = _matmul("dx", [[(dproj, w_in_t, "nn")]], s, d, inw, tm, tn_d, inw,
                           [(dr1, (tm, tn_d), _tile_ij)], [((s, d), _F32, (tm, tn_d), _tile_ij)], residual_epilogue,
                           comm=[_chip_send_op([(q_in, None, 0, r_in)])])

    small_parts = [d_conv8[:3], d_sinks, d_g_attn, d_g_conv, d_ln1_g, d_ln1_b, d_ln2_g, d_ln2_b, loss_acc[0:1, 0:1]]
    packed, spans = _pack(small_parts)
    reduced = _unpack(_all_reduce_small("reduce_small", packed), spans, [p.shape for p in small_parts])
    g_conv_full, g_sinks, g_g_attn, g_g_conv, g_ln1_g, g_ln1_b, g_ln2_g, g_ln2_b, loss_sum = reduced
    me = _linear(*_position())
    g_conv_w = lax.dynamic_slice(g_conv_full, (0, me * conv_cols), (3, conv_cols))
    loss = loss_sum[0, 0]

    big = {"w_in": (w_in[0].T, l_in, m_w_in[0].T, v_w_in[0].T), "w_out": (w_out[0], l_out, m_w_out[0], v_w_out[0]),
           "w_gate": (w_gate[0].T, l_gate, m_w_gate[0].T, v_w_gate[0].T),
           "w_up": (w_up[0].T, l_up, m_w_up[0].T, v_w_up[0].T), "w_down": (w_down[0], l_down, m_w_down[0], v_w_down[0])}
    res = {nm: tuple(_adamw(f"adamw_{nm}", w, slots, m, v)) for nm, (w, slots, m, v) in big.items()}
    for nm in ("w_in", "w_gate", "w_up"):
        res[nm] = tuple(a.T for a in res[nm])
    small_names = ["conv_w", "sinks", "g_attn", "g_conv", "ln1_g", "ln1_b", "ln2_g", "ln2_b"]
    small_w = [conv_w, sinks, g_attn, g_conv, ln1_g, ln1_b, ln2_g, ln2_b]
    small_g = [g_conv_w[None], g_sinks, g_g_attn, g_g_conv, g_ln1_g, g_ln1_b, g_ln2_g, g_ln2_b]
    small_m = [m_conv_w, m_sinks, m_g_attn, m_g_conv, m_ln1_g, m_ln1_b, m_ln2_g, m_ln2_b]
    small_v = [v_conv_w, v_sinks, v_g_attn, v_g_conv, v_ln1_g, v_ln1_b, v_ln2_g, v_ln2_b]
    pw, sp = _pack(small_w)
    pg, _ = _pack(small_g)
    pm, _ = _pack(small_m)
    pv, _ = _pack(small_v)
    shapes = [w.shape for w in small_w]
    _, sd, sm, sv = [_unpack(p, sp, shapes) for p in _adamw("adamw_small", pw, pg[None], pm, pv)]
    for i, nm in enumerate(small_names):
        res[nm] = (small_g[i].reshape(shapes[i]), sd[i], sm[i], sv[i])

    order = ["w_in", "conv_w", "sinks", "g_attn", "g_conv", "w_out", "ln1_g", "ln1_b", "w_gate", "w_up", "w_down", "ln2_g", "ln2_b"]

    def lead(a, nm):
        return a[None] if nm in big else a

    return (loss, grad_x[None],
            *[lead(res[nm][0], nm) for nm in order], *[lead(res[nm][1], nm) for nm in order],
            *[lead(res[nm][2], nm) for nm in order], *[lead(res[nm][3], nm) for nm in order])
```
